```python
import jax, jax.numpy as jnp
from jax import lax
import numpy as np

D_MODEL = 1024
BATCH = 8
SEQ = 2048
DEPTH = 2

GDN_HEADS = 8
GDN_HEAD_DIM = 128
GDN_WIDTH = GDN_HEADS * GDN_HEAD_DIM
SHORT_CONV = 4
CHUNK = 64
CONF_CH = D_MODEL
CONF_KERNEL = 31
N_EXPERTS = 64
TOP_K = 8
N_GROUPS = 8
TOPK_GROUPS = 4
EXPERT_FF = 256
SHARED_FF = 256
ROUTED_SCALE = 2.5
EXPERT_BLOCK = 128
DEEPNORM_ALPHA = (2 * DEPTH) ** 0.25
DEEPNORM_BETA = (8 * DEPTH) ** -0.25
EPS = 1e-6
N_IN = 4 * GDN_WIDTH + 2 * GDN_HEADS + 2 * CONF_CH + 2 * D_MODEL

kernel_name = "hybrid_gdn_conformer_moe_deepnorm"


def layer_norm(x, g, b):
    xf = x.astype(jnp.float32)
    mu = jnp.mean(xf, axis=-1, keepdims=True)
    var = jnp.mean(jnp.square(xf - mu), axis=-1, keepdims=True)
    y = (xf - mu) * lax.rsqrt(var + EPS) * g.astype(jnp.float32) + b.astype(jnp.float32)
    return y.astype(x.dtype)


def l2norm(x):
    return x * lax.rsqrt(jnp.sum(jnp.square(x), axis=-1, keepdims=True) + EPS)


def causal_depthwise_conv(x, w):
    k, c = w.shape
    return lax.conv_general_dilated(
        x, w[:, None, :], window_strides=(1,), padding=[(k - 1, 0)],
        dimension_numbers=("NWC", "WIO", "NWC"), feature_group_count=c)


def gated_delta_rule(q, k, v, g, beta):
    b_, s_, h_, dk = q.shape
    dv = v.shape[-1]
    n = s_ // CHUNK

    def chunks(t):
        return t.astype(jnp.float32).reshape(b_, n, CHUNK, h_, -1).transpose(0, 3, 1, 2, 4)

    q = chunks(q) * (dk ** -0.5)
    k = chunks(k)
    v = chunks(v)
    g = g.astype(jnp.float32).reshape(b_, n, CHUNK, h_).transpose(0, 3, 1, 2)
    beta = beta.astype(jnp.float32).reshape(b_, n, CHUNK, h_).transpose(0, 3, 1, 2)
    gc = jnp.cumsum(g, axis=-1)

    idx = jnp.arange(CHUNK)
    causal = idx[:, None] >= idx[None, :]
    strict = idx[:, None] > idx[None, :]
    diff = gc[..., :, None] - gc[..., None, :]
    decay = jnp.where(causal, jnp.exp(jnp.where(causal, diff, 0.0)), 0.0)

    k_beta = k * beta[..., None]
    a_mat = jnp.where(strict, jnp.einsum("bhncd,bhnsd->bhncs", k_beta, k) * decay, 0.0)
    rhs = jnp.concatenate([v * beta[..., None], k_beta * jnp.exp(gc)[..., None]], axis=-1)
    sol = lax.linalg.triangular_solve(a_mat, rhs, left_side=True, lower=True, unit_diagonal=True)
    u, w = sol[..., :dv], sol[..., dv:]

    qk = jnp.einsum("bhncd,bhnsd->bhncs", q, k) * decay
    q_dec = q * jnp.exp(gc)[..., None]
    k_dec = k * jnp.exp(gc[..., -1:] - gc)[..., None]
    g_last = jnp.exp(gc[..., -1])

    def step(state, xs):
        u_c, w_c, qk_c, qd_c, kd_c, gl_c = xs
        v_new = u_c - jnp.einsum("bhcd,bhdv->bhcv", w_c, state)
        o_c = (jnp.einsum("bhcd,bhdv->bhcv", qd_c, state)
               + jnp.einsum("bhcs,bhsv->bhcv", qk_c, v_new))
        state = state * gl_c[..., None, None] + jnp.einsum("bhcd,bhcv->bhdv", kd_c, v_new)
        return state, o_c

    xs = tuple(jnp.moveaxis(t, 2, 0) for t in (u, w, qk, q_dec, k_dec, g_last))
    state0 = jnp.zeros((b_, h_, dk, dv), jnp.float32)
    _, o = lax.scan(step, state0, xs)
    return o.transpose(1, 0, 3, 2, 4).reshape(b_, s_, h_, dv)


def hybrid_mixer(x, w_in, conv_qkv, a_log, dt_bias, o_norm_g, w_oa,
                 dw_w, dw_b, cln_g, cln_b, w_ob, w_o):
    b_, s_, _ = x.shape
    proj = x @ w_in
    cuts = [3 * GDN_WIDTH, 4 * GDN_WIDTH, 4 * GDN_WIDTH + GDN_HEADS,
            4 * GDN_WIDTH + 2 * GDN_HEADS, 4 * GDN_WIDTH + 2 * GDN_HEADS + 2 * CONF_CH,
            4 * GDN_WIDTH + 2 * GDN_HEADS + 2 * CONF_CH + D_MODEL]
    qkv, z, a_raw, b_raw, glu, gate_a, gate_b = jnp.split(proj, cuts, axis=-1)

    qkv = jax.nn.silu(causal_depthwise_conv(qkv, conv_qkv))
    q, k, v = jnp.split(qkv, 3, axis=-1)
    q = l2norm(q.reshape(b_, s_, GDN_HEADS, GDN_HEAD_DIM).astype(jnp.float32))
    k = l2norm(k.reshape(b_, s_, GDN_HEADS, GDN_HEAD_DIM).astype(jnp.float32))
    v = v.reshape(b_, s_, GDN_HEADS, GDN_HEAD_DIM)
    g = -jnp.exp(a_log.astype(jnp.float32)) * jax.nn.softplus(
        a_raw.astype(jnp.float32) + dt_bias.astype(jnp.float32))
    beta = jax.nn.sigmoid(b_raw.astype(jnp.float32))
    o = gated_delta_rule(q, k, v, g, beta)
    o = o * lax.rsqrt(jnp.mean(jnp.square(o), axis=-1, keepdims=True) + EPS) * o_norm_g.astype(jnp.float32)
    o = o * jax.nn.silu(z.reshape(b_, s_, GDN_HEADS, GDN_HEAD_DIM).astype(jnp.float32))
    branch_a = o.reshape(b_, s_, GDN_WIDTH).astype(x.dtype) @ w_oa

    u = glu[..., :CONF_CH] * jax.nn.sigmoid(glu[..., CONF_CH:])
    u = causal_depthwise_conv(u, dw_w) + dw_b
    u = jax.nn.silu(layer_norm(u, cln_g, cln_b))
    branch_b = u @ w_ob

    h = jax.nn.sigmoid(gate_a) * branch_a + jax.nn.sigmoid(gate_b) * branch_b
    return h @ w_o


def route(xf, w_router, router_bias):
    t = xf.shape[0]
    s = jax.nn.sigmoid((xf @ w_router).astype(jnp.float32))
    biased = s + router_bias.astype(jnp.float32)
    grp = biased.reshape(t, N_GROUPS, N_EXPERTS // N_GROUPS)
    grp_score = jnp.sum(lax.top_k(grp, 2)[0], axis=-1)
    _, gidx = lax.top_k(grp_score, TOPK_GROUPS)
    gmask = jnp.sum(jax.nn.one_hot(gidx, N_GROUPS, dtype=jnp.float32), axis=-2)
    emask = jnp.repeat(gmask, N_EXPERTS // N_GROUPS, axis=-1) > 0
    _, eidx = lax.top_k(jnp.where(emask, biased, -jnp.inf), TOP_K)
    wts = jnp.take_along_axis(s, eidx, axis=-1)
    wts = wts / jnp.sum(wts, axis=-1, keepdims=True) * ROUTED_SCALE
    return eidx.astype(jnp.int32), wts


def routed_experts(xf, eidx, wts, w_gate, w_up, w_down):
    t, d = xf.shape
    n_assign = t * TOP_K
    n_blocks = -(-(n_assign + N_EXPERTS * (EXPERT_BLOCK - 1)) // EXPERT_BLOCK)
    n_rows = n_blocks * EXPERT_BLOCK
    e_flat = eidx.reshape(-1)
    tok_flat = jnp.repeat(jnp.arange(t, dtype=jnp.int32), TOP_K)
    w_flat = wts.reshape(-1)
    order = jnp.argsort(e_flat)
    e_s, tok_s, w_s = e_flat[order], tok_flat[order], w_flat[order]
    counts = jnp.zeros((N_EXPERTS,), jnp.int32).at[e_flat].add(1)
    padded = (counts + EXPERT_BLOCK - 1) // EXPERT_BLOCK * EXPERT_BLOCK
    start = jnp.cumsum(counts) - counts
    pend = jnp.cumsum(padded)
    pstart = pend - padded
    dest = pstart[e_s] + (jnp.arange(n_assign, dtype=jnp.int32) - start[e_s])
    row_tok = jnp.full((n_rows,), t, jnp.int32).at[dest].set(tok_s)
    row_w = jnp.zeros((n_rows,), xf.dtype).at[dest].set(w_s.astype(xf.dtype))
    block_start = jnp.arange(n_blocks, dtype=jnp.int32) * EXPERT_BLOCK
    block_e = jnp.minimum(jnp.searchsorted(pend, block_start, side="right"), N_EXPERTS - 1)
    x_pad = jnp.concatenate([xf, jnp.zeros((1, d), xf.dtype)], axis=0)
    xb = x_pad[row_tok].reshape(n_blocks, EXPERT_BLOCK, d)

    def expert_block(args):
        xblk, e = args
        hid = jax.nn.silu(xblk @ w_gate[e]) * (xblk @ w_up[e])
        return hid @ w_down[e]

    yb = lax.map(expert_block, (xb, block_e)).reshape(n_rows, d)
    y = jax.ops.segment_sum(yb * row_w[:, None], row_tok, num_segments=t + 1)
    return y[:t]


def moe_ffn(x, w_router, router_bias, w_gate_e, w_up_e, w_down_e, w_sh_gate, w_sh_up, w_sh_down):
    b_, s_, d = x.shape
    xf = x.reshape(b_ * s_, d)
    eidx, wts = route(xf, w_router, router_bias)
    routed = routed_experts(xf, eidx, wts, w_gate_e, w_up_e, w_down_e)
    shared = (jax.nn.silu(xf @ w_sh_gate) * (xf @ w_sh_up)) @ w_sh_down
    return (routed + shared).reshape(b_, s_, d)


def setup_inputs(seed: int = 0) -> dict:
    key = jax.random.key(seed)
    ks = jax.random.split(key, 26)
    L, D = DEPTH, D_MODEL
    f32 = jnp.float32

    def nrm(k, shape, scale):
        return jax.random.normal(k, shape, f32) * scale

    dt = jnp.exp(jax.random.uniform(ks[4], (L, GDN_HEADS), f32, jnp.log(1e-3), jnp.log(1e-1)))
    return {
        "x": nrm(ks[0], (BATCH, SEQ, D), 1.0),
        "w_in": nrm(ks[1], (L, D, N_IN), D ** -0.5),
        "conv_qkv": nrm(ks[2], (L, SHORT_CONV, 3 * GDN_WIDTH), SHORT_CONV ** -0.5),
        "a_log": jnp.log(jax.random.uniform(ks[3], (L, GDN_HEADS), f32, 1.0, 16.0)),
        "dt_bias": dt + jnp.log(-jnp.expm1(-dt)),
        "o_norm_g": 1.0 + nrm(ks[5], (L, GDN_HEAD_DIM), 0.02),
        "w_oa": nrm(ks[6], (L, GDN_WIDTH, D), GDN_WIDTH ** -0.5 * DEEPNORM_BETA),
        "dw_w": nrm(ks[7], (L, CONF_KERNEL, CONF_CH), CONF_KERNEL ** -0.5),
        "dw_b": nrm(ks[8], (L, CONF_CH), 0.02),
        "cln_g": 1.0 + nrm(ks[9], (L, CONF_CH), 0.02),
        "cln_b": nrm(ks[10], (L, CONF_CH), 0.02),
        "w_ob": nrm(ks[11], (L, CONF_CH, D), CONF_CH ** -0.5 * DEEPNORM_BETA),
        "w_o": nrm(ks[12], (L, D, D), D ** -0.5 * DEEPNORM_BETA),
        "ln1_g": 1.0 + nrm(ks[13], (L, D), 0.02),
        "ln1_b": nrm(ks[14], (L, D), 0.02),
        "w_router": nrm(ks[15], (L, D, N_EXPERTS), D ** -0.5),
        "router_bias": nrm(ks[16], (L, N_EXPERTS), 0.01),
        "w_gate_e": nrm(ks[17], (L, N_EXPERTS, D, EXPERT_FF), D ** -0.5),
        "w_up_e": nrm(ks[18], (L, N_EXPERTS, D, EXPERT_FF), D ** -0.5),
        "w_down_e": nrm(ks[19], (L, N_EXPERTS, EXPERT_FF, D), EXPERT_FF ** -0.5 * DEEPNORM_BETA),
        "w_sh_gate": nrm(ks[20], (L, D, SHARED_FF), D ** -0.5),
        "w_sh_up": nrm(ks[21], (L, D, SHARED_FF), D ** -0.5),
        "w_sh_down": nrm(ks[22], (L, SHARED_FF, D), SHARED_FF ** -0.5 * DEEPNORM_BETA),
        "ln2_g": 1.0 + nrm(ks[23], (L, D), 0.02),
        "ln2_b": nrm(ks[24], (L, D), 0.02),
    }


def reference(x, w_in, conv_qkv, a_log, dt_bias, o_norm_g, w_oa, dw_w, dw_b, cln_g, cln_b,
              w_ob, w_o, ln1_g, ln1_b, w_router, router_bias, w_gate_e, w_up_e, w_down_e,
              w_sh_gate, w_sh_up, w_sh_down, ln2_g, ln2_b):
    for l in range(DEPTH):
        mix = hybrid_mixer(x, w_in[l], conv_qkv[l], a_log[l], dt_bias[l], o_norm_g[l], w_oa[l],
                           dw_w[l], dw_b[l], cln_g[l], cln_b[l], w_ob[l], w_o[l])
        x = layer_norm(DEEPNORM_ALPHA * x + mix, ln1_g[l], ln1_b[l])
        ffn = moe_ffn(x, w_router[l], router_bias[l], w_gate_e[l], w_up_e[l], w_down_e[l],
                      w_sh_gate[l], w_sh_up[l], w_sh_down[l])
        x = layer_norm(DEEPNORM_ALPHA * x + ffn, ln2_g[l], ln2_b[l])
    return x
```

```python
import functools

import jax
import jax.numpy as jnp
from jax import lax
from jax.experimental import pallas as pl
from jax.experimental.pallas import tpu as pltpu

F32 = jnp.float32
BF16 = jnp.bfloat16
I32 = jnp.int32

D_MODEL = 1024
GDN_HEADS = 8
HEAD_DIM = 128
GDN_WIDTH = GDN_HEADS * HEAD_DIM
SHORT_CONV = 4
CHUNK = 64
SOLVE_BLOCK = 16
CONF_CH = D_MODEL
CONF_KERNEL = 31
CONF_HALO = 32
N_EXPERTS = 64
TOP_K = 8
N_GROUPS = 8
GROUP_SIZE = N_EXPERTS // N_GROUPS
TOPK_GROUPS = 4
EXPERT_FF = 256
SHARED_FF = 256
ROUTED_SCALE = 2.5
DEPTH = 2
DEEPNORM_ALPHA = (2 * DEPTH) ** 0.25
EPS = 1e-6

LANES = 128
PROJ_COLS = 8 * D_MODEL
ROW_BLOCK = 256
VMEM_LIMIT = 56 * 1024 * 1024


def _params(*sem):
    return pltpu.CompilerParams(dimension_semantics=sem, vmem_limit_bytes=VMEM_LIMIT)


def _dot(a, b):
    return jnp.dot(a, b, preferred_element_type=F32)


def _dot_nt(a, b):
    return lax.dot_general(a, b, (((1,), (1,)), ((), ())), preferred_element_type=F32)


def _split(a):
    hi = a.astype(BF16)
    lo = (a - hi.astype(F32)).astype(BF16)
    return hi, lo


def _dot3(a, b):
    ah, al = _split(a)
    bh, bl = _split(b)
    return _dot(ah, bh) + _dot(al, bh) + _dot(ah, bl)


def _sigmoid(x):
    return 1.0 / (1.0 + jnp.exp(-x))


def _silu(x):
    return x * _sigmoid(x)


def _layer_norm(y, g, b):
    mu = jnp.mean(y, axis=-1, keepdims=True)
    yc = y - mu
    var = jnp.mean(yc * yc, axis=-1, keepdims=True)
    return yc * lax.rsqrt(var + EPS) * g + b


def _proj_kernel(a_ref, w_ref, o_ref):
    o_ref[...] = _dot(a_ref[...], w_ref[...])


def _proj(xb, w, tm, tn):
    m, k = xb.shape
    n = w.shape[1]
    return pl.pallas_call(
        _proj_kernel,
        grid=(m // tm, n // tn),
        in_specs=[pl.BlockSpec((tm, k), lambda i, j: (i, 0)),
                  pl.BlockSpec((k, tn), lambda i, j: (0, j))],
        out_specs=pl.BlockSpec((tm, tn), lambda i, j: (i, j)),
        out_shape=jax.ShapeDtypeStruct((m, n), F32),
        compiler_params=_params("parallel", "parallel"),
        name="proj",
    )(xb, w)


def _unit_lower_inverse(a_strict, row, col):
    eye = (row == col).astype(F32)
    shift = SOLVE_BLOCK.bit_length() - 1
    same = jnp.right_shift(row, shift) == jnp.right_shift(col, shift)
    a_diag = jnp.where(same, a_strict, 0.0)
    a_off = a_strict - a_diag
    b1 = -a_diag
    p = eye + b1
    bp = b1
    for _ in range(3):
        bp = _dot3(bp, bp)
        p = p + _dot3(p, bp)
    n1 = _dot3(p, a_off)
    n2 = _dot3(n1, n1)
    q = p + _dot3(n2, p)
    return q - _dot3(n1, q)


def _gdn_kernel(qkv_ref, z_ref, x_ref, wabh_ref, wabl_ref, cw_ref, prm_ref, ong_ref,
                o_ref, tail_ref, s_ref, xe_ref):
    c = CHUNK

    @pl.when(pl.program_id(1) == 0)
    def _():
        tail_ref[...] = jnp.zeros_like(tail_ref)
        s_ref[...] = jnp.zeros_like(s_ref)

    xe_ref[0:8, :] = tail_ref[...]
    xe_ref[8:8 + c, :] = qkv_ref[...]
    tail_ref[...] = qkv_ref[c - 8:c, :]

    xh, xl = _split(x_ref[...])
    ab = _dot(xh, wabh_ref[...]) + _dot(xl, wabh_ref[...]) + _dot(xh, wabl_ref[...])
    a_log = prm_ref[0:1, :]
    dt_bias = prm_ref[1:2, :]
    sp_in = ab + dt_bias
    softplus = jnp.maximum(sp_in, 0.0) + jnp.log1p(jnp.exp(-jnp.abs(sp_in)))
    g_all = -jnp.exp(a_log) * softplus
    beta_all = _sigmoid(ab)

    row = lax.broadcasted_iota(I32, (c, c), 0)
    col = lax.broadcasted_iota(I32, (c, c), 1)
    causal = row >= col
    strict = row > col
    ltri = causal.astype(BF16)
    g_hi = g_all.astype(BF16)
    g_r = g_all - g_hi.astype(F32)
    g_mid = g_r.astype(BF16)
    g_lo = (g_r - g_mid.astype(F32)).astype(BF16)
    gc = _dot(ltri, g_hi) + _dot(ltri, g_mid) + _dot(ltri, g_lo)
    gct = gc.T

    ong = ong_ref[...]
    for h in range(GDN_HEADS):
        lo, hi = h * HEAD_DIM, (h + 1) * HEAD_DIM

        def conv(base):
            acc = cw_ref[SHORT_CONV - 1:SHORT_CONV, base + lo:base + hi] * xe_ref[8:8 + c, base + lo:base + hi]
            for j in range(SHORT_CONV - 1):
                s0 = 8 - (SHORT_CONV - 1) + j
                acc = acc + cw_ref[j:j + 1, base + lo:base + hi] * xe_ref[s0:s0 + c, base + lo:base + hi]
            return _silu(acc)

        q = conv(0)
        k = conv(GDN_WIDTH)
        v = conv(2 * GDN_WIDTH)
        q = q * lax.rsqrt(jnp.sum(q * q, axis=-1, keepdims=True) + EPS) * (HEAD_DIM ** -0.5)
        k = k * lax.rsqrt(jnp.sum(k * k, axis=-1, keepdims=True) + EPS)

        gch = gc[:, h:h + 1]
        beta = beta_all[:, GDN_HEADS + h:GDN_HEADS + h + 1]
        diff = gch - gct[h:h + 1, :]
        decay = jnp.where(causal, jnp.exp(jnp.where(causal, diff, 0.0)), 0.0)
        egc = jnp.exp(gch)
        g_end = gc[c - 1:c, h:h + 1]

        kb = k * beta
        kbf = k.astype(BF16)
        a_mat = jnp.where(strict, _dot_nt(kb.astype(BF16), kbf) * decay, 0.0)
        qk = _dot_nt(q.astype(BF16), kbf) * decay
        tinv = _unit_lower_inverse(a_mat, row, col)
        rhs = jnp.concatenate([v * beta, kb * egc], axis=1)
        sol = _dot3(tinv, rhs)
        u = sol[:, :HEAD_DIM]
        w = sol[:, HEAD_DIM:]

        state = s_ref[h]
        wq = jnp.concatenate([w, q * egc], axis=0).astype(BF16)
        r = _dot(wq, state.astype(BF16))
        v_new = u - r[:c]
        v_new_b = v_new.astype(BF16)
        o = r[c:] + _dot(qk.astype(BF16), v_new_b)
        k_dec = k * jnp.exp(g_end - gch)
        s_ref[h] = state * jnp.exp(g_end) + _dot(k_dec.T.astype(BF16), v_new_b)

        o = o * lax.rsqrt(jnp.mean(o * o, axis=-1, keepdims=True) + EPS) * ong
        o = o * _silu(z_ref[:, lo:hi])
        o_ref[:, lo:hi] = o.astype(o_ref.dtype)


def _gdn(proj, x, wab_hi, wab_lo, conv_w, prm, ong, batch, seq):
    t = batch * seq
    nch = seq // CHUNK
    c = CHUNK
    rows = lambda b, n: b * nch + n
    return pl.pallas_call(
        _gdn_kernel,
        grid=(batch, nch),
        in_specs=[
            pl.BlockSpec((c, 3 * GDN_WIDTH), lambda b, n: (rows(b, n), 0)),
            pl.BlockSpec((c, GDN_WIDTH), lambda b, n: (rows(b, n), 3)),
            pl.BlockSpec((c, D_MODEL), lambda b, n: (rows(b, n), 0)),
            pl.BlockSpec((D_MODEL, LANES), lambda b, n: (0, 0)),
            pl.BlockSpec((D_MODEL, LANES), lambda b, n: (0, 0)),
            pl.BlockSpec((SHORT_CONV, 3 * GDN_WIDTH), lambda b, n: (0, 0)),
            pl.BlockSpec((8, LANES), lambda b, n: (0, 0)),
            pl.BlockSpec((1, HEAD_DIM), lambda b, n: (0, 0)),
        ],
        out_specs=pl.BlockSpec((c, GDN_WIDTH), lambda b, n: (rows(b, n), 0)),
        out_shape=jax.ShapeDtypeStruct((t, GDN_WIDTH), BF16),
        scratch_shapes=[
            pltpu.VMEM((8, 3 * GDN_WIDTH), F32),
            pltpu.VMEM((GDN_HEADS, HEAD_DIM, HEAD_DIM), F32),
            pltpu.VMEM((8 + c, 3 * GDN_WIDTH), F32),
        ],
        compiler_params=_params("parallel", "arbitrary"),
        name="gdn",
    )(proj, proj, x, wab_hi, wab_lo, conv_w, prm, ong)


def _mixer_kernel(glu_ref, ga_ref, gb_ref, o_ref, x_ref, woa_ref, wob_ref, wo_ref, dww_ref, vec_ref,
                  x1_ref, x1b_ref, ubuf_ref, conv_ref, *, ts, rc):
    halo = CONF_HALO

    @pl.when(pl.program_id(1) == 0)
    def _():
        ubuf_ref[0:halo, :] = jnp.zeros((halo, CONF_CH), F32)

    @pl.when(pl.program_id(1) != 0)
    def _():
        ubuf_ref[0:halo, :] = ubuf_ref[ts:ts + halo, :]

    ubuf_ref[halo:halo + ts, :] = glu_ref[:, :CONF_CH] * _sigmoid(glu_ref[:, CONF_CH:])

    dw_b = vec_ref[0:1, :]
    cln_g = vec_ref[1:2, :]
    cln_b = vec_ref[2:3, :]
    ln1_g = vec_ref[3:4, :]
    ln1_b = vec_ref[4:5, :]

    off = halo - (CONF_KERNEL - 1)
    for r0 in range(0, ts, rc):
        acc = dww_ref[0:1, :] * ubuf_ref[r0 + off:r0 + off + rc, :]
        for j in range(1, CONF_KERNEL):
            acc = acc + dww_ref[j:j + 1, :] * ubuf_ref[r0 + off + j:r0 + off + j + rc, :]
        conv_ref[r0:r0 + rc, :] = acc

    uc = _silu(_layer_norm(conv_ref[...] + dw_b, cln_g, cln_b))
    branch_b = _dot(uc.astype(BF16), wob_ref[...])
    branch_a = _dot(o_ref[...], woa_ref[...])
    hmix = _sigmoid(ga_ref[...]) * branch_a + _sigmoid(gb_ref[...]) * branch_b
    mix = _dot(hmix.astype(BF16), wo_ref[...])
    x1 = _layer_norm(DEEPNORM_ALPHA * x_ref[...] + mix, ln1_g, ln1_b)
    x1_ref[...] = x1
    x1b_ref[...] = x1.astype(BF16)


def _mixer(proj, o_gdn, x, woa, wob, wo, dww, vec, batch, seq, ts):
    t = batch * seq
    nt = seq // ts
    rows = lambda b, n: b * nt + n
    full = lambda b, n: (0, 0)
    kern = functools.partial(_mixer_kernel, ts=ts, rc=32)
    return pl.pallas_call(
        kern,
        grid=(batch, nt),
        in_specs=[
            pl.BlockSpec((ts, 2 * CONF_CH), lambda b, n: (rows(b, n), 2)),
            pl.BlockSpec((ts, D_MODEL), lambda b, n: (rows(b, n), 6)),
            pl.BlockSpec((ts, D_MODEL), lambda b, n: (rows(b, n), 7)),
            pl.BlockSpec((ts, GDN_WIDTH), lambda b, n: (rows(b, n), 0)),
            pl.BlockSpec((ts, D_MODEL), lambda b, n: (rows(b, n), 0)),
            pl.BlockSpec((GDN_WIDTH, D_MODEL), full),
            pl.BlockSpec((CONF_CH, D_MODEL), full),
            pl.BlockSpec((D_MODEL, D_MODEL), full),
            pl.BlockSpec((CONF_HALO, CONF_CH), full),
            pl.BlockSpec((8, D_MODEL), full),
        ],
        out_specs=[pl.BlockSpec((ts, D_MODEL), lambda b, n: (rows(b, n), 0)),
                   pl.BlockSpec((ts, D_MODEL), lambda b, n: (rows(b, n), 0))],
        out_shape=[jax.ShapeDtypeStruct((t, D_MODEL), F32),
                   jax.ShapeDtypeStruct((t, D_MODEL), BF16)],
        scratch_shapes=[pltpu.VMEM((CONF_HALO + ts, CONF_CH), F32),
                        pltpu.VMEM((ts, CONF_CH), F32)],
        compiler_params=_params("parallel", "arbitrary"),
        name="mixer",
    )(proj, proj, proj, o_gdn, x, woa, wob, wo, dww, vec)


def _route_kernel(x_ref, wrh_ref, wrl_ref, bias_ref, eidx_ref, wts_ref, rank_ref, cnt_ref, carry_ref, *, tt):
    @pl.when(pl.program_id(0) == 0)
    def _():
        carry_ref[...] = jnp.zeros_like(carry_ref)

    xh, xl = _split(x_ref[...])
    wrh = wrh_ref[...]
    logits = _dot_nt(wrh, xh) + _dot_nt(wrh, xl) + _dot_nt(wrl_ref[...], xh)
    s = _sigmoid(logits)
    biased = s + bias_ref[...]

    sub = lax.broadcasted_iota(I32, (GROUP_SIZE, tt), 0)
    groups = [biased[g * GROUP_SIZE:(g + 1) * GROUP_SIZE, :] for g in range(N_GROUPS)]
    gs = []
    for bg in groups:
        m1 = jnp.max(bg, axis=0, keepdims=True)
        first = jnp.min(jnp.where(bg == m1, sub, GROUP_SIZE), axis=0, keepdims=True)
        m2 = jnp.max(jnp.where(sub == first, -jnp.inf, bg), axis=0, keepdims=True)
        gs.append(m1 + m2)

    masked_parts = []
    for g in range(N_GROUPS):
        beaten = jnp.zeros((1, tt), I32)
        for o in range(N_GROUPS):
            if o == g:
                continue
            wins = (gs[o] >= gs[g]) if o < g else (gs[o] > gs[g])
            beaten = beaten + wins.astype(I32)
        keep = jnp.broadcast_to(beaten < TOPK_GROUPS, (GROUP_SIZE, tt))
        masked_parts.append(jnp.where(keep, groups[g], -jnp.inf))
    masked = jnp.concatenate(masked_parts, axis=0)

    eiota = lax.broadcasted_iota(I32, (N_EXPERTS, tt), 0)
    sel_all = jnp.zeros((N_EXPERTS, tt), F32)
    picks = []
    for _ in range(TOP_K):
        m = jnp.max(masked, axis=0, keepdims=True)
        idx = jnp.min(jnp.where(masked == m, eiota, N_EXPERTS), axis=0, keepdims=True)
        onehot = eiota == idx
        picks.append((idx, onehot))
        sel_all = jnp.where(onehot, 1.0, sel_all)
        masked = jnp.where(onehot, -jnp.inf, masked)

    tr = lax.broadcasted_iota(I32, (tt, tt), 0)
    tc = lax.broadcasted_iota(I32, (tt, tt), 1)
    before = (tr < tc).astype(BF16)
    sel_b = sel_all.astype(BF16)
    carry = carry_ref[...]
    rank_all = _dot(sel_b, before) + carry[:, 0:1]
    carry_new = carry + _dot(sel_b, jnp.ones((tt, LANES), BF16))
    carry_ref[...] = carry_new
    cnt_ref[...] = carry_new

    s_sel = [jnp.sum(jnp.where(oh, s, 0.0), axis=0, keepdims=True) for _, oh in picks]
    total = s_sel[0]
    for v in s_sel[1:]:
        total = total + v
    for k, (idx, oh) in enumerate(picks):
        eidx_ref[k:k + 1, :] = idx
        wts_ref[k:k + 1, :] = s_sel[k] / total * ROUTED_SCALE
        rank_ref[k:k + 1, :] = jnp.sum(jnp.where(oh, rank_all, 0.0), axis=0, keepdims=True).astype(I32)


def _route(x1, wr_hi, wr_lo, bias, tt):
    t = x1.shape[0]
    kern = functools.partial(_route_kernel, tt=tt)
    return pl.pallas_call(
        kern,
        grid=(t // tt,),
        in_specs=[pl.BlockSpec((tt, D_MODEL), lambda i: (i, 0)),
                  pl.BlockSpec((N_EXPERTS, D_MODEL), lambda i: (0, 0)),
                  pl.BlockSpec((N_EXPERTS, D_MODEL), lambda i: (0, 0)),
                  pl.BlockSpec((N_EXPERTS, tt), lambda i: (0, 0))],
        out_specs=[pl.BlockSpec((TOP_K, tt), lambda i: (0, i)),
                   pl.BlockSpec((TOP_K, tt), lambda i: (0, i)),
                   pl.BlockSpec((TOP_K, tt), lambda i: (0, i)),
                   pl.BlockSpec((N_EXPERTS, LANES), lambda i: (0, 0))],
        out_shape=[jax.ShapeDtypeStruct((TOP_K, t), I32),
                   jax.ShapeDtypeStruct((TOP_K, t), F32),
                   jax.ShapeDtypeStruct((TOP_K, t), I32),
                   jax.ShapeDtypeStruct((N_EXPERTS, LANES), F32)],
        scratch_shapes=[pltpu.VMEM((N_EXPERTS, LANES), F32)],
        compiler_params=_params("arbitrary"),
        name="route",
    )(x1, wr_hi, wr_lo, bias)


def _dest_kernel(eidx_ref, rank_ref, pstart_ref, dest_ref):
    eidx = eidx_ref[...]
    acc = rank_ref[...]
    for e in range(N_EXPERTS):
        acc = acc + jnp.where(eidx == e, pstart_ref[e], 0)
    dest_ref[0] = acc


def _dest(eidx_t, rank_t, pstart, tt):
    t = eidx_t.shape[1]
    return pl.pallas_call(
        _dest_kernel,
        grid=(t // tt,),
        in_specs=[pl.BlockSpec((TOP_K, tt), lambda i: (0, i)),
                  pl.BlockSpec((TOP_K, tt), lambda i: (0, i)),
                  pl.BlockSpec(memory_space=pltpu.SMEM)],
        out_specs=pl.BlockSpec((1, TOP_K, tt), lambda i: (i, 0, 0)),
        out_shape=jax.ShapeDtypeStruct((t // tt, TOP_K, tt), I32),
        compiler_params=_params("parallel"),
        name="dest",
    )(eidx_t, rank_t, pstart)


def _row_copy_out(x_ref, xs_ref, sem, t, row):
    return pltpu.make_async_copy(x_ref.at[pl.ds(t, 1)], xs_ref.at[pl.ds(row, 1)], sem)


def _dispatch_kernel(dest_hbm, x_ref, xs_in, xs_ref, idx_ref, isem, sem, *, tt):
    del xs_in
    i = pl.program_id(0)
    icopy = pltpu.make_async_copy(dest_hbm.at[i], idx_ref, isem)
    icopy.start()
    icopy.wait()

    def issue(t, carry):
        for k in range(TOP_K):
            _row_copy_out(x_ref, xs_ref, sem, t, idx_ref[k * tt + t]).start()
        return carry

    lax.fori_loop(0, tt, issue, 0)

    def drain(t, carry):
        for k in range(TOP_K):
            _row_copy_out(x_ref, xs_ref, sem, 0, 0).wait()
        return carry

    lax.fori_loop(0, tt, drain, 0)


def _dispatch(dest, x1, n_rows, tt):
    t = x1.shape[0]
    kern = functools.partial(_dispatch_kernel, tt=tt)
    xs0 = jnp.zeros((n_rows, D_MODEL), F32)
    return pl.pallas_call(
        kern,
        grid=(t // tt,),
        in_specs=[pl.BlockSpec(memory_space=pl.ANY),
                  pl.BlockSpec((tt, D_MODEL), lambda i: (i, 0)),
                  pl.BlockSpec(memory_space=pl.ANY)],
        out_specs=pl.BlockSpec(memory_space=pl.ANY),
        out_shape=jax.ShapeDtypeStruct((n_rows, D_MODEL), F32),
        scratch_shapes=[pltpu.SMEM((TOP_K * tt,), I32),
                        pltpu.SemaphoreType.DMA,
                        pltpu.SemaphoreType.DMA],
        input_output_aliases={2: 0},
        compiler_params=_params("arbitrary"),
        name="dispatch",
    )(dest, x1, xs0)


def _experts_kernel(be_ref, xs_ref, wg_ref, wu_ref, wd_ref, ys_ref):
    del be_ref
    xb = xs_ref[...].astype(BF16)
    gate = _dot(xb, wg_ref[...])
    up = _dot(xb, wu_ref[...])
    hid = (_silu(gate) * up).astype(BF16)
    ys_ref[...] = _dot(hid, wd_ref[...])


def _experts(block_e, xs, wg, wu, wd):
    n_rows = xs.shape[0]
    nb = n_rows // ROW_BLOCK
    grid_spec = pltpu.PrefetchScalarGridSpec(
        num_scalar_prefetch=1,
        grid=(nb,),
        in_specs=[pl.BlockSpec((ROW_BLOCK, D_MODEL), lambda i, be: (i, 0)),
                  pl.BlockSpec((None, D_MODEL, EXPERT_FF), lambda i, be: (be[i], 0, 0)),
                  pl.BlockSpec((None, D_MODEL, EXPERT_FF), lambda i, be: (be[i], 0, 0)),
                  pl.BlockSpec((None, EXPERT_FF, D_MODEL), lambda i, be: (be[i], 0, 0))],
        out_specs=pl.BlockSpec((ROW_BLOCK, D_MODEL), lambda i, be: (i, 0)),
    )
    return pl.pallas_call(
        _experts_kernel,
        grid_spec=grid_spec,
        out_shape=jax.ShapeDtypeStruct((n_rows, D_MODEL), F32),
        compiler_params=_params("arbitrary"),
        name="experts",
    )(block_e, xs, wg, wu, wd)


def _row_copy_in(ys_ref, buf_ref, sem, k, t, row):
    return pltpu.make_async_copy(ys_ref.at[pl.ds(row, 1)], buf_ref.at[k, pl.ds(t, 1)], sem)


def _combine_kernel(dest_hbm, ys_ref, w_ref, x1_ref, x1b_ref, wsg_ref, wsu_ref, wsd_ref, vec_ref,
                    x2_ref, x2b_ref, idx_ref, buf_ref, isem, sem, *, tt):
    i = pl.program_id(0)
    icopy = pltpu.make_async_copy(dest_hbm.at[i], idx_ref, isem)
    icopy.start()
    icopy.wait()

    def issue(t, carry):
        for k in range(TOP_K):
            _row_copy_in(ys_ref, buf_ref, sem, k, t, idx_ref[k * tt + t]).start()
        return carry

    lax.fori_loop(0, tt, issue, 0)

    xb = x1b_ref[...]
    hid = (_silu(_dot(xb, wsg_ref[...])) * _dot(xb, wsu_ref[...])).astype(BF16)
    acc = _dot(hid, wsd_ref[...])

    def drain(t, carry):
        for k in range(TOP_K):
            _row_copy_in(ys_ref, buf_ref, sem, 0, 0, 0).wait()
        return carry

    lax.fori_loop(0, tt, drain, 0)

    w = w_ref[...]
    for k in range(TOP_K):
        acc = acc + w[:, k:k + 1] * buf_ref[k]
    x2 = _layer_norm(DEEPNORM_ALPHA * x1_ref[...] + acc, vec_ref[0:1, :], vec_ref[1:2, :])
    x2_ref[...] = x2
    x2b_ref[...] = x2.astype(BF16)


def _combine(dest, ys, w_tok, x1, x1b, wsg, wsu, wsd, vec, tt):
    t = x1.shape[0]
    full = lambda i: (0, 0)
    kern = functools.partial(_combine_kernel, tt=tt)
    return pl.pallas_call(
        kern,
        grid=(t // tt,),
        in_specs=[pl.BlockSpec(memory_space=pl.ANY),
                  pl.BlockSpec(memory_space=pl.ANY),
                  pl.BlockSpec((tt, TOP_K), lambda i: (i, 0)),
                  pl.BlockSpec((tt, D_MODEL), lambda i: (i, 0)),
                  pl.BlockSpec((tt, D_MODEL), lambda i: (i, 0)),
                  pl.BlockSpec((D_MODEL, SHARED_FF), full),
                  pl.BlockSpec((D_MODEL, SHARED_FF), full),
                  pl.BlockSpec((SHARED_FF, D_MODEL), full),
                  pl.BlockSpec((8, D_MODEL), full)],
        out_specs=[pl.BlockSpec((tt, D_MODEL), lambda i: (i, 0)),
                   pl.BlockSpec((tt, D_MODEL), lambda i: (i, 0))],
        out_shape=[jax.ShapeDtypeStruct((t, D_MODEL), F32),
                   jax.ShapeDtypeStruct((t, D_MODEL), BF16)],
        scratch_shapes=[pltpu.SMEM((TOP_K * tt,), I32),
                        pltpu.VMEM((TOP_K, tt, D_MODEL), F32),
                        pltpu.SemaphoreType.DMA,
                        pltpu.SemaphoreType.DMA],
        compiler_params=_params("arbitrary"),
        name="combine",
    )(dest, ys, w_tok, x1, x1b, wsg, wsu, wsd, vec)


def _pad_rows(a, rows):
    return jnp.zeros((rows, a.shape[-1]), F32).at[:a.shape[0]].set(a.astype(F32))


def _layer(x, xb, p, batch, seq):
    t = batch * seq
    gw = GDN_WIDTH
    w_in = p["w_in"]
    c_z, c_a, c_glu = 3 * gw, 4 * gw, 4 * gw + 2 * GDN_HEADS
    c_ga = c_glu + 2 * CONF_CH
    c_gb = c_ga + D_MODEL
    w_main = jnp.concatenate([w_in[:, :c_a], w_in[:, c_glu:]], axis=1).astype(BF16)
    w_ab = jnp.zeros((D_MODEL, LANES), F32).at[:, :2 * GDN_HEADS].set(w_in[:, c_a:c_glu])
    wab_hi = w_ab.astype(BF16)
    wab_lo = (w_ab - wab_hi.astype(F32)).astype(BF16)
    del c_z, c_gb

    tm = min(1024, t)
    proj = _proj(xb, w_main, tm, 1024)

    prm = jnp.zeros((8, LANES), F32)
    prm = prm.at[0, :GDN_HEADS].set(p["a_log"]).at[1, :GDN_HEADS].set(p["dt_bias"])
    ong = p["o_norm_g"].reshape(1, HEAD_DIM).astype(F32)
    o_gdn = _gdn(proj, x, wab_hi, wab_lo, p["conv_qkv"].astype(F32), prm, ong, batch, seq)

    ts = min(256, seq)
    dww = _pad_rows(p["dw_w"], CONF_HALO)
    vec = _pad_rows(jnp.stack([p["dw_b"], p["cln_g"], p["cln_b"], p["ln1_g"], p["ln1_b"]]), 8)
    x1, x1b = _mixer(proj, o_gdn, x, p["w_oa"].astype(BF16), p["w_ob"].astype(BF16),
                     p["w_o"].astype(BF16), dww, vec, batch, seq, ts)

    tt_r = min(256, t)
    wr_t = p["w_router"].T.astype(F32)
    wr_hi = wr_t.astype(BF16)
    wr_lo = (wr_t - wr_hi.astype(F32)).astype(BF16)
    bias = jnp.broadcast_to(p["router_bias"].astype(F32)[:, None], (N_EXPERTS, tt_r))
    eidx_t, wts_t, rank_t, cnt = _route(x1, wr_hi, wr_lo, bias, tt_r)

    counts = cnt[:, 0].astype(I32)
    padded = (counts + ROW_BLOCK - 1) // ROW_BLOCK * ROW_BLOCK
    pend = jnp.cumsum(padded)
    pstart = (pend - padded).astype(I32)
    n_blocks = -(-(t * TOP_K + N_EXPERTS * (ROW_BLOCK - 1)) // ROW_BLOCK)
    n_rows = n_blocks * ROW_BLOCK
    block_start = jnp.arange(n_blocks, dtype=I32) * ROW_BLOCK
    block_e = jnp.minimum(jnp.searchsorted(pend, block_start, side="right"), N_EXPERTS - 1).astype(I32)

    tt_d = 128
    dest = _dest(eidx_t, rank_t, pstart, tt_d).reshape(t // tt_d, TOP_K * tt_d)
    xs = _dispatch(dest, x1, n_rows, tt_d)
    ys = _experts(block_e, xs, p["w_gate_e"].astype(BF16), p["w_up_e"].astype(BF16),
                  p["w_down_e"].astype(BF16))
    vec2 = _pad_rows(jnp.stack([p["ln2_g"], p["ln2_b"]]), 8)
    x2, x2b = _combine(dest, ys, wts_t.T, x1, x1b, p["w_sh_gate"].astype(BF16),
                       p["w_sh_up"].astype(BF16), p["w_sh_down"].astype(BF16), vec2, tt_d)
    return x2, x2b


_PARAM_NAMES = ("w_in", "conv_qkv", "a_log", "dt_bias", "o_norm_g", "w_oa", "dw_w", "dw_b", "cln_g",
                "cln_b", "w_ob", "w_o", "ln1_g", "ln1_b", "w_router", "router_bias", "w_gate_e",
                "w_up_e", "w_down_e", "w_sh_gate", "w_sh_up", "w_sh_down", "ln2_g", "ln2_b")


def kernel(x, w_in, conv_qkv, a_log, dt_bias, o_norm_g, w_oa, dw_w, dw_b, cln_g, cln_b, w_ob, w_o,
           ln1_g, ln1_b, w_router, router_bias, w_gate_e, w_up_e, w_down_e, w_sh_gate, w_sh_up,
           w_sh_down, ln2_g, ln2_b):
    stacked = dict(zip(_PARAM_NAMES, (w_in, conv_qkv, a_log, dt_bias, o_norm_g, w_oa, dw_w, dw_b, cln_g,
                                      cln_b, w_ob, w_o, ln1_g, ln1_b, w_router, router_bias, w_gate_e,
                                      w_up_e, w_down_e, w_sh_gate, w_sh_up, w_sh_down, ln2_g, ln2_b)))
    batch, seq, d = x.shape
    assert d == D_MODEL and seq % CHUNK == 0
    xf = x.reshape(batch * seq, d).astype(F32)
    xb = xf.astype(BF16)
    for layer in range(w_in.shape[0]):
        p = {name: arr[layer] for name, arr in stacked.items()}
        xf, xb = _layer(xf, xb, p, batch, seq)
    return xf.reshape(batch, seq, d).astype(x.dtype)
```

```python
import functools

import jax
import jax.numpy as jnp
import numpy as np
from jax import lax
from jax.experimental import pallas as pl
from jax.experimental.pallas import tpu as pltpu

F32 = jnp.float32
BF16 = jnp.bfloat16
I32 = jnp.int32
U32 = jnp.uint32
HI_MASK = np.uint32(0xFFFF0000)

D_MODEL = 1024
GDN_HEADS = 8
HEAD_DIM = 128
GDN_WIDTH = GDN_HEADS * HEAD_DIM
SHORT_CONV = 4
CHUNK = 64
SOLVE_BLOCK = 16
GDN_TILE = 256
CONF_CH = D_MODEL
CONF_KERNEL = 31
CONF_HALO = 32
N_EXPERTS = 64
TOP_K = 8
N_GROUPS = 8
GROUP_SIZE = N_EXPERTS // N_GROUPS
TOPK_GROUPS = 4
EXPERT_FF = 256
SHARED_FF = 256
ROUTED_SCALE = 2.5
DEPTH = 2
DEEPNORM_ALPHA = (2 * DEPTH) ** 0.25
EPS = 1e-6

LANES = 128
PROJ_COLS = 8 * D_MODEL
ROW_BLOCK = 256
VMEM_LIMIT = 56 * 1024 * 1024


def _params(*sem):
    return pltpu.CompilerParams(dimension_semantics=sem, vmem_limit_bytes=VMEM_LIMIT)


def _dot(a, b):
    return jnp.dot(a, b, preferred_element_type=F32)


def _dot_nt(a, b):
    return lax.dot_general(a, b, (((1,), (1,)), ((), ())), preferred_element_type=F32)


def _split(a):
    hi = a.astype(BF16)
    lo = (a - hi.astype(F32)).astype(BF16)
    return hi, lo


def _dot3(a, b):
    ah, al = _split(a)
    bh, bl = _split(b)
    return _dot(ah, bh) + _dot(al, bh) + _dot(ah, bl)


def _sigmoid(x):
    return 1.0 / (1.0 + jnp.exp(-x))


def _silu(x):
    return x * _sigmoid(x)


def _layer_norm(y, g, b):
    mu = jnp.mean(y, axis=-1, keepdims=True)
    yc = y - mu
    var = jnp.mean(yc * yc, axis=-1, keepdims=True)
    return yc * lax.rsqrt(var + EPS) * g + b


def _proj_kernel(a_ref, w_ref, o_ref):
    o_ref[...] = _dot(a_ref[...], w_ref[...])


def _proj(xb, w, tm, tn):
    m, k = xb.shape
    n = w.shape[1]
    return pl.pallas_call(
        _proj_kernel,
        grid=(m // tm, n // tn),
        in_specs=[pl.BlockSpec((tm, k), lambda i, j: (i, 0)),
                  pl.BlockSpec((k, tn), lambda i, j: (0, j))],
        out_specs=pl.BlockSpec((tm, tn), lambda i, j: (i, j)),
        out_shape=jax.ShapeDtypeStruct((m, n), F32),
        compiler_params=_params("parallel", "parallel"),
        name="proj",
    )(xb, w)


def _unit_lower_inverse4(als):
    c = CHUNK
    n = als[0].shape[1]
    row = lax.broadcasted_iota(I32, (c, n), 0)
    col = jnp.bitwise_and(lax.broadcasted_iota(I32, (c, n), 1), c - 1)
    shift = SOLVE_BLOCK.bit_length() - 1
    same = jnp.right_shift(row, shift) == jnp.right_shift(col, shift)
    eye = (row == col).astype(F32)
    cshift = c.bit_length() - 1
    brow = jnp.right_shift(lax.broadcasted_iota(I32, (n, n), 0), cshift)
    bcol = jnp.right_shift(lax.broadcasted_iota(I32, (n, n), 1), cshift)
    on_diag = brow == bcol

    def mm(x, y):
        yb = y.astype(BF16)
        bd = jnp.where(on_diag, jnp.concatenate([yb] * (n // c), axis=0), jnp.zeros((), BF16))
        return _dot(x.astype(BF16), bd)

    a_diag = [jnp.where(same, al, 0.0) for al in als]
    a_off = [al - ad for al, ad in zip(als, a_diag)]
    bp = [-ad for ad in a_diag]
    p = [eye + b for b in bp]
    for _ in range(3):
        bp = [mm(b, b) for b in bp]
        p = [x + mm(x, b) for x, b in zip(p, bp)]
    n1 = [mm(x, ao) for x, ao in zip(p, a_off)]
    n2 = [mm(x, x) for x in n1]
    q = [x + mm(y, x) for x, y in zip(p, n2)]
    return [x - mm(y, x) for x, y in zip(q, n1)]


def _gdn_intra_kernel(qkv_ref, prev_ref, x_ref, wah_ref, wal_ref, wbh_ref, wbl_ref, cw_ref, prm_ref,
                      u_ref, w_ref, qd_ref, m2_ref, eg_ref, xe_ref, *, rt, tiles_per_seq):
    c = CHUNK
    nc = rt // c
    first = (pl.program_id(0) % tiles_per_seq) == 0
    xe_ref[0:8, :] = jnp.where(first, 0.0, prev_ref[...])
    xe_ref[8:8 + rt, :] = qkv_ref[...]

    xh, xl = _split(x_ref[...])

    def proj3(wh_ref, wl_ref):
        wh = wh_ref[...]
        return _dot(xh, wh) + _dot(xl, wh) + _dot(xh, wl_ref[...])

    a_raw = proj3(wah_ref, wal_ref)
    b_raw = proj3(wbh_ref, wbl_ref)
    sp_in = a_raw + prm_ref[1:2, :]
    softplus = jnp.maximum(sp_in, 0.0) + jnp.log1p(jnp.exp(-jnp.abs(sp_in)))
    g = -jnp.exp(prm_ref[0:1, :]) * softplus
    beta = _sigmoid(b_raw)

    cshift = c.bit_length() - 1
    r2 = lax.broadcasted_iota(I32, (rt, rt), 0)
    c2 = lax.broadcasted_iota(I32, (rt, rt), 1)
    same_chunk = jnp.right_shift(r2, cshift) == jnp.right_shift(c2, cshift)
    ltri = jnp.where(r2 >= c2, jnp.where(same_chunk, 1.0, 0.0), 0.0).astype(BF16)
    g_hi = g.astype(BF16)
    g_r = g - g_hi.astype(F32)
    g_mid = g_r.astype(BF16)
    g_lo = (g_r - g_mid.astype(F32)).astype(BF16)
    gc = _dot(ltri, g_hi) + _dot(ltri, g_mid) + _dot(ltri, g_lo)
    gct = gc.T
    egc = jnp.exp(gc)
    gend = jnp.concatenate(
        [jnp.broadcast_to(gc[ci * c + c - 1:ci * c + c, :], (c, LANES)) for ci in range(nc)], axis=0)
    kfac = jnp.exp(gend - gc)
    bege = beta * egc
    for ci in range(nc):
        last = ci * c + c - 1
        eg_ref[ci * GDN_HEADS:(ci + 1) * GDN_HEADS, :] = jnp.broadcast_to(
            jnp.exp(gct[0:GDN_HEADS, last:last + 1]), (GDN_HEADS, LANES))

    lane_t = lax.broadcasted_iota(I32, (rt, LANES), 1) < c
    lane_lo = lax.broadcasted_iota(I32, (c, LANES), 1) < c
    lcol = jnp.bitwise_and(lax.broadcasted_iota(I32, (c, LANES), 1), c - 1)
    rowi = lax.broadcasted_iota(I32, (c, LANES), 0)
    causal = rowi >= lcol
    strict = rowi > lcol

    def conv(base, h):
        lo, hi = base + h * HEAD_DIM, base + (h + 1) * HEAD_DIM
        acc = cw_ref[SHORT_CONV - 1:SHORT_CONV, lo:hi] * xe_ref[8:8 + rt, lo:hi]
        for j in range(SHORT_CONV - 1):
            s0 = 8 - (SHORT_CONV - 1) + j
            acc = acc + cw_ref[j:j + 1, lo:hi] * xe_ref[s0:s0 + rt, lo:hi]
        return _silu(acc)

    a_pairs = [[None] * (GDN_HEADS // 2) for _ in range(nc)]
    rhs_pairs = [[None] * (GDN_HEADS // 2) for _ in range(nc)]
    for p in range(GDN_HEADS // 2):
        ks, kbs, qs, kds, rhss = [], [], [], [], []
        for h in (2 * p, 2 * p + 1):
            q = conv(0, h)
            k = conv(GDN_WIDTH, h)
            v = conv(2 * GDN_WIDTH, h)
            q = q * lax.rsqrt(jnp.sum(q * q, axis=-1, keepdims=True) + EPS) * (HEAD_DIM ** -0.5)
            k = k * lax.rsqrt(jnp.sum(k * k, axis=-1, keepdims=True) + EPS)
            beta_h = beta[:, h:h + 1]
            qd_ref[:, h * HEAD_DIM:(h + 1) * HEAD_DIM] = (q * egc[:, h:h + 1]).astype(BF16)
            ks.append(k)
            kbs.append(k * beta_h)
            qs.append(q)
            kds.append(k * kfac[:, h:h + 1])
            rhss.append(jnp.concatenate([v * beta_h, k * bege[:, h:h + 1]], axis=1))
        h0, h1 = 2 * p, 2 * p + 1
        gch = jnp.where(lane_t, gc[:, h0:h0 + 1], gc[:, h1:h1 + 1])
        for ci in range(nc):
            rows = slice(ci * c, (ci + 1) * c)
            wk = jnp.concatenate([ks[0][rows], ks[1][rows]], axis=0).astype(BF16)
            lhs = jnp.concatenate([kbs[0][rows], qs[0][rows], kbs[1][rows], qs[1][rows]],
                                  axis=0).astype(BF16)
            out = _dot_nt(lhs, wk)
            gcrow = jnp.concatenate([gct[h0:h0 + 1, rows], gct[h1:h1 + 1, rows]], axis=1)
            diff = gch[rows] - gcrow
            decay = jnp.where(causal, jnp.exp(jnp.where(causal, diff, 0.0)), 0.0)
            a_pairs[ci][p] = jnp.where(strict, jnp.where(lane_lo, out[0:c], out[2 * c:3 * c]) * decay, 0.0)
            qk = jnp.where(lane_lo, out[c:2 * c], out[3 * c:4 * c]) * decay
            kdt = jnp.concatenate([kds[0][rows], kds[1][rows]], axis=0).T
            m0 = ci * 3 * c
            m2_ref[m0:m0 + c, p * LANES:(p + 1) * LANES] = qk.astype(BF16)
            m2_ref[m0 + c:m0 + 3 * c, p * LANES:(p + 1) * LANES] = kdt.astype(BF16)
            rhs_pairs[ci][p] = (rhss[0][rows], rhss[1][rows])

    zeros = jnp.zeros((c, 2 * HEAD_DIM), BF16)
    ngrp = GDN_HEADS // 4
    tls = _unit_lower_inverse4(
        [jnp.concatenate([a_pairs[ci][2 * grp], a_pairs[ci][2 * grp + 1]], axis=1)
         for ci in range(nc) for grp in range(ngrp)])
    for ci in range(nc):
        rows = slice(ci * c, (ci + 1) * c)
        for grp in range(ngrp):
            tl = tls[ci * ngrp + grp]
            for j in range(2):
                p = 2 * grp + j
                r0, r1 = rhs_pairs[ci][p]
                bd = jnp.concatenate([jnp.concatenate([r0.astype(BF16), zeros], axis=1),
                                      jnp.concatenate([zeros, r1.astype(BF16)], axis=1)], axis=0)
                sol = _dot(tl[:, j * LANES:(j + 1) * LANES].astype(BF16), bd)
                for i in range(2):
                    h = 2 * p + i
                    lo, hi = h * HEAD_DIM, (h + 1) * HEAD_DIM
                    u_ref[rows, lo:hi] = sol[:, 2 * i * HEAD_DIM:(2 * i + 1) * HEAD_DIM]
                    w_ref[rows, lo:hi] = sol[:, (2 * i + 1) * HEAD_DIM:(2 * i + 2) * HEAD_DIM].astype(BF16)


def _gdn_intra(proj, x, wa_hi, wa_lo, wb_hi, wb_lo, conv_w, prm, seq, rt):
    t = x.shape[0]
    nc = rt // CHUNK
    kern = functools.partial(_gdn_intra_kernel, rt=rt, tiles_per_seq=seq // rt)
    full = lambda i: (0, 0)
    tile = lambda i: (i, 0)
    return pl.pallas_call(
        kern,
        grid=(t // rt,),
        in_specs=[
            pl.BlockSpec((rt, 3 * GDN_WIDTH), tile),
            pl.BlockSpec((8, 3 * GDN_WIDTH), lambda i: (jnp.maximum(i * (rt // 8) - 1, 0), 0)),
            pl.BlockSpec((rt, D_MODEL), tile),
            pl.BlockSpec((D_MODEL, LANES), full),
            pl.BlockSpec((D_MODEL, LANES), full),
            pl.BlockSpec((D_MODEL, LANES), full),
            pl.BlockSpec((D_MODEL, LANES), full),
            pl.BlockSpec((SHORT_CONV, 3 * GDN_WIDTH), full),
            pl.BlockSpec((8, LANES), full),
        ],
        out_specs=[pl.BlockSpec((rt, GDN_WIDTH), tile),
                   pl.BlockSpec((rt, GDN_WIDTH), tile),
                   pl.BlockSpec((rt, GDN_WIDTH), tile),
                   pl.BlockSpec((nc * 3 * CHUNK, GDN_HEADS // 2 * LANES), tile),
                   pl.BlockSpec((nc * GDN_HEADS, LANES), tile)],
        out_shape=[jax.ShapeDtypeStruct((t, GDN_WIDTH), F32),
                   jax.ShapeDtypeStruct((t, GDN_WIDTH), BF16),
                   jax.ShapeDtypeStruct((t, GDN_WIDTH), BF16),
                   jax.ShapeDtypeStruct((t // CHUNK * 3 * CHUNK, GDN_HEADS // 2 * LANES), BF16),
                   jax.ShapeDtypeStruct((t // CHUNK * GDN_HEADS, LANES), F32)],
        scratch_shapes=[pltpu.VMEM((8 + rt, 3 * GDN_WIDTH), F32)],
        compiler_params=_params("parallel"),
        name="gdn_intra",
    )(proj, proj, x, wa_hi, wa_lo, wb_hi, wb_lo, conv_w, prm)


def _gdn_scan_kernel(u_ref, w_ref, qd_ref, m2_ref, eg_ref, z_ref, ong_ref, o_ref, s_ref):
    c = CHUNK

    @pl.when(pl.program_id(1) == 0)
    def _():
        s_ref[...] = jnp.zeros_like(s_ref)

    ong = ong_ref[...]
    zeros = jnp.zeros((c, HEAD_DIM), BF16)
    heads = range(GDN_HEADS)
    span = lambda h: slice(h * HEAD_DIM, (h + 1) * HEAD_DIM)
    states = [s_ref[h] for h in heads]
    rs = [_dot(jnp.concatenate([w_ref[:, span(h)], qd_ref[:, span(h)]], axis=0), states[h].astype(BF16))
          for h in heads]
    v_new = [(u_ref[:, span(h)] - rs[h][:c]).astype(BF16) for h in heads]
    r2s = []
    for p in range(GDN_HEADS // 2):
        bd = jnp.concatenate([jnp.concatenate([v_new[2 * p], zeros], axis=1),
                              jnp.concatenate([zeros, v_new[2 * p + 1]], axis=1)], axis=0)
        r2s.append(_dot(m2_ref[:, p * LANES:(p + 1) * LANES], bd))
    for h in heads:
        half = span(h % 2)
        r2 = r2s[h // 2]
        s_ref[h] = states[h] * eg_ref[h:h + 1, :] + r2[c:, half]
        o = rs[h][c:] + r2[:c, half]
        o = o * lax.rsqrt(jnp.mean(o * o, axis=-1, keepdims=True) + EPS) * ong
        o = o * _silu(z_ref[:, span(h)])
        o_ref[:, span(h)] = o.astype(o_ref.dtype)


def _gdn_scan(u, w, qd, m2, eg, proj, ong, batch, seq):
    t = batch * seq
    nch = seq // CHUNK
    c = CHUNK
    blk = lambda b, n: (b * nch + n, 0)
    return pl.pallas_call(
        _gdn_scan_kernel,
        grid=(batch, nch),
        in_specs=[
            pl.BlockSpec((c, GDN_WIDTH), blk),
            pl.BlockSpec((c, GDN_WIDTH), blk),
            pl.BlockSpec((c, GDN_WIDTH), blk),
            pl.BlockSpec((3 * c, GDN_HEADS // 2 * LANES), blk),
            pl.BlockSpec((GDN_HEADS, LANES), blk),
            pl.BlockSpec((c, GDN_WIDTH), lambda b, n: (b * nch + n, 3)),
            pl.BlockSpec((1, HEAD_DIM), lambda b, n: (0, 0)),
        ],
        out_specs=pl.BlockSpec((c, GDN_WIDTH), blk),
        out_shape=jax.ShapeDtypeStruct((t, GDN_WIDTH), BF16),
        scratch_shapes=[pltpu.VMEM((GDN_HEADS, HEAD_DIM, HEAD_DIM), F32)],
        compiler_params=_params("parallel", "arbitrary"),
        name="gdn_scan",
    )(u, w, qd, m2, eg, proj, ong)


def _pack_bf16_pairs(y):
    n = y.shape[1] // 2
    yb = y.astype(BF16).astype(F32)
    lo = lax.bitcast_convert_type(yb[:, :n], U32)
    hi = lax.bitcast_convert_type(yb[:, n:], U32)
    return jnp.bitwise_or(jnp.right_shift(lo, 16), jnp.bitwise_and(hi, HI_MASK))


def _unpack_bf16_pairs(w):
    lo = lax.bitcast_convert_type(jnp.left_shift(w, 16), F32)
    hi = lax.bitcast_convert_type(jnp.bitwise_and(w, HI_MASK), F32)
    return lo, hi


def _mixer_kernel(glu_ref, ga_ref, gb_ref, o_ref, x_ref, woa_ref, wob_ref, wo_ref, dww_ref, vec_ref,
                  x1_ref, x1b_ref, x1p_ref, ubuf_ref, sh_ref, conv_ref, *, ts, rc):
    halo = CONF_HALO

    @pl.when(pl.program_id(1) == 0)
    def _():
        ubuf_ref[0:halo, :] = jnp.zeros((halo, CONF_CH), F32)

    @pl.when(pl.program_id(1) != 0)
    def _():
        ubuf_ref[0:halo, :] = ubuf_ref[ts:ts + halo, :]

    ubuf_ref[halo:halo + ts, :] = glu_ref[:, :CONF_CH] * _sigmoid(glu_ref[:, CONF_CH:])

    dw_b = vec_ref[0:1, :]
    cln_g = vec_ref[1:2, :]
    cln_b = vec_ref[2:3, :]
    ln1_g = vec_ref[3:4, :]
    ln1_b = vec_ref[4:5, :]

    span = ts + halo - 8
    for s in range(1, 8):
        sh_ref[s - 1] = ubuf_ref[s:s + span, :]

    def tap_rows(j, r0):
        o = halo - (CONF_KERNEL - 1) + j
        q, s = o // 8, o % 8
        if s == 0:
            return ubuf_ref[r0 + o:r0 + o + rc, :]
        return sh_ref[s - 1, r0 + 8 * q:r0 + 8 * q + rc, :]

    for r0 in range(0, ts, rc):
        acc = dww_ref[0:1, :] * tap_rows(0, r0)
        for j in range(1, CONF_KERNEL):
            acc = acc + dww_ref[j:j + 1, :] * tap_rows(j, r0)
        conv_ref[r0:r0 + rc, :] = acc

    uc = _silu(_layer_norm(conv_ref[...] + dw_b, cln_g, cln_b))
    branch_b = _dot(uc.astype(BF16), wob_ref[...])
    branch_a = _dot(o_ref[...], woa_ref[...])
    hmix = _sigmoid(ga_ref[...]) * branch_a + _sigmoid(gb_ref[...]) * branch_b
    mix = _dot(hmix.astype(BF16), wo_ref[...])
    x1 = _layer_norm(DEEPNORM_ALPHA * x_ref[...] + mix, ln1_g, ln1_b)
    x1_ref[...] = x1
    x1b_ref[...] = x1.astype(BF16)
    x1p_ref[...] = _pack_bf16_pairs(x1)


def _mixer(proj, o_gdn, x, woa, wob, wo, dww, vec, batch, seq, ts):
    t = batch * seq
    nt = seq // ts
    rows = lambda b, n: b * nt + n
    full = lambda b, n: (0, 0)
    kern = functools.partial(_mixer_kernel, ts=ts, rc=32)
    return pl.pallas_call(
        kern,
        grid=(batch, nt),
        in_specs=[
            pl.BlockSpec((ts, 2 * CONF_CH), lambda b, n: (rows(b, n), 2)),
            pl.BlockSpec((ts, D_MODEL), lambda b, n: (rows(b, n), 6)),
            pl.BlockSpec((ts, D_MODEL), lambda b, n: (rows(b, n), 7)),
            pl.BlockSpec((ts, GDN_WIDTH), lambda b, n: (rows(b, n), 0)),
            pl.BlockSpec((ts, D_MODEL), lambda b, n: (rows(b, n), 0)),
            pl.BlockSpec((GDN_WIDTH, D_MODEL), full),
            pl.BlockSpec((CONF_CH, D_MODEL), full),
            pl.BlockSpec((D_MODEL, D_MODEL), full),
            pl.BlockSpec((CONF_HALO, CONF_CH), full),
            pl.BlockSpec((8, D_MODEL), full),
        ],
        out_specs=[pl.BlockSpec((ts, D_MODEL), lambda b, n: (rows(b, n), 0)),
                   pl.BlockSpec((ts, D_MODEL), lambda b, n: (rows(b, n), 0)),
                   pl.BlockSpec((ts, D_MODEL // 2), lambda b, n: (rows(b, n), 0))],
        out_shape=[jax.ShapeDtypeStruct((t, D_MODEL), F32),
                   jax.ShapeDtypeStruct((t, D_MODEL), BF16),
                   jax.ShapeDtypeStruct((t, D_MODEL // 2), U32)],
        scratch_shapes=[pltpu.VMEM((CONF_HALO + ts, CONF_CH), F32),
                        pltpu.VMEM((7, ts + CONF_HALO - 8, CONF_CH), F32),
                        pltpu.VMEM((ts, CONF_CH), F32)],
        compiler_params=_params("parallel", "arbitrary"),
        name="mixer",
    )(proj, proj, proj, o_gdn, x, woa, wob, wo, dww, vec)


def _route_kernel(x_ref, wrh_ref, wrl_ref, bias_ref, eidx_ref, wts_ref, rank_ref, cnt_ref, carry_ref, *, tt):
    @pl.when(pl.program_id(0) == 0)
    def _():
        carry_ref[...] = jnp.zeros_like(carry_ref)

    xh, xl = _split(x_ref[...])
    wrh = wrh_ref[...]
    logits = _dot_nt(wrh, xh) + _dot_nt(wrh, xl) + _dot_nt(wrl_ref[...], xh)
    s = _sigmoid(logits)
    biased = s + bias_ref[...]

    sub = lax.broadcasted_iota(I32, (GROUP_SIZE, tt), 0)
    groups = [biased[g * GROUP_SIZE:(g + 1) * GROUP_SIZE, :] for g in range(N_GROUPS)]
    gs = []
    for bg in groups:
        m1 = jnp.max(bg, axis=0, keepdims=True)
        first = jnp.min(jnp.where(bg == m1, sub, GROUP_SIZE), axis=0, keepdims=True)
        m2 = jnp.max(jnp.where(sub == first, -jnp.inf, bg), axis=0, keepdims=True)
        gs.append(m1 + m2)

    masked_parts = []
    for g in range(N_GROUPS):
        beaten = jnp.zeros((1, tt), I32)
        for o in range(N_GROUPS):
            if o == g:
                continue
            wins = (gs[o] >= gs[g]) if o < g else (gs[o] > gs[g])
            beaten = beaten + wins.astype(I32)
        keep = jnp.broadcast_to(beaten < TOPK_GROUPS, (GROUP_SIZE, tt))
        masked_parts.append(jnp.where(keep, groups[g], -jnp.inf))
    masked = jnp.concatenate(masked_parts, axis=0)

    eiota = lax.broadcasted_iota(I32, (N_EXPERTS, tt), 0)
    sel_all = jnp.zeros((N_EXPERTS, tt), F32)
    picks = []
    for _ in range(TOP_K):
        m = jnp.max(masked, axis=0, keepdims=True)
        idx = jnp.min(jnp.where(masked == m, eiota, N_EXPERTS), axis=0, keepdims=True)
        onehot = eiota == idx
        picks.append((idx, onehot))
        sel_all = jnp.where(onehot, 1.0, sel_all)
        masked = jnp.where(onehot, -jnp.inf, masked)

    tr = lax.broadcasted_iota(I32, (tt, tt), 0)
    tc = lax.broadcasted_iota(I32, (tt, tt), 1)
    before = (tr < tc).astype(BF16)
    sel_b = sel_all.astype(BF16)
    carry = carry_ref[...]
    rank_all = _dot(sel_b, before) + carry[:, 0:1]
    carry_new = carry + _dot(sel_b, jnp.ones((tt, LANES), BF16))
    carry_ref[...] = carry_new
    cnt_ref[...] = carry_new

    s_sel = [jnp.sum(jnp.where(oh, s, 0.0), axis=0, keepdims=True) for _, oh in picks]
    total = s_sel[0]
    for v in s_sel[1:]:
        total = total + v
    for k, (idx, oh) in enumerate(picks):
        eidx_ref[k:k + 1, :] = idx
        wts_ref[k:k + 1, :] = s_sel[k] / total * ROUTED_SCALE
        rank_ref[k:k + 1, :] = jnp.sum(jnp.where(oh, rank_all, 0.0), axis=0, keepdims=True).astype(I32)


def _route(x1, wr_hi, wr_lo, bias, tt):
    t = x1.shape[0]
    kern = functools.partial(_route_kernel, tt=tt)
    return pl.pallas_call(
        kern,
        grid=(t // tt,),
        in_specs=[pl.BlockSpec((tt, D_MODEL), lambda i: (i, 0)),
                  pl.BlockSpec((N_EXPERTS, D_MODEL), lambda i: (0, 0)),
                  pl.BlockSpec((N_EXPERTS, D_MODEL), lambda i: (0, 0)),
                  pl.BlockSpec((N_EXPERTS, tt), lambda i: (0, 0))],
        out_specs=[pl.BlockSpec((TOP_K, tt), lambda i: (0, i)),
                   pl.BlockSpec((TOP_K, tt), lambda i: (0, i)),
                   pl.BlockSpec((TOP_K, tt), lambda i: (0, i)),
                   pl.BlockSpec((N_EXPERTS, LANES), lambda i: (0, 0))],
        out_shape=[jax.ShapeDtypeStruct((TOP_K, t), I32),
                   jax.ShapeDtypeStruct((TOP_K, t), F32),
                   jax.ShapeDtypeStruct((TOP_K, t), I32),
                   jax.ShapeDtypeStruct((N_EXPERTS, LANES), F32)],
        scratch_shapes=[pltpu.VMEM((N_EXPERTS, LANES), F32)],
        compiler_params=_params("arbitrary"),
        name="route",
    )(x1, wr_hi, wr_lo, bias)


def _dest_kernel(eidx_ref, rank_ref, pstart_ref, dest_ref):
    eidx = eidx_ref[...]
    acc = rank_ref[...]
    for e in range(N_EXPERTS):
        acc = acc + jnp.where(eidx == e, pstart_ref[e], 0)
    dest_ref[0] = acc


def _dest(eidx_t, rank_t, pstart, tt):
    t = eidx_t.shape[1]
    return pl.pallas_call(
        _dest_kernel,
        grid=(t // tt,),
        in_specs=[pl.BlockSpec((TOP_K, tt), lambda i: (0, i)),
                  pl.BlockSpec((TOP_K, tt), lambda i: (0, i)),
                  pl.BlockSpec(memory_space=pltpu.SMEM)],
        out_specs=pl.BlockSpec((1, TOP_K, tt), lambda i: (i, 0, 0)),
        out_shape=jax.ShapeDtypeStruct((t // tt, TOP_K, tt), I32),
        compiler_params=_params("parallel"),
        name="dest",
    )(eidx_t, rank_t, pstart)


def _row_copy_out(x_ref, xs_ref, sem, t, row):
    return pltpu.make_async_copy(x_ref.at[pl.ds(t, 1)], xs_ref.at[pl.ds(row, 1)], sem)


def _dispatch_kernel(dest_hbm, x_ref, xs_in, xs_ref, idx_ref, isem, sem, *, tt):
    del xs_in
    i = pl.program_id(0)
    icopy = pltpu.make_async_copy(dest_hbm.at[i], idx_ref, isem)
    icopy.start()
    icopy.wait()

    def issue(t, carry):
        for k in range(TOP_K):
            _row_copy_out(x_ref, xs_ref, sem, t, idx_ref[k * tt + t]).start(priority=k % 2)
        return carry

    lax.fori_loop(0, tt, issue, 0)

    def drain(t, carry):
        for k in range(TOP_K):
            _row_copy_out(x_ref, xs_ref, sem, 0, 0).wait()
        return carry

    lax.fori_loop(0, tt, drain, 0)


def _dispatch(dest, x1p, n_rows, tt):
    t, width = x1p.shape
    kern = functools.partial(_dispatch_kernel, tt=tt)
    xs0 = jnp.zeros((n_rows, width), x1p.dtype)
    return pl.pallas_call(
        kern,
        grid=(t // tt,),
        in_specs=[pl.BlockSpec(memory_space=pl.ANY),
                  pl.BlockSpec((tt, width), lambda i: (i, 0)),
                  pl.BlockSpec(memory_space=pl.ANY)],
        out_specs=pl.BlockSpec(memory_space=pl.ANY),
        out_shape=jax.ShapeDtypeStruct((n_rows, width), x1p.dtype),
        scratch_shapes=[pltpu.SMEM((TOP_K * tt,), I32),
                        pltpu.SemaphoreType.DMA,
                        pltpu.SemaphoreType.DMA],
        input_output_aliases={2: 0},
        compiler_params=_params("arbitrary"),
        name="dispatch",
    )(dest, x1p, xs0)


def _experts_kernel(be_ref, xs_ref, wg_ref, wu_ref, wd_ref, ys_ref):
    del be_ref
    lo, hi = _unpack_bf16_pairs(xs_ref[...])
    xb = jnp.concatenate([lo.astype(BF16), hi.astype(BF16)], axis=1)
    gate = _dot(xb, wg_ref[...])
    up = _dot(xb, wu_ref[...])
    hid = (_silu(gate) * up).astype(BF16)
    ys_ref[...] = _pack_bf16_pairs(_dot(hid, wd_ref[...]))


def _experts(block_e, xs, wg, wu, wd):
    n_rows, width = xs.shape
    nb = n_rows // ROW_BLOCK
    grid_spec = pltpu.PrefetchScalarGridSpec(
        num_scalar_prefetch=1,
        grid=(nb,),
        in_specs=[pl.BlockSpec((ROW_BLOCK, width), lambda i, be: (i, 0)),
                  pl.BlockSpec((None, D_MODEL, EXPERT_FF), lambda i, be: (be[i], 0, 0)),
                  pl.BlockSpec((None, D_MODEL, EXPERT_FF), lambda i, be: (be[i], 0, 0)),
                  pl.BlockSpec((None, EXPERT_FF, D_MODEL), lambda i, be: (be[i], 0, 0))],
        out_specs=pl.BlockSpec((ROW_BLOCK, width), lambda i, be: (i, 0)),
    )
    return pl.pallas_call(
        _experts_kernel,
        grid_spec=grid_spec,
        out_shape=jax.ShapeDtypeStruct((n_rows, width), xs.dtype),
        compiler_params=_params("arbitrary"),
        name="experts",
    )(block_e, xs, wg, wu, wd)


def _row_copy_in(ys_ref, buf_ref, sem, k, t, row):
    return pltpu.make_async_copy(ys_ref.at[pl.ds(row, 1)], buf_ref.at[k, pl.ds(t, 1)], sem)


def _combine_kernel(dest_hbm, ys_ref, w_ref, x1_ref, x1b_ref, wsg_ref, wsu_ref, wsd_ref, vec_ref,
                    x2_ref, x2b_ref, idx_ref, buf_ref, isem, sem, *, tt):
    i = pl.program_id(0)
    icopy = pltpu.make_async_copy(dest_hbm.at[i], idx_ref, isem)
    icopy.start()
    icopy.wait()

    def issue(t, carry):
        for k in range(TOP_K):
            _row_copy_in(ys_ref, buf_ref, sem, k, t, idx_ref[k * tt + t]).start(priority=k % 2)
        return carry

    lax.fori_loop(0, tt, issue, 0)

    xb = x1b_ref[...]
    hid = (_silu(_dot(xb, wsg_ref[...])) * _dot(xb, wsu_ref[...])).astype(BF16)
    shared = _dot(hid, wsd_ref[...])

    def drain(t, carry):
        for k in range(TOP_K):
            _row_copy_in(ys_ref, buf_ref, sem, 0, 0, 0).wait()
        return carry

    lax.fori_loop(0, tt, drain, 0)

    w = w_ref[...]
    half = D_MODEL // 2
    acc_lo, acc_hi = shared[:, :half], shared[:, half:]
    for k in range(TOP_K):
        lo, hi = _unpack_bf16_pairs(buf_ref[k])
        acc_lo = acc_lo + w[:, k:k + 1] * lo
        acc_hi = acc_hi + w[:, k:k + 1] * hi
    acc = jnp.concatenate([acc_lo, acc_hi], axis=1)
    x2 = _layer_norm(DEEPNORM_ALPHA * x1_ref[...] + acc, vec_ref[0:1, :], vec_ref[1:2, :])
    x2_ref[...] = x2
    x2b_ref[...] = x2.astype(BF16)


def _combine(dest, ys, w_tok, x1, x1b, wsg, wsu, wsd, vec, tt):
    t = x1.shape[0]
    full = lambda i: (0, 0)
    kern = functools.partial(_combine_kernel, tt=tt)
    return pl.pallas_call(
        kern,
        grid=(t // tt,),
        in_specs=[pl.BlockSpec(memory_space=pl.ANY),
                  pl.BlockSpec(memory_space=pl.ANY),
                  pl.BlockSpec((tt, TOP_K), lambda i: (i, 0)),
                  pl.BlockSpec((tt, D_MODEL), lambda i: (i, 0)),
                  pl.BlockSpec((tt, D_MODEL), lambda i: (i, 0)),
                  pl.BlockSpec((D_MODEL, SHARED_FF), full),
                  pl.BlockSpec((D_MODEL, SHARED_FF), full),
                  pl.BlockSpec((SHARED_FF, D_MODEL), full),
                  pl.BlockSpec((8, D_MODEL), full)],
        out_specs=[pl.BlockSpec((tt, D_MODEL), lambda i: (i, 0)),
                   pl.BlockSpec((tt, D_MODEL), lambda i: (i, 0))],
        out_shape=[jax.ShapeDtypeStruct((t, D_MODEL), F32),
                   jax.ShapeDtypeStruct((t, D_MODEL), BF16)],
        scratch_shapes=[pltpu.SMEM((TOP_K * tt,), I32),
                        pltpu.VMEM((TOP_K, tt, ys.shape[1]), ys.dtype),
                        pltpu.SemaphoreType.DMA,
                        pltpu.SemaphoreType.DMA],
        compiler_params=_params("arbitrary"),
        name="combine",
    )(dest, ys, w_tok, x1, x1b, wsg, wsu, wsd, vec)


def _pad_rows(a, rows):
    return jnp.zeros((rows, a.shape[-1]), F32).at[:a.shape[0]].set(a.astype(F32))


def _layer(x, xb, p, batch, seq):
    t = batch * seq
    gw = GDN_WIDTH
    w_in = p["w_in"]
    c_z, c_a, c_glu = 3 * gw, 4 * gw, 4 * gw + 2 * GDN_HEADS
    c_ga = c_glu + 2 * CONF_CH
    c_gb = c_ga + D_MODEL
    w_main = jnp.concatenate([w_in[:, :c_a], w_in[:, c_glu:]], axis=1).astype(BF16)
    w_a = jnp.zeros((D_MODEL, LANES), F32).at[:, :GDN_HEADS].set(w_in[:, c_a:c_a + GDN_HEADS])
    w_b = jnp.zeros((D_MODEL, LANES), F32).at[:, :GDN_HEADS].set(w_in[:, c_a + GDN_HEADS:c_glu])
    wa_hi, wb_hi = w_a.astype(BF16), w_b.astype(BF16)
    wa_lo = (w_a - wa_hi.astype(F32)).astype(BF16)
    wb_lo = (w_b - wb_hi.astype(F32)).astype(BF16)
    del c_z, c_gb

    tm = min(1024, t)
    proj = _proj(xb, w_main, tm, 1024)

    prm = jnp.zeros((8, LANES), F32)
    prm = prm.at[0, :GDN_HEADS].set(p["a_log"]).at[1, :GDN_HEADS].set(p["dt_bias"])
    ong = p["o_norm_g"].reshape(1, HEAD_DIM).astype(F32)
    u, w, qd, m2, eg = _gdn_intra(proj, x, wa_hi, wa_lo, wb_hi, wb_lo, p["conv_qkv"].astype(F32), prm,
                                  seq, min(GDN_TILE, seq))
    o_gdn = _gdn_scan(u, w, qd, m2, eg, proj, ong, batch, seq)

    ts = min(256, seq)
    dww = _pad_rows(p["dw_w"], CONF_HALO)
    vec = _pad_rows(jnp.stack([p["dw_b"], p["cln_g"], p["cln_b"], p["ln1_g"], p["ln1_b"]]), 8)
    x1, x1b, x1p = _mixer(proj, o_gdn, x, p["w_oa"].astype(BF16), p["w_ob"].astype(BF16),
                          p["w_o"].astype(BF16), dww, vec, batch, seq, ts)

    tt_r = min(256, t)
    wr_t = p["w_router"].T.astype(F32)
    wr_hi = wr_t.astype(BF16)
    wr_lo = (wr_t - wr_hi.astype(F32)).astype(BF16)
    bias = jnp.broadcast_to(p["router_bias"].astype(F32)[:, None], (N_EXPERTS, tt_r))
    eidx_t, wts_t, rank_t, cnt = _route(x1, wr_hi, wr_lo, bias, tt_r)

    counts = cnt[:, 0].astype(I32)
    padded = (counts + ROW_BLOCK - 1) // ROW_BLOCK * ROW_BLOCK
    pend = jnp.cumsum(padded)
    pstart = (pend - padded).astype(I32)
    n_blocks = -(-(t * TOP_K + N_EXPERTS * (ROW_BLOCK - 1)) // ROW_BLOCK)
    n_rows = n_blocks * ROW_BLOCK
    block_start = jnp.arange(n_blocks, dtype=I32) * ROW_BLOCK
    owner = jnp.sum((pend[None, :] <= block_start[:, None]).astype(I32), axis=1)
    block_e = jnp.minimum(owner, N_EXPERTS - 1).astype(I32)

    tt_d = 128
    dest = _dest(eidx_t, rank_t, pstart, tt_d).reshape(t // tt_d, TOP_K * tt_d)
    xs = _dispatch(dest, x1p, n_rows, tt_d)
    ys = _experts(block_e, xs, p["w_gate_e"].astype(BF16), p["w_up_e"].astype(BF16),
                  p["w_down_e"].astype(BF16))
    vec2 = _pad_rows(jnp.stack([p["ln2_g"], p["ln2_b"]]), 8)
    x2, x2b = _combine(dest, ys, wts_t.T, x1, x1b, p["w_sh_gate"].astype(BF16),
                       p["w_sh_up"].astype(BF16), p["w_sh_down"].astype(BF16), vec2, tt_d)
    return x2, x2b


_PARAM_NAMES = ("w_in", "conv_qkv", "a_log", "dt_bias", "o_norm_g", "w_oa", "dw_w", "dw_b", "cln_g",
                "cln_b", "w_ob", "w_o", "ln1_g", "ln1_b", "w_router", "router_bias", "w_gate_e",
                "w_up_e", "w_down_e", "w_sh_gate", "w_sh_up", "w_sh_down", "ln2_g", "ln2_b")


def kernel(x, w_in, conv_qkv, a_log, dt_bias, o_norm_g, w_oa, dw_w, dw_b, cln_g, cln_b, w_ob, w_o,
           ln1_g, ln1_b, w_router, router_bias, w_gate_e, w_up_e, w_down_e, w_sh_gate, w_sh_up,
           w_sh_down, ln2_g, ln2_b):
    stacked = dict(zip(_PARAM_NAMES, (w_in, conv_qkv, a_log, dt_bias, o_norm_g, w_oa, dw_w, dw_b, cln_g,
                                      cln_b, w_ob, w_o, ln1_g, ln1_b, w_router, router_bias, w_gate_e,
                                      w_up_e, w_down_e, w_sh_gate, w_sh_up, w_sh_down, ln2_g, ln2_b)))
    batch, seq, d = x.shape
    assert d == D_MODEL and seq % CHUNK == 0
    xf = x.reshape(batch * seq, d).astype(F32)
    xb = xf.astype(BF16)
    for layer in range(w_in.shape[0]):
        p = {name: arr[layer] for name, arr in stacked.items()}
        xf, xb = _layer(xf, xb, p, batch, seq)
    return xf.reshape(batch, seq, d).astype(x.dtype)
```

```python
import functools

import jax
import jax.numpy as jnp
import numpy as np
from jax import lax
from jax.experimental import pallas as pl
from jax.experimental.pallas import tpu as pltpu
from jax.experimental.pallas import tpu_sc as plsc

F32 = jnp.float32
BF16 = jnp.bfloat16
I32 = jnp.int32
U32 = jnp.uint32
HI_MASK = np.uint32(0xFFFF0000)

D_MODEL = 1024
GDN_HEADS = 8
HEAD_DIM = 128
GDN_WIDTH = GDN_HEADS * HEAD_DIM
SHORT_CONV = 4
CHUNK = 64
SOLVE_BLOCK = 16
GDN_TILE = 256
CONF_CH = D_MODEL
CONF_KERNEL = 31
CONF_HALO = 32
N_EXPERTS = 64
TOP_K = 8
N_GROUPS = 8
GROUP_SIZE = N_EXPERTS // N_GROUPS
TOPK_GROUPS = 4
EXPERT_FF = 256
SHARED_FF = 256
ROUTED_SCALE = 2.5
DEPTH = 2
DEEPNORM_ALPHA = (2 * DEPTH) ** 0.25
EPS = 1e-6

LANES = 128
PROJ_COLS = 8 * D_MODEL
ROW_BLOCK = 256
SC_SCATTER_ROWS = 128
SC_GATHER_ROWS = 64
VMEM_LIMIT = 56 * 1024 * 1024


def _params(*sem):
    return pltpu.CompilerParams(dimension_semantics=sem, vmem_limit_bytes=VMEM_LIMIT)


def _dot(a, b):
    return jnp.dot(a, b, preferred_element_type=F32)


def _dot_nt(a, b):
    return lax.dot_general(a, b, (((1,), (1,)), ((), ())), preferred_element_type=F32)


def _split(a):
    hi = a.astype(BF16)
    lo = (a - hi.astype(F32)).astype(BF16)
    return hi, lo


def _dot3(a, b):
    ah, al = _split(a)
    bh, bl = _split(b)
    return _dot(ah, bh) + _dot(al, bh) + _dot(ah, bl)


def _sigmoid(x):
    return 1.0 / (1.0 + jnp.exp(-x))


def _silu(x):
    return x * _sigmoid(x)


def _layer_norm(y, g, b):
    mu = jnp.mean(y, axis=-1, keepdims=True)
    yc = y - mu
    var = jnp.mean(yc * yc, axis=-1, keepdims=True)
    return yc * lax.rsqrt(var + EPS) * g + b


def _proj_kernel(a_ref, w_ref, o_ref):
    o_ref[...] = _dot(a_ref[...], w_ref[...])


def _proj(xb, w, tm, tn):
    m, k = xb.shape
    n = w.shape[1]
    return pl.pallas_call(
        _proj_kernel,
        grid=(m // tm, n // tn),
        in_specs=[pl.BlockSpec((tm, k), lambda i, j: (i, 0)),
                  pl.BlockSpec((k, tn), lambda i, j: (0, j))],
        out_specs=pl.BlockSpec((tm, tn), lambda i, j: (i, j)),
        out_shape=jax.ShapeDtypeStruct((m, n), F32),
        compiler_params=_params("parallel", "parallel"),
        name="proj",
    )(xb, w)


def _unit_lower_inverse4(als):
    c = CHUNK
    n = als[0].shape[1]
    row = lax.broadcasted_iota(I32, (c, n), 0)
    col = jnp.bitwise_and(lax.broadcasted_iota(I32, (c, n), 1), c - 1)
    shift = SOLVE_BLOCK.bit_length() - 1
    same = jnp.right_shift(row, shift) == jnp.right_shift(col, shift)
    eye = (row == col).astype(F32)
    cshift = c.bit_length() - 1
    brow = jnp.right_shift(lax.broadcasted_iota(I32, (n, n), 0), cshift)
    bcol = jnp.right_shift(lax.broadcasted_iota(I32, (n, n), 1), cshift)
    on_diag = brow == bcol

    def mm(x, y):
        yb = y.astype(BF16)
        bd = jnp.where(on_diag, jnp.concatenate([yb] * (n // c), axis=0), jnp.zeros((), BF16))
        return _dot(x.astype(BF16), bd)

    a_diag = [jnp.where(same, al, 0.0) for al in als]
    a_off = [al - ad for al, ad in zip(als, a_diag)]
    bp = [-ad for ad in a_diag]
    p = [eye + b for b in bp]
    for _ in range(3):
        bp = [mm(b, b) for b in bp]
        p = [x + mm(x, b) for x, b in zip(p, bp)]
    n1 = [mm(x, ao) for x, ao in zip(p, a_off)]
    n2 = [mm(x, x) for x in n1]
    q = [x + mm(y, x) for x, y in zip(p, n2)]
    return [x - mm(y, x) for x, y in zip(q, n1)]


def _gdn_intra_kernel(qkv_ref, prev_ref, x_ref, wah_ref, wal_ref, wbh_ref, wbl_ref, cw_ref, prm_ref,
                      u_ref, w_ref, qd_ref, m2_ref, eg_ref, xe_ref, *, rt, tiles_per_seq):
    c = CHUNK
    nc = rt // c
    first = (pl.program_id(0) % tiles_per_seq) == 0
    xe_ref[0:8, :] = jnp.where(first, 0.0, prev_ref[...])
    xe_ref[8:8 + rt, :] = qkv_ref[...]

    xh, xl = _split(x_ref[...])

    def proj3(wh_ref, wl_ref):
        wh = wh_ref[...]
        return _dot(xh, wh) + _dot(xl, wh) + _dot(xh, wl_ref[...])

    a_raw = proj3(wah_ref, wal_ref)
    b_raw = proj3(wbh_ref, wbl_ref)
    sp_in = a_raw + prm_ref[1:2, :]
    softplus = jnp.maximum(sp_in, 0.0) + jnp.log1p(jnp.exp(-jnp.abs(sp_in)))
    g = -jnp.exp(prm_ref[0:1, :]) * softplus
    beta = _sigmoid(b_raw)

    cshift = c.bit_length() - 1
    r2 = lax.broadcasted_iota(I32, (rt, rt), 0)
    c2 = lax.broadcasted_iota(I32, (rt, rt), 1)
    same_chunk = jnp.right_shift(r2, cshift) == jnp.right_shift(c2, cshift)
    ltri = jnp.where(r2 >= c2, jnp.where(same_chunk, 1.0, 0.0), 0.0).astype(BF16)
    g_hi = g.astype(BF16)
    g_r = g - g_hi.astype(F32)
    g_mid = g_r.astype(BF16)
    g_lo = (g_r - g_mid.astype(F32)).astype(BF16)
    gc = _dot(ltri, g_hi) + _dot(ltri, g_mid) + _dot(ltri, g_lo)
    gct = gc.T
    egc = jnp.exp(gc)
    gend = jnp.concatenate(
        [jnp.broadcast_to(gc[ci * c + c - 1:ci * c + c, :], (c, LANES)) for ci in range(nc)], axis=0)
    kfac = jnp.exp(gend - gc)
    bege = beta * egc
    for ci in range(nc):
        last = ci * c + c - 1
        eg_ref[ci * GDN_HEADS:(ci + 1) * GDN_HEADS, :] = jnp.broadcast_to(
            jnp.exp(gct[0:GDN_HEADS, last:last + 1]), (GDN_HEADS, LANES))

    lane_t = lax.broadcasted_iota(I32, (rt, LANES), 1) < c
    lane_lo = lax.broadcasted_iota(I32, (c, LANES), 1) < c
    lcol = jnp.bitwise_and(lax.broadcasted_iota(I32, (c, LANES), 1), c - 1)
    rowi = lax.broadcasted_iota(I32, (c, LANES), 0)
    causal = rowi >= lcol
    strict = rowi > lcol

    def conv(base, h):
        lo, hi = base + h * HEAD_DIM, base + (h + 1) * HEAD_DIM
        acc = cw_ref[SHORT_CONV - 1:SHORT_CONV, lo:hi] * xe_ref[8:8 + rt, lo:hi]
        for j in range(SHORT_CONV - 1):
            s0 = 8 - (SHORT_CONV - 1) + j
            acc = acc + cw_ref[j:j + 1, lo:hi] * xe_ref[s0:s0 + rt, lo:hi]
        return _silu(acc)

    a_pairs = [[None] * (GDN_HEADS // 2) for _ in range(nc)]
    rhs_pairs = [[None] * (GDN_HEADS // 2) for _ in range(nc)]
    for p in range(GDN_HEADS // 2):
        ks, kbs, qs, kds, rhss = [], [], [], [], []
        for h in (2 * p, 2 * p + 1):
            q = conv(0, h)
            k = conv(GDN_WIDTH, h)
            v = conv(2 * GDN_WIDTH, h)
            q = q * lax.rsqrt(jnp.sum(q * q, axis=-1, keepdims=True) + EPS) * (HEAD_DIM ** -0.5)
            k = k * lax.rsqrt(jnp.sum(k * k, axis=-1, keepdims=True) + EPS)
            beta_h = beta[:, h:h + 1]
            qd_ref[:, h * HEAD_DIM:(h + 1) * HEAD_DIM] = (q * egc[:, h:h + 1]).astype(BF16)
            ks.append(k)
            kbs.append(k * beta_h)
            qs.append(q)
            kds.append(k * kfac[:, h:h + 1])
            rhss.append(jnp.concatenate([v * beta_h, k * bege[:, h:h + 1]], axis=1))
        h0, h1 = 2 * p, 2 * p + 1
        gch = jnp.where(lane_t, gc[:, h0:h0 + 1], gc[:, h1:h1 + 1])
        for ci in range(nc):
            rows = slice(ci * c, (ci + 1) * c)
            wk = jnp.concatenate([ks[0][rows], ks[1][rows]], axis=0).astype(BF16)
            lhs = jnp.concatenate([kbs[0][rows], qs[0][rows], kbs[1][rows], qs[1][rows]],
                                  axis=0).astype(BF16)
            out = _dot_nt(lhs, wk)
            gcrow = jnp.concatenate([gct[h0:h0 + 1, rows], gct[h1:h1 + 1, rows]], axis=1)
            diff = gch[rows] - gcrow
            decay = jnp.where(causal, jnp.exp(jnp.where(causal, diff, 0.0)), 0.0)
            a_pairs[ci][p] = jnp.where(strict, jnp.where(lane_lo, out[0:c], out[2 * c:3 * c]) * decay, 0.0)
            qk = jnp.where(lane_lo, out[c:2 * c], out[3 * c:4 * c]) * decay
            kdt = jnp.concatenate([kds[0][rows], kds[1][rows]], axis=0).T
            m0 = ci * 3 * c
            m2_ref[m0:m0 + c, p * LANES:(p + 1) * LANES] = qk.astype(BF16)
            m2_ref[m0 + c:m0 + 3 * c, p * LANES:(p + 1) * LANES] = kdt.astype(BF16)
            rhs_pairs[ci][p] = (rhss[0][rows], rhss[1][rows])

    zeros = jnp.zeros((c, 2 * HEAD_DIM), BF16)
    ngrp = GDN_HEADS // 4
    tls = _unit_lower_inverse4(
        [jnp.concatenate([a_pairs[ci][2 * grp], a_pairs[ci][2 * grp + 1]], axis=1)
         for ci in range(nc) for grp in range(ngrp)])
    for ci in range(nc):
        rows = slice(ci * c, (ci + 1) * c)
        for grp in range(ngrp):
            tl = tls[ci * ngrp + grp]
            for j in range(2):
                p = 2 * grp + j
                r0, r1 = rhs_pairs[ci][p]
                bd = jnp.concatenate([jnp.concatenate([r0.astype(BF16), zeros], axis=1),
                                      jnp.concatenate([zeros, r1.astype(BF16)], axis=1)], axis=0)
                sol = _dot(tl[:, j * LANES:(j + 1) * LANES].astype(BF16), bd)
                for i in range(2):
                    h = 2 * p + i
                    lo, hi = h * HEAD_DIM, (h + 1) * HEAD_DIM
                    u_ref[rows, lo:hi] = sol[:, 2 * i * HEAD_DIM:(2 * i + 1) * HEAD_DIM]
                    w_ref[rows, lo:hi] = sol[:, (2 * i + 1) * HEAD_DIM:(2 * i + 2) * HEAD_DIM].astype(BF16)


def _gdn_intra(proj, x, wa_hi, wa_lo, wb_hi, wb_lo, conv_w, prm, seq, rt):
    t = x.shape[0]
    nc = rt // CHUNK
    kern = functools.partial(_gdn_intra_kernel, rt=rt, tiles_per_seq=seq // rt)
    full = lambda i: (0, 0)
    tile = lambda i: (i, 0)
    return pl.pallas_call(
        kern,
        grid=(t // rt,),
        in_specs=[
            pl.BlockSpec((rt, 3 * GDN_WIDTH), tile),
            pl.BlockSpec((8, 3 * GDN_WIDTH), lambda i: (jnp.maximum(i * (rt // 8) - 1, 0), 0)),
            pl.BlockSpec((rt, D_MODEL), tile),
            pl.BlockSpec((D_MODEL, LANES), full),
            pl.BlockSpec((D_MODEL, LANES), full),
            pl.BlockSpec((D_MODEL, LANES), full),
            pl.BlockSpec((D_MODEL, LANES), full),
            pl.BlockSpec((SHORT_CONV, 3 * GDN_WIDTH), full),
            pl.BlockSpec((8, LANES), full),
        ],
        out_specs=[pl.BlockSpec((rt, GDN_WIDTH), tile),
                   pl.BlockSpec((rt, GDN_WIDTH), tile),
                   pl.BlockSpec((rt, GDN_WIDTH), tile),
                   pl.BlockSpec((nc * 3 * CHUNK, GDN_HEADS // 2 * LANES), tile),
                   pl.BlockSpec((nc * GDN_HEADS, LANES), tile)],
        out_shape=[jax.ShapeDtypeStruct((t, GDN_WIDTH), F32),
                   jax.ShapeDtypeStruct((t, GDN_WIDTH), BF16),
                   jax.ShapeDtypeStruct((t, GDN_WIDTH), BF16),
                   jax.ShapeDtypeStruct((t // CHUNK * 3 * CHUNK, GDN_HEADS // 2 * LANES), BF16),
                   jax.ShapeDtypeStruct((t // CHUNK * GDN_HEADS, LANES), F32)],
        scratch_shapes=[pltpu.VMEM((8 + rt, 3 * GDN_WIDTH), F32)],
        compiler_params=_params("parallel"),
        name="gdn_intra",
    )(proj, proj, x, wa_hi, wa_lo, wb_hi, wb_lo, conv_w, prm)


def _gdn_scan_kernel(u_ref, w_ref, qd_ref, m2_ref, eg_ref, z_ref, ong_ref, o_ref, s_ref):
    c = CHUNK

    @pl.when(pl.program_id(1) == 0)
    def _():
        s_ref[...] = jnp.zeros_like(s_ref)

    ong = ong_ref[...]
    zeros = jnp.zeros((c, HEAD_DIM), BF16)
    heads = range(GDN_HEADS)
    span = lambda h: slice(h * HEAD_DIM, (h + 1) * HEAD_DIM)
    states = [s_ref[h] for h in heads]
    rs = [_dot(jnp.concatenate([w_ref[:, span(h)], qd_ref[:, span(h)]], axis=0), states[h].astype(BF16))
          for h in heads]
    v_new = [(u_ref[:, span(h)] - rs[h][:c]).astype(BF16) for h in heads]
    r2s = []
    for p in range(GDN_HEADS // 2):
        bd = jnp.concatenate([jnp.concatenate([v_new[2 * p], zeros], axis=1),
                              jnp.concatenate([zeros, v_new[2 * p + 1]], axis=1)], axis=0)
        r2s.append(_dot(m2_ref[:, p * LANES:(p + 1) * LANES], bd))
    for h in heads:
        half = span(h % 2)
        r2 = r2s[h // 2]
        s_ref[h] = states[h] * eg_ref[h:h + 1, :] + r2[c:, half]
        o = rs[h][c:] + r2[:c, half]
        o = o * lax.rsqrt(jnp.mean(o * o, axis=-1, keepdims=True) + EPS) * ong
        o = o * _silu(z_ref[:, span(h)])
        o_ref[:, span(h)] = o.astype(o_ref.dtype)


def _gdn_scan(u, w, qd, m2, eg, proj, ong, batch, seq):
    t = batch * seq
    nch = seq // CHUNK
    c = CHUNK
    blk = lambda b, n: (b * nch + n, 0)
    return pl.pallas_call(
        _gdn_scan_kernel,
        grid=(batch, nch),
        in_specs=[
            pl.BlockSpec((c, GDN_WIDTH), blk),
            pl.BlockSpec((c, GDN_WIDTH), blk),
            pl.BlockSpec((c, GDN_WIDTH), blk),
            pl.BlockSpec((3 * c, GDN_HEADS // 2 * LANES), blk),
            pl.BlockSpec((GDN_HEADS, LANES), blk),
            pl.BlockSpec((c, GDN_WIDTH), lambda b, n: (b * nch + n, 3)),
            pl.BlockSpec((1, HEAD_DIM), lambda b, n: (0, 0)),
        ],
        out_specs=pl.BlockSpec((c, GDN_WIDTH), blk),
        out_shape=jax.ShapeDtypeStruct((t, GDN_WIDTH), BF16),
        scratch_shapes=[pltpu.VMEM((GDN_HEADS, HEAD_DIM, HEAD_DIM), F32)],
        compiler_params=_params("parallel", "arbitrary"),
        name="gdn_scan",
    )(u, w, qd, m2, eg, proj, ong)


def _pack_bf16_pairs(y):
    n = y.shape[1] // 2
    yb = y.astype(BF16).astype(F32)
    lo = lax.bitcast_convert_type(yb[:, :n], U32)
    hi = lax.bitcast_convert_type(yb[:, n:], U32)
    return jnp.bitwise_or(jnp.right_shift(lo, 16), jnp.bitwise_and(hi, HI_MASK))


def _unpack_bf16_pairs(w):
    lo = lax.bitcast_convert_type(jnp.left_shift(w, 16), F32)
    hi = lax.bitcast_convert_type(jnp.bitwise_and(w, HI_MASK), F32)
    return lo, hi


def _mixer_kernel(glu_ref, ga_ref, gb_ref, o_ref, x_ref, woa_ref, wob_ref, wo_ref, dww_ref, vec_ref,
                  x1_ref, x1b_ref, x1p_ref, ubuf_ref, sh_ref, conv_ref, *, ts, rc):
    halo = CONF_HALO

    @pl.when(pl.program_id(1) == 0)
    def _():
        ubuf_ref[0:halo, :] = jnp.zeros((halo, CONF_CH), F32)

    @pl.when(pl.program_id(1) != 0)
    def _():
        ubuf_ref[0:halo, :] = ubuf_ref[ts:ts + halo, :]

    ubuf_ref[halo:halo + ts, :] = glu_ref[:, :CONF_CH] * _sigmoid(glu_ref[:, CONF_CH:])

    dw_b = vec_ref[0:1, :]
    cln_g = vec_ref[1:2, :]
    cln_b = vec_ref[2:3, :]
    ln1_g = vec_ref[3:4, :]
    ln1_b = vec_ref[4:5, :]

    span = ts + halo - 8
    for s in range(1, 8):
        sh_ref[s - 1] = ubuf_ref[s:s + span, :]

    def tap_rows(j, r0):
        o = halo - (CONF_KERNEL - 1) + j
        q, s = o // 8, o % 8
        if s == 0:
            return ubuf_ref[r0 + o:r0 + o + rc, :]
        return sh_ref[s - 1, r0 + 8 * q:r0 + 8 * q + rc, :]

    for r0 in range(0, ts, rc):
        acc = dww_ref[0:1, :] * tap_rows(0, r0)
        for j in range(1, CONF_KERNEL):
            acc = acc + dww_ref[j:j + 1, :] * tap_rows(j, r0)
        conv_ref[r0:r0 + rc, :] = acc

    uc = _silu(_layer_norm(conv_ref[...] + dw_b, cln_g, cln_b))
    branch_b = _dot(uc.astype(BF16), wob_ref[...])
    branch_a = _dot(o_ref[...], woa_ref[...])
    hmix = _sigmoid(ga_ref[...]) * branch_a + _sigmoid(gb_ref[...]) * branch_b
    mix = _dot(hmix.astype(BF16), wo_ref[...])
    x1 = _layer_norm(DEEPNORM_ALPHA * x_ref[...] + mix, ln1_g, ln1_b)
    x1_ref[...] = x1
    x1b_ref[...] = x1.astype(BF16)
    x1p_ref[...] = _pack_bf16_pairs(x1)


def _mixer(proj, o_gdn, x, woa, wob, wo, dww, vec, batch, seq, ts):
    t = batch * seq
    nt = seq // ts
    rows = lambda b, n: b * nt + n
    full = lambda b, n: (0, 0)
    kern = functools.partial(_mixer_kernel, ts=ts, rc=32)
    return pl.pallas_call(
        kern,
        grid=(batch, nt),
        in_specs=[
            pl.BlockSpec((ts, 2 * CONF_CH), lambda b, n: (rows(b, n), 2)),
            pl.BlockSpec((ts, D_MODEL), lambda b, n: (rows(b, n), 6)),
            pl.BlockSpec((ts, D_MODEL), lambda b, n: (rows(b, n), 7)),
            pl.BlockSpec((ts, GDN_WIDTH), lambda b, n: (rows(b, n), 0)),
            pl.BlockSpec((ts, D_MODEL), lambda b, n: (rows(b, n), 0)),
            pl.BlockSpec((GDN_WIDTH, D_MODEL), full),
            pl.BlockSpec((CONF_CH, D_MODEL), full),
            pl.BlockSpec((D_MODEL, D_MODEL), full),
            pl.BlockSpec((CONF_HALO, CONF_CH), full),
            pl.BlockSpec((8, D_MODEL), full),
        ],
        out_specs=[pl.BlockSpec((ts, D_MODEL), lambda b, n: (rows(b, n), 0)),
                   pl.BlockSpec((ts, D_MODEL), lambda b, n: (rows(b, n), 0)),
                   pl.BlockSpec((ts, D_MODEL // 2), lambda b, n: (rows(b, n), 0))],
        out_shape=[jax.ShapeDtypeStruct((t, D_MODEL), F32),
                   jax.ShapeDtypeStruct((t, D_MODEL), BF16),
                   jax.ShapeDtypeStruct((t, D_MODEL // 2), U32)],
        scratch_shapes=[pltpu.VMEM((CONF_HALO + ts, CONF_CH), F32),
                        pltpu.VMEM((7, ts + CONF_HALO - 8, CONF_CH), F32),
                        pltpu.VMEM((ts, CONF_CH), F32)],
        compiler_params=_params("parallel", "arbitrary"),
        name="mixer",
    )(proj, proj, proj, o_gdn, x, woa, wob, wo, dww, vec)


def _route_kernel(x_ref, wrh_ref, wrl_ref, bias_ref, eidx_ref, wts_ref, rank_ref, cnt_ref, carry_ref, *, tt):
    @pl.when(pl.program_id(0) == 0)
    def _():
        carry_ref[...] = jnp.zeros_like(carry_ref)

    xh, xl = _split(x_ref[...])
    wrh = wrh_ref[...]
    logits = _dot_nt(wrh, xh) + _dot_nt(wrh, xl) + _dot_nt(wrl_ref[...], xh)
    s = _sigmoid(logits)
    biased = s + bias_ref[...]

    sub = lax.broadcasted_iota(I32, (GROUP_SIZE, tt), 0)
    groups = [biased[g * GROUP_SIZE:(g + 1) * GROUP_SIZE, :] for g in range(N_GROUPS)]
    gs = []
    for bg in groups:
        m1 = jnp.max(bg, axis=0, keepdims=True)
        first = jnp.min(jnp.where(bg == m1, sub, GROUP_SIZE), axis=0, keepdims=True)
        m2 = jnp.max(jnp.where(sub == first, -jnp.inf, bg), axis=0, keepdims=True)
        gs.append(m1 + m2)

    masked_parts = []
    for g in range(N_GROUPS):
        beaten = jnp.zeros((1, tt), I32)
        for o in range(N_GROUPS):
            if o == g:
                continue
            wins = (gs[o] >= gs[g]) if o < g else (gs[o] > gs[g])
            beaten = beaten + wins.astype(I32)
        keep = jnp.broadcast_to(beaten < TOPK_GROUPS, (GROUP_SIZE, tt))
        masked_parts.append(jnp.where(keep, groups[g], -jnp.inf))
    masked = jnp.concatenate(masked_parts, axis=0)

    eiota = lax.broadcasted_iota(I32, (N_EXPERTS, tt), 0)
    sel_all = jnp.zeros((N_EXPERTS, tt), F32)
    picks = []
    for _ in range(TOP_K):
        m = jnp.max(masked, axis=0, keepdims=True)
        idx = jnp.min(jnp.where(masked == m, eiota, N_EXPERTS), axis=0, keepdims=True)
        onehot = eiota == idx
        picks.append((idx, onehot))
        sel_all = jnp.where(onehot, 1.0, sel_all)
        masked = jnp.where(onehot, -jnp.inf, masked)

    tr = lax.broadcasted_iota(I32, (tt, tt), 0)
    tc = lax.broadcasted_iota(I32, (tt, tt), 1)
    before = (tr < tc).astype(BF16)
    sel_b = sel_all.astype(BF16)
    carry = carry_ref[...]
    rank_all = _dot(sel_b, before) + carry[:, 0:1]
    carry_new = carry + _dot(sel_b, jnp.ones((tt, LANES), BF16))
    carry_ref[...] = carry_new
    cnt_ref[...] = carry_new

    s_sel = [jnp.sum(jnp.where(oh, s, 0.0), axis=0, keepdims=True) for _, oh in picks]
    total = s_sel[0]
    for v in s_sel[1:]:
        total = total + v
    for k, (idx, oh) in enumerate(picks):
        eidx_ref[k:k + 1, :] = idx
        wts_ref[k:k + 1, :] = s_sel[k] / total * ROUTED_SCALE
        rank_ref[k:k + 1, :] = jnp.sum(jnp.where(oh, rank_all, 0.0), axis=0, keepdims=True).astype(I32)


def _route(x1, wr_hi, wr_lo, bias, tt):
    t = x1.shape[0]
    kern = functools.partial(_route_kernel, tt=tt)
    return pl.pallas_call(
        kern,
        grid=(t // tt,),
        in_specs=[pl.BlockSpec((tt, D_MODEL), lambda i: (i, 0)),
                  pl.BlockSpec((N_EXPERTS, D_MODEL), lambda i: (0, 0)),
                  pl.BlockSpec((N_EXPERTS, D_MODEL), lambda i: (0, 0)),
                  pl.BlockSpec((N_EXPERTS, tt), lambda i: (0, 0))],
        out_specs=[pl.BlockSpec((TOP_K, tt), lambda i: (0, i)),
                   pl.BlockSpec((TOP_K, tt), lambda i: (0, i)),
                   pl.BlockSpec((TOP_K, tt), lambda i: (0, i)),
                   pl.BlockSpec((N_EXPERTS, LANES), lambda i: (0, 0))],
        out_shape=[jax.ShapeDtypeStruct((TOP_K, t), I32),
                   jax.ShapeDtypeStruct((TOP_K, t), F32),
                   jax.ShapeDtypeStruct((TOP_K, t), I32),
                   jax.ShapeDtypeStruct((N_EXPERTS, LANES), F32)],
        scratch_shapes=[pltpu.VMEM((N_EXPERTS, LANES), F32)],
        compiler_params=_params("arbitrary"),
        name="route",
    )(x1, wr_hi, wr_lo, bias)


def _dest_kernel(eidx_ref, rank_ref, pstart_ref, dest_ref):
    eidx = eidx_ref[...]
    acc = rank_ref[...]
    for e in range(N_EXPERTS):
        acc = acc + jnp.where(eidx == e, pstart_ref[e], 0)
    dest_ref[...] = acc


def _dest(eidx_t, rank_t, pstart, tt):
    t = eidx_t.shape[1]
    return pl.pallas_call(
        _dest_kernel,
        grid=(t // tt,),
        in_specs=[pl.BlockSpec((TOP_K, tt), lambda i: (0, i)),
                  pl.BlockSpec((TOP_K, tt), lambda i: (0, i)),
                  pl.BlockSpec(memory_space=pltpu.SMEM)],
        out_specs=pl.BlockSpec((TOP_K, tt), lambda i: (0, i)),
        out_shape=jax.ShapeDtypeStruct((TOP_K, t), I32),
        compiler_params=_params("parallel"),
        name="dest",
    )(eidx_t, rank_t, pstart)


def _dispatch(dest_kt, x1p, n_rows):
    t, width = x1p.shape
    info = plsc.get_sparse_core_info()
    nc, nw = info.num_cores, info.num_cores * info.num_subcores
    chunk = SC_SCATTER_ROWS
    per_w = t // nw
    n_chunks = per_w // chunk
    assert per_w % chunk == 0
    idx = dest_kt.reshape(TOP_K, nw, n_chunks, chunk).transpose(1, 2, 0, 3).reshape(nw, n_chunks * TOP_K, chunk)
    mesh = plsc.VectorSubcoreMesh(core_axis_name="c", subcore_axis_name="s")

    @functools.partial(
        pl.kernel, mesh=mesh, name="dispatch",
        out_type=jax.ShapeDtypeStruct((n_rows, width), x1p.dtype),
        scratch_types=[pltpu.VMEM((n_chunks * TOP_K, chunk), I32),
                       pltpu.VMEM((chunk, width), x1p.dtype),
                       pltpu.SemaphoreType.DMA])
    def scatter(x_hbm, idx_hbm, xs_hbm, idx_v, rows_v, sem):
        wid = lax.axis_index("s") * nc + lax.axis_index("c")
        base = wid * per_w
        pltpu.sync_copy(idx_hbm.at[wid], idx_v)
        for j in range(n_chunks):
            pltpu.sync_copy(x_hbm.at[pl.ds(base + j * chunk, chunk)], rows_v)
            copies = [pltpu.make_async_copy(rows_v, xs_hbm.at[idx_v.at[j * TOP_K + k]], sem)
                      for k in range(TOP_K)]
            for cp in copies:
                cp.start()
            for cp in copies:
                cp.wait()

    return scatter(x1p, idx)


def _gather_rows(table, idx):
    n = idx.shape[0]
    width = table.shape[1]
    info = plsc.get_sparse_core_info()
    nc, nw = info.num_cores, info.num_cores * info.num_subcores
    chunk = SC_GATHER_ROWS
    per_w = n // nw
    n_chunks = per_w // chunk
    assert per_w % (2 * chunk) == 0
    mesh = plsc.VectorSubcoreMesh(core_axis_name="c", subcore_axis_name="s")

    @functools.partial(
        pl.kernel, mesh=mesh, name="gather_rows",
        out_type=jax.ShapeDtypeStruct((n, width), table.dtype),
        scratch_types=[pltpu.VMEM((n_chunks, chunk), I32),
                       pltpu.VMEM((2, chunk, width), table.dtype),
                       pltpu.SemaphoreType.DMA((2,)),
                       pltpu.SemaphoreType.DMA((2,))])
    def gather(table_hbm, idx_hbm, out_hbm, idx_v, rows_v, gsem, osem):
        wid = lax.axis_index("s") * nc + lax.axis_index("c")
        base = wid * per_w
        pltpu.sync_copy(idx_hbm.at[wid], idx_v)

        def fetch(j, b):
            return pltpu.make_async_copy(table_hbm.at[idx_v.at[j]], rows_v.at[b], gsem.at[b])

        def put(j, b):
            return pltpu.make_async_copy(rows_v.at[b], out_hbm.at[pl.ds(base + j * chunk, chunk)], osem.at[b])

        fetch(0, 0).start()

        @pl.loop(0, n_chunks, step=2)
        def _(j0):
            for b in range(2):
                j = j0 + b
                fetch(j, b).wait()

                @pl.when(j + 1 < n_chunks)
                def _():
                    @pl.when(j >= 1)
                    def _():
                        put(j - 1, 1 - b).wait()

                    fetch(j + 1, 1 - b).start()

                put(j, b).start()

        put(n_chunks - 2, 0).wait()
        put(n_chunks - 1, 1).wait()

    return gather(table, idx.reshape(nw, n_chunks, chunk))


def _experts_kernel(be_ref, nv_ref, xs_ref, wg_ref, wu_ref, wd_ref, ys_ref):
    del be_ref
    n_valid = nv_ref[pl.program_id(0)]

    @pl.when(n_valid > 0)
    def _():
        valid = lax.broadcasted_iota(I32, xs_ref.shape, 0) < n_valid
        lo, hi = _unpack_bf16_pairs(jnp.where(valid, xs_ref[...], jnp.zeros((), U32)))
        xb = jnp.concatenate([lo.astype(BF16), hi.astype(BF16)], axis=1)
        gate = _dot(xb, wg_ref[...])
        up = _dot(xb, wu_ref[...])
        hid = (_silu(gate) * up).astype(BF16)
        ys_ref[...] = _pack_bf16_pairs(_dot(hid, wd_ref[...]))

    @pl.when(n_valid <= 0)
    def _():
        ys_ref[...] = jnp.zeros_like(ys_ref)


def _experts(block_e, n_valid, xs, wg, wu, wd):
    n_rows, width = xs.shape
    nb = n_rows // ROW_BLOCK
    grid_spec = pltpu.PrefetchScalarGridSpec(
        num_scalar_prefetch=2,
        grid=(nb,),
        in_specs=[pl.BlockSpec((ROW_BLOCK, width), lambda i, be, nv: (i, 0)),
                  pl.BlockSpec((None, D_MODEL, EXPERT_FF), lambda i, be, nv: (be[i], 0, 0)),
                  pl.BlockSpec((None, D_MODEL, EXPERT_FF), lambda i, be, nv: (be[i], 0, 0)),
                  pl.BlockSpec((None, EXPERT_FF, D_MODEL), lambda i, be, nv: (be[i], 0, 0))],
        out_specs=pl.BlockSpec((ROW_BLOCK, width), lambda i, be, nv: (i, 0)),
    )
    return pl.pallas_call(
        _experts_kernel,
        grid_spec=grid_spec,
        out_shape=jax.ShapeDtypeStruct((n_rows, width), xs.dtype),
        compiler_params=_params("arbitrary"),
        name="experts",
    )(block_e, n_valid, xs, wg, wu, wd)


def _combine_kernel(yg_ref, w_ref, x1_ref, x1b_ref, wsg_ref, wsu_ref, wsd_ref, vec_ref, x2_ref, x2b_ref):
    xb = x1b_ref[...]
    hid = (_silu(_dot(xb, wsg_ref[...])) * _dot(xb, wsu_ref[...])).astype(BF16)
    shared = _dot(hid, wsd_ref[...])

    w = w_ref[...]
    half = D_MODEL // 2
    acc_lo, acc_hi = shared[:, :half], shared[:, half:]
    for k in range(TOP_K):
        lo, hi = _unpack_bf16_pairs(yg_ref[k])
        acc_lo = acc_lo + w[:, k:k + 1] * lo
        acc_hi = acc_hi + w[:, k:k + 1] * hi
    acc = jnp.concatenate([acc_lo, acc_hi], axis=1)
    x2 = _layer_norm(DEEPNORM_ALPHA * x1_ref[...] + acc, vec_ref[0:1, :], vec_ref[1:2, :])
    x2_ref[...] = x2
    x2b_ref[...] = x2.astype(BF16)


def _combine(yg, w_tok, x1, x1b, wsg, wsu, wsd, vec, tt):
    t = x1.shape[0]
    full = lambda i: (0, 0)
    return pl.pallas_call(
        _combine_kernel,
        grid=(t // tt,),
        in_specs=[pl.BlockSpec((TOP_K, tt, yg.shape[2]), lambda i: (0, i, 0)),
                  pl.BlockSpec((tt, TOP_K), lambda i: (i, 0)),
                  pl.BlockSpec((tt, D_MODEL), lambda i: (i, 0)),
                  pl.BlockSpec((tt, D_MODEL), lambda i: (i, 0)),
                  pl.BlockSpec((D_MODEL, SHARED_FF), full),
                  pl.BlockSpec((D_MODEL, SHARED_FF), full),
                  pl.BlockSpec((SHARED_FF, D_MODEL), full),
                  pl.BlockSpec((8, D_MODEL), full)],
        out_specs=[pl.BlockSpec((tt, D_MODEL), lambda i: (i, 0)),
                   pl.BlockSpec((tt, D_MODEL), lambda i: (i, 0))],
        out_shape=[jax.ShapeDtypeStruct((t, D_MODEL), F32),
                   jax.ShapeDtypeStruct((t, D_MODEL), BF16)],
        compiler_params=_params("parallel"),
        name="combine",
    )(yg, w_tok, x1, x1b, wsg, wsu, wsd, vec)


def _pad_rows(a, rows):
    return jnp.zeros((rows, a.shape[-1]), F32).at[:a.shape[0]].set(a.astype(F32))


def _layer(x, xb, p, batch, seq):
    t = batch * seq
    gw = GDN_WIDTH
    w_in = p["w_in"]
    c_z, c_a, c_glu = 3 * gw, 4 * gw, 4 * gw + 2 * GDN_HEADS
    c_ga = c_glu + 2 * CONF_CH
    c_gb = c_ga + D_MODEL
    w_main = jnp.concatenate([w_in[:, :c_a], w_in[:, c_glu:]], axis=1).astype(BF16)
    w_a = jnp.zeros((D_MODEL, LANES), F32).at[:, :GDN_HEADS].set(w_in[:, c_a:c_a + GDN_HEADS])
    w_b = jnp.zeros((D_MODEL, LANES), F32).at[:, :GDN_HEADS].set(w_in[:, c_a + GDN_HEADS:c_glu])
    wa_hi, wb_hi = w_a.astype(BF16), w_b.astype(BF16)
    wa_lo = (w_a - wa_hi.astype(F32)).astype(BF16)
    wb_lo = (w_b - wb_hi.astype(F32)).astype(BF16)
    del c_z, c_gb

    tm = min(1024, t)
    proj = _proj(xb, w_main, tm, 1024)

    prm = jnp.zeros((8, LANES), F32)
    prm = prm.at[0, :GDN_HEADS].set(p["a_log"]).at[1, :GDN_HEADS].set(p["dt_bias"])
    ong = p["o_norm_g"].reshape(1, HEAD_DIM).astype(F32)
    u, w, qd, m2, eg = _gdn_intra(proj, x, wa_hi, wa_lo, wb_hi, wb_lo, p["conv_qkv"].astype(F32), prm,
                                  seq, min(GDN_TILE, seq))
    o_gdn = _gdn_scan(u, w, qd, m2, eg, proj, ong, batch, seq)

    ts = min(256, seq)
    dww = _pad_rows(p["dw_w"], CONF_HALO)
    vec = _pad_rows(jnp.stack([p["dw_b"], p["cln_g"], p["cln_b"], p["ln1_g"], p["ln1_b"]]), 8)
    x1, x1b, x1p = _mixer(proj, o_gdn, x, p["w_oa"].astype(BF16), p["w_ob"].astype(BF16),
                          p["w_o"].astype(BF16), dww, vec, batch, seq, ts)

    tt_r = min(256, t)
    wr_t = p["w_router"].T.astype(F32)
    wr_hi = wr_t.astype(BF16)
    wr_lo = (wr_t - wr_hi.astype(F32)).astype(BF16)
    bias = jnp.broadcast_to(p["router_bias"].astype(F32)[:, None], (N_EXPERTS, tt_r))
    eidx_t, wts_t, rank_t, cnt = _route(x1, wr_hi, wr_lo, bias, tt_r)

    counts = cnt[:, 0].astype(I32)
    padded = (counts + ROW_BLOCK - 1) // ROW_BLOCK * ROW_BLOCK
    pend = jnp.cumsum(padded)
    pstart = (pend - padded).astype(I32)
    n_blocks = -(-(t * TOP_K + N_EXPERTS * (ROW_BLOCK - 1)) // ROW_BLOCK)
    n_rows = n_blocks * ROW_BLOCK
    block_start = jnp.arange(n_blocks, dtype=I32) * ROW_BLOCK
    owner = jnp.sum((pend[None, :] <= block_start[:, None]).astype(I32), axis=1)
    block_e = jnp.minimum(owner, N_EXPERTS - 1).astype(I32)
    used_end = (pstart + counts)[block_e]
    n_valid = jnp.where(owner < N_EXPERTS, jnp.clip(used_end - block_start, 0, ROW_BLOCK), 0).astype(I32)

    dest = _dest(eidx_t, rank_t, pstart, min(2048, t))
    xs = _dispatch(dest, x1p, n_rows)
    ys = _experts(block_e, n_valid, xs, p["w_gate_e"].astype(BF16), p["w_up_e"].astype(BF16),
                  p["w_down_e"].astype(BF16))
    yg = _gather_rows(ys, dest.reshape(TOP_K * t)).reshape(TOP_K, t, ys.shape[1])
    vec2 = _pad_rows(jnp.stack([p["ln2_g"], p["ln2_b"]]), 8)
    x2, x2b = _combine(yg, wts_t.T, x1, x1b, p["w_sh_gate"].astype(BF16),
                       p["w_sh_up"].astype(BF16), p["w_sh_down"].astype(BF16), vec2, min(256, t))
    return x2, x2b


_PARAM_NAMES = ("w_in", "conv_qkv", "a_log", "dt_bias", "o_norm_g", "w_oa", "dw_w", "dw_b", "cln_g",
                "cln_b", "w_ob", "w_o", "ln1_g", "ln1_b", "w_router", "router_bias", "w_gate_e",
                "w_up_e", "w_down_e", "w_sh_gate", "w_sh_up", "w_sh_down", "ln2_g", "ln2_b")


def kernel(x, w_in, conv_qkv, a_log, dt_bias, o_norm_g, w_oa, dw_w, dw_b, cln_g, cln_b, w_ob, w_o,
           ln1_g, ln1_b, w_router, router_bias, w_gate_e, w_up_e, w_down_e, w_sh_gate, w_sh_up,
           w_sh_down, ln2_g, ln2_b):
    stacked = dict(zip(_PARAM_NAMES, (w_in, conv_qkv, a_log, dt_bias, o_norm_g, w_oa, dw_w, dw_b, cln_g,
                                      cln_b, w_ob, w_o, ln1_g, ln1_b, w_router, router_bias, w_gate_e,
                                      w_up_e, w_down_e, w_sh_gate, w_sh_up, w_sh_down, ln2_g, ln2_b)))
    batch, seq, d = x.shape
    assert d == D_MODEL and seq % CHUNK == 0
    xf = x.reshape(batch * seq, d).astype(F32)
    xb = xf.astype(BF16)
    for layer in range(w_in.shape[0]):
        p = {name: arr[layer] for name, arr in stacked.items()}
        xf, xb = _layer(xf, xb, p, batch, seq)
    return xf.reshape(batch, seq, d).astype(x.dtype)
```

```python
import functools

import jax
import jax.numpy as jnp
import numpy as np
from jax import lax
from jax.experimental import pallas as pl
from jax.experimental.pallas import tpu as pltpu
from jax.experimental.pallas import tpu_sc as plsc

F32 = jnp.float32
BF16 = jnp.bfloat16
I32 = jnp.int32
U32 = jnp.uint32
HI_MASK = np.uint32(0xFFFF0000)

D_MODEL = 1024
GDN_HEADS = 8
HEAD_DIM = 128
GDN_WIDTH = GDN_HEADS * HEAD_DIM
SHORT_CONV = 4
CHUNK = 64
SOLVE_BLOCK = 16
GDN_TILE = 256
SCAN_CHUNKS = 4
CONF_CH = D_MODEL
CONF_KERNEL = 31
CONF_HALO = 32
N_EXPERTS = 64
TOP_K = 8
N_GROUPS = 8
GROUP_SIZE = N_EXPERTS // N_GROUPS
TOPK_GROUPS = 4
EXPERT_FF = 256
SHARED_FF = 256
ROUTED_SCALE = 2.5
DEPTH = 2
DEEPNORM_ALPHA = (2 * DEPTH) ** 0.25
EPS = 1e-6

LANES = 128
PROJ_COLS = 8 * D_MODEL
ROW_BLOCK = 512
SC_SCATTER_ROWS = 128
SC_GATHER_ROWS = 64
VMEM_LIMIT = 56 * 1024 * 1024


def _params(*sem):
    return pltpu.CompilerParams(dimension_semantics=sem, vmem_limit_bytes=VMEM_LIMIT)


def _dot(a, b):
    return jnp.dot(a, b, preferred_element_type=F32)


def _dot_nt(a, b):
    return lax.dot_general(a, b, (((1,), (1,)), ((), ())), preferred_element_type=F32)


def _split(a):
    hi = a.astype(BF16)
    lo = (a - hi.astype(F32)).astype(BF16)
    return hi, lo


def _dot3(a, b):
    ah, al = _split(a)
    bh, bl = _split(b)
    return _dot(ah, bh) + _dot(al, bh) + _dot(ah, bl)


def _sigmoid(x):
    return 1.0 / (1.0 + jnp.exp(-x))


def _silu(x):
    return x * _sigmoid(x)


def _layer_norm(y, g, b):
    mu = jnp.mean(y, axis=-1, keepdims=True)
    yc = y - mu
    var = jnp.mean(yc * yc, axis=-1, keepdims=True)
    return yc * lax.rsqrt(var + EPS) * g + b


def _proj_kernel(a_ref, w_ref, o_ref):
    o_ref[...] = _dot(a_ref[...], w_ref[...])


def _proj(xb, w, tm, tn):
    m, k = xb.shape
    n = w.shape[1]
    return pl.pallas_call(
        _proj_kernel,
        grid=(m // tm, n // tn),
        in_specs=[pl.BlockSpec((tm, k), lambda i, j: (i, 0)),
                  pl.BlockSpec((k, tn), lambda i, j: (0, j))],
        out_specs=pl.BlockSpec((tm, tn), lambda i, j: (i, j)),
        out_shape=jax.ShapeDtypeStruct((m, n), F32),
        compiler_params=_params("parallel", "parallel"),
        name="proj",
    )(xb, w)


def _unit_lower_inverse4(als):
    c = CHUNK
    n = als[0].shape[1]
    row = lax.broadcasted_iota(I32, (c, n), 0)
    col = jnp.bitwise_and(lax.broadcasted_iota(I32, (c, n), 1), c - 1)
    shift = SOLVE_BLOCK.bit_length() - 1
    same = jnp.right_shift(row, shift) == jnp.right_shift(col, shift)
    eye = (row == col).astype(F32)
    cshift = c.bit_length() - 1
    brow = jnp.right_shift(lax.broadcasted_iota(I32, (n, n), 0), cshift)
    bcol = jnp.right_shift(lax.broadcasted_iota(I32, (n, n), 1), cshift)
    on_diag = brow == bcol

    def mm(x, y):
        yb = y.astype(BF16)
        bd = jnp.where(on_diag, jnp.concatenate([yb] * (n // c), axis=0), jnp.zeros((), BF16))
        return _dot(x.astype(BF16), bd)

    a_diag = [jnp.where(same, al, 0.0) for al in als]
    a_off = [al - ad for al, ad in zip(als, a_diag)]
    bp = [-ad for ad in a_diag]
    p = [eye + b for b in bp]
    for _ in range(3):
        bp = [mm(b, b) for b in bp]
        p = [x + mm(x, b) for x, b in zip(p, bp)]
    n1 = [mm(x, ao) for x, ao in zip(p, a_off)]
    n2 = [mm(x, x) for x in n1]
    q = [x + mm(y, x) for x, y in zip(p, n2)]
    return [x - mm(y, x) for x, y in zip(q, n1)]


def _gdn_intra_kernel(qkv_ref, prev_ref, x_ref, wah_ref, wal_ref, wbh_ref, wbl_ref, cw_ref, prm_ref,
                      u_ref, w_ref, qd_ref, m2_ref, eg_ref, xe_ref, *, rt, tiles_per_seq):
    c = CHUNK
    nc = rt // c
    first = (pl.program_id(0) % tiles_per_seq) == 0
    xe_ref[0:8, :] = jnp.where(first, 0.0, prev_ref[...])
    xe_ref[8:8 + rt, :] = qkv_ref[...]

    xh, xl = _split(x_ref[...])

    def proj3(wh_ref, wl_ref):
        wh = wh_ref[...]
        return _dot(xh, wh) + _dot(xl, wh) + _dot(xh, wl_ref[...])

    a_raw = proj3(wah_ref, wal_ref)
    b_raw = proj3(wbh_ref, wbl_ref)
    sp_in = a_raw + prm_ref[1:2, :]
    softplus = jnp.maximum(sp_in, 0.0) + jnp.log1p(jnp.exp(-jnp.abs(sp_in)))
    g = -jnp.exp(prm_ref[0:1, :]) * softplus
    beta = _sigmoid(b_raw)

    cshift = c.bit_length() - 1
    r2 = lax.broadcasted_iota(I32, (rt, rt), 0)
    c2 = lax.broadcasted_iota(I32, (rt, rt), 1)
    same_chunk = jnp.right_shift(r2, cshift) == jnp.right_shift(c2, cshift)
    ltri = jnp.where(r2 >= c2, jnp.where(same_chunk, 1.0, 0.0), 0.0).astype(BF16)
    g_hi = g.astype(BF16)
    g_r = g - g_hi.astype(F32)
    g_mid = g_r.astype(BF16)
    g_lo = (g_r - g_mid.astype(F32)).astype(BF16)
    gc = _dot(ltri, g_hi) + _dot(ltri, g_mid) + _dot(ltri, g_lo)
    gct = gc.T
    egc = jnp.exp(gc)
    gend = jnp.concatenate(
        [jnp.broadcast_to(gc[ci * c + c - 1:ci * c + c, :], (c, LANES)) for ci in range(nc)], axis=0)
    kfac = jnp.exp(gend - gc)
    bege = beta * egc
    for ci in range(nc):
        last = ci * c + c - 1
        eg_ref[ci * GDN_HEADS:(ci + 1) * GDN_HEADS, :] = jnp.broadcast_to(
            jnp.exp(gct[0:GDN_HEADS, last:last + 1]), (GDN_HEADS, LANES))

    lane_t = lax.broadcasted_iota(I32, (rt, LANES), 1) < c
    lane_lo = lax.broadcasted_iota(I32, (c, LANES), 1) < c
    lcol = jnp.bitwise_and(lax.broadcasted_iota(I32, (c, LANES), 1), c - 1)
    rowi = lax.broadcasted_iota(I32, (c, LANES), 0)
    causal = rowi >= lcol
    strict = rowi > lcol

    def conv(base, h):
        lo, hi = base + h * HEAD_DIM, base + (h + 1) * HEAD_DIM
        acc = cw_ref[SHORT_CONV - 1:SHORT_CONV, lo:hi] * xe_ref[8:8 + rt, lo:hi]
        for j in range(SHORT_CONV - 1):
            s0 = 8 - (SHORT_CONV - 1) + j
            acc = acc + cw_ref[j:j + 1, lo:hi] * xe_ref[s0:s0 + rt, lo:hi]
        return _silu(acc)

    a_pairs = [[None] * (GDN_HEADS // 2) for _ in range(nc)]
    rhs_pairs = [[None] * (GDN_HEADS // 2) for _ in range(nc)]
    for p in range(GDN_HEADS // 2):
        ks, kbs, qs, kds, rhss = [], [], [], [], []
        for h in (2 * p, 2 * p + 1):
            q = conv(0, h)
            k = conv(GDN_WIDTH, h)
            v = conv(2 * GDN_WIDTH, h)
            q = q * lax.rsqrt(jnp.sum(q * q, axis=-1, keepdims=True) + EPS) * (HEAD_DIM ** -0.5)
            k = k * lax.rsqrt(jnp.sum(k * k, axis=-1, keepdims=True) + EPS)
            beta_h = beta[:, h:h + 1]
            qd_ref[:, h * HEAD_DIM:(h + 1) * HEAD_DIM] = (q * egc[:, h:h + 1]).astype(BF16)
            ks.append(k)
            kbs.append(k * beta_h)
            qs.append(q)
            kds.append(k * kfac[:, h:h + 1])
            rhss.append(jnp.concatenate([v * beta_h, k * bege[:, h:h + 1]], axis=1))
        h0, h1 = 2 * p, 2 * p + 1
        gch = jnp.where(lane_t, gc[:, h0:h0 + 1], gc[:, h1:h1 + 1])
        for ci in range(nc):
            rows = slice(ci * c, (ci + 1) * c)
            wk = jnp.concatenate([ks[0][rows], ks[1][rows]], axis=0).astype(BF16)
            lhs = jnp.concatenate([kbs[0][rows], qs[0][rows], kbs[1][rows], qs[1][rows]],
                                  axis=0).astype(BF16)
            out = _dot_nt(lhs, wk)
            gcrow = jnp.concatenate([gct[h0:h0 + 1, rows], gct[h1:h1 + 1, rows]], axis=1)
            diff = gch[rows] - gcrow
            decay = jnp.where(causal, jnp.exp(jnp.where(causal, diff, 0.0)), 0.0)
            a_pairs[ci][p] = jnp.where(strict, jnp.where(lane_lo, out[0:c], out[2 * c:3 * c]) * decay, 0.0)
            qk = jnp.where(lane_lo, out[c:2 * c], out[3 * c:4 * c]) * decay
            kdt = jnp.concatenate([kds[0][rows], kds[1][rows]], axis=0).T
            m0 = ci * 3 * c
            m2_ref[m0:m0 + c, p * LANES:(p + 1) * LANES] = qk.astype(BF16)
            m2_ref[m0 + c:m0 + 3 * c, p * LANES:(p + 1) * LANES] = kdt.astype(BF16)
            rhs_pairs[ci][p] = (rhss[0][rows], rhss[1][rows])

    zeros = jnp.zeros((c, 2 * HEAD_DIM), BF16)
    ngrp = GDN_HEADS // 4
    tls = _unit_lower_inverse4(
        [jnp.concatenate([a_pairs[ci][2 * grp], a_pairs[ci][2 * grp + 1]], axis=1)
         for ci in range(nc) for grp in range(ngrp)])
    for ci in range(nc):
        rows = slice(ci * c, (ci + 1) * c)
        for grp in range(ngrp):
            tl = tls[ci * ngrp + grp]
            for j in range(2):
                p = 2 * grp + j
                r0, r1 = rhs_pairs[ci][p]
                bd = jnp.concatenate([jnp.concatenate([r0.astype(BF16), zeros], axis=1),
                                      jnp.concatenate([zeros, r1.astype(BF16)], axis=1)], axis=0)
                sol = _dot(tl[:, j * LANES:(j + 1) * LANES].astype(BF16), bd)
                for i in range(2):
                    h = 2 * p + i
                    lo, hi = h * HEAD_DIM, (h + 1) * HEAD_DIM
                    u_ref[rows, lo:hi] = sol[:, 2 * i * HEAD_DIM:(2 * i + 1) * HEAD_DIM]
                    w_ref[rows, lo:hi] = sol[:, (2 * i + 1) * HEAD_DIM:(2 * i + 2) * HEAD_DIM].astype(BF16)


def _gdn_intra(proj, x, wa_hi, wa_lo, wb_hi, wb_lo, conv_w, prm, seq, rt):
    t = x.shape[0]
    nc = rt // CHUNK
    kern = functools.partial(_gdn_intra_kernel, rt=rt, tiles_per_seq=seq // rt)
    full = lambda i: (0, 0)
    tile = lambda i: (i, 0)
    return pl.pallas_call(
        kern,
        grid=(t // rt,),
        in_specs=[
            pl.BlockSpec((rt, 3 * GDN_WIDTH), tile),
            pl.BlockSpec((8, 3 * GDN_WIDTH), lambda i: (jnp.maximum(i * (rt // 8) - 1, 0), 0)),
            pl.BlockSpec((rt, D_MODEL), tile),
            pl.BlockSpec((D_MODEL, LANES), full),
            pl.BlockSpec((D_MODEL, LANES), full),
            pl.BlockSpec((D_MODEL, LANES), full),
            pl.BlockSpec((D_MODEL, LANES), full),
            pl.BlockSpec((SHORT_CONV, 3 * GDN_WIDTH), full),
            pl.BlockSpec((8, LANES), full),
        ],
        out_specs=[pl.BlockSpec((rt, GDN_WIDTH), tile),
                   pl.BlockSpec((rt, GDN_WIDTH), tile),
                   pl.BlockSpec((rt, GDN_WIDTH), tile),
                   pl.BlockSpec((nc * 3 * CHUNK, GDN_HEADS // 2 * LANES), tile),
                   pl.BlockSpec((nc * GDN_HEADS, LANES), tile)],
        out_shape=[jax.ShapeDtypeStruct((t, GDN_WIDTH), F32),
                   jax.ShapeDtypeStruct((t, GDN_WIDTH), BF16),
                   jax.ShapeDtypeStruct((t, GDN_WIDTH), BF16),
                   jax.ShapeDtypeStruct((t // CHUNK * 3 * CHUNK, GDN_HEADS // 2 * LANES), BF16),
                   jax.ShapeDtypeStruct((t // CHUNK * GDN_HEADS, LANES), F32)],
        scratch_shapes=[pltpu.VMEM((8 + rt, 3 * GDN_WIDTH), F32)],
        compiler_params=_params("parallel"),
        name="gdn_intra",
    )(proj, proj, x, wa_hi, wa_lo, wb_hi, wb_lo, conv_w, prm)


def _gdn_scan_kernel(u_ref, w_ref, qd_ref, m2_ref, eg_ref, z_ref, ong_ref, o_ref, s_ref, *, nck):
    c = CHUNK

    @pl.when(pl.program_id(1) == 0)
    def _():
        s_ref[...] = jnp.zeros_like(s_ref)

    ong = ong_ref[...]
    zeros = jnp.zeros((c, HEAD_DIM), BF16)
    heads = range(GDN_HEADS)
    span = lambda h: slice(h * HEAD_DIM, (h + 1) * HEAD_DIM)
    states = [s_ref[h] for h in heads]
    for ci in range(nck):
        rows = slice(ci * c, (ci + 1) * c)
        rs = [_dot(jnp.concatenate([w_ref[rows, span(h)], qd_ref[rows, span(h)]], axis=0),
                   states[h].astype(BF16)) for h in heads]
        v_new = [(u_ref[rows, span(h)] - rs[h][:c]).astype(BF16) for h in heads]
        r2s = []
        for p in range(GDN_HEADS // 2):
            bd = jnp.concatenate([jnp.concatenate([v_new[2 * p], zeros], axis=1),
                                  jnp.concatenate([zeros, v_new[2 * p + 1]], axis=1)], axis=0)
            r2s.append(_dot(m2_ref[ci * 3 * c:(ci + 1) * 3 * c, p * LANES:(p + 1) * LANES], bd))
        for h in heads:
            half = span(h % 2)
            r2 = r2s[h // 2]
            states[h] = states[h] * eg_ref[ci * GDN_HEADS + h:ci * GDN_HEADS + h + 1, :] + r2[c:, half]
            o = rs[h][c:] + r2[:c, half]
            o = o * lax.rsqrt(jnp.mean(o * o, axis=-1, keepdims=True) + EPS) * ong
            o = o * _silu(z_ref[rows, span(h)])
            o_ref[rows, span(h)] = o.astype(o_ref.dtype)
    for h in heads:
        s_ref[h] = states[h]


def _gdn_scan(u, w, qd, m2, eg, proj, ong, batch, seq):
    t = batch * seq
    nck = min(SCAN_CHUNKS, seq // CHUNK)
    c = nck * CHUNK
    nch = seq // c
    blk = lambda b, n: (b * nch + n, 0)
    return pl.pallas_call(
        functools.partial(_gdn_scan_kernel, nck=nck),
        grid=(batch, nch),
        in_specs=[
            pl.BlockSpec((c, GDN_WIDTH), blk),
            pl.BlockSpec((c, GDN_WIDTH), blk),
            pl.BlockSpec((c, GDN_WIDTH), blk),
            pl.BlockSpec((3 * c, GDN_HEADS // 2 * LANES), blk),
            pl.BlockSpec((nck * GDN_HEADS, LANES), blk),
            pl.BlockSpec((c, GDN_WIDTH), lambda b, n: (b * nch + n, 3)),
            pl.BlockSpec((1, HEAD_DIM), lambda b, n: (0, 0)),
        ],
        out_specs=pl.BlockSpec((c, GDN_WIDTH), blk),
        out_shape=jax.ShapeDtypeStruct((t, GDN_WIDTH), BF16),
        scratch_shapes=[pltpu.VMEM((GDN_HEADS, HEAD_DIM, HEAD_DIM), F32)],
        compiler_params=_params("parallel", "arbitrary"),
        name="gdn_scan",
    )(u, w, qd, m2, eg, proj, ong)


def _pack_bf16_pairs(y):
    n = y.shape[1] // 2
    yb = y.astype(BF16).astype(F32)
    lo = lax.bitcast_convert_type(yb[:, :n], U32)
    hi = lax.bitcast_convert_type(yb[:, n:], U32)
    return jnp.bitwise_or(jnp.right_shift(lo, 16), jnp.bitwise_and(hi, HI_MASK))


def _unpack_bf16_pairs(w):
    lo = lax.bitcast_convert_type(jnp.left_shift(w, 16), F32)
    hi = lax.bitcast_convert_type(jnp.bitwise_and(w, HI_MASK), F32)
    return lo, hi


def _mixer_kernel(glu_ref, ga_ref, gb_ref, o_ref, x_ref, woa_ref, wob_ref, wo_ref, dww_ref, vec_ref,
                  x1_ref, x1b_ref, x1p_ref, ubuf_ref, sh_ref, conv_ref, *, ts, rc):
    halo = CONF_HALO

    @pl.when(pl.program_id(1) == 0)
    def _():
        ubuf_ref[0:halo, :] = jnp.zeros((halo, CONF_CH), F32)

    @pl.when(pl.program_id(1) != 0)
    def _():
        ubuf_ref[0:halo, :] = ubuf_ref[ts:ts + halo, :]

    ubuf_ref[halo:halo + ts, :] = glu_ref[:, :CONF_CH] * _sigmoid(glu_ref[:, CONF_CH:])

    dw_b = vec_ref[0:1, :]
    cln_g = vec_ref[1:2, :]
    cln_b = vec_ref[2:3, :]
    ln1_g = vec_ref[3:4, :]
    ln1_b = vec_ref[4:5, :]

    span = ts + halo - 8
    for s in range(1, 8):
        sh_ref[s - 1] = ubuf_ref[s:s + span, :]

    def tap_rows(j, r0):
        o = halo - (CONF_KERNEL - 1) + j
        q, s = o // 8, o % 8
        if s == 0:
            return ubuf_ref[r0 + o:r0 + o + rc, :]
        return sh_ref[s - 1, r0 + 8 * q:r0 + 8 * q + rc, :]

    for r0 in range(0, ts, rc):
        acc = dww_ref[0:1, :] * tap_rows(0, r0)
        for j in range(1, CONF_KERNEL):
            acc = acc + dww_ref[j:j + 1, :] * tap_rows(j, r0)
        conv_ref[r0:r0 + rc, :] = acc

    uc = _silu(_layer_norm(conv_ref[...] + dw_b, cln_g, cln_b))
    branch_b = _dot(uc.astype(BF16), wob_ref[...])
    branch_a = _dot(o_ref[...], woa_ref[...])
    hmix = _sigmoid(ga_ref[...]) * branch_a + _sigmoid(gb_ref[...]) * branch_b
    mix = _dot(hmix.astype(BF16), wo_ref[...])
    x1 = _layer_norm(DEEPNORM_ALPHA * x_ref[...] + mix, ln1_g, ln1_b)
    x1_ref[...] = x1
    x1b_ref[...] = x1.astype(BF16)
    x1p_ref[...] = _pack_bf16_pairs(x1)


def _mixer(proj, o_gdn, x, woa, wob, wo, dww, vec, batch, seq, ts):
    t = batch * seq
    nt = seq // ts
    rows = lambda b, n: b * nt + n
    full = lambda b, n: (0, 0)
    kern = functools.partial(_mixer_kernel, ts=ts, rc=32)
    return pl.pallas_call(
        kern,
        grid=(batch, nt),
        in_specs=[
            pl.BlockSpec((ts, 2 * CONF_CH), lambda b, n: (rows(b, n), 2)),
            pl.BlockSpec((ts, D_MODEL), lambda b, n: (rows(b, n), 6)),
            pl.BlockSpec((ts, D_MODEL), lambda b, n: (rows(b, n), 7)),
            pl.BlockSpec((ts, GDN_WIDTH), lambda b, n: (rows(b, n), 0)),
            pl.BlockSpec((ts, D_MODEL), lambda b, n: (rows(b, n), 0)),
            pl.BlockSpec((GDN_WIDTH, D_MODEL), full),
            pl.BlockSpec((CONF_CH, D_MODEL), full),
            pl.BlockSpec((D_MODEL, D_MODEL), full),
            pl.BlockSpec((CONF_HALO, CONF_CH), full),
            pl.BlockSpec((8, D_MODEL), full),
        ],
        out_specs=[pl.BlockSpec((ts, D_MODEL), lambda b, n: (rows(b, n), 0)),
                   pl.BlockSpec((ts, D_MODEL), lambda b, n: (rows(b, n), 0)),
                   pl.BlockSpec((ts, D_MODEL // 2), lambda b, n: (rows(b, n), 0))],
        out_shape=[jax.ShapeDtypeStruct((t, D_MODEL), F32),
                   jax.ShapeDtypeStruct((t, D_MODEL), BF16),
                   jax.ShapeDtypeStruct((t, D_MODEL // 2), U32)],
        scratch_shapes=[pltpu.VMEM((CONF_HALO + ts, CONF_CH), F32),
                        pltpu.VMEM((7, ts + CONF_HALO - 8, CONF_CH), F32),
                        pltpu.VMEM((ts, CONF_CH), F32)],
        compiler_params=_params("parallel", "arbitrary"),
        name="mixer",
    )(proj, proj, proj, o_gdn, x, woa, wob, wo, dww, vec)


def _route_kernel(x_ref, wrh_ref, wrl_ref, bias_ref, eidx_ref, wts_ref, rank_ref, cnt_ref, carry_ref, *, tt):
    @pl.when(pl.program_id(0) == 0)
    def _():
        carry_ref[...] = jnp.zeros_like(carry_ref)

    xh, xl = _split(x_ref[...])
    wrh = wrh_ref[...]
    logits = _dot_nt(wrh, xh) + _dot_nt(wrh, xl) + _dot_nt(wrl_ref[...], xh)
    s = _sigmoid(logits)
    biased = s + bias_ref[...]

    sub = lax.broadcasted_iota(I32, (GROUP_SIZE, tt), 0)
    groups = [biased[g * GROUP_SIZE:(g + 1) * GROUP_SIZE, :] for g in range(N_GROUPS)]
    gs = []
    for bg in groups:
        m1 = jnp.max(bg, axis=0, keepdims=True)
        first = jnp.min(jnp.where(bg == m1, sub, GROUP_SIZE), axis=0, keepdims=True)
        m2 = jnp.max(jnp.where(sub == first, -jnp.inf, bg), axis=0, keepdims=True)
        gs.append(m1 + m2)

    masked_parts = []
    for g in range(N_GROUPS):
        beaten = jnp.zeros((1, tt), I32)
        for o in range(N_GROUPS):
            if o == g:
                continue
            wins = (gs[o] >= gs[g]) if o < g else (gs[o] > gs[g])
            beaten = beaten + wins.astype(I32)
        keep = jnp.broadcast_to(beaten < TOPK_GROUPS, (GROUP_SIZE, tt))
        masked_parts.append(jnp.where(keep, groups[g], -jnp.inf))
    masked = jnp.concatenate(masked_parts, axis=0)

    eiota = lax.broadcasted_iota(I32, (N_EXPERTS, tt), 0)
    sel_all = jnp.zeros((N_EXPERTS, tt), F32)
    picks = []
    for _ in range(TOP_K):
        m = jnp.max(masked, axis=0, keepdims=True)
        idx = jnp.min(jnp.where(masked == m, eiota, N_EXPERTS), axis=0, keepdims=True)
        onehot = eiota == idx
        picks.append((idx, onehot))
        sel_all = jnp.where(onehot, 1.0, sel_all)
        masked = jnp.where(onehot, -jnp.inf, masked)

    tr = lax.broadcasted_iota(I32, (tt, tt), 0)
    tc = lax.broadcasted_iota(I32, (tt, tt), 1)
    before = (tr < tc).astype(BF16)
    sel_b = sel_all.astype(BF16)
    carry = carry_ref[...]
    rank_all = _dot(sel_b, before) + carry[:, 0:1]
    carry_new = carry + _dot(sel_b, jnp.ones((tt, LANES), BF16))
    carry_ref[...] = carry_new
    cnt_ref[...] = carry_new

    s_sel = [jnp.sum(jnp.where(oh, s, 0.0), axis=0, keepdims=True) for _, oh in picks]
    total = s_sel[0]
    for v in s_sel[1:]:
        total = total + v
    for k, (idx, oh) in enumerate(picks):
        eidx_ref[k:k + 1, :] = idx
        wts_ref[k:k + 1, :] = s_sel[k] / total * ROUTED_SCALE
        rank_ref[k:k + 1, :] = jnp.sum(jnp.where(oh, rank_all, 0.0), axis=0, keepdims=True).astype(I32)


def _route(x1, wr_hi, wr_lo, bias, tt):
    t = x1.shape[0]
    kern = functools.partial(_route_kernel, tt=tt)
    return pl.pallas_call(
        kern,
        grid=(t // tt,),
        in_specs=[pl.BlockSpec((tt, D_MODEL), lambda i: (i, 0)),
                  pl.BlockSpec((N_EXPERTS, D_MODEL), lambda i: (0, 0)),
                  pl.BlockSpec((N_EXPERTS, D_MODEL), lambda i: (0, 0)),
                  pl.BlockSpec((N_EXPERTS, tt), lambda i: (0, 0))],
        out_specs=[pl.BlockSpec((TOP_K, tt), lambda i: (0, i)),
                   pl.BlockSpec((TOP_K, tt), lambda i: (0, i)),
                   pl.BlockSpec((TOP_K, tt), lambda i: (0, i)),
                   pl.BlockSpec((N_EXPERTS, LANES), lambda i: (0, 0))],
        out_shape=[jax.ShapeDtypeStruct((TOP_K, t), I32),
                   jax.ShapeDtypeStruct((TOP_K, t), F32),
                   jax.ShapeDtypeStruct((TOP_K, t), I32),
                   jax.ShapeDtypeStruct((N_EXPERTS, LANES), F32)],
        scratch_shapes=[pltpu.VMEM((N_EXPERTS, LANES), F32)],
        compiler_params=_params("arbitrary"),
        name="route",
    )(x1, wr_hi, wr_lo, bias)


def _dest_kernel(eidx_ref, rank_ref, pstart_ref, dest_ref):
    eidx = eidx_ref[...]
    acc = rank_ref[...]
    for e in range(N_EXPERTS):
        acc = acc + jnp.where(eidx == e, pstart_ref[e], 0)
    dest_ref[...] = acc


def _dest(eidx_t, rank_t, pstart, tt):
    t = eidx_t.shape[1]
    return pl.pallas_call(
        _dest_kernel,
        grid=(t // tt,),
        in_specs=[pl.BlockSpec((TOP_K, tt), lambda i: (0, i)),
                  pl.BlockSpec((TOP_K, tt), lambda i: (0, i)),
                  pl.BlockSpec(memory_space=pltpu.SMEM)],
        out_specs=pl.BlockSpec((TOP_K, tt), lambda i: (0, i)),
        out_shape=jax.ShapeDtypeStruct((TOP_K, t), I32),
        compiler_params=_params("parallel"),
        name="dest",
    )(eidx_t, rank_t, pstart)


def _dispatch(dest_kt, x1p, n_rows):
    t, width = x1p.shape
    info = plsc.get_sparse_core_info()
    nc, nw = info.num_cores, info.num_cores * info.num_subcores
    chunk = SC_SCATTER_ROWS
    per_w = t // nw
    n_chunks = per_w // chunk
    assert per_w % chunk == 0
    idx = dest_kt.reshape(TOP_K, nw, n_chunks, chunk).transpose(1, 2, 0, 3).reshape(nw, n_chunks * TOP_K, chunk)
    mesh = plsc.VectorSubcoreMesh(core_axis_name="c", subcore_axis_name="s")

    @functools.partial(
        pl.kernel, mesh=mesh, name="dispatch",
        out_type=jax.ShapeDtypeStruct((n_rows, width), x1p.dtype),
        scratch_types=[pltpu.VMEM((n_chunks * TOP_K, chunk), I32),
                       pltpu.VMEM((chunk, width), x1p.dtype),
                       pltpu.SemaphoreType.DMA])
    def scatter(x_hbm, idx_hbm, xs_hbm, idx_v, rows_v, sem):
        wid = lax.axis_index("s") * nc + lax.axis_index("c")
        base = wid * per_w
        pltpu.sync_copy(idx_hbm.at[wid], idx_v)
        for j in range(n_chunks):
            pltpu.sync_copy(x_hbm.at[pl.ds(base + j * chunk, chunk)], rows_v)
            copies = [pltpu.make_async_copy(rows_v, xs_hbm.at[idx_v.at[j * TOP_K + k]], sem)
                      for k in range(TOP_K)]
            for cp in copies:
                cp.start()
            for cp in copies:
                cp.wait()

    return scatter(x1p, idx)


def _gather_rows(table, idx):
    n = idx.shape[0]
    width = table.shape[1]
    info = plsc.get_sparse_core_info()
    nc, nw = info.num_cores, info.num_cores * info.num_subcores
    chunk = SC_GATHER_ROWS
    per_w = n // nw
    n_chunks = per_w // chunk
    assert per_w % (2 * chunk) == 0
    mesh = plsc.VectorSubcoreMesh(core_axis_name="c", subcore_axis_name="s")

    @functools.partial(
        pl.kernel, mesh=mesh, name="gather_rows",
        out_type=jax.ShapeDtypeStruct((n, width), table.dtype),
        scratch_types=[pltpu.VMEM((n_chunks, chunk), I32),
                       pltpu.VMEM((2, chunk, width), table.dtype),
                       pltpu.SemaphoreType.DMA((2,)),
                       pltpu.SemaphoreType.DMA((2,))])
    def gather(table_hbm, idx_hbm, out_hbm, idx_v, rows_v, gsem, osem):
        wid = lax.axis_index("s") * nc + lax.axis_index("c")
        base = wid * per_w
        pltpu.sync_copy(idx_hbm.at[wid], idx_v)

        def fetch(j, b):
            return pltpu.make_async_copy(table_hbm.at[idx_v.at[j]], rows_v.at[b], gsem.at[b])

        def put(j, b):
            return pltpu.make_async_copy(rows_v.at[b], out_hbm.at[pl.ds(base + j * chunk, chunk)], osem.at[b])

        fetch(0, 0).start()

        @pl.loop(0, n_chunks, step=2)
        def _(j0):
            for b in range(2):
                j = j0 + b
                fetch(j, b).wait()

                @pl.when(j + 1 < n_chunks)
                def _():
                    @pl.when(j >= 1)
                    def _():
                        put(j - 1, 1 - b).wait()

                    fetch(j + 1, 1 - b).start()

                put(j, b).start()

        put(n_chunks - 2, 0).wait()
        put(n_chunks - 1, 1).wait()

    return gather(table, idx.reshape(nw, n_chunks, chunk))


def _experts_kernel(be_ref, nv_ref, xs_ref, wg_ref, wu_ref, wd_ref, ys_ref, wgu_s, wd_s, cur_ref):
    i = pl.program_id(0)
    n_valid = nv_ref[i]
    half = ROW_BLOCK // 2

    @pl.when(i == 0)
    def _():
        cur_ref[0] = -1

    @pl.when((n_valid > 0) & (cur_ref[0] != be_ref[i]))
    def _():
        wgu_s[:, :EXPERT_FF] = wg_ref[...].astype(BF16)
        wgu_s[:, EXPERT_FF:] = wu_ref[...].astype(BF16)
        wd_s[...] = wd_ref[...].astype(BF16)
        cur_ref[0] = be_ref[i]

    def rows_bf16(r0):
        valid = lax.broadcasted_iota(I32, (half, xs_ref.shape[1]), 0) + r0 < n_valid
        lo, hi = _unpack_bf16_pairs(jnp.where(valid, xs_ref[r0:r0 + half, :], jnp.zeros((), U32)))
        return jnp.concatenate([lo.astype(BF16), hi.astype(BF16)], axis=1)

    def hidden(gu):
        return (_silu(gu[:, :EXPERT_FF]) * gu[:, EXPERT_FF:]).astype(BF16)

    @pl.when(n_valid > half)
    def _():
        xa, xb = rows_bf16(0), rows_bf16(half)
        gua = _dot(xa, wgu_s[...])
        gub = _dot(xb, wgu_s[...])
        ya = _dot(hidden(gua), wd_s[...])
        yb = _dot(hidden(gub), wd_s[...])
        ys_ref[0:half, :] = _pack_bf16_pairs(ya)
        ys_ref[half:, :] = _pack_bf16_pairs(yb)

    @pl.when((n_valid > 0) & (n_valid <= half))
    def _():
        ya = _dot(hidden(_dot(rows_bf16(0), wgu_s[...])), wd_s[...])
        ys_ref[0:half, :] = _pack_bf16_pairs(ya)
        ys_ref[half:, :] = jnp.zeros((half, ys_ref.shape[1]), ys_ref.dtype)

    @pl.when(n_valid <= 0)
    def _():
        ys_ref[...] = jnp.zeros_like(ys_ref)


def _experts(block_e, n_valid, xs, wg, wu, wd):
    n_rows, width = xs.shape
    nb = n_rows // ROW_BLOCK
    grid_spec = pltpu.PrefetchScalarGridSpec(
        num_scalar_prefetch=2,
        grid=(nb,),
        in_specs=[pl.BlockSpec((ROW_BLOCK, width), lambda i, be, nv: (i, 0)),
                  pl.BlockSpec((None, D_MODEL, EXPERT_FF), lambda i, be, nv: (be[i], 0, 0)),
                  pl.BlockSpec((None, D_MODEL, EXPERT_FF), lambda i, be, nv: (be[i], 0, 0)),
                  pl.BlockSpec((None, EXPERT_FF, D_MODEL), lambda i, be, nv: (be[i], 0, 0))],
        out_specs=pl.BlockSpec((ROW_BLOCK, width), lambda i, be, nv: (i, 0)),
        scratch_shapes=[pltpu.VMEM((D_MODEL, 2 * EXPERT_FF), BF16),
                        pltpu.VMEM((EXPERT_FF, D_MODEL), BF16),
                        pltpu.SMEM((1,), I32)],
    )
    return pl.pallas_call(
        _experts_kernel,
        grid_spec=grid_spec,
        out_shape=jax.ShapeDtypeStruct((n_rows, width), xs.dtype),
        compiler_params=_params("arbitrary"),
        name="experts",
    )(block_e, n_valid, xs, wg, wu, wd)


def _shared_kernel(x1b_ref, wsg_ref, wsu_ref, wsd_ref, anchor_ref, sh_ref):
    del anchor_ref
    xb = x1b_ref[...]
    hid = (_silu(_dot(xb, wsg_ref[...])) * _dot(xb, wsu_ref[...])).astype(BF16)
    sh_ref[...] = _dot(hid, wsd_ref[...])


def _shared(x1b, wsg, wsu, wsd, anchor, tt):
    t = x1b.shape[0]
    full = lambda i: (0, 0)
    return pl.pallas_call(
        _shared_kernel,
        grid=(t // tt,),
        in_specs=[pl.BlockSpec((tt, D_MODEL), lambda i: (i, 0)),
                  pl.BlockSpec((D_MODEL, SHARED_FF), full),
                  pl.BlockSpec((D_MODEL, SHARED_FF), full),
                  pl.BlockSpec((SHARED_FF, D_MODEL), full),
                  pl.BlockSpec(memory_space=pl.ANY)],
        out_specs=pl.BlockSpec((tt, D_MODEL), lambda i: (i, 0)),
        out_shape=jax.ShapeDtypeStruct((t, D_MODEL), F32),
        compiler_params=_params("parallel"),
        name="shared",
    )(x1b, wsg, wsu, wsd, anchor)


def _combine_kernel(yg_ref, w_ref, x1_ref, sh_ref, vec_ref, x2_ref, x2b_ref):
    shared = sh_ref[...]
    w = w_ref[...]
    half = D_MODEL // 2
    acc_lo, acc_hi = shared[:, :half], shared[:, half:]
    for k in range(TOP_K):
        lo, hi = _unpack_bf16_pairs(yg_ref[k])
        acc_lo = acc_lo + w[:, k:k + 1] * lo
        acc_hi = acc_hi + w[:, k:k + 1] * hi
    acc = jnp.concatenate([acc_lo, acc_hi], axis=1)
    x2 = _layer_norm(DEEPNORM_ALPHA * x1_ref[...] + acc, vec_ref[0:1, :], vec_ref[1:2, :])
    x2_ref[...] = x2
    x2b_ref[...] = x2.astype(BF16)


def _combine(yg, w_tok, x1, shared, vec, tt):
    t = x1.shape[0]
    full = lambda i: (0, 0)
    return pl.pallas_call(
        _combine_kernel,
        grid=(t // tt,),
        in_specs=[pl.BlockSpec((TOP_K, tt, yg.shape[2]), lambda i: (0, i, 0)),
                  pl.BlockSpec((tt, TOP_K), lambda i: (i, 0)),
                  pl.BlockSpec((tt, D_MODEL), lambda i: (i, 0)),
                  pl.BlockSpec((tt, D_MODEL), lambda i: (i, 0)),
                  pl.BlockSpec((8, D_MODEL), full)],
        out_specs=[pl.BlockSpec((tt, D_MODEL), lambda i: (i, 0)),
                   pl.BlockSpec((tt, D_MODEL), lambda i: (i, 0))],
        out_shape=[jax.ShapeDtypeStruct((t, D_MODEL), F32),
                   jax.ShapeDtypeStruct((t, D_MODEL), BF16)],
        compiler_params=_params("parallel"),
        name="combine",
    )(yg, w_tok, x1, shared, vec)


def _pad_rows(a, rows):
    return jnp.zeros((rows, a.shape[-1]), F32).at[:a.shape[0]].set(a.astype(F32))


def _layer(x, xb, p, batch, seq):
    t = batch * seq
    gw = GDN_WIDTH
    w_in = p["w_in"]
    c_z, c_a, c_glu = 3 * gw, 4 * gw, 4 * gw + 2 * GDN_HEADS
    c_ga = c_glu + 2 * CONF_CH
    c_gb = c_ga + D_MODEL
    w_main = jnp.concatenate([w_in[:, :c_a], w_in[:, c_glu:]], axis=1).astype(BF16)
    w_a = jnp.zeros((D_MODEL, LANES), F32).at[:, :GDN_HEADS].set(w_in[:, c_a:c_a + GDN_HEADS])
    w_b = jnp.zeros((D_MODEL, LANES), F32).at[:, :GDN_HEADS].set(w_in[:, c_a + GDN_HEADS:c_glu])
    wa_hi, wb_hi = w_a.astype(BF16), w_b.astype(BF16)
    wa_lo = (w_a - wa_hi.astype(F32)).astype(BF16)
    wb_lo = (w_b - wb_hi.astype(F32)).astype(BF16)
    del c_z, c_gb

    tm = min(1024, t)
    proj = _proj(xb, w_main, tm, 1024)

    prm = jnp.zeros((8, LANES), F32)
    prm = prm.at[0, :GDN_HEADS].set(p["a_log"]).at[1, :GDN_HEADS].set(p["dt_bias"])
    ong = p["o_norm_g"].reshape(1, HEAD_DIM).astype(F32)
    u, w, qd, m2, eg = _gdn_intra(proj, x, wa_hi, wa_lo, wb_hi, wb_lo, p["conv_qkv"].astype(F32), prm,
                                  seq, min(GDN_TILE, seq))
    o_gdn = _gdn_scan(u, w, qd, m2, eg, proj, ong, batch, seq)

    ts = min(256, seq)
    dww = _pad_rows(p["dw_w"], CONF_HALO)
    vec = _pad_rows(jnp.stack([p["dw_b"], p["cln_g"], p["cln_b"], p["ln1_g"], p["ln1_b"]]), 8)
    x1, x1b, x1p = _mixer(proj, o_gdn, x, p["w_oa"].astype(BF16), p["w_ob"].astype(BF16),
                          p["w_o"].astype(BF16), dww, vec, batch, seq, ts)

    tt_r = min(256, t)
    wr_t = p["w_router"].T.astype(F32)
    wr_hi = wr_t.astype(BF16)
    wr_lo = (wr_t - wr_hi.astype(F32)).astype(BF16)
    bias = jnp.broadcast_to(p["router_bias"].astype(F32)[:, None], (N_EXPERTS, tt_r))
    eidx_t, wts_t, rank_t, cnt = _route(x1, wr_hi, wr_lo, bias, tt_r)

    counts = cnt[:, 0].astype(I32)
    padded = (counts + ROW_BLOCK - 1) // ROW_BLOCK * ROW_BLOCK
    pend = jnp.cumsum(padded)
    pstart = (pend - padded).astype(I32)
    n_blocks = -(-(t * TOP_K + N_EXPERTS * (ROW_BLOCK - 1)) // ROW_BLOCK)
    n_rows = n_blocks * ROW_BLOCK
    block_start = jnp.arange(n_blocks, dtype=I32) * ROW_BLOCK
    owner = jnp.sum((pend[None, :] <= block_start[:, None]).astype(I32), axis=1)
    block_e = jnp.minimum(owner, N_EXPERTS - 1).astype(I32)
    used_end = (pstart + counts)[block_e]
    n_valid = jnp.where(owner < N_EXPERTS, jnp.clip(used_end - block_start, 0, ROW_BLOCK), 0).astype(I32)

    dest = _dest(eidx_t, rank_t, pstart, min(2048, t))
    xs = _dispatch(dest, x1p, n_rows)
    ys = _experts(block_e, n_valid, xs, p["w_gate_e"], p["w_up_e"], p["w_down_e"])
    yg = _gather_rows(ys, dest.reshape(TOP_K * t)).reshape(TOP_K, t, ys.shape[1])
    shared = _shared(x1b, p["w_sh_gate"].astype(BF16), p["w_sh_up"].astype(BF16),
                     p["w_sh_down"].astype(BF16), ys, min(512, t))
    vec2 = _pad_rows(jnp.stack([p["ln2_g"], p["ln2_b"]]), 8)
    x2, x2b = _combine(yg, wts_t.T, x1, shared, vec2, min(256, t))
    return x2, x2b


_PARAM_NAMES = ("w_in", "conv_qkv", "a_log", "dt_bias", "o_norm_g", "w_oa", "dw_w", "dw_b", "cln_g",
                "cln_b", "w_ob", "w_o", "ln1_g", "ln1_b", "w_router", "router_bias", "w_gate_e",
                "w_up_e", "w_down_e", "w_sh_gate", "w_sh_up", "w_sh_down", "ln2_g", "ln2_b")


def kernel(x, w_in, conv_qkv, a_log, dt_bias, o_norm_g, w_oa, dw_w, dw_b, cln_g, cln_b, w_ob, w_o,
           ln1_g, ln1_b, w_router, router_bias, w_gate_e, w_up_e, w_down_e, w_sh_gate, w_sh_up,
           w_sh_down, ln2_g, ln2_b):
    stacked = dict(zip(_PARAM_NAMES, (w_in, conv_qkv, a_log, dt_bias, o_norm_g, w_oa, dw_w, dw_b, cln_g,
                                      cln_b, w_ob, w_o, ln1_g, ln1_b, w_router, router_bias, w_gate_e,
                                      w_up_e, w_down_e, w_sh_gate, w_sh_up, w_sh_down, ln2_g, ln2_b)))
    batch, seq, d = x.shape
    assert d == D_MODEL and seq % CHUNK == 0
    xf = x.reshape(batch * seq, d).astype(F32)
    xb = xf.astype(BF16)
    for layer in range(w_in.shape[0]):
        p = {name: arr[layer] for name, arr in stacked.items()}
        xf, xb = _layer(xf, xb, p, batch, seq)
    return xf.reshape(batch, seq, d).astype(x.dtype)
```

```python
import functools

import jax
import jax.numpy as jnp
import numpy as np
from jax import lax
from jax.experimental import pallas as pl
from jax.experimental.pallas import tpu as pltpu
from jax.experimental.pallas import tpu_sc as plsc

F32 = jnp.float32
BF16 = jnp.bfloat16
I32 = jnp.int32
U32 = jnp.uint32
HI_MASK = np.uint32(0xFFFF0000)

D_MODEL = 1024
GDN_HEADS = 8
HEAD_DIM = 128
GDN_WIDTH = GDN_HEADS * HEAD_DIM
SHORT_CONV = 4
CHUNK = 64
SOLVE_BLOCK = 16
GDN_TILE = 256
SCAN_CHUNKS = 4
CONF_CH = D_MODEL
CONF_KERNEL = 31
CONF_HALO = 32
N_EXPERTS = 64
TOP_K = 8
N_GROUPS = 8
GROUP_SIZE = N_EXPERTS // N_GROUPS
TOPK_GROUPS = 4
EXPERT_FF = 256
SHARED_FF = 256
ROUTED_SCALE = 2.5
DEPTH = 2
DEEPNORM_ALPHA = (2 * DEPTH) ** 0.25
EPS = 1e-6

LANES = 128
PROJ_COLS = 8 * D_MODEL
ROW_BLOCK = 512
SC_SCATTER_ROWS = 128
SC_GATHER_ROWS = 64
VMEM_LIMIT = 56 * 1024 * 1024


def _params(*sem):
    return pltpu.CompilerParams(dimension_semantics=sem, vmem_limit_bytes=VMEM_LIMIT)


def _dot(a, b):
    return jnp.dot(a, b, preferred_element_type=F32)


def _dot_nt(a, b):
    return lax.dot_general(a, b, (((1,), (1,)), ((), ())), preferred_element_type=F32)


def _split(a):
    hi = a.astype(BF16)
    lo = (a - hi.astype(F32)).astype(BF16)
    return hi, lo


def _dot3(a, b):
    ah, al = _split(a)
    bh, bl = _split(b)
    return _dot(ah, bh) + _dot(al, bh) + _dot(ah, bl)


def _sigmoid(x):
    return 1.0 / (1.0 + jnp.exp(-x))


def _silu(x):
    return x * _sigmoid(x)


def _layer_norm(y, g, b):
    mu = jnp.mean(y, axis=-1, keepdims=True)
    yc = y - mu
    var = jnp.mean(yc * yc, axis=-1, keepdims=True)
    return yc * lax.rsqrt(var + EPS) * g + b


def _proj_kernel(a_ref, w_ref, o_ref):
    o_ref[...] = _dot(a_ref[...].astype(BF16), w_ref[...])


def _proj(xb, w, tm, tn):
    m, k = xb.shape
    n = w.shape[1]
    return pl.pallas_call(
        _proj_kernel,
        grid=(m // tm, n // tn),
        in_specs=[pl.BlockSpec((tm, k), lambda i, j: (i, 0)),
                  pl.BlockSpec((k, tn), lambda i, j: (0, j))],
        out_specs=pl.BlockSpec((tm, tn), lambda i, j: (i, j)),
        out_shape=jax.ShapeDtypeStruct((m, n), F32),
        compiler_params=_params("parallel", "parallel"),
        name="proj",
    )(xb, w)


def _unit_lower_inverse4(als):
    c = CHUNK
    n = als[0].shape[1]
    row = lax.broadcasted_iota(I32, (c, n), 0)
    col = jnp.bitwise_and(lax.broadcasted_iota(I32, (c, n), 1), c - 1)
    shift = SOLVE_BLOCK.bit_length() - 1
    same = jnp.right_shift(row, shift) == jnp.right_shift(col, shift)
    eye = (row == col).astype(F32)
    cshift = c.bit_length() - 1
    brow = jnp.right_shift(lax.broadcasted_iota(I32, (n, n), 0), cshift)
    bcol = jnp.right_shift(lax.broadcasted_iota(I32, (n, n), 1), cshift)
    on_diag = brow == bcol

    def mm(x, y):
        yb = y.astype(BF16)
        bd = jnp.where(on_diag, jnp.concatenate([yb] * (n // c), axis=0), jnp.zeros((), BF16))
        return _dot(x.astype(BF16), bd)

    a_diag = [jnp.where(same, al, 0.0) for al in als]
    a_off = [al - ad for al, ad in zip(als, a_diag)]
    bp = [-ad for ad in a_diag]
    p = [eye + b for b in bp]
    for _ in range(3):
        bp = [mm(b, b) for b in bp]
        p = [x + mm(x, b) for x, b in zip(p, bp)]
    n1 = [mm(x, ao) for x, ao in zip(p, a_off)]
    n2 = [mm(x, x) for x in n1]
    q = [x + mm(y, x) for x, y in zip(p, n2)]
    return [x - mm(y, x) for x, y in zip(q, n1)]


def _gdn_intra_kernel(qkv_ref, prev_ref, x_ref, wah_ref, wal_ref, wbh_ref, wbl_ref, cw_ref, prm_ref,
                      u_ref, w_ref, qd_ref, m2_ref, eg_ref, xe_ref, *, rt, tiles_per_seq):
    c = CHUNK
    nc = rt // c
    first = (pl.program_id(0) % tiles_per_seq) == 0
    xe_ref[0:8, :] = jnp.where(first, 0.0, prev_ref[...])
    xe_ref[8:8 + rt, :] = qkv_ref[...]

    xh, xl = _split(x_ref[...])

    def proj3(wh_ref, wl_ref):
        wh = wh_ref[...]
        return _dot(xh, wh) + _dot(xl, wh) + _dot(xh, wl_ref[...])

    a_raw = proj3(wah_ref, wal_ref)
    b_raw = proj3(wbh_ref, wbl_ref)
    sp_in = a_raw + prm_ref[1:2, :]
    softplus = jnp.maximum(sp_in, 0.0) + jnp.log(1.0 + jnp.exp(-jnp.abs(sp_in)))
    g = -jnp.exp(prm_ref[0:1, :]) * softplus
    beta = _sigmoid(b_raw)

    cshift = c.bit_length() - 1
    r2 = lax.broadcasted_iota(I32, (rt, rt), 0)
    c2 = lax.broadcasted_iota(I32, (rt, rt), 1)
    same_chunk = jnp.right_shift(r2, cshift) == jnp.right_shift(c2, cshift)
    ltri = jnp.where(r2 >= c2, jnp.where(same_chunk, 1.0, 0.0), 0.0).astype(BF16)
    g_hi = g.astype(BF16)
    g_r = g - g_hi.astype(F32)
    g_mid = g_r.astype(BF16)
    g_lo = (g_r - g_mid.astype(F32)).astype(BF16)
    gc = _dot(ltri, g_hi) + _dot(ltri, g_mid) + _dot(ltri, g_lo)
    gct = gc.T
    egc = jnp.exp(gc)
    gend = jnp.concatenate(
        [jnp.broadcast_to(gc[ci * c + c - 1:ci * c + c, :], (c, LANES)) for ci in range(nc)], axis=0)
    kfac = jnp.exp(gend - gc)
    bege = beta * egc
    for ci in range(nc):
        last = ci * c + c - 1
        eg_ref[ci * GDN_HEADS:(ci + 1) * GDN_HEADS, :] = jnp.broadcast_to(
            jnp.exp(gct[0:GDN_HEADS, last:last + 1]), (GDN_HEADS, LANES))

    lane_t = lax.broadcasted_iota(I32, (rt, LANES), 1) < c
    lane_lo = lax.broadcasted_iota(I32, (c, LANES), 1) < c
    lcol = jnp.bitwise_and(lax.broadcasted_iota(I32, (c, LANES), 1), c - 1)
    rowi = lax.broadcasted_iota(I32, (c, LANES), 0)
    causal = rowi >= lcol
    strict = rowi > lcol

    def conv(base, h):
        lo, hi = base + h * HEAD_DIM, base + (h + 1) * HEAD_DIM
        acc = cw_ref[SHORT_CONV - 1:SHORT_CONV, lo:hi] * xe_ref[8:8 + rt, lo:hi]
        for j in range(SHORT_CONV - 1):
            s0 = 8 - (SHORT_CONV - 1) + j
            acc = acc + cw_ref[j:j + 1, lo:hi] * xe_ref[s0:s0 + rt, lo:hi]
        return _silu(acc)

    a_pairs = [[None] * (GDN_HEADS // 2) for _ in range(nc)]
    rhs_pairs = [[None] * (GDN_HEADS // 2) for _ in range(nc)]
    for p in range(GDN_HEADS // 2):
        ks, kbs, qs, kds, rhss = [], [], [], [], []
        for h in (2 * p, 2 * p + 1):
            q = conv(0, h)
            k = conv(GDN_WIDTH, h)
            v = conv(2 * GDN_WIDTH, h)
            q = q * lax.rsqrt(jnp.sum(q * q, axis=-1, keepdims=True) + EPS) * (HEAD_DIM ** -0.5)
            k = k * lax.rsqrt(jnp.sum(k * k, axis=-1, keepdims=True) + EPS)
            beta_h = beta[:, h:h + 1]
            qd_ref[:, h * HEAD_DIM:(h + 1) * HEAD_DIM] = (q * egc[:, h:h + 1]).astype(BF16)
            ks.append(k)
            kbs.append(k * beta_h)
            qs.append(q)
            kds.append(k * kfac[:, h:h + 1])
            rhss.append(jnp.concatenate([v * beta_h, k * bege[:, h:h + 1]], axis=1))
        h0, h1 = 2 * p, 2 * p + 1
        gch = jnp.where(lane_t, gc[:, h0:h0 + 1], gc[:, h1:h1 + 1])
        for ci in range(nc):
            rows = slice(ci * c, (ci + 1) * c)
            wk = jnp.concatenate([ks[0][rows], ks[1][rows]], axis=0).astype(BF16)
            lhs = jnp.concatenate([kbs[0][rows], qs[0][rows], kbs[1][rows], qs[1][rows]],
                                  axis=0).astype(BF16)
            out = _dot_nt(lhs, wk)
            gcrow = jnp.concatenate([gct[h0:h0 + 1, rows], gct[h1:h1 + 1, rows]], axis=1)
            diff = gch[rows] - gcrow
            decay = jnp.where(causal, jnp.exp(jnp.where(causal, diff, 0.0)), 0.0)
            a_pairs[ci][p] = jnp.where(strict, jnp.where(lane_lo, out[0:c], out[2 * c:3 * c]) * decay, 0.0)
            qk = jnp.where(lane_lo, out[c:2 * c], out[3 * c:4 * c]) * decay
            kdt = jnp.concatenate([kds[0][rows], kds[1][rows]], axis=0).T
            m0 = ci * 3 * c
            m2_ref[m0:m0 + c, p * LANES:(p + 1) * LANES] = qk.astype(BF16)
            m2_ref[m0 + c:m0 + 3 * c, p * LANES:(p + 1) * LANES] = kdt.astype(BF16)
            rhs_pairs[ci][p] = (rhss[0][rows], rhss[1][rows])

    zeros = jnp.zeros((c, 2 * HEAD_DIM), BF16)
    ngrp = GDN_HEADS // 4
    tls = _unit_lower_inverse4(
        [jnp.concatenate([a_pairs[ci][2 * grp], a_pairs[ci][2 * grp + 1]], axis=1)
         for ci in range(nc) for grp in range(ngrp)])
    for ci in range(nc):
        rows = slice(ci * c, (ci + 1) * c)
        for grp in range(ngrp):
            tl = tls[ci * ngrp + grp]
            for j in range(2):
                p = 2 * grp + j
                r0, r1 = rhs_pairs[ci][p]
                bd = jnp.concatenate([jnp.concatenate([r0.astype(BF16), zeros], axis=1),
                                      jnp.concatenate([zeros, r1.astype(BF16)], axis=1)], axis=0)
                sol = _dot(tl[:, j * LANES:(j + 1) * LANES].astype(BF16), bd)
                for i in range(2):
                    h = 2 * p + i
                    lo, hi = h * HEAD_DIM, (h + 1) * HEAD_DIM
                    u_ref[rows, lo:hi] = sol[:, 2 * i * HEAD_DIM:(2 * i + 1) * HEAD_DIM]
                    w_ref[rows, lo:hi] = sol[:, (2 * i + 1) * HEAD_DIM:(2 * i + 2) * HEAD_DIM].astype(BF16)


def _gdn_intra(proj, x, wa_hi, wa_lo, wb_hi, wb_lo, conv_w, prm, seq, rt):
    t = x.shape[0]
    nc = rt // CHUNK
    kern = functools.partial(_gdn_intra_kernel, rt=rt, tiles_per_seq=seq // rt)
    full = lambda i: (0, 0)
    tile = lambda i: (i, 0)
    return pl.pallas_call(
        kern,
        grid=(t // rt,),
        in_specs=[
            pl.BlockSpec((rt, 3 * GDN_WIDTH), tile),
            pl.BlockSpec((8, 3 * GDN_WIDTH), lambda i: (jnp.maximum(i * (rt // 8) - 1, 0), 0)),
            pl.BlockSpec((rt, D_MODEL), tile),
            pl.BlockSpec((D_MODEL, LANES), full),
            pl.BlockSpec((D_MODEL, LANES), full),
            pl.BlockSpec((D_MODEL, LANES), full),
            pl.BlockSpec((D_MODEL, LANES), full),
            pl.BlockSpec((SHORT_CONV, 3 * GDN_WIDTH), full),
            pl.BlockSpec((8, LANES), full),
        ],
        out_specs=[pl.BlockSpec((rt, GDN_WIDTH), tile),
                   pl.BlockSpec((rt, GDN_WIDTH), tile),
                   pl.BlockSpec((rt, GDN_WIDTH), tile),
                   pl.BlockSpec((nc * 3 * CHUNK, GDN_HEADS // 2 * LANES), tile),
                   pl.BlockSpec((nc * GDN_HEADS, LANES), tile)],
        out_shape=[jax.ShapeDtypeStruct((t, GDN_WIDTH), F32),
                   jax.ShapeDtypeStruct((t, GDN_WIDTH), BF16),
                   jax.ShapeDtypeStruct((t, GDN_WIDTH), BF16),
                   jax.ShapeDtypeStruct((t // CHUNK * 3 * CHUNK, GDN_HEADS // 2 * LANES), BF16),
                   jax.ShapeDtypeStruct((t // CHUNK * GDN_HEADS, LANES), F32)],
        scratch_shapes=[pltpu.VMEM((8 + rt, 3 * GDN_WIDTH), F32)],
        compiler_params=_params("parallel"),
        name="gdn_intra",
    )(proj, proj, x, wa_hi, wa_lo, wb_hi, wb_lo, conv_w, prm)


def _gdn_scan_kernel(u_ref, w_ref, qd_ref, m2_ref, eg_ref, z_ref, ong_ref, o_ref, s_ref, *, nck):
    c = CHUNK

    @pl.when(pl.program_id(1) == 0)
    def _():
        s_ref[...] = jnp.zeros_like(s_ref)

    ong = ong_ref[...]
    zeros = jnp.zeros((c, HEAD_DIM), BF16)
    heads = range(GDN_HEADS)
    span = lambda h: slice(h * HEAD_DIM, (h + 1) * HEAD_DIM)
    states = [s_ref[h] for h in heads]
    for ci in range(nck):
        rows = slice(ci * c, (ci + 1) * c)
        rs = [_dot(jnp.concatenate([w_ref[rows, span(h)], qd_ref[rows, span(h)]], axis=0),
                   states[h].astype(BF16)) for h in heads]
        v_new = [(u_ref[rows, span(h)] - rs[h][:c]).astype(BF16) for h in heads]
        r2s = []
        for p in range(GDN_HEADS // 2):
            bd = jnp.concatenate([jnp.concatenate([v_new[2 * p], zeros], axis=1),
                                  jnp.concatenate([zeros, v_new[2 * p + 1]], axis=1)], axis=0)
            r2s.append(_dot(m2_ref[ci * 3 * c:(ci + 1) * 3 * c, p * LANES:(p + 1) * LANES], bd))
        for h in heads:
            half = span(h % 2)
            r2 = r2s[h // 2]
            states[h] = states[h] * eg_ref[ci * GDN_HEADS + h:ci * GDN_HEADS + h + 1, :] + r2[c:, half]
            o = rs[h][c:] + r2[:c, half]
            o = o * lax.rsqrt(jnp.mean(o * o, axis=-1, keepdims=True) + EPS) * ong
            o = o * _silu(z_ref[rows, span(h)])
            o_ref[rows, span(h)] = o.astype(o_ref.dtype)
    for h in heads:
        s_ref[h] = states[h]


def _gdn_scan(u, w, qd, m2, eg, proj, ong, batch, seq):
    t = batch * seq
    nck = min(SCAN_CHUNKS, seq // CHUNK)
    c = nck * CHUNK
    nch = seq // c
    blk = lambda b, n: (b * nch + n, 0)
    return pl.pallas_call(
        functools.partial(_gdn_scan_kernel, nck=nck),
        grid=(batch, nch),
        in_specs=[
            pl.BlockSpec((c, GDN_WIDTH), blk),
            pl.BlockSpec((c, GDN_WIDTH), blk),
            pl.BlockSpec((c, GDN_WIDTH), blk),
            pl.BlockSpec((3 * c, GDN_HEADS // 2 * LANES), blk),
            pl.BlockSpec((nck * GDN_HEADS, LANES), blk),
            pl.BlockSpec((c, GDN_WIDTH), lambda b, n: (b * nch + n, 3)),
            pl.BlockSpec((1, HEAD_DIM), lambda b, n: (0, 0)),
        ],
        out_specs=pl.BlockSpec((c, GDN_WIDTH), blk),
        out_shape=jax.ShapeDtypeStruct((t, GDN_WIDTH), BF16),
        scratch_shapes=[pltpu.VMEM((GDN_HEADS, HEAD_DIM, HEAD_DIM), F32)],
        compiler_params=_params("parallel", "arbitrary"),
        name="gdn_scan",
    )(u, w, qd, m2, eg, proj, ong)


def _pack_bf16_pairs(y):
    n = y.shape[1] // 2
    yb = y.astype(BF16).astype(F32)
    lo = lax.bitcast_convert_type(yb[:, :n], U32)
    hi = lax.bitcast_convert_type(yb[:, n:], U32)
    return jnp.bitwise_or(jnp.right_shift(lo, 16), jnp.bitwise_and(hi, HI_MASK))


def _unpack_bf16_pairs(w):
    lo = lax.bitcast_convert_type(jnp.left_shift(w, 16), F32)
    hi = lax.bitcast_convert_type(jnp.bitwise_and(w, HI_MASK), F32)
    return lo, hi


def _mixer_kernel(glu_ref, ga_ref, gb_ref, o_ref, x_ref, woa_ref, wob_ref, wo_ref, dww_ref, vec_ref,
                  x1_ref, x1b_ref, x1p_ref, ubuf_ref, sh_ref, conv_ref, *, ts, rc):
    halo = CONF_HALO

    @pl.when(pl.program_id(1) == 0)
    def _():
        ubuf_ref[0:halo, :] = jnp.zeros((halo, CONF_CH), F32)

    @pl.when(pl.program_id(1) != 0)
    def _():
        ubuf_ref[0:halo, :] = ubuf_ref[ts:ts + halo, :]

    ubuf_ref[halo:halo + ts, :] = glu_ref[:, :CONF_CH] * _sigmoid(glu_ref[:, CONF_CH:])

    dw_b = vec_ref[0:1, :]
    cln_g = vec_ref[1:2, :]
    cln_b = vec_ref[2:3, :]
    ln1_g = vec_ref[3:4, :]
    ln1_b = vec_ref[4:5, :]

    span = ts + halo - 8
    for s in range(1, 8):
        sh_ref[s - 1] = ubuf_ref[s:s + span, :]

    def tap_rows(j, r0):
        o = halo - (CONF_KERNEL - 1) + j
        q, s = o // 8, o % 8
        if s == 0:
            return ubuf_ref[r0 + o:r0 + o + rc, :]
        return sh_ref[s - 1, r0 + 8 * q:r0 + 8 * q + rc, :]

    hs = ts // 2
    for h0 in range(0, ts, hs):
        rows = slice(h0, h0 + hs)
        gated_a = _sigmoid(ga_ref[rows, :]) * _dot(o_ref[rows, :], woa_ref[...])
        for r0 in range(h0, h0 + hs, rc):
            tap_w = lambda j: jnp.concatenate([dww_ref[j]] * (rc // 8), axis=0)
            acc = tap_w(0) * tap_rows(0, r0)
            for j in range(1, CONF_KERNEL):
                acc = acc + tap_w(j) * tap_rows(j, r0)
            conv_ref[r0:r0 + rc, :] = acc
        uc = _silu(_layer_norm(conv_ref[rows, :] + dw_b, cln_g, cln_b))
        branch_b = _dot(uc.astype(BF16), wob_ref[...])
        hmix = gated_a + _sigmoid(gb_ref[rows, :]) * branch_b
        mix = _dot(hmix.astype(BF16), wo_ref[...])
        x1 = _layer_norm(DEEPNORM_ALPHA * x_ref[rows, :] + mix, ln1_g, ln1_b)
        x1_ref[rows, :] = x1
        x1b_ref[rows, :] = x1.astype(BF16)
        x1p_ref[rows, :] = _pack_bf16_pairs(x1)


def _mixer(proj, o_gdn, x, woa, wob, wo, dww, vec, batch, seq, ts):
    t = batch * seq
    nt = seq // ts
    rows = lambda b, n: b * nt + n
    full = lambda b, n: (0, 0)
    kern = functools.partial(_mixer_kernel, ts=ts, rc=32)
    return pl.pallas_call(
        kern,
        grid=(batch, nt),
        in_specs=[
            pl.BlockSpec((ts, 2 * CONF_CH), lambda b, n: (rows(b, n), 2)),
            pl.BlockSpec((ts, D_MODEL), lambda b, n: (rows(b, n), 6)),
            pl.BlockSpec((ts, D_MODEL), lambda b, n: (rows(b, n), 7)),
            pl.BlockSpec((ts, GDN_WIDTH), lambda b, n: (rows(b, n), 0)),
            pl.BlockSpec((ts, D_MODEL), lambda b, n: (rows(b, n), 0)),
            pl.BlockSpec((GDN_WIDTH, D_MODEL), full),
            pl.BlockSpec((CONF_CH, D_MODEL), full),
            pl.BlockSpec((D_MODEL, D_MODEL), full),
            pl.BlockSpec((CONF_KERNEL, 8, CONF_CH), lambda b, n: (0, 0, 0)),
            pl.BlockSpec((8, D_MODEL), full),
        ],
        out_specs=[pl.BlockSpec((ts, D_MODEL), lambda b, n: (rows(b, n), 0)),
                   pl.BlockSpec((ts, D_MODEL), lambda b, n: (rows(b, n), 0)),
                   pl.BlockSpec((ts, D_MODEL // 2), lambda b, n: (rows(b, n), 0))],
        out_shape=[jax.ShapeDtypeStruct((t, D_MODEL), F32),
                   jax.ShapeDtypeStruct((t, D_MODEL), BF16),
                   jax.ShapeDtypeStruct((t, D_MODEL // 2), U32)],
        scratch_shapes=[pltpu.VMEM((CONF_HALO + ts, CONF_CH), F32),
                        pltpu.VMEM((7, ts + CONF_HALO - 8, CONF_CH), F32),
                        pltpu.VMEM((ts, CONF_CH), F32)],
        compiler_params=_params("parallel", "arbitrary"),
        name="mixer",
    )(proj, proj, proj, o_gdn, x, woa, wob, wo, dww, vec)


def _route_kernel(x_ref, wrh_ref, wrl_ref, bias_ref, eidx_ref, wts_ref, rank_ref, cnt_ref, carry_ref, *, tt):
    @pl.when(pl.program_id(0) == 0)
    def _():
        carry_ref[...] = jnp.zeros_like(carry_ref)

    xh, xl = _split(x_ref[...])
    wrh = wrh_ref[...]
    logits = _dot_nt(wrh, xh) + _dot_nt(wrh, xl) + _dot_nt(wrl_ref[...], xh)
    s = _sigmoid(logits)
    biased = s + bias_ref[...]

    sub = lax.broadcasted_iota(I32, (GROUP_SIZE, tt), 0)
    groups = [biased[g * GROUP_SIZE:(g + 1) * GROUP_SIZE, :] for g in range(N_GROUPS)]
    gs = []
    for bg in groups:
        m1 = jnp.max(bg, axis=0, keepdims=True)
        first = jnp.min(jnp.where(bg == m1, sub, GROUP_SIZE), axis=0, keepdims=True)
        m2 = jnp.max(jnp.where(sub == first, -jnp.inf, bg), axis=0, keepdims=True)
        gs.append(m1 + m2)

    masked_parts = []
    for g in range(N_GROUPS):
        beaten = jnp.zeros((1, tt), I32)
        for o in range(N_GROUPS):
            if o == g:
                continue
            wins = (gs[o] >= gs[g]) if o < g else (gs[o] > gs[g])
            beaten = beaten + wins.astype(I32)
        keep = jnp.broadcast_to(beaten < TOPK_GROUPS, (GROUP_SIZE, tt))
        masked_parts.append(jnp.where(keep, groups[g], -jnp.inf))
    masked = jnp.concatenate(masked_parts, axis=0)

    eiota = lax.broadcasted_iota(I32, (N_EXPERTS, tt), 0)
    sel_all = jnp.zeros((N_EXPERTS, tt), F32)
    picks = []
    for _ in range(TOP_K):
        m = jnp.max(masked, axis=0, keepdims=True)
        idx = jnp.min(jnp.where(masked == m, eiota, N_EXPERTS), axis=0, keepdims=True)
        onehot = eiota == idx
        picks.append((idx, onehot))
        sel_all = jnp.where(onehot, 1.0, sel_all)
        masked = jnp.where(onehot, -jnp.inf, masked)

    tr = lax.broadcasted_iota(I32, (tt, tt), 0)
    tc = lax.broadcasted_iota(I32, (tt, tt), 1)
    before = (tr < tc).astype(BF16)
    sel_b = sel_all.astype(BF16)
    carry = carry_ref[...]
    rank_all = _dot(sel_b, before) + carry[:, 0:1]
    carry_new = carry + _dot(sel_b, jnp.ones((tt, LANES), BF16))
    carry_ref[...] = carry_new
    cnt_ref[...] = carry_new

    s_sel = [jnp.sum(jnp.where(oh, s, 0.0), axis=0, keepdims=True) for _, oh in picks]
    total = s_sel[0]
    for v in s_sel[1:]:
        total = total + v
    for k, (idx, oh) in enumerate(picks):
        eidx_ref[k:k + 1, :] = idx
        wts_ref[k:k + 1, :] = s_sel[k] / total * ROUTED_SCALE
        rank_ref[k:k + 1, :] = jnp.sum(jnp.where(oh, rank_all, 0.0), axis=0, keepdims=True).astype(I32)


def _route(x1, wr_hi, wr_lo, bias, tt):
    t = x1.shape[0]
    kern = functools.partial(_route_kernel, tt=tt)
    return pl.pallas_call(
        kern,
        grid=(t // tt,),
        in_specs=[pl.BlockSpec((tt, D_MODEL), lambda i: (i, 0)),
                  pl.BlockSpec((N_EXPERTS, D_MODEL), lambda i: (0, 0)),
                  pl.BlockSpec((N_EXPERTS, D_MODEL), lambda i: (0, 0)),
                  pl.BlockSpec((N_EXPERTS, tt), lambda i: (0, 0))],
        out_specs=[pl.BlockSpec((TOP_K, tt), lambda i: (0, i)),
                   pl.BlockSpec((TOP_K, tt), lambda i: (0, i)),
                   pl.BlockSpec((TOP_K, tt), lambda i: (0, i)),
                   pl.BlockSpec((N_EXPERTS, LANES), lambda i: (0, 0))],
        out_shape=[jax.ShapeDtypeStruct((TOP_K, t), I32),
                   jax.ShapeDtypeStruct((TOP_K, t), F32),
                   jax.ShapeDtypeStruct((TOP_K, t), I32),
                   jax.ShapeDtypeStruct((N_EXPERTS, LANES), F32)],
        scratch_shapes=[pltpu.VMEM((N_EXPERTS, LANES), F32)],
        compiler_params=_params("arbitrary"),
        name="route",
    )(x1, wr_hi, wr_lo, bias)


def _plan_kernel(cnt_ref, pstart_ref, plan_ref):
    e, nb = N_EXPERTS, plan_ref.shape[1]
    counts = cnt_ref[...]
    nblk = jnp.floor((counts + (ROW_BLOCK - 1)) * (1.0 / ROW_BLOCK))
    hi = jnp.floor(nblk * (1.0 / 256.0))
    lo = nblk - 256.0 * hi
    r = lax.broadcasted_iota(I32, (e, e), 0)
    c = lax.broadcasted_iota(I32, (e, e), 1)
    ltri = (r >= c).astype(BF16)
    bend = 256.0 * _dot(ltri, hi.astype(BF16)) + _dot(ltri, lo.astype(BF16))
    pend = bend * ROW_BLOCK
    pstart = pend - nblk * ROW_BLOCK
    pstart_ref[...] = pstart.astype(I32)

    bs = (lax.broadcasted_iota(I32, (e, nb), 1) * ROW_BLOCK).astype(F32)
    pend_b = jnp.broadcast_to(pend[:, 0:1], (e, nb))
    pstart_b = jnp.broadcast_to(pstart[:, 0:1], (e, nb))
    used_b = jnp.broadcast_to((pstart + counts)[:, 0:1], (e, nb))
    owner = jnp.sum(jnp.where(pend_b <= bs, 1.0, 0.0), axis=0, keepdims=True)
    inside = jnp.where(pstart_b <= bs, jnp.where(bs < pend_b, 1.0, 0.0), 0.0)
    real = jnp.sum(inside * jnp.clip(used_b - bs, 0.0, float(ROW_BLOCK)), axis=0, keepdims=True)
    plan_ref[0:1, :] = jnp.minimum(owner, float(e - 1)).astype(I32)
    plan_ref[1:2, :] = real.astype(I32)
    plan_ref[2:8, :] = jnp.zeros((6, nb), I32)


def _plan(cnt, n_blocks):
    nb = -(-n_blocks // LANES) * LANES
    pstart, plan = pl.pallas_call(
        _plan_kernel,
        out_shape=[jax.ShapeDtypeStruct((N_EXPERTS, LANES), I32),
                   jax.ShapeDtypeStruct((8, nb), I32)],
        name="plan",
    )(cnt)
    return pstart[:, 0], plan[0, :n_blocks], plan[1, :n_blocks]


def _dest_kernel(eidx_ref, rank_ref, pstart_ref, dest_ref):
    eidx = eidx_ref[...]
    acc = rank_ref[...]
    for e in range(N_EXPERTS):
        acc = acc + jnp.where(eidx == e, pstart_ref[e], 0)
    dest_ref[...] = acc


def _dest(eidx_t, rank_t, pstart, tt):
    t = eidx_t.shape[1]
    return pl.pallas_call(
        _dest_kernel,
        grid=(t // tt,),
        in_specs=[pl.BlockSpec((TOP_K, tt), lambda i: (0, i)),
                  pl.BlockSpec((TOP_K, tt), lambda i: (0, i)),
                  pl.BlockSpec(memory_space=pltpu.SMEM)],
        out_specs=pl.BlockSpec((TOP_K, tt), lambda i: (0, i)),
        out_shape=jax.ShapeDtypeStruct((TOP_K, t), I32),
        compiler_params=_params("parallel"),
        name="dest",
    )(eidx_t, rank_t, pstart)


def _dispatch(dest_kt, x1p, n_rows):
    t, width = x1p.shape
    info = plsc.get_sparse_core_info()
    nc, nw = info.num_cores, info.num_cores * info.num_subcores
    chunk = SC_SCATTER_ROWS
    per_w = t // nw
    n_chunks = per_w // chunk
    assert per_w % chunk == 0
    idx = dest_kt.reshape(TOP_K, nw, n_chunks, chunk).transpose(1, 2, 0, 3).reshape(nw, n_chunks * TOP_K, chunk)
    mesh = plsc.VectorSubcoreMesh(core_axis_name="c", subcore_axis_name="s")

    @functools.partial(
        pl.kernel, mesh=mesh, name="dispatch",
        out_type=jax.ShapeDtypeStruct((n_rows, width), x1p.dtype),
        scratch_types=[pltpu.VMEM((n_chunks * TOP_K, chunk), I32),
                       pltpu.VMEM((chunk, width), x1p.dtype),
                       pltpu.SemaphoreType.DMA])
    def scatter(x_hbm, idx_hbm, xs_hbm, idx_v, rows_v, sem):
        wid = lax.axis_index("s") * nc + lax.axis_index("c")
        base = wid * per_w
        pltpu.sync_copy(idx_hbm.at[wid], idx_v)
        for j in range(n_chunks):
            pltpu.sync_copy(x_hbm.at[pl.ds(base + j * chunk, chunk)], rows_v)
            copies = [pltpu.make_async_copy(rows_v, xs_hbm.at[idx_v.at[j * TOP_K + k]], sem)
                      for k in range(TOP_K)]
            for cp in copies:
                cp.start()
            for cp in copies:
                cp.wait()

    return scatter(x1p, idx)


def _gather_rows(table, idx):
    n = idx.shape[0]
    width = table.shape[1]
    info = plsc.get_sparse_core_info()
    nc, nw = info.num_cores, info.num_cores * info.num_subcores
    chunk = SC_GATHER_ROWS
    per_w = n // nw
    n_chunks = per_w // chunk
    assert per_w % (2 * chunk) == 0
    mesh = plsc.VectorSubcoreMesh(core_axis_name="c", subcore_axis_name="s")

    @functools.partial(
        pl.kernel, mesh=mesh, name="gather_rows",
        out_type=jax.ShapeDtypeStruct((n, width), table.dtype),
        scratch_types=[pltpu.VMEM((n_chunks, chunk), I32),
                       pltpu.VMEM((2, chunk, width), table.dtype),
                       pltpu.SemaphoreType.DMA((2,)),
                       pltpu.SemaphoreType.DMA((2,))])
    def gather(table_hbm, idx_hbm, out_hbm, idx_v, rows_v, gsem, osem):
        wid = lax.axis_index("s") * nc + lax.axis_index("c")
        base = wid * per_w
        pltpu.sync_copy(idx_hbm.at[wid], idx_v)

        def fetch(j, b):
            return pltpu.make_async_copy(table_hbm.at[idx_v.at[j]], rows_v.at[b], gsem.at[b])

        def put(j, b):
            return pltpu.make_async_copy(rows_v.at[b], out_hbm.at[pl.ds(base + j * chunk, chunk)], osem.at[b])

        fetch(0, 0).start()

        @pl.loop(0, n_chunks, step=2)
        def _(j0):
            for b in range(2):
                j = j0 + b
                fetch(j, b).wait()

                @pl.when(j + 1 < n_chunks)
                def _():
                    @pl.when(j >= 1)
                    def _():
                        put(j - 1, 1 - b).wait()

                    fetch(j + 1, 1 - b).start()

                put(j, b).start()

        put(n_chunks - 2, 0).wait()
        put(n_chunks - 1, 1).wait()

    return gather(table, idx.reshape(nw, n_chunks, chunk))


def _experts_kernel(be_ref, nv_ref, xs_ref, wg_ref, wu_ref, wd_ref, ys_ref, wgu_s, wd_s, cur_ref):
    i = pl.program_id(0)
    n_valid = nv_ref[i]
    half = ROW_BLOCK // 2

    @pl.when(i == 0)
    def _():
        cur_ref[0] = -1

    @pl.when((n_valid > 0) & (cur_ref[0] != be_ref[i]))
    def _():
        wgu_s[:, :EXPERT_FF] = wg_ref[...].astype(BF16)
        wgu_s[:, EXPERT_FF:] = wu_ref[...].astype(BF16)
        wd_s[...] = wd_ref[...].astype(BF16)
        cur_ref[0] = be_ref[i]

    def rows_bf16(r0):
        valid = lax.broadcasted_iota(I32, (half, xs_ref.shape[1]), 0) + r0 < n_valid
        lo, hi = _unpack_bf16_pairs(jnp.where(valid, xs_ref[r0:r0 + half, :], jnp.zeros((), U32)))
        return jnp.concatenate([lo.astype(BF16), hi.astype(BF16)], axis=1)

    def hidden(gu):
        return (_silu(gu[:, :EXPERT_FF]) * gu[:, EXPERT_FF:]).astype(BF16)

    @pl.when(n_valid > half)
    def _():
        xa, xb = rows_bf16(0), rows_bf16(half)
        gua = _dot(xa, wgu_s[...])
        gub = _dot(xb, wgu_s[...])
        ya = _dot(hidden(gua), wd_s[...])
        yb = _dot(hidden(gub), wd_s[...])
        ys_ref[0:half, :] = _pack_bf16_pairs(ya)
        ys_ref[half:, :] = _pack_bf16_pairs(yb)

    @pl.when((n_valid > 0) & (n_valid <= half))
    def _():
        ya = _dot(hidden(_dot(rows_bf16(0), wgu_s[...])), wd_s[...])
        ys_ref[0:half, :] = _pack_bf16_pairs(ya)
        ys_ref[half:, :] = jnp.zeros((half, ys_ref.shape[1]), ys_ref.dtype)

    @pl.when(n_valid <= 0)
    def _():
        ys_ref[...] = jnp.zeros_like(ys_ref)


def _experts(block_e, n_valid, xs, wg, wu, wd, layer):
    n_rows, width = xs.shape
    nb = n_rows // ROW_BLOCK
    grid_spec = pltpu.PrefetchScalarGridSpec(
        num_scalar_prefetch=2,
        grid=(nb,),
        in_specs=[pl.BlockSpec((ROW_BLOCK, width), lambda i, be, nv: (i, 0)),
                  pl.BlockSpec((None, None, D_MODEL, EXPERT_FF), lambda i, be, nv: (layer, be[i], 0, 0)),
                  pl.BlockSpec((None, None, D_MODEL, EXPERT_FF), lambda i, be, nv: (layer, be[i], 0, 0)),
                  pl.BlockSpec((None, None, EXPERT_FF, D_MODEL), lambda i, be, nv: (layer, be[i], 0, 0))],
        out_specs=pl.BlockSpec((ROW_BLOCK, width), lambda i, be, nv: (i, 0)),
        scratch_shapes=[pltpu.VMEM((D_MODEL, 2 * EXPERT_FF), BF16),
                        pltpu.VMEM((EXPERT_FF, D_MODEL), BF16),
                        pltpu.SMEM((1,), I32)],
    )
    return pl.pallas_call(
        _experts_kernel,
        grid_spec=grid_spec,
        out_shape=jax.ShapeDtypeStruct((n_rows, width), xs.dtype),
        compiler_params=_params("arbitrary"),
        name="experts",
    )(block_e, n_valid, xs, wg, wu, wd)


def _shared_kernel(x1b_ref, wsg_ref, wsu_ref, wsd_ref, anchor_ref, sh_ref):
    del anchor_ref
    xb = x1b_ref[...]
    hid = (_silu(_dot(xb, wsg_ref[...])) * _dot(xb, wsu_ref[...])).astype(BF16)
    sh_ref[...] = _dot(hid, wsd_ref[...])


def _shared(x1b, wsg, wsu, wsd, anchor, tt):
    t = x1b.shape[0]
    full = lambda i: (0, 0)
    return pl.pallas_call(
        _shared_kernel,
        grid=(t // tt,),
        in_specs=[pl.BlockSpec((tt, D_MODEL), lambda i: (i, 0)),
                  pl.BlockSpec((D_MODEL, SHARED_FF), full),
                  pl.BlockSpec((D_MODEL, SHARED_FF), full),
                  pl.BlockSpec((SHARED_FF, D_MODEL), full),
                  pl.BlockSpec(memory_space=pl.ANY)],
        out_specs=pl.BlockSpec((tt, D_MODEL), lambda i: (i, 0)),
        out_shape=jax.ShapeDtypeStruct((t, D_MODEL), F32),
        compiler_params=_params("parallel"),
        name="shared",
    )(x1b, wsg, wsu, wsd, anchor)


def _combine_kernel(yg_ref, w_ref, x1_ref, sh_ref, vec_ref, x2_ref):
    shared = sh_ref[...]
    w = w_ref[...]
    half = D_MODEL // 2
    acc_lo, acc_hi = shared[:, :half], shared[:, half:]
    for k in range(TOP_K):
        lo, hi = _unpack_bf16_pairs(yg_ref[k])
        acc_lo = acc_lo + w[:, k:k + 1] * lo
        acc_hi = acc_hi + w[:, k:k + 1] * hi
    acc = jnp.concatenate([acc_lo, acc_hi], axis=1)
    x2 = _layer_norm(DEEPNORM_ALPHA * x1_ref[...] + acc, vec_ref[0:1, :], vec_ref[1:2, :])
    x2_ref[...] = x2


def _combine(yg, w_tok, x1, shared, vec, tt):
    t = x1.shape[0]
    full = lambda i: (0, 0)
    return pl.pallas_call(
        _combine_kernel,
        grid=(t // tt,),
        in_specs=[pl.BlockSpec((TOP_K, tt, yg.shape[2]), lambda i: (0, i, 0)),
                  pl.BlockSpec((tt, TOP_K), lambda i: (i, 0)),
                  pl.BlockSpec((tt, D_MODEL), lambda i: (i, 0)),
                  pl.BlockSpec((tt, D_MODEL), lambda i: (i, 0)),
                  pl.BlockSpec((8, D_MODEL), full)],
        out_specs=pl.BlockSpec((tt, D_MODEL), lambda i: (i, 0)),
        out_shape=jax.ShapeDtypeStruct((t, D_MODEL), F32),
        compiler_params=_params("parallel"),
        name="combine",
    )(yg, w_tok, x1, shared, vec)


def _pad_rows(a, rows):
    return jnp.zeros((rows, a.shape[-1]), F32).at[:a.shape[0]].set(a.astype(F32))


def _layer(x, p, stacked, layer, batch, seq):
    t = batch * seq
    gw = GDN_WIDTH
    w_in = p["w_in"]
    c_z, c_a, c_glu = 3 * gw, 4 * gw, 4 * gw + 2 * GDN_HEADS
    c_ga = c_glu + 2 * CONF_CH
    c_gb = c_ga + D_MODEL
    w_main = jnp.concatenate([w_in[:, :c_a], w_in[:, c_glu:]], axis=1).astype(BF16)
    w_a = jnp.zeros((D_MODEL, LANES), F32).at[:, :GDN_HEADS].set(w_in[:, c_a:c_a + GDN_HEADS])
    w_b = jnp.zeros((D_MODEL, LANES), F32).at[:, :GDN_HEADS].set(w_in[:, c_a + GDN_HEADS:c_glu])
    wa_hi, wb_hi = w_a.astype(BF16), w_b.astype(BF16)
    wa_lo = (w_a - wa_hi.astype(F32)).astype(BF16)
    wb_lo = (w_b - wb_hi.astype(F32)).astype(BF16)
    del c_z, c_gb

    tm = min(1024, t)
    proj = _proj(x, w_main, tm, 1024)

    prm = jnp.zeros((8, LANES), F32)
    prm = prm.at[0, :GDN_HEADS].set(p["a_log"]).at[1, :GDN_HEADS].set(p["dt_bias"])
    ong = p["o_norm_g"].reshape(1, HEAD_DIM).astype(F32)
    u, w, qd, m2, eg = _gdn_intra(proj, x, wa_hi, wa_lo, wb_hi, wb_lo, p["conv_qkv"].astype(F32), prm,
                                  seq, min(GDN_TILE, seq))
    o_gdn = _gdn_scan(u, w, qd, m2, eg, proj, ong, batch, seq)

    ts = min(256, seq)
    dww = jnp.broadcast_to(p["dw_w"].astype(F32)[:, None, :], (CONF_KERNEL, 8, CONF_CH))
    vec = _pad_rows(jnp.stack([p["dw_b"], p["cln_g"], p["cln_b"], p["ln1_g"], p["ln1_b"]]), 8)
    x1, x1b, x1p = _mixer(proj, o_gdn, x, p["w_oa"].astype(BF16), p["w_ob"].astype(BF16),
                          p["w_o"].astype(BF16), dww, vec, batch, seq, ts)

    tt_r = min(256, t)
    wr_t = p["w_router"].T.astype(F32)
    wr_hi = wr_t.astype(BF16)
    wr_lo = (wr_t - wr_hi.astype(F32)).astype(BF16)
    bias = jnp.broadcast_to(p["router_bias"].astype(F32)[:, None], (N_EXPERTS, tt_r))
    eidx_t, wts_t, rank_t, cnt = _route(x1, wr_hi, wr_lo, bias, tt_r)

    n_blocks = -(-(t * TOP_K + N_EXPERTS * (ROW_BLOCK - 1)) // ROW_BLOCK)
    n_rows = n_blocks * ROW_BLOCK
    pstart, block_e, n_valid = _plan(cnt, n_blocks)

    dest = _dest(eidx_t, rank_t, pstart, min(2048, t))
    xs = _dispatch(dest, x1p, n_rows)
    ys = _experts(block_e, n_valid, xs, stacked["w_gate_e"], stacked["w_up_e"], stacked["w_down_e"], layer)
    yg = _gather_rows(ys, dest.reshape(TOP_K * t)).reshape(TOP_K, t, ys.shape[1])
    shared = _shared(x1b, p["w_sh_gate"].astype(BF16), p["w_sh_up"].astype(BF16),
                     p["w_sh_down"].astype(BF16), ys, min(512, t))
    vec2 = _pad_rows(jnp.stack([p["ln2_g"], p["ln2_b"]]), 8)
    return _combine(yg, wts_t.T, x1, shared, vec2, min(256, t))


_PARAM_NAMES = ("w_in", "conv_qkv", "a_log", "dt_bias", "o_norm_g", "w_oa", "dw_w", "dw_b", "cln_g",
                "cln_b", "w_ob", "w_o", "ln1_g", "ln1_b", "w_router", "router_bias", "w_gate_e",
                "w_up_e", "w_down_e", "w_sh_gate", "w_sh_up", "w_sh_down", "ln2_g", "ln2_b")


_EXPERT_WEIGHTS = ("w_gate_e", "w_up_e", "w_down_e")


def kernel(x, w_in, conv_qkv, a_log, dt_bias, o_norm_g, w_oa, dw_w, dw_b, cln_g, cln_b, w_ob, w_o,
           ln1_g, ln1_b, w_router, router_bias, w_gate_e, w_up_e, w_down_e, w_sh_gate, w_sh_up,
           w_sh_down, ln2_g, ln2_b):
    stacked = dict(zip(_PARAM_NAMES, (w_in, conv_qkv, a_log, dt_bias, o_norm_g, w_oa, dw_w, dw_b, cln_g,
                                      cln_b, w_ob, w_o, ln1_g, ln1_b, w_router, router_bias, w_gate_e,
                                      w_up_e, w_down_e, w_sh_gate, w_sh_up, w_sh_down, ln2_g, ln2_b)))
    batch, seq, d = x.shape
    assert d == D_MODEL and seq % CHUNK == 0
    xf = x.reshape(batch * seq, d).astype(F32)
    for layer in range(w_in.shape[0]):
        p = {name: arr[layer] for name, arr in stacked.items() if name not in _EXPERT_WEIGHTS}
        xf = _layer(xf, p, stacked, layer, batch, seq)
    return xf.reshape(batch, seq, d).astype(x.dtype)
```

```python
import functools

import jax
import jax.numpy as jnp
import numpy as np
from jax import lax
from jax.experimental import pallas as pl
from jax.experimental.pallas import tpu as pltpu
from jax.experimental.pallas import tpu_sc as plsc

F32 = jnp.float32
BF16 = jnp.bfloat16
I32 = jnp.int32
U32 = jnp.uint32
HI_MASK = np.uint32(0xFFFF0000)

D_MODEL = 1024
GDN_HEADS = 8
HEAD_DIM = 128
GDN_WIDTH = GDN_HEADS * HEAD_DIM
SHORT_CONV = 4
CHUNK = 64
SOLVE_BLOCK = 16
GDN_TILE = 256
SCAN_CHUNKS = 4
CONF_CH = D_MODEL
CONF_KERNEL = 31
CONF_HALO = 32
N_EXPERTS = 64
TOP_K = 8
N_GROUPS = 8
GROUP_SIZE = N_EXPERTS // N_GROUPS
TOPK_GROUPS = 4
EXPERT_FF = 256
SHARED_FF = 256
ROUTED_SCALE = 2.5
DEPTH = 2
DEEPNORM_ALPHA = (2 * DEPTH) ** 0.25
EPS = 1e-6

LANES = 128
PROJ_COLS = 8 * D_MODEL
ROW_BLOCK = 1024
EXPERT_SUB = 256
SC_SCATTER_ROWS = 128
SC_GATHER_ROWS = 64
VMEM_LIMIT = 56 * 1024 * 1024


def _params(*sem):
    return pltpu.CompilerParams(dimension_semantics=sem, vmem_limit_bytes=VMEM_LIMIT)


def _dot(a, b):
    return jnp.dot(a, b, preferred_element_type=F32)


def _dot_nt(a, b):
    return lax.dot_general(a, b, (((1,), (1,)), ((), ())), preferred_element_type=F32)


def _split(a):
    hi = a.astype(BF16)
    lo = (a - hi.astype(F32)).astype(BF16)
    return hi, lo


def _dot3(a, b):
    ah, al = _split(a)
    bh, bl = _split(b)
    return _dot(ah, bh) + _dot(al, bh) + _dot(ah, bl)


def _sigmoid(x):
    return 1.0 / (1.0 + jnp.exp(-x))


def _silu(x):
    return x * _sigmoid(x)


def _layer_norm(y, g, b):
    mu = jnp.mean(y, axis=-1, keepdims=True)
    yc = y - mu
    var = jnp.mean(yc * yc, axis=-1, keepdims=True)
    return yc * lax.rsqrt(var + EPS) * g + b


def _proj_kernel(a_ref, w_ref, o_ref):
    o_ref[...] = _dot(a_ref[...].astype(BF16), w_ref[...])


def _proj(xb, w, tm, tn):
    m, k = xb.shape
    n = w.shape[1]
    return pl.pallas_call(
        _proj_kernel,
        grid=(m // tm, n // tn),
        in_specs=[pl.BlockSpec((tm, k), lambda i, j: (i, 0)),
                  pl.BlockSpec((k, tn), lambda i, j: (0, j))],
        out_specs=pl.BlockSpec((tm, tn), lambda i, j: (i, j)),
        out_shape=jax.ShapeDtypeStruct((m, n), F32),
        compiler_params=_params("parallel", "parallel"),
        name="proj",
    )(xb, w)


def _unit_lower_inverse4(als):
    c = CHUNK
    n = als[0].shape[1]
    row = lax.broadcasted_iota(I32, (c, n), 0)
    col = jnp.bitwise_and(lax.broadcasted_iota(I32, (c, n), 1), c - 1)
    shift = SOLVE_BLOCK.bit_length() - 1
    same = jnp.right_shift(row, shift) == jnp.right_shift(col, shift)
    eye = (row == col).astype(F32)
    cshift = c.bit_length() - 1
    brow = jnp.right_shift(lax.broadcasted_iota(I32, (n, n), 0), cshift)
    bcol = jnp.right_shift(lax.broadcasted_iota(I32, (n, n), 1), cshift)
    on_diag = brow == bcol

    def mm(x, y):
        yb = y.astype(BF16)
        bd = jnp.where(on_diag, jnp.concatenate([yb] * (n // c), axis=0), jnp.zeros((), BF16))
        return _dot(x.astype(BF16), bd)

    a_diag = [jnp.where(same, al, 0.0) for al in als]
    a_off = [al - ad for al, ad in zip(als, a_diag)]
    bp = [-ad for ad in a_diag]
    p = [eye + b for b in bp]
    for _ in range(3):
        bp = [mm(b, b) for b in bp]
        p = [x + mm(x, b) for x, b in zip(p, bp)]
    n1 = [mm(x, ao) for x, ao in zip(p, a_off)]
    n2 = [mm(x, x) for x in n1]
    q = [x + mm(y, x) for x, y in zip(p, n2)]
    return [x - mm(y, x) for x, y in zip(q, n1)]


def _gdn_intra_kernel(qkv_ref, prev_ref, x_ref, wah_ref, wal_ref, wbh_ref, wbl_ref, cw_ref, prm_ref,
                      u_ref, w_ref, qd_ref, m2_ref, eg_ref, xe_ref, *, rt, tiles_per_seq):
    c = CHUNK
    nc = rt // c
    first = (pl.program_id(0) % tiles_per_seq) == 0
    xe_ref[0:8, :] = jnp.where(first, 0.0, prev_ref[...])
    xe_ref[8:8 + rt, :] = qkv_ref[...]

    xh, xl = _split(x_ref[...])

    def proj3(wh_ref, wl_ref):
        wh = wh_ref[...]
        return _dot(xh, wh) + _dot(xl, wh) + _dot(xh, wl_ref[...])

    a_raw = proj3(wah_ref, wal_ref)
    b_raw = proj3(wbh_ref, wbl_ref)
    sp_in = a_raw + prm_ref[1:2, :]
    softplus = jnp.maximum(sp_in, 0.0) + jnp.log(1.0 + jnp.exp(-jnp.abs(sp_in)))
    g = -jnp.exp(prm_ref[0:1, :]) * softplus
    beta = _sigmoid(b_raw)

    cshift = c.bit_length() - 1
    r2 = lax.broadcasted_iota(I32, (rt, rt), 0)
    c2 = lax.broadcasted_iota(I32, (rt, rt), 1)
    same_chunk = jnp.right_shift(r2, cshift) == jnp.right_shift(c2, cshift)
    ltri = jnp.where(r2 >= c2, jnp.where(same_chunk, 1.0, 0.0), 0.0).astype(BF16)
    g_hi = g.astype(BF16)
    g_r = g - g_hi.astype(F32)
    g_mid = g_r.astype(BF16)
    g_lo = (g_r - g_mid.astype(F32)).astype(BF16)
    gc = _dot(ltri, g_hi) + _dot(ltri, g_mid) + _dot(ltri, g_lo)
    gct = gc.T
    egc = jnp.exp(gc)
    gend = jnp.concatenate(
        [jnp.broadcast_to(gc[ci * c + c - 1:ci * c + c, :], (c, LANES)) for ci in range(nc)], axis=0)
    kfac = jnp.exp(gend - gc)
    bege = beta * egc
    for ci in range(nc):
        last = ci * c + c - 1
        eg_ref[ci * GDN_HEADS:(ci + 1) * GDN_HEADS, :] = jnp.broadcast_to(
            jnp.exp(gct[0:GDN_HEADS, last:last + 1]), (GDN_HEADS, LANES))

    lane_t = lax.broadcasted_iota(I32, (rt, LANES), 1) < c
    lane_lo = lax.broadcasted_iota(I32, (c, LANES), 1) < c
    lcol = jnp.bitwise_and(lax.broadcasted_iota(I32, (c, LANES), 1), c - 1)
    rowi = lax.broadcasted_iota(I32, (c, LANES), 0)
    causal = rowi >= lcol
    strict = rowi > lcol

    def conv(base, h):
        lo, hi = base + h * HEAD_DIM, base + (h + 1) * HEAD_DIM
        acc = cw_ref[SHORT_CONV - 1:SHORT_CONV, lo:hi] * xe_ref[8:8 + rt, lo:hi]
        for j in range(SHORT_CONV - 1):
            s0 = 8 - (SHORT_CONV - 1) + j
            acc = acc + cw_ref[j:j + 1, lo:hi] * xe_ref[s0:s0 + rt, lo:hi]
        return _silu(acc)

    a_pairs = [[None] * (GDN_HEADS // 2) for _ in range(nc)]
    rhs_pairs = [[None] * (GDN_HEADS // 2) for _ in range(nc)]
    for p in range(GDN_HEADS // 2):
        ks, kbs, qs, kds, rhss = [], [], [], [], []
        for h in (2 * p, 2 * p + 1):
            q = conv(0, h)
            k = conv(GDN_WIDTH, h)
            v = conv(2 * GDN_WIDTH, h)
            q = q * lax.rsqrt(jnp.sum(q * q, axis=-1, keepdims=True) + EPS) * (HEAD_DIM ** -0.5)
            k = k * lax.rsqrt(jnp.sum(k * k, axis=-1, keepdims=True) + EPS)
            beta_h = beta[:, h:h + 1]
            qd_ref[:, h * HEAD_DIM:(h + 1) * HEAD_DIM] = (q * egc[:, h:h + 1]).astype(BF16)
            ks.append(k)
            kbs.append(k * beta_h)
            qs.append(q)
            kds.append(k * kfac[:, h:h + 1])
            rhss.append(jnp.concatenate([v * beta_h, k * bege[:, h:h + 1]], axis=1))
        h0, h1 = 2 * p, 2 * p + 1
        gch = jnp.where(lane_t, gc[:, h0:h0 + 1], gc[:, h1:h1 + 1])
        for ci in range(nc):
            rows = slice(ci * c, (ci + 1) * c)
            wk = jnp.concatenate([ks[0][rows], ks[1][rows]], axis=0).astype(BF16)
            lhs = jnp.concatenate([kbs[0][rows], qs[0][rows], kbs[1][rows], qs[1][rows]],
                                  axis=0).astype(BF16)
            out = _dot_nt(lhs, wk)
            gcrow = jnp.concatenate([gct[h0:h0 + 1, rows], gct[h1:h1 + 1, rows]], axis=1)
            diff = gch[rows] - gcrow
            decay = jnp.where(causal, jnp.exp(jnp.where(causal, diff, 0.0)), 0.0)
            a_pairs[ci][p] = jnp.where(strict, jnp.where(lane_lo, out[0:c], out[2 * c:3 * c]) * decay, 0.0)
            qk = jnp.where(lane_lo, out[c:2 * c], out[3 * c:4 * c]) * decay
            kdt = jnp.concatenate([kds[0][rows], kds[1][rows]], axis=0).T
            m0 = ci * 3 * c
            m2_ref[m0:m0 + c, p * LANES:(p + 1) * LANES] = qk.astype(BF16)
            m2_ref[m0 + c:m0 + 3 * c, p * LANES:(p + 1) * LANES] = kdt.astype(BF16)
            rhs_pairs[ci][p] = (rhss[0][rows], rhss[1][rows])

    zeros = jnp.zeros((c, 2 * HEAD_DIM), BF16)
    ngrp = GDN_HEADS // 4
    tls = _unit_lower_inverse4(
        [jnp.concatenate([a_pairs[ci][2 * grp], a_pairs[ci][2 * grp + 1]], axis=1)
         for ci in range(nc) for grp in range(ngrp)])
    for ci in range(nc):
        rows = slice(ci * c, (ci + 1) * c)
        for grp in range(ngrp):
            tl = tls[ci * ngrp + grp]
            for j in range(2):
                p = 2 * grp + j
                r0, r1 = rhs_pairs[ci][p]
                bd = jnp.concatenate([jnp.concatenate([r0.astype(BF16), zeros], axis=1),
                                      jnp.concatenate([zeros, r1.astype(BF16)], axis=1)], axis=0)
                sol = _dot(tl[:, j * LANES:(j + 1) * LANES].astype(BF16), bd)
                for i in range(2):
                    h = 2 * p + i
                    lo, hi = h * HEAD_DIM, (h + 1) * HEAD_DIM
                    u_ref[rows, lo:hi] = sol[:, 2 * i * HEAD_DIM:(2 * i + 1) * HEAD_DIM]
                    w_ref[rows, lo:hi] = sol[:, (2 * i + 1) * HEAD_DIM:(2 * i + 2) * HEAD_DIM].astype(BF16)


def _gdn_intra(proj, x, wa_hi, wa_lo, wb_hi, wb_lo, conv_w, prm, seq, rt):
    t = x.shape[0]
    nc = rt // CHUNK
    kern = functools.partial(_gdn_intra_kernel, rt=rt, tiles_per_seq=seq // rt)
    full = lambda i: (0, 0)
    tile = lambda i: (i, 0)
    return pl.pallas_call(
        kern,
        grid=(t // rt,),
        in_specs=[
            pl.BlockSpec((rt, 3 * GDN_WIDTH), tile),
            pl.BlockSpec((8, 3 * GDN_WIDTH), lambda i: (jnp.maximum(i * (rt // 8) - 1, 0), 0)),
            pl.BlockSpec((rt, D_MODEL), tile),
            pl.BlockSpec((D_MODEL, LANES), full),
            pl.BlockSpec((D_MODEL, LANES), full),
            pl.BlockSpec((D_MODEL, LANES), full),
            pl.BlockSpec((D_MODEL, LANES), full),
            pl.BlockSpec((SHORT_CONV, 3 * GDN_WIDTH), full),
            pl.BlockSpec((8, LANES), full),
        ],
        out_specs=[pl.BlockSpec((rt, GDN_WIDTH), tile),
                   pl.BlockSpec((rt, GDN_WIDTH), tile),
                   pl.BlockSpec((rt, GDN_WIDTH), tile),
                   pl.BlockSpec((nc * 3 * CHUNK, GDN_HEADS // 2 * LANES), tile),
                   pl.BlockSpec((nc * GDN_HEADS, LANES), tile)],
        out_shape=[jax.ShapeDtypeStruct((t, GDN_WIDTH), F32),
                   jax.ShapeDtypeStruct((t, GDN_WIDTH), BF16),
                   jax.ShapeDtypeStruct((t, GDN_WIDTH), BF16),
                   jax.ShapeDtypeStruct((t // CHUNK * 3 * CHUNK, GDN_HEADS // 2 * LANES), BF16),
                   jax.ShapeDtypeStruct((t // CHUNK * GDN_HEADS, LANES), F32)],
        scratch_shapes=[pltpu.VMEM((8 + rt, 3 * GDN_WIDTH), F32)],
        compiler_params=_params("parallel"),
        name="gdn_intra",
    )(proj, proj, x, wa_hi, wa_lo, wb_hi, wb_lo, conv_w, prm)


def _gdn_scan_kernel(u_ref, w_ref, qd_ref, m2_ref, eg_ref, z_ref, ong_ref, o_ref, s_ref, *, nck):
    c = CHUNK

    @pl.when(pl.program_id(1) == 0)
    def _():
        s_ref[...] = jnp.zeros_like(s_ref)

    ong = ong_ref[...]
    zeros = jnp.zeros((c, HEAD_DIM), BF16)
    heads = range(GDN_HEADS)
    span = lambda h: slice(h * HEAD_DIM, (h + 1) * HEAD_DIM)
    states = [s_ref[h] for h in heads]
    for ci in range(nck):
        rows = slice(ci * c, (ci + 1) * c)
        rs = [_dot(jnp.concatenate([w_ref[rows, span(h)], qd_ref[rows, span(h)]], axis=0),
                   states[h].astype(BF16)) for h in heads]
        v_new = [(u_ref[rows, span(h)] - rs[h][:c]).astype(BF16) for h in heads]
        r2s = []
        for p in range(GDN_HEADS // 2):
            bd = jnp.concatenate([jnp.concatenate([v_new[2 * p], zeros], axis=1),
                                  jnp.concatenate([zeros, v_new[2 * p + 1]], axis=1)], axis=0)
            r2s.append(_dot(m2_ref[ci * 3 * c:(ci + 1) * 3 * c, p * LANES:(p + 1) * LANES], bd))
        for h in heads:
            half = span(h % 2)
            r2 = r2s[h // 2]
            states[h] = states[h] * eg_ref[ci * GDN_HEADS + h:ci * GDN_HEADS + h + 1, :] + r2[c:, half]
            o = rs[h][c:] + r2[:c, half]
            o = o * lax.rsqrt(jnp.mean(o * o, axis=-1, keepdims=True) + EPS) * ong
            o = o * _silu(z_ref[rows, span(h)])
            o_ref[rows, span(h)] = o.astype(o_ref.dtype)
    for h in heads:
        s_ref[h] = states[h]


def _gdn_scan(u, w, qd, m2, eg, proj, ong, batch, seq):
    t = batch * seq
    nck = min(SCAN_CHUNKS, seq // CHUNK)
    c = nck * CHUNK
    nch = seq // c
    blk = lambda b, n: (b * nch + n, 0)
    return pl.pallas_call(
        functools.partial(_gdn_scan_kernel, nck=nck),
        grid=(batch, nch),
        in_specs=[
            pl.BlockSpec((c, GDN_WIDTH), blk),
            pl.BlockSpec((c, GDN_WIDTH), blk),
            pl.BlockSpec((c, GDN_WIDTH), blk),
            pl.BlockSpec((3 * c, GDN_HEADS // 2 * LANES), blk),
            pl.BlockSpec((nck * GDN_HEADS, LANES), blk),
            pl.BlockSpec((c, GDN_WIDTH), lambda b, n: (b * nch + n, 3)),
            pl.BlockSpec((1, HEAD_DIM), lambda b, n: (0, 0)),
        ],
        out_specs=pl.BlockSpec((c, GDN_WIDTH), blk),
        out_shape=jax.ShapeDtypeStruct((t, GDN_WIDTH), BF16),
        scratch_shapes=[pltpu.VMEM((GDN_HEADS, HEAD_DIM, HEAD_DIM), F32)],
        compiler_params=_params("parallel", "arbitrary"),
        name="gdn_scan",
    )(u, w, qd, m2, eg, proj, ong)


def _pack_bf16_pairs(y):
    n = y.shape[1] // 2
    yb = y.astype(BF16).astype(F32)
    lo = lax.bitcast_convert_type(yb[:, :n], U32)
    hi = lax.bitcast_convert_type(yb[:, n:], U32)
    return jnp.bitwise_or(jnp.right_shift(lo, 16), jnp.bitwise_and(hi, HI_MASK))


def _unpack_bf16_pairs(w):
    lo = lax.bitcast_convert_type(jnp.left_shift(w, 16), F32)
    hi = lax.bitcast_convert_type(jnp.bitwise_and(w, HI_MASK), F32)
    return lo, hi


def _mixer_kernel(glu_ref, ga_ref, gb_ref, o_ref, x_ref, woa_ref, wob_ref, wo_ref, dww_ref, vec_ref,
                  x1_ref, x1b_ref, x1p_ref, ubuf_ref, sh_ref, conv_ref, *, ts, rc):
    halo = CONF_HALO

    @pl.when(pl.program_id(1) == 0)
    def _():
        ubuf_ref[0:halo, :] = jnp.zeros((halo, CONF_CH), F32)

    @pl.when(pl.program_id(1) != 0)
    def _():
        ubuf_ref[0:halo, :] = ubuf_ref[ts:ts + halo, :]

    ubuf_ref[halo:halo + ts, :] = glu_ref[:, :CONF_CH] * _sigmoid(glu_ref[:, CONF_CH:])

    dw_b = vec_ref[0:1, :]
    cln_g = vec_ref[1:2, :]
    cln_b = vec_ref[2:3, :]
    ln1_g = vec_ref[3:4, :]
    ln1_b = vec_ref[4:5, :]

    span = ts + halo - 8
    for s in range(1, 8):
        sh_ref[s - 1] = ubuf_ref[s:s + span, :]

    def tap_rows(j, r0):
        o = halo - (CONF_KERNEL - 1) + j
        q, s = o // 8, o % 8
        if s == 0:
            return ubuf_ref[r0 + o:r0 + o + rc, :]
        return sh_ref[s - 1, r0 + 8 * q:r0 + 8 * q + rc, :]

    hs = ts // 2
    for h0 in range(0, ts, hs):
        rows = slice(h0, h0 + hs)
        gated_a = _sigmoid(ga_ref[rows, :]) * _dot(o_ref[rows, :], woa_ref[...])
        for r0 in range(h0, h0 + hs, rc):
            tap_w = lambda j: jnp.concatenate([dww_ref[j]] * (rc // 8), axis=0)
            acc = tap_w(0) * tap_rows(0, r0)
            for j in range(1, CONF_KERNEL):
                acc = acc + tap_w(j) * tap_rows(j, r0)
            conv_ref[r0:r0 + rc, :] = acc
        uc = _silu(_layer_norm(conv_ref[rows, :] + dw_b, cln_g, cln_b))
        branch_b = _dot(uc.astype(BF16), wob_ref[...])
        hmix = gated_a + _sigmoid(gb_ref[rows, :]) * branch_b
        mix = _dot(hmix.astype(BF16), wo_ref[...])
        x1 = _layer_norm(DEEPNORM_ALPHA * x_ref[rows, :] + mix, ln1_g, ln1_b)
        x1_ref[rows, :] = x1
        x1b_ref[rows, :] = x1.astype(BF16)
        x1p_ref[rows, :] = _pack_bf16_pairs(x1)


def _mixer(proj, o_gdn, x, woa, wob, wo, dww, vec, batch, seq, ts):
    t = batch * seq
    nt = seq // ts
    rows = lambda b, n: b * nt + n
    full = lambda b, n: (0, 0)
    kern = functools.partial(_mixer_kernel, ts=ts, rc=32)
    return pl.pallas_call(
        kern,
        grid=(batch, nt),
        in_specs=[
            pl.BlockSpec((ts, 2 * CONF_CH), lambda b, n: (rows(b, n), 2)),
            pl.BlockSpec((ts, D_MODEL), lambda b, n: (rows(b, n), 6)),
            pl.BlockSpec((ts, D_MODEL), lambda b, n: (rows(b, n), 7)),
            pl.BlockSpec((ts, GDN_WIDTH), lambda b, n: (rows(b, n), 0)),
            pl.BlockSpec((ts, D_MODEL), lambda b, n: (rows(b, n), 0)),
            pl.BlockSpec((GDN_WIDTH, D_MODEL), full),
            pl.BlockSpec((CONF_CH, D_MODEL), full),
            pl.BlockSpec((D_MODEL, D_MODEL), full),
            pl.BlockSpec((CONF_KERNEL, 8, CONF_CH), lambda b, n: (0, 0, 0)),
            pl.BlockSpec((8, D_MODEL), full),
        ],
        out_specs=[pl.BlockSpec((ts, D_MODEL), lambda b, n: (rows(b, n), 0)),
                   pl.BlockSpec((ts, D_MODEL), lambda b, n: (rows(b, n), 0)),
                   pl.BlockSpec((ts, D_MODEL // 2), lambda b, n: (rows(b, n), 0))],
        out_shape=[jax.ShapeDtypeStruct((t, D_MODEL), F32),
                   jax.ShapeDtypeStruct((t, D_MODEL), BF16),
                   jax.ShapeDtypeStruct((t, D_MODEL // 2), U32)],
        scratch_shapes=[pltpu.VMEM((CONF_HALO + ts, CONF_CH), F32),
                        pltpu.VMEM((7, ts + CONF_HALO - 8, CONF_CH), F32),
                        pltpu.VMEM((ts, CONF_CH), F32)],
        compiler_params=_params("parallel", "arbitrary"),
        name="mixer",
    )(proj, proj, proj, o_gdn, x, woa, wob, wo, dww, vec)


def _route_kernel(x_ref, wrh_ref, wrl_ref, bias_ref, eidx_ref, wts_ref, rank_ref, cnt_ref, carry_ref, *, tt):
    @pl.when(pl.program_id(0) == 0)
    def _():
        carry_ref[...] = jnp.zeros_like(carry_ref)

    xh, xl = _split(x_ref[...])
    wrh = wrh_ref[...]
    logits = _dot_nt(wrh, xh) + _dot_nt(wrh, xl) + _dot_nt(wrl_ref[...], xh)
    s = _sigmoid(logits)
    biased = s + bias_ref[...]

    sub = lax.broadcasted_iota(I32, (GROUP_SIZE, tt), 0)
    groups = [biased[g * GROUP_SIZE:(g + 1) * GROUP_SIZE, :] for g in range(N_GROUPS)]
    gs = []
    for bg in groups:
        m1 = jnp.max(bg, axis=0, keepdims=True)
        first = jnp.min(jnp.where(bg == m1, sub, GROUP_SIZE), axis=0, keepdims=True)
        m2 = jnp.max(jnp.where(sub == first, -jnp.inf, bg), axis=0, keepdims=True)
        gs.append(m1 + m2)

    masked_parts = []
    for g in range(N_GROUPS):
        beaten = jnp.zeros((1, tt), I32)
        for o in range(N_GROUPS):
            if o == g:
                continue
            wins = (gs[o] >= gs[g]) if o < g else (gs[o] > gs[g])
            beaten = beaten + wins.astype(I32)
        keep = jnp.broadcast_to(beaten < TOPK_GROUPS, (GROUP_SIZE, tt))
        masked_parts.append(jnp.where(keep, groups[g], -jnp.inf))
    masked = jnp.concatenate(masked_parts, axis=0)

    eiota = lax.broadcasted_iota(I32, (N_EXPERTS, tt), 0)
    sel_all = jnp.zeros((N_EXPERTS, tt), F32)
    picks = []
    for _ in range(TOP_K):
        m = jnp.max(masked, axis=0, keepdims=True)
        idx = jnp.min(jnp.where(masked == m, eiota, N_EXPERTS), axis=0, keepdims=True)
        onehot = eiota == idx
        picks.append((idx, onehot))
        sel_all = jnp.where(onehot, 1.0, sel_all)
        masked = jnp.where(onehot, -jnp.inf, masked)

    tr = lax.broadcasted_iota(I32, (tt, tt), 0)
    tc = lax.broadcasted_iota(I32, (tt, tt), 1)
    before = (tr < tc).astype(BF16)
    sel_b = sel_all.astype(BF16)
    carry = carry_ref[...]
    rank_all = _dot(sel_b, before) + carry[:, 0:1]
    carry_new = carry + _dot(sel_b, jnp.ones((tt, LANES), BF16))
    carry_ref[...] = carry_new
    cnt_ref[...] = carry_new

    s_sel = [jnp.sum(jnp.where(oh, s, 0.0), axis=0, keepdims=True) for _, oh in picks]
    total = s_sel[0]
    for v in s_sel[1:]:
        total = total + v
    for k, (idx, oh) in enumerate(picks):
        eidx_ref[k:k + 1, :] = idx
        wts_ref[k:k + 1, :] = s_sel[k] / total * ROUTED_SCALE
        rank_ref[k:k + 1, :] = jnp.sum(jnp.where(oh, rank_all, 0.0), axis=0, keepdims=True).astype(I32)


def _route(x1, wr_hi, wr_lo, bias, tt):
    t = x1.shape[0]
    kern = functools.partial(_route_kernel, tt=tt)
    return pl.pallas_call(
        kern,
        grid=(t // tt,),
        in_specs=[pl.BlockSpec((tt, D_MODEL), lambda i: (i, 0)),
                  pl.BlockSpec((N_EXPERTS, D_MODEL), lambda i: (0, 0)),
                  pl.BlockSpec((N_EXPERTS, D_MODEL), lambda i: (0, 0)),
                  pl.BlockSpec((N_EXPERTS, tt), lambda i: (0, 0))],
        out_specs=[pl.BlockSpec((TOP_K, tt), lambda i: (0, i)),
                   pl.BlockSpec((TOP_K, tt), lambda i: (0, i)),
                   pl.BlockSpec((TOP_K, tt), lambda i: (0, i)),
                   pl.BlockSpec((N_EXPERTS, LANES), lambda i: (0, 0))],
        out_shape=[jax.ShapeDtypeStruct((TOP_K, t), I32),
                   jax.ShapeDtypeStruct((TOP_K, t), F32),
                   jax.ShapeDtypeStruct((TOP_K, t), I32),
                   jax.ShapeDtypeStruct((N_EXPERTS, LANES), F32)],
        scratch_shapes=[pltpu.VMEM((N_EXPERTS, LANES), F32)],
        compiler_params=_params("arbitrary"),
        name="route",
    )(x1, wr_hi, wr_lo, bias)


def _plan_kernel(cnt_ref, pstart_ref, plan_ref):
    e, nb = N_EXPERTS, plan_ref.shape[1]
    counts = cnt_ref[...]
    nblk = jnp.floor((counts + (ROW_BLOCK - 1)) * (1.0 / ROW_BLOCK))
    hi = jnp.floor(nblk * (1.0 / 256.0))
    lo = nblk - 256.0 * hi
    r = lax.broadcasted_iota(I32, (e, e), 0)
    c = lax.broadcasted_iota(I32, (e, e), 1)
    ltri = (r >= c).astype(BF16)
    bend = 256.0 * _dot(ltri, hi.astype(BF16)) + _dot(ltri, lo.astype(BF16))
    pend = bend * ROW_BLOCK
    pstart = pend - nblk * ROW_BLOCK
    pstart_ref[...] = pstart.astype(I32)

    bs = (lax.broadcasted_iota(I32, (e, nb), 1) * ROW_BLOCK).astype(F32)
    pend_b = jnp.broadcast_to(pend[:, 0:1], (e, nb))
    pstart_b = jnp.broadcast_to(pstart[:, 0:1], (e, nb))
    used_b = jnp.broadcast_to((pstart + counts)[:, 0:1], (e, nb))
    owner = jnp.sum(jnp.where(pend_b <= bs, 1.0, 0.0), axis=0, keepdims=True)
    inside = jnp.where(pstart_b <= bs, jnp.where(bs < pend_b, 1.0, 0.0), 0.0)
    real = jnp.sum(inside * jnp.clip(used_b - bs, 0.0, float(ROW_BLOCK)), axis=0, keepdims=True)
    plan_ref[0:1, :] = jnp.minimum(owner, float(e - 1)).astype(I32)
    plan_ref[1:2, :] = real.astype(I32)
    plan_ref[2:8, :] = jnp.zeros((6, nb), I32)


def _plan(cnt, n_blocks):
    nb = -(-n_blocks // LANES) * LANES
    pstart, plan = pl.pallas_call(
        _plan_kernel,
        out_shape=[jax.ShapeDtypeStruct((N_EXPERTS, LANES), I32),
                   jax.ShapeDtypeStruct((8, nb), I32)],
        name="plan",
    )(cnt)
    return pstart[:, 0], plan[0, :n_blocks], plan[1, :n_blocks]


def _dest_kernel(eidx_ref, rank_ref, pstart_ref, dest_ref):
    eidx = eidx_ref[...]
    acc = rank_ref[...]
    for e in range(N_EXPERTS):
        acc = acc + jnp.where(eidx == e, pstart_ref[e], 0)
    dest_ref[...] = acc


def _dest(eidx_t, rank_t, pstart, tt):
    t = eidx_t.shape[1]
    return pl.pallas_call(
        _dest_kernel,
        grid=(t // tt,),
        in_specs=[pl.BlockSpec((TOP_K, tt), lambda i: (0, i)),
                  pl.BlockSpec((TOP_K, tt), lambda i: (0, i)),
                  pl.BlockSpec(memory_space=pltpu.SMEM)],
        out_specs=pl.BlockSpec((TOP_K, tt), lambda i: (0, i)),
        out_shape=jax.ShapeDtypeStruct((TOP_K, t), I32),
        compiler_params=_params("parallel"),
        name="dest",
    )(eidx_t, rank_t, pstart)


def _dispatch(dest_kt, x1p, n_rows):
    t, width = x1p.shape
    info = plsc.get_sparse_core_info()
    nc, nw = info.num_cores, info.num_cores * info.num_subcores
    chunk = SC_SCATTER_ROWS
    per_w = t // nw
    n_chunks = per_w // chunk
    assert per_w % chunk == 0
    idx = dest_kt.reshape(TOP_K, nw, n_chunks, chunk).transpose(1, 2, 0, 3).reshape(nw, n_chunks * TOP_K, chunk)
    mesh = plsc.VectorSubcoreMesh(core_axis_name="c", subcore_axis_name="s")

    @functools.partial(
        pl.kernel, mesh=mesh, name="dispatch",
        out_type=jax.ShapeDtypeStruct((n_rows, width), x1p.dtype),
        scratch_types=[pltpu.VMEM((n_chunks * TOP_K, chunk), I32),
                       pltpu.VMEM((chunk, width), x1p.dtype),
                       pltpu.SemaphoreType.DMA])
    def scatter(x_hbm, idx_hbm, xs_hbm, idx_v, rows_v, sem):
        wid = lax.axis_index("s") * nc + lax.axis_index("c")
        base = wid * per_w
        pltpu.sync_copy(idx_hbm.at[wid], idx_v)
        for j in range(n_chunks):
            pltpu.sync_copy(x_hbm.at[pl.ds(base + j * chunk, chunk)], rows_v)
            copies = [pltpu.make_async_copy(rows_v, xs_hbm.at[idx_v.at[j * TOP_K + k]], sem)
                      for k in range(TOP_K)]
            for cp in copies:
                cp.start()
            for cp in copies:
                cp.wait()

    return scatter(x1p, idx)


def _gather_rows(table, idx):
    n = idx.shape[0]
    width = table.shape[1]
    info = plsc.get_sparse_core_info()
    nc, nw = info.num_cores, info.num_cores * info.num_subcores
    chunk = SC_GATHER_ROWS
    per_w = n // nw
    n_chunks = per_w // chunk
    assert per_w % (2 * chunk) == 0
    mesh = plsc.VectorSubcoreMesh(core_axis_name="c", subcore_axis_name="s")

    @functools.partial(
        pl.kernel, mesh=mesh, name="gather_rows",
        out_type=jax.ShapeDtypeStruct((n, width), table.dtype),
        scratch_types=[pltpu.VMEM((n_chunks, chunk), I32),
                       pltpu.VMEM((2, chunk, width), table.dtype),
                       pltpu.SemaphoreType.DMA((2,)),
                       pltpu.SemaphoreType.DMA((2,))])
    def gather(table_hbm, idx_hbm, out_hbm, idx_v, rows_v, gsem, osem):
        wid = lax.axis_index("s") * nc + lax.axis_index("c")
        base = wid * per_w
        pltpu.sync_copy(idx_hbm.at[wid], idx_v)

        def fetch(j, b):
            return pltpu.make_async_copy(table_hbm.at[idx_v.at[j]], rows_v.at[b], gsem.at[b])

        def put(j, b):
            return pltpu.make_async_copy(rows_v.at[b], out_hbm.at[pl.ds(base + j * chunk, chunk)], osem.at[b])

        fetch(0, 0).start()

        @pl.loop(0, n_chunks, step=2)
        def _(j0):
            for b in range(2):
                j = j0 + b
                fetch(j, b).wait()

                @pl.when(j + 1 < n_chunks)
                def _():
                    @pl.when(j >= 1)
                    def _():
                        put(j - 1, 1 - b).wait()

                    fetch(j + 1, 1 - b).start()

                put(j, b).start()

        put(n_chunks - 2, 0).wait()
        put(n_chunks - 1, 1).wait()

    return gather(table, idx.reshape(nw, n_chunks, chunk))


def _experts_kernel(be_ref, nv_ref, xs_ref, wg_ref, wu_ref, wd_ref, ys_ref, wgu_s, wd_s, cur_ref):
    i = pl.program_id(0)
    n_valid = nv_ref[i]
    half = EXPERT_SUB

    @pl.when(i == 0)
    def _():
        cur_ref[0] = -1

    @pl.when((n_valid > 0) & (cur_ref[0] != be_ref[i]))
    def _():
        wgu_s[:, :EXPERT_FF] = wg_ref[...].astype(BF16)
        wgu_s[:, EXPERT_FF:] = wu_ref[...].astype(BF16)
        wd_s[...] = wd_ref[...].astype(BF16)
        cur_ref[0] = be_ref[i]

    def rows_bf16(r0):
        valid = lax.broadcasted_iota(I32, (half, xs_ref.shape[1]), 0) + r0 < n_valid
        lo, hi = _unpack_bf16_pairs(jnp.where(valid, xs_ref[r0:r0 + half, :], jnp.zeros((), U32)))
        return jnp.concatenate([lo.astype(BF16), hi.astype(BF16)], axis=1)

    def hidden(gu):
        return (_silu(gu[:, :EXPERT_FF]) * gu[:, EXPERT_FF:]).astype(BF16)

    zeros = jnp.zeros((half, ys_ref.shape[1]), ys_ref.dtype)
    for r0 in range(0, ROW_BLOCK, 2 * half):
        r1 = r0 + half

        @pl.when(n_valid > r1)
        def _():
            xa, xb = rows_bf16(r0), rows_bf16(r1)
            gua = _dot(xa, wgu_s[...])
            gub = _dot(xb, wgu_s[...])
            ya = _dot(hidden(gua), wd_s[...])
            yb = _dot(hidden(gub), wd_s[...])
            ys_ref[r0:r1, :] = _pack_bf16_pairs(ya)
            ys_ref[r1:r1 + half, :] = _pack_bf16_pairs(yb)

        @pl.when((n_valid > r0) & (n_valid <= r1))
        def _():
            ya = _dot(hidden(_dot(rows_bf16(r0), wgu_s[...])), wd_s[...])
            ys_ref[r0:r1, :] = _pack_bf16_pairs(ya)
            ys_ref[r1:r1 + half, :] = zeros

        @pl.when(n_valid <= r0)
        def _():
            ys_ref[r0:r1, :] = zeros
            ys_ref[r1:r1 + half, :] = zeros


def _experts(block_e, n_valid, xs, wg, wu, wd, layer):
    n_rows, width = xs.shape
    nb = n_rows // ROW_BLOCK
    grid_spec = pltpu.PrefetchScalarGridSpec(
        num_scalar_prefetch=2,
        grid=(nb,),
        in_specs=[pl.BlockSpec((ROW_BLOCK, width), lambda i, be, nv: (i, 0)),
                  pl.BlockSpec((None, None, D_MODEL, EXPERT_FF), lambda i, be, nv: (layer, be[i], 0, 0)),
                  pl.BlockSpec((None, None, D_MODEL, EXPERT_FF), lambda i, be, nv: (layer, be[i], 0, 0)),
                  pl.BlockSpec((None, None, EXPERT_FF, D_MODEL), lambda i, be, nv: (layer, be[i], 0, 0))],
        out_specs=pl.BlockSpec((ROW_BLOCK, width), lambda i, be, nv: (i, 0)),
        scratch_shapes=[pltpu.VMEM((D_MODEL, 2 * EXPERT_FF), BF16),
                        pltpu.VMEM((EXPERT_FF, D_MODEL), BF16),
                        pltpu.SMEM((1,), I32)],
    )
    return pl.pallas_call(
        _experts_kernel,
        grid_spec=grid_spec,
        out_shape=jax.ShapeDtypeStruct((n_rows, width), xs.dtype),
        compiler_params=_params("arbitrary"),
        name="experts",
    )(block_e, n_valid, xs, wg, wu, wd)


def _shared_kernel(x1b_ref, wsg_ref, wsu_ref, wsd_ref, anchor_ref, sh_ref):
    del anchor_ref
    xb = x1b_ref[...]
    hid = (_silu(_dot(xb, wsg_ref[...])) * _dot(xb, wsu_ref[...])).astype(BF16)
    sh_ref[...] = _dot(hid, wsd_ref[...])


def _shared(x1b, wsg, wsu, wsd, anchor, tt, row0, nrows):
    full = lambda i: (0, 0)
    first = row0 // tt
    return pl.pallas_call(
        _shared_kernel,
        grid=(nrows // tt,),
        in_specs=[pl.BlockSpec((tt, D_MODEL), lambda i: (first + i, 0)),
                  pl.BlockSpec((D_MODEL, SHARED_FF), full),
                  pl.BlockSpec((D_MODEL, SHARED_FF), full),
                  pl.BlockSpec((SHARED_FF, D_MODEL), full),
                  pl.BlockSpec(memory_space=pl.ANY)],
        out_specs=pl.BlockSpec((tt, D_MODEL), lambda i: (i, 0)),
        out_shape=jax.ShapeDtypeStruct((nrows, D_MODEL), F32),
        compiler_params=_params("parallel"),
        name="shared",
    )(x1b, wsg, wsu, wsd, anchor)


def _combine_kernel(yg_ref, w_ref, x1_ref, sha_ref, shb_ref, vec_ref, x2_ref, *, half_steps):
    in_first = pl.program_id(0) < half_steps
    shared = jnp.where(in_first, sha_ref[...], shb_ref[...])
    w = w_ref[...]
    half = D_MODEL // 2
    acc_lo, acc_hi = shared[:, :half], shared[:, half:]
    for k in range(TOP_K):
        lo, hi = _unpack_bf16_pairs(yg_ref[k])
        acc_lo = acc_lo + w[:, k:k + 1] * lo
        acc_hi = acc_hi + w[:, k:k + 1] * hi
    acc = jnp.concatenate([acc_lo, acc_hi], axis=1)
    x2 = _layer_norm(DEEPNORM_ALPHA * x1_ref[...] + acc, vec_ref[0:1, :], vec_ref[1:2, :])
    x2_ref[...] = x2


def _combine(yg, w_tok, x1, shared_a, shared_b, vec, tt):
    t = x1.shape[0]
    half_steps = shared_a.shape[0] // tt
    full = lambda i: (0, 0)
    return pl.pallas_call(
        functools.partial(_combine_kernel, half_steps=half_steps),
        grid=(t // tt,),
        in_specs=[pl.BlockSpec((TOP_K, tt, yg.shape[2]), lambda i: (0, i, 0)),
                  pl.BlockSpec((tt, TOP_K), lambda i: (i, 0)),
                  pl.BlockSpec((tt, D_MODEL), lambda i: (i, 0)),
                  pl.BlockSpec((tt, D_MODEL), lambda i: (jnp.minimum(i, half_steps - 1), 0)),
                  pl.BlockSpec((tt, D_MODEL), lambda i: (jnp.maximum(i - half_steps, 0), 0)),
                  pl.BlockSpec((8, D_MODEL), full)],
        out_specs=pl.BlockSpec((tt, D_MODEL), lambda i: (i, 0)),
        out_shape=jax.ShapeDtypeStruct((t, D_MODEL), F32),
        compiler_params=_params("parallel"),
        name="combine",
    )(yg, w_tok, x1, shared_a, shared_b, vec)


def _pad_rows(a, rows):
    return jnp.zeros((rows, a.shape[-1]), F32).at[:a.shape[0]].set(a.astype(F32))


def _layer(x, p, stacked, layer, batch, seq):
    t = batch * seq
    gw = GDN_WIDTH
    w_in = p["w_in"]
    c_z, c_a, c_glu = 3 * gw, 4 * gw, 4 * gw + 2 * GDN_HEADS
    c_ga = c_glu + 2 * CONF_CH
    c_gb = c_ga + D_MODEL
    w_main = jnp.concatenate([w_in[:, :c_a], w_in[:, c_glu:]], axis=1).astype(BF16)
    w_a = jnp.zeros((D_MODEL, LANES), F32).at[:, :GDN_HEADS].set(w_in[:, c_a:c_a + GDN_HEADS])
    w_b = jnp.zeros((D_MODEL, LANES), F32).at[:, :GDN_HEADS].set(w_in[:, c_a + GDN_HEADS:c_glu])
    wa_hi, wb_hi = w_a.astype(BF16), w_b.astype(BF16)
    wa_lo = (w_a - wa_hi.astype(F32)).astype(BF16)
    wb_lo = (w_b - wb_hi.astype(F32)).astype(BF16)
    del c_z, c_gb

    tm = min(1024, t)
    proj = _proj(x, w_main, tm, 2048)

    prm = jnp.zeros((8, LANES), F32)
    prm = prm.at[0, :GDN_HEADS].set(p["a_log"]).at[1, :GDN_HEADS].set(p["dt_bias"])
    ong = p["o_norm_g"].reshape(1, HEAD_DIM).astype(F32)
    u, w, qd, m2, eg = _gdn_intra(proj, x, wa_hi, wa_lo, wb_hi, wb_lo, p["conv_qkv"].astype(F32), prm,
                                  seq, min(GDN_TILE, seq))
    o_gdn = _gdn_scan(u, w, qd, m2, eg, proj, ong, batch, seq)

    ts = min(256, seq)
    dww = jnp.broadcast_to(p["dw_w"].astype(F32)[:, None, :], (CONF_KERNEL, 8, CONF_CH))
    vec = _pad_rows(jnp.stack([p["dw_b"], p["cln_g"], p["cln_b"], p["ln1_g"], p["ln1_b"]]), 8)
    x1, x1b, x1p = _mixer(proj, o_gdn, x, p["w_oa"].astype(BF16), p["w_ob"].astype(BF16),
                          p["w_o"].astype(BF16), dww, vec, batch, seq, ts)

    tt_r = min(256, t)
    wr_t = p["w_router"].T.astype(F32)
    wr_hi = wr_t.astype(BF16)
    wr_lo = (wr_t - wr_hi.astype(F32)).astype(BF16)
    bias = jnp.broadcast_to(p["router_bias"].astype(F32)[:, None], (N_EXPERTS, tt_r))
    eidx_t, wts_t, rank_t, cnt = _route(x1, wr_hi, wr_lo, bias, tt_r)

    n_blocks = -(-(t * TOP_K + N_EXPERTS * (ROW_BLOCK - 1)) // ROW_BLOCK)
    n_rows = n_blocks * ROW_BLOCK
    pstart, block_e, n_valid = _plan(cnt, n_blocks)

    dest = _dest(eidx_t, rank_t, pstart, min(2048, t))
    xs = _dispatch(dest, x1p, n_rows)
    ys = _experts(block_e, n_valid, xs, stacked["w_gate_e"], stacked["w_up_e"], stacked["w_down_e"], layer)
    yg = _gather_rows(ys, dest.reshape(TOP_K * t)).reshape(TOP_K, t, ys.shape[1])
    sh_w = (p["w_sh_gate"].astype(BF16), p["w_sh_up"].astype(BF16), p["w_sh_down"].astype(BF16))
    tt_s = min(512, t // 2)
    shared_a = _shared(x1b, *sh_w, dest, tt_s, 0, t // 2)
    shared_b = _shared(x1b, *sh_w, ys, tt_s, t // 2, t // 2)
    vec2 = _pad_rows(jnp.stack([p["ln2_g"], p["ln2_b"]]), 8)
    return _combine(yg, wts_t.T, x1, shared_a, shared_b, vec2, min(256, t // 2))


_PARAM_NAMES = ("w_in", "conv_qkv", "a_log", "dt_bias", "o_norm_g", "w_oa", "dw_w", "dw_b", "cln_g",
                "cln_b", "w_ob", "w_o", "ln1_g", "ln1_b", "w_router", "router_bias", "w_gate_e",
                "w_up_e", "w_down_e", "w_sh_gate", "w_sh_up", "w_sh_down", "ln2_g", "ln2_b")


_EXPERT_WEIGHTS = ("w_gate_e", "w_up_e", "w_down_e")


def kernel(x, w_in, conv_qkv, a_log, dt_bias, o_norm_g, w_oa, dw_w, dw_b, cln_g, cln_b, w_ob, w_o,
           ln1_g, ln1_b, w_router, router_bias, w_gate_e, w_up_e, w_down_e, w_sh_gate, w_sh_up,
           w_sh_down, ln2_g, ln2_b):
    stacked = dict(zip(_PARAM_NAMES, (w_in, conv_qkv, a_log, dt_bias, o_norm_g, w_oa, dw_w, dw_b, cln_g,
                                      cln_b, w_ob, w_o, ln1_g, ln1_b, w_router, router_bias, w_gate_e,
                                      w_up_e, w_down_e, w_sh_gate, w_sh_up, w_sh_down, ln2_g, ln2_b)))
    batch, seq, d = x.shape
    assert d == D_MODEL and seq % CHUNK == 0
    xf = x.reshape(batch * seq, d).astype(F32)
    for layer in range(w_in.shape[0]):
        p = {name: arr[layer] for name, arr in stacked.items() if name not in _EXPERT_WEIGHTS}
        xf = _layer(xf, p, stacked, layer, batch, seq)
    return xf.reshape(batch, seq, d).astype(x.dtype)
```

```python
import functools

import jax
import jax.numpy as jnp
import numpy as np
from jax import lax
from jax.experimental import pallas as pl
from jax.experimental.pallas import tpu as pltpu
from jax.experimental.pallas import tpu_sc as plsc

F32 = jnp.float32
BF16 = jnp.bfloat16
I32 = jnp.int32
U32 = jnp.uint32
HI_MASK = np.uint32(0xFFFF0000)

D_MODEL = 1024
GDN_HEADS = 8
HEAD_DIM = 128
GDN_WIDTH = GDN_HEADS * HEAD_DIM
SHORT_CONV = 4
CHUNK = 64
SOLVE_BLOCK = 16
GDN_TILE = 256
SCAN_CHUNKS = 4
CONF_CH = D_MODEL
CONF_KERNEL = 31
CONF_HALO = 32
N_EXPERTS = 64
TOP_K = 8
N_GROUPS = 8
GROUP_SIZE = N_EXPERTS // N_GROUPS
TOPK_GROUPS = 4
EXPERT_FF = 256
SHARED_FF = 256
ROUTED_SCALE = 2.5
DEPTH = 2
DEEPNORM_ALPHA = (2 * DEPTH) ** 0.25
EPS = 1e-6

LANES = 128
PROJ_COLS = 8 * D_MODEL
ROW_BLOCK = 512
EXPERT_SUB = ROW_BLOCK // 2
EXPERT_IN_SLOTS = 3
SC_SCATTER_ROWS = 128
SC_GATHER_ROWS = 64
VMEM_LIMIT = 56 * 1024 * 1024


def _params(*sem):
    return pltpu.CompilerParams(dimension_semantics=sem, vmem_limit_bytes=VMEM_LIMIT)


def _dot(a, b):
    return jnp.dot(a, b, preferred_element_type=F32)


def _dot_nt(a, b):
    return lax.dot_general(a, b, (((1,), (1,)), ((), ())), preferred_element_type=F32)


def _split(a):
    hi = a.astype(BF16)
    lo = (a - hi.astype(F32)).astype(BF16)
    return hi, lo


def _dot3(a, b):
    ah, al = _split(a)
    bh, bl = _split(b)
    return _dot(ah, bh) + _dot(al, bh) + _dot(ah, bl)


def _sigmoid(x):
    return 1.0 / (1.0 + jnp.exp(-x))


def _silu(x):
    return x * _sigmoid(x)


def _layer_norm(y, g, b):
    mu = jnp.mean(y, axis=-1, keepdims=True)
    yc = y - mu
    var = jnp.mean(yc * yc, axis=-1, keepdims=True)
    return yc * lax.rsqrt(var + EPS) * g + b


def _proj_kernel(a_ref, w_ref, o_ref):
    o_ref[...] = _dot(a_ref[...].astype(BF16), w_ref[...])


def _proj(xb, w, tm, tn):
    m, k = xb.shape
    n = w.shape[1]
    return pl.pallas_call(
        _proj_kernel,
        grid=(m // tm, n // tn),
        in_specs=[pl.BlockSpec((tm, k), lambda i, j: (i, 0)),
                  pl.BlockSpec((k, tn), lambda i, j: (0, j))],
        out_specs=pl.BlockSpec((tm, tn), lambda i, j: (i, j)),
        out_shape=jax.ShapeDtypeStruct((m, n), F32),
        compiler_params=_params("parallel", "parallel"),
        name="proj",
    )(xb, w)


def _unit_lower_inverse4(als):
    c = CHUNK
    n = als[0].shape[1]
    row = lax.broadcasted_iota(I32, (c, n), 0)
    col = jnp.bitwise_and(lax.broadcasted_iota(I32, (c, n), 1), c - 1)
    shift = SOLVE_BLOCK.bit_length() - 1
    same = jnp.right_shift(row, shift) == jnp.right_shift(col, shift)
    eye = (row == col).astype(F32)
    cshift = c.bit_length() - 1
    brow = jnp.right_shift(lax.broadcasted_iota(I32, (n, n), 0), cshift)
    bcol = jnp.right_shift(lax.broadcasted_iota(I32, (n, n), 1), cshift)
    on_diag = brow == bcol

    def mm(x, y):
        yb = y.astype(BF16)
        bd = jnp.where(on_diag, jnp.concatenate([yb] * (n // c), axis=0), jnp.zeros((), BF16))
        return _dot(x.astype(BF16), bd)

    a_diag = [jnp.where(same, al, 0.0) for al in als]
    a_off = [al - ad for al, ad in zip(als, a_diag)]
    bp = [-ad for ad in a_diag]
    p = [eye + b for b in bp]
    for _ in range(3):
        bp = [mm(b, b) for b in bp]
        p = [x + mm(x, b) for x, b in zip(p, bp)]
    n1 = [mm(x, ao) for x, ao in zip(p, a_off)]
    n2 = [mm(x, x) for x in n1]
    q = [x + mm(y, x) for x, y in zip(p, n2)]
    return [x - mm(y, x) for x, y in zip(q, n1)]


def _gdn_intra_kernel(qkv_ref, prev_ref, x_ref, wah_ref, wal_ref, wbh_ref, wbl_ref, cw_ref, prm_ref,
                      u_ref, w_ref, qd_ref, m2_ref, eg_ref, xe_ref, *, rt, tiles_per_seq):
    c = CHUNK
    nc = rt // c
    first = (pl.program_id(0) % tiles_per_seq) == 0
    xe_ref[0:8, :] = jnp.where(first, 0.0, prev_ref[...])
    xe_ref[8:8 + rt, :] = qkv_ref[...]

    xh, xl = _split(x_ref[...])

    def proj3(wh_ref, wl_ref):
        wh = wh_ref[...]
        return _dot(xh, wh) + _dot(xl, wh) + _dot(xh, wl_ref[...])

    a_raw = proj3(wah_ref, wal_ref)
    b_raw = proj3(wbh_ref, wbl_ref)
    sp_in = a_raw + prm_ref[1:2, :]
    softplus = jnp.maximum(sp_in, 0.0) + jnp.log(1.0 + jnp.exp(-jnp.abs(sp_in)))
    g = -jnp.exp(prm_ref[0:1, :]) * softplus
    beta = _sigmoid(b_raw)

    cshift = c.bit_length() - 1
    r2 = lax.broadcasted_iota(I32, (rt, rt), 0)
    c2 = lax.broadcasted_iota(I32, (rt, rt), 1)
    same_chunk = jnp.right_shift(r2, cshift) == jnp.right_shift(c2, cshift)
    ltri = jnp.where(r2 >= c2, jnp.where(same_chunk, 1.0, 0.0), 0.0).astype(BF16)
    g_hi = g.astype(BF16)
    g_r = g - g_hi.astype(F32)
    g_mid = g_r.astype(BF16)
    g_lo = (g_r - g_mid.astype(F32)).astype(BF16)
    gc = _dot(ltri, g_hi) + _dot(ltri, g_mid) + _dot(ltri, g_lo)
    gct = gc.T
    egc = jnp.exp(gc)
    gend = jnp.concatenate(
        [jnp.broadcast_to(gc[ci * c + c - 1:ci * c + c, :], (c, LANES)) for ci in range(nc)], axis=0)
    kfac = jnp.exp(gend - gc)
    bege = beta * egc
    for ci in range(nc):
        last = ci * c + c - 1
        eg_ref[ci * GDN_HEADS:(ci + 1) * GDN_HEADS, :] = jnp.broadcast_to(
            jnp.exp(gct[0:GDN_HEADS, last:last + 1]), (GDN_HEADS, LANES))

    lane_t = lax.broadcasted_iota(I32, (rt, LANES), 1) < c
    lane_lo = lax.broadcasted_iota(I32, (c, LANES), 1) < c
    lcol = jnp.bitwise_and(lax.broadcasted_iota(I32, (c, LANES), 1), c - 1)
    rowi = lax.broadcasted_iota(I32, (c, LANES), 0)
    causal = rowi >= lcol
    strict = rowi > lcol

    def conv(base, h):
        lo, hi = base + h * HEAD_DIM, base + (h + 1) * HEAD_DIM
        acc = cw_ref[SHORT_CONV - 1:SHORT_CONV, lo:hi] * xe_ref[8:8 + rt, lo:hi]
        for j in range(SHORT_CONV - 1):
            s0 = 8 - (SHORT_CONV - 1) + j
            acc = acc + cw_ref[j:j + 1, lo:hi] * xe_ref[s0:s0 + rt, lo:hi]
        return _silu(acc)

    a_pairs = [[None] * (GDN_HEADS // 2) for _ in range(nc)]
    rhs_pairs = [[None] * (GDN_HEADS // 2) for _ in range(nc)]
    for p in range(GDN_HEADS // 2):
        ks, kbs, qs, kds, rhss = [], [], [], [], []
        for h in (2 * p, 2 * p + 1):
            q = conv(0, h)
            k = conv(GDN_WIDTH, h)
            v = conv(2 * GDN_WIDTH, h)
            q = q * lax.rsqrt(jnp.sum(q * q, axis=-1, keepdims=True) + EPS) * (HEAD_DIM ** -0.5)
            k = k * lax.rsqrt(jnp.sum(k * k, axis=-1, keepdims=True) + EPS)
            beta_h = beta[:, h:h + 1]
            qd_ref[:, h * HEAD_DIM:(h + 1) * HEAD_DIM] = (q * egc[:, h:h + 1]).astype(BF16)
            ks.append(k)
            kbs.append(k * beta_h)
            qs.append(q)
            kds.append(k * kfac[:, h:h + 1])
            rhss.append(jnp.concatenate([v * beta_h, k * bege[:, h:h + 1]], axis=1))
        h0, h1 = 2 * p, 2 * p + 1
        gch = jnp.where(lane_t, gc[:, h0:h0 + 1], gc[:, h1:h1 + 1])
        for ci in range(nc):
            rows = slice(ci * c, (ci + 1) * c)
            wk = jnp.concatenate([ks[0][rows], ks[1][rows]], axis=0).astype(BF16)
            lhs = jnp.concatenate([kbs[0][rows], qs[0][rows], kbs[1][rows], qs[1][rows]],
                                  axis=0).astype(BF16)
            out = _dot_nt(lhs, wk)
            gcrow = jnp.concatenate([gct[h0:h0 + 1, rows], gct[h1:h1 + 1, rows]], axis=1)
            diff = gch[rows] - gcrow
            decay = jnp.where(causal, jnp.exp(jnp.where(causal, diff, 0.0)), 0.0)
            a_pairs[ci][p] = jnp.where(strict, jnp.where(lane_lo, out[0:c], out[2 * c:3 * c]) * decay, 0.0)
            qk = jnp.where(lane_lo, out[c:2 * c], out[3 * c:4 * c]) * decay
            kdt = jnp.concatenate([kds[0][rows], kds[1][rows]], axis=0).T
            m0 = ci * 3 * c
            m2_ref[m0:m0 + c, p * LANES:(p + 1) * LANES] = qk.astype(BF16)
            m2_ref[m0 + c:m0 + 3 * c, p * LANES:(p + 1) * LANES] = kdt.astype(BF16)
            rhs_pairs[ci][p] = (rhss[0][rows], rhss[1][rows])

    zeros = jnp.zeros((c, 2 * HEAD_DIM), BF16)
    ngrp = GDN_HEADS // 4
    tls = _unit_lower_inverse4(
        [jnp.concatenate([a_pairs[ci][2 * grp], a_pairs[ci][2 * grp + 1]], axis=1)
         for ci in range(nc) for grp in range(ngrp)])
    for ci in range(nc):
        rows = slice(ci * c, (ci + 1) * c)
        for grp in range(ngrp):
            tl = tls[ci * ngrp + grp]
            for j in range(2):
                p = 2 * grp + j
                r0, r1 = rhs_pairs[ci][p]
                bd = jnp.concatenate([jnp.concatenate([r0.astype(BF16), zeros], axis=1),
                                      jnp.concatenate([zeros, r1.astype(BF16)], axis=1)], axis=0)
                sol = _dot(tl[:, j * LANES:(j + 1) * LANES].astype(BF16), bd)
                for i in range(2):
                    h = 2 * p + i
                    lo, hi = h * HEAD_DIM, (h + 1) * HEAD_DIM
                    u_ref[rows, lo:hi] = sol[:, 2 * i * HEAD_DIM:(2 * i + 1) * HEAD_DIM]
                    w_ref[rows, lo:hi] = sol[:, (2 * i + 1) * HEAD_DIM:(2 * i + 2) * HEAD_DIM].astype(BF16)


def _gdn_intra(proj, x, wa_hi, wa_lo, wb_hi, wb_lo, conv_w, prm, seq, rt):
    t = x.shape[0]
    nc = rt // CHUNK
    kern = functools.partial(_gdn_intra_kernel, rt=rt, tiles_per_seq=seq // rt)
    full = lambda i: (0, 0)
    tile = lambda i: (i, 0)
    return pl.pallas_call(
        kern,
        grid=(t // rt,),
        in_specs=[
            pl.BlockSpec((rt, 3 * GDN_WIDTH), tile),
            pl.BlockSpec((8, 3 * GDN_WIDTH), lambda i: (jnp.maximum(i * (rt // 8) - 1, 0), 0)),
            pl.BlockSpec((rt, D_MODEL), tile),
            pl.BlockSpec((D_MODEL, LANES), full),
            pl.BlockSpec((D_MODEL, LANES), full),
            pl.BlockSpec((D_MODEL, LANES), full),
            pl.BlockSpec((D_MODEL, LANES), full),
            pl.BlockSpec((SHORT_CONV, 3 * GDN_WIDTH), full),
            pl.BlockSpec((8, LANES), full),
        ],
        out_specs=[pl.BlockSpec((rt, GDN_WIDTH), tile),
                   pl.BlockSpec((rt, GDN_WIDTH), tile),
                   pl.BlockSpec((rt, GDN_WIDTH), tile),
                   pl.BlockSpec((nc * 3 * CHUNK, GDN_HEADS // 2 * LANES), tile),
                   pl.BlockSpec((nc * GDN_HEADS, LANES), tile)],
        out_shape=[jax.ShapeDtypeStruct((t, GDN_WIDTH), F32),
                   jax.ShapeDtypeStruct((t, GDN_WIDTH), BF16),
                   jax.ShapeDtypeStruct((t, GDN_WIDTH), BF16),
                   jax.ShapeDtypeStruct((t // CHUNK * 3 * CHUNK, GDN_HEADS // 2 * LANES), BF16),
                   jax.ShapeDtypeStruct((t // CHUNK * GDN_HEADS, LANES), F32)],
        scratch_shapes=[pltpu.VMEM((8 + rt, 3 * GDN_WIDTH), F32)],
        compiler_params=_params("parallel"),
        name="gdn_intra",
    )(proj, proj, x, wa_hi, wa_lo, wb_hi, wb_lo, conv_w, prm)


def _gdn_scan_kernel(u_ref, w_ref, qd_ref, m2_ref, eg_ref, z_ref, ong_ref, o_ref, s_ref, *, nck):
    c = CHUNK

    @pl.when(pl.program_id(1) == 0)
    def _():
        s_ref[...] = jnp.zeros_like(s_ref)

    ong = ong_ref[...]
    zeros = jnp.zeros((c, HEAD_DIM), BF16)
    heads = range(GDN_HEADS)
    span = lambda h: slice(h * HEAD_DIM, (h + 1) * HEAD_DIM)
    states = [s_ref[h] for h in heads]
    for ci in range(nck):
        rows = slice(ci * c, (ci + 1) * c)
        rs = [_dot(jnp.concatenate([w_ref[rows, span(h)], qd_ref[rows, span(h)]], axis=0),
                   states[h].astype(BF16)) for h in heads]
        v_new = [(u_ref[rows, span(h)] - rs[h][:c]).astype(BF16) for h in heads]
        r2s = []
        for p in range(GDN_HEADS // 2):
            bd = jnp.concatenate([jnp.concatenate([v_new[2 * p], zeros], axis=1),
                                  jnp.concatenate([zeros, v_new[2 * p + 1]], axis=1)], axis=0)
            r2s.append(_dot(m2_ref[ci * 3 * c:(ci + 1) * 3 * c, p * LANES:(p + 1) * LANES], bd))
        for h in heads:
            half = span(h % 2)
            r2 = r2s[h // 2]
            states[h] = states[h] * eg_ref[ci * GDN_HEADS + h:ci * GDN_HEADS + h + 1, :] + r2[c:, half]
            o = rs[h][c:] + r2[:c, half]
            o = o * lax.rsqrt(jnp.mean(o * o, axis=-1, keepdims=True) + EPS) * ong
            o = o * _silu(z_ref[rows, span(h)])
            o_ref[rows, span(h)] = o.astype(o_ref.dtype)
    for h in heads:
        s_ref[h] = states[h]


def _gdn_scan(u, w, qd, m2, eg, proj, ong, batch, seq):
    t = batch * seq
    nck = min(SCAN_CHUNKS, seq // CHUNK)
    c = nck * CHUNK
    nch = seq // c
    blk = lambda b, n: (b * nch + n, 0)
    return pl.pallas_call(
        functools.partial(_gdn_scan_kernel, nck=nck),
        grid=(batch, nch),
        in_specs=[
            pl.BlockSpec((c, GDN_WIDTH), blk),
            pl.BlockSpec((c, GDN_WIDTH), blk),
            pl.BlockSpec((c, GDN_WIDTH), blk),
            pl.BlockSpec((3 * c, GDN_HEADS // 2 * LANES), blk),
            pl.BlockSpec((nck * GDN_HEADS, LANES), blk),
            pl.BlockSpec((c, GDN_WIDTH), lambda b, n: (b * nch + n, 3)),
            pl.BlockSpec((1, HEAD_DIM), lambda b, n: (0, 0)),
        ],
        out_specs=pl.BlockSpec((c, GDN_WIDTH), blk),
        out_shape=jax.ShapeDtypeStruct((t, GDN_WIDTH), BF16),
        scratch_shapes=[pltpu.VMEM((GDN_HEADS, HEAD_DIM, HEAD_DIM), F32)],
        compiler_params=_params("parallel", "arbitrary"),
        name="gdn_scan",
    )(u, w, qd, m2, eg, proj, ong)


def _pack_bf16_pairs(y):
    n = y.shape[1] // 2
    yb = y.astype(BF16).astype(F32)
    lo = lax.bitcast_convert_type(yb[:, :n], U32)
    hi = lax.bitcast_convert_type(yb[:, n:], U32)
    return jnp.bitwise_or(jnp.right_shift(lo, 16), jnp.bitwise_and(hi, HI_MASK))


def _unpack_bf16_pairs(w):
    lo = lax.bitcast_convert_type(jnp.left_shift(w, 16), F32)
    hi = lax.bitcast_convert_type(jnp.bitwise_and(w, HI_MASK), F32)
    return lo, hi


def _mixer_kernel(glu_ref, ga_ref, gb_ref, o_ref, x_ref, woa_ref, wob_ref, wo_ref, dww_ref, vec_ref,
                  x1_ref, x1b_ref, x1p_ref, ubuf_ref, sh_ref, conv_ref, *, ts, rc):
    halo = CONF_HALO

    @pl.when(pl.program_id(1) == 0)
    def _():
        ubuf_ref[0:halo, :] = jnp.zeros((halo, CONF_CH), F32)

    @pl.when(pl.program_id(1) != 0)
    def _():
        ubuf_ref[0:halo, :] = ubuf_ref[ts:ts + halo, :]

    ubuf_ref[halo:halo + ts, :] = glu_ref[:, :CONF_CH] * _sigmoid(glu_ref[:, CONF_CH:])

    dw_b = vec_ref[0:1, :]
    cln_g = vec_ref[1:2, :]
    cln_b = vec_ref[2:3, :]
    ln1_g = vec_ref[3:4, :]
    ln1_b = vec_ref[4:5, :]

    span = ts + halo - 8
    for s in range(1, 8):
        sh_ref[s - 1] = ubuf_ref[s:s + span, :]

    def tap_rows(j, r0):
        o = halo - (CONF_KERNEL - 1) + j
        q, s = o // 8, o % 8
        if s == 0:
            return ubuf_ref[r0 + o:r0 + o + rc, :]
        return sh_ref[s - 1, r0 + 8 * q:r0 + 8 * q + rc, :]

    hs = ts // 2
    for h0 in range(0, ts, hs):
        rows = slice(h0, h0 + hs)
        gated_a = _sigmoid(ga_ref[rows, :]) * _dot(o_ref[rows, :], woa_ref[...])
        for r0 in range(h0, h0 + hs, rc):
            tap_w = lambda j: jnp.concatenate([dww_ref[j]] * (rc // 8), axis=0)
            acc = tap_w(0) * tap_rows(0, r0)
            for j in range(1, CONF_KERNEL):
                acc = acc + tap_w(j) * tap_rows(j, r0)
            conv_ref[r0:r0 + rc, :] = acc
        uc = _silu(_layer_norm(conv_ref[rows, :] + dw_b, cln_g, cln_b))
        branch_b = _dot(uc.astype(BF16), wob_ref[...])
        hmix = gated_a + _sigmoid(gb_ref[rows, :]) * branch_b
        mix = _dot(hmix.astype(BF16), wo_ref[...])
        x1 = _layer_norm(DEEPNORM_ALPHA * x_ref[rows, :] + mix, ln1_g, ln1_b)
        x1_ref[rows, :] = x1
        x1b_ref[rows, :] = x1.astype(BF16)
        x1p_ref[rows, :] = _pack_bf16_pairs(x1)


def _mixer(proj, o_gdn, x, woa, wob, wo, dww, vec, batch, seq, ts):
    t = batch * seq
    nt = seq // ts
    rows = lambda b, n: b * nt + n
    full = lambda b, n: (0, 0)
    kern = functools.partial(_mixer_kernel, ts=ts, rc=32)
    return pl.pallas_call(
        kern,
        grid=(batch, nt),
        in_specs=[
            pl.BlockSpec((ts, 2 * CONF_CH), lambda b, n: (rows(b, n), 2)),
            pl.BlockSpec((ts, D_MODEL), lambda b, n: (rows(b, n), 6)),
            pl.BlockSpec((ts, D_MODEL), lambda b, n: (rows(b, n), 7)),
            pl.BlockSpec((ts, GDN_WIDTH), lambda b, n: (rows(b, n), 0)),
            pl.BlockSpec((ts, D_MODEL), lambda b, n: (rows(b, n), 0)),
            pl.BlockSpec((GDN_WIDTH, D_MODEL), full),
            pl.BlockSpec((CONF_CH, D_MODEL), full),
            pl.BlockSpec((D_MODEL, D_MODEL), full),
            pl.BlockSpec((CONF_KERNEL, 8, CONF_CH), lambda b, n: (0, 0, 0)),
            pl.BlockSpec((8, D_MODEL), full),
        ],
        out_specs=[pl.BlockSpec((ts, D_MODEL), lambda b, n: (rows(b, n), 0)),
                   pl.BlockSpec((ts, D_MODEL), lambda b, n: (rows(b, n), 0)),
                   pl.BlockSpec((ts, D_MODEL // 2), lambda b, n: (rows(b, n), 0))],
        out_shape=[jax.ShapeDtypeStruct((t, D_MODEL), F32),
                   jax.ShapeDtypeStruct((t, D_MODEL), BF16),
                   jax.ShapeDtypeStruct((t, D_MODEL // 2), U32)],
        scratch_shapes=[pltpu.VMEM((CONF_HALO + ts, CONF_CH), F32),
                        pltpu.VMEM((7, ts + CONF_HALO - 8, CONF_CH), F32),
                        pltpu.VMEM((ts, CONF_CH), F32)],
        compiler_params=_params("parallel", "arbitrary"),
        name="mixer",
    )(proj, proj, proj, o_gdn, x, woa, wob, wo, dww, vec)


def _route_kernel(x_ref, wrh_ref, wrl_ref, bias_ref, eidx_ref, wts_ref, rank_ref, cnt_ref, carry_ref, *, tt):
    @pl.when(pl.program_id(0) == 0)
    def _():
        carry_ref[...] = jnp.zeros_like(carry_ref)

    xh, xl = _split(x_ref[...])
    wrh = wrh_ref[...]
    logits = _dot_nt(wrh, xh) + _dot_nt(wrh, xl) + _dot_nt(wrl_ref[...], xh)
    s = _sigmoid(logits)
    biased = s + bias_ref[...]

    sub = lax.broadcasted_iota(I32, (GROUP_SIZE, tt), 0)
    groups = [biased[g * GROUP_SIZE:(g + 1) * GROUP_SIZE, :] for g in range(N_GROUPS)]
    gs = []
    for bg in groups:
        m1 = jnp.max(bg, axis=0, keepdims=True)
        first = jnp.min(jnp.where(bg == m1, sub, GROUP_SIZE), axis=0, keepdims=True)
        m2 = jnp.max(jnp.where(sub == first, -jnp.inf, bg), axis=0, keepdims=True)
        gs.append(m1 + m2)

    masked_parts = []
    for g in range(N_GROUPS):
        beaten = jnp.zeros((1, tt), I32)
        for o in range(N_GROUPS):
            if o == g:
                continue
            wins = (gs[o] >= gs[g]) if o < g else (gs[o] > gs[g])
            beaten = beaten + wins.astype(I32)
        keep = jnp.broadcast_to(beaten < TOPK_GROUPS, (GROUP_SIZE, tt))
        masked_parts.append(jnp.where(keep, groups[g], -jnp.inf))
    masked = jnp.concatenate(masked_parts, axis=0)

    eiota = lax.broadcasted_iota(I32, (N_EXPERTS, tt), 0)
    sel_all = jnp.zeros((N_EXPERTS, tt), F32)
    picks = []
    for _ in range(TOP_K):
        m = jnp.max(masked, axis=0, keepdims=True)
        idx = jnp.min(jnp.where(masked == m, eiota, N_EXPERTS), axis=0, keepdims=True)
        onehot = eiota == idx
        picks.append((idx, onehot))
        sel_all = jnp.where(onehot, 1.0, sel_all)
        masked = jnp.where(onehot, -jnp.inf, masked)

    tr = lax.broadcasted_iota(I32, (tt, tt), 0)
    tc = lax.broadcasted_iota(I32, (tt, tt), 1)
    before = (tr < tc).astype(BF16)
    sel_b = sel_all.astype(BF16)
    carry = carry_ref[...]
    rank_all = _dot(sel_b, before) + carry[:, 0:1]
    carry_new = carry + _dot(sel_b, jnp.ones((tt, LANES), BF16))
    carry_ref[...] = carry_new
    cnt_ref[...] = carry_new

    s_sel = [jnp.sum(jnp.where(oh, s, 0.0), axis=0, keepdims=True) for _, oh in picks]
    total = s_sel[0]
    for v in s_sel[1:]:
        total = total + v
    for k, (idx, oh) in enumerate(picks):
        eidx_ref[k:k + 1, :] = idx
        wts_ref[k:k + 1, :] = s_sel[k] / total * ROUTED_SCALE
        rank_ref[k:k + 1, :] = jnp.sum(jnp.where(oh, rank_all, 0.0), axis=0, keepdims=True).astype(I32)


def _route(x1, wr_hi, wr_lo, bias, tt):
    t = x1.shape[0]
    kern = functools.partial(_route_kernel, tt=tt)
    return pl.pallas_call(
        kern,
        grid=(t // tt,),
        in_specs=[pl.BlockSpec((tt, D_MODEL), lambda i: (i, 0)),
                  pl.BlockSpec((N_EXPERTS, D_MODEL), lambda i: (0, 0)),
                  pl.BlockSpec((N_EXPERTS, D_MODEL), lambda i: (0, 0)),
                  pl.BlockSpec((N_EXPERTS, tt), lambda i: (0, 0))],
        out_specs=[pl.BlockSpec((TOP_K, tt), lambda i: (0, i)),
                   pl.BlockSpec((TOP_K, tt), lambda i: (0, i)),
                   pl.BlockSpec((TOP_K, tt), lambda i: (0, i)),
                   pl.BlockSpec((N_EXPERTS, LANES), lambda i: (0, 0))],
        out_shape=[jax.ShapeDtypeStruct((TOP_K, t), I32),
                   jax.ShapeDtypeStruct((TOP_K, t), F32),
                   jax.ShapeDtypeStruct((TOP_K, t), I32),
                   jax.ShapeDtypeStruct((N_EXPERTS, LANES), F32)],
        scratch_shapes=[pltpu.VMEM((N_EXPERTS, LANES), F32)],
        compiler_params=_params("arbitrary"),
        name="route",
    )(x1, wr_hi, wr_lo, bias)


def _plan_kernel(cnt_ref, pstart_ref, plan_ref):
    e, nb = N_EXPERTS, plan_ref.shape[1]
    counts = cnt_ref[...]
    nblk = jnp.floor((counts + (ROW_BLOCK - 1)) * (1.0 / ROW_BLOCK))
    hi = jnp.floor(nblk * (1.0 / 256.0))
    lo = nblk - 256.0 * hi
    r = lax.broadcasted_iota(I32, (e, e), 0)
    c = lax.broadcasted_iota(I32, (e, e), 1)
    ltri = (r >= c).astype(BF16)
    bend = 256.0 * _dot(ltri, hi.astype(BF16)) + _dot(ltri, lo.astype(BF16))
    pend = bend * ROW_BLOCK
    pstart = pend - nblk * ROW_BLOCK
    pstart_ref[...] = pstart.astype(I32)

    bs = (lax.broadcasted_iota(I32, (e, nb), 1) * ROW_BLOCK).astype(F32)
    pend_b = jnp.broadcast_to(pend[:, 0:1], (e, nb))
    pstart_b = jnp.broadcast_to(pstart[:, 0:1], (e, nb))
    used_b = jnp.broadcast_to((pstart + counts)[:, 0:1], (e, nb))
    owner = jnp.sum(jnp.where(pend_b <= bs, 1.0, 0.0), axis=0, keepdims=True)
    inside = jnp.where(pstart_b <= bs, jnp.where(bs < pend_b, 1.0, 0.0), 0.0)
    real = jnp.sum(inside * jnp.clip(used_b - bs, 0.0, float(ROW_BLOCK)), axis=0, keepdims=True)
    plan_ref[0:1, :] = jnp.minimum(owner, float(e - 1)).astype(I32)
    plan_ref[1:2, :] = real.astype(I32)
    plan_ref[2:8, :] = jnp.zeros((6, nb), I32)


def _plan(cnt, n_blocks):
    nb = -(-n_blocks // LANES) * LANES
    pstart, plan = pl.pallas_call(
        _plan_kernel,
        out_shape=[jax.ShapeDtypeStruct((N_EXPERTS, LANES), I32),
                   jax.ShapeDtypeStruct((8, nb), I32)],
        name="plan",
    )(cnt)
    return pstart[:, 0], plan[0, :n_blocks], plan[1, :n_blocks]


def _dest_kernel(eidx_ref, rank_ref, pstart_ref, dest_ref):
    eidx = eidx_ref[...]
    acc = rank_ref[...]
    for e in range(N_EXPERTS):
        acc = acc + jnp.where(eidx == e, pstart_ref[e], 0)
    dest_ref[...] = acc


def _dest(eidx_t, rank_t, pstart, tt):
    t = eidx_t.shape[1]
    return pl.pallas_call(
        _dest_kernel,
        grid=(t // tt,),
        in_specs=[pl.BlockSpec((TOP_K, tt), lambda i: (0, i)),
                  pl.BlockSpec((TOP_K, tt), lambda i: (0, i)),
                  pl.BlockSpec(memory_space=pltpu.SMEM)],
        out_specs=pl.BlockSpec((TOP_K, tt), lambda i: (0, i)),
        out_shape=jax.ShapeDtypeStruct((TOP_K, t), I32),
        compiler_params=_params("parallel"),
        name="dest",
    )(eidx_t, rank_t, pstart)


def _dispatch(dest_kt, x1p, n_rows):
    t, width = x1p.shape
    info = plsc.get_sparse_core_info()
    nc, nw = info.num_cores, info.num_cores * info.num_subcores
    chunk = SC_SCATTER_ROWS
    per_w = t // nw
    n_chunks = per_w // chunk
    assert per_w % chunk == 0
    idx = dest_kt.reshape(TOP_K, nw, n_chunks, chunk).transpose(1, 2, 0, 3).reshape(nw, n_chunks * TOP_K, chunk)
    mesh = plsc.VectorSubcoreMesh(core_axis_name="c", subcore_axis_name="s")

    @functools.partial(
        pl.kernel, mesh=mesh, name="dispatch",
        out_type=jax.ShapeDtypeStruct((n_rows, width), x1p.dtype),
        scratch_types=[pltpu.VMEM((n_chunks * TOP_K, chunk), I32),
                       pltpu.VMEM((chunk, width), x1p.dtype),
                       pltpu.SemaphoreType.DMA])
    def scatter(x_hbm, idx_hbm, xs_hbm, idx_v, rows_v, sem):
        wid = lax.axis_index("s") * nc + lax.axis_index("c")
        base = wid * per_w
        pltpu.sync_copy(idx_hbm.at[wid], idx_v)
        for j in range(n_chunks):
            pltpu.sync_copy(x_hbm.at[pl.ds(base + j * chunk, chunk)], rows_v)
            copies = [pltpu.make_async_copy(rows_v, xs_hbm.at[idx_v.at[j * TOP_K + k]], sem)
                      for k in range(TOP_K)]
            for cp in copies:
                cp.start()
            for cp in copies:
                cp.wait()

    return scatter(x1p, idx)


def _gather_rows(table, idx):
    n = idx.shape[0]
    width = table.shape[1]
    info = plsc.get_sparse_core_info()
    nc, nw = info.num_cores, info.num_cores * info.num_subcores
    chunk = SC_GATHER_ROWS
    per_w = n // nw
    n_chunks = per_w // chunk
    assert per_w % (2 * chunk) == 0
    mesh = plsc.VectorSubcoreMesh(core_axis_name="c", subcore_axis_name="s")

    @functools.partial(
        pl.kernel, mesh=mesh, name="gather_rows",
        out_type=jax.ShapeDtypeStruct((n, width), table.dtype),
        scratch_types=[pltpu.VMEM((n_chunks, chunk), I32),
                       pltpu.VMEM((2, chunk, width), table.dtype),
                       pltpu.SemaphoreType.DMA((2,)),
                       pltpu.SemaphoreType.DMA((2,))])
    def gather(table_hbm, idx_hbm, out_hbm, idx_v, rows_v, gsem, osem):
        wid = lax.axis_index("s") * nc + lax.axis_index("c")
        base = wid * per_w
        pltpu.sync_copy(idx_hbm.at[wid], idx_v)

        def fetch(j, b):
            return pltpu.make_async_copy(table_hbm.at[idx_v.at[j]], rows_v.at[b], gsem.at[b])

        def put(j, b):
            return pltpu.make_async_copy(rows_v.at[b], out_hbm.at[pl.ds(base + j * chunk, chunk)], osem.at[b])

        fetch(0, 0).start()

        @pl.loop(0, n_chunks, step=2)
        def _(j0):
            for b in range(2):
                j = j0 + b
                fetch(j, b).wait()

                @pl.when(j + 1 < n_chunks)
                def _():
                    @pl.when(j >= 1)
                    def _():
                        put(j - 1, 1 - b).wait()

                    fetch(j + 1, 1 - b).start()

                put(j, b).start()

        put(n_chunks - 2, 0).wait()
        put(n_chunks - 1, 1).wait()

    return gather(table, idx.reshape(nw, n_chunks, chunk))


def _xs_copy(xs_hbm, xbuf, isem, j, slot):
    return pltpu.make_async_copy(xs_hbm.at[pl.ds(j * ROW_BLOCK, ROW_BLOCK)], xbuf.at[slot], isem.at[slot])


def _ys_copy(ybuf, ys_hbm, osem, j, slot):
    return pltpu.make_async_copy(ybuf.at[slot], ys_hbm.at[pl.ds(j * ROW_BLOCK, ROW_BLOCK)], osem.at[slot])


def _experts_kernel(be_ref, nv_ref, xs_hbm, wg_ref, wu_ref, wd_ref, ys_hbm,
                    xbuf, ybuf, wgu_s, wd_s, cur_ref, isem, osem):
    i = pl.program_id(0)
    n_valid = nv_ref[i]
    half = EXPERT_SUB
    slot = lax.rem(i, EXPERT_IN_SLOTS)
    oslot = lax.rem(i, 2)

    @pl.when(i == 0)
    def _():
        cur_ref[0] = -1
        for j in range(2):
            @pl.when(nv_ref[j] > 0)
            def _():
                _xs_copy(xs_hbm, xbuf, isem, j, j).start()

    @pl.when(nv_ref[i + 2] > 0)
    def _():
        _xs_copy(xs_hbm, xbuf, isem, i + 2, lax.rem(i + 2, EXPERT_IN_SLOTS)).start()

    @pl.when((i >= 2) & (nv_ref[jnp.maximum(i - 2, 0)] > 0))
    def _():
        _ys_copy(ybuf, ys_hbm, osem, i - 2, oslot).wait()

    @pl.when((n_valid > 0) & (cur_ref[0] != be_ref[i]))
    def _():
        wgu_s[:, :EXPERT_FF] = wg_ref[...].astype(BF16)
        wgu_s[:, EXPERT_FF:] = wu_ref[...].astype(BF16)
        wd_s[...] = wd_ref[...].astype(BF16)
        cur_ref[0] = be_ref[i]

    def rows_bf16(r0):
        valid = lax.broadcasted_iota(I32, (half, xbuf.shape[2]), 0) + r0 < n_valid
        lo, hi = _unpack_bf16_pairs(jnp.where(valid, xbuf[slot, r0:r0 + half, :], jnp.zeros((), U32)))
        return jnp.concatenate([lo.astype(BF16), hi.astype(BF16)], axis=1)

    def hidden(gu):
        return (_silu(gu[:, :EXPERT_FF]) * gu[:, EXPERT_FF:]).astype(BF16)

    @pl.when(n_valid > 0)
    def _():
        _xs_copy(xs_hbm, xbuf, isem, i, slot).wait()

    @pl.when(n_valid > half)
    def _():
        xa, xb = rows_bf16(0), rows_bf16(half)
        gua = _dot(xa, wgu_s[...])
        gub = _dot(xb, wgu_s[...])
        ya = _dot(hidden(gua), wd_s[...])
        yb = _dot(hidden(gub), wd_s[...])
        ybuf[oslot, 0:half, :] = _pack_bf16_pairs(ya)
        ybuf[oslot, half:, :] = _pack_bf16_pairs(yb)

    @pl.when((n_valid > 0) & (n_valid <= half))
    def _():
        ya = _dot(hidden(_dot(rows_bf16(0), wgu_s[...])), wd_s[...])
        ybuf[oslot, 0:half, :] = _pack_bf16_pairs(ya)
        ybuf[oslot, half:, :] = jnp.zeros((half, ybuf.shape[2]), ybuf.dtype)

    @pl.when(n_valid > 0)
    def _():
        _ys_copy(ybuf, ys_hbm, osem, i, oslot).start()


def _experts(block_e, n_valid, xs, wg, wu, wd, layer):
    n_rows, width = xs.shape
    steps = n_rows // ROW_BLOCK + 2
    be = jnp.concatenate([block_e, jnp.full((2,), N_EXPERTS - 1, I32)])
    nv = jnp.concatenate([n_valid, jnp.zeros((4,), I32)])
    grid_spec = pltpu.PrefetchScalarGridSpec(
        num_scalar_prefetch=2,
        grid=(steps,),
        in_specs=[pl.BlockSpec(memory_space=pl.ANY),
                  pl.BlockSpec((None, None, D_MODEL, EXPERT_FF), lambda i, be, nv: (layer, be[i], 0, 0)),
                  pl.BlockSpec((None, None, D_MODEL, EXPERT_FF), lambda i, be, nv: (layer, be[i], 0, 0)),
                  pl.BlockSpec((None, None, EXPERT_FF, D_MODEL), lambda i, be, nv: (layer, be[i], 0, 0))],
        out_specs=pl.BlockSpec(memory_space=pl.ANY),
        scratch_shapes=[pltpu.VMEM((EXPERT_IN_SLOTS, ROW_BLOCK, width), xs.dtype),
                        pltpu.VMEM((2, ROW_BLOCK, width), xs.dtype),
                        pltpu.VMEM((D_MODEL, 2 * EXPERT_FF), BF16),
                        pltpu.VMEM((EXPERT_FF, D_MODEL), BF16),
                        pltpu.SMEM((1,), I32),
                        pltpu.SemaphoreType.DMA((EXPERT_IN_SLOTS,)),
                        pltpu.SemaphoreType.DMA((2,))],
    )
    return pl.pallas_call(
        _experts_kernel,
        grid_spec=grid_spec,
        out_shape=jax.ShapeDtypeStruct((n_rows, width), xs.dtype),
        compiler_params=_params("arbitrary"),
        name="experts",
    )(be, nv, xs, wg, wu, wd)


def _shared_kernel(x1b_ref, wsg_ref, wsu_ref, wsd_ref, anchor_ref, sh_ref):
    del anchor_ref
    xb = x1b_ref[...]
    hid = (_silu(_dot(xb, wsg_ref[...])) * _dot(xb, wsu_ref[...])).astype(BF16)
    sh_ref[...] = _dot(hid, wsd_ref[...])


def _shared(x1b, wsg, wsu, wsd, anchor, tt, row0, nrows):
    full = lambda i: (0, 0)
    first = row0 // tt
    return pl.pallas_call(
        _shared_kernel,
        grid=(nrows // tt,),
        in_specs=[pl.BlockSpec((tt, D_MODEL), lambda i: (first + i, 0)),
                  pl.BlockSpec((D_MODEL, SHARED_FF), full),
                  pl.BlockSpec((D_MODEL, SHARED_FF), full),
                  pl.BlockSpec((SHARED_FF, D_MODEL), full),
                  pl.BlockSpec(memory_space=pl.ANY)],
        out_specs=pl.BlockSpec((tt, D_MODEL), lambda i: (i, 0)),
        out_shape=jax.ShapeDtypeStruct((nrows, D_MODEL), F32),
        compiler_params=_params("parallel"),
        name="shared",
    )(x1b, wsg, wsu, wsd, anchor)


def _combine_kernel(yg_ref, w_ref, x1_ref, sha_ref, shb_ref, vec_ref, x2_ref, *, half_steps):
    in_first = pl.program_id(0) < half_steps
    shared = jnp.where(in_first, sha_ref[...], shb_ref[...])
    w = w_ref[...]
    half = D_MODEL // 2
    acc_lo, acc_hi = shared[:, :half], shared[:, half:]
    for k in range(TOP_K):
        lo, hi = _unpack_bf16_pairs(yg_ref[k])
        acc_lo = acc_lo + w[:, k:k + 1] * lo
        acc_hi = acc_hi + w[:, k:k + 1] * hi
    acc = jnp.concatenate([acc_lo, acc_hi], axis=1)
    x2 = _layer_norm(DEEPNORM_ALPHA * x1_ref[...] + acc, vec_ref[0:1, :], vec_ref[1:2, :])
    x2_ref[...] = x2


def _combine(yg, w_tok, x1, shared_a, shared_b, vec, tt):
    t = x1.shape[0]
    half_steps = shared_a.shape[0] // tt
    full = lambda i: (0, 0)
    return pl.pallas_call(
        functools.partial(_combine_kernel, half_steps=half_steps),
        grid=(t // tt,),
        in_specs=[pl.BlockSpec((TOP_K, tt, yg.shape[2]), lambda i: (0, i, 0)),
                  pl.BlockSpec((tt, TOP_K), lambda i: (i, 0)),
                  pl.BlockSpec((tt, D_MODEL), lambda i: (i, 0)),
                  pl.BlockSpec((tt, D_MODEL), lambda i: (jnp.minimum(i, half_steps - 1), 0)),
                  pl.BlockSpec((tt, D_MODEL), lambda i: (jnp.maximum(i - half_steps, 0), 0)),
                  pl.BlockSpec((8, D_MODEL), full)],
        out_specs=pl.BlockSpec((tt, D_MODEL), lambda i: (i, 0)),
        out_shape=jax.ShapeDtypeStruct((t, D_MODEL), F32),
        compiler_params=_params("parallel"),
        name="combine",
    )(yg, w_tok, x1, shared_a, shared_b, vec)


def _pad_rows(a, rows):
    return jnp.zeros((rows, a.shape[-1]), F32).at[:a.shape[0]].set(a.astype(F32))


def _layer(x, p, stacked, layer, batch, seq):
    t = batch * seq
    gw = GDN_WIDTH
    w_in = p["w_in"]
    c_z, c_a, c_glu = 3 * gw, 4 * gw, 4 * gw + 2 * GDN_HEADS
    c_ga = c_glu + 2 * CONF_CH
    c_gb = c_ga + D_MODEL
    w_main = jnp.concatenate([w_in[:, :c_a], w_in[:, c_glu:]], axis=1).astype(BF16)
    w_a = jnp.zeros((D_MODEL, LANES), F32).at[:, :GDN_HEADS].set(w_in[:, c_a:c_a + GDN_HEADS])
    w_b = jnp.zeros((D_MODEL, LANES), F32).at[:, :GDN_HEADS].set(w_in[:, c_a + GDN_HEADS:c_glu])
    wa_hi, wb_hi = w_a.astype(BF16), w_b.astype(BF16)
    wa_lo = (w_a - wa_hi.astype(F32)).astype(BF16)
    wb_lo = (w_b - wb_hi.astype(F32)).astype(BF16)
    del c_z, c_gb

    tm = min(1024, t)
    proj = _proj(x, w_main, tm, 2048)

    prm = jnp.zeros((8, LANES), F32)
    prm = prm.at[0, :GDN_HEADS].set(p["a_log"]).at[1, :GDN_HEADS].set(p["dt_bias"])
    ong = p["o_norm_g"].reshape(1, HEAD_DIM).astype(F32)
    u, w, qd, m2, eg = _gdn_intra(proj, x, wa_hi, wa_lo, wb_hi, wb_lo, p["conv_qkv"].astype(F32), prm,
                                  seq, min(GDN_TILE, seq))
    o_gdn = _gdn_scan(u, w, qd, m2, eg, proj, ong, batch, seq)

    ts = min(256, seq)
    dww = jnp.broadcast_to(p["dw_w"].astype(F32)[:, None, :], (CONF_KERNEL, 8, CONF_CH))
    vec = _pad_rows(jnp.stack([p["dw_b"], p["cln_g"], p["cln_b"], p["ln1_g"], p["ln1_b"]]), 8)
    x1, x1b, x1p = _mixer(proj, o_gdn, x, p["w_oa"].astype(BF16), p["w_ob"].astype(BF16),
                          p["w_o"].astype(BF16), dww, vec, batch, seq, ts)

    tt_r = min(256, t)
    wr_t = p["w_router"].T.astype(F32)
    wr_hi = wr_t.astype(BF16)
    wr_lo = (wr_t - wr_hi.astype(F32)).astype(BF16)
    bias = jnp.broadcast_to(p["router_bias"].astype(F32)[:, None], (N_EXPERTS, tt_r))
    eidx_t, wts_t, rank_t, cnt = _route(x1, wr_hi, wr_lo, bias, tt_r)

    n_blocks = -(-(t * TOP_K + N_EXPERTS * (ROW_BLOCK - 1)) // ROW_BLOCK)
    n_rows = n_blocks * ROW_BLOCK
    pstart, block_e, n_valid = _plan(cnt, n_blocks)

    dest = _dest(eidx_t, rank_t, pstart, min(2048, t))
    sh_w = (p["w_sh_gate"].astype(BF16), p["w_sh_up"].astype(BF16), p["w_sh_down"].astype(BF16))
    tt_s = min(512, t // 2)
    xs = _dispatch(dest, x1p, n_rows)
    shared_a = _shared(x1b, *sh_w, dest, tt_s, 0, t // 2)
    ys = _experts(block_e, n_valid, xs, stacked["w_gate_e"], stacked["w_up_e"], stacked["w_down_e"], layer)
    yg = _gather_rows(ys, dest.reshape(TOP_K * t)).reshape(TOP_K, t, ys.shape[1])
    shared_b = _shared(x1b, *sh_w, ys, tt_s, t // 2, t // 2)
    vec2 = _pad_rows(jnp.stack([p["ln2_g"], p["ln2_b"]]), 8)
    return _combine(yg, wts_t.T, x1, shared_a, shared_b, vec2, min(256, t // 2))


_PARAM_NAMES = ("w_in", "conv_qkv", "a_log", "dt_bias", "o_norm_g", "w_oa", "dw_w", "dw_b", "cln_g",
                "cln_b", "w_ob", "w_o", "ln1_g", "ln1_b", "w_router", "router_bias", "w_gate_e",
                "w_up_e", "w_down_e", "w_sh_gate", "w_sh_up", "w_sh_down", "ln2_g", "ln2_b")


_EXPERT_WEIGHTS = ("w_gate_e", "w_up_e", "w_down_e")


def kernel(x, w_in, conv_qkv, a_log, dt_bias, o_norm_g, w_oa, dw_w, dw_b, cln_g, cln_b, w_ob, w_o,
           ln1_g, ln1_b, w_router, router_bias, w_gate_e, w_up_e, w_down_e, w_sh_gate, w_sh_up,
           w_sh_down, ln2_g, ln2_b):
    stacked = dict(zip(_PARAM_NAMES, (w_in, conv_qkv, a_log, dt_bias, o_norm_g, w_oa, dw_w, dw_b, cln_g,
                                      cln_b, w_ob, w_o, ln1_g, ln1_b, w_router, router_bias, w_gate_e,
                                      w_up_e, w_down_e, w_sh_gate, w_sh_up, w_sh_down, ln2_g, ln2_b)))
    batch, seq, d = x.shape
    assert d == D_MODEL and seq % CHUNK == 0
    xf = x.reshape(batch * seq, d).astype(F32)
    for layer in range(w_in.shape[0]):
        p = {name: arr[layer] for name, arr in stacked.items() if name not in _EXPERT_WEIGHTS}
        xf = _layer(xf, p, stacked, layer, batch, seq)
    return xf.reshape(batch, seq, d).astype(x.dtype)
```

```python
import functools

import jax
import jax.numpy as jnp
import numpy as np
from jax import lax
from jax.experimental import pallas as pl
from jax.experimental.pallas import tpu as pltpu
from jax.experimental.pallas import tpu_sc as plsc

F32 = jnp.float32
BF16 = jnp.bfloat16
I32 = jnp.int32
U32 = jnp.uint32
HI_MASK = np.uint32(0xFFFF0000)

D_MODEL = 1024
GDN_HEADS = 8
HEAD_DIM = 128
GDN_WIDTH = GDN_HEADS * HEAD_DIM
SHORT_CONV = 4
CONV_HALO = 16
CHUNK = 64
SOLVE_BLOCK = 16
GDN_TILE = 256
SCAN_CHUNKS = 4
CONF_CH = D_MODEL
CONF_KERNEL = 31
CONF_HALO = 32
N_EXPERTS = 64
TOP_K = 8
N_GROUPS = 8
GROUP_SIZE = N_EXPERTS // N_GROUPS
TOPK_GROUPS = 4
EXPERT_FF = 256
SHARED_FF = 256
ROUTED_SCALE = 2.5
DEPTH = 2
DEEPNORM_ALPHA = (2 * DEPTH) ** 0.25
EPS = 1e-6

LANES = 128
PROJ_COLS = 8 * D_MODEL
ROW_BLOCK = 512
EXPERT_SUB = ROW_BLOCK // 2
EXPERT_IN_SLOTS = 3
SC_SCATTER_ROWS = 128
SC_GATHER_ROWS = 64
VMEM_LIMIT = 56 * 1024 * 1024


def _params(*sem):
    return pltpu.CompilerParams(dimension_semantics=sem, vmem_limit_bytes=VMEM_LIMIT)


def _dot(a, b):
    return jnp.dot(a, b, preferred_element_type=F32)


def _dot_nt(a, b):
    return lax.dot_general(a, b, (((1,), (1,)), ((), ())), preferred_element_type=F32)


def _split(a):
    hi = a.astype(BF16)
    lo = (a - hi.astype(F32)).astype(BF16)
    return hi, lo


def _dot3(a, b):
    ah, al = _split(a)
    bh, bl = _split(b)
    return _dot(ah, bh) + _dot(al, bh) + _dot(ah, bl)


def _sigmoid(x):
    return 1.0 / (1.0 + jnp.exp(-x))


def _silu(x):
    return x * _sigmoid(x)


def _layer_norm(y, g, b):
    mu = jnp.mean(y, axis=-1, keepdims=True)
    yc = y - mu
    var = jnp.mean(yc * yc, axis=-1, keepdims=True)
    return yc * lax.rsqrt(var + EPS) * g + b


def _proj_kernel(a_ref, w_ref, o_ref):
    o_ref[...] = _dot(a_ref[...].astype(BF16), w_ref[...]).astype(o_ref.dtype)


def _proj(xb, w, tm, tn):
    m, k = xb.shape
    n = w.shape[1]
    return pl.pallas_call(
        _proj_kernel,
        grid=(m // tm, n // tn),
        in_specs=[pl.BlockSpec((tm, k), lambda i, j: (i, 0)),
                  pl.BlockSpec((k, tn), lambda i, j: (0, j))],
        out_specs=pl.BlockSpec((tm, tn), lambda i, j: (i, j)),
        out_shape=jax.ShapeDtypeStruct((m, n), BF16),
        compiler_params=_params("parallel", "parallel"),
        name="proj",
    )(xb, w)


def _unit_lower_inverse4(als):
    c = CHUNK
    n = als[0].shape[1]
    row = lax.broadcasted_iota(I32, (c, n), 0)
    col = jnp.bitwise_and(lax.broadcasted_iota(I32, (c, n), 1), c - 1)
    shift = SOLVE_BLOCK.bit_length() - 1
    same = jnp.right_shift(row, shift) == jnp.right_shift(col, shift)
    eye = (row == col).astype(F32)
    cshift = c.bit_length() - 1
    brow = jnp.right_shift(lax.broadcasted_iota(I32, (n, n), 0), cshift)
    bcol = jnp.right_shift(lax.broadcasted_iota(I32, (n, n), 1), cshift)
    on_diag = brow == bcol

    def mm(x, y):
        yb = y.astype(BF16)
        bd = jnp.where(on_diag, jnp.concatenate([yb] * (n // c), axis=0), jnp.zeros((), BF16))
        return _dot(x.astype(BF16), bd)

    a_diag = [jnp.where(same, al, 0.0) for al in als]
    a_off = [al - ad for al, ad in zip(als, a_diag)]
    bp = [-ad for ad in a_diag]
    p = [eye + b for b in bp]
    for _ in range(3):
        bp = [mm(b, b) for b in bp]
        p = [x + mm(x, b) for x, b in zip(p, bp)]
    n1 = [mm(x, ao) for x, ao in zip(p, a_off)]
    n2 = [mm(x, x) for x in n1]
    q = [x + mm(y, x) for x, y in zip(p, n2)]
    return [x - mm(y, x) for x, y in zip(q, n1)]


def _shift_selectors(rt):
    sel = np.zeros((SHORT_CONV * rt, rt), np.float32)
    sel_halo = np.zeros((SHORT_CONV * 8, CONV_HALO), np.float32)
    for d in range(SHORT_CONV):
        for t in range(d, rt):
            sel[d * rt + t, t - d] = 1.0
        for t in range(d):
            sel_halo[d * 8 + t, CONV_HALO + t - d] = 1.0
    return jnp.asarray(sel, BF16), jnp.asarray(sel_halo, BF16)


def _gdn_intra_kernel(qkv_ref, prev_ref, x_ref, sel_ref, selh_ref, wah_ref, wal_ref, wbh_ref, wbl_ref, cw_ref,
                      prm_ref, u_ref, w_ref, qd_ref, m2_ref, eg_ref, xs_ref, *, rt, tiles_per_seq):
    c = CHUNK
    nc = rt // c
    first = (pl.program_id(0) % tiles_per_seq) == 0
    edge = jnp.where(first, 0.0, _dot(selh_ref[...], prev_ref[...]))

    def move_rows(lo, hi):
        moved = _dot(sel_ref[...], qkv_ref[:, lo:hi])
        for d in range(SHORT_CONV):
            xs_ref[d, 0:8, lo:hi] = moved[d * rt:d * rt + 8] + edge[d * 8:(d + 1) * 8, lo:hi]
            xs_ref[d, 8:rt, lo:hi] = moved[d * rt + 8:(d + 1) * rt]

    for part in range(3):
        move_rows(part * GDN_WIDTH, part * GDN_WIDTH + 2 * HEAD_DIM)

    xh, xl = _split(x_ref[...])

    def proj3(wh_ref, wl_ref):
        wh = wh_ref[...]
        return _dot(xh, wh) + _dot(xl, wh) + _dot(xh, wl_ref[...])

    a_raw = proj3(wah_ref, wal_ref)
    b_raw = proj3(wbh_ref, wbl_ref)
    sp_in = a_raw + prm_ref[1:2, :]
    softplus = jnp.maximum(sp_in, 0.0) + jnp.log(1.0 + jnp.exp(-jnp.abs(sp_in)))
    g = -jnp.exp(prm_ref[0:1, :]) * softplus
    beta = _sigmoid(b_raw)

    cshift = c.bit_length() - 1
    r2 = lax.broadcasted_iota(I32, (rt, rt), 0)
    c2 = lax.broadcasted_iota(I32, (rt, rt), 1)
    same_chunk = jnp.right_shift(r2, cshift) == jnp.right_shift(c2, cshift)
    ltri = jnp.where(r2 >= c2, jnp.where(same_chunk, 1.0, 0.0), 0.0).astype(BF16)
    g_hi = g.astype(BF16)
    g_r = g - g_hi.astype(F32)
    g_mid = g_r.astype(BF16)
    g_lo = (g_r - g_mid.astype(F32)).astype(BF16)
    gc = _dot(ltri, g_hi) + _dot(ltri, g_mid) + _dot(ltri, g_lo)
    gct = gc.T
    egc = jnp.exp(gc)
    gend = jnp.concatenate(
        [jnp.broadcast_to(gc[ci * c + c - 1:ci * c + c, :], (c, LANES)) for ci in range(nc)], axis=0)
    kfac = jnp.exp(gend - gc)
    bege = beta * egc
    for ci in range(nc):
        last = ci * c + c - 1
        eg_ref[ci * GDN_HEADS:(ci + 1) * GDN_HEADS, :] = jnp.broadcast_to(
            jnp.exp(gct[0:GDN_HEADS, last:last + 1]), (GDN_HEADS, LANES))

    lane_t = lax.broadcasted_iota(I32, (rt, LANES), 1) < c
    lane_lo = lax.broadcasted_iota(I32, (c, LANES), 1) < c
    lcol = jnp.bitwise_and(lax.broadcasted_iota(I32, (c, LANES), 1), c - 1)
    rowi = lax.broadcasted_iota(I32, (c, LANES), 0)
    causal = rowi >= lcol
    strict = rowi > lcol

    def conv(base, h):
        lo, hi = base + h * HEAD_DIM, base + (h + 1) * HEAD_DIM
        acc = cw_ref[SHORT_CONV - 1:SHORT_CONV, lo:hi] * xs_ref[0, :, lo:hi]
        for j in range(SHORT_CONV - 1):
            acc = acc + cw_ref[j:j + 1, lo:hi] * xs_ref[SHORT_CONV - 1 - j, :, lo:hi]
        return _silu(acc)

    a_pairs = [[None] * (GDN_HEADS // 2) for _ in range(nc)]
    rhs_pairs = [[None] * (GDN_HEADS // 2) for _ in range(nc)]
    for p in range(GDN_HEADS // 2):
        if p + 1 < GDN_HEADS // 2:
            for part in range(3):
                lo = part * GDN_WIDTH + (p + 1) * 2 * HEAD_DIM
                move_rows(lo, lo + 2 * HEAD_DIM)
        ks, kbs, qs, kds, rhss = [], [], [], [], []
        for h in (2 * p, 2 * p + 1):
            q = conv(0, h)
            k = conv(GDN_WIDTH, h)
            v = conv(2 * GDN_WIDTH, h)
            q = q * lax.rsqrt(jnp.sum(q * q, axis=-1, keepdims=True) + EPS) * (HEAD_DIM ** -0.5)
            k = k * lax.rsqrt(jnp.sum(k * k, axis=-1, keepdims=True) + EPS)
            beta_h = beta[:, h:h + 1]
            qd_ref[:, h * HEAD_DIM:(h + 1) * HEAD_DIM] = (q * egc[:, h:h + 1]).astype(BF16)
            ks.append(k)
            kbs.append(k * beta_h)
            qs.append(q)
            kds.append(k * kfac[:, h:h + 1])
            rhss.append(jnp.concatenate([v * beta_h, k * bege[:, h:h + 1]], axis=1))
        h0, h1 = 2 * p, 2 * p + 1
        gch = jnp.where(lane_t, gc[:, h0:h0 + 1], gc[:, h1:h1 + 1])
        for ci in range(nc):
            rows = slice(ci * c, (ci + 1) * c)
            wk = jnp.concatenate([ks[0][rows], ks[1][rows]], axis=0).astype(BF16)
            lhs = jnp.concatenate([kbs[0][rows], qs[0][rows], kbs[1][rows], qs[1][rows]],
                                  axis=0).astype(BF16)
            out = _dot_nt(lhs, wk)
            gcrow = jnp.concatenate([gct[h0:h0 + 1, rows], gct[h1:h1 + 1, rows]], axis=1)
            diff = gch[rows] - gcrow
            decay = jnp.where(causal, jnp.exp(jnp.where(causal, diff, 0.0)), 0.0)
            a_pairs[ci][p] = jnp.where(strict, jnp.where(lane_lo, out[0:c], out[2 * c:3 * c]) * decay, 0.0)
            qk = jnp.where(lane_lo, out[c:2 * c], out[3 * c:4 * c]) * decay
            kdt = jnp.concatenate([kds[0][rows], kds[1][rows]], axis=0).T
            m0 = ci * 3 * c
            m2_ref[m0:m0 + c, p * LANES:(p + 1) * LANES] = qk.astype(BF16)
            m2_ref[m0 + c:m0 + 3 * c, p * LANES:(p + 1) * LANES] = kdt.astype(BF16)
            rhs_pairs[ci][p] = (rhss[0][rows], rhss[1][rows])

    zeros = jnp.zeros((c, 2 * HEAD_DIM), BF16)
    ngrp = GDN_HEADS // 4
    tls = _unit_lower_inverse4(
        [jnp.concatenate([a_pairs[ci][2 * grp], a_pairs[ci][2 * grp + 1]], axis=1)
         for ci in range(nc) for grp in range(ngrp)])
    for ci in range(nc):
        rows = slice(ci * c, (ci + 1) * c)
        for grp in range(ngrp):
            tl = tls[ci * ngrp + grp]
            for j in range(2):
                p = 2 * grp + j
                r0, r1 = rhs_pairs[ci][p]
                bd = jnp.concatenate([jnp.concatenate([r0.astype(BF16), zeros], axis=1),
                                      jnp.concatenate([zeros, r1.astype(BF16)], axis=1)], axis=0)
                sol = _dot(tl[:, j * LANES:(j + 1) * LANES].astype(BF16), bd)
                for i in range(2):
                    h = 2 * p + i
                    lo, hi = h * HEAD_DIM, (h + 1) * HEAD_DIM
                    u_ref[rows, lo:hi] = sol[:, 2 * i * HEAD_DIM:(2 * i + 1) * HEAD_DIM]
                    w_ref[rows, lo:hi] = sol[:, (2 * i + 1) * HEAD_DIM:(2 * i + 2) * HEAD_DIM].astype(BF16)


def _gdn_intra(proj, x, wa_hi, wa_lo, wb_hi, wb_lo, conv_w, prm, seq, rt):
    t = x.shape[0]
    nc = rt // CHUNK
    kern = functools.partial(_gdn_intra_kernel, rt=rt, tiles_per_seq=seq // rt)
    full = lambda i: (0, 0)
    tile = lambda i: (i, 0)
    sel, sel_halo = _shift_selectors(rt)
    return pl.pallas_call(
        kern,
        grid=(t // rt,),
        in_specs=[
            pl.BlockSpec((rt, 3 * GDN_WIDTH), tile),
            pl.BlockSpec((CONV_HALO, 3 * GDN_WIDTH),
                         lambda i: (jnp.maximum(i * (rt // CONV_HALO) - 1, 0), 0)),
            pl.BlockSpec((rt, D_MODEL), tile),
            pl.BlockSpec(sel.shape, full),
            pl.BlockSpec(sel_halo.shape, full),
            pl.BlockSpec((D_MODEL, LANES), full),
            pl.BlockSpec((D_MODEL, LANES), full),
            pl.BlockSpec((D_MODEL, LANES), full),
            pl.BlockSpec((D_MODEL, LANES), full),
            pl.BlockSpec((SHORT_CONV, 3 * GDN_WIDTH), full),
            pl.BlockSpec((8, LANES), full),
        ],
        out_specs=[pl.BlockSpec((rt, GDN_WIDTH), tile),
                   pl.BlockSpec((rt, GDN_WIDTH), tile),
                   pl.BlockSpec((rt, GDN_WIDTH), tile),
                   pl.BlockSpec((nc * 3 * CHUNK, GDN_HEADS // 2 * LANES), tile),
                   pl.BlockSpec((nc * GDN_HEADS, LANES), tile)],
        out_shape=[jax.ShapeDtypeStruct((t, GDN_WIDTH), F32),
                   jax.ShapeDtypeStruct((t, GDN_WIDTH), BF16),
                   jax.ShapeDtypeStruct((t, GDN_WIDTH), BF16),
                   jax.ShapeDtypeStruct((t // CHUNK * 3 * CHUNK, GDN_HEADS // 2 * LANES), BF16),
                   jax.ShapeDtypeStruct((t // CHUNK * GDN_HEADS, LANES), F32)],
        scratch_shapes=[pltpu.VMEM((SHORT_CONV, rt, 3 * GDN_WIDTH), F32)],
        compiler_params=_params("parallel"),
        name="gdn_intra",
    )(proj, proj, x, sel, sel_halo, wa_hi, wa_lo, wb_hi, wb_lo, conv_w, prm)


def _gdn_scan_kernel(u_ref, w_ref, qd_ref, m2_ref, eg_ref, z_ref, ong_ref, o_ref, s_ref, *, nck):
    c = CHUNK

    @pl.when(pl.program_id(1) == 0)
    def _():
        s_ref[...] = jnp.zeros_like(s_ref)

    ong = ong_ref[...]
    zeros = jnp.zeros((c, HEAD_DIM), BF16)
    heads = range(GDN_HEADS)
    span = lambda h: slice(h * HEAD_DIM, (h + 1) * HEAD_DIM)
    states = [s_ref[h] for h in heads]
    for ci in range(nck):
        rows = slice(ci * c, (ci + 1) * c)
        rs = [_dot(jnp.concatenate([w_ref[rows, span(h)], qd_ref[rows, span(h)]], axis=0),
                   states[h].astype(BF16)) for h in heads]
        v_new = [(u_ref[rows, span(h)] - rs[h][:c]).astype(BF16) for h in heads]
        r2s = []
        for p in range(GDN_HEADS // 2):
            bd = jnp.concatenate([jnp.concatenate([v_new[2 * p], zeros], axis=1),
                                  jnp.concatenate([zeros, v_new[2 * p + 1]], axis=1)], axis=0)
            r2s.append(_dot(m2_ref[ci * 3 * c:(ci + 1) * 3 * c, p * LANES:(p + 1) * LANES], bd))
        for h in heads:
            half = span(h % 2)
            r2 = r2s[h // 2]
            states[h] = states[h] * eg_ref[ci * GDN_HEADS + h:ci * GDN_HEADS + h + 1, :] + r2[c:, half]
            o = rs[h][c:] + r2[:c, half]
            o = o * lax.rsqrt(jnp.mean(o * o, axis=-1, keepdims=True) + EPS) * ong
            o = o * _silu(z_ref[rows, span(h)].astype(F32))
            o_ref[rows, span(h)] = o.astype(o_ref.dtype)
    for h in heads:
        s_ref[h] = states[h]


def _gdn_scan(u, w, qd, m2, eg, proj, ong, batch, seq):
    t = batch * seq
    nck = min(SCAN_CHUNKS, seq // CHUNK)
    c = nck * CHUNK
    nch = seq // c
    blk = lambda b, n: (b * nch + n, 0)
    return pl.pallas_call(
        functools.partial(_gdn_scan_kernel, nck=nck),
        grid=(batch, nch),
        in_specs=[
            pl.BlockSpec((c, GDN_WIDTH), blk),
            pl.BlockSpec((c, GDN_WIDTH), blk),
            pl.BlockSpec((c, GDN_WIDTH), blk),
            pl.BlockSpec((3 * c, GDN_HEADS // 2 * LANES), blk),
            pl.BlockSpec((nck * GDN_HEADS, LANES), blk),
            pl.BlockSpec((c, GDN_WIDTH), lambda b, n: (b * nch + n, 3)),
            pl.BlockSpec((1, HEAD_DIM), lambda b, n: (0, 0)),
        ],
        out_specs=pl.BlockSpec((c, GDN_WIDTH), blk),
        out_shape=jax.ShapeDtypeStruct((t, GDN_WIDTH), BF16),
        scratch_shapes=[pltpu.VMEM((GDN_HEADS, HEAD_DIM, HEAD_DIM), F32)],
        compiler_params=_params("parallel", "arbitrary"),
        name="gdn_scan",
    )(u, w, qd, m2, eg, proj, ong)


def _pack_bf16_pairs(y):
    n = y.shape[1] // 2
    yb = y.astype(BF16).astype(F32)
    lo = lax.bitcast_convert_type(yb[:, :n], U32)
    hi = lax.bitcast_convert_type(yb[:, n:], U32)
    return jnp.bitwise_or(jnp.right_shift(lo, 16), jnp.bitwise_and(hi, HI_MASK))


def _unpack_bf16_pairs(w):
    lo = lax.bitcast_convert_type(jnp.left_shift(w, 16), F32)
    hi = lax.bitcast_convert_type(jnp.bitwise_and(w, HI_MASK), F32)
    return lo, hi


def _mixer_kernel(glu_ref, ga_ref, gb_ref, o_ref, x_ref, woa_ref, wob_ref, wo_ref, dww_ref, vec_ref,
                  x1_ref, x1b_ref, x1p_ref, ubuf_ref, sh_ref, conv_ref, *, ts, rc):
    halo = CONF_HALO

    @pl.when(pl.program_id(1) == 0)
    def _():
        ubuf_ref[0:halo, :] = jnp.zeros((halo, CONF_CH), F32)

    @pl.when(pl.program_id(1) != 0)
    def _():
        ubuf_ref[0:halo, :] = ubuf_ref[ts:ts + halo, :]

    glu_a = glu_ref[:, :CONF_CH].astype(F32)
    glu_b = glu_ref[:, CONF_CH:].astype(F32)
    ubuf_ref[halo:halo + ts, :] = glu_a * _sigmoid(glu_b)

    dw_b = vec_ref[0:1, :]
    cln_g = vec_ref[1:2, :]
    cln_b = vec_ref[2:3, :]
    ln1_g = vec_ref[3:4, :]
    ln1_b = vec_ref[4:5, :]

    span = ts + halo - 8
    for s in range(1, 8):
        sh_ref[s - 1] = ubuf_ref[s:s + span, :]

    def tap_rows(j, r0):
        o = halo - (CONF_KERNEL - 1) + j
        q, s = o // 8, o % 8
        if s == 0:
            return ubuf_ref[r0 + o:r0 + o + rc, :]
        return sh_ref[s - 1, r0 + 8 * q:r0 + 8 * q + rc, :]

    hs = ts // 2
    for h0 in range(0, ts, hs):
        rows = slice(h0, h0 + hs)
        gated_a = _sigmoid(ga_ref[rows, :].astype(F32)) * _dot(o_ref[rows, :], woa_ref[...])
        for r0 in range(h0, h0 + hs, rc):
            tap_w = lambda j: jnp.concatenate([dww_ref[j]] * (rc // 8), axis=0)
            acc = tap_w(0) * tap_rows(0, r0)
            for j in range(1, CONF_KERNEL):
                acc = acc + tap_w(j) * tap_rows(j, r0)
            conv_ref[r0:r0 + rc, :] = acc
        uc = _silu(_layer_norm(conv_ref[rows, :] + dw_b, cln_g, cln_b))
        branch_b = _dot(uc.astype(BF16), wob_ref[...])
        hmix = gated_a + _sigmoid(gb_ref[rows, :].astype(F32)) * branch_b
        mix = _dot(hmix.astype(BF16), wo_ref[...])
        x1 = _layer_norm(DEEPNORM_ALPHA * x_ref[rows, :] + mix, ln1_g, ln1_b)
        x1_ref[rows, :] = x1
        x1b_ref[rows, :] = x1.astype(BF16)
        x1p_ref[rows, :] = _pack_bf16_pairs(x1)


def _mixer(proj, o_gdn, x, woa, wob, wo, dww, vec, batch, seq, ts):
    t = batch * seq
    nt = seq // ts
    rows = lambda b, n: b * nt + n
    full = lambda b, n: (0, 0)
    kern = functools.partial(_mixer_kernel, ts=ts, rc=32)
    return pl.pallas_call(
        kern,
        grid=(batch, nt),
        in_specs=[
            pl.BlockSpec((ts, 2 * CONF_CH), lambda b, n: (rows(b, n), 2)),
            pl.BlockSpec((ts, D_MODEL), lambda b, n: (rows(b, n), 6)),
            pl.BlockSpec((ts, D_MODEL), lambda b, n: (rows(b, n), 7)),
            pl.BlockSpec((ts, GDN_WIDTH), lambda b, n: (rows(b, n), 0)),
            pl.BlockSpec((ts, D_MODEL), lambda b, n: (rows(b, n), 0)),
            pl.BlockSpec((GDN_WIDTH, D_MODEL), full),
            pl.BlockSpec((CONF_CH, D_MODEL), full),
            pl.BlockSpec((D_MODEL, D_MODEL), full),
            pl.BlockSpec((CONF_KERNEL, 8, CONF_CH), lambda b, n: (0, 0, 0)),
            pl.BlockSpec((8, D_MODEL), full),
        ],
        out_specs=[pl.BlockSpec((ts, D_MODEL), lambda b, n: (rows(b, n), 0)),
                   pl.BlockSpec((ts, D_MODEL), lambda b, n: (rows(b, n), 0)),
                   pl.BlockSpec((ts, D_MODEL // 2), lambda b, n: (rows(b, n), 0))],
        out_shape=[jax.ShapeDtypeStruct((t, D_MODEL), F32),
                   jax.ShapeDtypeStruct((t, D_MODEL), BF16),
                   jax.ShapeDtypeStruct((t, D_MODEL // 2), U32)],
        scratch_shapes=[pltpu.VMEM((CONF_HALO + ts, CONF_CH), F32),
                        pltpu.VMEM((7, ts + CONF_HALO - 8, CONF_CH), F32),
                        pltpu.VMEM((ts, CONF_CH), F32)],
        compiler_params=_params("parallel", "arbitrary"),
        name="mixer",
    )(proj, proj, proj, o_gdn, x, woa, wob, wo, dww, vec)


def _route_kernel(x_ref, wrh_ref, wrl_ref, bias_ref, eidx_ref, wts_ref, rank_ref, cnt_ref, carry_ref, *, tt):
    @pl.when(pl.program_id(0) == 0)
    def _():
        carry_ref[...] = jnp.zeros_like(carry_ref)

    xh, xl = _split(x_ref[...])
    wrh = wrh_ref[...]
    logits = _dot_nt(wrh, xh) + _dot_nt(wrh, xl) + _dot_nt(wrl_ref[...], xh)
    s = _sigmoid(logits)
    biased = s + bias_ref[...]

    sub = lax.broadcasted_iota(I32, (GROUP_SIZE, tt), 0)
    groups = [biased[g * GROUP_SIZE:(g + 1) * GROUP_SIZE, :] for g in range(N_GROUPS)]
    gs = []
    for bg in groups:
        m1 = jnp.max(bg, axis=0, keepdims=True)
        first = jnp.min(jnp.where(bg == m1, sub, GROUP_SIZE), axis=0, keepdims=True)
        m2 = jnp.max(jnp.where(sub == first, -jnp.inf, bg), axis=0, keepdims=True)
        gs.append(m1 + m2)

    masked_parts = []
    for g in range(N_GROUPS):
        beaten = jnp.zeros((1, tt), I32)
        for o in range(N_GROUPS):
            if o == g:
                continue
            wins = (gs[o] >= gs[g]) if o < g else (gs[o] > gs[g])
            beaten = beaten + wins.astype(I32)
        keep = jnp.broadcast_to(beaten < TOPK_GROUPS, (GROUP_SIZE, tt))
        masked_parts.append(jnp.where(keep, groups[g], -jnp.inf))
    masked = jnp.concatenate(masked_parts, axis=0)

    eiota = lax.broadcasted_iota(I32, (N_EXPERTS, tt), 0)
    sel_all = jnp.zeros((N_EXPERTS, tt), F32)
    picks = []
    for _ in range(TOP_K):
        m = jnp.max(masked, axis=0, keepdims=True)
        idx = jnp.min(jnp.where(masked == m, eiota, N_EXPERTS), axis=0, keepdims=True)
        onehot = eiota == idx
        picks.append((idx, onehot))
        sel_all = jnp.where(onehot, 1.0, sel_all)
        masked = jnp.where(onehot, -jnp.inf, masked)

    tr = lax.broadcasted_iota(I32, (tt, tt), 0)
    tc = lax.broadcasted_iota(I32, (tt, tt), 1)
    before = (tr < tc).astype(BF16)
    sel_b = sel_all.astype(BF16)
    carry = carry_ref[...]
    rank_all = _dot(sel_b, before) + carry[:, 0:1]
    carry_new = carry + _dot(sel_b, jnp.ones((tt, LANES), BF16))
    carry_ref[...] = carry_new
    cnt_ref[...] = carry_new

    s_sel = [jnp.sum(jnp.where(oh, s, 0.0), axis=0, keepdims=True) for _, oh in picks]
    total = s_sel[0]
    for v in s_sel[1:]:
        total = total + v
    for k, (idx, oh) in enumerate(picks):
        eidx_ref[k:k + 1, :] = idx
        wts_ref[k:k + 1, :] = s_sel[k] / total * ROUTED_SCALE
        rank_ref[k:k + 1, :] = jnp.sum(jnp.where(oh, rank_all, 0.0), axis=0, keepdims=True).astype(I32)


def _route(x1, wr_hi, wr_lo, bias, tt):
    t = x1.shape[0]
    kern = functools.partial(_route_kernel, tt=tt)
    return pl.pallas_call(
        kern,
        grid=(t // tt,),
        in_specs=[pl.BlockSpec((tt, D_MODEL), lambda i: (i, 0)),
                  pl.BlockSpec((N_EXPERTS, D_MODEL), lambda i: (0, 0)),
                  pl.BlockSpec((N_EXPERTS, D_MODEL), lambda i: (0, 0)),
                  pl.BlockSpec((N_EXPERTS, tt), lambda i: (0, 0))],
        out_specs=[pl.BlockSpec((TOP_K, tt), lambda i: (0, i)),
                   pl.BlockSpec((TOP_K, tt), lambda i: (0, i)),
                   pl.BlockSpec((TOP_K, tt), lambda i: (0, i)),
                   pl.BlockSpec((N_EXPERTS, LANES), lambda i: (0, 0))],
        out_shape=[jax.ShapeDtypeStruct((TOP_K, t), I32),
                   jax.ShapeDtypeStruct((TOP_K, t), F32),
                   jax.ShapeDtypeStruct((TOP_K, t), I32),
                   jax.ShapeDtypeStruct((N_EXPERTS, LANES), F32)],
        scratch_shapes=[pltpu.VMEM((N_EXPERTS, LANES), F32)],
        compiler_params=_params("arbitrary"),
        name="route",
    )(x1, wr_hi, wr_lo, bias)


def _plan_kernel(cnt_ref, pstart_ref, plan_ref):
    e, nb = N_EXPERTS, plan_ref.shape[1]
    counts = cnt_ref[...]
    nblk = jnp.floor((counts + (ROW_BLOCK - 1)) * (1.0 / ROW_BLOCK))
    hi = jnp.floor(nblk * (1.0 / 256.0))
    lo = nblk - 256.0 * hi
    r = lax.broadcasted_iota(I32, (e, e), 0)
    c = lax.broadcasted_iota(I32, (e, e), 1)
    ltri = (r >= c).astype(BF16)
    bend = 256.0 * _dot(ltri, hi.astype(BF16)) + _dot(ltri, lo.astype(BF16))
    pend = bend * ROW_BLOCK
    pstart = pend - nblk * ROW_BLOCK
    pstart_ref[...] = pstart.astype(I32)

    bs = (lax.broadcasted_iota(I32, (e, nb), 1) * ROW_BLOCK).astype(F32)
    pend_b = jnp.broadcast_to(pend[:, 0:1], (e, nb))
    pstart_b = jnp.broadcast_to(pstart[:, 0:1], (e, nb))
    used_b = jnp.broadcast_to((pstart + counts)[:, 0:1], (e, nb))
    owner = jnp.sum(jnp.where(pend_b <= bs, 1.0, 0.0), axis=0, keepdims=True)
    inside = jnp.where(pstart_b <= bs, jnp.where(bs < pend_b, 1.0, 0.0), 0.0)
    real = jnp.sum(inside * jnp.clip(used_b - bs, 0.0, float(ROW_BLOCK)), axis=0, keepdims=True)
    plan_ref[0:1, :] = jnp.minimum(owner, float(e - 1)).astype(I32)
    plan_ref[1:2, :] = real.astype(I32)
    plan_ref[2:8, :] = jnp.zeros((6, nb), I32)


def _plan(cnt, n_blocks):
    nb = -(-n_blocks // LANES) * LANES
    pstart, plan = pl.pallas_call(
        _plan_kernel,
        out_shape=[jax.ShapeDtypeStruct((N_EXPERTS, LANES), I32),
                   jax.ShapeDtypeStruct((8, nb), I32)],
        name="plan",
    )(cnt)
    return pstart[:, 0], plan[0, :n_blocks], plan[1, :n_blocks]


def _dest_kernel(eidx_ref, rank_ref, pstart_ref, dest_ref):
    eidx = eidx_ref[...]
    acc = rank_ref[...]
    for e in range(N_EXPERTS):
        acc = acc + jnp.where(eidx == e, pstart_ref[e], 0)
    dest_ref[...] = acc


def _dest(eidx_t, rank_t, pstart, tt):
    t = eidx_t.shape[1]
    return pl.pallas_call(
        _dest_kernel,
        grid=(t // tt,),
        in_specs=[pl.BlockSpec((TOP_K, tt), lambda i: (0, i)),
                  pl.BlockSpec((TOP_K, tt), lambda i: (0, i)),
                  pl.BlockSpec(memory_space=pltpu.SMEM)],
        out_specs=pl.BlockSpec((TOP_K, tt), lambda i: (0, i)),
        out_shape=jax.ShapeDtypeStruct((TOP_K, t), I32),
        compiler_params=_params("parallel"),
        name="dest",
    )(eidx_t, rank_t, pstart)


def _dispatch(dest_kt, x1p, n_rows):
    t, width = x1p.shape
    info = plsc.get_sparse_core_info()
    nc, nw = info.num_cores, info.num_cores * info.num_subcores
    chunk = SC_SCATTER_ROWS
    per_w = t // nw
    n_chunks = per_w // chunk
    assert per_w % chunk == 0
    idx = dest_kt.reshape(TOP_K, nw, n_chunks, chunk).transpose(1, 2, 0, 3).reshape(nw, n_chunks * TOP_K, chunk)
    mesh = plsc.VectorSubcoreMesh(core_axis_name="c", subcore_axis_name="s")

    @functools.partial(
        pl.kernel, mesh=mesh, name="dispatch",
        out_type=jax.ShapeDtypeStruct((n_rows, width), x1p.dtype),
        scratch_types=[pltpu.VMEM((n_chunks * TOP_K, chunk), I32),
                       pltpu.VMEM((chunk, width), x1p.dtype),
                       pltpu.SemaphoreType.DMA])
    def scatter(x_hbm, idx_hbm, xs_hbm, idx_v, rows_v, sem):
        wid = lax.axis_index("s") * nc + lax.axis_index("c")
        base = wid * per_w
        pltpu.sync_copy(idx_hbm.at[wid], idx_v)
        for j in range(n_chunks):
            pltpu.sync_copy(x_hbm.at[pl.ds(base + j * chunk, chunk)], rows_v)
            copies = [pltpu.make_async_copy(rows_v, xs_hbm.at[idx_v.at[j * TOP_K + k]], sem)
                      for k in range(TOP_K)]
            for cp in copies:
                cp.start()
            for cp in copies:
                cp.wait()

    return scatter(x1p, idx)


def _gather_rows(table, idx):
    n = idx.shape[0]
    width = table.shape[1]
    info = plsc.get_sparse_core_info()
    nc, nw = info.num_cores, info.num_cores * info.num_subcores
    chunk = SC_GATHER_ROWS
    per_w = n // nw
    n_chunks = per_w // chunk
    assert per_w % (2 * chunk) == 0
    mesh = plsc.VectorSubcoreMesh(core_axis_name="c", subcore_axis_name="s")

    @functools.partial(
        pl.kernel, mesh=mesh, name="gather_rows",
        out_type=jax.ShapeDtypeStruct((n, width), table.dtype),
        scratch_types=[pltpu.VMEM((n_chunks, chunk), I32),
                       pltpu.VMEM((2, chunk, width), table.dtype),
                       pltpu.SemaphoreType.DMA((2,)),
                       pltpu.SemaphoreType.DMA((2,))])
    def gather(table_hbm, idx_hbm, out_hbm, idx_v, rows_v, gsem, osem):
        wid = lax.axis_index("s") * nc + lax.axis_index("c")
        base = wid * per_w
        pltpu.sync_copy(idx_hbm.at[wid], idx_v)

        def fetch(j, b):
            return pltpu.make_async_copy(table_hbm.at[idx_v.at[j]], rows_v.at[b], gsem.at[b])

        def put(j, b):
            return pltpu.make_async_copy(rows_v.at[b], out_hbm.at[pl.ds(base + j * chunk, chunk)], osem.at[b])

        fetch(0, 0).start()

        @pl.loop(0, n_chunks, step=2)
        def _(j0):
            for b in range(2):
                j = j0 + b
                fetch(j, b).wait()

                @pl.when(j + 1 < n_chunks)
                def _():
                    @pl.when(j >= 1)
                    def _():
                        put(j - 1, 1 - b).wait()

                    fetch(j + 1, 1 - b).start()

                put(j, b).start()

        put(n_chunks - 2, 0).wait()
        put(n_chunks - 1, 1).wait()

    return gather(table, idx.reshape(nw, n_chunks, chunk))


def _xs_copy(xs_hbm, xbuf, isem, j, slot):
    return pltpu.make_async_copy(xs_hbm.at[pl.ds(j * ROW_BLOCK, ROW_BLOCK)], xbuf.at[slot], isem.at[slot])


def _ys_copy(ybuf, ys_hbm, osem, j, slot):
    return pltpu.make_async_copy(ybuf.at[slot], ys_hbm.at[pl.ds(j * ROW_BLOCK, ROW_BLOCK)], osem.at[slot])


def _experts_kernel(be_ref, nv_ref, xs_hbm, wg_ref, wu_ref, wd_ref, ys_hbm,
                    xbuf, ybuf, wgu_s, wd_s, cur_ref, isem, osem):
    i = pl.program_id(0)
    n_valid = nv_ref[i]
    half = EXPERT_SUB
    slot = lax.rem(i, EXPERT_IN_SLOTS)
    oslot = lax.rem(i, 2)

    @pl.when(i == 0)
    def _():
        cur_ref[0] = -1
        for j in range(2):
            @pl.when(nv_ref[j] > 0)
            def _():
                _xs_copy(xs_hbm, xbuf, isem, j, j).start()

    @pl.when(nv_ref[i + 2] > 0)
    def _():
        _xs_copy(xs_hbm, xbuf, isem, i + 2, lax.rem(i + 2, EXPERT_IN_SLOTS)).start()

    @pl.when((i >= 2) & (nv_ref[jnp.maximum(i - 2, 0)] > 0))
    def _():
        _ys_copy(ybuf, ys_hbm, osem, i - 2, oslot).wait()

    @pl.when((n_valid > 0) & (cur_ref[0] != be_ref[i]))
    def _():
        wgu_s[:, :EXPERT_FF] = wg_ref[...].astype(BF16)
        wgu_s[:, EXPERT_FF:] = wu_ref[...].astype(BF16)
        wd_s[...] = wd_ref[...].astype(BF16)
        cur_ref[0] = be_ref[i]

    def rows_bf16(r0):
        valid = lax.broadcasted_iota(I32, (half, xbuf.shape[2]), 0) + r0 < n_valid
        lo, hi = _unpack_bf16_pairs(jnp.where(valid, xbuf[slot, r0:r0 + half, :], jnp.zeros((), U32)))
        return jnp.concatenate([lo.astype(BF16), hi.astype(BF16)], axis=1)

    def hidden(gu):
        return (_silu(gu[:, :EXPERT_FF]) * gu[:, EXPERT_FF:]).astype(BF16)

    @pl.when(n_valid > 0)
    def _():
        _xs_copy(xs_hbm, xbuf, isem, i, slot).wait()

    @pl.when(n_valid > half)
    def _():
        xa, xb = rows_bf16(0), rows_bf16(half)
        gua = _dot(xa, wgu_s[...])
        gub = _dot(xb, wgu_s[...])
        ya = _dot(hidden(gua), wd_s[...])
        yb = _dot(hidden(gub), wd_s[...])
        ybuf[oslot, 0:half, :] = _pack_bf16_pairs(ya)
        ybuf[oslot, half:, :] = _pack_bf16_pairs(yb)

    @pl.when((n_valid > 0) & (n_valid <= half))
    def _():
        ya = _dot(hidden(_dot(rows_bf16(0), wgu_s[...])), wd_s[...])
        ybuf[oslot, 0:half, :] = _pack_bf16_pairs(ya)
        ybuf[oslot, half:, :] = jnp.zeros((half, ybuf.shape[2]), ybuf.dtype)

    @pl.when(n_valid > 0)
    def _():
        _ys_copy(ybuf, ys_hbm, osem, i, oslot).start()


def _experts(block_e, n_valid, xs, wg, wu, wd, layer):
    n_rows, width = xs.shape
    steps = n_rows // ROW_BLOCK + 2
    be = jnp.concatenate([block_e, jnp.full((2,), N_EXPERTS - 1, I32)])
    nv = jnp.concatenate([n_valid, jnp.zeros((4,), I32)])
    grid_spec = pltpu.PrefetchScalarGridSpec(
        num_scalar_prefetch=2,
        grid=(steps,),
        in_specs=[pl.BlockSpec(memory_space=pl.ANY),
                  pl.BlockSpec((None, None, D_MODEL, EXPERT_FF), lambda i, be, nv: (layer, be[i], 0, 0)),
                  pl.BlockSpec((None, None, D_MODEL, EXPERT_FF), lambda i, be, nv: (layer, be[i], 0, 0)),
                  pl.BlockSpec((None, None, EXPERT_FF, D_MODEL), lambda i, be, nv: (layer, be[i], 0, 0))],
        out_specs=pl.BlockSpec(memory_space=pl.ANY),
        scratch_shapes=[pltpu.VMEM((EXPERT_IN_SLOTS, ROW_BLOCK, width), xs.dtype),
                        pltpu.VMEM((2, ROW_BLOCK, width), xs.dtype),
                        pltpu.VMEM((D_MODEL, 2 * EXPERT_FF), BF16),
                        pltpu.VMEM((EXPERT_FF, D_MODEL), BF16),
                        pltpu.SMEM((1,), I32),
                        pltpu.SemaphoreType.DMA((EXPERT_IN_SLOTS,)),
                        pltpu.SemaphoreType.DMA((2,))],
    )
    return pl.pallas_call(
        _experts_kernel,
        grid_spec=grid_spec,
        out_shape=jax.ShapeDtypeStruct((n_rows, width), xs.dtype),
        compiler_params=_params("arbitrary"),
        name="experts",
    )(be, nv, xs, wg, wu, wd)


def _shared_kernel(x1b_ref, wsg_ref, wsu_ref, wsd_ref, anchor_ref, sh_ref):
    del anchor_ref
    xb = x1b_ref[...]
    hid = (_silu(_dot(xb, wsg_ref[...])) * _dot(xb, wsu_ref[...])).astype(BF16)
    sh_ref[...] = _dot(hid, wsd_ref[...])


def _shared(x1b, wsg, wsu, wsd, anchor, tt, row0, nrows):
    full = lambda i: (0, 0)
    first = row0 // tt
    return pl.pallas_call(
        _shared_kernel,
        grid=(nrows // tt,),
        in_specs=[pl.BlockSpec((tt, D_MODEL), lambda i: (first + i, 0)),
                  pl.BlockSpec((D_MODEL, SHARED_FF), full),
                  pl.BlockSpec((D_MODEL, SHARED_FF), full),
                  pl.BlockSpec((SHARED_FF, D_MODEL), full),
                  pl.BlockSpec(memory_space=pl.ANY)],
        out_specs=pl.BlockSpec((tt, D_MODEL), lambda i: (i, 0)),
        out_shape=jax.ShapeDtypeStruct((nrows, D_MODEL), F32),
        compiler_params=_params("parallel"),
        name="shared",
    )(x1b, wsg, wsu, wsd, anchor)


def _combine_kernel(yg_ref, w_ref, x1_ref, sha_ref, shb_ref, vec_ref, x2_ref, *, half_steps):
    in_first = pl.program_id(0) < half_steps
    shared = jnp.where(in_first, sha_ref[...], shb_ref[...])
    w = w_ref[...]
    half = D_MODEL // 2
    acc_lo, acc_hi = shared[:, :half], shared[:, half:]
    for k in range(TOP_K):
        lo, hi = _unpack_bf16_pairs(yg_ref[k])
        acc_lo = acc_lo + w[:, k:k + 1] * lo
        acc_hi = acc_hi + w[:, k:k + 1] * hi
    acc = jnp.concatenate([acc_lo, acc_hi], axis=1)
    x2 = _layer_norm(DEEPNORM_ALPHA * x1_ref[...] + acc, vec_ref[0:1, :], vec_ref[1:2, :])
    x2_ref[...] = x2


def _combine(yg, w_tok, x1, shared_a, shared_b, vec, tt):
    t = x1.shape[0]
    half_steps = shared_a.shape[0] // tt
    full = lambda i: (0, 0)
    return pl.pallas_call(
        functools.partial(_combine_kernel, half_steps=half_steps),
        grid=(t // tt,),
        in_specs=[pl.BlockSpec((TOP_K, tt, yg.shape[2]), lambda i: (0, i, 0)),
                  pl.BlockSpec((tt, TOP_K), lambda i: (i, 0)),
                  pl.BlockSpec((tt, D_MODEL), lambda i: (i, 0)),
                  pl.BlockSpec((tt, D_MODEL), lambda i: (jnp.minimum(i, half_steps - 1), 0)),
                  pl.BlockSpec((tt, D_MODEL), lambda i: (jnp.maximum(i - half_steps, 0), 0)),
                  pl.BlockSpec((8, D_MODEL), full)],
        out_specs=pl.BlockSpec((tt, D_MODEL), lambda i: (i, 0)),
        out_shape=jax.ShapeDtypeStruct((t, D_MODEL), F32),
        compiler_params=_params("parallel"),
        name="combine",
    )(yg, w_tok, x1, shared_a, shared_b, vec)


def _pad_rows(a, rows):
    return jnp.zeros((rows, a.shape[-1]), F32).at[:a.shape[0]].set(a.astype(F32))


def _layer(x, p, stacked, layer, batch, seq):
    t = batch * seq
    gw = GDN_WIDTH
    w_in = p["w_in"]
    c_z, c_a, c_glu = 3 * gw, 4 * gw, 4 * gw + 2 * GDN_HEADS
    c_ga = c_glu + 2 * CONF_CH
    c_gb = c_ga + D_MODEL
    w_main = jnp.concatenate([w_in[:, :c_a], w_in[:, c_glu:]], axis=1).astype(BF16)
    w_a = jnp.zeros((D_MODEL, LANES), F32).at[:, :GDN_HEADS].set(w_in[:, c_a:c_a + GDN_HEADS])
    w_b = jnp.zeros((D_MODEL, LANES), F32).at[:, :GDN_HEADS].set(w_in[:, c_a + GDN_HEADS:c_glu])
    wa_hi, wb_hi = w_a.astype(BF16), w_b.astype(BF16)
    wa_lo = (w_a - wa_hi.astype(F32)).astype(BF16)
    wb_lo = (w_b - wb_hi.astype(F32)).astype(BF16)
    del c_z, c_gb

    tm = min(1024, t)
    proj = _proj(x, w_main, tm, 2048)

    prm = jnp.zeros((8, LANES), F32)
    prm = prm.at[0, :GDN_HEADS].set(p["a_log"]).at[1, :GDN_HEADS].set(p["dt_bias"])
    ong = p["o_norm_g"].reshape(1, HEAD_DIM).astype(F32)
    u, w, qd, m2, eg = _gdn_intra(proj, x, wa_hi, wa_lo, wb_hi, wb_lo, p["conv_qkv"].astype(F32), prm,
                                  seq, min(GDN_TILE, seq))
    o_gdn = _gdn_scan(u, w, qd, m2, eg, proj, ong, batch, seq)

    ts = min(256, seq)
    dww = jnp.broadcast_to(p["dw_w"].astype(F32)[:, None, :], (CONF_KERNEL, 8, CONF_CH))
    vec = _pad_rows(jnp.stack([p["dw_b"], p["cln_g"], p["cln_b"], p["ln1_g"], p["ln1_b"]]), 8)
    x1, x1b, x1p = _mixer(proj, o_gdn, x, p["w_oa"].astype(BF16), p["w_ob"].astype(BF16),
                          p["w_o"].astype(BF16), dww, vec, batch, seq, ts)

    tt_r = min(256, t)
    wr_t = p["w_router"].T.astype(F32)
    wr_hi = wr_t.astype(BF16)
    wr_lo = (wr_t - wr_hi.astype(F32)).astype(BF16)
    bias = jnp.broadcast_to(p["router_bias"].astype(F32)[:, None], (N_EXPERTS, tt_r))
    eidx_t, wts_t, rank_t, cnt = _route(x1, wr_hi, wr_lo, bias, tt_r)

    n_blocks = -(-(t * TOP_K + N_EXPERTS * (ROW_BLOCK - 1)) // ROW_BLOCK)
    n_rows = n_blocks * ROW_BLOCK
    pstart, block_e, n_valid = _plan(cnt, n_blocks)

    dest = _dest(eidx_t, rank_t, pstart, min(2048, t))
    sh_w = (p["w_sh_gate"].astype(BF16), p["w_sh_up"].astype(BF16), p["w_sh_down"].astype(BF16))
    tt_s = min(512, t // 2)
    xs = _dispatch(dest, x1p, n_rows)
    shared_a = _shared(x1b, *sh_w, dest, tt_s, 0, t // 2)
    ys = _experts(block_e, n_valid, xs, stacked["w_gate_e"], stacked["w_up_e"], stacked["w_down_e"], layer)
    yg = _gather_rows(ys, dest.reshape(TOP_K * t)).reshape(TOP_K, t, ys.shape[1])
    shared_b = _shared(x1b, *sh_w, ys, tt_s, t // 2, t // 2)
    vec2 = _pad_rows(jnp.stack([p["ln2_g"], p["ln2_b"]]), 8)
    return _combine(yg, wts_t.T, x1, shared_a, shared_b, vec2, min(256, t // 2))


_PARAM_NAMES = ("w_in", "conv_qkv", "a_log", "dt_bias", "o_norm_g", "w_oa", "dw_w", "dw_b", "cln_g",
                "cln_b", "w_ob", "w_o", "ln1_g", "ln1_b", "w_router", "router_bias", "w_gate_e",
                "w_up_e", "w_down_e", "w_sh_gate", "w_sh_up", "w_sh_down", "ln2_g", "ln2_b")


_EXPERT_WEIGHTS = ("w_gate_e", "w_up_e", "w_down_e")


def kernel(x, w_in, conv_qkv, a_log, dt_bias, o_norm_g, w_oa, dw_w, dw_b, cln_g, cln_b, w_ob, w_o,
           ln1_g, ln1_b, w_router, router_bias, w_gate_e, w_up_e, w_down_e, w_sh_gate, w_sh_up,
           w_sh_down, ln2_g, ln2_b):
    stacked = dict(zip(_PARAM_NAMES, (w_in, conv_qkv, a_log, dt_bias, o_norm_g, w_oa, dw_w, dw_b, cln_g,
                                      cln_b, w_ob, w_o, ln1_g, ln1_b, w_router, router_bias, w_gate_e,
                                      w_up_e, w_down_e, w_sh_gate, w_sh_up, w_sh_down, ln2_g, ln2_b)))
    batch, seq, d = x.shape
    assert d == D_MODEL and seq % CHUNK == 0
    xf = x.reshape(batch * seq, d).astype(F32)
    for layer in range(w_in.shape[0]):
        p = {name: arr[layer] for name, arr in stacked.items() if name not in _EXPERT_WEIGHTS}
        xf = _layer(xf, p, stacked, layer, batch, seq)
    return xf.reshape(batch, seq, d).astype(x.dtype)
```

```python
import functools

import jax
import jax.numpy as jnp
import numpy as np
from jax import lax
from jax.experimental import pallas as pl
from jax.experimental.pallas import tpu as pltpu
from jax.experimental.pallas import tpu_sc as plsc

F32 = jnp.float32
BF16 = jnp.bfloat16
I32 = jnp.int32
U32 = jnp.uint32
HI_MASK = np.uint32(0xFFFF0000)

D_MODEL = 1024
GDN_HEADS = 8
HEAD_DIM = 128
GDN_WIDTH = GDN_HEADS * HEAD_DIM
SHORT_CONV = 4
CONV_HALO = 16
CHUNK = 64
SOLVE_BLOCK = 16
GDN_TILE = 256
SCAN_CHUNKS = 4
CONF_CH = D_MODEL
CONF_KERNEL = 31
CONF_HALO = 32
N_EXPERTS = 64
TOP_K = 8
N_GROUPS = 8
GROUP_SIZE = N_EXPERTS // N_GROUPS
TOPK_GROUPS = 4
EXPERT_FF = 256
SHARED_FF = 256
ROUTED_SCALE = 2.5
DEPTH = 2
DEEPNORM_ALPHA = (2 * DEPTH) ** 0.25
EPS = 1e-6

LANES = 128
PROJ_COLS = 8 * D_MODEL
ROW_BLOCK = 512
EXPERT_SUB = ROW_BLOCK // 2
EXPERT_IN_SLOTS = 3
SC_SCATTER_ROWS = 128
SC_GATHER_ROWS = 64
VMEM_LIMIT = 56 * 1024 * 1024


def _params(*sem):
    return pltpu.CompilerParams(dimension_semantics=sem, vmem_limit_bytes=VMEM_LIMIT)


def _dot(a, b):
    return jnp.dot(a, b, preferred_element_type=F32)


def _dot_nt(a, b):
    return lax.dot_general(a, b, (((1,), (1,)), ((), ())), preferred_element_type=F32)


def _split(a):
    hi = a.astype(BF16)
    lo = (a - hi.astype(F32)).astype(BF16)
    return hi, lo


def _dot3(a, b):
    ah, al = _split(a)
    bh, bl = _split(b)
    return _dot(ah, bh) + _dot(al, bh) + _dot(ah, bl)


def _sigmoid(x):
    return 1.0 / (1.0 + jnp.exp(-x))


def _silu(x):
    return x * _sigmoid(x)


def _layer_norm(y, g, b):
    mu = jnp.mean(y, axis=-1, keepdims=True)
    yc = y - mu
    var = jnp.mean(yc * yc, axis=-1, keepdims=True)
    return yc * lax.rsqrt(var + EPS) * g + b


def _proj_kernel(a_ref, w_ref, o_ref):
    o_ref[...] = _dot(a_ref[...].astype(BF16), w_ref[...]).astype(o_ref.dtype)


def _regroup_kernel(wl_ref, wr_ref, wt_ref, main_ref, ahi_ref, alo_ref, bhi_ref, blo_ref):
    half = PROJ_COLS // 2
    skip = 2 * GDN_HEADS
    main_ref[:, :half] = wl_ref[...].astype(BF16)
    right = jnp.concatenate([wr_ref[...], wt_ref[...]], axis=1)
    main_ref[:, half:] = right[:, skip:skip + half].astype(BF16)
    lane = lax.broadcasted_iota(I32, (wl_ref.shape[0], LANES), 1)
    for off, hi_ref, lo_ref in ((0, ahi_ref, alo_ref), (GDN_HEADS, bhi_ref, blo_ref)):
        w = jnp.where(lane < GDN_HEADS, right[:, off:off + LANES], 0.0)
        hi, lo = _split(w)
        hi_ref[...] = hi
        lo_ref[...] = lo


def _regroup(w_in, layer, tr):
    d = w_in.shape[1]
    half = PROJ_COLS // 2
    small = jax.ShapeDtypeStruct((d, LANES), BF16)
    row = lambda i: (i, 0)
    return pl.pallas_call(
        _regroup_kernel,
        grid=(d // tr,),
        in_specs=[pl.BlockSpec((None, tr, half), lambda i: (layer, i, 0)),
                  pl.BlockSpec((None, tr, half), lambda i: (layer, i, 1)),
                  pl.BlockSpec((None, tr, LANES), lambda i: (layer, i, 2 * half // LANES))],
        out_specs=[pl.BlockSpec((tr, PROJ_COLS), row)] + [pl.BlockSpec((tr, LANES), row)] * 4,
        out_shape=[jax.ShapeDtypeStruct((d, PROJ_COLS), BF16), small, small, small, small],
        compiler_params=_params("parallel"),
        name="regroup",
    )(w_in, w_in, w_in)


def _proj(xb, w, tm, tn):
    m, k = xb.shape
    n = w.shape[1]
    return pl.pallas_call(
        _proj_kernel,
        grid=(m // tm, n // tn),
        in_specs=[pl.BlockSpec((tm, k), lambda i, j: (i, 0)),
                  pl.BlockSpec((k, tn), lambda i, j: (0, j))],
        out_specs=pl.BlockSpec((tm, tn), lambda i, j: (i, j)),
        out_shape=jax.ShapeDtypeStruct((m, n), BF16),
        compiler_params=_params("parallel", "parallel"),
        name="proj",
    )(xb, w)


def _unit_lower_inverse4(als):
    c = CHUNK
    n = als[0].shape[1]
    row = lax.broadcasted_iota(I32, (c, n), 0)
    col = jnp.bitwise_and(lax.broadcasted_iota(I32, (c, n), 1), c - 1)
    shift = SOLVE_BLOCK.bit_length() - 1
    same = jnp.right_shift(row, shift) == jnp.right_shift(col, shift)
    eye = (row == col).astype(F32)
    cshift = c.bit_length() - 1
    brow = jnp.right_shift(lax.broadcasted_iota(I32, (n, n), 0), cshift)
    bcol = jnp.right_shift(lax.broadcasted_iota(I32, (n, n), 1), cshift)
    on_diag = brow == bcol

    def mm(x, y):
        yb = y.astype(BF16)
        bd = jnp.where(on_diag, jnp.concatenate([yb] * (n // c), axis=0), jnp.zeros((), BF16))
        return _dot(x.astype(BF16), bd)

    a_diag = [jnp.where(same, al, 0.0) for al in als]
    a_off = [al - ad for al, ad in zip(als, a_diag)]
    bp = [-ad for ad in a_diag]
    p = [eye + b for b in bp]
    for _ in range(3):
        bp = [mm(b, b) for b in bp]
        p = [x + mm(x, b) for x, b in zip(p, bp)]
    n1 = [mm(x, ao) for x, ao in zip(p, a_off)]
    n2 = [mm(x, x) for x in n1]
    q = [x + mm(y, x) for x, y in zip(p, n2)]
    return [x - mm(y, x) for x, y in zip(q, n1)]


def _shift_selectors(rt):
    sel = np.zeros((SHORT_CONV * rt, rt), np.float32)
    sel_halo = np.zeros((SHORT_CONV * 8, CONV_HALO), np.float32)
    for d in range(SHORT_CONV):
        for t in range(d, rt):
            sel[d * rt + t, t - d] = 1.0
        for t in range(d):
            sel_halo[d * 8 + t, CONV_HALO + t - d] = 1.0
    return jnp.asarray(sel, BF16), jnp.asarray(sel_halo, BF16)


def _gdn_intra_kernel(qkv_ref, prev_ref, x_ref, sel_ref, selh_ref, wah_ref, wal_ref, wbh_ref, wbl_ref, cw_ref,
                      prm_ref, u_ref, w_ref, qd_ref, m2_ref, eg_ref, xs_ref, *, rt, tiles_per_seq):
    c = CHUNK
    nc = rt // c
    first = (pl.program_id(0) % tiles_per_seq) == 0
    edge = jnp.where(first, 0.0, _dot(selh_ref[...], prev_ref[...]))

    def move_rows(lo, hi):
        moved = _dot(sel_ref[...], qkv_ref[:, lo:hi])
        for d in range(SHORT_CONV):
            xs_ref[d, 0:8, lo:hi] = moved[d * rt:d * rt + 8] + edge[d * 8:(d + 1) * 8, lo:hi]
            xs_ref[d, 8:rt, lo:hi] = moved[d * rt + 8:(d + 1) * rt]

    for part in range(3):
        move_rows(part * GDN_WIDTH, part * GDN_WIDTH + 2 * HEAD_DIM)

    xh, xl = _split(x_ref[...])

    def proj3(wh_ref, wl_ref):
        wh = wh_ref[...]
        return _dot(xh, wh) + _dot(xl, wh) + _dot(xh, wl_ref[...])

    a_raw = proj3(wah_ref, wal_ref)
    b_raw = proj3(wbh_ref, wbl_ref)
    sp_in = a_raw + prm_ref[1:2, :]
    softplus = jnp.maximum(sp_in, 0.0) + jnp.log(1.0 + jnp.exp(-jnp.abs(sp_in)))
    g = -jnp.exp(prm_ref[0:1, :]) * softplus
    beta = _sigmoid(b_raw)

    cshift = c.bit_length() - 1
    r2 = lax.broadcasted_iota(I32, (rt, rt), 0)
    c2 = lax.broadcasted_iota(I32, (rt, rt), 1)
    same_chunk = jnp.right_shift(r2, cshift) == jnp.right_shift(c2, cshift)
    ltri = jnp.where(r2 >= c2, jnp.where(same_chunk, 1.0, 0.0), 0.0).astype(BF16)
    g_hi = g.astype(BF16)
    g_r = g - g_hi.astype(F32)
    g_mid = g_r.astype(BF16)
    g_lo = (g_r - g_mid.astype(F32)).astype(BF16)
    gc = _dot(ltri, g_hi) + _dot(ltri, g_mid) + _dot(ltri, g_lo)
    gct = gc.T
    egc = jnp.exp(gc)
    gend = jnp.concatenate(
        [jnp.broadcast_to(gc[ci * c + c - 1:ci * c + c, :], (c, LANES)) for ci in range(nc)], axis=0)
    kfac = jnp.exp(gend - gc)
    bege = beta * egc
    for ci in range(nc):
        last = ci * c + c - 1
        eg_ref[ci * GDN_HEADS:(ci + 1) * GDN_HEADS, :] = jnp.broadcast_to(
            jnp.exp(gct[0:GDN_HEADS, last:last + 1]), (GDN_HEADS, LANES))

    lane_t = lax.broadcasted_iota(I32, (rt, LANES), 1) < c
    lane_lo = lax.broadcasted_iota(I32, (c, LANES), 1) < c
    lcol = jnp.bitwise_and(lax.broadcasted_iota(I32, (c, LANES), 1), c - 1)
    rowi = lax.broadcasted_iota(I32, (c, LANES), 0)
    causal = rowi >= lcol
    strict = rowi > lcol

    def conv(base, h):
        lo, hi = base + h * HEAD_DIM, base + (h + 1) * HEAD_DIM
        acc = cw_ref[SHORT_CONV - 1:SHORT_CONV, lo:hi] * xs_ref[0, :, lo:hi]
        for j in range(SHORT_CONV - 1):
            acc = acc + cw_ref[j:j + 1, lo:hi] * xs_ref[SHORT_CONV - 1 - j, :, lo:hi]
        return _silu(acc)

    a_pairs = [[None] * (GDN_HEADS // 2) for _ in range(nc)]
    rhs_pairs = [[None] * (GDN_HEADS // 2) for _ in range(nc)]
    for p in range(GDN_HEADS // 2):
        if p + 1 < GDN_HEADS // 2:
            for part in range(3):
                lo = part * GDN_WIDTH + (p + 1) * 2 * HEAD_DIM
                move_rows(lo, lo + 2 * HEAD_DIM)
        ks, kbs, qs, kds, rhss = [], [], [], [], []
        for h in (2 * p, 2 * p + 1):
            q = conv(0, h)
            k = conv(GDN_WIDTH, h)
            v = conv(2 * GDN_WIDTH, h)
            q = q * lax.rsqrt(jnp.sum(q * q, axis=-1, keepdims=True) + EPS) * (HEAD_DIM ** -0.5)
            k = k * lax.rsqrt(jnp.sum(k * k, axis=-1, keepdims=True) + EPS)
            beta_h = beta[:, h:h + 1]
            qd_ref[:, h * HEAD_DIM:(h + 1) * HEAD_DIM] = (q * egc[:, h:h + 1]).astype(BF16)
            ks.append(k)
            kbs.append(k * beta_h)
            qs.append(q)
            kds.append(k * kfac[:, h:h + 1])
            rhss.append(jnp.concatenate([v * beta_h, k * bege[:, h:h + 1]], axis=1))
        h0, h1 = 2 * p, 2 * p + 1
        gch = jnp.where(lane_t, gc[:, h0:h0 + 1], gc[:, h1:h1 + 1])
        for ci in range(nc):
            rows = slice(ci * c, (ci + 1) * c)
            wk = jnp.concatenate([ks[0][rows], ks[1][rows]], axis=0).astype(BF16)
            lhs = jnp.concatenate([kbs[0][rows], qs[0][rows], kbs[1][rows], qs[1][rows]],
                                  axis=0).astype(BF16)
            out = _dot_nt(lhs, wk)
            gcrow = jnp.concatenate([gct[h0:h0 + 1, rows], gct[h1:h1 + 1, rows]], axis=1)
            diff = gch[rows] - gcrow
            decay = jnp.where(causal, jnp.exp(jnp.where(causal, diff, 0.0)), 0.0)
            a_pairs[ci][p] = jnp.where(strict, jnp.where(lane_lo, out[0:c], out[2 * c:3 * c]) * decay, 0.0)
            qk = jnp.where(lane_lo, out[c:2 * c], out[3 * c:4 * c]) * decay
            kdt = jnp.concatenate([kds[0][rows], kds[1][rows]], axis=0).T
            m0 = ci * 3 * c
            m2_ref[m0:m0 + c, p * LANES:(p + 1) * LANES] = qk.astype(BF16)
            m2_ref[m0 + c:m0 + 3 * c, p * LANES:(p + 1) * LANES] = kdt.astype(BF16)
            rhs_pairs[ci][p] = (rhss[0][rows], rhss[1][rows])

    zeros = jnp.zeros((c, 2 * HEAD_DIM), BF16)
    ngrp = GDN_HEADS // 4
    tls = _unit_lower_inverse4(
        [jnp.concatenate([a_pairs[ci][2 * grp], a_pairs[ci][2 * grp + 1]], axis=1)
         for ci in range(nc) for grp in range(ngrp)])
    for ci in range(nc):
        rows = slice(ci * c, (ci + 1) * c)
        for grp in range(ngrp):
            tl = tls[ci * ngrp + grp]
            for j in range(2):
                p = 2 * grp + j
                r0, r1 = rhs_pairs[ci][p]
                bd = jnp.concatenate([jnp.concatenate([r0.astype(BF16), zeros], axis=1),
                                      jnp.concatenate([zeros, r1.astype(BF16)], axis=1)], axis=0)
                sol = _dot(tl[:, j * LANES:(j + 1) * LANES].astype(BF16), bd)
                for i in range(2):
                    h = 2 * p + i
                    lo, hi = h * HEAD_DIM, (h + 1) * HEAD_DIM
                    u_ref[rows, lo:hi] = sol[:, 2 * i * HEAD_DIM:(2 * i + 1) * HEAD_DIM]
                    w_ref[rows, lo:hi] = sol[:, (2 * i + 1) * HEAD_DIM:(2 * i + 2) * HEAD_DIM].astype(BF16)


def _gdn_intra(proj, x, wa_hi, wa_lo, wb_hi, wb_lo, conv_w, prm, seq, rt):
    t = x.shape[0]
    nc = rt // CHUNK
    kern = functools.partial(_gdn_intra_kernel, rt=rt, tiles_per_seq=seq // rt)
    full = lambda i: (0, 0)
    tile = lambda i: (i, 0)
    sel, sel_halo = _shift_selectors(rt)
    return pl.pallas_call(
        kern,
        grid=(t // rt,),
        in_specs=[
            pl.BlockSpec((rt, 3 * GDN_WIDTH), tile),
            pl.BlockSpec((CONV_HALO, 3 * GDN_WIDTH),
                         lambda i: (jnp.maximum(i * (rt // CONV_HALO) - 1, 0), 0)),
            pl.BlockSpec((rt, D_MODEL), tile),
            pl.BlockSpec(sel.shape, full),
            pl.BlockSpec(sel_halo.shape, full),
            pl.BlockSpec((D_MODEL, LANES), full),
            pl.BlockSpec((D_MODEL, LANES), full),
            pl.BlockSpec((D_MODEL, LANES), full),
            pl.BlockSpec((D_MODEL, LANES), full),
            pl.BlockSpec((SHORT_CONV, 3 * GDN_WIDTH), full),
            pl.BlockSpec((8, LANES), full),
        ],
        out_specs=[pl.BlockSpec((rt, GDN_WIDTH), tile),
                   pl.BlockSpec((rt, GDN_WIDTH), tile),
                   pl.BlockSpec((rt, GDN_WIDTH), tile),
                   pl.BlockSpec((nc * 3 * CHUNK, GDN_HEADS // 2 * LANES), tile),
                   pl.BlockSpec((nc * GDN_HEADS, LANES), tile)],
        out_shape=[jax.ShapeDtypeStruct((t, GDN_WIDTH), F32),
                   jax.ShapeDtypeStruct((t, GDN_WIDTH), BF16),
                   jax.ShapeDtypeStruct((t, GDN_WIDTH), BF16),
                   jax.ShapeDtypeStruct((t // CHUNK * 3 * CHUNK, GDN_HEADS // 2 * LANES), BF16),
                   jax.ShapeDtypeStruct((t // CHUNK * GDN_HEADS, LANES), F32)],
        scratch_shapes=[pltpu.VMEM((SHORT_CONV, rt, 3 * GDN_WIDTH), F32)],
        compiler_params=_params("parallel"),
        name="gdn_intra",
    )(proj, proj, x, sel, sel_halo, wa_hi, wa_lo, wb_hi, wb_lo, conv_w, prm)


def _gdn_scan_kernel(u_ref, w_ref, qd_ref, m2_ref, eg_ref, z_ref, ong_ref, o_ref, s_ref, *, nck):
    c = CHUNK

    @pl.when(pl.program_id(1) == 0)
    def _():
        s_ref[...] = jnp.zeros_like(s_ref)

    ong = ong_ref[...]
    zeros = jnp.zeros((c, HEAD_DIM), BF16)
    heads = range(GDN_HEADS)
    span = lambda h: slice(h * HEAD_DIM, (h + 1) * HEAD_DIM)
    states = [s_ref[h] for h in heads]
    for ci in range(nck):
        rows = slice(ci * c, (ci + 1) * c)
        rs = [_dot(jnp.concatenate([w_ref[rows, span(h)], qd_ref[rows, span(h)]], axis=0),
                   states[h].astype(BF16)) for h in heads]
        v_new = [(u_ref[rows, span(h)] - rs[h][:c]).astype(BF16) for h in heads]
        r2s = []
        for p in range(GDN_HEADS // 2):
            bd = jnp.concatenate([jnp.concatenate([v_new[2 * p], zeros], axis=1),
                                  jnp.concatenate([zeros, v_new[2 * p + 1]], axis=1)], axis=0)
            r2s.append(_dot(m2_ref[ci * 3 * c:(ci + 1) * 3 * c, p * LANES:(p + 1) * LANES], bd))
        for h in heads:
            half = span(h % 2)
            r2 = r2s[h // 2]
            states[h] = states[h] * eg_ref[ci * GDN_HEADS + h:ci * GDN_HEADS + h + 1, :] + r2[c:, half]
            o = rs[h][c:] + r2[:c, half]
            o = o * lax.rsqrt(jnp.mean(o * o, axis=-1, keepdims=True) + EPS) * ong
            o = o * _silu(z_ref[rows, span(h)].astype(F32))
            o_ref[rows, span(h)] = o.astype(o_ref.dtype)
    for h in heads:
        s_ref[h] = states[h]


def _gdn_scan(u, w, qd, m2, eg, proj, ong, batch, seq):
    t = batch * seq
    nck = min(SCAN_CHUNKS, seq // CHUNK)
    c = nck * CHUNK
    nch = seq // c
    blk = lambda b, n: (b * nch + n, 0)
    return pl.pallas_call(
        functools.partial(_gdn_scan_kernel, nck=nck),
        grid=(batch, nch),
        in_specs=[
            pl.BlockSpec((c, GDN_WIDTH), blk),
            pl.BlockSpec((c, GDN_WIDTH), blk),
            pl.BlockSpec((c, GDN_WIDTH), blk),
            pl.BlockSpec((3 * c, GDN_HEADS // 2 * LANES), blk),
            pl.BlockSpec((nck * GDN_HEADS, LANES), blk),
            pl.BlockSpec((c, GDN_WIDTH), lambda b, n: (b * nch + n, 3)),
            pl.BlockSpec((1, HEAD_DIM), lambda b, n: (0, 0)),
        ],
        out_specs=pl.BlockSpec((c, GDN_WIDTH), blk),
        out_shape=jax.ShapeDtypeStruct((t, GDN_WIDTH), BF16),
        scratch_shapes=[pltpu.VMEM((GDN_HEADS, HEAD_DIM, HEAD_DIM), F32)],
        compiler_params=_params("parallel", "arbitrary"),
        name="gdn_scan",
    )(u, w, qd, m2, eg, proj, ong)


def _pack_bf16_pairs(y):
    n = y.shape[1] // 2
    yb = y.astype(BF16).astype(F32)
    lo = lax.bitcast_convert_type(yb[:, :n], U32)
    hi = lax.bitcast_convert_type(yb[:, n:], U32)
    return jnp.bitwise_or(jnp.right_shift(lo, 16), jnp.bitwise_and(hi, HI_MASK))


def _unpack_bf16_pairs(w):
    lo = lax.bitcast_convert_type(jnp.left_shift(w, 16), F32)
    hi = lax.bitcast_convert_type(jnp.bitwise_and(w, HI_MASK), F32)
    return lo, hi


def _mixer_kernel(glu_ref, ga_ref, gb_ref, o_ref, x_ref, woa_ref, wob_ref, wo_ref, dww_ref, vec_ref,
                  x1_ref, x1b_ref, x1p_ref, ubuf_ref, sh_ref, conv_ref, *, ts, rc):
    halo = CONF_HALO

    @pl.when(pl.program_id(1) == 0)
    def _():
        ubuf_ref[0:halo, :] = jnp.zeros((halo, CONF_CH), F32)

    @pl.when(pl.program_id(1) != 0)
    def _():
        ubuf_ref[0:halo, :] = ubuf_ref[ts:ts + halo, :]

    glu_a = glu_ref[:, :CONF_CH].astype(F32)
    glu_b = glu_ref[:, CONF_CH:].astype(F32)
    ubuf_ref[halo:halo + ts, :] = glu_a * _sigmoid(glu_b)

    dw_b = vec_ref[0:1, :]
    cln_g = vec_ref[1:2, :]
    cln_b = vec_ref[2:3, :]
    ln1_g = vec_ref[3:4, :]
    ln1_b = vec_ref[4:5, :]

    span = ts + halo - 8
    for s in range(1, 8):
        sh_ref[s - 1] = ubuf_ref[s:s + span, :]

    def tap_rows(j, r0):
        o = halo - (CONF_KERNEL - 1) + j
        q, s = o // 8, o % 8
        if s == 0:
            return ubuf_ref[r0 + o:r0 + o + rc, :]
        return sh_ref[s - 1, r0 + 8 * q:r0 + 8 * q + rc, :]

    hs = ts // 2
    for h0 in range(0, ts, hs):
        rows = slice(h0, h0 + hs)
        gated_a = _sigmoid(ga_ref[rows, :].astype(F32)) * _dot(o_ref[rows, :], woa_ref[...])
        for r0 in range(h0, h0 + hs, rc):
            tap_w = lambda j: jnp.concatenate([dww_ref[j]] * (rc // 8), axis=0)
            acc = tap_w(0) * tap_rows(0, r0)
            for j in range(1, CONF_KERNEL):
                acc = acc + tap_w(j) * tap_rows(j, r0)
            conv_ref[r0:r0 + rc, :] = acc
        uc = _silu(_layer_norm(conv_ref[rows, :] + dw_b, cln_g, cln_b))
        branch_b = _dot(uc.astype(BF16), wob_ref[...])
        hmix = gated_a + _sigmoid(gb_ref[rows, :].astype(F32)) * branch_b
        mix = _dot(hmix.astype(BF16), wo_ref[...])
        x1 = _layer_norm(DEEPNORM_ALPHA * x_ref[rows, :] + mix, ln1_g, ln1_b)
        x1_ref[rows, :] = x1
        x1b_ref[rows, :] = x1.astype(BF16)
        x1p_ref[rows, :] = _pack_bf16_pairs(x1)


def _mixer(proj, o_gdn, x, woa, wob, wo, dww, vec, batch, seq, ts):
    t = batch * seq
    nt = seq // ts
    rows = lambda b, n: b * nt + n
    full = lambda b, n: (0, 0)
    kern = functools.partial(_mixer_kernel, ts=ts, rc=32)
    return pl.pallas_call(
        kern,
        grid=(batch, nt),
        in_specs=[
            pl.BlockSpec((ts, 2 * CONF_CH), lambda b, n: (rows(b, n), 2)),
            pl.BlockSpec((ts, D_MODEL), lambda b, n: (rows(b, n), 6)),
            pl.BlockSpec((ts, D_MODEL), lambda b, n: (rows(b, n), 7)),
            pl.BlockSpec((ts, GDN_WIDTH), lambda b, n: (rows(b, n), 0)),
            pl.BlockSpec((ts, D_MODEL), lambda b, n: (rows(b, n), 0)),
            pl.BlockSpec((GDN_WIDTH, D_MODEL), full),
            pl.BlockSpec((CONF_CH, D_MODEL), full),
            pl.BlockSpec((D_MODEL, D_MODEL), full),
            pl.BlockSpec((CONF_KERNEL, 8, CONF_CH), lambda b, n: (0, 0, 0)),
            pl.BlockSpec((8, D_MODEL), full),
        ],
        out_specs=[pl.BlockSpec((ts, D_MODEL), lambda b, n: (rows(b, n), 0)),
                   pl.BlockSpec((ts, D_MODEL), lambda b, n: (rows(b, n), 0)),
                   pl.BlockSpec((ts, D_MODEL // 2), lambda b, n: (rows(b, n), 0))],
        out_shape=[jax.ShapeDtypeStruct((t, D_MODEL), F32),
                   jax.ShapeDtypeStruct((t, D_MODEL), BF16),
                   jax.ShapeDtypeStruct((t, D_MODEL // 2), U32)],
        scratch_shapes=[pltpu.VMEM((CONF_HALO + ts, CONF_CH), F32),
                        pltpu.VMEM((7, ts + CONF_HALO - 8, CONF_CH), F32),
                        pltpu.VMEM((ts, CONF_CH), F32)],
        compiler_params=_params("parallel", "arbitrary"),
        name="mixer",
    )(proj, proj, proj, o_gdn, x, woa, wob, wo, dww, vec)


def _route_kernel(x_ref, wrh_ref, wrl_ref, bias_ref, eidx_ref, wts_ref, rank_ref, cnt_ref, carry_ref, *, tt):
    @pl.when(pl.program_id(0) == 0)
    def _():
        carry_ref[...] = jnp.zeros_like(carry_ref)

    xh, xl = _split(x_ref[...])
    wrh = wrh_ref[...]
    logits = _dot_nt(wrh, xh) + _dot_nt(wrh, xl) + _dot_nt(wrl_ref[...], xh)
    s = _sigmoid(logits)
    biased = s + bias_ref[...]

    sub = lax.broadcasted_iota(I32, (GROUP_SIZE, tt), 0)
    groups = [biased[g * GROUP_SIZE:(g + 1) * GROUP_SIZE, :] for g in range(N_GROUPS)]
    gs = []
    for bg in groups:
        m1 = jnp.max(bg, axis=0, keepdims=True)
        first = jnp.min(jnp.where(bg == m1, sub, GROUP_SIZE), axis=0, keepdims=True)
        m2 = jnp.max(jnp.where(sub == first, -jnp.inf, bg), axis=0, keepdims=True)
        gs.append(m1 + m2)

    masked_parts = []
    for g in range(N_GROUPS):
        beaten = jnp.zeros((1, tt), I32)
        for o in range(N_GROUPS):
            if o == g:
                continue
            wins = (gs[o] >= gs[g]) if o < g else (gs[o] > gs[g])
            beaten = beaten + wins.astype(I32)
        keep = jnp.broadcast_to(beaten < TOPK_GROUPS, (GROUP_SIZE, tt))
        masked_parts.append(jnp.where(keep, groups[g], -jnp.inf))
    masked = jnp.concatenate(masked_parts, axis=0)

    eiota = lax.broadcasted_iota(I32, (N_EXPERTS, tt), 0)
    sel_all = jnp.zeros((N_EXPERTS, tt), F32)
    picks = []
    for _ in range(TOP_K):
        m = jnp.max(masked, axis=0, keepdims=True)
        idx = jnp.min(jnp.where(masked == m, eiota, N_EXPERTS), axis=0, keepdims=True)
        onehot = eiota == idx
        picks.append((idx, onehot))
        sel_all = jnp.where(onehot, 1.0, sel_all)
        masked = jnp.where(onehot, -jnp.inf, masked)

    tr = lax.broadcasted_iota(I32, (tt, tt), 0)
    tc = lax.broadcasted_iota(I32, (tt, tt), 1)
    before = (tr < tc).astype(BF16)
    sel_b = sel_all.astype(BF16)
    carry = carry_ref[...]
    rank_all = _dot(sel_b, before) + carry[:, 0:1]
    carry_new = carry + _dot(sel_b, jnp.ones((tt, LANES), BF16))
    carry_ref[...] = carry_new
    cnt_ref[...] = carry_new

    s_sel = [jnp.sum(jnp.where(oh, s, 0.0), axis=0, keepdims=True) for _, oh in picks]
    total = s_sel[0]
    for v in s_sel[1:]:
        total = total + v
    for k, (idx, oh) in enumerate(picks):
        eidx_ref[k:k + 1, :] = idx
        wts_ref[k:k + 1, :] = s_sel[k] / total * ROUTED_SCALE
        rank_ref[k:k + 1, :] = jnp.sum(jnp.where(oh, rank_all, 0.0), axis=0, keepdims=True).astype(I32)


def _route(x1, wr_hi, wr_lo, bias, tt):
    t = x1.shape[0]
    kern = functools.partial(_route_kernel, tt=tt)
    return pl.pallas_call(
        kern,
        grid=(t // tt,),
        in_specs=[pl.BlockSpec((tt, D_MODEL), lambda i: (i, 0)),
                  pl.BlockSpec((N_EXPERTS, D_MODEL), lambda i: (0, 0)),
                  pl.BlockSpec((N_EXPERTS, D_MODEL), lambda i: (0, 0)),
                  pl.BlockSpec((N_EXPERTS, tt), lambda i: (0, 0))],
        out_specs=[pl.BlockSpec((TOP_K, tt), lambda i: (0, i)),
                   pl.BlockSpec((TOP_K, tt), lambda i: (0, i)),
                   pl.BlockSpec((TOP_K, tt), lambda i: (0, i)),
                   pl.BlockSpec((N_EXPERTS, LANES), lambda i: (0, 0))],
        out_shape=[jax.ShapeDtypeStruct((TOP_K, t), I32),
                   jax.ShapeDtypeStruct((TOP_K, t), F32),
                   jax.ShapeDtypeStruct((TOP_K, t), I32),
                   jax.ShapeDtypeStruct((N_EXPERTS, LANES), F32)],
        scratch_shapes=[pltpu.VMEM((N_EXPERTS, LANES), F32)],
        compiler_params=_params("arbitrary"),
        name="route",
    )(x1, wr_hi, wr_lo, bias)


def _plan_kernel(cnt_ref, pstart_ref, plan_ref):
    e, nb = N_EXPERTS, plan_ref.shape[1]
    counts = cnt_ref[...]
    nblk = jnp.floor((counts + (ROW_BLOCK - 1)) * (1.0 / ROW_BLOCK))
    hi = jnp.floor(nblk * (1.0 / 256.0))
    lo = nblk - 256.0 * hi
    r = lax.broadcasted_iota(I32, (e, e), 0)
    c = lax.broadcasted_iota(I32, (e, e), 1)
    ltri = (r >= c).astype(BF16)
    bend = 256.0 * _dot(ltri, hi.astype(BF16)) + _dot(ltri, lo.astype(BF16))
    pend = bend * ROW_BLOCK
    pstart = pend - nblk * ROW_BLOCK
    pstart_ref[...] = pstart.astype(I32)

    bs = (lax.broadcasted_iota(I32, (e, nb), 1) * ROW_BLOCK).astype(F32)
    pend_b = jnp.broadcast_to(pend[:, 0:1], (e, nb))
    pstart_b = jnp.broadcast_to(pstart[:, 0:1], (e, nb))
    used_b = jnp.broadcast_to((pstart + counts)[:, 0:1], (e, nb))
    owner = jnp.sum(jnp.where(pend_b <= bs, 1.0, 0.0), axis=0, keepdims=True)
    inside = jnp.where(pstart_b <= bs, jnp.where(bs < pend_b, 1.0, 0.0), 0.0)
    real = jnp.sum(inside * jnp.clip(used_b - bs, 0.0, float(ROW_BLOCK)), axis=0, keepdims=True)
    plan_ref[0:1, :] = jnp.minimum(owner, float(e - 1)).astype(I32)
    plan_ref[1:2, :] = real.astype(I32)
    plan_ref[2:8, :] = jnp.zeros((6, nb), I32)


def _plan(cnt, n_blocks):
    nb = -(-n_blocks // LANES) * LANES
    pstart, plan = pl.pallas_call(
        _plan_kernel,
        out_shape=[jax.ShapeDtypeStruct((N_EXPERTS, LANES), I32),
                   jax.ShapeDtypeStruct((8, nb), I32)],
        name="plan",
    )(cnt)
    return pstart[:, 0], plan[0, :n_blocks], plan[1, :n_blocks]


def _dest_kernel(eidx_ref, rank_ref, pstart_ref, dest_ref):
    eidx = eidx_ref[...]
    acc = rank_ref[...]
    for e in range(N_EXPERTS):
        acc = acc + jnp.where(eidx == e, pstart_ref[e], 0)
    dest_ref[...] = acc


def _dest(eidx_t, rank_t, pstart, tt):
    t = eidx_t.shape[1]
    return pl.pallas_call(
        _dest_kernel,
        grid=(t // tt,),
        in_specs=[pl.BlockSpec((TOP_K, tt), lambda i: (0, i)),
                  pl.BlockSpec((TOP_K, tt), lambda i: (0, i)),
                  pl.BlockSpec(memory_space=pltpu.SMEM)],
        out_specs=pl.BlockSpec((TOP_K, tt), lambda i: (0, i)),
        out_shape=jax.ShapeDtypeStruct((TOP_K, t), I32),
        compiler_params=_params("parallel"),
        name="dest",
    )(eidx_t, rank_t, pstart)


def _dispatch(dest_kt, x1p, n_rows):
    t, width = x1p.shape
    info = plsc.get_sparse_core_info()
    nc, nw = info.num_cores, info.num_cores * info.num_subcores
    chunk = SC_SCATTER_ROWS
    per_w = t // nw
    n_chunks = per_w // chunk
    assert per_w % chunk == 0
    idx = dest_kt.reshape(TOP_K, nw, n_chunks, chunk).transpose(1, 2, 0, 3).reshape(nw, n_chunks * TOP_K, chunk)
    mesh = plsc.VectorSubcoreMesh(core_axis_name="c", subcore_axis_name="s")

    @functools.partial(
        pl.kernel, mesh=mesh, name="dispatch",
        out_type=jax.ShapeDtypeStruct((n_rows, width), x1p.dtype),
        scratch_types=[pltpu.VMEM((n_chunks * TOP_K, chunk), I32),
                       pltpu.VMEM((chunk, width), x1p.dtype),
                       pltpu.SemaphoreType.DMA])
    def scatter(x_hbm, idx_hbm, xs_hbm, idx_v, rows_v, sem):
        wid = lax.axis_index("s") * nc + lax.axis_index("c")
        base = wid * per_w
        pltpu.sync_copy(idx_hbm.at[wid], idx_v)
        for j in range(n_chunks):
            pltpu.sync_copy(x_hbm.at[pl.ds(base + j * chunk, chunk)], rows_v)
            copies = [pltpu.make_async_copy(rows_v, xs_hbm.at[idx_v.at[j * TOP_K + k]], sem)
                      for k in range(TOP_K)]
            for cp in copies:
                cp.start()
            for cp in copies:
                cp.wait()

    return scatter(x1p, idx)


def _gather_rows(table, idx):
    n = idx.shape[0]
    width = table.shape[1]
    info = plsc.get_sparse_core_info()
    nc, nw = info.num_cores, info.num_cores * info.num_subcores
    chunk = SC_GATHER_ROWS
    per_w = n // nw
    n_chunks = per_w // chunk
    assert per_w % (2 * chunk) == 0
    mesh = plsc.VectorSubcoreMesh(core_axis_name="c", subcore_axis_name="s")

    @functools.partial(
        pl.kernel, mesh=mesh, name="gather_rows",
        out_type=jax.ShapeDtypeStruct((n, width), table.dtype),
        scratch_types=[pltpu.VMEM((n_chunks, chunk), I32),
                       pltpu.VMEM((2, chunk, width), table.dtype),
                       pltpu.SemaphoreType.DMA((2,)),
                       pltpu.SemaphoreType.DMA((2,))])
    def gather(table_hbm, idx_hbm, out_hbm, idx_v, rows_v, gsem, osem):
        wid = lax.axis_index("s") * nc + lax.axis_index("c")
        base = wid * per_w
        pltpu.sync_copy(idx_hbm.at[wid], idx_v)

        def fetch(j, b):
            return pltpu.make_async_copy(table_hbm.at[idx_v.at[j]], rows_v.at[b], gsem.at[b])

        def put(j, b):
            return pltpu.make_async_copy(rows_v.at[b], out_hbm.at[pl.ds(base + j * chunk, chunk)], osem.at[b])

        fetch(0, 0).start()

        @pl.loop(0, n_chunks, step=2)
        def _(j0):
            for b in range(2):
                j = j0 + b
                fetch(j, b).wait()

                @pl.when(j + 1 < n_chunks)
                def _():
                    @pl.when(j >= 1)
                    def _():
                        put(j - 1, 1 - b).wait()

                    fetch(j + 1, 1 - b).start()

                put(j, b).start()

        put(n_chunks - 2, 0).wait()
        put(n_chunks - 1, 1).wait()

    return gather(table, idx.reshape(nw, n_chunks, chunk))


def _xs_copy(xs_hbm, xbuf, isem, j, slot):
    return pltpu.make_async_copy(xs_hbm.at[pl.ds(j * ROW_BLOCK, ROW_BLOCK)], xbuf.at[slot], isem.at[slot])


def _ys_copy(ybuf, ys_hbm, osem, j, slot):
    return pltpu.make_async_copy(ybuf.at[slot], ys_hbm.at[pl.ds(j * ROW_BLOCK, ROW_BLOCK)], osem.at[slot])


def _experts_kernel(be_ref, nv_ref, xs_hbm, wg_ref, wu_ref, wd_ref, ys_hbm,
                    xbuf, ybuf, wgu_s, wd_s, cur_ref, isem, osem):
    i = pl.program_id(0)
    n_valid = nv_ref[i]
    half = EXPERT_SUB
    slot = lax.rem(i, EXPERT_IN_SLOTS)
    oslot = lax.rem(i, 2)

    @pl.when(i == 0)
    def _():
        cur_ref[0] = -1
        for j in range(2):
            @pl.when(nv_ref[j] > 0)
            def _():
                _xs_copy(xs_hbm, xbuf, isem, j, j).start()

    @pl.when(nv_ref[i + 2] > 0)
    def _():
        _xs_copy(xs_hbm, xbuf, isem, i + 2, lax.rem(i + 2, EXPERT_IN_SLOTS)).start()

    @pl.when((i >= 2) & (nv_ref[jnp.maximum(i - 2, 0)] > 0))
    def _():
        _ys_copy(ybuf, ys_hbm, osem, i - 2, oslot).wait()

    @pl.when((n_valid > 0) & (cur_ref[0] != be_ref[i]))
    def _():
        wgu_s[:, :EXPERT_FF] = wg_ref[...].astype(BF16)
        wgu_s[:, EXPERT_FF:] = wu_ref[...].astype(BF16)
        wd_s[...] = wd_ref[...].astype(BF16)
        cur_ref[0] = be_ref[i]

    def rows_bf16(r0):
        valid = lax.broadcasted_iota(I32, (half, xbuf.shape[2]), 0) + r0 < n_valid
        lo, hi = _unpack_bf16_pairs(jnp.where(valid, xbuf[slot, r0:r0 + half, :], jnp.zeros((), U32)))
        return jnp.concatenate([lo.astype(BF16), hi.astype(BF16)], axis=1)

    def hidden(gu):
        return (_silu(gu[:, :EXPERT_FF]) * gu[:, EXPERT_FF:]).astype(BF16)

    @pl.when(n_valid > 0)
    def _():
        _xs_copy(xs_hbm, xbuf, isem, i, slot).wait()

    @pl.when(n_valid > half)
    def _():
        xa, xb = rows_bf16(0), rows_bf16(half)
        gua = _dot(xa, wgu_s[...])
        gub = _dot(xb, wgu_s[...])
        ya = _dot(hidden(gua), wd_s[...])
        yb = _dot(hidden(gub), wd_s[...])
        ybuf[oslot, 0:half, :] = _pack_bf16_pairs(ya)
        ybuf[oslot, half:, :] = _pack_bf16_pairs(yb)

    @pl.when((n_valid > 0) & (n_valid <= half))
    def _():
        ya = _dot(hidden(_dot(rows_bf16(0), wgu_s[...])), wd_s[...])
        ybuf[oslot, 0:half, :] = _pack_bf16_pairs(ya)
        ybuf[oslot, half:, :] = jnp.zeros((half, ybuf.shape[2]), ybuf.dtype)

    @pl.when(n_valid > 0)
    def _():
        _ys_copy(ybuf, ys_hbm, osem, i, oslot).start()


def _experts(block_e, n_valid, xs, wg, wu, wd, layer):
    n_rows, width = xs.shape
    steps = n_rows // ROW_BLOCK + 2
    be = jnp.concatenate([block_e, jnp.full((2,), N_EXPERTS - 1, I32)])
    nv = jnp.concatenate([n_valid, jnp.zeros((4,), I32)])
    grid_spec = pltpu.PrefetchScalarGridSpec(
        num_scalar_prefetch=2,
        grid=(steps,),
        in_specs=[pl.BlockSpec(memory_space=pl.ANY),
                  pl.BlockSpec((None, None, D_MODEL, EXPERT_FF), lambda i, be, nv: (layer, be[i], 0, 0)),
                  pl.BlockSpec((None, None, D_MODEL, EXPERT_FF), lambda i, be, nv: (layer, be[i], 0, 0)),
                  pl.BlockSpec((None, None, EXPERT_FF, D_MODEL), lambda i, be, nv: (layer, be[i], 0, 0))],
        out_specs=pl.BlockSpec(memory_space=pl.ANY),
        scratch_shapes=[pltpu.VMEM((EXPERT_IN_SLOTS, ROW_BLOCK, width), xs.dtype),
                        pltpu.VMEM((2, ROW_BLOCK, width), xs.dtype),
                        pltpu.VMEM((D_MODEL, 2 * EXPERT_FF), BF16),
                        pltpu.VMEM((EXPERT_FF, D_MODEL), BF16),
                        pltpu.SMEM((1,), I32),
                        pltpu.SemaphoreType.DMA((EXPERT_IN_SLOTS,)),
                        pltpu.SemaphoreType.DMA((2,))],
    )
    return pl.pallas_call(
        _experts_kernel,
        grid_spec=grid_spec,
        out_shape=jax.ShapeDtypeStruct((n_rows, width), xs.dtype),
        compiler_params=_params("arbitrary"),
        name="experts",
    )(be, nv, xs, wg, wu, wd)


def _shared_kernel(x1b_ref, wsg_ref, wsu_ref, wsd_ref, anchor_ref, sh_ref):
    del anchor_ref
    xb = x1b_ref[...]
    hid = (_silu(_dot(xb, wsg_ref[...])) * _dot(xb, wsu_ref[...])).astype(BF16)
    sh_ref[...] = _dot(hid, wsd_ref[...]).astype(sh_ref.dtype)


def _shared(x1b, wsg, wsu, wsd, anchor, tt, row0, nrows):
    full = lambda i: (0, 0)
    first = row0 // tt
    return pl.pallas_call(
        _shared_kernel,
        grid=(nrows // tt,),
        in_specs=[pl.BlockSpec((tt, D_MODEL), lambda i: (first + i, 0)),
                  pl.BlockSpec((D_MODEL, SHARED_FF), full),
                  pl.BlockSpec((D_MODEL, SHARED_FF), full),
                  pl.BlockSpec((SHARED_FF, D_MODEL), full),
                  pl.BlockSpec(memory_space=pl.ANY)],
        out_specs=pl.BlockSpec((tt, D_MODEL), lambda i: (i, 0)),
        out_shape=jax.ShapeDtypeStruct((nrows, D_MODEL), BF16),
        compiler_params=_params("parallel"),
        name="shared",
    )(x1b, wsg, wsu, wsd, anchor)


def _combine_kernel(yg_ref, w_ref, x1_ref, sha_ref, shb_ref, vec_ref, x2_ref, *, half_steps):
    in_first = pl.program_id(0) < half_steps
    shared = jnp.where(in_first, sha_ref[...], shb_ref[...]).astype(F32)
    w = w_ref[...]
    half = D_MODEL // 2
    acc_lo, acc_hi = shared[:, :half], shared[:, half:]
    for k in range(TOP_K):
        lo, hi = _unpack_bf16_pairs(yg_ref[k])
        acc_lo = acc_lo + w[:, k:k + 1] * lo
        acc_hi = acc_hi + w[:, k:k + 1] * hi
    acc = jnp.concatenate([acc_lo, acc_hi], axis=1)
    x2 = _layer_norm(DEEPNORM_ALPHA * x1_ref[...] + acc, vec_ref[0:1, :], vec_ref[1:2, :])
    x2_ref[...] = x2


def _combine(yg, w_tok, x1, shared_a, shared_b, vec, tt):
    t = x1.shape[0]
    half_steps = shared_a.shape[0] // tt
    full = lambda i: (0, 0)
    return pl.pallas_call(
        functools.partial(_combine_kernel, half_steps=half_steps),
        grid=(t // tt,),
        in_specs=[pl.BlockSpec((TOP_K, tt, yg.shape[2]), lambda i: (0, i, 0)),
                  pl.BlockSpec((tt, TOP_K), lambda i: (i, 0)),
                  pl.BlockSpec((tt, D_MODEL), lambda i: (i, 0)),
                  pl.BlockSpec((tt, D_MODEL), lambda i: (jnp.minimum(i, half_steps - 1), 0)),
                  pl.BlockSpec((tt, D_MODEL), lambda i: (jnp.maximum(i - half_steps, 0), 0)),
                  pl.BlockSpec((8, D_MODEL), full)],
        out_specs=pl.BlockSpec((tt, D_MODEL), lambda i: (i, 0)),
        out_shape=jax.ShapeDtypeStruct((t, D_MODEL), F32),
        compiler_params=_params("parallel"),
        name="combine",
    )(yg, w_tok, x1, shared_a, shared_b, vec)


def _pad_rows(a, rows):
    return jnp.zeros((rows, a.shape[-1]), F32).at[:a.shape[0]].set(a.astype(F32))


def _layer(x, p, stacked, layer, batch, seq):
    t = batch * seq
    w_main, wa_hi, wa_lo, wb_hi, wb_lo = _regroup(stacked["w_in"], layer, 256)

    tm = min(1024, t)
    proj = _proj(x, w_main, tm, 2048)

    prm = jnp.zeros((8, LANES), F32)
    prm = prm.at[0, :GDN_HEADS].set(p["a_log"]).at[1, :GDN_HEADS].set(p["dt_bias"])
    ong = p["o_norm_g"].reshape(1, HEAD_DIM).astype(F32)
    u, w, qd, m2, eg = _gdn_intra(proj, x, wa_hi, wa_lo, wb_hi, wb_lo, p["conv_qkv"].astype(F32), prm,
                                  seq, min(GDN_TILE, seq))
    o_gdn = _gdn_scan(u, w, qd, m2, eg, proj, ong, batch, seq)

    ts = min(256, seq)
    dww = jnp.broadcast_to(p["dw_w"].astype(F32)[:, None, :], (CONF_KERNEL, 8, CONF_CH))
    vec = _pad_rows(jnp.stack([p["dw_b"], p["cln_g"], p["cln_b"], p["ln1_g"], p["ln1_b"]]), 8)
    x1, x1b, x1p = _mixer(proj, o_gdn, x, p["w_oa"].astype(BF16), p["w_ob"].astype(BF16),
                          p["w_o"].astype(BF16), dww, vec, batch, seq, ts)

    tt_r = min(512, t)
    wr_t = p["w_router"].T.astype(F32)
    wr_hi = wr_t.astype(BF16)
    wr_lo = (wr_t - wr_hi.astype(F32)).astype(BF16)
    bias = jnp.broadcast_to(p["router_bias"].astype(F32)[:, None], (N_EXPERTS, tt_r))
    eidx_t, wts_t, rank_t, cnt = _route(x1, wr_hi, wr_lo, bias, tt_r)

    n_blocks = -(-(t * TOP_K + N_EXPERTS * (ROW_BLOCK - 1)) // ROW_BLOCK)
    n_rows = n_blocks * ROW_BLOCK
    pstart, block_e, n_valid = _plan(cnt, n_blocks)

    dest = _dest(eidx_t, rank_t, pstart, min(2048, t))
    sh_w = (p["w_sh_gate"].astype(BF16), p["w_sh_up"].astype(BF16), p["w_sh_down"].astype(BF16))
    tt_s = min(512, t // 2)
    xs = _dispatch(dest, x1p, n_rows)
    shared_a = _shared(x1b, *sh_w, dest, tt_s, 0, t // 2)
    ys = _experts(block_e, n_valid, xs, stacked["w_gate_e"], stacked["w_up_e"], stacked["w_down_e"], layer)
    yg = _gather_rows(ys, dest.reshape(TOP_K * t)).reshape(TOP_K, t, ys.shape[1])
    shared_b = _shared(x1b, *sh_w, ys, tt_s, t // 2, t // 2)
    vec2 = _pad_rows(jnp.stack([p["ln2_g"], p["ln2_b"]]), 8)
    return _combine(yg, wts_t.T, x1, shared_a, shared_b, vec2, min(256, t // 2))


_PARAM_NAMES = ("w_in", "conv_qkv", "a_log", "dt_bias", "o_norm_g", "w_oa", "dw_w", "dw_b", "cln_g",
                "cln_b", "w_ob", "w_o", "ln1_g", "ln1_b", "w_router", "router_bias", "w_gate_e",
                "w_up_e", "w_down_e", "w_sh_gate", "w_sh_up", "w_sh_down", "ln2_g", "ln2_b")


_EXPERT_WEIGHTS = ("w_in", "w_gate_e", "w_up_e", "w_down_e")


def kernel(x, w_in, conv_qkv, a_log, dt_bias, o_norm_g, w_oa, dw_w, dw_b, cln_g, cln_b, w_ob, w_o,
           ln1_g, ln1_b, w_router, router_bias, w_gate_e, w_up_e, w_down_e, w_sh_gate, w_sh_up,
           w_sh_down, ln2_g, ln2_b):
    stacked = dict(zip(_PARAM_NAMES, (w_in, conv_qkv, a_log, dt_bias, o_norm_g, w_oa, dw_w, dw_b, cln_g,
                                      cln_b, w_ob, w_o, ln1_g, ln1_b, w_router, router_bias, w_gate_e,
                                      w_up_e, w_down_e, w_sh_gate, w_sh_up, w_sh_down, ln2_g, ln2_b)))
    batch, seq, d = x.shape
    assert d == D_MODEL and seq % CHUNK == 0
    xf = x.reshape(batch * seq, d).astype(F32)
    for layer in range(w_in.shape[0]):
        p = {name: arr[layer] for name, arr in stacked.items() if name not in _EXPERT_WEIGHTS}
        xf = _layer(xf, p, stacked, layer, batch, seq)
    return xf.reshape(batch, seq, d).astype(x.dtype)
```

```python
import functools

import jax
import jax.numpy as jnp
import numpy as np
from jax import lax
from jax.experimental import pallas as pl
from jax.experimental.pallas import tpu as pltpu
from jax.experimental.pallas import tpu_sc as plsc

F32 = jnp.float32
BF16 = jnp.bfloat16
I32 = jnp.int32
U32 = jnp.uint32
HI_MASK = np.uint32(0xFFFF0000)

D_MODEL = 1024
GDN_HEADS = 8
HEAD_DIM = 128
GDN_WIDTH = GDN_HEADS * HEAD_DIM
SHORT_CONV = 4
CONV_HALO = 16
CHUNK = 64
SOLVE_BLOCK = 16
GDN_TILE = 256
SCAN_CHUNKS = 4
CONF_CH = D_MODEL
CONF_KERNEL = 31
CONF_HALO = 32
N_EXPERTS = 64
TOP_K = 8
N_GROUPS = 8
GROUP_SIZE = N_EXPERTS // N_GROUPS
TOPK_GROUPS = 4
EXPERT_FF = 256
SHARED_FF = 256
ROUTED_SCALE = 2.5
DEPTH = 2
DEEPNORM_ALPHA = (2 * DEPTH) ** 0.25
EPS = 1e-6

LANES = 128
PROJ_COLS = 8 * D_MODEL
ROW_BLOCK = 512
EXPERT_SUB = ROW_BLOCK // 2
EXPERT_IN_SLOTS = 3
SC_SCATTER_ROWS = 128
SC_GATHER_ROWS = 64
VMEM_LIMIT = 56 * 1024 * 1024


def _params(*sem):
    return pltpu.CompilerParams(dimension_semantics=sem, vmem_limit_bytes=VMEM_LIMIT)


def _dot(a, b):
    return jnp.dot(a, b, preferred_element_type=F32)


def _dot_nt(a, b):
    return lax.dot_general(a, b, (((1,), (1,)), ((), ())), preferred_element_type=F32)


def _split(a):
    hi = a.astype(BF16)
    lo = (a - hi.astype(F32)).astype(BF16)
    return hi, lo


def _dot3(a, b):
    ah, al = _split(a)
    bh, bl = _split(b)
    return _dot(ah, bh) + _dot(al, bh) + _dot(ah, bl)


def _sigmoid(x):
    return 1.0 / (1.0 + jnp.exp(-x))


def _silu(x):
    return x * _sigmoid(x)


def _layer_norm(y, g, b):
    mu = jnp.mean(y, axis=-1, keepdims=True)
    yc = y - mu
    var = jnp.mean(yc * yc, axis=-1, keepdims=True)
    return yc * lax.rsqrt(var + EPS) * g + b


def _proj_kernel(a_ref, w_ref, o_ref):
    o_ref[...] = _dot_nt(a_ref[...].astype(BF16), w_ref[...]).astype(o_ref.dtype)


def _regroup_kernel(wt_hbm, main_ref, ahi_ref, alo_ref, bhi_ref, blo_ref, buf, ab_buf, sem, *, layer, tr):
    i = pl.program_id(0)
    half = PROJ_COLS // 2
    skip = 2 * GDN_HEADS
    start = pl.multiple_of(jnp.where(i < half // tr, 0, skip) + i * tr, skip)
    cp = pltpu.make_async_copy(wt_hbm.at[layer, pl.ds(start, tr), :], buf, sem)
    cp.start()
    cp.wait()
    main_ref[...] = buf[...].astype(BF16)

    @pl.when(i == 0)
    def _():
        cp_ab = pltpu.make_async_copy(wt_hbm.at[layer, pl.ds(half, LANES), :], ab_buf, sem)
        cp_ab.start()
        cp_ab.wait()
        cols = ab_buf[...].T
        wide = jnp.concatenate([cols, jnp.zeros_like(cols)], axis=1)
        lane = lax.broadcasted_iota(I32, cols.shape, 1)
        for off, hi_ref, lo_ref in ((0, ahi_ref, alo_ref), (GDN_HEADS, bhi_ref, blo_ref)):
            hi, lo = _split(jnp.where(lane < GDN_HEADS, wide[:, off:off + LANES], 0.0))
            hi_ref[...] = hi
            lo_ref[...] = lo


def _regroup(w_in_t, layer, tr):
    d = w_in_t.shape[2]
    small = jax.ShapeDtypeStruct((d, LANES), BF16)
    whole = lambda i: (0, 0)
    return pl.pallas_call(
        functools.partial(_regroup_kernel, layer=layer, tr=tr),
        grid=(PROJ_COLS // tr,),
        in_specs=[pl.BlockSpec(memory_space=pl.ANY)],
        out_specs=[pl.BlockSpec((tr, d), lambda i: (i, 0))] + [pl.BlockSpec((d, LANES), whole)] * 4,
        out_shape=[jax.ShapeDtypeStruct((PROJ_COLS, d), BF16), small, small, small, small],
        scratch_shapes=[pltpu.VMEM((tr, d), F32), pltpu.VMEM((LANES, d), F32), pltpu.SemaphoreType.DMA],
        compiler_params=_params("arbitrary"),
        name="regroup",
    )(w_in_t)


def _proj(xb, w, tm, tn):
    m, k = xb.shape
    n = w.shape[0]
    return pl.pallas_call(
        _proj_kernel,
        grid=(m // tm, n // tn),
        in_specs=[pl.BlockSpec((tm, k), lambda i, j: (i, 0)),
                  pl.BlockSpec((tn, k), lambda i, j: (j, 0))],
        out_specs=pl.BlockSpec((tm, tn), lambda i, j: (i, j)),
        out_shape=jax.ShapeDtypeStruct((m, n), BF16),
        compiler_params=_params("parallel", "parallel"),
        name="proj",
    )(xb, w)


def _unit_lower_inverse4(als):
    c = CHUNK
    n = als[0].shape[1]
    row = lax.broadcasted_iota(I32, (c, n), 0)
    col = jnp.bitwise_and(lax.broadcasted_iota(I32, (c, n), 1), c - 1)
    shift = SOLVE_BLOCK.bit_length() - 1
    same = jnp.right_shift(row, shift) == jnp.right_shift(col, shift)
    eye = (row == col).astype(F32)
    cshift = c.bit_length() - 1
    brow = jnp.right_shift(lax.broadcasted_iota(I32, (n, n), 0), cshift)
    bcol = jnp.right_shift(lax.broadcasted_iota(I32, (n, n), 1), cshift)
    on_diag = brow == bcol

    def mm(x, y):
        yb = y.astype(BF16)
        bd = jnp.where(on_diag, jnp.concatenate([yb] * (n // c), axis=0), jnp.zeros((), BF16))
        return _dot(x.astype(BF16), bd)

    a_diag = [jnp.where(same, al, 0.0) for al in als]
    a_off = [al - ad for al, ad in zip(als, a_diag)]
    bp = [-ad for ad in a_diag]
    p = [eye + b for b in bp]
    for _ in range(3):
        bp = [mm(b, b) for b in bp]
        p = [x + mm(x, b) for x, b in zip(p, bp)]
    n1 = [mm(x, ao) for x, ao in zip(p, a_off)]
    n2 = [mm(x, x) for x in n1]
    q = [x + mm(y, x) for x, y in zip(p, n2)]
    return [x - mm(y, x) for x, y in zip(q, n1)]


def _shift_selectors(rt):
    sel = np.zeros((SHORT_CONV * rt, rt), np.float32)
    sel_halo = np.zeros((SHORT_CONV * 8, CONV_HALO), np.float32)
    for d in range(SHORT_CONV):
        for t in range(d, rt):
            sel[d * rt + t, t - d] = 1.0
        for t in range(d):
            sel_halo[d * 8 + t, CONV_HALO + t - d] = 1.0
    return jnp.asarray(sel, BF16), jnp.asarray(sel_halo, BF16)


def _gdn_intra_kernel(qkv_ref, prev_ref, x_ref, sel_ref, selh_ref, wah_ref, wal_ref, wbh_ref, wbl_ref, cw_ref,
                      prm_ref, u_ref, w_ref, qd_ref, m2_ref, eg_ref, xs_ref, *, rt, tiles_per_seq):
    c = CHUNK
    nc = rt // c
    first = (pl.program_id(0) % tiles_per_seq) == 0
    edge = jnp.where(first, 0.0, _dot(selh_ref[...], prev_ref[...]))

    def move_rows(lo, hi):
        moved = _dot(sel_ref[...], qkv_ref[:, lo:hi])
        for d in range(SHORT_CONV):
            xs_ref[d, 0:8, lo:hi] = moved[d * rt:d * rt + 8] + edge[d * 8:(d + 1) * 8, lo:hi]
            xs_ref[d, 8:rt, lo:hi] = moved[d * rt + 8:(d + 1) * rt]

    for part in range(3):
        move_rows(part * GDN_WIDTH, part * GDN_WIDTH + 2 * HEAD_DIM)

    xh, xl = _split(x_ref[...])

    def proj3(wh_ref, wl_ref):
        wh = wh_ref[...]
        return _dot(xh, wh) + _dot(xl, wh) + _dot(xh, wl_ref[...])

    a_raw = proj3(wah_ref, wal_ref)
    b_raw = proj3(wbh_ref, wbl_ref)
    sp_in = a_raw + prm_ref[1:2, :]
    softplus = jnp.maximum(sp_in, 0.0) + jnp.log(1.0 + jnp.exp(-jnp.abs(sp_in)))
    g = -jnp.exp(prm_ref[0:1, :]) * softplus
    beta = _sigmoid(b_raw)

    cshift = c.bit_length() - 1
    r2 = lax.broadcasted_iota(I32, (rt, rt), 0)
    c2 = lax.broadcasted_iota(I32, (rt, rt), 1)
    same_chunk = jnp.right_shift(r2, cshift) == jnp.right_shift(c2, cshift)
    ltri = jnp.where(r2 >= c2, jnp.where(same_chunk, 1.0, 0.0), 0.0).astype(BF16)
    g_hi = g.astype(BF16)
    g_r = g - g_hi.astype(F32)
    g_mid = g_r.astype(BF16)
    g_lo = (g_r - g_mid.astype(F32)).astype(BF16)
    gc = _dot(ltri, g_hi) + _dot(ltri, g_mid) + _dot(ltri, g_lo)
    gct = gc.T
    egc = jnp.exp(gc)
    gend = jnp.concatenate(
        [jnp.broadcast_to(gc[ci * c + c - 1:ci * c + c, :], (c, LANES)) for ci in range(nc)], axis=0)
    kfac = jnp.exp(gend - gc)
    bege = beta * egc
    for ci in range(nc):
        last = ci * c + c - 1
        eg_ref[ci * GDN_HEADS:(ci + 1) * GDN_HEADS, :] = jnp.broadcast_to(
            jnp.exp(gct[0:GDN_HEADS, last:last + 1]), (GDN_HEADS, LANES))

    lane_t = lax.broadcasted_iota(I32, (rt, LANES), 1) < c
    lane_lo = lax.broadcasted_iota(I32, (c, LANES), 1) < c
    lcol = jnp.bitwise_and(lax.broadcasted_iota(I32, (c, LANES), 1), c - 1)
    rowi = lax.broadcasted_iota(I32, (c, LANES), 0)
    causal = rowi >= lcol
    strict = rowi > lcol

    def conv(base, h):
        lo, hi = base + h * HEAD_DIM, base + (h + 1) * HEAD_DIM
        acc = cw_ref[SHORT_CONV - 1:SHORT_CONV, lo:hi] * xs_ref[0, :, lo:hi]
        for j in range(SHORT_CONV - 1):
            acc = acc + cw_ref[j:j + 1, lo:hi] * xs_ref[SHORT_CONV - 1 - j, :, lo:hi]
        return _silu(acc)

    a_pairs = [[None] * (GDN_HEADS // 2) for _ in range(nc)]
    rhs_pairs = [[None] * (GDN_HEADS // 2) for _ in range(nc)]
    for p in range(GDN_HEADS // 2):
        if p + 1 < GDN_HEADS // 2:
            for part in range(3):
                lo = part * GDN_WIDTH + (p + 1) * 2 * HEAD_DIM
                move_rows(lo, lo + 2 * HEAD_DIM)
        ks, kbs, qs, kds, rhss = [], [], [], [], []
        for h in (2 * p, 2 * p + 1):
            q = conv(0, h)
            k = conv(GDN_WIDTH, h)
            v = conv(2 * GDN_WIDTH, h)
            q = q * lax.rsqrt(jnp.sum(q * q, axis=-1, keepdims=True) + EPS) * (HEAD_DIM ** -0.5)
            k = k * lax.rsqrt(jnp.sum(k * k, axis=-1, keepdims=True) + EPS)
            beta_h = beta[:, h:h + 1]
            qd_ref[:, h * HEAD_DIM:(h + 1) * HEAD_DIM] = (q * egc[:, h:h + 1]).astype(BF16)
            ks.append(k)
            kbs.append(k * beta_h)
            qs.append(q)
            kds.append(k * kfac[:, h:h + 1])
            rhss.append(jnp.concatenate([v * beta_h, k * bege[:, h:h + 1]], axis=1))
        h0, h1 = 2 * p, 2 * p + 1
        gch = jnp.where(lane_t, gc[:, h0:h0 + 1], gc[:, h1:h1 + 1])
        for ci in range(nc):
            rows = slice(ci * c, (ci + 1) * c)
            wk = jnp.concatenate([ks[0][rows], ks[1][rows]], axis=0).astype(BF16)
            lhs = jnp.concatenate([kbs[0][rows], qs[0][rows], kbs[1][rows], qs[1][rows]],
                                  axis=0).astype(BF16)
            out = _dot_nt(lhs, wk)
            gcrow = jnp.concatenate([gct[h0:h0 + 1, rows], gct[h1:h1 + 1, rows]], axis=1)
            diff = gch[rows] - gcrow
            decay = jnp.where(causal, jnp.exp(jnp.where(causal, diff, 0.0)), 0.0)
            a_pairs[ci][p] = jnp.where(strict, jnp.where(lane_lo, out[0:c], out[2 * c:3 * c]) * decay, 0.0)
            qk = jnp.where(lane_lo, out[c:2 * c], out[3 * c:4 * c]) * decay
            kdt = jnp.concatenate([kds[0][rows], kds[1][rows]], axis=0).T
            m0 = ci * 3 * c
            m2_ref[m0:m0 + c, p * LANES:(p + 1) * LANES] = qk.astype(BF16)
            m2_ref[m0 + c:m0 + 3 * c, p * LANES:(p + 1) * LANES] = kdt.astype(BF16)
            rhs_pairs[ci][p] = (rhss[0][rows], rhss[1][rows])

    zeros = jnp.zeros((c, 2 * HEAD_DIM), BF16)
    ngrp = GDN_HEADS // 4
    tls = _unit_lower_inverse4(
        [jnp.concatenate([a_pairs[ci][2 * grp], a_pairs[ci][2 * grp + 1]], axis=1)
         for ci in range(nc) for grp in range(ngrp)])
    for ci in range(nc):
        rows = slice(ci * c, (ci + 1) * c)
        for grp in range(ngrp):
            tl = tls[ci * ngrp + grp]
            for j in range(2):
                p = 2 * grp + j
                r0, r1 = rhs_pairs[ci][p]
                bd = jnp.concatenate([jnp.concatenate([r0.astype(BF16), zeros], axis=1),
                                      jnp.concatenate([zeros, r1.astype(BF16)], axis=1)], axis=0)
                sol = _dot(tl[:, j * LANES:(j + 1) * LANES].astype(BF16), bd)
                for i in range(2):
                    h = 2 * p + i
                    lo, hi = h * HEAD_DIM, (h + 1) * HEAD_DIM
                    u_ref[rows, lo:hi] = sol[:, 2 * i * HEAD_DIM:(2 * i + 1) * HEAD_DIM]
                    w_ref[rows, lo:hi] = sol[:, (2 * i + 1) * HEAD_DIM:(2 * i + 2) * HEAD_DIM].astype(BF16)


def _gdn_intra(proj, x, wa_hi, wa_lo, wb_hi, wb_lo, conv_w, prm, seq, rt):
    t = x.shape[0]
    nc = rt // CHUNK
    kern = functools.partial(_gdn_intra_kernel, rt=rt, tiles_per_seq=seq // rt)
    full = lambda i: (0, 0)
    tile = lambda i: (i, 0)
    sel, sel_halo = _shift_selectors(rt)
    return pl.pallas_call(
        kern,
        grid=(t // rt,),
        in_specs=[
            pl.BlockSpec((rt, 3 * GDN_WIDTH), tile),
            pl.BlockSpec((CONV_HALO, 3 * GDN_WIDTH),
                         lambda i: (jnp.maximum(i * (rt // CONV_HALO) - 1, 0), 0)),
            pl.BlockSpec((rt, D_MODEL), tile),
            pl.BlockSpec(sel.shape, full),
            pl.BlockSpec(sel_halo.shape, full),
            pl.BlockSpec((D_MODEL, LANES), full),
            pl.BlockSpec((D_MODEL, LANES), full),
            pl.BlockSpec((D_MODEL, LANES), full),
            pl.BlockSpec((D_MODEL, LANES), full),
            pl.BlockSpec((SHORT_CONV, 3 * GDN_WIDTH), full),
            pl.BlockSpec((8, LANES), full),
        ],
        out_specs=[pl.BlockSpec((rt, GDN_WIDTH), tile),
                   pl.BlockSpec((rt, GDN_WIDTH), tile),
                   pl.BlockSpec((rt, GDN_WIDTH), tile),
                   pl.BlockSpec((nc * 3 * CHUNK, GDN_HEADS // 2 * LANES), tile),
                   pl.BlockSpec((nc * GDN_HEADS, LANES), tile)],
        out_shape=[jax.ShapeDtypeStruct((t, GDN_WIDTH), F32),
                   jax.ShapeDtypeStruct((t, GDN_WIDTH), BF16),
                   jax.ShapeDtypeStruct((t, GDN_WIDTH), BF16),
                   jax.ShapeDtypeStruct((t // CHUNK * 3 * CHUNK, GDN_HEADS // 2 * LANES), BF16),
                   jax.ShapeDtypeStruct((t // CHUNK * GDN_HEADS, LANES), F32)],
        scratch_shapes=[pltpu.VMEM((SHORT_CONV, rt, 3 * GDN_WIDTH), F32)],
        compiler_params=_params("parallel"),
        name="gdn_intra",
    )(proj, proj, x, sel, sel_halo, wa_hi, wa_lo, wb_hi, wb_lo, conv_w, prm)


def _gdn_scan_kernel(u_ref, w_ref, qd_ref, m2_ref, eg_ref, z_ref, ong_ref, o_ref, s_ref, *, nck):
    c = CHUNK

    @pl.when(pl.program_id(1) == 0)
    def _():
        s_ref[...] = jnp.zeros_like(s_ref)

    ong = ong_ref[...]
    zeros = jnp.zeros((c, HEAD_DIM), BF16)
    heads = range(GDN_HEADS)
    span = lambda h: slice(h * HEAD_DIM, (h + 1) * HEAD_DIM)
    states = [s_ref[h] for h in heads]
    for ci in range(nck):
        rows = slice(ci * c, (ci + 1) * c)
        rs = [_dot(jnp.concatenate([w_ref[rows, span(h)], qd_ref[rows, span(h)]], axis=0),
                   states[h].astype(BF16)) for h in heads]
        v_new = [(u_ref[rows, span(h)] - rs[h][:c]).astype(BF16) for h in heads]
        r2s = []
        for p in range(GDN_HEADS // 2):
            bd = jnp.concatenate([jnp.concatenate([v_new[2 * p], zeros], axis=1),
                                  jnp.concatenate([zeros, v_new[2 * p + 1]], axis=1)], axis=0)
            r2s.append(_dot(m2_ref[ci * 3 * c:(ci + 1) * 3 * c, p * LANES:(p + 1) * LANES], bd))
        for h in heads:
            half = span(h % 2)
            r2 = r2s[h // 2]
            states[h] = states[h] * eg_ref[ci * GDN_HEADS + h:ci * GDN_HEADS + h + 1, :] + r2[c:, half]
            o = rs[h][c:] + r2[:c, half]
            o = o * lax.rsqrt(jnp.mean(o * o, axis=-1, keepdims=True) + EPS) * ong
            o = o * _silu(z_ref[rows, span(h)].astype(F32))
            o_ref[rows, span(h)] = o.astype(o_ref.dtype)
    for h in heads:
        s_ref[h] = states[h]


def _gdn_scan(u, w, qd, m2, eg, proj, ong, batch, seq):
    t = batch * seq
    nck = min(SCAN_CHUNKS, seq // CHUNK)
    c = nck * CHUNK
    nch = seq // c
    blk = lambda b, n: (b * nch + n, 0)
    return pl.pallas_call(
        functools.partial(_gdn_scan_kernel, nck=nck),
        grid=(batch, nch),
        in_specs=[
            pl.BlockSpec((c, GDN_WIDTH), blk),
            pl.BlockSpec((c, GDN_WIDTH), blk),
            pl.BlockSpec((c, GDN_WIDTH), blk),
            pl.BlockSpec((3 * c, GDN_HEADS // 2 * LANES), blk),
            pl.BlockSpec((nck * GDN_HEADS, LANES), blk),
            pl.BlockSpec((c, GDN_WIDTH), lambda b, n: (b * nch + n, 3)),
            pl.BlockSpec((1, HEAD_DIM), lambda b, n: (0, 0)),
        ],
        out_specs=pl.BlockSpec((c, GDN_WIDTH), blk),
        out_shape=jax.ShapeDtypeStruct((t, GDN_WIDTH), BF16),
        scratch_shapes=[pltpu.VMEM((GDN_HEADS, HEAD_DIM, HEAD_DIM), F32)],
        compiler_params=_params("parallel", "arbitrary"),
        name="gdn_scan",
    )(u, w, qd, m2, eg, proj, ong)


def _pack_bf16_pairs(y):
    n = y.shape[1] // 2
    yb = y.astype(BF16).astype(F32)
    lo = lax.bitcast_convert_type(yb[:, :n], U32)
    hi = lax.bitcast_convert_type(yb[:, n:], U32)
    return jnp.bitwise_or(jnp.right_shift(lo, 16), jnp.bitwise_and(hi, HI_MASK))


def _unpack_bf16_pairs(w):
    lo = lax.bitcast_convert_type(jnp.left_shift(w, 16), F32)
    hi = lax.bitcast_convert_type(jnp.bitwise_and(w, HI_MASK), F32)
    return lo, hi


def _mixer_kernel(glu_ref, ga_ref, gb_ref, o_ref, x_ref, woa_ref, wob_ref, wo_ref, dww_ref, vec_ref,
                  x1_ref, x1b_ref, x1p_ref, ubuf_ref, sh_ref, conv_ref, *, ts, rc):
    halo = CONF_HALO

    @pl.when(pl.program_id(1) == 0)
    def _():
        ubuf_ref[0:halo, :] = jnp.zeros((halo, CONF_CH), F32)

    @pl.when(pl.program_id(1) != 0)
    def _():
        ubuf_ref[0:halo, :] = ubuf_ref[ts:ts + halo, :]

    glu_a = glu_ref[:, :CONF_CH].astype(F32)
    glu_b = glu_ref[:, CONF_CH:].astype(F32)
    ubuf_ref[halo:halo + ts, :] = glu_a * _sigmoid(glu_b)

    dw_b = vec_ref[0:1, :]
    cln_g = vec_ref[1:2, :]
    cln_b = vec_ref[2:3, :]
    ln1_g = vec_ref[3:4, :]
    ln1_b = vec_ref[4:5, :]

    span = ts + halo - 8
    for s in range(1, 8):
        sh_ref[s - 1] = ubuf_ref[s:s + span, :]

    def tap_rows(j, r0):
        o = halo - (CONF_KERNEL - 1) + j
        q, s = o // 8, o % 8
        if s == 0:
            return ubuf_ref[r0 + o:r0 + o + rc, :]
        return sh_ref[s - 1, r0 + 8 * q:r0 + 8 * q + rc, :]

    hs = ts // 2
    for h0 in range(0, ts, hs):
        rows = slice(h0, h0 + hs)
        gated_a = _sigmoid(ga_ref[rows, :].astype(F32)) * _dot(o_ref[rows, :], woa_ref[...])
        for r0 in range(h0, h0 + hs, rc):
            tap_w = lambda j: jnp.concatenate([dww_ref[j]] * (rc // 8), axis=0)
            acc = tap_w(0) * tap_rows(0, r0)
            for j in range(1, CONF_KERNEL):
                acc = acc + tap_w(j) * tap_rows(j, r0)
            conv_ref[r0:r0 + rc, :] = acc
        uc = _silu(_layer_norm(conv_ref[rows, :] + dw_b, cln_g, cln_b))
        branch_b = _dot(uc.astype(BF16), wob_ref[...])
        hmix = gated_a + _sigmoid(gb_ref[rows, :].astype(F32)) * branch_b
        mix = _dot(hmix.astype(BF16), wo_ref[...])
        x1 = _layer_norm(DEEPNORM_ALPHA * x_ref[rows, :] + mix, ln1_g, ln1_b)
        x1_ref[rows, :] = x1
        x1b_ref[rows, :] = x1.astype(BF16)
        x1p_ref[rows, :] = _pack_bf16_pairs(x1)


def _mixer(proj, o_gdn, x, woa, wob, wo, dww, vec, batch, seq, ts):
    t = batch * seq
    nt = seq // ts
    rows = lambda b, n: b * nt + n
    full = lambda b, n: (0, 0)
    kern = functools.partial(_mixer_kernel, ts=ts, rc=32)
    return pl.pallas_call(
        kern,
        grid=(batch, nt),
        in_specs=[
            pl.BlockSpec((ts, 2 * CONF_CH), lambda b, n: (rows(b, n), 2)),
            pl.BlockSpec((ts, D_MODEL), lambda b, n: (rows(b, n), 6)),
            pl.BlockSpec((ts, D_MODEL), lambda b, n: (rows(b, n), 7)),
            pl.BlockSpec((ts, GDN_WIDTH), lambda b, n: (rows(b, n), 0)),
            pl.BlockSpec((ts, D_MODEL), lambda b, n: (rows(b, n), 0)),
            pl.BlockSpec((GDN_WIDTH, D_MODEL), full),
            pl.BlockSpec((CONF_CH, D_MODEL), full),
            pl.BlockSpec((D_MODEL, D_MODEL), full),
            pl.BlockSpec((CONF_KERNEL, 8, CONF_CH), lambda b, n: (0, 0, 0)),
            pl.BlockSpec((8, D_MODEL), full),
        ],
        out_specs=[pl.BlockSpec((ts, D_MODEL), lambda b, n: (rows(b, n), 0)),
                   pl.BlockSpec((ts, D_MODEL), lambda b, n: (rows(b, n), 0)),
                   pl.BlockSpec((ts, D_MODEL // 2), lambda b, n: (rows(b, n), 0))],
        out_shape=[jax.ShapeDtypeStruct((t, D_MODEL), F32),
                   jax.ShapeDtypeStruct((t, D_MODEL), BF16),
                   jax.ShapeDtypeStruct((t, D_MODEL // 2), U32)],
        scratch_shapes=[pltpu.VMEM((CONF_HALO + ts, CONF_CH), F32),
                        pltpu.VMEM((7, ts + CONF_HALO - 8, CONF_CH), F32),
                        pltpu.VMEM((ts, CONF_CH), F32)],
        compiler_params=_params("parallel", "arbitrary"),
        name="mixer",
    )(proj, proj, proj, o_gdn, x, woa, wob, wo, dww, vec)


def _route_kernel(x_ref, wrh_ref, wrl_ref, bias_ref, eidx_ref, wts_ref, rank_ref, cnt_ref, carry_ref, *, tt):
    @pl.when(pl.program_id(0) == 0)
    def _():
        carry_ref[...] = jnp.zeros_like(carry_ref)

    xh, xl = _split(x_ref[...])
    wrh = wrh_ref[...]
    logits = _dot_nt(wrh, xh) + _dot_nt(wrh, xl) + _dot_nt(wrl_ref[...], xh)
    s = _sigmoid(logits)
    biased = s + bias_ref[...]

    sub = lax.broadcasted_iota(I32, (GROUP_SIZE, tt), 0)
    groups = [biased[g * GROUP_SIZE:(g + 1) * GROUP_SIZE, :] for g in range(N_GROUPS)]
    gs = []
    for bg in groups:
        m1 = jnp.max(bg, axis=0, keepdims=True)
        first = jnp.min(jnp.where(bg == m1, sub, GROUP_SIZE), axis=0, keepdims=True)
        m2 = jnp.max(jnp.where(sub == first, -jnp.inf, bg), axis=0, keepdims=True)
        gs.append(m1 + m2)

    masked_parts = []
    for g in range(N_GROUPS):
        beaten = jnp.zeros((1, tt), I32)
        for o in range(N_GROUPS):
            if o == g:
                continue
            wins = (gs[o] >= gs[g]) if o < g else (gs[o] > gs[g])
            beaten = beaten + wins.astype(I32)
        keep = jnp.broadcast_to(beaten < TOPK_GROUPS, (GROUP_SIZE, tt))
        masked_parts.append(jnp.where(keep, groups[g], -jnp.inf))
    masked = jnp.concatenate(masked_parts, axis=0)

    eiota = lax.broadcasted_iota(I32, (N_EXPERTS, tt), 0)
    sel_all = jnp.zeros((N_EXPERTS, tt), F32)
    picks = []
    for _ in range(TOP_K):
        m = jnp.max(masked, axis=0, keepdims=True)
        idx = jnp.min(jnp.where(masked == m, eiota, N_EXPERTS), axis=0, keepdims=True)
        onehot = eiota == idx
        picks.append((idx, onehot))
        sel_all = jnp.where(onehot, 1.0, sel_all)
        masked = jnp.where(onehot, -jnp.inf, masked)

    tr = lax.broadcasted_iota(I32, (tt, tt), 0)
    tc = lax.broadcasted_iota(I32, (tt, tt), 1)
    before = (tr < tc).astype(BF16)
    sel_b = sel_all.astype(BF16)
    carry = carry_ref[...]
    rank_all = _dot(sel_b, before) + carry[:, 0:1]
    carry_new = carry + _dot(sel_b, jnp.ones((tt, LANES), BF16))
    carry_ref[...] = carry_new
    cnt_ref[...] = carry_new

    s_sel = [jnp.sum(jnp.where(oh, s, 0.0), axis=0, keepdims=True) for _, oh in picks]
    total = s_sel[0]
    for v in s_sel[1:]:
        total = total + v
    for k, (idx, oh) in enumerate(picks):
        eidx_ref[k:k + 1, :] = idx
        wts_ref[k:k + 1, :] = s_sel[k] / total * ROUTED_SCALE
        rank_ref[k:k + 1, :] = jnp.sum(jnp.where(oh, rank_all, 0.0), axis=0, keepdims=True).astype(I32)


def _route(x1, wr_hi, wr_lo, bias, tt):
    t = x1.shape[0]
    kern = functools.partial(_route_kernel, tt=tt)
    return pl.pallas_call(
        kern,
        grid=(t // tt,),
        in_specs=[pl.BlockSpec((tt, D_MODEL), lambda i: (i, 0)),
                  pl.BlockSpec((N_EXPERTS, D_MODEL), lambda i: (0, 0)),
                  pl.BlockSpec((N_EXPERTS, D_MODEL), lambda i: (0, 0)),
                  pl.BlockSpec((N_EXPERTS, tt), lambda i: (0, 0))],
        out_specs=[pl.BlockSpec((TOP_K, tt), lambda i: (0, i)),
                   pl.BlockSpec((TOP_K, tt), lambda i: (0, i)),
                   pl.BlockSpec((TOP_K, tt), lambda i: (0, i)),
                   pl.BlockSpec((N_EXPERTS, LANES), lambda i: (0, 0))],
        out_shape=[jax.ShapeDtypeStruct((TOP_K, t), I32),
                   jax.ShapeDtypeStruct((TOP_K, t), F32),
                   jax.ShapeDtypeStruct((TOP_K, t), I32),
                   jax.ShapeDtypeStruct((N_EXPERTS, LANES), F32)],
        scratch_shapes=[pltpu.VMEM((N_EXPERTS, LANES), F32)],
        compiler_params=_params("arbitrary"),
        name="route",
    )(x1, wr_hi, wr_lo, bias)


def _plan_kernel(cnt_ref, pstart_ref, plan_ref):
    e, nb = N_EXPERTS, plan_ref.shape[1]
    counts = cnt_ref[...]
    nblk = jnp.floor((counts + (ROW_BLOCK - 1)) * (1.0 / ROW_BLOCK))
    hi = jnp.floor(nblk * (1.0 / 256.0))
    lo = nblk - 256.0 * hi
    r = lax.broadcasted_iota(I32, (e, e), 0)
    c = lax.broadcasted_iota(I32, (e, e), 1)
    ltri = (r >= c).astype(BF16)
    bend = 256.0 * _dot(ltri, hi.astype(BF16)) + _dot(ltri, lo.astype(BF16))
    pend = bend * ROW_BLOCK
    pstart = pend - nblk * ROW_BLOCK
    pstart_ref[...] = pstart.astype(I32)

    bs = (lax.broadcasted_iota(I32, (e, nb), 1) * ROW_BLOCK).astype(F32)
    pend_b = jnp.broadcast_to(pend[:, 0:1], (e, nb))
    pstart_b = jnp.broadcast_to(pstart[:, 0:1], (e, nb))
    used_b = jnp.broadcast_to((pstart + counts)[:, 0:1], (e, nb))
    owner = jnp.sum(jnp.where(pend_b <= bs, 1.0, 0.0), axis=0, keepdims=True)
    inside = jnp.where(pstart_b <= bs, jnp.where(bs < pend_b, 1.0, 0.0), 0.0)
    real = jnp.sum(inside * jnp.clip(used_b - bs, 0.0, float(ROW_BLOCK)), axis=0, keepdims=True)
    plan_ref[0:1, :] = jnp.minimum(owner, float(e - 1)).astype(I32)
    plan_ref[1:2, :] = real.astype(I32)
    plan_ref[2:8, :] = jnp.zeros((6, nb), I32)


def _plan(cnt, n_blocks):
    nb = -(-n_blocks // LANES) * LANES
    pstart, plan = pl.pallas_call(
        _plan_kernel,
        out_shape=[jax.ShapeDtypeStruct((N_EXPERTS, LANES), I32),
                   jax.ShapeDtypeStruct((8, nb), I32)],
        name="plan",
    )(cnt)
    return pstart[:, 0], plan[0, :n_blocks], plan[1, :n_blocks]


def _dest_kernel(eidx_ref, rank_ref, pstart_ref, dest_ref):
    eidx = eidx_ref[...]
    acc = rank_ref[...]
    for e in range(N_EXPERTS):
        acc = acc + jnp.where(eidx == e, pstart_ref[e], 0)
    dest_ref[...] = acc


def _dest(eidx_t, rank_t, pstart, tt):
    t = eidx_t.shape[1]
    return pl.pallas_call(
        _dest_kernel,
        grid=(t // tt,),
        in_specs=[pl.BlockSpec((TOP_K, tt), lambda i: (0, i)),
                  pl.BlockSpec((TOP_K, tt), lambda i: (0, i)),
                  pl.BlockSpec(memory_space=pltpu.SMEM)],
        out_specs=pl.BlockSpec((TOP_K, tt), lambda i: (0, i)),
        out_shape=jax.ShapeDtypeStruct((TOP_K, t), I32),
        compiler_params=_params("parallel"),
        name="dest",
    )(eidx_t, rank_t, pstart)


def _dispatch(dest_kt, x1p, n_rows):
    t, width = x1p.shape
    info = plsc.get_sparse_core_info()
    nc, nw = info.num_cores, info.num_cores * info.num_subcores
    chunk = SC_SCATTER_ROWS
    per_w = t // nw
    n_chunks = per_w // chunk
    assert per_w % chunk == 0
    idx = dest_kt.reshape(TOP_K, nw, n_chunks, chunk).transpose(1, 2, 0, 3).reshape(nw, n_chunks * TOP_K, chunk)
    mesh = plsc.VectorSubcoreMesh(core_axis_name="c", subcore_axis_name="s")

    @functools.partial(
        pl.kernel, mesh=mesh, name="dispatch",
        out_type=jax.ShapeDtypeStruct((n_rows, width), x1p.dtype),
        scratch_types=[pltpu.VMEM((n_chunks * TOP_K, chunk), I32),
                       pltpu.VMEM((chunk, width), x1p.dtype),
                       pltpu.SemaphoreType.DMA])
    def scatter(x_hbm, idx_hbm, xs_hbm, idx_v, rows_v, sem):
        wid = lax.axis_index("s") * nc + lax.axis_index("c")
        base = wid * per_w
        pltpu.sync_copy(idx_hbm.at[wid], idx_v)
        for j in range(n_chunks):
            pltpu.sync_copy(x_hbm.at[pl.ds(base + j * chunk, chunk)], rows_v)
            copies = [pltpu.make_async_copy(rows_v, xs_hbm.at[idx_v.at[j * TOP_K + k]], sem)
                      for k in range(TOP_K)]
            for cp in copies:
                cp.start()
            for cp in copies:
                cp.wait()

    return scatter(x1p, idx)


def _gather_rows(table, idx):
    n = idx.shape[0]
    width = table.shape[1]
    info = plsc.get_sparse_core_info()
    nc, nw = info.num_cores, info.num_cores * info.num_subcores
    chunk = SC_GATHER_ROWS
    per_w = n // nw
    n_chunks = per_w // chunk
    assert per_w % (2 * chunk) == 0
    mesh = plsc.VectorSubcoreMesh(core_axis_name="c", subcore_axis_name="s")

    @functools.partial(
        pl.kernel, mesh=mesh, name="gather_rows",
        out_type=jax.ShapeDtypeStruct((n, width), table.dtype),
        scratch_types=[pltpu.VMEM((n_chunks, chunk), I32),
                       pltpu.VMEM((2, chunk, width), table.dtype),
                       pltpu.SemaphoreType.DMA((2,)),
                       pltpu.SemaphoreType.DMA((2,))])
    def gather(table_hbm, idx_hbm, out_hbm, idx_v, rows_v, gsem, osem):
        wid = lax.axis_index("s") * nc + lax.axis_index("c")
        base = wid * per_w
        pltpu.sync_copy(idx_hbm.at[wid], idx_v)

        def fetch(j, b):
            return pltpu.make_async_copy(table_hbm.at[idx_v.at[j]], rows_v.at[b], gsem.at[b])

        def put(j, b):
            return pltpu.make_async_copy(rows_v.at[b], out_hbm.at[pl.ds(base + j * chunk, chunk)], osem.at[b])

        fetch(0, 0).start()

        @pl.loop(0, n_chunks, step=2)
        def _(j0):
            for b in range(2):
                j = j0 + b
                fetch(j, b).wait()

                @pl.when(j + 1 < n_chunks)
                def _():
                    @pl.when(j >= 1)
                    def _():
                        put(j - 1, 1 - b).wait()

                    fetch(j + 1, 1 - b).start()

                put(j, b).start()

        put(n_chunks - 2, 0).wait()
        put(n_chunks - 1, 1).wait()

    return gather(table, idx.reshape(nw, n_chunks, chunk))


def _xs_copy(xs_hbm, xbuf, isem, j, slot):
    return pltpu.make_async_copy(xs_hbm.at[pl.ds(j * ROW_BLOCK, ROW_BLOCK)], xbuf.at[slot], isem.at[slot])


def _ys_copy(ybuf, ys_hbm, osem, j, slot):
    return pltpu.make_async_copy(ybuf.at[slot], ys_hbm.at[pl.ds(j * ROW_BLOCK, ROW_BLOCK)], osem.at[slot])


def _experts_kernel(be_ref, nv_ref, xs_hbm, wg_ref, wu_ref, wd_ref, ys_hbm,
                    xbuf, ybuf, wgu_s, wd_s, cur_ref, isem, osem):
    i = pl.program_id(0)
    n_valid = nv_ref[i]
    half = EXPERT_SUB
    slot = lax.rem(i, EXPERT_IN_SLOTS)
    oslot = lax.rem(i, 2)

    @pl.when(i == 0)
    def _():
        cur_ref[0] = -1
        for j in range(2):
            @pl.when(nv_ref[j] > 0)
            def _():
                _xs_copy(xs_hbm, xbuf, isem, j, j).start()

    @pl.when(nv_ref[i + 2] > 0)
    def _():
        _xs_copy(xs_hbm, xbuf, isem, i + 2, lax.rem(i + 2, EXPERT_IN_SLOTS)).start()

    @pl.when((i >= 2) & (nv_ref[jnp.maximum(i - 2, 0)] > 0))
    def _():
        _ys_copy(ybuf, ys_hbm, osem, i - 2, oslot).wait()

    @pl.when((n_valid > 0) & (cur_ref[0] != be_ref[i]))
    def _():
        wgu_s[:, :EXPERT_FF] = wg_ref[...].astype(BF16)
        wgu_s[:, EXPERT_FF:] = wu_ref[...].astype(BF16)
        wd_s[...] = wd_ref[...].astype(BF16)
        cur_ref[0] = be_ref[i]

    def rows_bf16(r0):
        valid = lax.broadcasted_iota(I32, (half, xbuf.shape[2]), 0) + r0 < n_valid
        lo, hi = _unpack_bf16_pairs(jnp.where(valid, xbuf[slot, r0:r0 + half, :], jnp.zeros((), U32)))
        return jnp.concatenate([lo.astype(BF16), hi.astype(BF16)], axis=1)

    def hidden(gu):
        return (_silu(gu[:, :EXPERT_FF]) * gu[:, EXPERT_FF:]).astype(BF16)

    @pl.when(n_valid > 0)
    def _():
        _xs_copy(xs_hbm, xbuf, isem, i, slot).wait()

    @pl.when(n_valid > half)
    def _():
        xa, xb = rows_bf16(0), rows_bf16(half)
        gua = _dot(xa, wgu_s[...])
        gub = _dot(xb, wgu_s[...])
        ya = _dot(hidden(gua), wd_s[...])
        yb = _dot(hidden(gub), wd_s[...])
        ybuf[oslot, 0:half, :] = _pack_bf16_pairs(ya)
        ybuf[oslot, half:, :] = _pack_bf16_pairs(yb)

    @pl.when((n_valid > 0) & (n_valid <= half))
    def _():
        ya = _dot(hidden(_dot(rows_bf16(0), wgu_s[...])), wd_s[...])
        ybuf[oslot, 0:half, :] = _pack_bf16_pairs(ya)
        ybuf[oslot, half:, :] = jnp.zeros((half, ybuf.shape[2]), ybuf.dtype)

    @pl.when(n_valid > 0)
    def _():
        _ys_copy(ybuf, ys_hbm, osem, i, oslot).start()


def _experts(block_e, n_valid, xs, wg, wu, wd, layer):
    n_rows, width = xs.shape
    steps = n_rows // ROW_BLOCK + 2
    be = jnp.concatenate([block_e, jnp.full((2,), N_EXPERTS - 1, I32)])
    nv = jnp.concatenate([n_valid, jnp.zeros((4,), I32)])
    grid_spec = pltpu.PrefetchScalarGridSpec(
        num_scalar_prefetch=2,
        grid=(steps,),
        in_specs=[pl.BlockSpec(memory_space=pl.ANY),
                  pl.BlockSpec((None, None, D_MODEL, EXPERT_FF), lambda i, be, nv: (layer, be[i], 0, 0)),
                  pl.BlockSpec((None, None, D_MODEL, EXPERT_FF), lambda i, be, nv: (layer, be[i], 0, 0)),
                  pl.BlockSpec((None, None, EXPERT_FF, D_MODEL), lambda i, be, nv: (layer, be[i], 0, 0))],
        out_specs=pl.BlockSpec(memory_space=pl.ANY),
        scratch_shapes=[pltpu.VMEM((EXPERT_IN_SLOTS, ROW_BLOCK, width), xs.dtype),
                        pltpu.VMEM((2, ROW_BLOCK, width), xs.dtype),
                        pltpu.VMEM((D_MODEL, 2 * EXPERT_FF), BF16),
                        pltpu.VMEM((EXPERT_FF, D_MODEL), BF16),
                        pltpu.SMEM((1,), I32),
                        pltpu.SemaphoreType.DMA((EXPERT_IN_SLOTS,)),
                        pltpu.SemaphoreType.DMA((2,))],
    )
    return pl.pallas_call(
        _experts_kernel,
        grid_spec=grid_spec,
        out_shape=jax.ShapeDtypeStruct((n_rows, width), xs.dtype),
        compiler_params=_params("arbitrary"),
        name="experts",
    )(be, nv, xs, wg, wu, wd)


def _shared_kernel(x1b_ref, wsg_ref, wsu_ref, wsd_ref, anchor_ref, sh_ref):
    del anchor_ref
    xb = x1b_ref[...]
    hid = (_silu(_dot(xb, wsg_ref[...])) * _dot(xb, wsu_ref[...])).astype(BF16)
    sh_ref[...] = _dot(hid, wsd_ref[...]).astype(sh_ref.dtype)


def _shared(x1b, wsg, wsu, wsd, anchor, tt, row0, nrows):
    full = lambda i: (0, 0)
    first = row0 // tt
    return pl.pallas_call(
        _shared_kernel,
        grid=(nrows // tt,),
        in_specs=[pl.BlockSpec((tt, D_MODEL), lambda i: (first + i, 0)),
                  pl.BlockSpec((D_MODEL, SHARED_FF), full),
                  pl.BlockSpec((D_MODEL, SHARED_FF), full),
                  pl.BlockSpec((SHARED_FF, D_MODEL), full),
                  pl.BlockSpec(memory_space=pl.ANY)],
        out_specs=pl.BlockSpec((tt, D_MODEL), lambda i: (i, 0)),
        out_shape=jax.ShapeDtypeStruct((nrows, D_MODEL), BF16),
        compiler_params=_params("parallel"),
        name="shared",
    )(x1b, wsg, wsu, wsd, anchor)


def _combine_kernel(yg_ref, w_ref, x1_ref, sha_ref, shb_ref, vec_ref, x2_ref, *, half_steps):
    in_first = pl.program_id(0) < half_steps
    shared = jnp.where(in_first, sha_ref[...], shb_ref[...]).astype(F32)
    w = w_ref[...]
    half = D_MODEL // 2
    acc_lo, acc_hi = shared[:, :half], shared[:, half:]
    for k in range(TOP_K):
        lo, hi = _unpack_bf16_pairs(yg_ref[k])
        acc_lo = acc_lo + w[:, k:k + 1] * lo
        acc_hi = acc_hi + w[:, k:k + 1] * hi
    acc = jnp.concatenate([acc_lo, acc_hi], axis=1)
    x2 = _layer_norm(DEEPNORM_ALPHA * x1_ref[...] + acc, vec_ref[0:1, :], vec_ref[1:2, :])
    x2_ref[...] = x2


def _combine(yg, w_tok, x1, shared_a, shared_b, vec, tt):
    t = x1.shape[0]
    half_steps = shared_a.shape[0] // tt
    full = lambda i: (0, 0)
    return pl.pallas_call(
        functools.partial(_combine_kernel, half_steps=half_steps),
        grid=(t // tt,),
        in_specs=[pl.BlockSpec((TOP_K, tt, yg.shape[2]), lambda i: (0, i, 0)),
                  pl.BlockSpec((tt, TOP_K), lambda i: (i, 0)),
                  pl.BlockSpec((tt, D_MODEL), lambda i: (i, 0)),
                  pl.BlockSpec((tt, D_MODEL), lambda i: (jnp.minimum(i, half_steps - 1), 0)),
                  pl.BlockSpec((tt, D_MODEL), lambda i: (jnp.maximum(i - half_steps, 0), 0)),
                  pl.BlockSpec((8, D_MODEL), full)],
        out_specs=pl.BlockSpec((tt, D_MODEL), lambda i: (i, 0)),
        out_shape=jax.ShapeDtypeStruct((t, D_MODEL), F32),
        compiler_params=_params("parallel"),
        name="combine",
    )(yg, w_tok, x1, shared_a, shared_b, vec)


def _pad_rows(a, rows):
    return jnp.zeros((rows, a.shape[-1]), F32).at[:a.shape[0]].set(a.astype(F32))


def _layer(x, p, stacked, layer, batch, seq):
    t = batch * seq
    w_main, wa_hi, wa_lo, wb_hi, wb_lo = _regroup(jnp.swapaxes(stacked["w_in"], 1, 2), layer, 512)

    tm = min(1024, t)
    proj = _proj(x, w_main, tm, 2048)

    prm = jnp.zeros((8, LANES), F32)
    prm = prm.at[0, :GDN_HEADS].set(p["a_log"]).at[1, :GDN_HEADS].set(p["dt_bias"])
    ong = p["o_norm_g"].reshape(1, HEAD_DIM).astype(F32)
    u, w, qd, m2, eg = _gdn_intra(proj, x, wa_hi, wa_lo, wb_hi, wb_lo, p["conv_qkv"].astype(F32), prm,
                                  seq, min(GDN_TILE, seq))
    o_gdn = _gdn_scan(u, w, qd, m2, eg, proj, ong, batch, seq)

    ts = min(256, seq)
    dww = jnp.broadcast_to(p["dw_w"].astype(F32)[:, None, :], (CONF_KERNEL, 8, CONF_CH))
    vec = _pad_rows(jnp.stack([p["dw_b"], p["cln_g"], p["cln_b"], p["ln1_g"], p["ln1_b"]]), 8)
    x1, x1b, x1p = _mixer(proj, o_gdn, x, p["w_oa"].astype(BF16), p["w_ob"].astype(BF16),
                          p["w_o"].astype(BF16), dww, vec, batch, seq, ts)

    tt_r = min(512, t)
    wr_t = p["w_router"].T.astype(F32)
    wr_hi = wr_t.astype(BF16)
    wr_lo = (wr_t - wr_hi.astype(F32)).astype(BF16)
    bias = jnp.broadcast_to(p["router_bias"].astype(F32)[:, None], (N_EXPERTS, tt_r))
    eidx_t, wts_t, rank_t, cnt = _route(x1, wr_hi, wr_lo, bias, tt_r)

    n_blocks = -(-(t * TOP_K + N_EXPERTS * (ROW_BLOCK - 1)) // ROW_BLOCK)
    n_rows = n_blocks * ROW_BLOCK
    pstart, block_e, n_valid = _plan(cnt, n_blocks)

    dest = _dest(eidx_t, rank_t, pstart, min(2048, t))
    sh_w = (p["w_sh_gate"].astype(BF16), p["w_sh_up"].astype(BF16), p["w_sh_down"].astype(BF16))
    tt_s = min(512, t // 2)
    xs = _dispatch(dest, x1p, n_rows)
    shared_a = _shared(x1b, *sh_w, dest, tt_s, 0, t // 2)
    ys = _experts(block_e, n_valid, xs, stacked["w_gate_e"], stacked["w_up_e"], stacked["w_down_e"], layer)
    yg = _gather_rows(ys, dest.reshape(TOP_K * t)).reshape(TOP_K, t, ys.shape[1])
    shared_b = _shared(x1b, *sh_w, ys, tt_s, t // 2, t // 2)
    vec2 = _pad_rows(jnp.stack([p["ln2_g"], p["ln2_b"]]), 8)
    return _combine(yg, wts_t.T, x1, shared_a, shared_b, vec2, min(256, t // 2))


_PARAM_NAMES = ("w_in", "conv_qkv", "a_log", "dt_bias", "o_norm_g", "w_oa", "dw_w", "dw_b", "cln_g",
                "cln_b", "w_ob", "w_o", "ln1_g", "ln1_b", "w_router", "router_bias", "w_gate_e",
                "w_up_e", "w_down_e", "w_sh_gate", "w_sh_up", "w_sh_down", "ln2_g", "ln2_b")


_EXPERT_WEIGHTS = ("w_in", "w_gate_e", "w_up_e", "w_down_e")


def kernel(x, w_in, conv_qkv, a_log, dt_bias, o_norm_g, w_oa, dw_w, dw_b, cln_g, cln_b, w_ob, w_o,
           ln1_g, ln1_b, w_router, router_bias, w_gate_e, w_up_e, w_down_e, w_sh_gate, w_sh_up,
           w_sh_down, ln2_g, ln2_b):
    stacked = dict(zip(_PARAM_NAMES, (w_in, conv_qkv, a_log, dt_bias, o_norm_g, w_oa, dw_w, dw_b, cln_g,
                                      cln_b, w_ob, w_o, ln1_g, ln1_b, w_router, router_bias, w_gate_e,
                                      w_up_e, w_down_e, w_sh_gate, w_sh_up, w_sh_down, ln2_g, ln2_b)))
    batch, seq, d = x.shape
    assert d == D_MODEL and seq % CHUNK == 0
    xf = x.reshape(batch * seq, d).astype(F32)
    for layer in range(w_in.shape[0]):
        p = {name: arr[layer] for name, arr in stacked.items() if name not in _EXPERT_WEIGHTS}
        xf = _layer(xf, p, stacked, layer, batch, seq)
    return xf.reshape(batch, seq, d).astype(x.dtype)
```

```python
import functools

import jax
import jax.numpy as jnp
import numpy as np
from jax import lax
from jax.experimental import pallas as pl
from jax.experimental.pallas import tpu as pltpu
from jax.experimental.pallas import tpu_sc as plsc

F32 = jnp.float32
BF16 = jnp.bfloat16
I32 = jnp.int32
U32 = jnp.uint32
HI_MASK = np.uint32(0xFFFF0000)

D_MODEL = 1024
GDN_HEADS = 8
HEAD_DIM = 128
GDN_WIDTH = GDN_HEADS * HEAD_DIM
SHORT_CONV = 4
CONV_HALO = 16
CHUNK = 64
SOLVE_BLOCK = 16
GDN_TILE = 256
SCAN_CHUNKS = 4
SCAN_SEQS = 2
CONF_CH = D_MODEL
CONF_KERNEL = 31
CONF_HALO = 32
N_EXPERTS = 64
TOP_K = 8
N_GROUPS = 8
GROUP_SIZE = N_EXPERTS // N_GROUPS
TOPK_GROUPS = 4
EXPERT_FF = 256
SHARED_FF = 256
ROUTED_SCALE = 2.5
DEPTH = 2
DEEPNORM_ALPHA = (2 * DEPTH) ** 0.25
EPS = 1e-6

LANES = 128
PROJ_COLS = 8 * D_MODEL
ROW_BLOCK = 512
EXPERT_SUB = ROW_BLOCK // 2
EXPERT_IN_SLOTS = 3
SC_SCATTER_ROWS = 128
SC_GATHER_ROWS = 64
VMEM_LIMIT = 56 * 1024 * 1024


def _params(*sem):
    return pltpu.CompilerParams(dimension_semantics=sem, vmem_limit_bytes=VMEM_LIMIT)


def _dot(a, b):
    return jnp.dot(a, b, preferred_element_type=F32)


def _dot_nt(a, b):
    return lax.dot_general(a, b, (((1,), (1,)), ((), ())), preferred_element_type=F32)


def _split(a):
    hi = a.astype(BF16)
    lo = (a - hi.astype(F32)).astype(BF16)
    return hi, lo


def _dot3(a, b):
    ah, al = _split(a)
    bh, bl = _split(b)
    return _dot(ah, bh) + _dot(al, bh) + _dot(ah, bl)


def _sigmoid(x):
    return 1.0 / (1.0 + jnp.exp(-x))


def _silu(x):
    return x * _sigmoid(x)


def _layer_norm(y, g, b):
    mu = jnp.mean(y, axis=-1, keepdims=True)
    yc = y - mu
    var = jnp.mean(yc * yc, axis=-1, keepdims=True)
    return yc * lax.rsqrt(var + EPS) * g + b


def _proj_kernel(a_ref, w_ref, o_ref):
    o_ref[...] = _dot_nt(a_ref[...].astype(BF16), w_ref[...]).astype(o_ref.dtype)


def _regroup_kernel(wt_hbm, main_ref, ahi_ref, alo_ref, bhi_ref, blo_ref, buf, ab_buf, sem, *, layer, tr):
    i = pl.program_id(0)
    half = PROJ_COLS // 2
    skip = 2 * GDN_HEADS
    start = pl.multiple_of(jnp.where(i < half // tr, 0, skip) + i * tr, skip)
    cp = pltpu.make_async_copy(wt_hbm.at[layer, pl.ds(start, tr), :], buf, sem)
    cp.start()
    cp.wait()
    main_ref[...] = buf[...].astype(BF16)

    @pl.when(i == 0)
    def _():
        cp_ab = pltpu.make_async_copy(wt_hbm.at[layer, pl.ds(half, LANES), :], ab_buf, sem)
        cp_ab.start()
        cp_ab.wait()
        cols = ab_buf[...].T
        wide = jnp.concatenate([cols, jnp.zeros_like(cols)], axis=1)
        lane = lax.broadcasted_iota(I32, cols.shape, 1)
        for off, hi_ref, lo_ref in ((0, ahi_ref, alo_ref), (GDN_HEADS, bhi_ref, blo_ref)):
            hi, lo = _split(jnp.where(lane < GDN_HEADS, wide[:, off:off + LANES], 0.0))
            hi_ref[...] = hi
            lo_ref[...] = lo


def _regroup(w_in_t, layer, tr):
    d = w_in_t.shape[2]
    small = jax.ShapeDtypeStruct((d, LANES), BF16)
    whole = lambda i: (0, 0)
    return pl.pallas_call(
        functools.partial(_regroup_kernel, layer=layer, tr=tr),
        grid=(PROJ_COLS // tr,),
        in_specs=[pl.BlockSpec(memory_space=pl.ANY)],
        out_specs=[pl.BlockSpec((tr, d), lambda i: (i, 0))] + [pl.BlockSpec((d, LANES), whole)] * 4,
        out_shape=[jax.ShapeDtypeStruct((PROJ_COLS, d), BF16), small, small, small, small],
        scratch_shapes=[pltpu.VMEM((tr, d), F32), pltpu.VMEM((LANES, d), F32), pltpu.SemaphoreType.DMA],
        compiler_params=_params("arbitrary"),
        name="regroup",
    )(w_in_t)


def _proj(xb, w, tm, tn):
    m, k = xb.shape
    n = w.shape[0]
    return pl.pallas_call(
        _proj_kernel,
        grid=(m // tm, n // tn),
        in_specs=[pl.BlockSpec((tm, k), lambda i, j: (i, 0)),
                  pl.BlockSpec((tn, k), lambda i, j: (j, 0))],
        out_specs=pl.BlockSpec((tm, tn), lambda i, j: (i, j)),
        out_shape=jax.ShapeDtypeStruct((m, n), BF16),
        compiler_params=_params("parallel", "parallel"),
        name="proj",
    )(xb, w)


def _unit_lower_inverse4(als):
    c = CHUNK
    n = als[0].shape[1]
    row = lax.broadcasted_iota(I32, (c, n), 0)
    col = jnp.bitwise_and(lax.broadcasted_iota(I32, (c, n), 1), c - 1)
    shift = SOLVE_BLOCK.bit_length() - 1
    same = jnp.right_shift(row, shift) == jnp.right_shift(col, shift)
    eye = (row == col).astype(F32)
    cshift = c.bit_length() - 1
    brow = jnp.right_shift(lax.broadcasted_iota(I32, (n, n), 0), cshift)
    bcol = jnp.right_shift(lax.broadcasted_iota(I32, (n, n), 1), cshift)
    on_diag = brow == bcol

    def mm(x, y):
        yb = y.astype(BF16)
        bd = jnp.where(on_diag, jnp.concatenate([yb] * (n // c), axis=0), jnp.zeros((), BF16))
        return _dot(x.astype(BF16), bd)

    a_diag = [jnp.where(same, al, 0.0) for al in als]
    a_off = [al - ad for al, ad in zip(als, a_diag)]
    bp = [-ad for ad in a_diag]
    p = [eye + b for b in bp]
    for _ in range(3):
        bp = [mm(b, b) for b in bp]
        p = [x + mm(x, b) for x, b in zip(p, bp)]
    n1 = [mm(x, ao) for x, ao in zip(p, a_off)]
    n2 = [mm(x, x) for x in n1]
    q = [x + mm(y, x) for x, y in zip(p, n2)]
    return [x - mm(y, x) for x, y in zip(q, n1)]


def _shift_selectors(rt):
    sel = np.zeros((SHORT_CONV * rt, rt), np.float32)
    sel_halo = np.zeros((SHORT_CONV * 8, CONV_HALO), np.float32)
    for d in range(SHORT_CONV):
        for t in range(d, rt):
            sel[d * rt + t, t - d] = 1.0
        for t in range(d):
            sel_halo[d * 8 + t, CONV_HALO + t - d] = 1.0
    return jnp.asarray(sel, BF16), jnp.asarray(sel_halo, BF16)


def _gdn_intra_kernel(qkv_ref, prev_ref, x_ref, sel_ref, selh_ref, wah_ref, wal_ref, wbh_ref, wbl_ref, cw_ref,
                      prm_ref, u_ref, w_ref, qd_ref, m2_ref, eg_ref, xs_ref, *, rt, tiles_per_seq):
    c = CHUNK
    nc = rt // c
    first = (pl.program_id(0) % tiles_per_seq) == 0
    edge = jnp.where(first, 0.0, _dot(selh_ref[...], prev_ref[...]))

    def move_rows(lo, hi):
        moved = _dot(sel_ref[...], qkv_ref[:, lo:hi])
        for d in range(SHORT_CONV):
            xs_ref[d, 0:8, lo:hi] = moved[d * rt:d * rt + 8] + edge[d * 8:(d + 1) * 8, lo:hi]
            xs_ref[d, 8:rt, lo:hi] = moved[d * rt + 8:(d + 1) * rt]

    for part in range(3):
        move_rows(part * GDN_WIDTH, part * GDN_WIDTH + 2 * HEAD_DIM)

    xh, xl = _split(x_ref[...])

    def proj3(wh_ref, wl_ref):
        wh = wh_ref[...]
        return _dot(xh, wh) + _dot(xl, wh) + _dot(xh, wl_ref[...])

    a_raw = proj3(wah_ref, wal_ref)
    b_raw = proj3(wbh_ref, wbl_ref)
    sp_in = a_raw + prm_ref[1:2, :]
    softplus = jnp.maximum(sp_in, 0.0) + jnp.log(1.0 + jnp.exp(-jnp.abs(sp_in)))
    g = -jnp.exp(prm_ref[0:1, :]) * softplus
    beta = _sigmoid(b_raw)

    cshift = c.bit_length() - 1
    r2 = lax.broadcasted_iota(I32, (rt, rt), 0)
    c2 = lax.broadcasted_iota(I32, (rt, rt), 1)
    same_chunk = jnp.right_shift(r2, cshift) == jnp.right_shift(c2, cshift)
    ltri = jnp.where(r2 >= c2, jnp.where(same_chunk, 1.0, 0.0), 0.0).astype(BF16)
    g_hi = g.astype(BF16)
    g_r = g - g_hi.astype(F32)
    g_mid = g_r.astype(BF16)
    g_lo = (g_r - g_mid.astype(F32)).astype(BF16)
    gc = _dot(ltri, g_hi) + _dot(ltri, g_mid) + _dot(ltri, g_lo)
    gct = gc.T
    egc = jnp.exp(gc)
    gend = jnp.concatenate(
        [jnp.broadcast_to(gc[ci * c + c - 1:ci * c + c, :], (c, LANES)) for ci in range(nc)], axis=0)
    kfac = jnp.exp(gend - gc)
    bege = beta * egc
    for ci in range(nc):
        last = ci * c + c - 1
        eg_ref[ci * GDN_HEADS:(ci + 1) * GDN_HEADS, :] = jnp.broadcast_to(
            jnp.exp(gct[0:GDN_HEADS, last:last + 1]), (GDN_HEADS, LANES))

    lane_t = lax.broadcasted_iota(I32, (rt, LANES), 1) < c
    lane_lo = lax.broadcasted_iota(I32, (c, LANES), 1) < c
    lcol = jnp.bitwise_and(lax.broadcasted_iota(I32, (c, LANES), 1), c - 1)
    rowi = lax.broadcasted_iota(I32, (c, LANES), 0)
    causal = rowi >= lcol
    strict = rowi > lcol

    def conv(base, h):
        lo, hi = base + h * HEAD_DIM, base + (h + 1) * HEAD_DIM
        acc = cw_ref[SHORT_CONV - 1:SHORT_CONV, lo:hi] * xs_ref[0, :, lo:hi]
        for j in range(SHORT_CONV - 1):
            acc = acc + cw_ref[j:j + 1, lo:hi] * xs_ref[SHORT_CONV - 1 - j, :, lo:hi]
        return _silu(acc)

    a_pairs = [[None] * (GDN_HEADS // 2) for _ in range(nc)]
    rhs_pairs = [[None] * (GDN_HEADS // 2) for _ in range(nc)]
    for p in range(GDN_HEADS // 2):
        if p + 1 < GDN_HEADS // 2:
            for part in range(3):
                lo = part * GDN_WIDTH + (p + 1) * 2 * HEAD_DIM
                move_rows(lo, lo + 2 * HEAD_DIM)
        ks, kbs, qs, kds, rhss = [], [], [], [], []
        for h in (2 * p, 2 * p + 1):
            q = conv(0, h)
            k = conv(GDN_WIDTH, h)
            v = conv(2 * GDN_WIDTH, h)
            q = q * lax.rsqrt(jnp.sum(q * q, axis=-1, keepdims=True) + EPS) * (HEAD_DIM ** -0.5)
            k = k * lax.rsqrt(jnp.sum(k * k, axis=-1, keepdims=True) + EPS)
            beta_h = beta[:, h:h + 1]
            qd_ref[:, h * HEAD_DIM:(h + 1) * HEAD_DIM] = (q * egc[:, h:h + 1]).astype(BF16)
            ks.append(k)
            kbs.append(k * beta_h)
            qs.append(q)
            kds.append(k * kfac[:, h:h + 1])
            rhss.append(jnp.concatenate([v * beta_h, k * bege[:, h:h + 1]], axis=1))
        h0, h1 = 2 * p, 2 * p + 1
        gch = jnp.where(lane_t, gc[:, h0:h0 + 1], gc[:, h1:h1 + 1])
        for ci in range(nc):
            rows = slice(ci * c, (ci + 1) * c)
            wk = jnp.concatenate([ks[0][rows], ks[1][rows]], axis=0).astype(BF16)
            lhs = jnp.concatenate([kbs[0][rows], qs[0][rows], kbs[1][rows], qs[1][rows]],
                                  axis=0).astype(BF16)
            out = _dot_nt(lhs, wk)
            gcrow = jnp.concatenate([gct[h0:h0 + 1, rows], gct[h1:h1 + 1, rows]], axis=1)
            diff = gch[rows] - gcrow
            decay = jnp.where(causal, jnp.exp(jnp.where(causal, diff, 0.0)), 0.0)
            a_pairs[ci][p] = jnp.where(strict, jnp.where(lane_lo, out[0:c], out[2 * c:3 * c]) * decay, 0.0)
            qk = jnp.where(lane_lo, out[c:2 * c], out[3 * c:4 * c]) * decay
            kdt = jnp.concatenate([kds[0][rows], kds[1][rows]], axis=0).T
            m0 = ci * 3 * c
            m2_ref[m0:m0 + c, p * LANES:(p + 1) * LANES] = qk.astype(BF16)
            m2_ref[m0 + c:m0 + 3 * c, p * LANES:(p + 1) * LANES] = kdt.astype(BF16)
            rhs_pairs[ci][p] = (rhss[0][rows], rhss[1][rows])

    zeros = jnp.zeros((c, 2 * HEAD_DIM), BF16)
    ngrp = GDN_HEADS // 4
    tls = _unit_lower_inverse4(
        [jnp.concatenate([a_pairs[ci][2 * grp], a_pairs[ci][2 * grp + 1]], axis=1)
         for ci in range(nc) for grp in range(ngrp)])
    for ci in range(nc):
        rows = slice(ci * c, (ci + 1) * c)
        for grp in range(ngrp):
            tl = tls[ci * ngrp + grp]
            for j in range(2):
                p = 2 * grp + j
                r0, r1 = rhs_pairs[ci][p]
                bd = jnp.concatenate([jnp.concatenate([r0.astype(BF16), zeros], axis=1),
                                      jnp.concatenate([zeros, r1.astype(BF16)], axis=1)], axis=0)
                sol = _dot(tl[:, j * LANES:(j + 1) * LANES].astype(BF16), bd)
                for i in range(2):
                    h = 2 * p + i
                    lo, hi = h * HEAD_DIM, (h + 1) * HEAD_DIM
                    u_ref[rows, lo:hi] = sol[:, 2 * i * HEAD_DIM:(2 * i + 1) * HEAD_DIM]
                    w_ref[rows, lo:hi] = sol[:, (2 * i + 1) * HEAD_DIM:(2 * i + 2) * HEAD_DIM].astype(BF16)


def _gdn_intra(proj, x, wa_hi, wa_lo, wb_hi, wb_lo, conv_w, prm, seq, rt):
    t = x.shape[0]
    nc = rt // CHUNK
    kern = functools.partial(_gdn_intra_kernel, rt=rt, tiles_per_seq=seq // rt)
    full = lambda i: (0, 0)
    tile = lambda i: (i, 0)
    sel, sel_halo = _shift_selectors(rt)
    return pl.pallas_call(
        kern,
        grid=(t // rt,),
        in_specs=[
            pl.BlockSpec((rt, 3 * GDN_WIDTH), tile),
            pl.BlockSpec((CONV_HALO, 3 * GDN_WIDTH),
                         lambda i: (jnp.maximum(i * (rt // CONV_HALO) - 1, 0), 0)),
            pl.BlockSpec((rt, D_MODEL), tile),
            pl.BlockSpec(sel.shape, full),
            pl.BlockSpec(sel_halo.shape, full),
            pl.BlockSpec((D_MODEL, LANES), full),
            pl.BlockSpec((D_MODEL, LANES), full),
            pl.BlockSpec((D_MODEL, LANES), full),
            pl.BlockSpec((D_MODEL, LANES), full),
            pl.BlockSpec((SHORT_CONV, 3 * GDN_WIDTH), full),
            pl.BlockSpec((8, LANES), full),
        ],
        out_specs=[pl.BlockSpec((rt, GDN_WIDTH), tile),
                   pl.BlockSpec((rt, GDN_WIDTH), tile),
                   pl.BlockSpec((rt, GDN_WIDTH), tile),
                   pl.BlockSpec((nc * 3 * CHUNK, GDN_HEADS // 2 * LANES), tile),
                   pl.BlockSpec((nc * GDN_HEADS, LANES), tile)],
        out_shape=[jax.ShapeDtypeStruct((t, GDN_WIDTH), F32),
                   jax.ShapeDtypeStruct((t, GDN_WIDTH), BF16),
                   jax.ShapeDtypeStruct((t, GDN_WIDTH), BF16),
                   jax.ShapeDtypeStruct((t // CHUNK * 3 * CHUNK, GDN_HEADS // 2 * LANES), BF16),
                   jax.ShapeDtypeStruct((t // CHUNK * GDN_HEADS, LANES), F32)],
        scratch_shapes=[pltpu.VMEM((SHORT_CONV, rt, 3 * GDN_WIDTH), F32)],
        compiler_params=_params("parallel"),
        name="gdn_intra",
    )(proj, proj, x, sel, sel_halo, wa_hi, wa_lo, wb_hi, wb_lo, conv_w, prm)


def _gdn_scan_kernel(u_ref, w_ref, qd_ref, m2_ref, eg_ref, z_ref, ong_ref, o_ref, s_ref, *, nck):
    c = CHUNK
    nseq = u_ref.shape[0]

    @pl.when(pl.program_id(1) == 0)
    def _():
        s_ref[...] = jnp.zeros_like(s_ref)

    ong = ong_ref[...]
    zeros = jnp.zeros((c, HEAD_DIM), BF16)
    span = lambda h: slice(h * HEAD_DIM, (h + 1) * HEAD_DIM)
    chains = [(b, h) for b in range(nseq) for h in range(GDN_HEADS)]
    states = {bh: s_ref[bh[0], bh[1]] for bh in chains}
    for ci in range(nck):
        rows = slice(ci * c, (ci + 1) * c)
        rs = {(b, h): _dot(jnp.concatenate([w_ref[b, rows, span(h)], qd_ref[b, rows, span(h)]], axis=0),
                           states[b, h].astype(BF16)) for b, h in chains}
        v_new = {(b, h): (u_ref[b, rows, span(h)] - rs[b, h][:c]).astype(BF16) for b, h in chains}
        r2s = {}
        for b in range(nseq):
            for p in range(GDN_HEADS // 2):
                bd = jnp.concatenate([jnp.concatenate([v_new[b, 2 * p], zeros], axis=1),
                                      jnp.concatenate([zeros, v_new[b, 2 * p + 1]], axis=1)], axis=0)
                r2s[b, p] = _dot(m2_ref[b, ci * 3 * c:(ci + 1) * 3 * c, p * LANES:(p + 1) * LANES], bd)
        for b, h in chains:
            half = span(h % 2)
            r2 = r2s[b, h // 2]
            decay = eg_ref[b, ci * GDN_HEADS + h:ci * GDN_HEADS + h + 1, :]
            states[b, h] = states[b, h] * decay + r2[c:, half]
            o = rs[b, h][c:] + r2[:c, half]
            o = o * lax.rsqrt(jnp.mean(o * o, axis=-1, keepdims=True) + EPS) * ong
            o = o * _silu(z_ref[b, rows, span(h)].astype(F32))
            o_ref[b, rows, span(h)] = o.astype(o_ref.dtype)
    for b, h in chains:
        s_ref[b, h] = states[b, h]


def _gdn_scan(u, w, qd, m2, eg, proj, ong, batch, seq):
    t = batch * seq
    nck = min(SCAN_CHUNKS, seq // CHUNK)
    c = nck * CHUNK
    nch = seq // c
    ns = SCAN_SEQS if batch % SCAN_SEQS == 0 else 1
    groups = batch // ns
    split = lambda a: a.reshape(ns, a.shape[0] // ns, a.shape[1])
    blk = lambda b, n: (0, b * nch + n, 0)
    out = pl.pallas_call(
        functools.partial(_gdn_scan_kernel, nck=nck),
        grid=(groups, nch),
        in_specs=[
            pl.BlockSpec((ns, c, GDN_WIDTH), blk),
            pl.BlockSpec((ns, c, GDN_WIDTH), blk),
            pl.BlockSpec((ns, c, GDN_WIDTH), blk),
            pl.BlockSpec((ns, 3 * c, GDN_HEADS // 2 * LANES), blk),
            pl.BlockSpec((ns, nck * GDN_HEADS, LANES), blk),
            pl.BlockSpec((ns, c, GDN_WIDTH), lambda b, n: (0, b * nch + n, 3)),
            pl.BlockSpec((1, HEAD_DIM), lambda b, n: (0, 0)),
        ],
        out_specs=pl.BlockSpec((ns, c, GDN_WIDTH), blk),
        out_shape=jax.ShapeDtypeStruct((ns, t // ns, GDN_WIDTH), BF16),
        scratch_shapes=[pltpu.VMEM((ns, GDN_HEADS, HEAD_DIM, HEAD_DIM), F32)],
        compiler_params=_params("parallel", "arbitrary"),
        name="gdn_scan",
    )(split(u), split(w), split(qd), split(m2), split(eg), split(proj), ong)
    return out.reshape(t, GDN_WIDTH)


def _pack_bf16_pairs(y):
    n = y.shape[1] // 2
    yb = y.astype(BF16).astype(F32)
    lo = lax.bitcast_convert_type(yb[:, :n], U32)
    hi = lax.bitcast_convert_type(yb[:, n:], U32)
    return jnp.bitwise_or(jnp.right_shift(lo, 16), jnp.bitwise_and(hi, HI_MASK))


def _unpack_bf16_pairs(w):
    lo = lax.bitcast_convert_type(jnp.left_shift(w, 16), F32)
    hi = lax.bitcast_convert_type(jnp.bitwise_and(w, HI_MASK), F32)
    return lo, hi


def _mixer_kernel(glu_ref, ga_ref, gb_ref, o_ref, x_ref, woa_ref, wob_ref, wo_ref, dww_ref, vec_ref,
                  x1_ref, x1b_ref, x1p_ref, ubuf_ref, sh_ref, conv_ref, *, ts, rc):
    halo = CONF_HALO

    @pl.when(pl.program_id(1) == 0)
    def _():
        ubuf_ref[0:halo, :] = jnp.zeros((halo, CONF_CH), F32)

    @pl.when(pl.program_id(1) != 0)
    def _():
        ubuf_ref[0:halo, :] = ubuf_ref[ts:ts + halo, :]

    glu_a = glu_ref[:, :CONF_CH].astype(F32)
    glu_b = glu_ref[:, CONF_CH:].astype(F32)
    ubuf_ref[halo:halo + ts, :] = glu_a * _sigmoid(glu_b)

    dw_b = vec_ref[0:1, :]
    cln_g = vec_ref[1:2, :]
    cln_b = vec_ref[2:3, :]
    ln1_g = vec_ref[3:4, :]
    ln1_b = vec_ref[4:5, :]

    span = ts + halo - 8
    for s in range(1, 8):
        sh_ref[s - 1] = ubuf_ref[s:s + span, :]

    def tap_rows(j, r0):
        o = halo - (CONF_KERNEL - 1) + j
        q, s = o // 8, o % 8
        if s == 0:
            return ubuf_ref[r0 + o:r0 + o + rc, :]
        return sh_ref[s - 1, r0 + 8 * q:r0 + 8 * q + rc, :]

    hs = ts // 2
    for h0 in range(0, ts, hs):
        rows = slice(h0, h0 + hs)
        gated_a = _sigmoid(ga_ref[rows, :].astype(F32)) * _dot(o_ref[rows, :], woa_ref[...])
        for r0 in range(h0, h0 + hs, rc):
            tap_w = lambda j: jnp.concatenate([dww_ref[j]] * (rc // 8), axis=0)
            acc = tap_w(0) * tap_rows(0, r0)
            for j in range(1, CONF_KERNEL):
                acc = acc + tap_w(j) * tap_rows(j, r0)
            conv_ref[r0:r0 + rc, :] = acc
        uc = _silu(_layer_norm(conv_ref[rows, :] + dw_b, cln_g, cln_b))
        branch_b = _dot(uc.astype(BF16), wob_ref[...])
        hmix = gated_a + _sigmoid(gb_ref[rows, :].astype(F32)) * branch_b
        mix = _dot(hmix.astype(BF16), wo_ref[...])
        x1 = _layer_norm(DEEPNORM_ALPHA * x_ref[rows, :] + mix, ln1_g, ln1_b)
        x1_ref[rows, :] = x1
        x1b_ref[rows, :] = x1.astype(BF16)
        x1p_ref[rows, :] = _pack_bf16_pairs(x1)


def _mixer(proj, o_gdn, x, woa, wob, wo, dww, vec, batch, seq, ts):
    t = batch * seq
    nt = seq // ts
    rows = lambda b, n: b * nt + n
    full = lambda b, n: (0, 0)
    kern = functools.partial(_mixer_kernel, ts=ts, rc=32)
    return pl.pallas_call(
        kern,
        grid=(batch, nt),
        in_specs=[
            pl.BlockSpec((ts, 2 * CONF_CH), lambda b, n: (rows(b, n), 2)),
            pl.BlockSpec((ts, D_MODEL), lambda b, n: (rows(b, n), 6)),
            pl.BlockSpec((ts, D_MODEL), lambda b, n: (rows(b, n), 7)),
            pl.BlockSpec((ts, GDN_WIDTH), lambda b, n: (rows(b, n), 0)),
            pl.BlockSpec((ts, D_MODEL), lambda b, n: (rows(b, n), 0)),
            pl.BlockSpec((GDN_WIDTH, D_MODEL), full),
            pl.BlockSpec((CONF_CH, D_MODEL), full),
            pl.BlockSpec((D_MODEL, D_MODEL), full),
            pl.BlockSpec((CONF_KERNEL, 8, CONF_CH), lambda b, n: (0, 0, 0)),
            pl.BlockSpec((8, D_MODEL), full),
        ],
        out_specs=[pl.BlockSpec((ts, D_MODEL), lambda b, n: (rows(b, n), 0)),
                   pl.BlockSpec((ts, D_MODEL), lambda b, n: (rows(b, n), 0)),
                   pl.BlockSpec((ts, D_MODEL // 2), lambda b, n: (rows(b, n), 0))],
        out_shape=[jax.ShapeDtypeStruct((t, D_MODEL), F32),
                   jax.ShapeDtypeStruct((t, D_MODEL), BF16),
                   jax.ShapeDtypeStruct((t, D_MODEL // 2), U32)],
        scratch_shapes=[pltpu.VMEM((CONF_HALO + ts, CONF_CH), F32),
                        pltpu.VMEM((7, ts + CONF_HALO - 8, CONF_CH), F32),
                        pltpu.VMEM((ts, CONF_CH), F32)],
        compiler_params=_params("parallel", "arbitrary"),
        name="mixer",
    )(proj, proj, proj, o_gdn, x, woa, wob, wo, dww, vec)


def _route_kernel(x_ref, wrh_ref, wrl_ref, bias_ref, eidx_ref, wts_ref, rank_ref, cnt_ref, carry_ref, *, tt):
    @pl.when(pl.program_id(0) == 0)
    def _():
        carry_ref[...] = jnp.zeros_like(carry_ref)

    xh, xl = _split(x_ref[...])
    wrh = wrh_ref[...]
    logits = _dot_nt(wrh, xh) + _dot_nt(wrh, xl) + _dot_nt(wrl_ref[...], xh)
    s = _sigmoid(logits)
    biased = s + bias_ref[...]

    sub = lax.broadcasted_iota(I32, (GROUP_SIZE, tt), 0)
    groups = [biased[g * GROUP_SIZE:(g + 1) * GROUP_SIZE, :] for g in range(N_GROUPS)]
    gs = []
    for bg in groups:
        m1 = jnp.max(bg, axis=0, keepdims=True)
        first = jnp.min(jnp.where(bg == m1, sub, GROUP_SIZE), axis=0, keepdims=True)
        m2 = jnp.max(jnp.where(sub == first, -jnp.inf, bg), axis=0, keepdims=True)
        gs.append(m1 + m2)

    masked_parts = []
    for g in range(N_GROUPS):
        beaten = jnp.zeros((1, tt), I32)
        for o in range(N_GROUPS):
            if o == g:
                continue
            wins = (gs[o] >= gs[g]) if o < g else (gs[o] > gs[g])
            beaten = beaten + wins.astype(I32)
        keep = jnp.broadcast_to(beaten < TOPK_GROUPS, (GROUP_SIZE, tt))
        masked_parts.append(jnp.where(keep, groups[g], -jnp.inf))
    masked = jnp.concatenate(masked_parts, axis=0)

    eiota = lax.broadcasted_iota(I32, (N_EXPERTS, tt), 0)
    sel_all = jnp.zeros((N_EXPERTS, tt), F32)
    picks = []
    for _ in range(TOP_K):
        m = jnp.max(masked, axis=0, keepdims=True)
        idx = jnp.min(jnp.where(masked == m, eiota, N_EXPERTS), axis=0, keepdims=True)
        onehot = eiota == idx
        picks.append((idx, onehot))
        sel_all = jnp.where(onehot, 1.0, sel_all)
        masked = jnp.where(onehot, -jnp.inf, masked)

    tr = lax.broadcasted_iota(I32, (tt, tt), 0)
    tc = lax.broadcasted_iota(I32, (tt, tt), 1)
    before = (tr < tc).astype(BF16)
    sel_b = sel_all.astype(BF16)
    carry = carry_ref[...]
    rank_all = _dot(sel_b, before) + carry[:, 0:1]
    carry_new = carry + _dot(sel_b, jnp.ones((tt, LANES), BF16))
    carry_ref[...] = carry_new
    cnt_ref[...] = carry_new

    s_sel = [jnp.sum(jnp.where(oh, s, 0.0), axis=0, keepdims=True) for _, oh in picks]
    total = s_sel[0]
    for v in s_sel[1:]:
        total = total + v
    for k, (idx, oh) in enumerate(picks):
        eidx_ref[k:k + 1, :] = idx
        wts_ref[k:k + 1, :] = s_sel[k] / total * ROUTED_SCALE
        rank_ref[k:k + 1, :] = jnp.sum(jnp.where(oh, rank_all, 0.0), axis=0, keepdims=True).astype(I32)


def _route(x1, wr_hi, wr_lo, bias, tt):
    t = x1.shape[0]
    kern = functools.partial(_route_kernel, tt=tt)
    return pl.pallas_call(
        kern,
        grid=(t // tt,),
        in_specs=[pl.BlockSpec((tt, D_MODEL), lambda i: (i, 0)),
                  pl.BlockSpec((N_EXPERTS, D_MODEL), lambda i: (0, 0)),
                  pl.BlockSpec((N_EXPERTS, D_MODEL), lambda i: (0, 0)),
                  pl.BlockSpec((N_EXPERTS, tt), lambda i: (0, 0))],
        out_specs=[pl.BlockSpec((TOP_K, tt), lambda i: (0, i)),
                   pl.BlockSpec((TOP_K, tt), lambda i: (0, i)),
                   pl.BlockSpec((TOP_K, tt), lambda i: (0, i)),
                   pl.BlockSpec((N_EXPERTS, LANES), lambda i: (0, 0))],
        out_shape=[jax.ShapeDtypeStruct((TOP_K, t), I32),
                   jax.ShapeDtypeStruct((TOP_K, t), F32),
                   jax.ShapeDtypeStruct((TOP_K, t), I32),
                   jax.ShapeDtypeStruct((N_EXPERTS, LANES), F32)],
        scratch_shapes=[pltpu.VMEM((N_EXPERTS, LANES), F32)],
        compiler_params=_params("arbitrary"),
        name="route",
    )(x1, wr_hi, wr_lo, bias)


def _plan_kernel(cnt_ref, pstart_ref, plan_ref):
    e, nb = N_EXPERTS, plan_ref.shape[1]
    counts = cnt_ref[...]
    nblk = jnp.floor((counts + (ROW_BLOCK - 1)) * (1.0 / ROW_BLOCK))
    hi = jnp.floor(nblk * (1.0 / 256.0))
    lo = nblk - 256.0 * hi
    r = lax.broadcasted_iota(I32, (e, e), 0)
    c = lax.broadcasted_iota(I32, (e, e), 1)
    ltri = (r >= c).astype(BF16)
    bend = 256.0 * _dot(ltri, hi.astype(BF16)) + _dot(ltri, lo.astype(BF16))
    pend = bend * ROW_BLOCK
    pstart = pend - nblk * ROW_BLOCK
    pstart_ref[...] = pstart.astype(I32)

    bs = (lax.broadcasted_iota(I32, (e, nb), 1) * ROW_BLOCK).astype(F32)
    pend_b = jnp.broadcast_to(pend[:, 0:1], (e, nb))
    pstart_b = jnp.broadcast_to(pstart[:, 0:1], (e, nb))
    used_b = jnp.broadcast_to((pstart + counts)[:, 0:1], (e, nb))
    owner = jnp.sum(jnp.where(pend_b <= bs, 1.0, 0.0), axis=0, keepdims=True)
    inside = jnp.where(pstart_b <= bs, jnp.where(bs < pend_b, 1.0, 0.0), 0.0)
    real = jnp.sum(inside * jnp.clip(used_b - bs, 0.0, float(ROW_BLOCK)), axis=0, keepdims=True)
    plan_ref[0:1, :] = jnp.minimum(owner, float(e - 1)).astype(I32)
    plan_ref[1:2, :] = real.astype(I32)
    plan_ref[2:8, :] = jnp.zeros((6, nb), I32)


def _plan(cnt, n_blocks):
    nb = -(-n_blocks // LANES) * LANES
    pstart, plan = pl.pallas_call(
        _plan_kernel,
        out_shape=[jax.ShapeDtypeStruct((N_EXPERTS, LANES), I32),
                   jax.ShapeDtypeStruct((8, nb), I32)],
        name="plan",
    )(cnt)
    return pstart[:, 0], plan[0, :n_blocks], plan[1, :n_blocks]


def _dest_kernel(eidx_ref, rank_ref, pstart_ref, dest_ref):
    eidx = eidx_ref[...]
    acc = rank_ref[...]
    for e in range(N_EXPERTS):
        acc = acc + jnp.where(eidx == e, pstart_ref[e], 0)
    dest_ref[...] = acc


def _dest(eidx_t, rank_t, pstart, tt):
    t = eidx_t.shape[1]
    return pl.pallas_call(
        _dest_kernel,
        grid=(t // tt,),
        in_specs=[pl.BlockSpec((TOP_K, tt), lambda i: (0, i)),
                  pl.BlockSpec((TOP_K, tt), lambda i: (0, i)),
                  pl.BlockSpec(memory_space=pltpu.SMEM)],
        out_specs=pl.BlockSpec((TOP_K, tt), lambda i: (0, i)),
        out_shape=jax.ShapeDtypeStruct((TOP_K, t), I32),
        compiler_params=_params("parallel"),
        name="dest",
    )(eidx_t, rank_t, pstart)


def _dispatch(dest_kt, x1p, n_rows):
    t, width = x1p.shape
    info = plsc.get_sparse_core_info()
    nc, nw = info.num_cores, info.num_cores * info.num_subcores
    chunk = SC_SCATTER_ROWS
    per_w = t // nw
    n_chunks = per_w // chunk
    assert per_w % chunk == 0
    idx = dest_kt.reshape(TOP_K, nw, n_chunks, chunk).transpose(1, 2, 0, 3).reshape(nw, n_chunks * TOP_K, chunk)
    mesh = plsc.VectorSubcoreMesh(core_axis_name="c", subcore_axis_name="s")

    @functools.partial(
        pl.kernel, mesh=mesh, name="dispatch",
        out_type=jax.ShapeDtypeStruct((n_rows, width), x1p.dtype),
        scratch_types=[pltpu.VMEM((n_chunks * TOP_K, chunk), I32),
                       pltpu.VMEM((chunk, width), x1p.dtype),
                       pltpu.SemaphoreType.DMA])
    def scatter(x_hbm, idx_hbm, xs_hbm, idx_v, rows_v, sem):
        wid = lax.axis_index("s") * nc + lax.axis_index("c")
        base = wid * per_w
        pltpu.sync_copy(idx_hbm.at[wid], idx_v)
        for j in range(n_chunks):
            pltpu.sync_copy(x_hbm.at[pl.ds(base + j * chunk, chunk)], rows_v)
            copies = [pltpu.make_async_copy(rows_v, xs_hbm.at[idx_v.at[j * TOP_K + k]], sem)
                      for k in range(TOP_K)]
            for cp in copies:
                cp.start()
            for cp in copies:
                cp.wait()

    return scatter(x1p, idx)


def _gather_rows(table, idx):
    n = idx.shape[0]
    width = table.shape[1]
    info = plsc.get_sparse_core_info()
    nc, nw = info.num_cores, info.num_cores * info.num_subcores
    chunk = SC_GATHER_ROWS
    per_w = n // nw
    n_chunks = per_w // chunk
    assert per_w % (2 * chunk) == 0
    mesh = plsc.VectorSubcoreMesh(core_axis_name="c", subcore_axis_name="s")

    @functools.partial(
        pl.kernel, mesh=mesh, name="gather_rows",
        out_type=jax.ShapeDtypeStruct((n, width), table.dtype),
        scratch_types=[pltpu.VMEM((n_chunks, chunk), I32),
                       pltpu.VMEM((2, chunk, width), table.dtype),
                       pltpu.SemaphoreType.DMA((2,)),
                       pltpu.SemaphoreType.DMA((2,))])
    def gather(table_hbm, idx_hbm, out_hbm, idx_v, rows_v, gsem, osem):
        wid = lax.axis_index("s") * nc + lax.axis_index("c")
        base = wid * per_w
        pltpu.sync_copy(idx_hbm.at[wid], idx_v)

        def fetch(j, b):
            return pltpu.make_async_copy(table_hbm.at[idx_v.at[j]], rows_v.at[b], gsem.at[b])

        def put(j, b):
            return pltpu.make_async_copy(rows_v.at[b], out_hbm.at[pl.ds(base + j * chunk, chunk)], osem.at[b])

        fetch(0, 0).start()

        @pl.loop(0, n_chunks, step=2)
        def _(j0):
            for b in range(2):
                j = j0 + b
                fetch(j, b).wait()

                @pl.when(j + 1 < n_chunks)
                def _():
                    @pl.when(j >= 1)
                    def _():
                        put(j - 1, 1 - b).wait()

                    fetch(j + 1, 1 - b).start()

                put(j, b).start()

        put(n_chunks - 2, 0).wait()
        put(n_chunks - 1, 1).wait()

    return gather(table, idx.reshape(nw, n_chunks, chunk))


def _xs_copy(xs_hbm, xbuf, isem, j, slot):
    return pltpu.make_async_copy(xs_hbm.at[pl.ds(j * ROW_BLOCK, ROW_BLOCK)], xbuf.at[slot], isem.at[slot])


def _ys_copy(ybuf, ys_hbm, osem, j, slot):
    return pltpu.make_async_copy(ybuf.at[slot], ys_hbm.at[pl.ds(j * ROW_BLOCK, ROW_BLOCK)], osem.at[slot])


def _experts_kernel(be_ref, nv_ref, xs_hbm, wg_ref, wu_ref, wd_ref, ys_hbm,
                    xbuf, ybuf, wgu_s, wd_s, cur_ref, isem, osem):
    i = pl.program_id(0)
    n_valid = nv_ref[i]
    half = EXPERT_SUB
    slot = lax.rem(i, EXPERT_IN_SLOTS)
    oslot = lax.rem(i, 2)

    @pl.when(i == 0)
    def _():
        cur_ref[0] = -1
        for j in range(2):
            @pl.when(nv_ref[j] > 0)
            def _():
                _xs_copy(xs_hbm, xbuf, isem, j, j).start()

    @pl.when(nv_ref[i + 2] > 0)
    def _():
        _xs_copy(xs_hbm, xbuf, isem, i + 2, lax.rem(i + 2, EXPERT_IN_SLOTS)).start()

    @pl.when((i >= 2) & (nv_ref[jnp.maximum(i - 2, 0)] > 0))
    def _():
        _ys_copy(ybuf, ys_hbm, osem, i - 2, oslot).wait()

    @pl.when((n_valid > 0) & (cur_ref[0] != be_ref[i]))
    def _():
        wgu_s[:, :EXPERT_FF] = wg_ref[...].astype(BF16)
        wgu_s[:, EXPERT_FF:] = wu_ref[...].astype(BF16)
        wd_s[...] = wd_ref[...].astype(BF16)
        cur_ref[0] = be_ref[i]

    def rows_bf16(r0):
        valid = lax.broadcasted_iota(I32, (half, xbuf.shape[2]), 0) + r0 < n_valid
        lo, hi = _unpack_bf16_pairs(jnp.where(valid, xbuf[slot, r0:r0 + half, :], jnp.zeros((), U32)))
        return jnp.concatenate([lo.astype(BF16), hi.astype(BF16)], axis=1)

    def hidden(gu):
        return (_silu(gu[:, :EXPERT_FF]) * gu[:, EXPERT_FF:]).astype(BF16)

    @pl.when(n_valid > 0)
    def _():
        _xs_copy(xs_hbm, xbuf, isem, i, slot).wait()

    @pl.when(n_valid > half)
    def _():
        xa, xb = rows_bf16(0), rows_bf16(half)
        gua = _dot(xa, wgu_s[...])
        gub = _dot(xb, wgu_s[...])
        ya = _dot(hidden(gua), wd_s[...])
        yb = _dot(hidden(gub), wd_s[...])
        ybuf[oslot, 0:half, :] = _pack_bf16_pairs(ya)
        ybuf[oslot, half:, :] = _pack_bf16_pairs(yb)

    @pl.when((n_valid > 0) & (n_valid <= half))
    def _():
        ya = _dot(hidden(_dot(rows_bf16(0), wgu_s[...])), wd_s[...])
        ybuf[oslot, 0:half, :] = _pack_bf16_pairs(ya)
        ybuf[oslot, half:, :] = jnp.zeros((half, ybuf.shape[2]), ybuf.dtype)

    @pl.when(n_valid > 0)
    def _():
        _ys_copy(ybuf, ys_hbm, osem, i, oslot).start()


def _experts(block_e, n_valid, xs, wg, wu, wd, layer):
    n_rows, width = xs.shape
    steps = n_rows // ROW_BLOCK + 2
    be = jnp.concatenate([block_e, jnp.full((2,), N_EXPERTS - 1, I32)])
    nv = jnp.concatenate([n_valid, jnp.zeros((4,), I32)])
    grid_spec = pltpu.PrefetchScalarGridSpec(
        num_scalar_prefetch=2,
        grid=(steps,),
        in_specs=[pl.BlockSpec(memory_space=pl.ANY),
                  pl.BlockSpec((None, None, D_MODEL, EXPERT_FF), lambda i, be, nv: (layer, be[i], 0, 0)),
                  pl.BlockSpec((None, None, D_MODEL, EXPERT_FF), lambda i, be, nv: (layer, be[i], 0, 0)),
                  pl.BlockSpec((None, None, EXPERT_FF, D_MODEL), lambda i, be, nv: (layer, be[i], 0, 0))],
        out_specs=pl.BlockSpec(memory_space=pl.ANY),
        scratch_shapes=[pltpu.VMEM((EXPERT_IN_SLOTS, ROW_BLOCK, width), xs.dtype),
                        pltpu.VMEM((2, ROW_BLOCK, width), xs.dtype),
                        pltpu.VMEM((D_MODEL, 2 * EXPERT_FF), BF16),
                        pltpu.VMEM((EXPERT_FF, D_MODEL), BF16),
                        pltpu.SMEM((1,), I32),
                        pltpu.SemaphoreType.DMA((EXPERT_IN_SLOTS,)),
                        pltpu.SemaphoreType.DMA((2,))],
    )
    return pl.pallas_call(
        _experts_kernel,
        grid_spec=grid_spec,
        out_shape=jax.ShapeDtypeStruct((n_rows, width), xs.dtype),
        compiler_params=_params("arbitrary"),
        name="experts",
    )(be, nv, xs, wg, wu, wd)


def _shared_kernel(x1b_ref, wsg_ref, wsu_ref, wsd_ref, anchor_ref, sh_ref):
    del anchor_ref
    xb = x1b_ref[...]
    hid = (_silu(_dot(xb, wsg_ref[...])) * _dot(xb, wsu_ref[...])).astype(BF16)
    sh_ref[...] = _dot(hid, wsd_ref[...]).astype(sh_ref.dtype)


def _shared(x1b, wsg, wsu, wsd, anchor, tt, row0, nrows):
    full = lambda i: (0, 0)
    first = row0 // tt
    return pl.pallas_call(
        _shared_kernel,
        grid=(nrows // tt,),
        in_specs=[pl.BlockSpec((tt, D_MODEL), lambda i: (first + i, 0)),
                  pl.BlockSpec((D_MODEL, SHARED_FF), full),
                  pl.BlockSpec((D_MODEL, SHARED_FF), full),
                  pl.BlockSpec((SHARED_FF, D_MODEL), full),
                  pl.BlockSpec(memory_space=pl.ANY)],
        out_specs=pl.BlockSpec((tt, D_MODEL), lambda i: (i, 0)),
        out_shape=jax.ShapeDtypeStruct((nrows, D_MODEL), BF16),
        compiler_params=_params("parallel"),
        name="shared",
    )(x1b, wsg, wsu, wsd, anchor)


def _combine_kernel(yg_ref, w_ref, x1_ref, sha_ref, shb_ref, vec_ref, x2_ref, *, half_steps):
    in_first = pl.program_id(0) < half_steps
    shared = jnp.where(in_first, sha_ref[...], shb_ref[...]).astype(F32)
    w = w_ref[...]
    half = D_MODEL // 2
    acc_lo, acc_hi = shared[:, :half], shared[:, half:]
    for k in range(TOP_K):
        lo, hi = _unpack_bf16_pairs(yg_ref[k])
        acc_lo = acc_lo + w[:, k:k + 1] * lo
        acc_hi = acc_hi + w[:, k:k + 1] * hi
    acc = jnp.concatenate([acc_lo, acc_hi], axis=1)
    x2 = _layer_norm(DEEPNORM_ALPHA * x1_ref[...] + acc, vec_ref[0:1, :], vec_ref[1:2, :])
    x2_ref[...] = x2


def _combine(yg, w_tok, x1, shared_a, shared_b, vec, tt):
    t = x1.shape[0]
    half_steps = shared_a.shape[0] // tt
    full = lambda i: (0, 0)
    return pl.pallas_call(
        functools.partial(_combine_kernel, half_steps=half_steps),
        grid=(t // tt,),
        in_specs=[pl.BlockSpec((TOP_K, tt, yg.shape[2]), lambda i: (0, i, 0)),
                  pl.BlockSpec((tt, TOP_K), lambda i: (i, 0)),
                  pl.BlockSpec((tt, D_MODEL), lambda i: (i, 0)),
                  pl.BlockSpec((tt, D_MODEL), lambda i: (jnp.minimum(i, half_steps - 1), 0)),
                  pl.BlockSpec((tt, D_MODEL), lambda i: (jnp.maximum(i - half_steps, 0), 0)),
                  pl.BlockSpec((8, D_MODEL), full)],
        out_specs=pl.BlockSpec((tt, D_MODEL), lambda i: (i, 0)),
        out_shape=jax.ShapeDtypeStruct((t, D_MODEL), F32),
        compiler_params=_params("parallel"),
        name="combine",
    )(yg, w_tok, x1, shared_a, shared_b, vec)


def _pad_rows(a, rows):
    return jnp.zeros((rows, a.shape[-1]), F32).at[:a.shape[0]].set(a.astype(F32))


def _layer(x, p, stacked, layer, batch, seq):
    t = batch * seq
    w_main, wa_hi, wa_lo, wb_hi, wb_lo = _regroup(jnp.swapaxes(stacked["w_in"], 1, 2), layer, 512)

    tm = min(1024, t)
    proj = _proj(x, w_main, tm, 2048)

    prm = jnp.zeros((8, LANES), F32)
    prm = prm.at[0, :GDN_HEADS].set(p["a_log"]).at[1, :GDN_HEADS].set(p["dt_bias"])
    ong = p["o_norm_g"].reshape(1, HEAD_DIM).astype(F32)
    u, w, qd, m2, eg = _gdn_intra(proj, x, wa_hi, wa_lo, wb_hi, wb_lo, p["conv_qkv"].astype(F32), prm,
                                  seq, min(GDN_TILE, seq))
    o_gdn = _gdn_scan(u, w, qd, m2, eg, proj, ong, batch, seq)

    ts = min(256, seq)
    dww = jnp.broadcast_to(p["dw_w"].astype(F32)[:, None, :], (CONF_KERNEL, 8, CONF_CH))
    vec = _pad_rows(jnp.stack([p["dw_b"], p["cln_g"], p["cln_b"], p["ln1_g"], p["ln1_b"]]), 8)
    x1, x1b, x1p = _mixer(proj, o_gdn, x, p["w_oa"].astype(BF16), p["w_ob"].astype(BF16),
                          p["w_o"].astype(BF16), dww, vec, batch, seq, ts)

    tt_r = min(512, t)
    wr_t = p["w_router"].T.astype(F32)
    wr_hi = wr_t.astype(BF16)
    wr_lo = (wr_t - wr_hi.astype(F32)).astype(BF16)
    bias = jnp.broadcast_to(p["router_bias"].astype(F32)[:, None], (N_EXPERTS, tt_r))
    eidx_t, wts_t, rank_t, cnt = _route(x1, wr_hi, wr_lo, bias, tt_r)

    n_blocks = -(-(t * TOP_K + N_EXPERTS * (ROW_BLOCK - 1)) // ROW_BLOCK)
    n_rows = n_blocks * ROW_BLOCK
    pstart, block_e, n_valid = _plan(cnt, n_blocks)

    dest = _dest(eidx_t, rank_t, pstart, min(2048, t))
    sh_w = (p["w_sh_gate"].astype(BF16), p["w_sh_up"].astype(BF16), p["w_sh_down"].astype(BF16))
    tt_s = min(512, t // 2)
    xs = _dispatch(dest, x1p, n_rows)
    shared_a = _shared(x1b, *sh_w, dest, tt_s, 0, t // 2)
    ys = _experts(block_e, n_valid, xs, stacked["w_gate_e"], stacked["w_up_e"], stacked["w_down_e"], layer)
    yg = _gather_rows(ys, dest.reshape(TOP_K * t)).reshape(TOP_K, t, ys.shape[1])
    shared_b = _shared(x1b, *sh_w, ys, tt_s, t // 2, t // 2)
    vec2 = _pad_rows(jnp.stack([p["ln2_g"], p["ln2_b"]]), 8)
    return _combine(yg, wts_t.T, x1, shared_a, shared_b, vec2, min(256, t // 2))


_PARAM_NAMES = ("w_in", "conv_qkv", "a_log", "dt_bias", "o_norm_g", "w_oa", "dw_w", "dw_b", "cln_g",
                "cln_b", "w_ob", "w_o", "ln1_g", "ln1_b", "w_router", "router_bias", "w_gate_e",
                "w_up_e", "w_down_e", "w_sh_gate", "w_sh_up", "w_sh_down", "ln2_g", "ln2_b")


_EXPERT_WEIGHTS = ("w_in", "w_gate_e", "w_up_e", "w_down_e")


def kernel(x, w_in, conv_qkv, a_log, dt_bias, o_norm_g, w_oa, dw_w, dw_b, cln_g, cln_b, w_ob, w_o,
           ln1_g, ln1_b, w_router, router_bias, w_gate_e, w_up_e, w_down_e, w_sh_gate, w_sh_up,
           w_sh_down, ln2_g, ln2_b):
    stacked = dict(zip(_PARAM_NAMES, (w_in, conv_qkv, a_log, dt_bias, o_norm_g, w_oa, dw_w, dw_b, cln_g,
                                      cln_b, w_ob, w_o, ln1_g, ln1_b, w_router, router_bias, w_gate_e,
                                      w_up_e, w_down_e, w_sh_gate, w_sh_up, w_sh_down, ln2_g, ln2_b)))
    batch, seq, d = x.shape
    assert d == D_MODEL and seq % CHUNK == 0
    xf = x.reshape(batch * seq, d).astype(F32)
    for layer in range(w_in.shape[0]):
        p = {name: arr[layer] for name, arr in stacked.items() if name not in _EXPERT_WEIGHTS}
        xf = _layer(xf, p, stacked, layer, batch, seq)
    return xf.reshape(batch, seq, d).astype(x.dtype)
```

```python
import functools

import jax
import jax.numpy as jnp
import numpy as np
from jax import lax
from jax.experimental import pallas as pl
from jax.experimental.pallas import tpu as pltpu
from jax.experimental.pallas import tpu_sc as plsc

F32 = jnp.float32
BF16 = jnp.bfloat16
I32 = jnp.int32
U32 = jnp.uint32
HI_MASK = np.uint32(0xFFFF0000)

D_MODEL = 1024
GDN_HEADS = 8
HEAD_DIM = 128
GDN_WIDTH = GDN_HEADS * HEAD_DIM
SHORT_CONV = 4
CONV_HALO = 16
CHUNK = 64
SOLVE_BLOCK = 16
GDN_TILE = 256
SCAN_CHUNKS = 4
SCAN_SEQS = 2
CONF_CH = D_MODEL
CONF_KERNEL = 31
CONF_HALO = 32
N_EXPERTS = 64
TOP_K = 8
N_GROUPS = 8
GROUP_SIZE = N_EXPERTS // N_GROUPS
TOPK_GROUPS = 4
EXPERT_FF = 256
SHARED_FF = 256
ROUTED_SCALE = 2.5
DEPTH = 2
DEEPNORM_ALPHA = (2 * DEPTH) ** 0.25
EPS = 1e-6

LANES = 128
PROJ_COLS = 8 * D_MODEL
ROW_BLOCK = 1024
EXPERT_SUB = 256
EXPERT_IN_SLOTS = 3
SC_SCATTER_ROWS = 128
SC_GATHER_ROWS = 64
VMEM_LIMIT = 56 * 1024 * 1024


def _params(*sem):
    return pltpu.CompilerParams(dimension_semantics=sem, vmem_limit_bytes=VMEM_LIMIT)


def _dot(a, b):
    return jnp.dot(a, b, preferred_element_type=F32)


def _dot_nt(a, b):
    return lax.dot_general(a, b, (((1,), (1,)), ((), ())), preferred_element_type=F32)


def _split(a):
    hi = a.astype(BF16)
    lo = (a - hi.astype(F32)).astype(BF16)
    return hi, lo


def _dot3(a, b):
    ah, al = _split(a)
    bh, bl = _split(b)
    return _dot(ah, bh) + _dot(al, bh) + _dot(ah, bl)


def _sigmoid(x):
    return 1.0 / (1.0 + jnp.exp(-x))


def _silu(x):
    return x * _sigmoid(x)


def _layer_norm(y, g, b):
    mu = jnp.mean(y, axis=-1, keepdims=True)
    yc = y - mu
    var = jnp.mean(yc * yc, axis=-1, keepdims=True)
    return yc * lax.rsqrt(var + EPS) * g + b


def _proj_kernel(a_ref, w_ref, o_ref):
    o_ref[...] = _dot_nt(a_ref[...].astype(BF16), w_ref[...]).astype(o_ref.dtype)


def _regroup_kernel(wt_hbm, main_ref, ahi_ref, alo_ref, bhi_ref, blo_ref, buf, ab_buf, sem, *, layer, tr):
    i = pl.program_id(0)
    half = PROJ_COLS // 2
    skip = 2 * GDN_HEADS
    start = pl.multiple_of(jnp.where(i < half // tr, 0, skip) + i * tr, skip)
    cp = pltpu.make_async_copy(wt_hbm.at[layer, pl.ds(start, tr), :], buf, sem)
    cp.start()
    cp.wait()
    main_ref[...] = buf[...].astype(BF16)

    @pl.when(i == 0)
    def _():
        cp_ab = pltpu.make_async_copy(wt_hbm.at[layer, pl.ds(half, LANES), :], ab_buf, sem)
        cp_ab.start()
        cp_ab.wait()
        cols = ab_buf[...].T
        wide = jnp.concatenate([cols, jnp.zeros_like(cols)], axis=1)
        lane = lax.broadcasted_iota(I32, cols.shape, 1)
        for off, hi_ref, lo_ref in ((0, ahi_ref, alo_ref), (GDN_HEADS, bhi_ref, blo_ref)):
            hi, lo = _split(jnp.where(lane < GDN_HEADS, wide[:, off:off + LANES], 0.0))
            hi_ref[...] = hi
            lo_ref[...] = lo


def _regroup(w_in_t, layer, tr):
    d = w_in_t.shape[2]
    small = jax.ShapeDtypeStruct((d, LANES), BF16)
    whole = lambda i: (0, 0)
    return pl.pallas_call(
        functools.partial(_regroup_kernel, layer=layer, tr=tr),
        grid=(PROJ_COLS // tr,),
        in_specs=[pl.BlockSpec(memory_space=pl.ANY)],
        out_specs=[pl.BlockSpec((tr, d), lambda i: (i, 0))] + [pl.BlockSpec((d, LANES), whole)] * 4,
        out_shape=[jax.ShapeDtypeStruct((PROJ_COLS, d), BF16), small, small, small, small],
        scratch_shapes=[pltpu.VMEM((tr, d), F32), pltpu.VMEM((LANES, d), F32), pltpu.SemaphoreType.DMA],
        compiler_params=_params("arbitrary"),
        name="regroup",
    )(w_in_t)


def _proj(xb, w, tm, tn):
    m, k = xb.shape
    n = w.shape[0]
    return pl.pallas_call(
        _proj_kernel,
        grid=(m // tm, n // tn),
        in_specs=[pl.BlockSpec((tm, k), lambda i, j: (i, 0)),
                  pl.BlockSpec((tn, k), lambda i, j: (j, 0))],
        out_specs=pl.BlockSpec((tm, tn), lambda i, j: (i, j)),
        out_shape=jax.ShapeDtypeStruct((m, n), BF16),
        compiler_params=_params("parallel", "parallel"),
        name="proj",
    )(xb, w)


def _unit_lower_inverse4(als):
    c = CHUNK
    n = als[0].shape[1]
    row = lax.broadcasted_iota(I32, (c, n), 0)
    col = jnp.bitwise_and(lax.broadcasted_iota(I32, (c, n), 1), c - 1)
    shift = SOLVE_BLOCK.bit_length() - 1
    same = jnp.right_shift(row, shift) == jnp.right_shift(col, shift)
    eye = (row == col).astype(F32)
    cshift = c.bit_length() - 1
    brow = jnp.right_shift(lax.broadcasted_iota(I32, (n, n), 0), cshift)
    bcol = jnp.right_shift(lax.broadcasted_iota(I32, (n, n), 1), cshift)
    on_diag = brow == bcol

    def mm(x, y):
        yb = y.astype(BF16)
        bd = jnp.where(on_diag, jnp.concatenate([yb] * (n // c), axis=0), jnp.zeros((), BF16))
        return _dot(x.astype(BF16), bd)

    a_diag = [jnp.where(same, al, 0.0) for al in als]
    a_off = [al - ad for al, ad in zip(als, a_diag)]
    bp = [-ad for ad in a_diag]
    p = [eye + b for b in bp]
    for _ in range(3):
        bp = [mm(b, b) for b in bp]
        p = [x + mm(x, b) for x, b in zip(p, bp)]
    n1 = [mm(x, ao) for x, ao in zip(p, a_off)]
    n2 = [mm(x, x) for x in n1]
    q = [x + mm(y, x) for x, y in zip(p, n2)]
    return [x - mm(y, x) for x, y in zip(q, n1)]


def _shift_selectors(rt):
    sel = np.zeros((SHORT_CONV * rt, rt), np.float32)
    sel_halo = np.zeros((SHORT_CONV * 8, CONV_HALO), np.float32)
    for d in range(SHORT_CONV):
        for t in range(d, rt):
            sel[d * rt + t, t - d] = 1.0
        for t in range(d):
            sel_halo[d * 8 + t, CONV_HALO + t - d] = 1.0
    return jnp.asarray(sel, BF16), jnp.asarray(sel_halo, BF16)


def _gdn_intra_kernel(qkv_ref, prev_ref, x_ref, sel_ref, selh_ref, wah_ref, wal_ref, wbh_ref, wbl_ref, cw_ref,
                      prm_ref, u_ref, w_ref, qd_ref, m2_ref, eg_ref, xs_ref, *, rt, tiles_per_seq):
    c = CHUNK
    nc = rt // c
    first = (pl.program_id(0) % tiles_per_seq) == 0
    edge = jnp.where(first, 0.0, _dot(selh_ref[...], prev_ref[...]))

    def move_rows(lo, hi):
        moved = _dot(sel_ref[...], qkv_ref[:, lo:hi])
        for d in range(SHORT_CONV):
            xs_ref[d, 0:8, lo:hi] = moved[d * rt:d * rt + 8] + edge[d * 8:(d + 1) * 8, lo:hi]
            xs_ref[d, 8:rt, lo:hi] = moved[d * rt + 8:(d + 1) * rt]

    for part in range(3):
        move_rows(part * GDN_WIDTH, part * GDN_WIDTH + 2 * HEAD_DIM)

    xh, xl = _split(x_ref[...])

    def proj3(wh_ref, wl_ref):
        wh = wh_ref[...]
        return _dot(xh, wh) + _dot(xl, wh) + _dot(xh, wl_ref[...])

    a_raw = proj3(wah_ref, wal_ref)
    b_raw = proj3(wbh_ref, wbl_ref)
    sp_in = a_raw + prm_ref[1:2, :]
    softplus = jnp.maximum(sp_in, 0.0) + jnp.log(1.0 + jnp.exp(-jnp.abs(sp_in)))
    g = -jnp.exp(prm_ref[0:1, :]) * softplus
    beta = _sigmoid(b_raw)

    cshift = c.bit_length() - 1
    r2 = lax.broadcasted_iota(I32, (rt, rt), 0)
    c2 = lax.broadcasted_iota(I32, (rt, rt), 1)
    same_chunk = jnp.right_shift(r2, cshift) == jnp.right_shift(c2, cshift)
    ltri = jnp.where(r2 >= c2, jnp.where(same_chunk, 1.0, 0.0), 0.0).astype(BF16)
    g_hi = g.astype(BF16)
    g_r = g - g_hi.astype(F32)
    g_mid = g_r.astype(BF16)
    g_lo = (g_r - g_mid.astype(F32)).astype(BF16)
    gc = _dot(ltri, g_hi) + _dot(ltri, g_mid) + _dot(ltri, g_lo)
    gct = gc.T
    egc = jnp.exp(gc)
    gend = jnp.concatenate(
        [jnp.broadcast_to(gc[ci * c + c - 1:ci * c + c, :], (c, LANES)) for ci in range(nc)], axis=0)
    kfac = jnp.exp(gend - gc)
    bege = beta * egc
    for ci in range(nc):
        last = ci * c + c - 1
        eg_ref[ci * GDN_HEADS:(ci + 1) * GDN_HEADS, :] = jnp.broadcast_to(
            jnp.exp(gct[0:GDN_HEADS, last:last + 1]), (GDN_HEADS, LANES))

    lane_t = lax.broadcasted_iota(I32, (rt, LANES), 1) < c
    lane_lo = lax.broadcasted_iota(I32, (c, LANES), 1) < c
    lcol = jnp.bitwise_and(lax.broadcasted_iota(I32, (c, LANES), 1), c - 1)
    rowi = lax.broadcasted_iota(I32, (c, LANES), 0)
    causal = rowi >= lcol
    strict = rowi > lcol

    def conv(base, h):
        lo, hi = base + h * HEAD_DIM, base + (h + 1) * HEAD_DIM
        acc = cw_ref[SHORT_CONV - 1:SHORT_CONV, lo:hi] * xs_ref[0, :, lo:hi]
        for j in range(SHORT_CONV - 1):
            acc = acc + cw_ref[j:j + 1, lo:hi] * xs_ref[SHORT_CONV - 1 - j, :, lo:hi]
        return _silu(acc)

    a_pairs = [[None] * (GDN_HEADS // 2) for _ in range(nc)]
    rhs_pairs = [[None] * (GDN_HEADS // 2) for _ in range(nc)]
    for p in range(GDN_HEADS // 2):
        if p + 1 < GDN_HEADS // 2:
            for part in range(3):
                lo = part * GDN_WIDTH + (p + 1) * 2 * HEAD_DIM
                move_rows(lo, lo + 2 * HEAD_DIM)
        ks, kbs, qs, kds, rhss = [], [], [], [], []
        for h in (2 * p, 2 * p + 1):
            q = conv(0, h)
            k = conv(GDN_WIDTH, h)
            v = conv(2 * GDN_WIDTH, h)
            q = q * lax.rsqrt(jnp.sum(q * q, axis=-1, keepdims=True) + EPS) * (HEAD_DIM ** -0.5)
            k = k * lax.rsqrt(jnp.sum(k * k, axis=-1, keepdims=True) + EPS)
            beta_h = beta[:, h:h + 1]
            qd_ref[:, h * HEAD_DIM:(h + 1) * HEAD_DIM] = (q * egc[:, h:h + 1]).astype(BF16)
            ks.append(k)
            kbs.append(k * beta_h)
            qs.append(q)
            kds.append(k * kfac[:, h:h + 1])
            rhss.append(jnp.concatenate([v * beta_h, k * bege[:, h:h + 1]], axis=1))
        h0, h1 = 2 * p, 2 * p + 1
        gch = jnp.where(lane_t, gc[:, h0:h0 + 1], gc[:, h1:h1 + 1])
        for ci in range(nc):
            rows = slice(ci * c, (ci + 1) * c)
            wk = jnp.concatenate([ks[0][rows], ks[1][rows]], axis=0).astype(BF16)
            lhs = jnp.concatenate([kbs[0][rows], qs[0][rows], kbs[1][rows], qs[1][rows]],
                                  axis=0).astype(BF16)
            out = _dot_nt(lhs, wk)
            gcrow = jnp.concatenate([gct[h0:h0 + 1, rows], gct[h1:h1 + 1, rows]], axis=1)
            diff = gch[rows] - gcrow
            decay = jnp.where(causal, jnp.exp(jnp.where(causal, diff, 0.0)), 0.0)
            a_pairs[ci][p] = jnp.where(strict, jnp.where(lane_lo, out[0:c], out[2 * c:3 * c]) * decay, 0.0)
            qk = jnp.where(lane_lo, out[c:2 * c], out[3 * c:4 * c]) * decay
            kdt = jnp.concatenate([kds[0][rows], kds[1][rows]], axis=0).T
            m0 = ci * 3 * c
            m2_ref[m0:m0 + c, p * LANES:(p + 1) * LANES] = qk.astype(BF16)
            m2_ref[m0 + c:m0 + 3 * c, p * LANES:(p + 1) * LANES] = kdt.astype(BF16)
            rhs_pairs[ci][p] = (rhss[0][rows], rhss[1][rows])

    zeros = jnp.zeros((c, 2 * HEAD_DIM), BF16)
    ngrp = GDN_HEADS // 4
    tls = _unit_lower_inverse4(
        [jnp.concatenate([a_pairs[ci][2 * grp], a_pairs[ci][2 * grp + 1]], axis=1)
         for ci in range(nc) for grp in range(ngrp)])
    for ci in range(nc):
        rows = slice(ci * c, (ci + 1) * c)
        for grp in range(ngrp):
            tl = tls[ci * ngrp + grp]
            for j in range(2):
                p = 2 * grp + j
                r0, r1 = rhs_pairs[ci][p]
                bd = jnp.concatenate([jnp.concatenate([r0.astype(BF16), zeros], axis=1),
                                      jnp.concatenate([zeros, r1.astype(BF16)], axis=1)], axis=0)
                sol = _dot(tl[:, j * LANES:(j + 1) * LANES].astype(BF16), bd)
                for i in range(2):
                    h = 2 * p + i
                    lo, hi = h * HEAD_DIM, (h + 1) * HEAD_DIM
                    u_ref[rows, lo:hi] = sol[:, 2 * i * HEAD_DIM:(2 * i + 1) * HEAD_DIM]
                    w_ref[rows, lo:hi] = sol[:, (2 * i + 1) * HEAD_DIM:(2 * i + 2) * HEAD_DIM].astype(BF16)


def _gdn_intra(proj, x, wa_hi, wa_lo, wb_hi, wb_lo, conv_w, prm, seq, rt):
    t = x.shape[0]
    nc = rt // CHUNK
    kern = functools.partial(_gdn_intra_kernel, rt=rt, tiles_per_seq=seq // rt)
    full = lambda i: (0, 0)
    tile = lambda i: (i, 0)
    sel, sel_halo = _shift_selectors(rt)
    return pl.pallas_call(
        kern,
        grid=(t // rt,),
        in_specs=[
            pl.BlockSpec((rt, 3 * GDN_WIDTH), tile),
            pl.BlockSpec((CONV_HALO, 3 * GDN_WIDTH),
                         lambda i: (jnp.maximum(i * (rt // CONV_HALO) - 1, 0), 0)),
            pl.BlockSpec((rt, D_MODEL), tile),
            pl.BlockSpec(sel.shape, full),
            pl.BlockSpec(sel_halo.shape, full),
            pl.BlockSpec((D_MODEL, LANES), full),
            pl.BlockSpec((D_MODEL, LANES), full),
            pl.BlockSpec((D_MODEL, LANES), full),
            pl.BlockSpec((D_MODEL, LANES), full),
            pl.BlockSpec((SHORT_CONV, 3 * GDN_WIDTH), full),
            pl.BlockSpec((8, LANES), full),
        ],
        out_specs=[pl.BlockSpec((rt, GDN_WIDTH), tile),
                   pl.BlockSpec((rt, GDN_WIDTH), tile),
                   pl.BlockSpec((rt, GDN_WIDTH), tile),
                   pl.BlockSpec((nc * 3 * CHUNK, GDN_HEADS // 2 * LANES), tile),
                   pl.BlockSpec((nc * GDN_HEADS, LANES), tile)],
        out_shape=[jax.ShapeDtypeStruct((t, GDN_WIDTH), F32),
                   jax.ShapeDtypeStruct((t, GDN_WIDTH), BF16),
                   jax.ShapeDtypeStruct((t, GDN_WIDTH), BF16),
                   jax.ShapeDtypeStruct((t // CHUNK * 3 * CHUNK, GDN_HEADS // 2 * LANES), BF16),
                   jax.ShapeDtypeStruct((t // CHUNK * GDN_HEADS, LANES), F32)],
        scratch_shapes=[pltpu.VMEM((SHORT_CONV, rt, 3 * GDN_WIDTH), F32)],
        compiler_params=_params("parallel"),
        name="gdn_intra",
    )(proj, proj, x, sel, sel_halo, wa_hi, wa_lo, wb_hi, wb_lo, conv_w, prm)


def _gdn_scan_kernel(u_ref, w_ref, qd_ref, m2_ref, eg_ref, z_ref, ong_ref, o_ref, s_ref, *, nck):
    c = CHUNK
    nseq = u_ref.shape[0]

    @pl.when(pl.program_id(1) == 0)
    def _():
        s_ref[...] = jnp.zeros_like(s_ref)

    ong = ong_ref[...]
    zeros = jnp.zeros((c, HEAD_DIM), BF16)
    span = lambda h: slice(h * HEAD_DIM, (h + 1) * HEAD_DIM)
    chains = [(b, h) for b in range(nseq) for h in range(GDN_HEADS)]
    states = {bh: s_ref[bh[0], bh[1]] for bh in chains}
    for ci in range(nck):
        rows = slice(ci * c, (ci + 1) * c)
        rs = {(b, h): _dot(jnp.concatenate([w_ref[b, rows, span(h)], qd_ref[b, rows, span(h)]], axis=0),
                           states[b, h].astype(BF16)) for b, h in chains}
        v_new = {(b, h): (u_ref[b, rows, span(h)] - rs[b, h][:c]).astype(BF16) for b, h in chains}
        r2s = {}
        for b in range(nseq):
            for p in range(GDN_HEADS // 2):
                bd = jnp.concatenate([jnp.concatenate([v_new[b, 2 * p], zeros], axis=1),
                                      jnp.concatenate([zeros, v_new[b, 2 * p + 1]], axis=1)], axis=0)
                r2s[b, p] = _dot(m2_ref[b, ci * 3 * c:(ci + 1) * 3 * c, p * LANES:(p + 1) * LANES], bd)
        for b, h in chains:
            half = span(h % 2)
            r2 = r2s[b, h // 2]
            decay = eg_ref[b, ci * GDN_HEADS + h:ci * GDN_HEADS + h + 1, :]
            states[b, h] = states[b, h] * decay + r2[c:, half]
            o = rs[b, h][c:] + r2[:c, half]
            o = o * lax.rsqrt(jnp.mean(o * o, axis=-1, keepdims=True) + EPS) * ong
            o = o * _silu(z_ref[b, rows, span(h)].astype(F32))
            o_ref[b, rows, span(h)] = o.astype(o_ref.dtype)
    for b, h in chains:
        s_ref[b, h] = states[b, h]


def _gdn_scan(u, w, qd, m2, eg, proj, ong, batch, seq):
    t = batch * seq
    nck = min(SCAN_CHUNKS, seq // CHUNK)
    c = nck * CHUNK
    nch = seq // c
    ns = SCAN_SEQS if batch % SCAN_SEQS == 0 else 1
    groups = batch // ns
    split = lambda a: a.reshape(ns, a.shape[0] // ns, a.shape[1])
    blk = lambda b, n: (0, b * nch + n, 0)
    out = pl.pallas_call(
        functools.partial(_gdn_scan_kernel, nck=nck),
        grid=(groups, nch),
        in_specs=[
            pl.BlockSpec((ns, c, GDN_WIDTH), blk),
            pl.BlockSpec((ns, c, GDN_WIDTH), blk),
            pl.BlockSpec((ns, c, GDN_WIDTH), blk),
            pl.BlockSpec((ns, 3 * c, GDN_HEADS // 2 * LANES), blk),
            pl.BlockSpec((ns, nck * GDN_HEADS, LANES), blk),
            pl.BlockSpec((ns, c, GDN_WIDTH), lambda b, n: (0, b * nch + n, 3)),
            pl.BlockSpec((1, HEAD_DIM), lambda b, n: (0, 0)),
        ],
        out_specs=pl.BlockSpec((ns, c, GDN_WIDTH), blk),
        out_shape=jax.ShapeDtypeStruct((ns, t // ns, GDN_WIDTH), BF16),
        scratch_shapes=[pltpu.VMEM((ns, GDN_HEADS, HEAD_DIM, HEAD_DIM), F32)],
        compiler_params=_params("parallel", "arbitrary"),
        name="gdn_scan",
    )(split(u), split(w), split(qd), split(m2), split(eg), split(proj), ong)
    return out.reshape(t, GDN_WIDTH)


def _pack_bf16_pairs(y):
    n = y.shape[1] // 2
    yb = y.astype(BF16).astype(F32)
    lo = lax.bitcast_convert_type(yb[:, :n], U32)
    hi = lax.bitcast_convert_type(yb[:, n:], U32)
    return jnp.bitwise_or(jnp.right_shift(lo, 16), jnp.bitwise_and(hi, HI_MASK))


def _unpack_bf16_pairs(w):
    lo = lax.bitcast_convert_type(jnp.left_shift(w, 16), F32)
    hi = lax.bitcast_convert_type(jnp.bitwise_and(w, HI_MASK), F32)
    return lo, hi


def _mixer_kernel(glu_ref, ga_ref, gb_ref, o_ref, x_ref, woa_ref, wob_ref, wo_ref, dww_ref, vec_ref,
                  x1_ref, x1b_ref, x1p_ref, ubuf_ref, sh_ref, conv_ref, *, ts, rc):
    halo = CONF_HALO

    @pl.when(pl.program_id(1) == 0)
    def _():
        ubuf_ref[0:halo, :] = jnp.zeros((halo, CONF_CH), F32)

    @pl.when(pl.program_id(1) != 0)
    def _():
        ubuf_ref[0:halo, :] = ubuf_ref[ts:ts + halo, :]

    glu_a = glu_ref[:, :CONF_CH].astype(F32)
    glu_b = glu_ref[:, CONF_CH:].astype(F32)
    ubuf_ref[halo:halo + ts, :] = glu_a * _sigmoid(glu_b)

    dw_b = vec_ref[0:1, :]
    cln_g = vec_ref[1:2, :]
    cln_b = vec_ref[2:3, :]
    ln1_g = vec_ref[3:4, :]
    ln1_b = vec_ref[4:5, :]

    span = ts + halo - 8
    for s in range(1, 8):
        sh_ref[s - 1] = ubuf_ref[s:s + span, :]

    def tap_rows(j, r0):
        o = halo - (CONF_KERNEL - 1) + j
        q, s = o // 8, o % 8
        if s == 0:
            return ubuf_ref[r0 + o:r0 + o + rc, :]
        return sh_ref[s - 1, r0 + 8 * q:r0 + 8 * q + rc, :]

    hs = ts // 2
    for h0 in range(0, ts, hs):
        rows = slice(h0, h0 + hs)
        gated_a = _sigmoid(ga_ref[rows, :].astype(F32)) * _dot(o_ref[rows, :], woa_ref[...])
        for r0 in range(h0, h0 + hs, rc):
            tap_w = lambda j: jnp.concatenate([dww_ref[j]] * (rc // 8), axis=0)
            acc = tap_w(0) * tap_rows(0, r0)
            for j in range(1, CONF_KERNEL):
                acc = acc + tap_w(j) * tap_rows(j, r0)
            conv_ref[r0:r0 + rc, :] = acc
        uc = _silu(_layer_norm(conv_ref[rows, :] + dw_b, cln_g, cln_b))
        branch_b = _dot(uc.astype(BF16), wob_ref[...])
        hmix = gated_a + _sigmoid(gb_ref[rows, :].astype(F32)) * branch_b
        mix = _dot(hmix.astype(BF16), wo_ref[...])
        x1 = _layer_norm(DEEPNORM_ALPHA * x_ref[rows, :] + mix, ln1_g, ln1_b)
        x1_ref[rows, :] = x1
        x1b_ref[rows, :] = x1.astype(BF16)
        x1p_ref[rows, :] = _pack_bf16_pairs(x1)


def _mixer(proj, o_gdn, x, woa, wob, wo, dww, vec, batch, seq, ts):
    t = batch * seq
    nt = seq // ts
    rows = lambda b, n: b * nt + n
    full = lambda b, n: (0, 0)
    kern = functools.partial(_mixer_kernel, ts=ts, rc=32)
    return pl.pallas_call(
        kern,
        grid=(batch, nt),
        in_specs=[
            pl.BlockSpec((ts, 2 * CONF_CH), lambda b, n: (rows(b, n), 2)),
            pl.BlockSpec((ts, D_MODEL), lambda b, n: (rows(b, n), 6)),
            pl.BlockSpec((ts, D_MODEL), lambda b, n: (rows(b, n), 7)),
            pl.BlockSpec((ts, GDN_WIDTH), lambda b, n: (rows(b, n), 0)),
            pl.BlockSpec((ts, D_MODEL), lambda b, n: (rows(b, n), 0)),
            pl.BlockSpec((GDN_WIDTH, D_MODEL), full),
            pl.BlockSpec((CONF_CH, D_MODEL), full),
            pl.BlockSpec((D_MODEL, D_MODEL), full),
            pl.BlockSpec((CONF_KERNEL, 8, CONF_CH), lambda b, n: (0, 0, 0)),
            pl.BlockSpec((8, D_MODEL), full),
        ],
        out_specs=[pl.BlockSpec((ts, D_MODEL), lambda b, n: (rows(b, n), 0)),
                   pl.BlockSpec((ts, D_MODEL), lambda b, n: (rows(b, n), 0)),
                   pl.BlockSpec((ts, D_MODEL // 2), lambda b, n: (rows(b, n), 0))],
        out_shape=[jax.ShapeDtypeStruct((t, D_MODEL), F32),
                   jax.ShapeDtypeStruct((t, D_MODEL), BF16),
                   jax.ShapeDtypeStruct((t, D_MODEL // 2), U32)],
        scratch_shapes=[pltpu.VMEM((CONF_HALO + ts, CONF_CH), F32),
                        pltpu.VMEM((7, ts + CONF_HALO - 8, CONF_CH), F32),
                        pltpu.VMEM((ts, CONF_CH), F32)],
        compiler_params=_params("parallel", "arbitrary"),
        name="mixer",
    )(proj, proj, proj, o_gdn, x, woa, wob, wo, dww, vec)


def _route_kernel(x_ref, wrh_ref, wrl_ref, bias_ref, eidx_ref, wts_ref, rank_ref, cnt_ref, carry_ref, *, tt):
    @pl.when(pl.program_id(0) == 0)
    def _():
        carry_ref[...] = jnp.zeros_like(carry_ref)

    xh, xl = _split(x_ref[...])
    wrh = wrh_ref[...]
    logits = _dot_nt(wrh, xh) + _dot_nt(wrh, xl) + _dot_nt(wrl_ref[...], xh)
    s = _sigmoid(logits)
    biased = s + bias_ref[...]

    sub = lax.broadcasted_iota(I32, (GROUP_SIZE, tt), 0)
    groups = [biased[g * GROUP_SIZE:(g + 1) * GROUP_SIZE, :] for g in range(N_GROUPS)]
    gs = []
    for bg in groups:
        m1 = jnp.max(bg, axis=0, keepdims=True)
        first = jnp.min(jnp.where(bg == m1, sub, GROUP_SIZE), axis=0, keepdims=True)
        m2 = jnp.max(jnp.where(sub == first, -jnp.inf, bg), axis=0, keepdims=True)
        gs.append(m1 + m2)

    masked_parts = []
    for g in range(N_GROUPS):
        beaten = jnp.zeros((1, tt), I32)
        for o in range(N_GROUPS):
            if o == g:
                continue
            wins = (gs[o] >= gs[g]) if o < g else (gs[o] > gs[g])
            beaten = beaten + wins.astype(I32)
        keep = jnp.broadcast_to(beaten < TOPK_GROUPS, (GROUP_SIZE, tt))
        masked_parts.append(jnp.where(keep, groups[g], -jnp.inf))
    masked = jnp.concatenate(masked_parts, axis=0)

    eiota = lax.broadcasted_iota(I32, (N_EXPERTS, tt), 0)
    sel_all = jnp.zeros((N_EXPERTS, tt), F32)
    picks = []
    for _ in range(TOP_K):
        m = jnp.max(masked, axis=0, keepdims=True)
        idx = jnp.min(jnp.where(masked == m, eiota, N_EXPERTS), axis=0, keepdims=True)
        onehot = eiota == idx
        picks.append((idx, onehot))
        sel_all = jnp.where(onehot, 1.0, sel_all)
        masked = jnp.where(onehot, -jnp.inf, masked)

    tr = lax.broadcasted_iota(I32, (tt, tt), 0)
    tc = lax.broadcasted_iota(I32, (tt, tt), 1)
    before = (tr < tc).astype(BF16)
    sel_b = sel_all.astype(BF16)
    carry = carry_ref[...]
    rank_all = _dot(sel_b, before) + carry[:, 0:1]
    carry_new = carry + _dot(sel_b, jnp.ones((tt, LANES), BF16))
    carry_ref[...] = carry_new
    cnt_ref[...] = carry_new

    s_sel = [jnp.sum(jnp.where(oh, s, 0.0), axis=0, keepdims=True) for _, oh in picks]
    total = s_sel[0]
    for v in s_sel[1:]:
        total = total + v
    for k, (idx, oh) in enumerate(picks):
        eidx_ref[k:k + 1, :] = idx
        wts_ref[k:k + 1, :] = s_sel[k] / total * ROUTED_SCALE
        rank_ref[k:k + 1, :] = jnp.sum(jnp.where(oh, rank_all, 0.0), axis=0, keepdims=True).astype(I32)


def _route(x1, wr_hi, wr_lo, bias, tt):
    t = x1.shape[0]
    kern = functools.partial(_route_kernel, tt=tt)
    return pl.pallas_call(
        kern,
        grid=(t // tt,),
        in_specs=[pl.BlockSpec((tt, D_MODEL), lambda i: (i, 0)),
                  pl.BlockSpec((N_EXPERTS, D_MODEL), lambda i: (0, 0)),
                  pl.BlockSpec((N_EXPERTS, D_MODEL), lambda i: (0, 0)),
                  pl.BlockSpec((N_EXPERTS, tt), lambda i: (0, 0))],
        out_specs=[pl.BlockSpec((TOP_K, tt), lambda i: (0, i)),
                   pl.BlockSpec((TOP_K, tt), lambda i: (0, i)),
                   pl.BlockSpec((TOP_K, tt), lambda i: (0, i)),
                   pl.BlockSpec((N_EXPERTS, LANES), lambda i: (0, 0))],
        out_shape=[jax.ShapeDtypeStruct((TOP_K, t), I32),
                   jax.ShapeDtypeStruct((TOP_K, t), F32),
                   jax.ShapeDtypeStruct((TOP_K, t), I32),
                   jax.ShapeDtypeStruct((N_EXPERTS, LANES), F32)],
        scratch_shapes=[pltpu.VMEM((N_EXPERTS, LANES), F32)],
        compiler_params=_params("arbitrary"),
        name="route",
    )(x1, wr_hi, wr_lo, bias)


def _plan_kernel(cnt_ref, pstart_ref, plan_ref):
    e, nb = N_EXPERTS, plan_ref.shape[1]
    counts = cnt_ref[...]
    nblk = jnp.floor((counts + (ROW_BLOCK - 1)) * (1.0 / ROW_BLOCK))
    hi = jnp.floor(nblk * (1.0 / 256.0))
    lo = nblk - 256.0 * hi
    r = lax.broadcasted_iota(I32, (e, e), 0)
    c = lax.broadcasted_iota(I32, (e, e), 1)
    ltri = (r >= c).astype(BF16)
    bend = 256.0 * _dot(ltri, hi.astype(BF16)) + _dot(ltri, lo.astype(BF16))
    pend = bend * ROW_BLOCK
    pstart = pend - nblk * ROW_BLOCK
    pstart_ref[...] = pstart.astype(I32)

    bs = (lax.broadcasted_iota(I32, (e, nb), 1) * ROW_BLOCK).astype(F32)
    pend_b = jnp.broadcast_to(pend[:, 0:1], (e, nb))
    pstart_b = jnp.broadcast_to(pstart[:, 0:1], (e, nb))
    used_b = jnp.broadcast_to((pstart + counts)[:, 0:1], (e, nb))
    owner = jnp.sum(jnp.where(pend_b <= bs, 1.0, 0.0), axis=0, keepdims=True)
    inside = jnp.where(pstart_b <= bs, jnp.where(bs < pend_b, 1.0, 0.0), 0.0)
    real = jnp.sum(inside * jnp.clip(used_b - bs, 0.0, float(ROW_BLOCK)), axis=0, keepdims=True)
    plan_ref[0:1, :] = jnp.minimum(owner, float(e - 1)).astype(I32)
    plan_ref[1:2, :] = real.astype(I32)
    plan_ref[2:8, :] = jnp.zeros((6, nb), I32)


def _plan(cnt, n_blocks):
    nb = -(-n_blocks // LANES) * LANES
    pstart, plan = pl.pallas_call(
        _plan_kernel,
        out_shape=[jax.ShapeDtypeStruct((N_EXPERTS, LANES), I32),
                   jax.ShapeDtypeStruct((8, nb), I32)],
        name="plan",
    )(cnt)
    return pstart[:, 0], plan[0, :n_blocks], plan[1, :n_blocks]


def _dest_kernel(eidx_ref, rank_ref, pstart_ref, dest_ref):
    eidx = eidx_ref[...]
    acc = rank_ref[...]
    for e in range(N_EXPERTS):
        acc = acc + jnp.where(eidx == e, pstart_ref[e], 0)
    dest_ref[...] = acc


def _dest(eidx_t, rank_t, pstart, tt):
    t = eidx_t.shape[1]
    return pl.pallas_call(
        _dest_kernel,
        grid=(t // tt,),
        in_specs=[pl.BlockSpec((TOP_K, tt), lambda i: (0, i)),
                  pl.BlockSpec((TOP_K, tt), lambda i: (0, i)),
                  pl.BlockSpec(memory_space=pltpu.SMEM)],
        out_specs=pl.BlockSpec((TOP_K, tt), lambda i: (0, i)),
        out_shape=jax.ShapeDtypeStruct((TOP_K, t), I32),
        compiler_params=_params("parallel"),
        name="dest",
    )(eidx_t, rank_t, pstart)


def _dispatch(dest_kt, x1p, n_rows):
    t, width = x1p.shape
    info = plsc.get_sparse_core_info()
    nc, nw = info.num_cores, info.num_cores * info.num_subcores
    chunk = SC_SCATTER_ROWS
    per_w = t // nw
    n_chunks = per_w // chunk
    assert per_w % chunk == 0
    idx = dest_kt.reshape(TOP_K, nw, n_chunks, chunk).transpose(1, 2, 0, 3).reshape(nw, n_chunks * TOP_K, chunk)
    mesh = plsc.VectorSubcoreMesh(core_axis_name="c", subcore_axis_name="s")

    @functools.partial(
        pl.kernel, mesh=mesh, name="dispatch",
        out_type=jax.ShapeDtypeStruct((n_rows, width), x1p.dtype),
        scratch_types=[pltpu.VMEM((n_chunks * TOP_K, chunk), I32),
                       pltpu.VMEM((chunk, width), x1p.dtype),
                       pltpu.SemaphoreType.DMA])
    def scatter(x_hbm, idx_hbm, xs_hbm, idx_v, rows_v, sem):
        wid = lax.axis_index("s") * nc + lax.axis_index("c")
        base = wid * per_w
        pltpu.sync_copy(idx_hbm.at[wid], idx_v)
        for j in range(n_chunks):
            pltpu.sync_copy(x_hbm.at[pl.ds(base + j * chunk, chunk)], rows_v)
            copies = [pltpu.make_async_copy(rows_v, xs_hbm.at[idx_v.at[j * TOP_K + k]], sem)
                      for k in range(TOP_K)]
            for cp in copies:
                cp.start()
            for cp in copies:
                cp.wait()

    return scatter(x1p, idx)


def _gather_rows(table, idx):
    n = idx.shape[0]
    width = table.shape[1]
    info = plsc.get_sparse_core_info()
    nc, nw = info.num_cores, info.num_cores * info.num_subcores
    chunk = SC_GATHER_ROWS
    per_w = n // nw
    n_chunks = per_w // chunk
    assert per_w % (2 * chunk) == 0
    mesh = plsc.VectorSubcoreMesh(core_axis_name="c", subcore_axis_name="s")

    @functools.partial(
        pl.kernel, mesh=mesh, name="gather_rows",
        out_type=jax.ShapeDtypeStruct((n, width), table.dtype),
        scratch_types=[pltpu.VMEM((n_chunks, chunk), I32),
                       pltpu.VMEM((2, chunk, width), table.dtype),
                       pltpu.SemaphoreType.DMA((2,)),
                       pltpu.SemaphoreType.DMA((2,))])
    def gather(table_hbm, idx_hbm, out_hbm, idx_v, rows_v, gsem, osem):
        wid = lax.axis_index("s") * nc + lax.axis_index("c")
        base = wid * per_w
        pltpu.sync_copy(idx_hbm.at[wid], idx_v)

        def fetch(j, b):
            return pltpu.make_async_copy(table_hbm.at[idx_v.at[j]], rows_v.at[b], gsem.at[b])

        def put(j, b):
            return pltpu.make_async_copy(rows_v.at[b], out_hbm.at[pl.ds(base + j * chunk, chunk)], osem.at[b])

        fetch(0, 0).start()

        @pl.loop(0, n_chunks, step=2)
        def _(j0):
            for b in range(2):
                j = j0 + b
                fetch(j, b).wait()

                @pl.when(j + 1 < n_chunks)
                def _():
                    @pl.when(j >= 1)
                    def _():
                        put(j - 1, 1 - b).wait()

                    fetch(j + 1, 1 - b).start()

                put(j, b).start()

        put(n_chunks - 2, 0).wait()
        put(n_chunks - 1, 1).wait()

    return gather(table, idx.reshape(nw, n_chunks, chunk))


def _xs_copy(xs_hbm, xbuf, isem, j, slot):
    return pltpu.make_async_copy(xs_hbm.at[pl.ds(j * ROW_BLOCK, ROW_BLOCK)], xbuf.at[slot], isem.at[slot])


def _ys_copy(ybuf, ys_hbm, osem, j, slot):
    return pltpu.make_async_copy(ybuf.at[slot], ys_hbm.at[pl.ds(j * ROW_BLOCK, ROW_BLOCK)], osem.at[slot])


def _experts_kernel(be_ref, nv_ref, xs_hbm, wg_ref, wu_ref, wd_ref, ys_hbm,
                    xbuf, ybuf, wgu_s, wd_s, cur_ref, isem, osem):
    i = pl.program_id(0)
    n_valid = nv_ref[i]
    half = EXPERT_SUB
    slot = lax.rem(i, EXPERT_IN_SLOTS)
    oslot = lax.rem(i, 2)

    @pl.when(i == 0)
    def _():
        cur_ref[0] = -1
        for j in range(2):
            @pl.when(nv_ref[j] > 0)
            def _():
                _xs_copy(xs_hbm, xbuf, isem, j, j).start()

    @pl.when(nv_ref[i + 2] > 0)
    def _():
        _xs_copy(xs_hbm, xbuf, isem, i + 2, lax.rem(i + 2, EXPERT_IN_SLOTS)).start()

    @pl.when((i >= 2) & (nv_ref[jnp.maximum(i - 2, 0)] > 0))
    def _():
        _ys_copy(ybuf, ys_hbm, osem, i - 2, oslot).wait()

    @pl.when((n_valid > 0) & (cur_ref[0] != be_ref[i]))
    def _():
        wgu_s[:, :EXPERT_FF] = wg_ref[...].astype(BF16)
        wgu_s[:, EXPERT_FF:] = wu_ref[...].astype(BF16)
        wd_s[...] = wd_ref[...].astype(BF16)
        cur_ref[0] = be_ref[i]

    def rows_bf16(r0):
        valid = lax.broadcasted_iota(I32, (half, xbuf.shape[2]), 0) + r0 < n_valid
        lo, hi = _unpack_bf16_pairs(jnp.where(valid, xbuf[slot, r0:r0 + half, :], jnp.zeros((), U32)))
        return jnp.concatenate([lo.astype(BF16), hi.astype(BF16)], axis=1)

    def hidden(gu):
        return (_silu(gu[:, :EXPERT_FF]) * gu[:, EXPERT_FF:]).astype(BF16)

    @pl.when(n_valid > 0)
    def _():
        _xs_copy(xs_hbm, xbuf, isem, i, slot).wait()

    zeros = jnp.zeros((half, ybuf.shape[2]), ybuf.dtype)
    for r0 in range(0, ROW_BLOCK, 2 * half):
        r1 = r0 + half

        @pl.when(n_valid > r1)
        def _():
            xa, xb = rows_bf16(r0), rows_bf16(r1)
            gua = _dot(xa, wgu_s[...])
            gub = _dot(xb, wgu_s[...])
            ya = _dot(hidden(gua), wd_s[...])
            yb = _dot(hidden(gub), wd_s[...])
            ybuf[oslot, r0:r1, :] = _pack_bf16_pairs(ya)
            ybuf[oslot, r1:r1 + half, :] = _pack_bf16_pairs(yb)

        @pl.when((n_valid > r0) & (n_valid <= r1))
        def _():
            ya = _dot(hidden(_dot(rows_bf16(r0), wgu_s[...])), wd_s[...])
            ybuf[oslot, r0:r1, :] = _pack_bf16_pairs(ya)
            ybuf[oslot, r1:r1 + half, :] = zeros

        @pl.when((n_valid > 0) & (n_valid <= r0))
        def _():
            ybuf[oslot, r0:r1, :] = zeros
            ybuf[oslot, r1:r1 + half, :] = zeros

    @pl.when(n_valid > 0)
    def _():
        _ys_copy(ybuf, ys_hbm, osem, i, oslot).start()


def _experts(block_e, n_valid, xs, wg, wu, wd, layer):
    n_rows, width = xs.shape
    steps = n_rows // ROW_BLOCK + 2
    be = jnp.concatenate([block_e, jnp.full((2,), N_EXPERTS - 1, I32)])
    nv = jnp.concatenate([n_valid, jnp.zeros((4,), I32)])
    grid_spec = pltpu.PrefetchScalarGridSpec(
        num_scalar_prefetch=2,
        grid=(steps,),
        in_specs=[pl.BlockSpec(memory_space=pl.ANY),
                  pl.BlockSpec((None, None, D_MODEL, EXPERT_FF), lambda i, be, nv: (layer, be[i], 0, 0)),
                  pl.BlockSpec((None, None, D_MODEL, EXPERT_FF), lambda i, be, nv: (layer, be[i], 0, 0)),
                  pl.BlockSpec((None, None, EXPERT_FF, D_MODEL), lambda i, be, nv: (layer, be[i], 0, 0))],
        out_specs=pl.BlockSpec(memory_space=pl.ANY),
        scratch_shapes=[pltpu.VMEM((EXPERT_IN_SLOTS, ROW_BLOCK, width), xs.dtype),
                        pltpu.VMEM((2, ROW_BLOCK, width), xs.dtype),
                        pltpu.VMEM((D_MODEL, 2 * EXPERT_FF), BF16),
                        pltpu.VMEM((EXPERT_FF, D_MODEL), BF16),
                        pltpu.SMEM((1,), I32),
                        pltpu.SemaphoreType.DMA((EXPERT_IN_SLOTS,)),
                        pltpu.SemaphoreType.DMA((2,))],
    )
    return pl.pallas_call(
        _experts_kernel,
        grid_spec=grid_spec,
        out_shape=jax.ShapeDtypeStruct((n_rows, width), xs.dtype),
        compiler_params=_params("arbitrary"),
        name="experts",
    )(be, nv, xs, wg, wu, wd)


def _shared_kernel(x1b_ref, wsg_ref, wsu_ref, wsd_ref, anchor_ref, sh_ref):
    del anchor_ref
    xb = x1b_ref[...]
    hid = (_silu(_dot(xb, wsg_ref[...])) * _dot(xb, wsu_ref[...])).astype(BF16)
    sh_ref[...] = _dot(hid, wsd_ref[...]).astype(sh_ref.dtype)


def _shared(x1b, wsg, wsu, wsd, anchor, tt, row0, nrows):
    full = lambda i: (0, 0)
    first = row0 // tt
    return pl.pallas_call(
        _shared_kernel,
        grid=(nrows // tt,),
        in_specs=[pl.BlockSpec((tt, D_MODEL), lambda i: (first + i, 0)),
                  pl.BlockSpec((D_MODEL, SHARED_FF), full),
                  pl.BlockSpec((D_MODEL, SHARED_FF), full),
                  pl.BlockSpec((SHARED_FF, D_MODEL), full),
                  pl.BlockSpec(memory_space=pl.ANY)],
        out_specs=pl.BlockSpec((tt, D_MODEL), lambda i: (i, 0)),
        out_shape=jax.ShapeDtypeStruct((nrows, D_MODEL), BF16),
        compiler_params=_params("parallel"),
        name="shared",
    )(x1b, wsg, wsu, wsd, anchor)


def _combine_kernel(yg_ref, w_ref, x1_ref, sha_ref, shb_ref, vec_ref, x2_ref, *, half_steps):
    in_first = pl.program_id(0) < half_steps
    shared = jnp.where(in_first, sha_ref[...], shb_ref[...]).astype(F32)
    w = w_ref[...]
    half = D_MODEL // 2
    acc_lo, acc_hi = shared[:, :half], shared[:, half:]
    for k in range(TOP_K):
        lo, hi = _unpack_bf16_pairs(yg_ref[k])
        acc_lo = acc_lo + w[:, k:k + 1] * lo
        acc_hi = acc_hi + w[:, k:k + 1] * hi
    acc = jnp.concatenate([acc_lo, acc_hi], axis=1)
    x2 = _layer_norm(DEEPNORM_ALPHA * x1_ref[...] + acc, vec_ref[0:1, :], vec_ref[1:2, :])
    x2_ref[...] = x2


def _combine(yg, w_tok, x1, shared_a, shared_b, vec, tt):
    t = x1.shape[0]
    half_steps = shared_a.shape[0] // tt
    full = lambda i: (0, 0)
    return pl.pallas_call(
        functools.partial(_combine_kernel, half_steps=half_steps),
        grid=(t // tt,),
        in_specs=[pl.BlockSpec((TOP_K, tt, yg.shape[2]), lambda i: (0, i, 0)),
                  pl.BlockSpec((tt, TOP_K), lambda i: (i, 0)),
                  pl.BlockSpec((tt, D_MODEL), lambda i: (i, 0)),
                  pl.BlockSpec((tt, D_MODEL), lambda i: (jnp.minimum(i, half_steps - 1), 0)),
                  pl.BlockSpec((tt, D_MODEL), lambda i: (jnp.maximum(i - half_steps, 0), 0)),
                  pl.BlockSpec((8, D_MODEL), full)],
        out_specs=pl.BlockSpec((tt, D_MODEL), lambda i: (i, 0)),
        out_shape=jax.ShapeDtypeStruct((t, D_MODEL), F32),
        compiler_params=_params("parallel"),
        name="combine",
    )(yg, w_tok, x1, shared_a, shared_b, vec)


def _pad_rows(a, rows):
    return jnp.zeros((rows, a.shape[-1]), F32).at[:a.shape[0]].set(a.astype(F32))


def _layer(x, p, stacked, layer, batch, seq):
    t = batch * seq
    w_main, wa_hi, wa_lo, wb_hi, wb_lo = _regroup(jnp.swapaxes(stacked["w_in"], 1, 2), layer, 512)

    tm = min(1024, t)
    proj = _proj(x, w_main, tm, 2048)

    prm = jnp.zeros((8, LANES), F32)
    prm = prm.at[0, :GDN_HEADS].set(p["a_log"]).at[1, :GDN_HEADS].set(p["dt_bias"])
    ong = p["o_norm_g"].reshape(1, HEAD_DIM).astype(F32)
    u, w, qd, m2, eg = _gdn_intra(proj, x, wa_hi, wa_lo, wb_hi, wb_lo, p["conv_qkv"].astype(F32), prm,
                                  seq, min(GDN_TILE, seq))
    o_gdn = _gdn_scan(u, w, qd, m2, eg, proj, ong, batch, seq)

    ts = min(256, seq)
    dww = jnp.broadcast_to(p["dw_w"].astype(F32)[:, None, :], (CONF_KERNEL, 8, CONF_CH))
    vec = _pad_rows(jnp.stack([p["dw_b"], p["cln_g"], p["cln_b"], p["ln1_g"], p["ln1_b"]]), 8)
    x1, x1b, x1p = _mixer(proj, o_gdn, x, p["w_oa"].astype(BF16), p["w_ob"].astype(BF16),
                          p["w_o"].astype(BF16), dww, vec, batch, seq, ts)

    tt_r = min(512, t)
    wr_t = p["w_router"].T.astype(F32)
    wr_hi = wr_t.astype(BF16)
    wr_lo = (wr_t - wr_hi.astype(F32)).astype(BF16)
    bias = jnp.broadcast_to(p["router_bias"].astype(F32)[:, None], (N_EXPERTS, tt_r))
    eidx_t, wts_t, rank_t, cnt = _route(x1, wr_hi, wr_lo, bias, tt_r)

    n_blocks = -(-(t * TOP_K + N_EXPERTS * (ROW_BLOCK - 1)) // ROW_BLOCK)
    n_rows = n_blocks * ROW_BLOCK
    pstart, block_e, n_valid = _plan(cnt, n_blocks)

    dest = _dest(eidx_t, rank_t, pstart, min(2048, t))
    sh_w = (p["w_sh_gate"].astype(BF16), p["w_sh_up"].astype(BF16), p["w_sh_down"].astype(BF16))
    tt_s = min(512, t // 2)
    xs = _dispatch(dest, x1p, n_rows)
    shared_a = _shared(x1b, *sh_w, dest, tt_s, 0, t // 2)
    ys = _experts(block_e, n_valid, xs, stacked["w_gate_e"], stacked["w_up_e"], stacked["w_down_e"], layer)
    yg = _gather_rows(ys, dest.reshape(TOP_K * t)).reshape(TOP_K, t, ys.shape[1])
    shared_b = _shared(x1b, *sh_w, ys, tt_s, t // 2, t // 2)
    vec2 = _pad_rows(jnp.stack([p["ln2_g"], p["ln2_b"]]), 8)
    return _combine(yg, wts_t.T, x1, shared_a, shared_b, vec2, min(256, t // 2))


_PARAM_NAMES = ("w_in", "conv_qkv", "a_log", "dt_bias", "o_norm_g", "w_oa", "dw_w", "dw_b", "cln_g",
                "cln_b", "w_ob", "w_o", "ln1_g", "ln1_b", "w_router", "router_bias", "w_gate_e",
                "w_up_e", "w_down_e", "w_sh_gate", "w_sh_up", "w_sh_down", "ln2_g", "ln2_b")


_EXPERT_WEIGHTS = ("w_in", "w_gate_e", "w_up_e", "w_down_e")


def kernel(x, w_in, conv_qkv, a_log, dt_bias, o_norm_g, w_oa, dw_w, dw_b, cln_g, cln_b, w_ob, w_o,
           ln1_g, ln1_b, w_router, router_bias, w_gate_e, w_up_e, w_down_e, w_sh_gate, w_sh_up,
           w_sh_down, ln2_g, ln2_b):
    stacked = dict(zip(_PARAM_NAMES, (w_in, conv_qkv, a_log, dt_bias, o_norm_g, w_oa, dw_w, dw_b, cln_g,
                                      cln_b, w_ob, w_o, ln1_g, ln1_b, w_router, router_bias, w_gate_e,
                                      w_up_e, w_down_e, w_sh_gate, w_sh_up, w_sh_down, ln2_g, ln2_b)))
    batch, seq, d = x.shape
    assert d == D_MODEL and seq % CHUNK == 0
    xf = x.reshape(batch * seq, d).astype(F32)
    for layer in range(w_in.shape[0]):
        p = {name: arr[layer] for name, arr in stacked.items() if name not in _EXPERT_WEIGHTS}
        xf = _layer(xf, p, stacked, layer, batch, seq)
    return xf.reshape(batch, seq, d).astype(x.dtype)
```

```python
import functools

import jax
import jax.numpy as jnp
import numpy as np
from jax import lax
from jax.experimental import pallas as pl
from jax.experimental.pallas import tpu as pltpu
from jax.experimental.pallas import tpu_sc as plsc

F32 = jnp.float32
BF16 = jnp.bfloat16
I32 = jnp.int32
U32 = jnp.uint32
HI_MASK = np.uint32(0xFFFF0000)

D_MODEL = 1024
GDN_HEADS = 8
HEAD_DIM = 128
GDN_WIDTH = GDN_HEADS * HEAD_DIM
SHORT_CONV = 4
CONV_HALO = 16
CHUNK = 64
SOLVE_BLOCK = 16
GDN_TILE = 256
SCAN_CHUNKS = 4
SCAN_SEQS = 2
CONF_CH = D_MODEL
CONF_KERNEL = 31
CONF_HALO = 32
N_EXPERTS = 64
TOP_K = 8
N_GROUPS = 8
GROUP_SIZE = N_EXPERTS // N_GROUPS
TOPK_GROUPS = 4
EXPERT_FF = 256
SHARED_FF = 256
ROUTED_SCALE = 2.5
DEPTH = 2
DEEPNORM_ALPHA = (2 * DEPTH) ** 0.25
EPS = 1e-6

LANES = 128
PROJ_COLS = 8 * D_MODEL
ROW_BLOCK = 2048
EXPERT_SUB = 256
EXPERT_IN_SLOTS = 3
SC_SCATTER_ROWS = 128
SC_GATHER_ROWS = 64
VMEM_LIMIT = 56 * 1024 * 1024


def _params(*sem):
    return pltpu.CompilerParams(dimension_semantics=sem, vmem_limit_bytes=VMEM_LIMIT)


def _dot(a, b):
    return jnp.dot(a, b, preferred_element_type=F32)


def _dot_nt(a, b):
    return lax.dot_general(a, b, (((1,), (1,)), ((), ())), preferred_element_type=F32)


def _split(a):
    hi = a.astype(BF16)
    lo = (a - hi.astype(F32)).astype(BF16)
    return hi, lo


def _dot3(a, b):
    ah, al = _split(a)
    bh, bl = _split(b)
    return _dot(ah, bh) + _dot(al, bh) + _dot(ah, bl)


def _sigmoid(x):
    return 1.0 / (1.0 + jnp.exp(-x))


def _silu(x):
    return x * _sigmoid(x)


def _layer_norm(y, g, b):
    mu = jnp.mean(y, axis=-1, keepdims=True)
    yc = y - mu
    var = jnp.mean(yc * yc, axis=-1, keepdims=True)
    return yc * lax.rsqrt(var + EPS) * g + b


def _proj_kernel(a_ref, w_ref, o_ref):
    o_ref[...] = _dot_nt(a_ref[...].astype(BF16), w_ref[...]).astype(o_ref.dtype)


def _regroup_kernel(wt_hbm, main_ref, ahi_ref, alo_ref, bhi_ref, blo_ref, buf, ab_buf, sem, *, layer, tr):
    i = pl.program_id(0)
    half = PROJ_COLS // 2
    skip = 2 * GDN_HEADS
    start = pl.multiple_of(jnp.where(i < half // tr, 0, skip) + i * tr, skip)
    cp = pltpu.make_async_copy(wt_hbm.at[layer, pl.ds(start, tr), :], buf, sem)
    cp.start()
    cp.wait()
    main_ref[...] = buf[...].astype(BF16)

    @pl.when(i == 0)
    def _():
        cp_ab = pltpu.make_async_copy(wt_hbm.at[layer, pl.ds(half, LANES), :], ab_buf, sem)
        cp_ab.start()
        cp_ab.wait()
        cols = ab_buf[...].T
        wide = jnp.concatenate([cols, jnp.zeros_like(cols)], axis=1)
        lane = lax.broadcasted_iota(I32, cols.shape, 1)
        for off, hi_ref, lo_ref in ((0, ahi_ref, alo_ref), (GDN_HEADS, bhi_ref, blo_ref)):
            hi, lo = _split(jnp.where(lane < GDN_HEADS, wide[:, off:off + LANES], 0.0))
            hi_ref[...] = hi
            lo_ref[...] = lo


def _regroup(w_in_t, layer, tr):
    d = w_in_t.shape[2]
    small = jax.ShapeDtypeStruct((d, LANES), BF16)
    whole = lambda i: (0, 0)
    return pl.pallas_call(
        functools.partial(_regroup_kernel, layer=layer, tr=tr),
        grid=(PROJ_COLS // tr,),
        in_specs=[pl.BlockSpec(memory_space=pl.ANY)],
        out_specs=[pl.BlockSpec((tr, d), lambda i: (i, 0))] + [pl.BlockSpec((d, LANES), whole)] * 4,
        out_shape=[jax.ShapeDtypeStruct((PROJ_COLS, d), BF16), small, small, small, small],
        scratch_shapes=[pltpu.VMEM((tr, d), F32), pltpu.VMEM((LANES, d), F32), pltpu.SemaphoreType.DMA],
        compiler_params=_params("arbitrary"),
        name="regroup",
    )(w_in_t)


def _proj(xb, w, tm, tn):
    m, k = xb.shape
    n = w.shape[0]
    return pl.pallas_call(
        _proj_kernel,
        grid=(m // tm, n // tn),
        in_specs=[pl.BlockSpec((tm, k), lambda i, j: (i, 0)),
                  pl.BlockSpec((tn, k), lambda i, j: (j, 0))],
        out_specs=pl.BlockSpec((tm, tn), lambda i, j: (i, j)),
        out_shape=jax.ShapeDtypeStruct((m, n), BF16),
        compiler_params=_params("parallel", "parallel"),
        name="proj",
    )(xb, w)


def _unit_lower_inverse4(als):
    c = CHUNK
    n = als[0].shape[1]
    row = lax.broadcasted_iota(I32, (c, n), 0)
    col = jnp.bitwise_and(lax.broadcasted_iota(I32, (c, n), 1), c - 1)
    shift = SOLVE_BLOCK.bit_length() - 1
    same = jnp.right_shift(row, shift) == jnp.right_shift(col, shift)
    eye = (row == col).astype(F32)
    cshift = c.bit_length() - 1
    brow = jnp.right_shift(lax.broadcasted_iota(I32, (n, n), 0), cshift)
    bcol = jnp.right_shift(lax.broadcasted_iota(I32, (n, n), 1), cshift)
    on_diag = brow == bcol

    def mm(x, y):
        yb = y.astype(BF16)
        bd = jnp.where(on_diag, jnp.concatenate([yb] * (n // c), axis=0), jnp.zeros((), BF16))
        return _dot(x.astype(BF16), bd)

    a_diag = [jnp.where(same, al, 0.0) for al in als]
    a_off = [al - ad for al, ad in zip(als, a_diag)]
    bp = [-ad for ad in a_diag]
    p = [eye + b for b in bp]
    for _ in range(3):
        bp = [mm(b, b) for b in bp]
        p = [x + mm(x, b) for x, b in zip(p, bp)]
    n1 = [mm(x, ao) for x, ao in zip(p, a_off)]
    n2 = [mm(x, x) for x in n1]
    q = [x + mm(y, x) for x, y in zip(p, n2)]
    return [x - mm(y, x) for x, y in zip(q, n1)]


def _shift_selectors(rt):
    sel = np.zeros((SHORT_CONV * rt, rt), np.float32)
    sel_halo = np.zeros((SHORT_CONV * 8, CONV_HALO), np.float32)
    for d in range(SHORT_CONV):
        for t in range(d, rt):
            sel[d * rt + t, t - d] = 1.0
        for t in range(d):
            sel_halo[d * 8 + t, CONV_HALO + t - d] = 1.0
    return jnp.asarray(sel, BF16), jnp.asarray(sel_halo, BF16)


def _gdn_intra_kernel(qkv_ref, prev_ref, x_ref, sel_ref, selh_ref, wah_ref, wal_ref, wbh_ref, wbl_ref, cw_ref,
                      prm_ref, u_ref, w_ref, qd_ref, m2_ref, eg_ref, xs_ref, *, rt, tiles_per_seq):
    c = CHUNK
    nc = rt // c
    first = (pl.program_id(0) % tiles_per_seq) == 0
    edge = jnp.where(first, 0.0, _dot(selh_ref[...], prev_ref[...]))

    def move_rows(lo, hi):
        moved = _dot(sel_ref[...], qkv_ref[:, lo:hi])
        for d in range(SHORT_CONV):
            xs_ref[d, 0:8, lo:hi] = moved[d * rt:d * rt + 8] + edge[d * 8:(d + 1) * 8, lo:hi]
            xs_ref[d, 8:rt, lo:hi] = moved[d * rt + 8:(d + 1) * rt]

    for part in range(3):
        move_rows(part * GDN_WIDTH, part * GDN_WIDTH + 2 * HEAD_DIM)

    xh, xl = _split(x_ref[...])

    def proj3(wh_ref, wl_ref):
        wh = wh_ref[...]
        return _dot(xh, wh) + _dot(xl, wh) + _dot(xh, wl_ref[...])

    a_raw = proj3(wah_ref, wal_ref)
    b_raw = proj3(wbh_ref, wbl_ref)
    sp_in = a_raw + prm_ref[1:2, :]
    softplus = jnp.maximum(sp_in, 0.0) + jnp.log(1.0 + jnp.exp(-jnp.abs(sp_in)))
    g = -jnp.exp(prm_ref[0:1, :]) * softplus
    beta = _sigmoid(b_raw)

    cshift = c.bit_length() - 1
    r2 = lax.broadcasted_iota(I32, (rt, rt), 0)
    c2 = lax.broadcasted_iota(I32, (rt, rt), 1)
    same_chunk = jnp.right_shift(r2, cshift) == jnp.right_shift(c2, cshift)
    ltri = jnp.where(r2 >= c2, jnp.where(same_chunk, 1.0, 0.0), 0.0).astype(BF16)
    g_hi = g.astype(BF16)
    g_r = g - g_hi.astype(F32)
    g_mid = g_r.astype(BF16)
    g_lo = (g_r - g_mid.astype(F32)).astype(BF16)
    gc = _dot(ltri, g_hi) + _dot(ltri, g_mid) + _dot(ltri, g_lo)
    gct = gc.T
    egc = jnp.exp(gc)
    gend = jnp.concatenate(
        [jnp.broadcast_to(gc[ci * c + c - 1:ci * c + c, :], (c, LANES)) for ci in range(nc)], axis=0)
    kfac = jnp.exp(gend - gc)
    bege = beta * egc
    for ci in range(nc):
        last = ci * c + c - 1
        eg_ref[ci * GDN_HEADS:(ci + 1) * GDN_HEADS, :] = jnp.broadcast_to(
            jnp.exp(gct[0:GDN_HEADS, last:last + 1]), (GDN_HEADS, LANES))

    lane_t = lax.broadcasted_iota(I32, (rt, LANES), 1) < c
    lane_lo = lax.broadcasted_iota(I32, (c, LANES), 1) < c
    lcol = jnp.bitwise_and(lax.broadcasted_iota(I32, (c, LANES), 1), c - 1)
    rowi = lax.broadcasted_iota(I32, (c, LANES), 0)
    causal = rowi >= lcol
    strict = rowi > lcol

    def conv(base, h):
        lo, hi = base + h * HEAD_DIM, base + (h + 1) * HEAD_DIM
        acc = cw_ref[SHORT_CONV - 1:SHORT_CONV, lo:hi] * xs_ref[0, :, lo:hi]
        for j in range(SHORT_CONV - 1):
            acc = acc + cw_ref[j:j + 1, lo:hi] * xs_ref[SHORT_CONV - 1 - j, :, lo:hi]
        return _silu(acc)

    a_pairs = [[None] * (GDN_HEADS // 2) for _ in range(nc)]
    rhs_pairs = [[None] * (GDN_HEADS // 2) for _ in range(nc)]
    for p in range(GDN_HEADS // 2):
        if p + 1 < GDN_HEADS // 2:
            for part in range(3):
                lo = part * GDN_WIDTH + (p + 1) * 2 * HEAD_DIM
                move_rows(lo, lo + 2 * HEAD_DIM)
        ks, kbs, qs, kds, rhss = [], [], [], [], []
        for h in (2 * p, 2 * p + 1):
            q = conv(0, h)
            k = conv(GDN_WIDTH, h)
            v = conv(2 * GDN_WIDTH, h)
            q = q * lax.rsqrt(jnp.sum(q * q, axis=-1, keepdims=True) + EPS) * (HEAD_DIM ** -0.5)
            k = k * lax.rsqrt(jnp.sum(k * k, axis=-1, keepdims=True) + EPS)
            beta_h = beta[:, h:h + 1]
            qd_ref[:, h * HEAD_DIM:(h + 1) * HEAD_DIM] = (q * egc[:, h:h + 1]).astype(BF16)
            ks.append(k)
            kbs.append(k * beta_h)
            qs.append(q)
            kds.append(k * kfac[:, h:h + 1])
            rhss.append(jnp.concatenate([v * beta_h, k * bege[:, h:h + 1]], axis=1))
        h0, h1 = 2 * p, 2 * p + 1
        gch = jnp.where(lane_t, gc[:, h0:h0 + 1], gc[:, h1:h1 + 1])
        for ci in range(nc):
            rows = slice(ci * c, (ci + 1) * c)
            wk = jnp.concatenate([ks[0][rows], ks[1][rows]], axis=0).astype(BF16)
            lhs = jnp.concatenate([kbs[0][rows], qs[0][rows], kbs[1][rows], qs[1][rows]],
                                  axis=0).astype(BF16)
            out = _dot_nt(lhs, wk)
            gcrow = jnp.concatenate([gct[h0:h0 + 1, rows], gct[h1:h1 + 1, rows]], axis=1)
            diff = gch[rows] - gcrow
            decay = jnp.where(causal, jnp.exp(jnp.where(causal, diff, 0.0)), 0.0)
            a_pairs[ci][p] = jnp.where(strict, jnp.where(lane_lo, out[0:c], out[2 * c:3 * c]) * decay, 0.0)
            qk = jnp.where(lane_lo, out[c:2 * c], out[3 * c:4 * c]) * decay
            kdt = jnp.concatenate([kds[0][rows], kds[1][rows]], axis=0).T
            m0 = ci * 3 * c
            m2_ref[m0:m0 + c, p * LANES:(p + 1) * LANES] = qk.astype(BF16)
            m2_ref[m0 + c:m0 + 3 * c, p * LANES:(p + 1) * LANES] = kdt.astype(BF16)
            rhs_pairs[ci][p] = (rhss[0][rows], rhss[1][rows])

    zeros = jnp.zeros((c, 2 * HEAD_DIM), BF16)
    ngrp = GDN_HEADS // 4
    tls = _unit_lower_inverse4(
        [jnp.concatenate([a_pairs[ci][2 * grp], a_pairs[ci][2 * grp + 1]], axis=1)
         for ci in range(nc) for grp in range(ngrp)])
    for ci in range(nc):
        rows = slice(ci * c, (ci + 1) * c)
        for grp in range(ngrp):
            tl = tls[ci * ngrp + grp]
            for j in range(2):
                p = 2 * grp + j
                r0, r1 = rhs_pairs[ci][p]
                bd = jnp.concatenate([jnp.concatenate([r0.astype(BF16), zeros], axis=1),
                                      jnp.concatenate([zeros, r1.astype(BF16)], axis=1)], axis=0)
                sol = _dot(tl[:, j * LANES:(j + 1) * LANES].astype(BF16), bd)
                for i in range(2):
                    h = 2 * p + i
                    lo, hi = h * HEAD_DIM, (h + 1) * HEAD_DIM
                    u_ref[rows, lo:hi] = sol[:, 2 * i * HEAD_DIM:(2 * i + 1) * HEAD_DIM]
                    w_ref[rows, lo:hi] = sol[:, (2 * i + 1) * HEAD_DIM:(2 * i + 2) * HEAD_DIM].astype(BF16)


def _gdn_intra(proj, x, wa_hi, wa_lo, wb_hi, wb_lo, conv_w, prm, seq, rt):
    t = x.shape[0]
    nc = rt // CHUNK
    kern = functools.partial(_gdn_intra_kernel, rt=rt, tiles_per_seq=seq // rt)
    full = lambda i: (0, 0)
    tile = lambda i: (i, 0)
    sel, sel_halo = _shift_selectors(rt)
    return pl.pallas_call(
        kern,
        grid=(t // rt,),
        in_specs=[
            pl.BlockSpec((rt, 3 * GDN_WIDTH), tile),
            pl.BlockSpec((CONV_HALO, 3 * GDN_WIDTH),
                         lambda i: (jnp.maximum(i * (rt // CONV_HALO) - 1, 0), 0)),
            pl.BlockSpec((rt, D_MODEL), tile),
            pl.BlockSpec(sel.shape, full),
            pl.BlockSpec(sel_halo.shape, full),
            pl.BlockSpec((D_MODEL, LANES), full),
            pl.BlockSpec((D_MODEL, LANES), full),
            pl.BlockSpec((D_MODEL, LANES), full),
            pl.BlockSpec((D_MODEL, LANES), full),
            pl.BlockSpec((SHORT_CONV, 3 * GDN_WIDTH), full),
            pl.BlockSpec((8, LANES), full),
        ],
        out_specs=[pl.BlockSpec((rt, GDN_WIDTH), tile),
                   pl.BlockSpec((rt, GDN_WIDTH), tile),
                   pl.BlockSpec((rt, GDN_WIDTH), tile),
                   pl.BlockSpec((nc * 3 * CHUNK, GDN_HEADS // 2 * LANES), tile),
                   pl.BlockSpec((nc * GDN_HEADS, LANES), tile)],
        out_shape=[jax.ShapeDtypeStruct((t, GDN_WIDTH), F32),
                   jax.ShapeDtypeStruct((t, GDN_WIDTH), BF16),
                   jax.ShapeDtypeStruct((t, GDN_WIDTH), BF16),
                   jax.ShapeDtypeStruct((t // CHUNK * 3 * CHUNK, GDN_HEADS // 2 * LANES), BF16),
                   jax.ShapeDtypeStruct((t // CHUNK * GDN_HEADS, LANES), F32)],
        scratch_shapes=[pltpu.VMEM((SHORT_CONV, rt, 3 * GDN_WIDTH), F32)],
        compiler_params=_params("parallel"),
        name="gdn_intra",
    )(proj, proj, x, sel, sel_halo, wa_hi, wa_lo, wb_hi, wb_lo, conv_w, prm)


def _gdn_scan_kernel(u_ref, w_ref, qd_ref, m2_ref, eg_ref, z_ref, ong_ref, o_ref, s_ref, *, nck):
    c = CHUNK
    nseq = u_ref.shape[0]

    @pl.when(pl.program_id(1) == 0)
    def _():
        s_ref[...] = jnp.zeros_like(s_ref)

    ong = ong_ref[...]
    zeros = jnp.zeros((c, HEAD_DIM), BF16)
    span = lambda h: slice(h * HEAD_DIM, (h + 1) * HEAD_DIM)
    chains = [(b, h) for b in range(nseq) for h in range(GDN_HEADS)]
    states = {bh: s_ref[bh[0], bh[1]] for bh in chains}
    for ci in range(nck):
        rows = slice(ci * c, (ci + 1) * c)
        rs = {(b, h): _dot(jnp.concatenate([w_ref[b, rows, span(h)], qd_ref[b, rows, span(h)]], axis=0),
                           states[b, h].astype(BF16)) for b, h in chains}
        v_new = {(b, h): (u_ref[b, rows, span(h)] - rs[b, h][:c]).astype(BF16) for b, h in chains}
        r2s = {}
        for b in range(nseq):
            for p in range(GDN_HEADS // 2):
                bd = jnp.concatenate([jnp.concatenate([v_new[b, 2 * p], zeros], axis=1),
                                      jnp.concatenate([zeros, v_new[b, 2 * p + 1]], axis=1)], axis=0)
                r2s[b, p] = _dot(m2_ref[b, ci * 3 * c:(ci + 1) * 3 * c, p * LANES:(p + 1) * LANES], bd)
        for b, h in chains:
            half = span(h % 2)
            r2 = r2s[b, h // 2]
            decay = eg_ref[b, ci * GDN_HEADS + h:ci * GDN_HEADS + h + 1, :]
            states[b, h] = states[b, h] * decay + r2[c:, half]
            o = rs[b, h][c:] + r2[:c, half]
            o = o * lax.rsqrt(jnp.mean(o * o, axis=-1, keepdims=True) + EPS) * ong
            o = o * _silu(z_ref[b, rows, span(h)].astype(F32))
            o_ref[b, rows, span(h)] = o.astype(o_ref.dtype)
    for b, h in chains:
        s_ref[b, h] = states[b, h]


def _gdn_scan(u, w, qd, m2, eg, proj, ong, batch, seq):
    t = batch * seq
    nck = min(SCAN_CHUNKS, seq // CHUNK)
    c = nck * CHUNK
    nch = seq // c
    ns = SCAN_SEQS if batch % SCAN_SEQS == 0 else 1
    groups = batch // ns
    split = lambda a: a.reshape(ns, a.shape[0] // ns, a.shape[1])
    blk = lambda b, n: (0, b * nch + n, 0)
    out = pl.pallas_call(
        functools.partial(_gdn_scan_kernel, nck=nck),
        grid=(groups, nch),
        in_specs=[
            pl.BlockSpec((ns, c, GDN_WIDTH), blk),
            pl.BlockSpec((ns, c, GDN_WIDTH), blk),
            pl.BlockSpec((ns, c, GDN_WIDTH), blk),
            pl.BlockSpec((ns, 3 * c, GDN_HEADS // 2 * LANES), blk),
            pl.BlockSpec((ns, nck * GDN_HEADS, LANES), blk),
            pl.BlockSpec((ns, c, GDN_WIDTH), lambda b, n: (0, b * nch + n, 3)),
            pl.BlockSpec((1, HEAD_DIM), lambda b, n: (0, 0)),
        ],
        out_specs=pl.BlockSpec((ns, c, GDN_WIDTH), blk),
        out_shape=jax.ShapeDtypeStruct((ns, t // ns, GDN_WIDTH), BF16),
        scratch_shapes=[pltpu.VMEM((ns, GDN_HEADS, HEAD_DIM, HEAD_DIM), F32)],
        compiler_params=_params("parallel", "arbitrary"),
        name="gdn_scan",
    )(split(u), split(w), split(qd), split(m2), split(eg), split(proj), ong)
    return out.reshape(t, GDN_WIDTH)


def _pack_bf16_pairs(y):
    n = y.shape[1] // 2
    yb = y.astype(BF16).astype(F32)
    lo = lax.bitcast_convert_type(yb[:, :n], U32)
    hi = lax.bitcast_convert_type(yb[:, n:], U32)
    return jnp.bitwise_or(jnp.right_shift(lo, 16), jnp.bitwise_and(hi, HI_MASK))


def _unpack_bf16_pairs(w):
    lo = lax.bitcast_convert_type(jnp.left_shift(w, 16), F32)
    hi = lax.bitcast_convert_type(jnp.bitwise_and(w, HI_MASK), F32)
    return lo, hi


def _mixer_kernel(glu_ref, ga_ref, gb_ref, o_ref, x_ref, woa_ref, wob_ref, wo_ref, dww_ref, vec_ref,
                  x1_ref, x1b_ref, x1p_ref, ubuf_ref, sh_ref, conv_ref, *, ts, rc):
    halo = CONF_HALO

    @pl.when(pl.program_id(1) == 0)
    def _():
        ubuf_ref[0:halo, :] = jnp.zeros((halo, CONF_CH), F32)

    @pl.when(pl.program_id(1) != 0)
    def _():
        ubuf_ref[0:halo, :] = ubuf_ref[ts:ts + halo, :]

    glu_a = glu_ref[:, :CONF_CH].astype(F32)
    glu_b = glu_ref[:, CONF_CH:].astype(F32)
    ubuf_ref[halo:halo + ts, :] = glu_a * _sigmoid(glu_b)

    dw_b = vec_ref[0:1, :]
    cln_g = vec_ref[1:2, :]
    cln_b = vec_ref[2:3, :]
    ln1_g = vec_ref[3:4, :]
    ln1_b = vec_ref[4:5, :]

    span = ts + halo - 8
    for s in range(1, 8):
        sh_ref[s - 1] = ubuf_ref[s:s + span, :]

    def tap_rows(j, r0):
        o = halo - (CONF_KERNEL - 1) + j
        q, s = o // 8, o % 8
        if s == 0:
            return ubuf_ref[r0 + o:r0 + o + rc, :]
        return sh_ref[s - 1, r0 + 8 * q:r0 + 8 * q + rc, :]

    hs = ts // 2
    for h0 in range(0, ts, hs):
        rows = slice(h0, h0 + hs)
        gated_a = _sigmoid(ga_ref[rows, :].astype(F32)) * _dot(o_ref[rows, :], woa_ref[...])
        for r0 in range(h0, h0 + hs, rc):
            tap_w = lambda j: jnp.concatenate([dww_ref[j]] * (rc // 8), axis=0)
            acc = tap_w(0) * tap_rows(0, r0)
            for j in range(1, CONF_KERNEL):
                acc = acc + tap_w(j) * tap_rows(j, r0)
            conv_ref[r0:r0 + rc, :] = acc
        uc = _silu(_layer_norm(conv_ref[rows, :] + dw_b, cln_g, cln_b))
        branch_b = _dot(uc.astype(BF16), wob_ref[...])
        hmix = gated_a + _sigmoid(gb_ref[rows, :].astype(F32)) * branch_b
        mix = _dot(hmix.astype(BF16), wo_ref[...])
        x1 = _layer_norm(DEEPNORM_ALPHA * x_ref[rows, :] + mix, ln1_g, ln1_b)
        x1_ref[rows, :] = x1
        x1b_ref[rows, :] = x1.astype(BF16)
        x1p_ref[rows, :] = _pack_bf16_pairs(x1)


def _mixer(proj, o_gdn, x, woa, wob, wo, dww, vec, batch, seq, ts):
    t = batch * seq
    nt = seq // ts
    rows = lambda b, n: b * nt + n
    full = lambda b, n: (0, 0)
    kern = functools.partial(_mixer_kernel, ts=ts, rc=32)
    return pl.pallas_call(
        kern,
        grid=(batch, nt),
        in_specs=[
            pl.BlockSpec((ts, 2 * CONF_CH), lambda b, n: (rows(b, n), 2)),
            pl.BlockSpec((ts, D_MODEL), lambda b, n: (rows(b, n), 6)),
            pl.BlockSpec((ts, D_MODEL), lambda b, n: (rows(b, n), 7)),
            pl.BlockSpec((ts, GDN_WIDTH), lambda b, n: (rows(b, n), 0)),
            pl.BlockSpec((ts, D_MODEL), lambda b, n: (rows(b, n), 0)),
            pl.BlockSpec((GDN_WIDTH, D_MODEL), full),
            pl.BlockSpec((CONF_CH, D_MODEL), full),
            pl.BlockSpec((D_MODEL, D_MODEL), full),
            pl.BlockSpec((CONF_KERNEL, 8, CONF_CH), lambda b, n: (0, 0, 0)),
            pl.BlockSpec((8, D_MODEL), full),
        ],
        out_specs=[pl.BlockSpec((ts, D_MODEL), lambda b, n: (rows(b, n), 0)),
                   pl.BlockSpec((ts, D_MODEL), lambda b, n: (rows(b, n), 0)),
                   pl.BlockSpec((ts, D_MODEL // 2), lambda b, n: (rows(b, n), 0))],
        out_shape=[jax.ShapeDtypeStruct((t, D_MODEL), F32),
                   jax.ShapeDtypeStruct((t, D_MODEL), BF16),
                   jax.ShapeDtypeStruct((t, D_MODEL // 2), U32)],
        scratch_shapes=[pltpu.VMEM((CONF_HALO + ts, CONF_CH), F32),
                        pltpu.VMEM((7, ts + CONF_HALO - 8, CONF_CH), F32),
                        pltpu.VMEM((ts, CONF_CH), F32)],
        compiler_params=_params("parallel", "arbitrary"),
        name="mixer",
    )(proj, proj, proj, o_gdn, x, woa, wob, wo, dww, vec)


def _route_kernel(x_ref, wrh_ref, wrl_ref, bias_ref, eidx_ref, wts_ref, rank_ref, cnt_ref, carry_ref, *, tt):
    @pl.when(pl.program_id(0) == 0)
    def _():
        carry_ref[...] = jnp.zeros_like(carry_ref)

    xh, xl = _split(x_ref[...])
    wrh = wrh_ref[...]
    logits = _dot_nt(wrh, xh) + _dot_nt(wrh, xl) + _dot_nt(wrl_ref[...], xh)
    s = _sigmoid(logits)
    biased = s + bias_ref[...]

    sub = lax.broadcasted_iota(I32, (GROUP_SIZE, tt), 0)
    groups = [biased[g * GROUP_SIZE:(g + 1) * GROUP_SIZE, :] for g in range(N_GROUPS)]
    gs = []
    for bg in groups:
        m1 = jnp.max(bg, axis=0, keepdims=True)
        first = jnp.min(jnp.where(bg == m1, sub, GROUP_SIZE), axis=0, keepdims=True)
        m2 = jnp.max(jnp.where(sub == first, -jnp.inf, bg), axis=0, keepdims=True)
        gs.append(m1 + m2)

    masked_parts = []
    for g in range(N_GROUPS):
        beaten = jnp.zeros((1, tt), I32)
        for o in range(N_GROUPS):
            if o == g:
                continue
            wins = (gs[o] >= gs[g]) if o < g else (gs[o] > gs[g])
            beaten = beaten + wins.astype(I32)
        keep = jnp.broadcast_to(beaten < TOPK_GROUPS, (GROUP_SIZE, tt))
        masked_parts.append(jnp.where(keep, groups[g], -jnp.inf))
    masked = jnp.concatenate(masked_parts, axis=0)

    eiota = lax.broadcasted_iota(I32, (N_EXPERTS, tt), 0)
    sel_all = jnp.zeros((N_EXPERTS, tt), F32)
    picks = []
    for _ in range(TOP_K):
        m = jnp.max(masked, axis=0, keepdims=True)
        idx = jnp.min(jnp.where(masked == m, eiota, N_EXPERTS), axis=0, keepdims=True)
        onehot = eiota == idx
        picks.append((idx, onehot))
        sel_all = jnp.where(onehot, 1.0, sel_all)
        masked = jnp.where(onehot, -jnp.inf, masked)

    tr = lax.broadcasted_iota(I32, (tt, tt), 0)
    tc = lax.broadcasted_iota(I32, (tt, tt), 1)
    before = (tr < tc).astype(BF16)
    sel_b = sel_all.astype(BF16)
    carry = carry_ref[...]
    rank_all = _dot(sel_b, before) + carry[:, 0:1]
    carry_new = carry + _dot(sel_b, jnp.ones((tt, LANES), BF16))
    carry_ref[...] = carry_new
    cnt_ref[...] = carry_new

    s_sel = [jnp.sum(jnp.where(oh, s, 0.0), axis=0, keepdims=True) for _, oh in picks]
    total = s_sel[0]
    for v in s_sel[1:]:
        total = total + v
    for k, (idx, oh) in enumerate(picks):
        eidx_ref[k:k + 1, :] = idx
        wts_ref[k:k + 1, :] = s_sel[k] / total * ROUTED_SCALE
        rank_ref[k:k + 1, :] = jnp.sum(jnp.where(oh, rank_all, 0.0), axis=0, keepdims=True).astype(I32)


def _route(x1, wr_hi, wr_lo, bias, tt):
    t = x1.shape[0]
    kern = functools.partial(_route_kernel, tt=tt)
    return pl.pallas_call(
        kern,
        grid=(t // tt,),
        in_specs=[pl.BlockSpec((tt, D_MODEL), lambda i: (i, 0)),
                  pl.BlockSpec((N_EXPERTS, D_MODEL), lambda i: (0, 0)),
                  pl.BlockSpec((N_EXPERTS, D_MODEL), lambda i: (0, 0)),
                  pl.BlockSpec((N_EXPERTS, tt), lambda i: (0, 0))],
        out_specs=[pl.BlockSpec((TOP_K, tt), lambda i: (0, i)),
                   pl.BlockSpec((TOP_K, tt), lambda i: (0, i)),
                   pl.BlockSpec((TOP_K, tt), lambda i: (0, i)),
                   pl.BlockSpec((N_EXPERTS, LANES), lambda i: (0, 0))],
        out_shape=[jax.ShapeDtypeStruct((TOP_K, t), I32),
                   jax.ShapeDtypeStruct((TOP_K, t), F32),
                   jax.ShapeDtypeStruct((TOP_K, t), I32),
                   jax.ShapeDtypeStruct((N_EXPERTS, LANES), F32)],
        scratch_shapes=[pltpu.VMEM((N_EXPERTS, LANES), F32)],
        compiler_params=_params("arbitrary"),
        name="route",
    )(x1, wr_hi, wr_lo, bias)


def _plan_kernel(cnt_ref, pstart_ref, plan_ref):
    e, nb = N_EXPERTS, plan_ref.shape[1]
    counts = cnt_ref[...]
    nblk = jnp.floor((counts + (ROW_BLOCK - 1)) * (1.0 / ROW_BLOCK))
    hi = jnp.floor(nblk * (1.0 / 256.0))
    lo = nblk - 256.0 * hi
    r = lax.broadcasted_iota(I32, (e, e), 0)
    c = lax.broadcasted_iota(I32, (e, e), 1)
    ltri = (r >= c).astype(BF16)
    bend = 256.0 * _dot(ltri, hi.astype(BF16)) + _dot(ltri, lo.astype(BF16))
    pend = bend * ROW_BLOCK
    pstart = pend - nblk * ROW_BLOCK
    pstart_ref[...] = pstart.astype(I32)

    bs = (lax.broadcasted_iota(I32, (e, nb), 1) * ROW_BLOCK).astype(F32)
    pend_b = jnp.broadcast_to(pend[:, 0:1], (e, nb))
    pstart_b = jnp.broadcast_to(pstart[:, 0:1], (e, nb))
    used_b = jnp.broadcast_to((pstart + counts)[:, 0:1], (e, nb))
    owner = jnp.sum(jnp.where(pend_b <= bs, 1.0, 0.0), axis=0, keepdims=True)
    inside = jnp.where(pstart_b <= bs, jnp.where(bs < pend_b, 1.0, 0.0), 0.0)
    real = jnp.sum(inside * jnp.clip(used_b - bs, 0.0, float(ROW_BLOCK)), axis=0, keepdims=True)
    plan_ref[0:1, :] = jnp.minimum(owner, float(e - 1)).astype(I32)
    plan_ref[1:2, :] = real.astype(I32)
    plan_ref[2:8, :] = jnp.zeros((6, nb), I32)


def _plan(cnt, n_blocks):
    nb = -(-n_blocks // LANES) * LANES
    pstart, plan = pl.pallas_call(
        _plan_kernel,
        out_shape=[jax.ShapeDtypeStruct((N_EXPERTS, LANES), I32),
                   jax.ShapeDtypeStruct((8, nb), I32)],
        name="plan",
    )(cnt)
    return pstart[:, 0], plan[0, :n_blocks], plan[1, :n_blocks]


def _dest_kernel(eidx_ref, rank_ref, pstart_ref, dest_ref):
    eidx = eidx_ref[...]
    acc = rank_ref[...]
    for e in range(N_EXPERTS):
        acc = acc + jnp.where(eidx == e, pstart_ref[e], 0)
    dest_ref[...] = acc


def _dest(eidx_t, rank_t, pstart, tt):
    t = eidx_t.shape[1]
    return pl.pallas_call(
        _dest_kernel,
        grid=(t // tt,),
        in_specs=[pl.BlockSpec((TOP_K, tt), lambda i: (0, i)),
                  pl.BlockSpec((TOP_K, tt), lambda i: (0, i)),
                  pl.BlockSpec(memory_space=pltpu.SMEM)],
        out_specs=pl.BlockSpec((TOP_K, tt), lambda i: (0, i)),
        out_shape=jax.ShapeDtypeStruct((TOP_K, t), I32),
        compiler_params=_params("parallel"),
        name="dest",
    )(eidx_t, rank_t, pstart)


def _dispatch(dest_kt, x1p, n_rows):
    t, width = x1p.shape
    info = plsc.get_sparse_core_info()
    nc, nw = info.num_cores, info.num_cores * info.num_subcores
    chunk = SC_SCATTER_ROWS
    per_w = t // nw
    n_chunks = per_w // chunk
    assert per_w % chunk == 0
    idx = dest_kt.reshape(TOP_K, nw, n_chunks, chunk).transpose(1, 2, 0, 3).reshape(nw, n_chunks * TOP_K, chunk)
    mesh = plsc.VectorSubcoreMesh(core_axis_name="c", subcore_axis_name="s")

    @functools.partial(
        pl.kernel, mesh=mesh, name="dispatch",
        out_type=jax.ShapeDtypeStruct((n_rows, width), x1p.dtype),
        scratch_types=[pltpu.VMEM((n_chunks * TOP_K, chunk), I32),
                       pltpu.VMEM((chunk, width), x1p.dtype),
                       pltpu.SemaphoreType.DMA])
    def scatter(x_hbm, idx_hbm, xs_hbm, idx_v, rows_v, sem):
        wid = lax.axis_index("s") * nc + lax.axis_index("c")
        base = wid * per_w
        pltpu.sync_copy(idx_hbm.at[wid], idx_v)
        for j in range(n_chunks):
            pltpu.sync_copy(x_hbm.at[pl.ds(base + j * chunk, chunk)], rows_v)
            copies = [pltpu.make_async_copy(rows_v, xs_hbm.at[idx_v.at[j * TOP_K + k]], sem)
                      for k in range(TOP_K)]
            for cp in copies:
                cp.start()
            for cp in copies:
                cp.wait()

    return scatter(x1p, idx)


def _gather_rows(table, idx):
    n = idx.shape[0]
    width = table.shape[1]
    info = plsc.get_sparse_core_info()
    nc, nw = info.num_cores, info.num_cores * info.num_subcores
    chunk = SC_GATHER_ROWS
    per_w = n // nw
    n_chunks = per_w // chunk
    assert per_w % (2 * chunk) == 0
    mesh = plsc.VectorSubcoreMesh(core_axis_name="c", subcore_axis_name="s")

    @functools.partial(
        pl.kernel, mesh=mesh, name="gather_rows",
        out_type=jax.ShapeDtypeStruct((n, width), table.dtype),
        scratch_types=[pltpu.VMEM((n_chunks, chunk), I32),
                       pltpu.VMEM((2, chunk, width), table.dtype),
                       pltpu.SemaphoreType.DMA((2,)),
                       pltpu.SemaphoreType.DMA((2,))])
    def gather(table_hbm, idx_hbm, out_hbm, idx_v, rows_v, gsem, osem):
        wid = lax.axis_index("s") * nc + lax.axis_index("c")
        base = wid * per_w
        pltpu.sync_copy(idx_hbm.at[wid], idx_v)

        def fetch(j, b):
            return pltpu.make_async_copy(table_hbm.at[idx_v.at[j]], rows_v.at[b], gsem.at[b])

        def put(j, b):
            return pltpu.make_async_copy(rows_v.at[b], out_hbm.at[pl.ds(base + j * chunk, chunk)], osem.at[b])

        fetch(0, 0).start()

        @pl.loop(0, n_chunks, step=2)
        def _(j0):
            for b in range(2):
                j = j0 + b
                fetch(j, b).wait()

                @pl.when(j + 1 < n_chunks)
                def _():
                    @pl.when(j >= 1)
                    def _():
                        put(j - 1, 1 - b).wait()

                    fetch(j + 1, 1 - b).start()

                put(j, b).start()

        put(n_chunks - 2, 0).wait()
        put(n_chunks - 1, 1).wait()

    return gather(table, idx.reshape(nw, n_chunks, chunk))


def _xs_copy(xs_hbm, xbuf, isem, j, slot):
    return pltpu.make_async_copy(xs_hbm.at[pl.ds(j * ROW_BLOCK, ROW_BLOCK)], xbuf.at[slot], isem.at[slot])


def _ys_copy(ybuf, ys_hbm, osem, j, slot):
    return pltpu.make_async_copy(ybuf.at[slot], ys_hbm.at[pl.ds(j * ROW_BLOCK, ROW_BLOCK)], osem.at[slot])


def _experts_kernel(be_ref, nv_ref, xs_hbm, wg_ref, wu_ref, wd_ref, ys_hbm,
                    xbuf, ybuf, wgu_s, wd_s, cur_ref, isem, osem):
    i = pl.program_id(0)
    n_valid = nv_ref[i]
    half = EXPERT_SUB
    slot = lax.rem(i, EXPERT_IN_SLOTS)
    oslot = lax.rem(i, 2)

    @pl.when(i == 0)
    def _():
        cur_ref[0] = -1
        for j in range(2):
            @pl.when(nv_ref[j] > 0)
            def _():
                _xs_copy(xs_hbm, xbuf, isem, j, j).start()

    @pl.when(nv_ref[i + 2] > 0)
    def _():
        _xs_copy(xs_hbm, xbuf, isem, i + 2, lax.rem(i + 2, EXPERT_IN_SLOTS)).start()

    @pl.when((i >= 2) & (nv_ref[jnp.maximum(i - 2, 0)] > 0))
    def _():
        _ys_copy(ybuf, ys_hbm, osem, i - 2, oslot).wait()

    @pl.when((n_valid > 0) & (cur_ref[0] != be_ref[i]))
    def _():
        wgu_s[:, :EXPERT_FF] = wg_ref[...].astype(BF16)
        wgu_s[:, EXPERT_FF:] = wu_ref[...].astype(BF16)
        wd_s[...] = wd_ref[...].astype(BF16)
        cur_ref[0] = be_ref[i]

    def rows_bf16(r0):
        valid = lax.broadcasted_iota(I32, (half, xbuf.shape[2]), 0) + r0 < n_valid
        lo, hi = _unpack_bf16_pairs(jnp.where(valid, xbuf[slot, r0:r0 + half, :], jnp.zeros((), U32)))
        return jnp.concatenate([lo.astype(BF16), hi.astype(BF16)], axis=1)

    def hidden(gu):
        return (_silu(gu[:, :EXPERT_FF]) * gu[:, EXPERT_FF:]).astype(BF16)

    @pl.when(n_valid > 0)
    def _():
        _xs_copy(xs_hbm, xbuf, isem, i, slot).wait()

    zeros = jnp.zeros((half, ybuf.shape[2]), ybuf.dtype)
    for r0 in range(0, ROW_BLOCK, 2 * half):
        r1 = r0 + half

        @pl.when(n_valid > r1)
        def _():
            xa, xb = rows_bf16(r0), rows_bf16(r1)
            gua = _dot(xa, wgu_s[...])
            gub = _dot(xb, wgu_s[...])
            ya = _dot(hidden(gua), wd_s[...])
            yb = _dot(hidden(gub), wd_s[...])
            ybuf[oslot, r0:r1, :] = _pack_bf16_pairs(ya)
            ybuf[oslot, r1:r1 + half, :] = _pack_bf16_pairs(yb)

        @pl.when((n_valid > r0) & (n_valid <= r1))
        def _():
            ya = _dot(hidden(_dot(rows_bf16(r0), wgu_s[...])), wd_s[...])
            ybuf[oslot, r0:r1, :] = _pack_bf16_pairs(ya)
            ybuf[oslot, r1:r1 + half, :] = zeros

        @pl.when((n_valid > 0) & (n_valid <= r0))
        def _():
            ybuf[oslot, r0:r1, :] = zeros
            ybuf[oslot, r1:r1 + half, :] = zeros

    @pl.when(n_valid > 0)
    def _():
        _ys_copy(ybuf, ys_hbm, osem, i, oslot).start()


def _experts(block_e, n_valid, xs, wg, wu, wd, layer):
    n_rows, width = xs.shape
    steps = n_rows // ROW_BLOCK + 2
    be = jnp.concatenate([block_e, jnp.full((2,), N_EXPERTS - 1, I32)])
    nv = jnp.concatenate([n_valid, jnp.zeros((4,), I32)])
    grid_spec = pltpu.PrefetchScalarGridSpec(
        num_scalar_prefetch=2,
        grid=(steps,),
        in_specs=[pl.BlockSpec(memory_space=pl.ANY),
                  pl.BlockSpec((None, None, D_MODEL, EXPERT_FF), lambda i, be, nv: (layer, be[i], 0, 0)),
                  pl.BlockSpec((None, None, D_MODEL, EXPERT_FF), lambda i, be, nv: (layer, be[i], 0, 0)),
                  pl.BlockSpec((None, None, EXPERT_FF, D_MODEL), lambda i, be, nv: (layer, be[i], 0, 0))],
        out_specs=pl.BlockSpec(memory_space=pl.ANY),
        scratch_shapes=[pltpu.VMEM((EXPERT_IN_SLOTS, ROW_BLOCK, width), xs.dtype),
                        pltpu.VMEM((2, ROW_BLOCK, width), xs.dtype),
                        pltpu.VMEM((D_MODEL, 2 * EXPERT_FF), BF16),
                        pltpu.VMEM((EXPERT_FF, D_MODEL), BF16),
                        pltpu.SMEM((1,), I32),
                        pltpu.SemaphoreType.DMA((EXPERT_IN_SLOTS,)),
                        pltpu.SemaphoreType.DMA((2,))],
    )
    return pl.pallas_call(
        _experts_kernel,
        grid_spec=grid_spec,
        out_shape=jax.ShapeDtypeStruct((n_rows, width), xs.dtype),
        compiler_params=_params("arbitrary"),
        name="experts",
    )(be, nv, xs, wg, wu, wd)


def _shared_kernel(x1b_ref, wsg_ref, wsu_ref, wsd_ref, anchor_ref, sh_ref):
    del anchor_ref
    xb = x1b_ref[...]
    hid = (_silu(_dot(xb, wsg_ref[...])) * _dot(xb, wsu_ref[...])).astype(BF16)
    sh_ref[...] = _dot(hid, wsd_ref[...]).astype(sh_ref.dtype)


def _shared(x1b, wsg, wsu, wsd, anchor, tt, row0, nrows):
    full = lambda i: (0, 0)
    first = row0 // tt
    return pl.pallas_call(
        _shared_kernel,
        grid=(nrows // tt,),
        in_specs=[pl.BlockSpec((tt, D_MODEL), lambda i: (first + i, 0)),
                  pl.BlockSpec((D_MODEL, SHARED_FF), full),
                  pl.BlockSpec((D_MODEL, SHARED_FF), full),
                  pl.BlockSpec((SHARED_FF, D_MODEL), full),
                  pl.BlockSpec(memory_space=pl.ANY)],
        out_specs=pl.BlockSpec((tt, D_MODEL), lambda i: (i, 0)),
        out_shape=jax.ShapeDtypeStruct((nrows, D_MODEL), BF16),
        compiler_params=_params("parallel"),
        name="shared",
    )(x1b, wsg, wsu, wsd, anchor)


def _combine_kernel(yg_ref, w_ref, x1_ref, sha_ref, shb_ref, vec_ref, x2_ref, *, half_steps):
    in_first = pl.program_id(0) < half_steps
    shared = jnp.where(in_first, sha_ref[...], shb_ref[...]).astype(F32)
    w = w_ref[...]
    half = D_MODEL // 2
    acc_lo, acc_hi = shared[:, :half], shared[:, half:]
    for k in range(TOP_K):
        lo, hi = _unpack_bf16_pairs(yg_ref[k])
        acc_lo = acc_lo + w[:, k:k + 1] * lo
        acc_hi = acc_hi + w[:, k:k + 1] * hi
    acc = jnp.concatenate([acc_lo, acc_hi], axis=1)
    x2 = _layer_norm(DEEPNORM_ALPHA * x1_ref[...] + acc, vec_ref[0:1, :], vec_ref[1:2, :])
    x2_ref[...] = x2


def _combine(yg, w_tok, x1, shared_a, shared_b, vec, tt):
    t = x1.shape[0]
    half_steps = shared_a.shape[0] // tt
    full = lambda i: (0, 0)
    return pl.pallas_call(
        functools.partial(_combine_kernel, half_steps=half_steps),
        grid=(t // tt,),
        in_specs=[pl.BlockSpec((TOP_K, tt, yg.shape[2]), lambda i: (0, i, 0)),
                  pl.BlockSpec((tt, TOP_K), lambda i: (i, 0)),
                  pl.BlockSpec((tt, D_MODEL), lambda i: (i, 0)),
                  pl.BlockSpec((tt, D_MODEL), lambda i: (jnp.minimum(i, half_steps - 1), 0)),
                  pl.BlockSpec((tt, D_MODEL), lambda i: (jnp.maximum(i - half_steps, 0), 0)),
                  pl.BlockSpec((8, D_MODEL), full)],
        out_specs=pl.BlockSpec((tt, D_MODEL), lambda i: (i, 0)),
        out_shape=jax.ShapeDtypeStruct((t, D_MODEL), F32),
        compiler_params=_params("parallel"),
        name="combine",
    )(yg, w_tok, x1, shared_a, shared_b, vec)


def _pad_rows(a, rows):
    return jnp.zeros((rows, a.shape[-1]), F32).at[:a.shape[0]].set(a.astype(F32))


def _layer(x, p, stacked, layer, batch, seq):
    t = batch * seq
    w_main, wa_hi, wa_lo, wb_hi, wb_lo = _regroup(jnp.swapaxes(stacked["w_in"], 1, 2), layer, 512)

    tm = min(2048, t)
    proj = _proj(x, w_main, tm, 2048)

    prm = jnp.zeros((8, LANES), F32)
    prm = prm.at[0, :GDN_HEADS].set(p["a_log"]).at[1, :GDN_HEADS].set(p["dt_bias"])
    ong = p["o_norm_g"].reshape(1, HEAD_DIM).astype(F32)
    u, w, qd, m2, eg = _gdn_intra(proj, x, wa_hi, wa_lo, wb_hi, wb_lo, p["conv_qkv"].astype(F32), prm,
                                  seq, min(GDN_TILE, seq))
    o_gdn = _gdn_scan(u, w, qd, m2, eg, proj, ong, batch, seq)

    ts = min(256, seq)
    dww = jnp.broadcast_to(p["dw_w"].astype(F32)[:, None, :], (CONF_KERNEL, 8, CONF_CH))
    vec = _pad_rows(jnp.stack([p["dw_b"], p["cln_g"], p["cln_b"], p["ln1_g"], p["ln1_b"]]), 8)
    x1, x1b, x1p = _mixer(proj, o_gdn, x, p["w_oa"].astype(BF16), p["w_ob"].astype(BF16),
                          p["w_o"].astype(BF16), dww, vec, batch, seq, ts)

    tt_r = min(512, t)
    wr_t = p["w_router"].T.astype(F32)
    wr_hi = wr_t.astype(BF16)
    wr_lo = (wr_t - wr_hi.astype(F32)).astype(BF16)
    bias = jnp.broadcast_to(p["router_bias"].astype(F32)[:, None], (N_EXPERTS, tt_r))
    eidx_t, wts_t, rank_t, cnt = _route(x1, wr_hi, wr_lo, bias, tt_r)

    n_blocks = -(-(t * TOP_K + N_EXPERTS * (ROW_BLOCK - 1)) // ROW_BLOCK)
    n_rows = n_blocks * ROW_BLOCK
    pstart, block_e, n_valid = _plan(cnt, n_blocks)

    dest = _dest(eidx_t, rank_t, pstart, min(2048, t))
    sh_w = (p["w_sh_gate"].astype(BF16), p["w_sh_up"].astype(BF16), p["w_sh_down"].astype(BF16))
    tt_s = min(512, t // 2)
    xs = _dispatch(dest, x1p, n_rows)
    shared_a = _shared(x1b, *sh_w, dest, tt_s, 0, t // 2)
    ys = _experts(block_e, n_valid, xs, stacked["w_gate_e"], stacked["w_up_e"], stacked["w_down_e"], layer)
    yg = _gather_rows(ys, dest.reshape(TOP_K * t)).reshape(TOP_K, t, ys.shape[1])
    shared_b = _shared(x1b, *sh_w, ys, tt_s, t // 2, t // 2)
    vec2 = _pad_rows(jnp.stack([p["ln2_g"], p["ln2_b"]]), 8)
    return _combine(yg, wts_t.T, x1, shared_a, shared_b, vec2, min(256, t // 2))


_PARAM_NAMES = ("w_in", "conv_qkv", "a_log", "dt_bias", "o_norm_g", "w_oa", "dw_w", "dw_b", "cln_g",
                "cln_b", "w_ob", "w_o", "ln1_g", "ln1_b", "w_router", "router_bias", "w_gate_e",
                "w_up_e", "w_down_e", "w_sh_gate", "w_sh_up", "w_sh_down", "ln2_g", "ln2_b")


_EXPERT_WEIGHTS = ("w_in", "w_gate_e", "w_up_e", "w_down_e")


def kernel(x, w_in, conv_qkv, a_log, dt_bias, o_norm_g, w_oa, dw_w, dw_b, cln_g, cln_b, w_ob, w_o,
           ln1_g, ln1_b, w_router, router_bias, w_gate_e, w_up_e, w_down_e, w_sh_gate, w_sh_up,
           w_sh_down, ln2_g, ln2_b):
    stacked = dict(zip(_PARAM_NAMES, (w_in, conv_qkv, a_log, dt_bias, o_norm_g, w_oa, dw_w, dw_b, cln_g,
                                      cln_b, w_ob, w_o, ln1_g, ln1_b, w_router, router_bias, w_gate_e,
                                      w_up_e, w_down_e, w_sh_gate, w_sh_up, w_sh_down, ln2_g, ln2_b)))
    batch, seq, d = x.shape
    assert d == D_MODEL and seq % CHUNK == 0
    xf = x.reshape(batch * seq, d).astype(F32)
    for layer in range(w_in.shape[0]):
        p = {name: arr[layer] for name, arr in stacked.items() if name not in _EXPERT_WEIGHTS}
        xf = _layer(xf, p, stacked, layer, batch, seq)
    return xf.reshape(batch, seq, d).astype(x.dtype)
```

```python
import functools

import jax
import jax.numpy as jnp
import numpy as np
from jax import lax
from jax.experimental import pallas as pl
from jax.experimental.pallas import tpu as pltpu
from jax.experimental.pallas import tpu_sc as plsc

F32 = jnp.float32
BF16 = jnp.bfloat16
I32 = jnp.int32
U32 = jnp.uint32
HI_MASK = np.uint32(0xFFFF0000)

D_MODEL = 1024
GDN_HEADS = 8
HEAD_DIM = 128
GDN_WIDTH = GDN_HEADS * HEAD_DIM
SHORT_CONV = 4
CONV_HALO = 16
CHUNK = 64
SOLVE_BLOCK = 16
GDN_TILE = 256
SCAN_CHUNKS = 4
SCAN_SEQS = 2
CONF_CH = D_MODEL
CONF_KERNEL = 31
CONF_HALO = 32
N_EXPERTS = 64
TOP_K = 8
N_GROUPS = 8
GROUP_SIZE = N_EXPERTS // N_GROUPS
TOPK_GROUPS = 4
EXPERT_FF = 256
SHARED_FF = 256
ROUTED_SCALE = 2.5
DEPTH = 2
DEEPNORM_ALPHA = (2 * DEPTH) ** 0.25
EPS = 1e-6

LANES = 128
PROJ_COLS = 8 * D_MODEL
ROW_BLOCK = 2048
EXPERT_SUB = 256
EXPERT_IN_SLOTS = 3
SC_SCATTER_ROWS = 128
SC_GATHER_ROWS = 64
VMEM_LIMIT = 56 * 1024 * 1024
REGROUP_ROWS = 512
PROJ_ROWS = 2048
PROJ_TILE_COLS = 2048
MIXER_ROWS = 256
MIXER_CONV_ROWS = 32
ROUTE_TOKENS = 512
DEST_TOKENS = 2048
SHARED_ROWS = 512
COMBINE_ROWS = 256


def _params(*sem):
    return pltpu.CompilerParams(dimension_semantics=sem, vmem_limit_bytes=VMEM_LIMIT)


def _dot(a, b):
    return jnp.dot(a, b, preferred_element_type=F32)


def _dot_nt(a, b):
    return lax.dot_general(a, b, (((1,), (1,)), ((), ())), preferred_element_type=F32)


def _split(a):
    hi = a.astype(BF16)
    lo = (a - hi.astype(F32)).astype(BF16)
    return hi, lo


def _sigmoid(x):
    return 1.0 / (1.0 + jnp.exp(-x))


def _silu(x):
    return x * _sigmoid(x)


def _layer_norm(y, g, b):
    mu = jnp.mean(y, axis=-1, keepdims=True)
    yc = y - mu
    var = jnp.mean(yc * yc, axis=-1, keepdims=True)
    return yc * lax.rsqrt(var + EPS) * g + b


def _proj_kernel(a_ref, w_ref, o_ref):
    o_ref[...] = _dot_nt(a_ref[...].astype(BF16), w_ref[...]).astype(o_ref.dtype)


def _regroup_kernel(wt_hbm, main_ref, ahi_ref, alo_ref, bhi_ref, blo_ref, buf, ab_buf, sem, *, layer, tr):
    i = pl.program_id(0)
    half = PROJ_COLS // 2
    skip = 2 * GDN_HEADS
    start = pl.multiple_of(jnp.where(i < half // tr, 0, skip) + i * tr, skip)
    cp = pltpu.make_async_copy(wt_hbm.at[layer, pl.ds(start, tr), :], buf, sem)
    cp.start()
    cp.wait()
    main_ref[...] = buf[...].astype(BF16)

    @pl.when(i == 0)
    def _():
        cp_ab = pltpu.make_async_copy(wt_hbm.at[layer, pl.ds(half, LANES), :], ab_buf, sem)
        cp_ab.start()
        cp_ab.wait()
        cols = ab_buf[...].T
        wide = jnp.concatenate([cols, jnp.zeros_like(cols)], axis=1)
        lane = lax.broadcasted_iota(I32, cols.shape, 1)
        for off, hi_ref, lo_ref in ((0, ahi_ref, alo_ref), (GDN_HEADS, bhi_ref, blo_ref)):
            hi, lo = _split(jnp.where(lane < GDN_HEADS, wide[:, off:off + LANES], 0.0))
            hi_ref[...] = hi
            lo_ref[...] = lo


def _regroup(w_in_t, layer, tr):
    d = w_in_t.shape[2]
    small = jax.ShapeDtypeStruct((d, LANES), BF16)
    whole = lambda i: (0, 0)
    return pl.pallas_call(
        functools.partial(_regroup_kernel, layer=layer, tr=tr),
        grid=(PROJ_COLS // tr,),
        in_specs=[pl.BlockSpec(memory_space=pl.ANY)],
        out_specs=[pl.BlockSpec((tr, d), lambda i: (i, 0))] + [pl.BlockSpec((d, LANES), whole)] * 4,
        out_shape=[jax.ShapeDtypeStruct((PROJ_COLS, d), BF16), small, small, small, small],
        scratch_shapes=[pltpu.VMEM((tr, d), F32), pltpu.VMEM((LANES, d), F32), pltpu.SemaphoreType.DMA],
        compiler_params=_params("arbitrary"),
        name="regroup",
    )(w_in_t)


def _proj(xb, w, tm, tn):
    m, k = xb.shape
    n = w.shape[0]
    return pl.pallas_call(
        _proj_kernel,
        grid=(m // tm, n // tn),
        in_specs=[pl.BlockSpec((tm, k), lambda i, j: (i, 0)),
                  pl.BlockSpec((tn, k), lambda i, j: (j, 0))],
        out_specs=pl.BlockSpec((tm, tn), lambda i, j: (i, j)),
        out_shape=jax.ShapeDtypeStruct((m, n), BF16),
        compiler_params=_params("parallel", "parallel"),
        name="proj",
    )(xb, w)


def _unit_lower_inverse4(als):
    c = CHUNK
    n = als[0].shape[1]
    row = lax.broadcasted_iota(I32, (c, n), 0)
    col = jnp.bitwise_and(lax.broadcasted_iota(I32, (c, n), 1), c - 1)
    shift = SOLVE_BLOCK.bit_length() - 1
    same = jnp.right_shift(row, shift) == jnp.right_shift(col, shift)
    eye = (row == col).astype(F32)
    cshift = c.bit_length() - 1
    brow = jnp.right_shift(lax.broadcasted_iota(I32, (n, n), 0), cshift)
    bcol = jnp.right_shift(lax.broadcasted_iota(I32, (n, n), 1), cshift)
    on_diag = brow == bcol

    def mm(x, y):
        yb = y.astype(BF16)
        bd = jnp.where(on_diag, jnp.concatenate([yb] * (n // c), axis=0), jnp.zeros((), BF16))
        return _dot(x.astype(BF16), bd)

    a_diag = [jnp.where(same, al, 0.0) for al in als]
    a_off = [al - ad for al, ad in zip(als, a_diag)]
    bp = [-ad for ad in a_diag]
    p = [eye + b for b in bp]
    for _ in range(3):
        bp = [mm(b, b) for b in bp]
        p = [x + mm(x, b) for x, b in zip(p, bp)]
    n1 = [mm(x, ao) for x, ao in zip(p, a_off)]
    n2 = [mm(x, x) for x in n1]
    q = [x + mm(y, x) for x, y in zip(p, n2)]
    return [x - mm(y, x) for x, y in zip(q, n1)]


def _shift_selectors(rt):
    sel = np.zeros((SHORT_CONV * rt, rt), np.float32)
    sel_halo = np.zeros((SHORT_CONV * 8, CONV_HALO), np.float32)
    for d in range(SHORT_CONV):
        for t in range(d, rt):
            sel[d * rt + t, t - d] = 1.0
        for t in range(d):
            sel_halo[d * 8 + t, CONV_HALO + t - d] = 1.0
    return jnp.asarray(sel, BF16), jnp.asarray(sel_halo, BF16)


def _gdn_intra_kernel(qkv_ref, prev_ref, x_ref, sel_ref, selh_ref, wah_ref, wal_ref, wbh_ref, wbl_ref, cw_ref,
                      prm_ref, u_ref, w_ref, qd_ref, m2_ref, eg_ref, xs_ref, *, rt, tiles_per_seq):
    c = CHUNK
    nc = rt // c
    first = (pl.program_id(0) % tiles_per_seq) == 0
    edge = jnp.where(first, 0.0, _dot(selh_ref[...], prev_ref[...]))

    def move_rows(lo, hi):
        moved = _dot(sel_ref[...], qkv_ref[:, lo:hi])
        for d in range(SHORT_CONV):
            xs_ref[d, 0:8, lo:hi] = moved[d * rt:d * rt + 8] + edge[d * 8:(d + 1) * 8, lo:hi]
            xs_ref[d, 8:rt, lo:hi] = moved[d * rt + 8:(d + 1) * rt]

    for part in range(3):
        move_rows(part * GDN_WIDTH, part * GDN_WIDTH + 2 * HEAD_DIM)

    xh, xl = _split(x_ref[...])

    def proj3(wh_ref, wl_ref):
        wh = wh_ref[...]
        return _dot(xh, wh) + _dot(xl, wh) + _dot(xh, wl_ref[...])

    a_raw = proj3(wah_ref, wal_ref)
    b_raw = proj3(wbh_ref, wbl_ref)
    sp_in = a_raw + prm_ref[1:2, :]
    softplus = jnp.maximum(sp_in, 0.0) + jnp.log(1.0 + jnp.exp(-jnp.abs(sp_in)))
    g = -jnp.exp(prm_ref[0:1, :]) * softplus
    beta = _sigmoid(b_raw)

    cshift = c.bit_length() - 1
    r2 = lax.broadcasted_iota(I32, (rt, rt), 0)
    c2 = lax.broadcasted_iota(I32, (rt, rt), 1)
    same_chunk = jnp.right_shift(r2, cshift) == jnp.right_shift(c2, cshift)
    ltri = jnp.where(r2 >= c2, jnp.where(same_chunk, 1.0, 0.0), 0.0).astype(BF16)
    g_hi = g.astype(BF16)
    g_r = g - g_hi.astype(F32)
    g_mid = g_r.astype(BF16)
    g_lo = (g_r - g_mid.astype(F32)).astype(BF16)
    gc = _dot(ltri, g_hi) + _dot(ltri, g_mid) + _dot(ltri, g_lo)
    gct = gc.T
    egc = jnp.exp(gc)
    gend = jnp.concatenate(
        [jnp.broadcast_to(gc[ci * c + c - 1:ci * c + c, :], (c, LANES)) for ci in range(nc)], axis=0)
    kfac = jnp.exp(gend - gc)
    bege = beta * egc
    for ci in range(nc):
        last = ci * c + c - 1
        eg_ref[ci * GDN_HEADS:(ci + 1) * GDN_HEADS, :] = jnp.broadcast_to(
            jnp.exp(gct[0:GDN_HEADS, last:last + 1]), (GDN_HEADS, LANES))

    lane_t = lax.broadcasted_iota(I32, (rt, LANES), 1) < c
    lane_lo = lax.broadcasted_iota(I32, (c, LANES), 1) < c
    lcol = jnp.bitwise_and(lax.broadcasted_iota(I32, (c, LANES), 1), c - 1)
    rowi = lax.broadcasted_iota(I32, (c, LANES), 0)
    causal = rowi >= lcol
    strict = rowi > lcol

    def conv(base, h):
        lo, hi = base + h * HEAD_DIM, base + (h + 1) * HEAD_DIM
        acc = cw_ref[SHORT_CONV - 1:SHORT_CONV, lo:hi] * xs_ref[0, :, lo:hi]
        for j in range(SHORT_CONV - 1):
            acc = acc + cw_ref[j:j + 1, lo:hi] * xs_ref[SHORT_CONV - 1 - j, :, lo:hi]
        return _silu(acc)

    a_pairs = [[None] * (GDN_HEADS // 2) for _ in range(nc)]
    rhs_pairs = [[None] * (GDN_HEADS // 2) for _ in range(nc)]
    for p in range(GDN_HEADS // 2):
        if p + 1 < GDN_HEADS // 2:
            for part in range(3):
                lo = part * GDN_WIDTH + (p + 1) * 2 * HEAD_DIM
                move_rows(lo, lo + 2 * HEAD_DIM)
        ks, kbs, qs, kds, rhss = [], [], [], [], []
        for h in (2 * p, 2 * p + 1):
            q = conv(0, h)
            k = conv(GDN_WIDTH, h)
            v = conv(2 * GDN_WIDTH, h)
            q = q * lax.rsqrt(jnp.sum(q * q, axis=-1, keepdims=True) + EPS) * (HEAD_DIM ** -0.5)
            k = k * lax.rsqrt(jnp.sum(k * k, axis=-1, keepdims=True) + EPS)
            beta_h = beta[:, h:h + 1]
            qd_ref[:, h * HEAD_DIM:(h + 1) * HEAD_DIM] = (q * egc[:, h:h + 1]).astype(BF16)
            ks.append(k)
            kbs.append(k * beta_h)
            qs.append(q)
            kds.append(k * kfac[:, h:h + 1])
            rhss.append(jnp.concatenate([v * beta_h, k * bege[:, h:h + 1]], axis=1))
        h0, h1 = 2 * p, 2 * p + 1
        gch = jnp.where(lane_t, gc[:, h0:h0 + 1], gc[:, h1:h1 + 1])
        for ci in range(nc):
            rows = slice(ci * c, (ci + 1) * c)
            wk = jnp.concatenate([ks[0][rows], ks[1][rows]], axis=0).astype(BF16)
            lhs = jnp.concatenate([kbs[0][rows], qs[0][rows], kbs[1][rows], qs[1][rows]],
                                  axis=0).astype(BF16)
            out = _dot_nt(lhs, wk)
            gcrow = jnp.concatenate([gct[h0:h0 + 1, rows], gct[h1:h1 + 1, rows]], axis=1)
            diff = gch[rows] - gcrow
            decay = jnp.where(causal, jnp.exp(jnp.where(causal, diff, 0.0)), 0.0)
            a_pairs[ci][p] = jnp.where(strict, jnp.where(lane_lo, out[0:c], out[2 * c:3 * c]) * decay, 0.0)
            qk = jnp.where(lane_lo, out[c:2 * c], out[3 * c:4 * c]) * decay
            kdt = jnp.concatenate([kds[0][rows], kds[1][rows]], axis=0).T
            m0 = ci * 3 * c
            m2_ref[m0:m0 + c, p * LANES:(p + 1) * LANES] = qk.astype(BF16)
            m2_ref[m0 + c:m0 + 3 * c, p * LANES:(p + 1) * LANES] = kdt.astype(BF16)
            rhs_pairs[ci][p] = (rhss[0][rows], rhss[1][rows])

    zeros = jnp.zeros((c, 2 * HEAD_DIM), BF16)
    ngrp = GDN_HEADS // 4
    tls = _unit_lower_inverse4(
        [jnp.concatenate([a_pairs[ci][2 * grp], a_pairs[ci][2 * grp + 1]], axis=1)
         for ci in range(nc) for grp in range(ngrp)])
    for ci in range(nc):
        rows = slice(ci * c, (ci + 1) * c)
        for grp in range(ngrp):
            tl = tls[ci * ngrp + grp]
            for j in range(2):
                p = 2 * grp + j
                r0, r1 = rhs_pairs[ci][p]
                bd = jnp.concatenate([jnp.concatenate([r0.astype(BF16), zeros], axis=1),
                                      jnp.concatenate([zeros, r1.astype(BF16)], axis=1)], axis=0)
                sol = _dot(tl[:, j * LANES:(j + 1) * LANES].astype(BF16), bd)
                for i in range(2):
                    h = 2 * p + i
                    lo, hi = h * HEAD_DIM, (h + 1) * HEAD_DIM
                    u_ref[rows, lo:hi] = sol[:, 2 * i * HEAD_DIM:(2 * i + 1) * HEAD_DIM]
                    w_ref[rows, lo:hi] = sol[:, (2 * i + 1) * HEAD_DIM:(2 * i + 2) * HEAD_DIM].astype(BF16)


def _gdn_intra(proj, x, wa_hi, wa_lo, wb_hi, wb_lo, conv_w, prm, seq, rt):
    t = x.shape[0]
    nc = rt // CHUNK
    kern = functools.partial(_gdn_intra_kernel, rt=rt, tiles_per_seq=seq // rt)
    full = lambda i: (0, 0)
    tile = lambda i: (i, 0)
    sel, sel_halo = _shift_selectors(rt)
    return pl.pallas_call(
        kern,
        grid=(t // rt,),
        in_specs=[
            pl.BlockSpec((rt, 3 * GDN_WIDTH), tile),
            pl.BlockSpec((CONV_HALO, 3 * GDN_WIDTH),
                         lambda i: (jnp.maximum(i * (rt // CONV_HALO) - 1, 0), 0)),
            pl.BlockSpec((rt, D_MODEL), tile),
            pl.BlockSpec(sel.shape, full),
            pl.BlockSpec(sel_halo.shape, full),
            pl.BlockSpec((D_MODEL, LANES), full),
            pl.BlockSpec((D_MODEL, LANES), full),
            pl.BlockSpec((D_MODEL, LANES), full),
            pl.BlockSpec((D_MODEL, LANES), full),
            pl.BlockSpec((SHORT_CONV, 3 * GDN_WIDTH), full),
            pl.BlockSpec((8, LANES), full),
        ],
        out_specs=[pl.BlockSpec((rt, GDN_WIDTH), tile),
                   pl.BlockSpec((rt, GDN_WIDTH), tile),
                   pl.BlockSpec((rt, GDN_WIDTH), tile),
                   pl.BlockSpec((nc * 3 * CHUNK, GDN_HEADS // 2 * LANES), tile),
                   pl.BlockSpec((nc * GDN_HEADS, LANES), tile)],
        out_shape=[jax.ShapeDtypeStruct((t, GDN_WIDTH), F32),
                   jax.ShapeDtypeStruct((t, GDN_WIDTH), BF16),
                   jax.ShapeDtypeStruct((t, GDN_WIDTH), BF16),
                   jax.ShapeDtypeStruct((t // CHUNK * 3 * CHUNK, GDN_HEADS // 2 * LANES), BF16),
                   jax.ShapeDtypeStruct((t // CHUNK * GDN_HEADS, LANES), F32)],
        scratch_shapes=[pltpu.VMEM((SHORT_CONV, rt, 3 * GDN_WIDTH), F32)],
        compiler_params=_params("parallel"),
        name="gdn_intra",
    )(proj, proj, x, sel, sel_halo, wa_hi, wa_lo, wb_hi, wb_lo, conv_w, prm)


def _gdn_scan_kernel(u_ref, w_ref, qd_ref, m2_ref, eg_ref, z_ref, ong_ref, o_ref, s_ref, *, nck):
    c = CHUNK
    nseq = u_ref.shape[0]

    @pl.when(pl.program_id(1) == 0)
    def _():
        s_ref[...] = jnp.zeros_like(s_ref)

    ong = ong_ref[...]
    zeros = jnp.zeros((c, HEAD_DIM), BF16)
    span = lambda h: slice(h * HEAD_DIM, (h + 1) * HEAD_DIM)
    chains = [(b, h) for b in range(nseq) for h in range(GDN_HEADS)]
    states = {bh: s_ref[bh[0], bh[1]] for bh in chains}
    for ci in range(nck):
        rows = slice(ci * c, (ci + 1) * c)
        rs = {(b, h): _dot(jnp.concatenate([w_ref[b, rows, span(h)], qd_ref[b, rows, span(h)]], axis=0),
                           states[b, h].astype(BF16)) for b, h in chains}
        v_new = {(b, h): (u_ref[b, rows, span(h)] - rs[b, h][:c]).astype(BF16) for b, h in chains}
        r2s = {}
        for b in range(nseq):
            for p in range(GDN_HEADS // 2):
                bd = jnp.concatenate([jnp.concatenate([v_new[b, 2 * p], zeros], axis=1),
                                      jnp.concatenate([zeros, v_new[b, 2 * p + 1]], axis=1)], axis=0)
                r2s[b, p] = _dot(m2_ref[b, ci * 3 * c:(ci + 1) * 3 * c, p * LANES:(p + 1) * LANES], bd)
        for b, h in chains:
            half = span(h % 2)
            r2 = r2s[b, h // 2]
            decay = eg_ref[b, ci * GDN_HEADS + h:ci * GDN_HEADS + h + 1, :]
            states[b, h] = states[b, h] * decay + r2[c:, half]
            o = rs[b, h][c:] + r2[:c, half]
            o = o * lax.rsqrt(jnp.mean(o * o, axis=-1, keepdims=True) + EPS) * ong
            o = o * _silu(z_ref[b, rows, span(h)].astype(F32))
            o_ref[b, rows, span(h)] = o.astype(o_ref.dtype)
    for b, h in chains:
        s_ref[b, h] = states[b, h]


def _gdn_scan(u, w, qd, m2, eg, proj, ong, batch, seq):
    t = batch * seq
    nck = min(SCAN_CHUNKS, seq // CHUNK)
    c = nck * CHUNK
    nch = seq // c
    ns = SCAN_SEQS if batch % SCAN_SEQS == 0 else 1
    groups = batch // ns
    split = lambda a: a.reshape(ns, a.shape[0] // ns, a.shape[1])
    blk = lambda b, n: (0, b * nch + n, 0)
    out = pl.pallas_call(
        functools.partial(_gdn_scan_kernel, nck=nck),
        grid=(groups, nch),
        in_specs=[
            pl.BlockSpec((ns, c, GDN_WIDTH), blk),
            pl.BlockSpec((ns, c, GDN_WIDTH), blk),
            pl.BlockSpec((ns, c, GDN_WIDTH), blk),
            pl.BlockSpec((ns, 3 * c, GDN_HEADS // 2 * LANES), blk),
            pl.BlockSpec((ns, nck * GDN_HEADS, LANES), blk),
            pl.BlockSpec((ns, c, GDN_WIDTH), lambda b, n: (0, b * nch + n, 3)),
            pl.BlockSpec((1, HEAD_DIM), lambda b, n: (0, 0)),
        ],
        out_specs=pl.BlockSpec((ns, c, GDN_WIDTH), blk),
        out_shape=jax.ShapeDtypeStruct((ns, t // ns, GDN_WIDTH), BF16),
        scratch_shapes=[pltpu.VMEM((ns, GDN_HEADS, HEAD_DIM, HEAD_DIM), F32)],
        compiler_params=_params("parallel", "arbitrary"),
        name="gdn_scan",
    )(split(u), split(w), split(qd), split(m2), split(eg), split(proj), ong)
    return out.reshape(t, GDN_WIDTH)


def _pack_bf16_pairs(y):
    n = y.shape[1] // 2
    yb = y.astype(BF16).astype(F32)
    lo = lax.bitcast_convert_type(yb[:, :n], U32)
    hi = lax.bitcast_convert_type(yb[:, n:], U32)
    return jnp.bitwise_or(jnp.right_shift(lo, 16), jnp.bitwise_and(hi, HI_MASK))


def _unpack_bf16_pairs(w):
    lo = lax.bitcast_convert_type(jnp.left_shift(w, 16), F32)
    hi = lax.bitcast_convert_type(jnp.bitwise_and(w, HI_MASK), F32)
    return lo, hi


def _mixer_kernel(glu_ref, ga_ref, gb_ref, o_ref, x_ref, woa_ref, wob_ref, wo_ref, dww_ref, vec_ref,
                  x1_ref, x1b_ref, x1p_ref, ubuf_ref, sh_ref, conv_ref, *, ts, rc):
    halo = CONF_HALO

    @pl.when(pl.program_id(1) == 0)
    def _():
        ubuf_ref[0:halo, :] = jnp.zeros((halo, CONF_CH), F32)

    @pl.when(pl.program_id(1) != 0)
    def _():
        ubuf_ref[0:halo, :] = ubuf_ref[ts:ts + halo, :]

    glu_a = glu_ref[:, :CONF_CH].astype(F32)
    glu_b = glu_ref[:, CONF_CH:].astype(F32)
    ubuf_ref[halo:halo + ts, :] = glu_a * _sigmoid(glu_b)

    dw_b = vec_ref[0:1, :]
    cln_g = vec_ref[1:2, :]
    cln_b = vec_ref[2:3, :]
    ln1_g = vec_ref[3:4, :]
    ln1_b = vec_ref[4:5, :]

    span = ts + halo - 8
    for s in range(1, 8):
        sh_ref[s - 1] = ubuf_ref[s:s + span, :]

    def tap_rows(j, r0):
        o = halo - (CONF_KERNEL - 1) + j
        q, s = o // 8, o % 8
        if s == 0:
            return ubuf_ref[r0 + o:r0 + o + rc, :]
        return sh_ref[s - 1, r0 + 8 * q:r0 + 8 * q + rc, :]

    hs = ts // 2
    for h0 in range(0, ts, hs):
        rows = slice(h0, h0 + hs)
        gated_a = _sigmoid(ga_ref[rows, :].astype(F32)) * _dot(o_ref[rows, :], woa_ref[...])
        for r0 in range(h0, h0 + hs, rc):
            tap_w = lambda j: jnp.concatenate([dww_ref[j]] * (rc // 8), axis=0)
            acc = tap_w(0) * tap_rows(0, r0)
            for j in range(1, CONF_KERNEL):
                acc = acc + tap_w(j) * tap_rows(j, r0)
            conv_ref[r0:r0 + rc, :] = acc
        uc = _silu(_layer_norm(conv_ref[rows, :] + dw_b, cln_g, cln_b))
        branch_b = _dot(uc.astype(BF16), wob_ref[...])
        hmix = gated_a + _sigmoid(gb_ref[rows, :].astype(F32)) * branch_b
        mix = _dot(hmix.astype(BF16), wo_ref[...])
        x1 = _layer_norm(DEEPNORM_ALPHA * x_ref[rows, :] + mix, ln1_g, ln1_b)
        x1_ref[rows, :] = x1
        x1b_ref[rows, :] = x1.astype(BF16)
        x1p_ref[rows, :] = _pack_bf16_pairs(x1)


def _mixer(proj, o_gdn, x, woa, wob, wo, dww, vec, batch, seq, ts):
    t = batch * seq
    nt = seq // ts
    rows = lambda b, n: b * nt + n
    full = lambda b, n: (0, 0)
    kern = functools.partial(_mixer_kernel, ts=ts, rc=MIXER_CONV_ROWS)
    return pl.pallas_call(
        kern,
        grid=(batch, nt),
        in_specs=[
            pl.BlockSpec((ts, 2 * CONF_CH), lambda b, n: (rows(b, n), 2)),
            pl.BlockSpec((ts, D_MODEL), lambda b, n: (rows(b, n), 6)),
            pl.BlockSpec((ts, D_MODEL), lambda b, n: (rows(b, n), 7)),
            pl.BlockSpec((ts, GDN_WIDTH), lambda b, n: (rows(b, n), 0)),
            pl.BlockSpec((ts, D_MODEL), lambda b, n: (rows(b, n), 0)),
            pl.BlockSpec((GDN_WIDTH, D_MODEL), full),
            pl.BlockSpec((CONF_CH, D_MODEL), full),
            pl.BlockSpec((D_MODEL, D_MODEL), full),
            pl.BlockSpec((CONF_KERNEL, 8, CONF_CH), lambda b, n: (0, 0, 0)),
            pl.BlockSpec((8, D_MODEL), full),
        ],
        out_specs=[pl.BlockSpec((ts, D_MODEL), lambda b, n: (rows(b, n), 0)),
                   pl.BlockSpec((ts, D_MODEL), lambda b, n: (rows(b, n), 0)),
                   pl.BlockSpec((ts, D_MODEL // 2), lambda b, n: (rows(b, n), 0))],
        out_shape=[jax.ShapeDtypeStruct((t, D_MODEL), F32),
                   jax.ShapeDtypeStruct((t, D_MODEL), BF16),
                   jax.ShapeDtypeStruct((t, D_MODEL // 2), U32)],
        scratch_shapes=[pltpu.VMEM((CONF_HALO + ts, CONF_CH), F32),
                        pltpu.VMEM((7, ts + CONF_HALO - 8, CONF_CH), F32),
                        pltpu.VMEM((ts, CONF_CH), F32)],
        compiler_params=_params("parallel", "arbitrary"),
        name="mixer",
    )(proj, proj, proj, o_gdn, x, woa, wob, wo, dww, vec)


def _route_kernel(x_ref, wrh_ref, wrl_ref, bias_ref, eidx_ref, wts_ref, rank_ref, cnt_ref, carry_ref, *, tt):
    @pl.when(pl.program_id(0) == 0)
    def _():
        carry_ref[...] = jnp.zeros_like(carry_ref)

    xh, xl = _split(x_ref[...])
    wrh = wrh_ref[...]
    logits = _dot_nt(wrh, xh) + _dot_nt(wrh, xl) + _dot_nt(wrl_ref[...], xh)
    s = _sigmoid(logits)
    biased = s + bias_ref[...]

    sub = lax.broadcasted_iota(I32, (GROUP_SIZE, tt), 0)
    groups = [biased[g * GROUP_SIZE:(g + 1) * GROUP_SIZE, :] for g in range(N_GROUPS)]
    gs = []
    for bg in groups:
        m1 = jnp.max(bg, axis=0, keepdims=True)
        first = jnp.min(jnp.where(bg == m1, sub, GROUP_SIZE), axis=0, keepdims=True)
        m2 = jnp.max(jnp.where(sub == first, -jnp.inf, bg), axis=0, keepdims=True)
        gs.append(m1 + m2)

    masked_parts = []
    for g in range(N_GROUPS):
        beaten = jnp.zeros((1, tt), I32)
        for o in range(N_GROUPS):
            if o == g:
                continue
            wins = (gs[o] >= gs[g]) if o < g else (gs[o] > gs[g])
            beaten = beaten + wins.astype(I32)
        keep = jnp.broadcast_to(beaten < TOPK_GROUPS, (GROUP_SIZE, tt))
        masked_parts.append(jnp.where(keep, groups[g], -jnp.inf))
    masked = jnp.concatenate(masked_parts, axis=0)

    eiota = lax.broadcasted_iota(I32, (N_EXPERTS, tt), 0)
    sel_all = jnp.zeros((N_EXPERTS, tt), F32)
    picks = []
    for _ in range(TOP_K):
        m = jnp.max(masked, axis=0, keepdims=True)
        idx = jnp.min(jnp.where(masked == m, eiota, N_EXPERTS), axis=0, keepdims=True)
        onehot = eiota == idx
        picks.append((idx, onehot))
        sel_all = jnp.where(onehot, 1.0, sel_all)
        masked = jnp.where(onehot, -jnp.inf, masked)

    tr = lax.broadcasted_iota(I32, (tt, tt), 0)
    tc = lax.broadcasted_iota(I32, (tt, tt), 1)
    before = (tr < tc).astype(BF16)
    sel_b = sel_all.astype(BF16)
    carry = carry_ref[...]
    rank_all = _dot(sel_b, before) + carry[:, 0:1]
    carry_new = carry + _dot(sel_b, jnp.ones((tt, LANES), BF16))
    carry_ref[...] = carry_new
    cnt_ref[...] = carry_new

    s_sel = [jnp.sum(jnp.where(oh, s, 0.0), axis=0, keepdims=True) for _, oh in picks]
    total = s_sel[0]
    for v in s_sel[1:]:
        total = total + v
    for k, (idx, oh) in enumerate(picks):
        eidx_ref[k:k + 1, :] = idx
        wts_ref[k:k + 1, :] = s_sel[k] / total * ROUTED_SCALE
        rank_ref[k:k + 1, :] = jnp.sum(jnp.where(oh, rank_all, 0.0), axis=0, keepdims=True).astype(I32)


def _route(x1, wr_hi, wr_lo, bias, tt):
    t = x1.shape[0]
    kern = functools.partial(_route_kernel, tt=tt)
    return pl.pallas_call(
        kern,
        grid=(t // tt,),
        in_specs=[pl.BlockSpec((tt, D_MODEL), lambda i: (i, 0)),
                  pl.BlockSpec((N_EXPERTS, D_MODEL), lambda i: (0, 0)),
                  pl.BlockSpec((N_EXPERTS, D_MODEL), lambda i: (0, 0)),
                  pl.BlockSpec((N_EXPERTS, tt), lambda i: (0, 0))],
        out_specs=[pl.BlockSpec((TOP_K, tt), lambda i: (0, i)),
                   pl.BlockSpec((TOP_K, tt), lambda i: (0, i)),
                   pl.BlockSpec((TOP_K, tt), lambda i: (0, i)),
                   pl.BlockSpec((N_EXPERTS, LANES), lambda i: (0, 0))],
        out_shape=[jax.ShapeDtypeStruct((TOP_K, t), I32),
                   jax.ShapeDtypeStruct((TOP_K, t), F32),
                   jax.ShapeDtypeStruct((TOP_K, t), I32),
                   jax.ShapeDtypeStruct((N_EXPERTS, LANES), F32)],
        scratch_shapes=[pltpu.VMEM((N_EXPERTS, LANES), F32)],
        compiler_params=_params("arbitrary"),
        name="route",
    )(x1, wr_hi, wr_lo, bias)


def _plan_kernel(cnt_ref, pstart_ref, plan_ref):
    e, nb = N_EXPERTS, plan_ref.shape[1]
    counts = cnt_ref[...]
    nblk = jnp.floor((counts + (ROW_BLOCK - 1)) * (1.0 / ROW_BLOCK))
    hi = jnp.floor(nblk * (1.0 / 256.0))
    lo = nblk - 256.0 * hi
    r = lax.broadcasted_iota(I32, (e, e), 0)
    c = lax.broadcasted_iota(I32, (e, e), 1)
    ltri = (r >= c).astype(BF16)
    bend = 256.0 * _dot(ltri, hi.astype(BF16)) + _dot(ltri, lo.astype(BF16))
    pend = bend * ROW_BLOCK
    pstart = pend - nblk * ROW_BLOCK
    pstart_ref[...] = pstart.astype(I32)

    bs = (lax.broadcasted_iota(I32, (e, nb), 1) * ROW_BLOCK).astype(F32)
    pend_b = jnp.broadcast_to(pend[:, 0:1], (e, nb))
    pstart_b = jnp.broadcast_to(pstart[:, 0:1], (e, nb))
    used_b = jnp.broadcast_to((pstart + counts)[:, 0:1], (e, nb))
    owner = jnp.sum(jnp.where(pend_b <= bs, 1.0, 0.0), axis=0, keepdims=True)
    inside = jnp.where(pstart_b <= bs, jnp.where(bs < pend_b, 1.0, 0.0), 0.0)
    real = jnp.sum(inside * jnp.clip(used_b - bs, 0.0, float(ROW_BLOCK)), axis=0, keepdims=True)
    plan_ref[0:1, :] = jnp.minimum(owner, float(e - 1)).astype(I32)
    plan_ref[1:2, :] = real.astype(I32)
    plan_ref[2:8, :] = jnp.zeros((6, nb), I32)


def _plan(cnt, n_blocks):
    nb = -(-n_blocks // LANES) * LANES
    pstart, plan = pl.pallas_call(
        _plan_kernel,
        out_shape=[jax.ShapeDtypeStruct((N_EXPERTS, LANES), I32),
                   jax.ShapeDtypeStruct((8, nb), I32)],
        name="plan",
    )(cnt)
    return pstart[:, 0], plan[0, :n_blocks], plan[1, :n_blocks]


def _dest_kernel(eidx_ref, rank_ref, pstart_ref, dest_ref):
    eidx = eidx_ref[...]
    acc = rank_ref[...]
    for e in range(N_EXPERTS):
        acc = acc + jnp.where(eidx == e, pstart_ref[e], 0)
    dest_ref[...] = acc


def _dest(eidx_t, rank_t, pstart, tt):
    t = eidx_t.shape[1]
    return pl.pallas_call(
        _dest_kernel,
        grid=(t // tt,),
        in_specs=[pl.BlockSpec((TOP_K, tt), lambda i: (0, i)),
                  pl.BlockSpec((TOP_K, tt), lambda i: (0, i)),
                  pl.BlockSpec(memory_space=pltpu.SMEM)],
        out_specs=pl.BlockSpec((TOP_K, tt), lambda i: (0, i)),
        out_shape=jax.ShapeDtypeStruct((TOP_K, t), I32),
        compiler_params=_params("parallel"),
        name="dest",
    )(eidx_t, rank_t, pstart)


def _dispatch(dest_kt, x1p, n_rows):
    t, width = x1p.shape
    info = plsc.get_sparse_core_info()
    nc, nw = info.num_cores, info.num_cores * info.num_subcores
    chunk = SC_SCATTER_ROWS
    per_w = t // nw
    n_chunks = per_w // chunk
    assert per_w % chunk == 0
    idx = dest_kt.reshape(TOP_K, nw, n_chunks, chunk).transpose(1, 2, 0, 3).reshape(nw, n_chunks * TOP_K, chunk)
    mesh = plsc.VectorSubcoreMesh(core_axis_name="c", subcore_axis_name="s")

    @functools.partial(
        pl.kernel, mesh=mesh, name="dispatch",
        out_type=jax.ShapeDtypeStruct((n_rows, width), x1p.dtype),
        scratch_types=[pltpu.VMEM((n_chunks * TOP_K, chunk), I32),
                       pltpu.VMEM((chunk, width), x1p.dtype),
                       pltpu.SemaphoreType.DMA])
    def scatter(x_hbm, idx_hbm, xs_hbm, idx_v, rows_v, sem):
        wid = lax.axis_index("s") * nc + lax.axis_index("c")
        base = wid * per_w
        pltpu.sync_copy(idx_hbm.at[wid], idx_v)
        for j in range(n_chunks):
            pltpu.sync_copy(x_hbm.at[pl.ds(base + j * chunk, chunk)], rows_v)
            copies = [pltpu.make_async_copy(rows_v, xs_hbm.at[idx_v.at[j * TOP_K + k]], sem)
                      for k in range(TOP_K)]
            for cp in copies:
                cp.start()
            for cp in copies:
                cp.wait()

    return scatter(x1p, idx)


def _gather_rows(table, idx):
    n = idx.shape[0]
    width = table.shape[1]
    info = plsc.get_sparse_core_info()
    nc, nw = info.num_cores, info.num_cores * info.num_subcores
    chunk = SC_GATHER_ROWS
    per_w = n // nw
    n_chunks = per_w // chunk
    assert per_w % (2 * chunk) == 0
    mesh = plsc.VectorSubcoreMesh(core_axis_name="c", subcore_axis_name="s")

    @functools.partial(
        pl.kernel, mesh=mesh, name="gather_rows",
        out_type=jax.ShapeDtypeStruct((n, width), table.dtype),
        scratch_types=[pltpu.VMEM((n_chunks, chunk), I32),
                       pltpu.VMEM((2, chunk, width), table.dtype),
                       pltpu.SemaphoreType.DMA((2,)),
                       pltpu.SemaphoreType.DMA((2,))])
    def gather(table_hbm, idx_hbm, out_hbm, idx_v, rows_v, gsem, osem):
        wid = lax.axis_index("s") * nc + lax.axis_index("c")
        base = wid * per_w
        pltpu.sync_copy(idx_hbm.at[wid], idx_v)

        def fetch(j, b):
            return pltpu.make_async_copy(table_hbm.at[idx_v.at[j]], rows_v.at[b], gsem.at[b])

        def put(j, b):
            return pltpu.make_async_copy(rows_v.at[b], out_hbm.at[pl.ds(base + j * chunk, chunk)], osem.at[b])

        fetch(0, 0).start()

        @pl.loop(0, n_chunks, step=2)
        def _(j0):
            for b in range(2):
                j = j0 + b
                fetch(j, b).wait()

                @pl.when(j + 1 < n_chunks)
                def _():
                    @pl.when(j >= 1)
                    def _():
                        put(j - 1, 1 - b).wait()

                    fetch(j + 1, 1 - b).start()

                put(j, b).start()

        put(n_chunks - 2, 0).wait()
        put(n_chunks - 1, 1).wait()

    return gather(table, idx.reshape(nw, n_chunks, chunk))


def _xs_copy(xs_hbm, xbuf, isem, j, slot):
    return pltpu.make_async_copy(xs_hbm.at[pl.ds(j * ROW_BLOCK, ROW_BLOCK)], xbuf.at[slot], isem.at[slot])


def _ys_copy(ybuf, ys_hbm, osem, j, slot):
    return pltpu.make_async_copy(ybuf.at[slot], ys_hbm.at[pl.ds(j * ROW_BLOCK, ROW_BLOCK)], osem.at[slot])


def _experts_kernel(be_ref, nv_ref, xs_hbm, wg_ref, wu_ref, wd_ref, ys_hbm,
                    xbuf, ybuf, wgu_s, wd_s, cur_ref, isem, osem):
    i = pl.program_id(0)
    n_valid = nv_ref[i]
    half = EXPERT_SUB
    slot = lax.rem(i, EXPERT_IN_SLOTS)
    oslot = lax.rem(i, 2)

    @pl.when(i == 0)
    def _():
        cur_ref[0] = -1
        for j in range(2):
            @pl.when(nv_ref[j] > 0)
            def _():
                _xs_copy(xs_hbm, xbuf, isem, j, j).start()

    @pl.when(nv_ref[i + 2] > 0)
    def _():
        _xs_copy(xs_hbm, xbuf, isem, i + 2, lax.rem(i + 2, EXPERT_IN_SLOTS)).start()

    @pl.when((i >= 2) & (nv_ref[jnp.maximum(i - 2, 0)] > 0))
    def _():
        _ys_copy(ybuf, ys_hbm, osem, i - 2, oslot).wait()

    @pl.when((n_valid > 0) & (cur_ref[0] != be_ref[i]))
    def _():
        wgu_s[:, :EXPERT_FF] = wg_ref[...].astype(BF16)
        wgu_s[:, EXPERT_FF:] = wu_ref[...].astype(BF16)
        wd_s[...] = wd_ref[...].astype(BF16)
        cur_ref[0] = be_ref[i]

    def rows_bf16(r0):
        valid = lax.broadcasted_iota(I32, (half, xbuf.shape[2]), 0) + r0 < n_valid
        lo, hi = _unpack_bf16_pairs(jnp.where(valid, xbuf[slot, r0:r0 + half, :], jnp.zeros((), U32)))
        return jnp.concatenate([lo.astype(BF16), hi.astype(BF16)], axis=1)

    def hidden(gu):
        return (_silu(gu[:, :EXPERT_FF]) * gu[:, EXPERT_FF:]).astype(BF16)

    @pl.when(n_valid > 0)
    def _():
        _xs_copy(xs_hbm, xbuf, isem, i, slot).wait()

    zeros = jnp.zeros((half, ybuf.shape[2]), ybuf.dtype)
    for r0 in range(0, ROW_BLOCK, 2 * half):
        r1 = r0 + half

        @pl.when(n_valid > r1)
        def _():
            xa, xb = rows_bf16(r0), rows_bf16(r1)
            gua = _dot(xa, wgu_s[...])
            gub = _dot(xb, wgu_s[...])
            ya = _dot(hidden(gua), wd_s[...])
            yb = _dot(hidden(gub), wd_s[...])
            ybuf[oslot, r0:r1, :] = _pack_bf16_pairs(ya)
            ybuf[oslot, r1:r1 + half, :] = _pack_bf16_pairs(yb)

        @pl.when((n_valid > r0) & (n_valid <= r1))
        def _():
            ya = _dot(hidden(_dot(rows_bf16(r0), wgu_s[...])), wd_s[...])
            ybuf[oslot, r0:r1, :] = _pack_bf16_pairs(ya)
            ybuf[oslot, r1:r1 + half, :] = zeros

        @pl.when((n_valid > 0) & (n_valid <= r0))
        def _():
            ybuf[oslot, r0:r1, :] = zeros
            ybuf[oslot, r1:r1 + half, :] = zeros

    @pl.when(n_valid > 0)
    def _():
        _ys_copy(ybuf, ys_hbm, osem, i, oslot).start()


def _experts(block_e, n_valid, xs, wg, wu, wd, layer):
    n_rows, width = xs.shape
    steps = n_rows // ROW_BLOCK + 2
    be = jnp.concatenate([block_e, jnp.full((2,), N_EXPERTS - 1, I32)])
    nv = jnp.concatenate([n_valid, jnp.zeros((4,), I32)])
    grid_spec = pltpu.PrefetchScalarGridSpec(
        num_scalar_prefetch=2,
        grid=(steps,),
        in_specs=[pl.BlockSpec(memory_space=pl.ANY),
                  pl.BlockSpec((None, None, D_MODEL, EXPERT_FF), lambda i, be, nv: (layer, be[i], 0, 0)),
                  pl.BlockSpec((None, None, D_MODEL, EXPERT_FF), lambda i, be, nv: (layer, be[i], 0, 0)),
                  pl.BlockSpec((None, None, EXPERT_FF, D_MODEL), lambda i, be, nv: (layer, be[i], 0, 0))],
        out_specs=pl.BlockSpec(memory_space=pl.ANY),
        scratch_shapes=[pltpu.VMEM((EXPERT_IN_SLOTS, ROW_BLOCK, width), xs.dtype),
                        pltpu.VMEM((2, ROW_BLOCK, width), xs.dtype),
                        pltpu.VMEM((D_MODEL, 2 * EXPERT_FF), BF16),
                        pltpu.VMEM((EXPERT_FF, D_MODEL), BF16),
                        pltpu.SMEM((1,), I32),
                        pltpu.SemaphoreType.DMA((EXPERT_IN_SLOTS,)),
                        pltpu.SemaphoreType.DMA((2,))],
    )
    return pl.pallas_call(
        _experts_kernel,
        grid_spec=grid_spec,
        out_shape=jax.ShapeDtypeStruct((n_rows, width), xs.dtype),
        compiler_params=_params("arbitrary"),
        name="experts",
    )(be, nv, xs, wg, wu, wd)


def _shared_kernel(x1b_ref, wsg_ref, wsu_ref, wsd_ref, anchor_ref, sh_ref):
    del anchor_ref
    xb = x1b_ref[...]
    hid = (_silu(_dot(xb, wsg_ref[...])) * _dot(xb, wsu_ref[...])).astype(BF16)
    sh_ref[...] = _dot(hid, wsd_ref[...]).astype(sh_ref.dtype)


def _shared(x1b, wsg, wsu, wsd, anchor, tt, row0, nrows):
    full = lambda i: (0, 0)
    first = row0 // tt
    return pl.pallas_call(
        _shared_kernel,
        grid=(nrows // tt,),
        in_specs=[pl.BlockSpec((tt, D_MODEL), lambda i: (first + i, 0)),
                  pl.BlockSpec((D_MODEL, SHARED_FF), full),
                  pl.BlockSpec((D_MODEL, SHARED_FF), full),
                  pl.BlockSpec((SHARED_FF, D_MODEL), full),
                  pl.BlockSpec(memory_space=pl.ANY)],
        out_specs=pl.BlockSpec((tt, D_MODEL), lambda i: (i, 0)),
        out_shape=jax.ShapeDtypeStruct((nrows, D_MODEL), BF16),
        compiler_params=_params("parallel"),
        name="shared",
    )(x1b, wsg, wsu, wsd, anchor)


def _combine_kernel(yg_ref, w_ref, x1_ref, sha_ref, shb_ref, vec_ref, x2_ref, *, half_steps):
    in_first = pl.program_id(0) < half_steps
    shared = jnp.where(in_first, sha_ref[...], shb_ref[...]).astype(F32)
    w = w_ref[...]
    half = D_MODEL // 2
    acc_lo, acc_hi = shared[:, :half], shared[:, half:]
    for k in range(TOP_K):
        lo, hi = _unpack_bf16_pairs(yg_ref[k])
        acc_lo = acc_lo + w[:, k:k + 1] * lo
        acc_hi = acc_hi + w[:, k:k + 1] * hi
    acc = jnp.concatenate([acc_lo, acc_hi], axis=1)
    x2 = _layer_norm(DEEPNORM_ALPHA * x1_ref[...] + acc, vec_ref[0:1, :], vec_ref[1:2, :])
    x2_ref[...] = x2


def _combine(yg, w_tok, x1, shared_a, shared_b, vec, tt):
    t = x1.shape[0]
    half_steps = shared_a.shape[0] // tt
    full = lambda i: (0, 0)
    return pl.pallas_call(
        functools.partial(_combine_kernel, half_steps=half_steps),
        grid=(t // tt,),
        in_specs=[pl.BlockSpec((TOP_K, tt, yg.shape[2]), lambda i: (0, i, 0)),
                  pl.BlockSpec((tt, TOP_K), lambda i: (i, 0)),
                  pl.BlockSpec((tt, D_MODEL), lambda i: (i, 0)),
                  pl.BlockSpec((tt, D_MODEL), lambda i: (jnp.minimum(i, half_steps - 1), 0)),
                  pl.BlockSpec((tt, D_MODEL), lambda i: (jnp.maximum(i - half_steps, 0), 0)),
                  pl.BlockSpec((8, D_MODEL), full)],
        out_specs=pl.BlockSpec((tt, D_MODEL), lambda i: (i, 0)),
        out_shape=jax.ShapeDtypeStruct((t, D_MODEL), F32),
        compiler_params=_params("parallel"),
        name="combine",
    )(yg, w_tok, x1, shared_a, shared_b, vec)


def _pad_rows(a, rows):
    return jnp.zeros((rows, a.shape[-1]), F32).at[:a.shape[0]].set(a.astype(F32))


def _layer(x, p, stacked, layer, batch, seq):
    t = batch * seq
    w_main, wa_hi, wa_lo, wb_hi, wb_lo = _regroup(jnp.swapaxes(stacked["w_in"], 1, 2), layer, REGROUP_ROWS)
    proj = _proj(x, w_main, min(PROJ_ROWS, t), PROJ_TILE_COLS)

    prm = jnp.zeros((8, LANES), F32)
    prm = prm.at[0, :GDN_HEADS].set(p["a_log"]).at[1, :GDN_HEADS].set(p["dt_bias"])
    ong = p["o_norm_g"].reshape(1, HEAD_DIM).astype(F32)
    u, w, qd, m2, eg = _gdn_intra(proj, x, wa_hi, wa_lo, wb_hi, wb_lo, p["conv_qkv"].astype(F32), prm,
                                  seq, min(GDN_TILE, seq))
    o_gdn = _gdn_scan(u, w, qd, m2, eg, proj, ong, batch, seq)

    ts = min(MIXER_ROWS, seq)
    dww = jnp.broadcast_to(p["dw_w"].astype(F32)[:, None, :], (CONF_KERNEL, 8, CONF_CH))
    vec = _pad_rows(jnp.stack([p["dw_b"], p["cln_g"], p["cln_b"], p["ln1_g"], p["ln1_b"]]), 8)
    x1, x1b, x1p = _mixer(proj, o_gdn, x, p["w_oa"].astype(BF16), p["w_ob"].astype(BF16),
                          p["w_o"].astype(BF16), dww, vec, batch, seq, ts)

    tt_r = min(ROUTE_TOKENS, t)
    wr_t = p["w_router"].T.astype(F32)
    wr_hi = wr_t.astype(BF16)
    wr_lo = (wr_t - wr_hi.astype(F32)).astype(BF16)
    bias = jnp.broadcast_to(p["router_bias"].astype(F32)[:, None], (N_EXPERTS, tt_r))
    eidx_t, wts_t, rank_t, cnt = _route(x1, wr_hi, wr_lo, bias, tt_r)

    n_blocks = -(-(t * TOP_K + N_EXPERTS * (ROW_BLOCK - 1)) // ROW_BLOCK)
    n_rows = n_blocks * ROW_BLOCK
    pstart, block_e, n_valid = _plan(cnt, n_blocks)

    dest = _dest(eidx_t, rank_t, pstart, min(DEST_TOKENS, t))
    sh_w = (p["w_sh_gate"].astype(BF16), p["w_sh_up"].astype(BF16), p["w_sh_down"].astype(BF16))
    tt_s = min(SHARED_ROWS, t // 2)
    xs = _dispatch(dest, x1p, n_rows)
    shared_a = _shared(x1b, *sh_w, dest, tt_s, 0, t // 2)
    ys = _experts(block_e, n_valid, xs, stacked["w_gate_e"], stacked["w_up_e"], stacked["w_down_e"], layer)
    yg = _gather_rows(ys, dest.reshape(TOP_K * t)).reshape(TOP_K, t, ys.shape[1])
    shared_b = _shared(x1b, *sh_w, ys, tt_s, t // 2, t // 2)
    vec2 = _pad_rows(jnp.stack([p["ln2_g"], p["ln2_b"]]), 8)
    return _combine(yg, wts_t.T, x1, shared_a, shared_b, vec2, min(COMBINE_ROWS, t // 2))


_PARAM_NAMES = ("w_in", "conv_qkv", "a_log", "dt_bias", "o_norm_g", "w_oa", "dw_w", "dw_b", "cln_g",
                "cln_b", "w_ob", "w_o", "ln1_g", "ln1_b", "w_router", "router_bias", "w_gate_e",
                "w_up_e", "w_down_e", "w_sh_gate", "w_sh_up", "w_sh_down", "ln2_g", "ln2_b")


_EXPERT_WEIGHTS = ("w_in", "w_gate_e", "w_up_e", "w_down_e")


def kernel(x, w_in, conv_qkv, a_log, dt_bias, o_norm_g, w_oa, dw_w, dw_b, cln_g, cln_b, w_ob, w_o,
           ln1_g, ln1_b, w_router, router_bias, w_gate_e, w_up_e, w_down_e, w_sh_gate, w_sh_up,
           w_sh_down, ln2_g, ln2_b):
    stacked = dict(zip(_PARAM_NAMES, (w_in, conv_qkv, a_log, dt_bias, o_norm_g, w_oa, dw_w, dw_b, cln_g,
                                      cln_b, w_ob, w_o, ln1_g, ln1_b, w_router, router_bias, w_gate_e,
                                      w_up_e, w_down_e, w_sh_gate, w_sh_up, w_sh_down, ln2_g, ln2_b)))
    batch, seq, d = x.shape
    assert d == D_MODEL and seq % CHUNK == 0
    xf = x.reshape(batch * seq, d).astype(F32)
    for layer in range(w_in.shape[0]):
        p = {name: arr[layer] for name, arr in stacked.items() if name not in _EXPERT_WEIGHTS}
        xf = _layer(xf, p, stacked, layer, batch, seq)
    return xf.reshape(batch, seq, d).astype(x.dtype)
```

```python
import functools

import jax
import jax.numpy as jnp
import numpy as np
from jax import lax
from jax.experimental import pallas as pl
from jax.experimental.pallas import tpu as pltpu
from jax.experimental.pallas import tpu_sc as plsc

F32 = jnp.float32
BF16 = jnp.bfloat16
I32 = jnp.int32
U32 = jnp.uint32
HI_MASK = np.uint32(0xFFFF0000)

D_MODEL = 1024
GDN_HEADS = 8
HEAD_DIM = 128
GDN_WIDTH = GDN_HEADS * HEAD_DIM
SHORT_CONV = 4
CONV_HALO = 16
CHUNK = 64
SOLVE_BLOCK = 16
GDN_TILE = 256
SCAN_CHUNKS = 4
SCAN_SEQS = 2
CONF_CH = D_MODEL
CONF_KERNEL = 31
CONF_HALO = 32
N_EXPERTS = 64
TOP_K = 8
N_GROUPS = 8
GROUP_SIZE = N_EXPERTS // N_GROUPS
TOPK_GROUPS = 4
EXPERT_FF = 256
SHARED_FF = 256
ROUTED_SCALE = 2.5
DEPTH = 2
DEEPNORM_ALPHA = (2 * DEPTH) ** 0.25
EPS = 1e-6

LANES = 128
PROJ_COLS = 8 * D_MODEL
ROW_BLOCK = 2048
EXPERT_SUB = 256
EXPERT_IN_SLOTS = 3
SC_SCATTER_ROWS = 128
SC_GATHER_ROWS = 64
VMEM_LIMIT = 56 * 1024 * 1024
REGROUP_ROWS = 512
PROJ_ROWS = 2048
PROJ_TILE_COLS = 2048
MIXER_ROWS = 256
MIXER_CONV_ROWS = 32
ROUTE_TOKENS = 512
DEST_TOKENS = 2048
SHARED_ROWS = 512
COMBINE_ROWS = 256


def _params(*sem):
    return pltpu.CompilerParams(dimension_semantics=sem, vmem_limit_bytes=VMEM_LIMIT)


def _dot(a, b):
    return jnp.dot(a, b, preferred_element_type=F32)


def _dot_nt(a, b):
    return lax.dot_general(a, b, (((1,), (1,)), ((), ())), preferred_element_type=F32)


def _split(a):
    hi = a.astype(BF16)
    lo = (a - hi.astype(F32)).astype(BF16)
    return hi, lo


def _sigmoid(x):
    return 1.0 / (1.0 + jnp.exp(-x))


def _silu(x):
    return x * _sigmoid(x)


def _layer_norm(y, g, b):
    mu = jnp.mean(y, axis=-1, keepdims=True)
    yc = y - mu
    var = jnp.mean(yc * yc, axis=-1, keepdims=True)
    return yc * lax.rsqrt(var + EPS) * g + b


def _proj_kernel(a_ref, w_ref, o_ref):
    o_ref[...] = _dot_nt(a_ref[...].astype(BF16), w_ref[...]).astype(o_ref.dtype)


def _regroup_kernel(wt_hbm, main_ref, ahi_ref, alo_ref, bhi_ref, blo_ref, buf, ab_buf, sem, ab_sem, *, layer, tr):
    i = pl.program_id(0)
    half = PROJ_COLS // 2
    skip = 2 * GDN_HEADS

    def rows_copy(j, slot):
        start = pl.multiple_of(jnp.where(j < half // tr, 0, skip) + j * tr, skip)
        return pltpu.make_async_copy(wt_hbm.at[layer, pl.ds(start, tr), :], buf.at[slot], sem.at[slot])

    @pl.when(i == 0)
    def _():
        rows_copy(0, 0).start()

    @pl.when(i + 1 < pl.num_programs(0))
    def _():
        rows_copy(i + 1, lax.rem(i + 1, 2)).start()

    slot = lax.rem(i, 2)
    rows_copy(i, slot).wait()
    main_ref[...] = buf[slot].astype(BF16)

    @pl.when(i == 0)
    def _():
        cp_ab = pltpu.make_async_copy(wt_hbm.at[layer, pl.ds(half, LANES), :], ab_buf, ab_sem)
        cp_ab.start()
        cp_ab.wait()
        cols = ab_buf[...].T
        wide = jnp.concatenate([cols, jnp.zeros_like(cols)], axis=1)
        lane = lax.broadcasted_iota(I32, cols.shape, 1)
        for off, hi_ref, lo_ref in ((0, ahi_ref, alo_ref), (GDN_HEADS, bhi_ref, blo_ref)):
            hi, lo = _split(jnp.where(lane < GDN_HEADS, wide[:, off:off + LANES], 0.0))
            hi_ref[...] = hi
            lo_ref[...] = lo


def _regroup(w_in_t, layer, tr):
    d = w_in_t.shape[2]
    small = jax.ShapeDtypeStruct((d, LANES), BF16)
    whole = lambda i: (0, 0)
    return pl.pallas_call(
        functools.partial(_regroup_kernel, layer=layer, tr=tr),
        grid=(PROJ_COLS // tr,),
        in_specs=[pl.BlockSpec(memory_space=pl.ANY)],
        out_specs=[pl.BlockSpec((tr, d), lambda i: (i, 0))] + [pl.BlockSpec((d, LANES), whole)] * 4,
        out_shape=[jax.ShapeDtypeStruct((PROJ_COLS, d), BF16), small, small, small, small],
        scratch_shapes=[pltpu.VMEM((2, tr, d), F32), pltpu.VMEM((LANES, d), F32),
                        pltpu.SemaphoreType.DMA((2,)), pltpu.SemaphoreType.DMA],
        compiler_params=_params("arbitrary"),
        name="regroup",
    )(w_in_t)


def _proj(xb, w, tm, tn):
    m, k = xb.shape
    n = w.shape[0]
    return pl.pallas_call(
        _proj_kernel,
        grid=(m // tm, n // tn),
        in_specs=[pl.BlockSpec((tm, k), lambda i, j: (i, 0)),
                  pl.BlockSpec((tn, k), lambda i, j: (j, 0))],
        out_specs=pl.BlockSpec((tm, tn), lambda i, j: (i, j)),
        out_shape=jax.ShapeDtypeStruct((m, n), BF16),
        compiler_params=_params("parallel", "parallel"),
        name="proj",
    )(xb, w)


def _unit_lower_inverse4(als):
    c = CHUNK
    n = als[0].shape[1]
    row = lax.broadcasted_iota(I32, (c, n), 0)
    col = jnp.bitwise_and(lax.broadcasted_iota(I32, (c, n), 1), c - 1)
    shift = SOLVE_BLOCK.bit_length() - 1
    same = jnp.right_shift(row, shift) == jnp.right_shift(col, shift)
    eye = (row == col).astype(F32)
    cshift = c.bit_length() - 1
    brow = jnp.right_shift(lax.broadcasted_iota(I32, (n, n), 0), cshift)
    bcol = jnp.right_shift(lax.broadcasted_iota(I32, (n, n), 1), cshift)
    on_diag = brow == bcol

    def mm(x, y):
        yb = y.astype(BF16)
        bd = jnp.where(on_diag, jnp.concatenate([yb] * (n // c), axis=0), jnp.zeros((), BF16))
        return _dot(x.astype(BF16), bd)

    a_diag = [jnp.where(same, al, 0.0) for al in als]
    a_off = [al - ad for al, ad in zip(als, a_diag)]
    bp = [-ad for ad in a_diag]
    p = [eye + b for b in bp]
    for _ in range(3):
        bp = [mm(b, b) for b in bp]
        p = [x + mm(x, b) for x, b in zip(p, bp)]
    n1 = [mm(x, ao) for x, ao in zip(p, a_off)]
    n2 = [mm(x, x) for x in n1]
    q = [x + mm(y, x) for x, y in zip(p, n2)]
    return [x - mm(y, x) for x, y in zip(q, n1)]


def _shift_selectors(rt):
    sel = np.zeros((SHORT_CONV * rt, rt), np.float32)
    sel_halo = np.zeros((SHORT_CONV * 8, CONV_HALO), np.float32)
    for d in range(SHORT_CONV):
        for t in range(d, rt):
            sel[d * rt + t, t - d] = 1.0
        for t in range(d):
            sel_halo[d * 8 + t, CONV_HALO + t - d] = 1.0
    return jnp.asarray(sel, BF16), jnp.asarray(sel_halo, BF16)


def _gdn_intra_kernel(qkv_ref, prev_ref, x_ref, sel_ref, selh_ref, wah_ref, wal_ref, wbh_ref, wbl_ref, cw_ref,
                      prm_ref, u_ref, w_ref, qd_ref, m2_ref, eg_ref, xs_ref, *, rt, tiles_per_seq):
    c = CHUNK
    nc = rt // c
    first = (pl.program_id(0) % tiles_per_seq) == 0
    edge = jnp.where(first, 0.0, _dot(selh_ref[...], prev_ref[...]))

    def move_rows(lo, hi):
        moved = _dot(sel_ref[...], qkv_ref[:, lo:hi])
        for d in range(SHORT_CONV):
            xs_ref[d, 0:8, lo:hi] = moved[d * rt:d * rt + 8] + edge[d * 8:(d + 1) * 8, lo:hi]
            xs_ref[d, 8:rt, lo:hi] = moved[d * rt + 8:(d + 1) * rt]

    for part in range(3):
        move_rows(part * GDN_WIDTH, part * GDN_WIDTH + 2 * HEAD_DIM)

    xh, xl = _split(x_ref[...])

    def proj3(wh_ref, wl_ref):
        wh = wh_ref[...]
        return _dot(xh, wh) + _dot(xl, wh) + _dot(xh, wl_ref[...])

    a_raw = proj3(wah_ref, wal_ref)
    b_raw = proj3(wbh_ref, wbl_ref)
    sp_in = a_raw + prm_ref[1:2, :]
    softplus = jnp.maximum(sp_in, 0.0) + jnp.log(1.0 + jnp.exp(-jnp.abs(sp_in)))
    g = -jnp.exp(prm_ref[0:1, :]) * softplus
    beta = _sigmoid(b_raw)

    cshift = c.bit_length() - 1
    r2 = lax.broadcasted_iota(I32, (rt, rt), 0)
    c2 = lax.broadcasted_iota(I32, (rt, rt), 1)
    same_chunk = jnp.right_shift(r2, cshift) == jnp.right_shift(c2, cshift)
    ltri = jnp.where(r2 >= c2, jnp.where(same_chunk, 1.0, 0.0), 0.0).astype(BF16)
    g_hi = g.astype(BF16)
    g_r = g - g_hi.astype(F32)
    g_mid = g_r.astype(BF16)
    g_lo = (g_r - g_mid.astype(F32)).astype(BF16)
    gc = _dot(ltri, g_hi) + _dot(ltri, g_mid) + _dot(ltri, g_lo)
    gct = gc.T
    egc = jnp.exp(gc)
    gend = jnp.concatenate(
        [jnp.broadcast_to(gc[ci * c + c - 1:ci * c + c, :], (c, LANES)) for ci in range(nc)], axis=0)
    kfac = jnp.exp(gend - gc)
    bege = beta * egc
    for ci in range(nc):
        last = ci * c + c - 1
        eg_ref[ci * GDN_HEADS:(ci + 1) * GDN_HEADS, :] = jnp.broadcast_to(
            jnp.exp(gct[0:GDN_HEADS, last:last + 1]), (GDN_HEADS, LANES))

    lane_t = lax.broadcasted_iota(I32, (rt, LANES), 1) < c
    lane_lo = lax.broadcasted_iota(I32, (c, LANES), 1) < c
    lcol = jnp.bitwise_and(lax.broadcasted_iota(I32, (c, LANES), 1), c - 1)
    rowi = lax.broadcasted_iota(I32, (c, LANES), 0)
    causal = rowi >= lcol
    strict = rowi > lcol

    def conv(base, h):
        lo, hi = base + h * HEAD_DIM, base + (h + 1) * HEAD_DIM
        acc = cw_ref[SHORT_CONV - 1:SHORT_CONV, lo:hi] * xs_ref[0, :, lo:hi]
        for j in range(SHORT_CONV - 1):
            acc = acc + cw_ref[j:j + 1, lo:hi] * xs_ref[SHORT_CONV - 1 - j, :, lo:hi]
        return _silu(acc)

    a_pairs = [[None] * (GDN_HEADS // 2) for _ in range(nc)]
    rhs_pairs = [[None] * (GDN_HEADS // 2) for _ in range(nc)]
    for p in range(GDN_HEADS // 2):
        if p + 1 < GDN_HEADS // 2:
            for part in range(3):
                lo = part * GDN_WIDTH + (p + 1) * 2 * HEAD_DIM
                move_rows(lo, lo + 2 * HEAD_DIM)
        ks, kbs, qs, kds, rhss = [], [], [], [], []
        for h in (2 * p, 2 * p + 1):
            q = conv(0, h)
            k = conv(GDN_WIDTH, h)
            v = conv(2 * GDN_WIDTH, h)
            q = q * lax.rsqrt(jnp.sum(q * q, axis=-1, keepdims=True) + EPS) * (HEAD_DIM ** -0.5)
            k = k * lax.rsqrt(jnp.sum(k * k, axis=-1, keepdims=True) + EPS)
            beta_h = beta[:, h:h + 1]
            qd_ref[:, h * HEAD_DIM:(h + 1) * HEAD_DIM] = (q * egc[:, h:h + 1]).astype(BF16)
            ks.append(k)
            kbs.append(k * beta_h)
            qs.append(q)
            kds.append(k * kfac[:, h:h + 1])
            rhss.append(jnp.concatenate([v * beta_h, k * bege[:, h:h + 1]], axis=1))
        h0, h1 = 2 * p, 2 * p + 1
        gch = jnp.where(lane_t, gc[:, h0:h0 + 1], gc[:, h1:h1 + 1])
        for ci in range(nc):
            rows = slice(ci * c, (ci + 1) * c)
            wk = jnp.concatenate([ks[0][rows], ks[1][rows]], axis=0).astype(BF16)
            lhs = jnp.concatenate([kbs[0][rows], qs[0][rows], kbs[1][rows], qs[1][rows]],
                                  axis=0).astype(BF16)
            out = _dot_nt(lhs, wk)
            gcrow = jnp.concatenate([gct[h0:h0 + 1, rows], gct[h1:h1 + 1, rows]], axis=1)
            diff = gch[rows] - gcrow
            decay = jnp.where(causal, jnp.exp(jnp.where(causal, diff, 0.0)), 0.0)
            a_pairs[ci][p] = jnp.where(strict, jnp.where(lane_lo, out[0:c], out[2 * c:3 * c]) * decay, 0.0)
            qk = jnp.where(lane_lo, out[c:2 * c], out[3 * c:4 * c]) * decay
            kdt = jnp.concatenate([kds[0][rows], kds[1][rows]], axis=0).T
            m0 = ci * 3 * c
            m2_ref[m0:m0 + c, p * LANES:(p + 1) * LANES] = qk.astype(BF16)
            m2_ref[m0 + c:m0 + 3 * c, p * LANES:(p + 1) * LANES] = kdt.astype(BF16)
            rhs_pairs[ci][p] = (rhss[0][rows], rhss[1][rows])

    zeros = jnp.zeros((c, 2 * HEAD_DIM), BF16)
    ngrp = GDN_HEADS // 4
    tls = _unit_lower_inverse4(
        [jnp.concatenate([a_pairs[ci][2 * grp], a_pairs[ci][2 * grp + 1]], axis=1)
         for ci in range(nc) for grp in range(ngrp)])
    for ci in range(nc):
        rows = slice(ci * c, (ci + 1) * c)
        for grp in range(ngrp):
            tl = tls[ci * ngrp + grp]
            for j in range(2):
                p = 2 * grp + j
                r0, r1 = rhs_pairs[ci][p]
                bd = jnp.concatenate([jnp.concatenate([r0.astype(BF16), zeros], axis=1),
                                      jnp.concatenate([zeros, r1.astype(BF16)], axis=1)], axis=0)
                sol = _dot(tl[:, j * LANES:(j + 1) * LANES].astype(BF16), bd)
                for i in range(2):
                    h = 2 * p + i
                    lo, hi = h * HEAD_DIM, (h + 1) * HEAD_DIM
                    u_ref[rows, lo:hi] = sol[:, 2 * i * HEAD_DIM:(2 * i + 1) * HEAD_DIM]
                    w_ref[rows, lo:hi] = sol[:, (2 * i + 1) * HEAD_DIM:(2 * i + 2) * HEAD_DIM].astype(BF16)


def _gdn_intra(proj, x, wa_hi, wa_lo, wb_hi, wb_lo, conv_w, prm, seq, rt):
    t = x.shape[0]
    nc = rt // CHUNK
    kern = functools.partial(_gdn_intra_kernel, rt=rt, tiles_per_seq=seq // rt)
    full = lambda i: (0, 0)
    tile = lambda i: (i, 0)
    sel, sel_halo = _shift_selectors(rt)
    return pl.pallas_call(
        kern,
        grid=(t // rt,),
        in_specs=[
            pl.BlockSpec((rt, 3 * GDN_WIDTH), tile),
            pl.BlockSpec((CONV_HALO, 3 * GDN_WIDTH),
                         lambda i: (jnp.maximum(i * (rt // CONV_HALO) - 1, 0), 0)),
            pl.BlockSpec((rt, D_MODEL), tile),
            pl.BlockSpec(sel.shape, full),
            pl.BlockSpec(sel_halo.shape, full),
            pl.BlockSpec((D_MODEL, LANES), full),
            pl.BlockSpec((D_MODEL, LANES), full),
            pl.BlockSpec((D_MODEL, LANES), full),
            pl.BlockSpec((D_MODEL, LANES), full),
            pl.BlockSpec((SHORT_CONV, 3 * GDN_WIDTH), full),
            pl.BlockSpec((8, LANES), full),
        ],
        out_specs=[pl.BlockSpec((rt, GDN_WIDTH), tile),
                   pl.BlockSpec((rt, GDN_WIDTH), tile),
                   pl.BlockSpec((rt, GDN_WIDTH), tile),
                   pl.BlockSpec((nc * 3 * CHUNK, GDN_HEADS // 2 * LANES), tile),
                   pl.BlockSpec((nc * GDN_HEADS, LANES), tile)],
        out_shape=[jax.ShapeDtypeStruct((t, GDN_WIDTH), F32),
                   jax.ShapeDtypeStruct((t, GDN_WIDTH), BF16),
                   jax.ShapeDtypeStruct((t, GDN_WIDTH), BF16),
                   jax.ShapeDtypeStruct((t // CHUNK * 3 * CHUNK, GDN_HEADS // 2 * LANES), BF16),
                   jax.ShapeDtypeStruct((t // CHUNK * GDN_HEADS, LANES), F32)],
        scratch_shapes=[pltpu.VMEM((SHORT_CONV, rt, 3 * GDN_WIDTH), F32)],
        compiler_params=_params("parallel"),
        name="gdn_intra",
    )(proj, proj, x, sel, sel_halo, wa_hi, wa_lo, wb_hi, wb_lo, conv_w, prm)


def _gdn_scan_kernel(u_ref, w_ref, qd_ref, m2_ref, eg_ref, z_ref, ong_ref, o_ref, s_ref, *, nck):
    c = CHUNK
    nseq = u_ref.shape[0]

    @pl.when(pl.program_id(1) == 0)
    def _():
        s_ref[...] = jnp.zeros_like(s_ref)

    ong = ong_ref[...]
    zeros = jnp.zeros((c, HEAD_DIM), BF16)
    span = lambda h: slice(h * HEAD_DIM, (h + 1) * HEAD_DIM)
    chains = [(b, h) for b in range(nseq) for h in range(GDN_HEADS)]
    states = {bh: s_ref[bh[0], bh[1]] for bh in chains}
    for ci in range(nck):
        rows = slice(ci * c, (ci + 1) * c)
        rs = {(b, h): _dot(jnp.concatenate([w_ref[b, rows, span(h)], qd_ref[b, rows, span(h)]], axis=0),
                           states[b, h].astype(BF16)) for b, h in chains}
        v_new = {(b, h): (u_ref[b, rows, span(h)] - rs[b, h][:c]).astype(BF16) for b, h in chains}
        r2s = {}
        for b in range(nseq):
            for p in range(GDN_HEADS // 2):
                bd = jnp.concatenate([jnp.concatenate([v_new[b, 2 * p], zeros], axis=1),
                                      jnp.concatenate([zeros, v_new[b, 2 * p + 1]], axis=1)], axis=0)
                r2s[b, p] = _dot(m2_ref[b, ci * 3 * c:(ci + 1) * 3 * c, p * LANES:(p + 1) * LANES], bd)
        for b, h in chains:
            half = span(h % 2)
            r2 = r2s[b, h // 2]
            decay = eg_ref[b, ci * GDN_HEADS + h:ci * GDN_HEADS + h + 1, :]
            states[b, h] = states[b, h] * decay + r2[c:, half]
            o = rs[b, h][c:] + r2[:c, half]
            o = o * lax.rsqrt(jnp.mean(o * o, axis=-1, keepdims=True) + EPS) * ong
            o = o * _silu(z_ref[b, rows, span(h)].astype(F32))
            o_ref[b, rows, span(h)] = o.astype(o_ref.dtype)
    for b, h in chains:
        s_ref[b, h] = states[b, h]


def _gdn_scan(u, w, qd, m2, eg, proj, ong, batch, seq):
    t = batch * seq
    nck = min(SCAN_CHUNKS, seq // CHUNK)
    c = nck * CHUNK
    nch = seq // c
    ns = SCAN_SEQS if batch % SCAN_SEQS == 0 else 1
    groups = batch // ns
    split = lambda a: a.reshape(ns, a.shape[0] // ns, a.shape[1])
    blk = lambda b, n: (0, b * nch + n, 0)
    out = pl.pallas_call(
        functools.partial(_gdn_scan_kernel, nck=nck),
        grid=(groups, nch),
        in_specs=[
            pl.BlockSpec((ns, c, GDN_WIDTH), blk),
            pl.BlockSpec((ns, c, GDN_WIDTH), blk),
            pl.BlockSpec((ns, c, GDN_WIDTH), blk),
            pl.BlockSpec((ns, 3 * c, GDN_HEADS // 2 * LANES), blk),
            pl.BlockSpec((ns, nck * GDN_HEADS, LANES), blk),
            pl.BlockSpec((ns, c, GDN_WIDTH), lambda b, n: (0, b * nch + n, 3)),
            pl.BlockSpec((1, HEAD_DIM), lambda b, n: (0, 0)),
        ],
        out_specs=pl.BlockSpec((ns, c, GDN_WIDTH), blk),
        out_shape=jax.ShapeDtypeStruct((ns, t // ns, GDN_WIDTH), BF16),
        scratch_shapes=[pltpu.VMEM((ns, GDN_HEADS, HEAD_DIM, HEAD_DIM), F32)],
        compiler_params=_params("parallel", "arbitrary"),
        name="gdn_scan",
    )(split(u), split(w), split(qd), split(m2), split(eg), split(proj), ong)
    return out.reshape(t, GDN_WIDTH)


def _pack_bf16_pairs(y):
    n = y.shape[1] // 2
    yb = y.astype(BF16).astype(F32)
    lo = lax.bitcast_convert_type(yb[:, :n], U32)
    hi = lax.bitcast_convert_type(yb[:, n:], U32)
    return jnp.bitwise_or(jnp.right_shift(lo, 16), jnp.bitwise_and(hi, HI_MASK))


def _unpack_bf16_pairs(w):
    lo = lax.bitcast_convert_type(jnp.left_shift(w, 16), F32)
    hi = lax.bitcast_convert_type(jnp.bitwise_and(w, HI_MASK), F32)
    return lo, hi


def _mixer_kernel(glu_ref, ga_ref, gb_ref, o_ref, x_ref, woa_ref, wob_ref, wo_ref, dww_ref, vec_ref,
                  x1_ref, x1b_ref, x1p_ref, ubuf_ref, sh_ref, conv_ref, *, ts, rc):
    halo = CONF_HALO

    @pl.when(pl.program_id(1) == 0)
    def _():
        ubuf_ref[0:halo, :] = jnp.zeros((halo, CONF_CH), F32)

    @pl.when(pl.program_id(1) != 0)
    def _():
        ubuf_ref[0:halo, :] = ubuf_ref[ts:ts + halo, :]

    glu_a = glu_ref[:, :CONF_CH].astype(F32)
    glu_b = glu_ref[:, CONF_CH:].astype(F32)
    ubuf_ref[halo:halo + ts, :] = glu_a * _sigmoid(glu_b)

    dw_b = vec_ref[0:1, :]
    cln_g = vec_ref[1:2, :]
    cln_b = vec_ref[2:3, :]
    ln1_g = vec_ref[3:4, :]
    ln1_b = vec_ref[4:5, :]

    span = ts + halo - 8
    for s in range(1, 8):
        sh_ref[s - 1] = ubuf_ref[s:s + span, :]

    def tap_rows(j, r0):
        o = halo - (CONF_KERNEL - 1) + j
        q, s = o // 8, o % 8
        if s == 0:
            return ubuf_ref[r0 + o:r0 + o + rc, :]
        return sh_ref[s - 1, r0 + 8 * q:r0 + 8 * q + rc, :]

    hs = ts // 2
    for h0 in range(0, ts, hs):
        rows = slice(h0, h0 + hs)
        gated_a = _sigmoid(ga_ref[rows, :].astype(F32)) * _dot(o_ref[rows, :], woa_ref[...])
        for r0 in range(h0, h0 + hs, rc):
            tap_w = lambda j: jnp.concatenate([dww_ref[j]] * (rc // 8), axis=0)
            acc = tap_w(0) * tap_rows(0, r0)
            for j in range(1, CONF_KERNEL):
                acc = acc + tap_w(j) * tap_rows(j, r0)
            conv_ref[r0:r0 + rc, :] = acc
        uc = _silu(_layer_norm(conv_ref[rows, :] + dw_b, cln_g, cln_b))
        branch_b = _dot(uc.astype(BF16), wob_ref[...])
        hmix = gated_a + _sigmoid(gb_ref[rows, :].astype(F32)) * branch_b
        mix = _dot(hmix.astype(BF16), wo_ref[...])
        x1 = _layer_norm(DEEPNORM_ALPHA * x_ref[rows, :] + mix, ln1_g, ln1_b)
        x1_ref[rows, :] = x1
        x1b_ref[rows, :] = x1.astype(BF16)
        x1p_ref[rows, :] = _pack_bf16_pairs(x1)


def _mixer(proj, o_gdn, x, woa, wob, wo, dww, vec, batch, seq, ts):
    t = batch * seq
    nt = seq // ts
    rows = lambda b, n: b * nt + n
    full = lambda b, n: (0, 0)
    kern = functools.partial(_mixer_kernel, ts=ts, rc=MIXER_CONV_ROWS)
    return pl.pallas_call(
        kern,
        grid=(batch, nt),
        in_specs=[
            pl.BlockSpec((ts, 2 * CONF_CH), lambda b, n: (rows(b, n), 2)),
            pl.BlockSpec((ts, D_MODEL), lambda b, n: (rows(b, n), 6)),
            pl.BlockSpec((ts, D_MODEL), lambda b, n: (rows(b, n), 7)),
            pl.BlockSpec((ts, GDN_WIDTH), lambda b, n: (rows(b, n), 0)),
            pl.BlockSpec((ts, D_MODEL), lambda b, n: (rows(b, n), 0)),
            pl.BlockSpec((GDN_WIDTH, D_MODEL), full),
            pl.BlockSpec((CONF_CH, D_MODEL), full),
            pl.BlockSpec((D_MODEL, D_MODEL), full),
            pl.BlockSpec((CONF_KERNEL, 8, CONF_CH), lambda b, n: (0, 0, 0)),
            pl.BlockSpec((8, D_MODEL), full),
        ],
        out_specs=[pl.BlockSpec((ts, D_MODEL), lambda b, n: (rows(b, n), 0)),
                   pl.BlockSpec((ts, D_MODEL), lambda b, n: (rows(b, n), 0)),
                   pl.BlockSpec((ts, D_MODEL // 2), lambda b, n: (rows(b, n), 0))],
        out_shape=[jax.ShapeDtypeStruct((t, D_MODEL), F32),
                   jax.ShapeDtypeStruct((t, D_MODEL), BF16),
                   jax.ShapeDtypeStruct((t, D_MODEL // 2), U32)],
        scratch_shapes=[pltpu.VMEM((CONF_HALO + ts, CONF_CH), F32),
                        pltpu.VMEM((7, ts + CONF_HALO - 8, CONF_CH), F32),
                        pltpu.VMEM((ts, CONF_CH), F32)],
        compiler_params=_params("parallel", "arbitrary"),
        name="mixer",
    )(proj, proj, proj, o_gdn, x, woa, wob, wo, dww, vec)


def _route_kernel(x_ref, wrh_ref, wrl_ref, bias_ref, eidx_ref, wts_ref, rank_ref, cnt_ref, carry_ref, *, tt):
    @pl.when(pl.program_id(0) == 0)
    def _():
        carry_ref[...] = jnp.zeros_like(carry_ref)

    xh, xl = _split(x_ref[...])
    wrh = wrh_ref[...]
    logits = _dot_nt(wrh, xh) + _dot_nt(wrh, xl) + _dot_nt(wrl_ref[...], xh)
    s = _sigmoid(logits)
    biased = s + bias_ref[...]

    sub = lax.broadcasted_iota(I32, (GROUP_SIZE, tt), 0)
    groups = [biased[g * GROUP_SIZE:(g + 1) * GROUP_SIZE, :] for g in range(N_GROUPS)]
    gs = []
    for bg in groups:
        m1 = jnp.max(bg, axis=0, keepdims=True)
        first = jnp.min(jnp.where(bg == m1, sub, GROUP_SIZE), axis=0, keepdims=True)
        m2 = jnp.max(jnp.where(sub == first, -jnp.inf, bg), axis=0, keepdims=True)
        gs.append(m1 + m2)

    masked_parts = []
    for g in range(N_GROUPS):
        beaten = jnp.zeros((1, tt), I32)
        for o in range(N_GROUPS):
            if o == g:
                continue
            wins = (gs[o] >= gs[g]) if o < g else (gs[o] > gs[g])
            beaten = beaten + wins.astype(I32)
        keep = jnp.broadcast_to(beaten < TOPK_GROUPS, (GROUP_SIZE, tt))
        masked_parts.append(jnp.where(keep, groups[g], -jnp.inf))
    masked = jnp.concatenate(masked_parts, axis=0)

    eiota = lax.broadcasted_iota(I32, (N_EXPERTS, tt), 0)
    sel_all = jnp.zeros((N_EXPERTS, tt), F32)
    picks = []
    for _ in range(TOP_K):
        m = jnp.max(masked, axis=0, keepdims=True)
        idx = jnp.min(jnp.where(masked == m, eiota, N_EXPERTS), axis=0, keepdims=True)
        onehot = eiota == idx
        picks.append((idx, onehot))
        sel_all = jnp.where(onehot, 1.0, sel_all)
        masked = jnp.where(onehot, -jnp.inf, masked)

    tr = lax.broadcasted_iota(I32, (tt, tt), 0)
    tc = lax.broadcasted_iota(I32, (tt, tt), 1)
    before = (tr < tc).astype(BF16)
    sel_b = sel_all.astype(BF16)
    carry = carry_ref[...]
    rank_all = _dot(sel_b, before) + carry[:, 0:1]
    carry_new = carry + _dot(sel_b, jnp.ones((tt, LANES), BF16))
    carry_ref[...] = carry_new
    cnt_ref[...] = carry_new

    s_sel = [jnp.sum(jnp.where(oh, s, 0.0), axis=0, keepdims=True) for _, oh in picks]
    total = s_sel[0]
    for v in s_sel[1:]:
        total = total + v
    for k, (idx, oh) in enumerate(picks):
        eidx_ref[k:k + 1, :] = idx
        wts_ref[k:k + 1, :] = s_sel[k] / total * ROUTED_SCALE
        rank_ref[k:k + 1, :] = jnp.sum(jnp.where(oh, rank_all, 0.0), axis=0, keepdims=True).astype(I32)


def _route(x1, wr_hi, wr_lo, bias, tt):
    t = x1.shape[0]
    kern = functools.partial(_route_kernel, tt=tt)
    return pl.pallas_call(
        kern,
        grid=(t // tt,),
        in_specs=[pl.BlockSpec((tt, D_MODEL), lambda i: (i, 0)),
                  pl.BlockSpec((N_EXPERTS, D_MODEL), lambda i: (0, 0)),
                  pl.BlockSpec((N_EXPERTS, D_MODEL), lambda i: (0, 0)),
                  pl.BlockSpec((N_EXPERTS, tt), lambda i: (0, 0))],
        out_specs=[pl.BlockSpec((TOP_K, tt), lambda i: (0, i)),
                   pl.BlockSpec((TOP_K, tt), lambda i: (0, i)),
                   pl.BlockSpec((TOP_K, tt), lambda i: (0, i)),
                   pl.BlockSpec((N_EXPERTS, LANES), lambda i: (0, 0))],
        out_shape=[jax.ShapeDtypeStruct((TOP_K, t), I32),
                   jax.ShapeDtypeStruct((TOP_K, t), F32),
                   jax.ShapeDtypeStruct((TOP_K, t), I32),
                   jax.ShapeDtypeStruct((N_EXPERTS, LANES), F32)],
        scratch_shapes=[pltpu.VMEM((N_EXPERTS, LANES), F32)],
        compiler_params=_params("arbitrary"),
        name="route",
    )(x1, wr_hi, wr_lo, bias)


def _plan_kernel(cnt_ref, pstart_ref, plan_ref):
    e, nb = N_EXPERTS, plan_ref.shape[1]
    counts = cnt_ref[...]
    nblk = jnp.floor((counts + (ROW_BLOCK - 1)) * (1.0 / ROW_BLOCK))
    hi = jnp.floor(nblk * (1.0 / 256.0))
    lo = nblk - 256.0 * hi
    r = lax.broadcasted_iota(I32, (e, e), 0)
    c = lax.broadcasted_iota(I32, (e, e), 1)
    ltri = (r >= c).astype(BF16)
    bend = 256.0 * _dot(ltri, hi.astype(BF16)) + _dot(ltri, lo.astype(BF16))
    pend = bend * ROW_BLOCK
    pstart = pend - nblk * ROW_BLOCK
    pstart_ref[...] = pstart.astype(I32)

    bs = (lax.broadcasted_iota(I32, (e, nb), 1) * ROW_BLOCK).astype(F32)
    pend_b = jnp.broadcast_to(pend[:, 0:1], (e, nb))
    pstart_b = jnp.broadcast_to(pstart[:, 0:1], (e, nb))
    used_b = jnp.broadcast_to((pstart + counts)[:, 0:1], (e, nb))
    owner = jnp.sum(jnp.where(pend_b <= bs, 1.0, 0.0), axis=0, keepdims=True)
    inside = jnp.where(pstart_b <= bs, jnp.where(bs < pend_b, 1.0, 0.0), 0.0)
    real = jnp.sum(inside * jnp.clip(used_b - bs, 0.0, float(ROW_BLOCK)), axis=0, keepdims=True)
    plan_ref[0:1, :] = jnp.minimum(owner, float(e - 1)).astype(I32)
    plan_ref[1:2, :] = real.astype(I32)
    plan_ref[2:8, :] = jnp.zeros((6, nb), I32)


def _plan(cnt, n_blocks):
    nb = -(-n_blocks // LANES) * LANES
    pstart, plan = pl.pallas_call(
        _plan_kernel,
        out_shape=[jax.ShapeDtypeStruct((N_EXPERTS, LANES), I32),
                   jax.ShapeDtypeStruct((8, nb), I32)],
        name="plan",
    )(cnt)
    return pstart[:, 0], plan[0, :n_blocks], plan[1, :n_blocks]


def _dest_kernel(eidx_ref, rank_ref, pstart_ref, dest_ref):
    eidx = eidx_ref[...]
    acc = rank_ref[...]
    for e in range(N_EXPERTS):
        acc = acc + jnp.where(eidx == e, pstart_ref[e], 0)
    dest_ref[...] = acc


def _dest(eidx_t, rank_t, pstart, tt):
    t = eidx_t.shape[1]
    return pl.pallas_call(
        _dest_kernel,
        grid=(t // tt,),
        in_specs=[pl.BlockSpec((TOP_K, tt), lambda i: (0, i)),
                  pl.BlockSpec((TOP_K, tt), lambda i: (0, i)),
                  pl.BlockSpec(memory_space=pltpu.SMEM)],
        out_specs=pl.BlockSpec((TOP_K, tt), lambda i: (0, i)),
        out_shape=jax.ShapeDtypeStruct((TOP_K, t), I32),
        compiler_params=_params("parallel"),
        name="dest",
    )(eidx_t, rank_t, pstart)


def _dispatch(dest_kt, x1p, n_rows):
    t, width = x1p.shape
    info = plsc.get_sparse_core_info()
    nc, nw = info.num_cores, info.num_cores * info.num_subcores
    chunk = SC_SCATTER_ROWS
    per_w = t // nw
    n_chunks = per_w // chunk
    assert per_w % chunk == 0
    idx = dest_kt.reshape(TOP_K, nw, n_chunks, chunk).transpose(1, 2, 0, 3).reshape(nw, n_chunks * TOP_K, chunk)
    mesh = plsc.VectorSubcoreMesh(core_axis_name="c", subcore_axis_name="s")

    @functools.partial(
        pl.kernel, mesh=mesh, name="dispatch",
        out_type=jax.ShapeDtypeStruct((n_rows, width), x1p.dtype),
        scratch_types=[pltpu.VMEM((n_chunks * TOP_K, chunk), I32),
                       pltpu.VMEM((chunk, width), x1p.dtype),
                       pltpu.SemaphoreType.DMA])
    def scatter(x_hbm, idx_hbm, xs_hbm, idx_v, rows_v, sem):
        wid = lax.axis_index("s") * nc + lax.axis_index("c")
        base = wid * per_w
        pltpu.sync_copy(idx_hbm.at[wid], idx_v)
        for j in range(n_chunks):
            pltpu.sync_copy(x_hbm.at[pl.ds(base + j * chunk, chunk)], rows_v)
            copies = [pltpu.make_async_copy(rows_v, xs_hbm.at[idx_v.at[j * TOP_K + k]], sem)
                      for k in range(TOP_K)]
            for cp in copies:
                cp.start()
            for cp in copies:
                cp.wait()

    return scatter(x1p, idx)


def _gather_rows(table, idx):
    n = idx.shape[0]
    width = table.shape[1]
    info = plsc.get_sparse_core_info()
    nc, nw = info.num_cores, info.num_cores * info.num_subcores
    chunk = SC_GATHER_ROWS
    per_w = n // nw
    n_chunks = per_w // chunk
    assert per_w % (2 * chunk) == 0
    mesh = plsc.VectorSubcoreMesh(core_axis_name="c", subcore_axis_name="s")

    @functools.partial(
        pl.kernel, mesh=mesh, name="gather_rows",
        out_type=jax.ShapeDtypeStruct((n, width), table.dtype),
        scratch_types=[pltpu.VMEM((n_chunks, chunk), I32),
                       pltpu.VMEM((2, chunk, width), table.dtype),
                       pltpu.SemaphoreType.DMA((2,)),
                       pltpu.SemaphoreType.DMA((2,))])
    def gather(table_hbm, idx_hbm, out_hbm, idx_v, rows_v, gsem, osem):
        wid = lax.axis_index("s") * nc + lax.axis_index("c")
        base = wid * per_w
        pltpu.sync_copy(idx_hbm.at[wid], idx_v)

        def fetch(j, b):
            return pltpu.make_async_copy(table_hbm.at[idx_v.at[j]], rows_v.at[b], gsem.at[b])

        def put(j, b):
            return pltpu.make_async_copy(rows_v.at[b], out_hbm.at[pl.ds(base + j * chunk, chunk)], osem.at[b])

        fetch(0, 0).start()

        @pl.loop(0, n_chunks, step=2)
        def _(j0):
            for b in range(2):
                j = j0 + b
                fetch(j, b).wait()

                @pl.when(j + 1 < n_chunks)
                def _():
                    @pl.when(j >= 1)
                    def _():
                        put(j - 1, 1 - b).wait()

                    fetch(j + 1, 1 - b).start()

                put(j, b).start()

        put(n_chunks - 2, 0).wait()
        put(n_chunks - 1, 1).wait()

    return gather(table, idx.reshape(nw, n_chunks, chunk))


def _xs_copy(xs_hbm, xbuf, isem, j, slot):
    return pltpu.make_async_copy(xs_hbm.at[pl.ds(j * ROW_BLOCK, ROW_BLOCK)], xbuf.at[slot], isem.at[slot])


def _ys_copy(ybuf, ys_hbm, osem, j, slot):
    return pltpu.make_async_copy(ybuf.at[slot], ys_hbm.at[pl.ds(j * ROW_BLOCK, ROW_BLOCK)], osem.at[slot])


def _experts_kernel(be_ref, nv_ref, xs_hbm, wg_ref, wu_ref, wd_ref, ys_hbm,
                    xbuf, ybuf, wgu_s, wd_s, cur_ref, isem, osem):
    i = pl.program_id(0)
    n_valid = nv_ref[i]
    half = EXPERT_SUB
    slot = lax.rem(i, EXPERT_IN_SLOTS)
    oslot = lax.rem(i, 2)

    @pl.when(i == 0)
    def _():
        cur_ref[0] = -1
        for j in range(2):
            @pl.when(nv_ref[j] > 0)
            def _():
                _xs_copy(xs_hbm, xbuf, isem, j, j).start()

    @pl.when(nv_ref[i + 2] > 0)
    def _():
        _xs_copy(xs_hbm, xbuf, isem, i + 2, lax.rem(i + 2, EXPERT_IN_SLOTS)).start()

    @pl.when((i >= 2) & (nv_ref[jnp.maximum(i - 2, 0)] > 0))
    def _():
        _ys_copy(ybuf, ys_hbm, osem, i - 2, oslot).wait()

    @pl.when((n_valid > 0) & (cur_ref[0] != be_ref[i]))
    def _():
        wgu_s[:, :EXPERT_FF] = wg_ref[...].astype(BF16)
        wgu_s[:, EXPERT_FF:] = wu_ref[...].astype(BF16)
        wd_s[...] = wd_ref[...].astype(BF16)
        cur_ref[0] = be_ref[i]

    def rows_bf16(r0):
        valid = lax.broadcasted_iota(I32, (half, xbuf.shape[2]), 0) + r0 < n_valid
        lo, hi = _unpack_bf16_pairs(jnp.where(valid, xbuf[slot, r0:r0 + half, :], jnp.zeros((), U32)))
        return jnp.concatenate([lo.astype(BF16), hi.astype(BF16)], axis=1)

    def hidden(gu):
        return (_silu(gu[:, :EXPERT_FF]) * gu[:, EXPERT_FF:]).astype(BF16)

    @pl.when(n_valid > 0)
    def _():
        _xs_copy(xs_hbm, xbuf, isem, i, slot).wait()

    zeros = jnp.zeros((half, ybuf.shape[2]), ybuf.dtype)
    for r0 in range(0, ROW_BLOCK, 2 * half):
        r1 = r0 + half

        @pl.when(n_valid > r1)
        def _():
            xa, xb = rows_bf16(r0), rows_bf16(r1)
            gua = _dot(xa, wgu_s[...])
            gub = _dot(xb, wgu_s[...])
            ya = _dot(hidden(gua), wd_s[...])
            yb = _dot(hidden(gub), wd_s[...])
            ybuf[oslot, r0:r1, :] = _pack_bf16_pairs(ya)
            ybuf[oslot, r1:r1 + half, :] = _pack_bf16_pairs(yb)

        @pl.when((n_valid > r0) & (n_valid <= r1))
        def _():
            ya = _dot(hidden(_dot(rows_bf16(r0), wgu_s[...])), wd_s[...])
            ybuf[oslot, r0:r1, :] = _pack_bf16_pairs(ya)
            ybuf[oslot, r1:r1 + half, :] = zeros

        @pl.when((n_valid > 0) & (n_valid <= r0))
        def _():
            ybuf[oslot, r0:r1, :] = zeros
            ybuf[oslot, r1:r1 + half, :] = zeros

    @pl.when(n_valid > 0)
    def _():
        _ys_copy(ybuf, ys_hbm, osem, i, oslot).start()


def _experts(block_e, n_valid, xs, wg, wu, wd, layer):
    n_rows, width = xs.shape
    steps = n_rows // ROW_BLOCK + 2
    be = jnp.concatenate([block_e, jnp.full((2,), N_EXPERTS - 1, I32)])
    nv = jnp.concatenate([n_valid, jnp.zeros((4,), I32)])
    grid_spec = pltpu.PrefetchScalarGridSpec(
        num_scalar_prefetch=2,
        grid=(steps,),
        in_specs=[pl.BlockSpec(memory_space=pl.ANY),
                  pl.BlockSpec((None, None, D_MODEL, EXPERT_FF), lambda i, be, nv: (layer, be[i], 0, 0)),
                  pl.BlockSpec((None, None, D_MODEL, EXPERT_FF), lambda i, be, nv: (layer, be[i], 0, 0)),
                  pl.BlockSpec((None, None, EXPERT_FF, D_MODEL), lambda i, be, nv: (layer, be[i], 0, 0))],
        out_specs=pl.BlockSpec(memory_space=pl.ANY),
        scratch_shapes=[pltpu.VMEM((EXPERT_IN_SLOTS, ROW_BLOCK, width), xs.dtype),
                        pltpu.VMEM((2, ROW_BLOCK, width), xs.dtype),
                        pltpu.VMEM((D_MODEL, 2 * EXPERT_FF), BF16),
                        pltpu.VMEM((EXPERT_FF, D_MODEL), BF16),
                        pltpu.SMEM((1,), I32),
                        pltpu.SemaphoreType.DMA((EXPERT_IN_SLOTS,)),
                        pltpu.SemaphoreType.DMA((2,))],
    )
    return pl.pallas_call(
        _experts_kernel,
        grid_spec=grid_spec,
        out_shape=jax.ShapeDtypeStruct((n_rows, width), xs.dtype),
        compiler_params=_params("arbitrary"),
        name="experts",
    )(be, nv, xs, wg, wu, wd)


def _shared_kernel(x1b_ref, wsg_ref, wsu_ref, wsd_ref, anchor_ref, sh_ref):
    del anchor_ref
    xb = x1b_ref[...]
    hid = (_silu(_dot(xb, wsg_ref[...])) * _dot(xb, wsu_ref[...])).astype(BF16)
    sh_ref[...] = _dot(hid, wsd_ref[...]).astype(sh_ref.dtype)


def _shared(x1b, wsg, wsu, wsd, anchor, tt, row0, nrows):
    full = lambda i: (0, 0)
    first = row0 // tt
    return pl.pallas_call(
        _shared_kernel,
        grid=(nrows // tt,),
        in_specs=[pl.BlockSpec((tt, D_MODEL), lambda i: (first + i, 0)),
                  pl.BlockSpec((D_MODEL, SHARED_FF), full),
                  pl.BlockSpec((D_MODEL, SHARED_FF), full),
                  pl.BlockSpec((SHARED_FF, D_MODEL), full),
                  pl.BlockSpec(memory_space=pl.ANY)],
        out_specs=pl.BlockSpec((tt, D_MODEL), lambda i: (i, 0)),
        out_shape=jax.ShapeDtypeStruct((nrows, D_MODEL), BF16),
        compiler_params=_params("parallel"),
        name="shared",
    )(x1b, wsg, wsu, wsd, anchor)


def _combine_kernel(yg_ref, w_ref, x1_ref, sha_ref, shb_ref, vec_ref, x2_ref, *, half_steps):
    in_first = pl.program_id(0) < half_steps
    shared = jnp.where(in_first, sha_ref[...], shb_ref[...]).astype(F32)
    w = w_ref[...]
    half = D_MODEL // 2
    acc_lo, acc_hi = shared[:, :half], shared[:, half:]
    for k in range(TOP_K):
        lo, hi = _unpack_bf16_pairs(yg_ref[k])
        acc_lo = acc_lo + w[:, k:k + 1] * lo
        acc_hi = acc_hi + w[:, k:k + 1] * hi
    acc = jnp.concatenate([acc_lo, acc_hi], axis=1)
    x2 = _layer_norm(DEEPNORM_ALPHA * x1_ref[...] + acc, vec_ref[0:1, :], vec_ref[1:2, :])
    x2_ref[...] = x2


def _combine(yg, w_tok, x1, shared_a, shared_b, vec, tt):
    t = x1.shape[0]
    half_steps = shared_a.shape[0] // tt
    full = lambda i: (0, 0)
    return pl.pallas_call(
        functools.partial(_combine_kernel, half_steps=half_steps),
        grid=(t // tt,),
        in_specs=[pl.BlockSpec((TOP_K, tt, yg.shape[2]), lambda i: (0, i, 0)),
                  pl.BlockSpec((tt, TOP_K), lambda i: (i, 0)),
                  pl.BlockSpec((tt, D_MODEL), lambda i: (i, 0)),
                  pl.BlockSpec((tt, D_MODEL), lambda i: (jnp.minimum(i, half_steps - 1), 0)),
                  pl.BlockSpec((tt, D_MODEL), lambda i: (jnp.maximum(i - half_steps, 0), 0)),
                  pl.BlockSpec((8, D_MODEL), full)],
        out_specs=pl.BlockSpec((tt, D_MODEL), lambda i: (i, 0)),
        out_shape=jax.ShapeDtypeStruct((t, D_MODEL), F32),
        compiler_params=_params("parallel"),
        name="combine",
    )(yg, w_tok, x1, shared_a, shared_b, vec)


def _pad_rows(a, rows):
    return jnp.zeros((rows, a.shape[-1]), F32).at[:a.shape[0]].set(a.astype(F32))


def _layer(x, p, stacked, layer, batch, seq):
    t = batch * seq
    w_main, wa_hi, wa_lo, wb_hi, wb_lo = _regroup(jnp.swapaxes(stacked["w_in"], 1, 2), layer, REGROUP_ROWS)
    proj = _proj(x, w_main, min(PROJ_ROWS, t), PROJ_TILE_COLS)

    prm = jnp.zeros((8, LANES), F32)
    prm = prm.at[0, :GDN_HEADS].set(p["a_log"]).at[1, :GDN_HEADS].set(p["dt_bias"])
    ong = p["o_norm_g"].reshape(1, HEAD_DIM).astype(F32)
    u, w, qd, m2, eg = _gdn_intra(proj, x, wa_hi, wa_lo, wb_hi, wb_lo, p["conv_qkv"].astype(F32), prm,
                                  seq, min(GDN_TILE, seq))
    o_gdn = _gdn_scan(u, w, qd, m2, eg, proj, ong, batch, seq)

    ts = min(MIXER_ROWS, seq)
    dww = jnp.broadcast_to(p["dw_w"].astype(F32)[:, None, :], (CONF_KERNEL, 8, CONF_CH))
    vec = _pad_rows(jnp.stack([p["dw_b"], p["cln_g"], p["cln_b"], p["ln1_g"], p["ln1_b"]]), 8)
    x1, x1b, x1p = _mixer(proj, o_gdn, x, p["w_oa"].astype(BF16), p["w_ob"].astype(BF16),
                          p["w_o"].astype(BF16), dww, vec, batch, seq, ts)

    tt_r = min(ROUTE_TOKENS, t)
    wr_t = p["w_router"].T.astype(F32)
    wr_hi = wr_t.astype(BF16)
    wr_lo = (wr_t - wr_hi.astype(F32)).astype(BF16)
    bias = jnp.broadcast_to(p["router_bias"].astype(F32)[:, None], (N_EXPERTS, tt_r))
    eidx_t, wts_t, rank_t, cnt = _route(x1, wr_hi, wr_lo, bias, tt_r)

    n_blocks = -(-(t * TOP_K + N_EXPERTS * (ROW_BLOCK - 1)) // ROW_BLOCK)
    n_rows = n_blocks * ROW_BLOCK
    pstart, block_e, n_valid = _plan(cnt, n_blocks)

    dest = _dest(eidx_t, rank_t, pstart, min(DEST_TOKENS, t))
    sh_w = (p["w_sh_gate"].astype(BF16), p["w_sh_up"].astype(BF16), p["w_sh_down"].astype(BF16))
    tt_s = min(SHARED_ROWS, t // 2)
    xs = _dispatch(dest, x1p, n_rows)
    shared_a = _shared(x1b, *sh_w, dest, tt_s, 0, t // 2)
    ys = _experts(block_e, n_valid, xs, stacked["w_gate_e"], stacked["w_up_e"], stacked["w_down_e"], layer)
    yg = _gather_rows(ys, dest.reshape(TOP_K * t)).reshape(TOP_K, t, ys.shape[1])
    shared_b = _shared(x1b, *sh_w, ys, tt_s, t // 2, t // 2)
    vec2 = _pad_rows(jnp.stack([p["ln2_g"], p["ln2_b"]]), 8)
    return _combine(yg, wts_t.T, x1, shared_a, shared_b, vec2, min(COMBINE_ROWS, t // 2))


_PARAM_NAMES = ("w_in", "conv_qkv", "a_log", "dt_bias", "o_norm_g", "w_oa", "dw_w", "dw_b", "cln_g",
                "cln_b", "w_ob", "w_o", "ln1_g", "ln1_b", "w_router", "router_bias", "w_gate_e",
                "w_up_e", "w_down_e", "w_sh_gate", "w_sh_up", "w_sh_down", "ln2_g", "ln2_b")


_EXPERT_WEIGHTS = ("w_in", "w_gate_e", "w_up_e", "w_down_e")


def kernel(x, w_in, conv_qkv, a_log, dt_bias, o_norm_g, w_oa, dw_w, dw_b, cln_g, cln_b, w_ob, w_o,
           ln1_g, ln1_b, w_router, router_bias, w_gate_e, w_up_e, w_down_e, w_sh_gate, w_sh_up,
           w_sh_down, ln2_g, ln2_b):
    stacked = dict(zip(_PARAM_NAMES, (w_in, conv_qkv, a_log, dt_bias, o_norm_g, w_oa, dw_w, dw_b, cln_g,
                                      cln_b, w_ob, w_o, ln1_g, ln1_b, w_router, router_bias, w_gate_e,
                                      w_up_e, w_down_e, w_sh_gate, w_sh_up, w_sh_down, ln2_g, ln2_b)))
    batch, seq, d = x.shape
    assert d == D_MODEL and seq % CHUNK == 0
    xf = x.reshape(batch * seq, d).astype(F32)
    for layer in range(w_in.shape[0]):
        p = {name: arr[layer] for name, arr in stacked.items() if name not in _EXPERT_WEIGHTS}
        xf = _layer(xf, p, stacked, layer, batch, seq)
    return xf.reshape(batch, seq, d).astype(x.dtype)
```

```python
import functools

import jax
import jax.numpy as jnp
import numpy as np
from jax import lax
from jax.experimental import pallas as pl
from jax.experimental.pallas import tpu as pltpu
from jax.experimental.pallas import tpu_sc as plsc

F32 = jnp.float32
BF16 = jnp.bfloat16
I32 = jnp.int32
U32 = jnp.uint32
HI_MASK = np.uint32(0xFFFF0000)

D_MODEL = 1024
GDN_HEADS = 8
HEAD_DIM = 128
GDN_WIDTH = GDN_HEADS * HEAD_DIM
SHORT_CONV = 4
CONV_HALO = 16
CHUNK = 64
SOLVE_BLOCK = 16
GDN_TILE = 256
SCAN_CHUNKS = 8
SCAN_SEQS = 2
CONF_CH = D_MODEL
CONF_KERNEL = 31
CONF_HALO = 32
N_EXPERTS = 64
TOP_K = 8
N_GROUPS = 8
GROUP_SIZE = N_EXPERTS // N_GROUPS
TOPK_GROUPS = 4
EXPERT_FF = 256
SHARED_FF = 256
ROUTED_SCALE = 2.5
DEPTH = 2
DEEPNORM_ALPHA = (2 * DEPTH) ** 0.25
EPS = 1e-6

LANES = 128
PROJ_COLS = 8 * D_MODEL
ROW_BLOCK = 2048
EXPERT_SUB = 256
EXPERT_IN_SLOTS = 3
SC_SCATTER_ROWS = 128
SC_GATHER_ROWS = 64
VMEM_LIMIT = 56 * 1024 * 1024
REGROUP_ROWS = 512
PROJ_ROWS = 2048
PROJ_TILE_COLS = 2048
MIXER_ROWS = 256
MIXER_CONV_ROWS = 64
ROUTE_TOKENS = 512
DEST_TOKENS = 2048
SHARED_ROWS = 512
COMBINE_ROWS = 512


def _params(*sem):
    return pltpu.CompilerParams(dimension_semantics=sem, vmem_limit_bytes=VMEM_LIMIT)


def _dot(a, b):
    return jnp.dot(a, b, preferred_element_type=F32)


def _dot_nt(a, b):
    return lax.dot_general(a, b, (((1,), (1,)), ((), ())), preferred_element_type=F32)


def _split(a):
    hi = a.astype(BF16)
    lo = (a - hi.astype(F32)).astype(BF16)
    return hi, lo


def _sigmoid(x):
    return 1.0 / (1.0 + jnp.exp(-x))


def _silu(x):
    return x * _sigmoid(x)


def _layer_norm(y, g, b):
    mu = jnp.mean(y, axis=-1, keepdims=True)
    yc = y - mu
    var = jnp.mean(yc * yc, axis=-1, keepdims=True)
    return yc * lax.rsqrt(var + EPS) * g + b


def _proj_kernel(a_ref, w_ref, o_ref):
    o_ref[...] = _dot_nt(a_ref[...].astype(BF16), w_ref[...]).astype(o_ref.dtype)


def _regroup_kernel(wt_hbm, main_ref, ahi_ref, alo_ref, bhi_ref, blo_ref, buf, ab_buf, sem, ab_sem, *, layer, tr):
    i = pl.program_id(0)
    half = PROJ_COLS // 2
    skip = 2 * GDN_HEADS

    def rows_copy(j, slot):
        start = pl.multiple_of(jnp.where(j < half // tr, 0, skip) + j * tr, skip)
        return pltpu.make_async_copy(wt_hbm.at[layer, pl.ds(start, tr), :], buf.at[slot], sem.at[slot])

    @pl.when(i == 0)
    def _():
        rows_copy(0, 0).start()

    @pl.when(i + 1 < pl.num_programs(0))
    def _():
        rows_copy(i + 1, lax.rem(i + 1, 2)).start()

    slot = lax.rem(i, 2)
    rows_copy(i, slot).wait()
    main_ref[...] = buf[slot].astype(BF16)

    @pl.when(i == 0)
    def _():
        cp_ab = pltpu.make_async_copy(wt_hbm.at[layer, pl.ds(half, LANES), :], ab_buf, ab_sem)
        cp_ab.start()
        cp_ab.wait()
        cols = ab_buf[...].T
        wide = jnp.concatenate([cols, jnp.zeros_like(cols)], axis=1)
        lane = lax.broadcasted_iota(I32, cols.shape, 1)
        for off, hi_ref, lo_ref in ((0, ahi_ref, alo_ref), (GDN_HEADS, bhi_ref, blo_ref)):
            hi, lo = _split(jnp.where(lane < GDN_HEADS, wide[:, off:off + LANES], 0.0))
            hi_ref[...] = hi
            lo_ref[...] = lo


def _regroup(w_in_t, layer, tr):
    d = w_in_t.shape[2]
    small = jax.ShapeDtypeStruct((d, LANES), BF16)
    whole = lambda i: (0, 0)
    return pl.pallas_call(
        functools.partial(_regroup_kernel, layer=layer, tr=tr),
        grid=(PROJ_COLS // tr,),
        in_specs=[pl.BlockSpec(memory_space=pl.ANY)],
        out_specs=[pl.BlockSpec((tr, d), lambda i: (i, 0))] + [pl.BlockSpec((d, LANES), whole)] * 4,
        out_shape=[jax.ShapeDtypeStruct((PROJ_COLS, d), BF16), small, small, small, small],
        scratch_shapes=[pltpu.VMEM((2, tr, d), F32), pltpu.VMEM((LANES, d), F32),
                        pltpu.SemaphoreType.DMA((2,)), pltpu.SemaphoreType.DMA],
        compiler_params=_params("arbitrary"),
        name="regroup",
    )(w_in_t)


def _proj(xb, w, tm, tn):
    m, k = xb.shape
    n = w.shape[0]
    return pl.pallas_call(
        _proj_kernel,
        grid=(m // tm, n // tn),
        in_specs=[pl.BlockSpec((tm, k), lambda i, j: (i, 0)),
                  pl.BlockSpec((tn, k), lambda i, j: (j, 0))],
        out_specs=pl.BlockSpec((tm, tn), lambda i, j: (i, j)),
        out_shape=jax.ShapeDtypeStruct((m, n), BF16),
        compiler_params=_params("parallel", "parallel"),
        name="proj",
    )(xb, w)


def _unit_lower_inverse4(als):
    c = CHUNK
    n = als[0].shape[1]
    row = lax.broadcasted_iota(I32, (c, n), 0)
    col = jnp.bitwise_and(lax.broadcasted_iota(I32, (c, n), 1), c - 1)
    shift = SOLVE_BLOCK.bit_length() - 1
    same = jnp.right_shift(row, shift) == jnp.right_shift(col, shift)
    eye = (row == col).astype(F32)
    cshift = c.bit_length() - 1
    brow = jnp.right_shift(lax.broadcasted_iota(I32, (n, n), 0), cshift)
    bcol = jnp.right_shift(lax.broadcasted_iota(I32, (n, n), 1), cshift)
    on_diag = brow == bcol

    def mm(x, y):
        yb = y.astype(BF16)
        bd = jnp.where(on_diag, jnp.concatenate([yb] * (n // c), axis=0), jnp.zeros((), BF16))
        return _dot(x.astype(BF16), bd)

    a_diag = [jnp.where(same, al, 0.0) for al in als]
    a_off = [al - ad for al, ad in zip(als, a_diag)]
    bp = [-ad for ad in a_diag]
    p = [eye + b for b in bp]
    for _ in range(3):
        bp = [mm(b, b) for b in bp]
        p = [x + mm(x, b) for x, b in zip(p, bp)]
    n1 = [mm(x, ao) for x, ao in zip(p, a_off)]
    n2 = [mm(x, x) for x in n1]
    q = [x + mm(y, x) for x, y in zip(p, n2)]
    return [x - mm(y, x) for x, y in zip(q, n1)]


def _shift_selectors(rt):
    sel = np.zeros((SHORT_CONV * rt, rt), np.float32)
    sel_halo = np.zeros((SHORT_CONV * 8, CONV_HALO), np.float32)
    for d in range(SHORT_CONV):
        for t in range(d, rt):
            sel[d * rt + t, t - d] = 1.0
        for t in range(d):
            sel_halo[d * 8 + t, CONV_HALO + t - d] = 1.0
    return jnp.asarray(sel, BF16), jnp.asarray(sel_halo, BF16)


def _gdn_intra_kernel(qkv_ref, prev_ref, x_ref, sel_ref, selh_ref, wah_ref, wal_ref, wbh_ref, wbl_ref, cw_ref,
                      prm_ref, u_ref, w_ref, qd_ref, m2_ref, eg_ref, xs_ref, *, rt, tiles_per_seq):
    c = CHUNK
    nc = rt // c
    first = (pl.program_id(0) % tiles_per_seq) == 0
    edge = jnp.where(first, 0.0, _dot(selh_ref[...], prev_ref[...]))

    def move_rows(lo, hi):
        moved = _dot(sel_ref[...], qkv_ref[:, lo:hi])
        for d in range(SHORT_CONV):
            xs_ref[d, 0:8, lo:hi] = moved[d * rt:d * rt + 8] + edge[d * 8:(d + 1) * 8, lo:hi]
            xs_ref[d, 8:rt, lo:hi] = moved[d * rt + 8:(d + 1) * rt]

    for part in range(3):
        move_rows(part * GDN_WIDTH, part * GDN_WIDTH + 2 * HEAD_DIM)

    xh, xl = _split(x_ref[...])

    def proj3(wh_ref, wl_ref):
        wh = wh_ref[...]
        return _dot(xh, wh) + _dot(xl, wh) + _dot(xh, wl_ref[...])

    a_raw = proj3(wah_ref, wal_ref)
    b_raw = proj3(wbh_ref, wbl_ref)
    sp_in = a_raw + prm_ref[1:2, :]
    softplus = jnp.maximum(sp_in, 0.0) + jnp.log(1.0 + jnp.exp(-jnp.abs(sp_in)))
    g = -jnp.exp(prm_ref[0:1, :]) * softplus
    beta = _sigmoid(b_raw)

    cshift = c.bit_length() - 1
    r2 = lax.broadcasted_iota(I32, (rt, rt), 0)
    c2 = lax.broadcasted_iota(I32, (rt, rt), 1)
    same_chunk = jnp.right_shift(r2, cshift) == jnp.right_shift(c2, cshift)
    ltri = jnp.where(r2 >= c2, jnp.where(same_chunk, 1.0, 0.0), 0.0).astype(BF16)
    g_hi = g.astype(BF16)
    g_r = g - g_hi.astype(F32)
    g_mid = g_r.astype(BF16)
    g_lo = (g_r - g_mid.astype(F32)).astype(BF16)
    gc = _dot(ltri, g_hi) + _dot(ltri, g_mid) + _dot(ltri, g_lo)
    gct = gc.T
    egc = jnp.exp(gc)
    gend = jnp.concatenate(
        [jnp.broadcast_to(gc[ci * c + c - 1:ci * c + c, :], (c, LANES)) for ci in range(nc)], axis=0)
    kfac = jnp.exp(gend - gc)
    bege = beta * egc
    for ci in range(nc):
        last = ci * c + c - 1
        eg_ref[ci * GDN_HEADS:(ci + 1) * GDN_HEADS, :] = jnp.broadcast_to(
            jnp.exp(gct[0:GDN_HEADS, last:last + 1]), (GDN_HEADS, LANES))

    lane_t = lax.broadcasted_iota(I32, (rt, LANES), 1) < c
    lane_lo = lax.broadcasted_iota(I32, (c, LANES), 1) < c
    lcol = jnp.bitwise_and(lax.broadcasted_iota(I32, (c, LANES), 1), c - 1)
    rowi = lax.broadcasted_iota(I32, (c, LANES), 0)
    causal = rowi >= lcol
    strict = rowi > lcol

    def conv(base, h):
        lo, hi = base + h * HEAD_DIM, base + (h + 1) * HEAD_DIM
        acc = cw_ref[SHORT_CONV - 1:SHORT_CONV, lo:hi] * xs_ref[0, :, lo:hi]
        for j in range(SHORT_CONV - 1):
            acc = acc + cw_ref[j:j + 1, lo:hi] * xs_ref[SHORT_CONV - 1 - j, :, lo:hi]
        return _silu(acc)

    a_pairs = [[None] * (GDN_HEADS // 2) for _ in range(nc)]
    rhs_pairs = [[None] * (GDN_HEADS // 2) for _ in range(nc)]
    for p in range(GDN_HEADS // 2):
        if p + 1 < GDN_HEADS // 2:
            for part in range(3):
                lo = part * GDN_WIDTH + (p + 1) * 2 * HEAD_DIM
                move_rows(lo, lo + 2 * HEAD_DIM)
        ks, kbs, qs, kds, rhss = [], [], [], [], []
        for h in (2 * p, 2 * p + 1):
            q = conv(0, h)
            k = conv(GDN_WIDTH, h)
            v = conv(2 * GDN_WIDTH, h)
            q = q * lax.rsqrt(jnp.sum(q * q, axis=-1, keepdims=True) + EPS) * (HEAD_DIM ** -0.5)
            k = k * lax.rsqrt(jnp.sum(k * k, axis=-1, keepdims=True) + EPS)
            beta_h = beta[:, h:h + 1]
            qd_ref[:, h * HEAD_DIM:(h + 1) * HEAD_DIM] = (q * egc[:, h:h + 1]).astype(BF16)
            ks.append(k)
            kbs.append(k * beta_h)
            qs.append(q)
            kds.append(k * kfac[:, h:h + 1])
            rhss.append(jnp.concatenate([v * beta_h, k * bege[:, h:h + 1]], axis=1))
        h0, h1 = 2 * p, 2 * p + 1
        gch = jnp.where(lane_t, gc[:, h0:h0 + 1], gc[:, h1:h1 + 1])
        for ci in range(nc):
            rows = slice(ci * c, (ci + 1) * c)
            wk = jnp.concatenate([ks[0][rows], ks[1][rows]], axis=0).astype(BF16)
            lhs = jnp.concatenate([kbs[0][rows], qs[0][rows], kbs[1][rows], qs[1][rows]],
                                  axis=0).astype(BF16)
            out = _dot_nt(lhs, wk)
            gcrow = jnp.concatenate([gct[h0:h0 + 1, rows], gct[h1:h1 + 1, rows]], axis=1)
            diff = gch[rows] - gcrow
            decay = jnp.where(causal, jnp.exp(jnp.where(causal, diff, 0.0)), 0.0)
            a_pairs[ci][p] = jnp.where(strict, jnp.where(lane_lo, out[0:c], out[2 * c:3 * c]) * decay, 0.0)
            qk = jnp.where(lane_lo, out[c:2 * c], out[3 * c:4 * c]) * decay
            kdt = jnp.concatenate([kds[0][rows], kds[1][rows]], axis=0).T
            m0 = ci * 3 * c
            m2_ref[m0:m0 + c, p * LANES:(p + 1) * LANES] = qk.astype(BF16)
            m2_ref[m0 + c:m0 + 3 * c, p * LANES:(p + 1) * LANES] = kdt.astype(BF16)
            rhs_pairs[ci][p] = (rhss[0][rows], rhss[1][rows])

    zeros = jnp.zeros((c, 2 * HEAD_DIM), BF16)
    ngrp = GDN_HEADS // 4
    tls = _unit_lower_inverse4(
        [jnp.concatenate([a_pairs[ci][2 * grp], a_pairs[ci][2 * grp + 1]], axis=1)
         for ci in range(nc) for grp in range(ngrp)])
    for ci in range(nc):
        rows = slice(ci * c, (ci + 1) * c)
        for grp in range(ngrp):
            tl = tls[ci * ngrp + grp]
            for j in range(2):
                p = 2 * grp + j
                r0, r1 = rhs_pairs[ci][p]
                bd = jnp.concatenate([jnp.concatenate([r0.astype(BF16), zeros], axis=1),
                                      jnp.concatenate([zeros, r1.astype(BF16)], axis=1)], axis=0)
                sol = _dot(tl[:, j * LANES:(j + 1) * LANES].astype(BF16), bd)
                for i in range(2):
                    h = 2 * p + i
                    lo, hi = h * HEAD_DIM, (h + 1) * HEAD_DIM
                    u_ref[rows, lo:hi] = sol[:, 2 * i * HEAD_DIM:(2 * i + 1) * HEAD_DIM]
                    w_ref[rows, lo:hi] = sol[:, (2 * i + 1) * HEAD_DIM:(2 * i + 2) * HEAD_DIM].astype(BF16)


def _gdn_intra(proj, x, wa_hi, wa_lo, wb_hi, wb_lo, conv_w, prm, seq, rt):
    t = x.shape[0]
    nc = rt // CHUNK
    kern = functools.partial(_gdn_intra_kernel, rt=rt, tiles_per_seq=seq // rt)
    full = lambda i: (0, 0)
    tile = lambda i: (i, 0)
    sel, sel_halo = _shift_selectors(rt)
    return pl.pallas_call(
        kern,
        grid=(t // rt,),
        in_specs=[
            pl.BlockSpec((rt, 3 * GDN_WIDTH), tile),
            pl.BlockSpec((CONV_HALO, 3 * GDN_WIDTH),
                         lambda i: (jnp.maximum(i * (rt // CONV_HALO) - 1, 0), 0)),
            pl.BlockSpec((rt, D_MODEL), tile),
            pl.BlockSpec(sel.shape, full),
            pl.BlockSpec(sel_halo.shape, full),
            pl.BlockSpec((D_MODEL, LANES), full),
            pl.BlockSpec((D_MODEL, LANES), full),
            pl.BlockSpec((D_MODEL, LANES), full),
            pl.BlockSpec((D_MODEL, LANES), full),
            pl.BlockSpec((SHORT_CONV, 3 * GDN_WIDTH), full),
            pl.BlockSpec((8, LANES), full),
        ],
        out_specs=[pl.BlockSpec((rt, GDN_WIDTH), tile),
                   pl.BlockSpec((rt, GDN_WIDTH), tile),
                   pl.BlockSpec((rt, GDN_WIDTH), tile),
                   pl.BlockSpec((nc * 3 * CHUNK, GDN_HEADS // 2 * LANES), tile),
                   pl.BlockSpec((nc * GDN_HEADS, LANES), tile)],
        out_shape=[jax.ShapeDtypeStruct((t, GDN_WIDTH), F32),
                   jax.ShapeDtypeStruct((t, GDN_WIDTH), BF16),
                   jax.ShapeDtypeStruct((t, GDN_WIDTH), BF16),
                   jax.ShapeDtypeStruct((t // CHUNK * 3 * CHUNK, GDN_HEADS // 2 * LANES), BF16),
                   jax.ShapeDtypeStruct((t // CHUNK * GDN_HEADS, LANES), F32)],
        scratch_shapes=[pltpu.VMEM((SHORT_CONV, rt, 3 * GDN_WIDTH), F32)],
        compiler_params=_params("parallel"),
        name="gdn_intra",
    )(proj, proj, x, sel, sel_halo, wa_hi, wa_lo, wb_hi, wb_lo, conv_w, prm)


def _gdn_scan_kernel(u_ref, w_ref, qd_ref, m2_ref, eg_ref, z_ref, ong_ref, o_ref, s_ref, *, nck):
    c = CHUNK
    nseq = u_ref.shape[0]

    @pl.when(pl.program_id(1) == 0)
    def _():
        s_ref[...] = jnp.zeros_like(s_ref)

    ong = ong_ref[...]
    zeros = jnp.zeros((c, HEAD_DIM), BF16)
    span = lambda h: slice(h * HEAD_DIM, (h + 1) * HEAD_DIM)
    chains = [(b, h) for b in range(nseq) for h in range(GDN_HEADS)]
    states = {bh: s_ref[bh[0], bh[1]] for bh in chains}
    for ci in range(nck):
        rows = slice(ci * c, (ci + 1) * c)
        rs = {(b, h): _dot(jnp.concatenate([w_ref[b, rows, span(h)], qd_ref[b, rows, span(h)]], axis=0),
                           states[b, h].astype(BF16)) for b, h in chains}
        v_new = {(b, h): (u_ref[b, rows, span(h)] - rs[b, h][:c]).astype(BF16) for b, h in chains}
        r2s = {}
        for b in range(nseq):
            for p in range(GDN_HEADS // 2):
                bd = jnp.concatenate([jnp.concatenate([v_new[b, 2 * p], zeros], axis=1),
                                      jnp.concatenate([zeros, v_new[b, 2 * p + 1]], axis=1)], axis=0)
                r2s[b, p] = _dot(m2_ref[b, ci * 3 * c:(ci + 1) * 3 * c, p * LANES:(p + 1) * LANES], bd)
        for b, h in chains:
            half = span(h % 2)
            r2 = r2s[b, h // 2]
            decay = eg_ref[b, ci * GDN_HEADS + h:ci * GDN_HEADS + h + 1, :]
            states[b, h] = states[b, h] * decay + r2[c:, half]
            o = rs[b, h][c:] + r2[:c, half]
            o = o * lax.rsqrt(jnp.mean(o * o, axis=-1, keepdims=True) + EPS) * ong
            o = o * _silu(z_ref[b, rows, span(h)].astype(F32))
            o_ref[b, rows, span(h)] = o.astype(o_ref.dtype)
    for b, h in chains:
        s_ref[b, h] = states[b, h]


def _gdn_scan(u, w, qd, m2, eg, proj, ong, batch, seq):
    t = batch * seq
    nck = min(SCAN_CHUNKS, seq // CHUNK)
    c = nck * CHUNK
    nch = seq // c
    ns = SCAN_SEQS if batch % SCAN_SEQS == 0 else 1
    groups = batch // ns
    split = lambda a: a.reshape(ns, a.shape[0] // ns, a.shape[1])
    blk = lambda b, n: (0, b * nch + n, 0)
    out = pl.pallas_call(
        functools.partial(_gdn_scan_kernel, nck=nck),
        grid=(groups, nch),
        in_specs=[
            pl.BlockSpec((ns, c, GDN_WIDTH), blk),
            pl.BlockSpec((ns, c, GDN_WIDTH), blk),
            pl.BlockSpec((ns, c, GDN_WIDTH), blk),
            pl.BlockSpec((ns, 3 * c, GDN_HEADS // 2 * LANES), blk),
            pl.BlockSpec((ns, nck * GDN_HEADS, LANES), blk),
            pl.BlockSpec((ns, c, GDN_WIDTH), lambda b, n: (0, b * nch + n, 3)),
            pl.BlockSpec((1, HEAD_DIM), lambda b, n: (0, 0)),
        ],
        out_specs=pl.BlockSpec((ns, c, GDN_WIDTH), blk),
        out_shape=jax.ShapeDtypeStruct((ns, t // ns, GDN_WIDTH), BF16),
        scratch_shapes=[pltpu.VMEM((ns, GDN_HEADS, HEAD_DIM, HEAD_DIM), F32)],
        compiler_params=_params("parallel", "arbitrary"),
        name="gdn_scan",
    )(split(u), split(w), split(qd), split(m2), split(eg), split(proj), ong)
    return out.reshape(t, GDN_WIDTH)


def _pack_bf16_pairs(y):
    n = y.shape[1] // 2
    yb = y.astype(BF16).astype(F32)
    lo = lax.bitcast_convert_type(yb[:, :n], U32)
    hi = lax.bitcast_convert_type(yb[:, n:], U32)
    return jnp.bitwise_or(jnp.right_shift(lo, 16), jnp.bitwise_and(hi, HI_MASK))


def _unpack_bf16_pairs(w):
    lo = lax.bitcast_convert_type(jnp.left_shift(w, 16), F32)
    hi = lax.bitcast_convert_type(jnp.bitwise_and(w, HI_MASK), F32)
    return lo, hi


def _mixer_kernel(glu_ref, ga_ref, gb_ref, o_ref, x_ref, woa_ref, wob_ref, wo_ref, dww_ref, vec_ref,
                  x1_ref, x1b_ref, x1p_ref, ubuf_ref, sh_ref, conv_ref, *, ts, rc):
    halo = CONF_HALO

    @pl.when(pl.program_id(1) == 0)
    def _():
        ubuf_ref[0:halo, :] = jnp.zeros((halo, CONF_CH), F32)

    @pl.when(pl.program_id(1) != 0)
    def _():
        ubuf_ref[0:halo, :] = ubuf_ref[ts:ts + halo, :]

    glu_a = glu_ref[:, :CONF_CH].astype(F32)
    glu_b = glu_ref[:, CONF_CH:].astype(F32)
    ubuf_ref[halo:halo + ts, :] = glu_a * _sigmoid(glu_b)

    dw_b = vec_ref[0:1, :]
    cln_g = vec_ref[1:2, :]
    cln_b = vec_ref[2:3, :]
    ln1_g = vec_ref[3:4, :]
    ln1_b = vec_ref[4:5, :]

    span = ts + halo - 8
    for s in range(1, 8):
        sh_ref[s - 1] = ubuf_ref[s:s + span, :]

    def tap_rows(j, r0):
        o = halo - (CONF_KERNEL - 1) + j
        q, s = o // 8, o % 8
        if s == 0:
            return ubuf_ref[r0 + o:r0 + o + rc, :]
        return sh_ref[s - 1, r0 + 8 * q:r0 + 8 * q + rc, :]

    hs = ts // 2
    for h0 in range(0, ts, hs):
        rows = slice(h0, h0 + hs)
        gated_a = _sigmoid(ga_ref[rows, :].astype(F32)) * _dot(o_ref[rows, :], woa_ref[...])
        for r0 in range(h0, h0 + hs, rc):
            tap_w = lambda j: jnp.concatenate([dww_ref[j]] * (rc // 8), axis=0)
            acc = tap_w(0) * tap_rows(0, r0)
            for j in range(1, CONF_KERNEL):
                acc = acc + tap_w(j) * tap_rows(j, r0)
            conv_ref[r0:r0 + rc, :] = acc
        uc = _silu(_layer_norm(conv_ref[rows, :] + dw_b, cln_g, cln_b))
        branch_b = _dot(uc.astype(BF16), wob_ref[...])
        hmix = gated_a + _sigmoid(gb_ref[rows, :].astype(F32)) * branch_b
        mix = _dot(hmix.astype(BF16), wo_ref[...])
        x1 = _layer_norm(DEEPNORM_ALPHA * x_ref[rows, :] + mix, ln1_g, ln1_b)
        x1_ref[rows, :] = x1
        x1b_ref[rows, :] = x1.astype(BF16)
        x1p_ref[rows, :] = _pack_bf16_pairs(x1)


def _mixer(proj, o_gdn, x, woa, wob, wo, dww, vec, batch, seq, ts):
    t = batch * seq
    nt = seq // ts
    rows = lambda b, n: b * nt + n
    full = lambda b, n: (0, 0)
    kern = functools.partial(_mixer_kernel, ts=ts, rc=MIXER_CONV_ROWS)
    return pl.pallas_call(
        kern,
        grid=(batch, nt),
        in_specs=[
            pl.BlockSpec((ts, 2 * CONF_CH), lambda b, n: (rows(b, n), 2)),
            pl.BlockSpec((ts, D_MODEL), lambda b, n: (rows(b, n), 6)),
            pl.BlockSpec((ts, D_MODEL), lambda b, n: (rows(b, n), 7)),
            pl.BlockSpec((ts, GDN_WIDTH), lambda b, n: (rows(b, n), 0)),
            pl.BlockSpec((ts, D_MODEL), lambda b, n: (rows(b, n), 0)),
            pl.BlockSpec((GDN_WIDTH, D_MODEL), full),
            pl.BlockSpec((CONF_CH, D_MODEL), full),
            pl.BlockSpec((D_MODEL, D_MODEL), full),
            pl.BlockSpec((CONF_KERNEL, 8, CONF_CH), lambda b, n: (0, 0, 0)),
            pl.BlockSpec((8, D_MODEL), full),
        ],
        out_specs=[pl.BlockSpec((ts, D_MODEL), lambda b, n: (rows(b, n), 0)),
                   pl.BlockSpec((ts, D_MODEL), lambda b, n: (rows(b, n), 0)),
                   pl.BlockSpec((ts, D_MODEL // 2), lambda b, n: (rows(b, n), 0))],
        out_shape=[jax.ShapeDtypeStruct((t, D_MODEL), F32),
                   jax.ShapeDtypeStruct((t, D_MODEL), BF16),
                   jax.ShapeDtypeStruct((t, D_MODEL // 2), U32)],
        scratch_shapes=[pltpu.VMEM((CONF_HALO + ts, CONF_CH), F32),
                        pltpu.VMEM((7, ts + CONF_HALO - 8, CONF_CH), F32),
                        pltpu.VMEM((ts, CONF_CH), F32)],
        compiler_params=_params("parallel", "arbitrary"),
        name="mixer",
    )(proj, proj, proj, o_gdn, x, woa, wob, wo, dww, vec)


def _route_kernel(x_ref, wrh_ref, wrl_ref, bias_ref, eidx_ref, wts_ref, rank_ref, cnt_ref, carry_ref, *, tt):
    @pl.when(pl.program_id(0) == 0)
    def _():
        carry_ref[...] = jnp.zeros_like(carry_ref)

    xh, xl = _split(x_ref[...])
    wrh = wrh_ref[...]
    logits = _dot_nt(wrh, xh) + _dot_nt(wrh, xl) + _dot_nt(wrl_ref[...], xh)
    s = _sigmoid(logits)
    biased = s + bias_ref[...]

    sub = lax.broadcasted_iota(I32, (GROUP_SIZE, tt), 0)
    groups = [biased[g * GROUP_SIZE:(g + 1) * GROUP_SIZE, :] for g in range(N_GROUPS)]
    gs = []
    for bg in groups:
        m1 = jnp.max(bg, axis=0, keepdims=True)
        first = jnp.min(jnp.where(bg == m1, sub, GROUP_SIZE), axis=0, keepdims=True)
        m2 = jnp.max(jnp.where(sub == first, -jnp.inf, bg), axis=0, keepdims=True)
        gs.append(m1 + m2)

    masked_parts = []
    for g in range(N_GROUPS):
        beaten = jnp.zeros((1, tt), I32)
        for o in range(N_GROUPS):
            if o == g:
                continue
            wins = (gs[o] >= gs[g]) if o < g else (gs[o] > gs[g])
            beaten = beaten + wins.astype(I32)
        keep = jnp.broadcast_to(beaten < TOPK_GROUPS, (GROUP_SIZE, tt))
        masked_parts.append(jnp.where(keep, groups[g], -jnp.inf))
    masked = jnp.concatenate(masked_parts, axis=0)

    eiota = lax.broadcasted_iota(I32, (N_EXPERTS, tt), 0)
    sel_all = jnp.zeros((N_EXPERTS, tt), F32)
    picks = []
    for _ in range(TOP_K):
        m = jnp.max(masked, axis=0, keepdims=True)
        idx = jnp.min(jnp.where(masked == m, eiota, N_EXPERTS), axis=0, keepdims=True)
        onehot = eiota == idx
        picks.append((idx, onehot))
        sel_all = jnp.where(onehot, 1.0, sel_all)
        masked = jnp.where(onehot, -jnp.inf, masked)

    tr = lax.broadcasted_iota(I32, (tt, tt), 0)
    tc = lax.broadcasted_iota(I32, (tt, tt), 1)
    before = (tr < tc).astype(BF16)
    sel_b = sel_all.astype(BF16)
    carry = carry_ref[...]
    rank_all = _dot(sel_b, before) + carry[:, 0:1]
    carry_new = carry + _dot(sel_b, jnp.ones((tt, LANES), BF16))
    carry_ref[...] = carry_new
    cnt_ref[...] = carry_new

    s_sel = [jnp.sum(jnp.where(oh, s, 0.0), axis=0, keepdims=True) for _, oh in picks]
    total = s_sel[0]
    for v in s_sel[1:]:
        total = total + v
    for k, (idx, oh) in enumerate(picks):
        eidx_ref[k:k + 1, :] = idx
        wts_ref[k:k + 1, :] = s_sel[k] / total * ROUTED_SCALE
        rank_ref[k:k + 1, :] = jnp.sum(jnp.where(oh, rank_all, 0.0), axis=0, keepdims=True).astype(I32)


def _route(x1, wr_hi, wr_lo, bias, tt):
    t = x1.shape[0]
    kern = functools.partial(_route_kernel, tt=tt)
    return pl.pallas_call(
        kern,
        grid=(t // tt,),
        in_specs=[pl.BlockSpec((tt, D_MODEL), lambda i: (i, 0)),
                  pl.BlockSpec((N_EXPERTS, D_MODEL), lambda i: (0, 0)),
                  pl.BlockSpec((N_EXPERTS, D_MODEL), lambda i: (0, 0)),
                  pl.BlockSpec((N_EXPERTS, tt), lambda i: (0, 0))],
        out_specs=[pl.BlockSpec((TOP_K, tt), lambda i: (0, i)),
                   pl.BlockSpec((TOP_K, tt), lambda i: (0, i)),
                   pl.BlockSpec((TOP_K, tt), lambda i: (0, i)),
                   pl.BlockSpec((N_EXPERTS, LANES), lambda i: (0, 0))],
        out_shape=[jax.ShapeDtypeStruct((TOP_K, t), I32),
                   jax.ShapeDtypeStruct((TOP_K, t), F32),
                   jax.ShapeDtypeStruct((TOP_K, t), I32),
                   jax.ShapeDtypeStruct((N_EXPERTS, LANES), F32)],
        scratch_shapes=[pltpu.VMEM((N_EXPERTS, LANES), F32)],
        compiler_params=_params("arbitrary"),
        name="route",
    )(x1, wr_hi, wr_lo, bias)


def _plan_kernel(cnt_ref, pstart_ref, plan_ref):
    e, nb = N_EXPERTS, plan_ref.shape[1]
    counts = cnt_ref[...]
    nblk = jnp.floor((counts + (ROW_BLOCK - 1)) * (1.0 / ROW_BLOCK))
    hi = jnp.floor(nblk * (1.0 / 256.0))
    lo = nblk - 256.0 * hi
    r = lax.broadcasted_iota(I32, (e, e), 0)
    c = lax.broadcasted_iota(I32, (e, e), 1)
    ltri = (r >= c).astype(BF16)
    bend = 256.0 * _dot(ltri, hi.astype(BF16)) + _dot(ltri, lo.astype(BF16))
    pend = bend * ROW_BLOCK
    pstart = pend - nblk * ROW_BLOCK
    pstart_ref[...] = pstart.astype(I32)

    bs = (lax.broadcasted_iota(I32, (e, nb), 1) * ROW_BLOCK).astype(F32)
    pend_b = jnp.broadcast_to(pend[:, 0:1], (e, nb))
    pstart_b = jnp.broadcast_to(pstart[:, 0:1], (e, nb))
    used_b = jnp.broadcast_to((pstart + counts)[:, 0:1], (e, nb))
    owner = jnp.sum(jnp.where(pend_b <= bs, 1.0, 0.0), axis=0, keepdims=True)
    inside = jnp.where(pstart_b <= bs, jnp.where(bs < pend_b, 1.0, 0.0), 0.0)
    real = jnp.sum(inside * jnp.clip(used_b - bs, 0.0, float(ROW_BLOCK)), axis=0, keepdims=True)
    plan_ref[0:1, :] = jnp.minimum(owner, float(e - 1)).astype(I32)
    plan_ref[1:2, :] = real.astype(I32)
    plan_ref[2:8, :] = jnp.zeros((6, nb), I32)


def _plan(cnt, n_blocks):
    nb = -(-n_blocks // LANES) * LANES
    pstart, plan = pl.pallas_call(
        _plan_kernel,
        out_shape=[jax.ShapeDtypeStruct((N_EXPERTS, LANES), I32),
                   jax.ShapeDtypeStruct((8, nb), I32)],
        name="plan",
    )(cnt)
    return pstart[:, 0], plan[0, :n_blocks], plan[1, :n_blocks]


def _dest_kernel(eidx_ref, rank_ref, pstart_ref, dest_ref):
    eidx = eidx_ref[...]
    acc = rank_ref[...]
    for e in range(N_EXPERTS):
        acc = acc + jnp.where(eidx == e, pstart_ref[e], 0)
    dest_ref[...] = acc


def _dest(eidx_t, rank_t, pstart, tt):
    t = eidx_t.shape[1]
    return pl.pallas_call(
        _dest_kernel,
        grid=(t // tt,),
        in_specs=[pl.BlockSpec((TOP_K, tt), lambda i: (0, i)),
                  pl.BlockSpec((TOP_K, tt), lambda i: (0, i)),
                  pl.BlockSpec(memory_space=pltpu.SMEM)],
        out_specs=pl.BlockSpec((TOP_K, tt), lambda i: (0, i)),
        out_shape=jax.ShapeDtypeStruct((TOP_K, t), I32),
        compiler_params=_params("parallel"),
        name="dest",
    )(eidx_t, rank_t, pstart)


def _dispatch(dest_kt, x1p, n_rows):
    t, width = x1p.shape
    info = plsc.get_sparse_core_info()
    nc, nw = info.num_cores, info.num_cores * info.num_subcores
    chunk = SC_SCATTER_ROWS
    per_w = t // nw
    n_chunks = per_w // chunk
    assert per_w % chunk == 0
    idx = dest_kt.reshape(TOP_K, nw, n_chunks, chunk).transpose(1, 2, 0, 3).reshape(nw, n_chunks * TOP_K, chunk)
    mesh = plsc.VectorSubcoreMesh(core_axis_name="c", subcore_axis_name="s")

    @functools.partial(
        pl.kernel, mesh=mesh, name="dispatch",
        out_type=jax.ShapeDtypeStruct((n_rows, width), x1p.dtype),
        scratch_types=[pltpu.VMEM((n_chunks * TOP_K, chunk), I32),
                       pltpu.VMEM((chunk, width), x1p.dtype),
                       pltpu.SemaphoreType.DMA])
    def scatter(x_hbm, idx_hbm, xs_hbm, idx_v, rows_v, sem):
        wid = lax.axis_index("s") * nc + lax.axis_index("c")
        base = wid * per_w
        pltpu.sync_copy(idx_hbm.at[wid], idx_v)
        for j in range(n_chunks):
            pltpu.sync_copy(x_hbm.at[pl.ds(base + j * chunk, chunk)], rows_v)
            copies = [pltpu.make_async_copy(rows_v, xs_hbm.at[idx_v.at[j * TOP_K + k]], sem)
                      for k in range(TOP_K)]
            for cp in copies:
                cp.start()
            for cp in copies:
                cp.wait()

    return scatter(x1p, idx)


def _gather_rows(table, idx):
    n = idx.shape[0]
    width = table.shape[1]
    info = plsc.get_sparse_core_info()
    nc, nw = info.num_cores, info.num_cores * info.num_subcores
    chunk = SC_GATHER_ROWS
    per_w = n // nw
    n_chunks = per_w // chunk
    assert per_w % (2 * chunk) == 0
    mesh = plsc.VectorSubcoreMesh(core_axis_name="c", subcore_axis_name="s")

    @functools.partial(
        pl.kernel, mesh=mesh, name="gather_rows",
        out_type=jax.ShapeDtypeStruct((n, width), table.dtype),
        scratch_types=[pltpu.VMEM((n_chunks, chunk), I32),
                       pltpu.VMEM((2, chunk, width), table.dtype),
                       pltpu.SemaphoreType.DMA((2,)),
                       pltpu.SemaphoreType.DMA((2,))])
    def gather(table_hbm, idx_hbm, out_hbm, idx_v, rows_v, gsem, osem):
        wid = lax.axis_index("s") * nc + lax.axis_index("c")
        base = wid * per_w
        pltpu.sync_copy(idx_hbm.at[wid], idx_v)

        def fetch(j, b):
            return pltpu.make_async_copy(table_hbm.at[idx_v.at[j]], rows_v.at[b], gsem.at[b])

        def put(j, b):
            return pltpu.make_async_copy(rows_v.at[b], out_hbm.at[pl.ds(base + j * chunk, chunk)], osem.at[b])

        fetch(0, 0).start()

        @pl.loop(0, n_chunks, step=2)
        def _(j0):
            for b in range(2):
                j = j0 + b
                fetch(j, b).wait()

                @pl.when(j + 1 < n_chunks)
                def _():
                    @pl.when(j >= 1)
                    def _():
                        put(j - 1, 1 - b).wait()

                    fetch(j + 1, 1 - b).start()

                put(j, b).start()

        put(n_chunks - 2, 0).wait()
        put(n_chunks - 1, 1).wait()

    return gather(table, idx.reshape(nw, n_chunks, chunk))


def _xs_copy(xs_hbm, xbuf, isem, j, slot):
    return pltpu.make_async_copy(xs_hbm.at[pl.ds(j * ROW_BLOCK, ROW_BLOCK)], xbuf.at[slot], isem.at[slot])


def _ys_copy(ybuf, ys_hbm, osem, j, slot):
    return pltpu.make_async_copy(ybuf.at[slot], ys_hbm.at[pl.ds(j * ROW_BLOCK, ROW_BLOCK)], osem.at[slot])


def _experts_kernel(be_ref, nv_ref, xs_hbm, wg_ref, wu_ref, wd_ref, ys_hbm,
                    xbuf, ybuf, wgu_s, wd_s, cur_ref, isem, osem):
    i = pl.program_id(0)
    n_valid = nv_ref[i]
    half = EXPERT_SUB
    slot = lax.rem(i, EXPERT_IN_SLOTS)
    oslot = lax.rem(i, 2)

    @pl.when(i == 0)
    def _():
        cur_ref[0] = -1
        for j in range(2):
            @pl.when(nv_ref[j] > 0)
            def _():
                _xs_copy(xs_hbm, xbuf, isem, j, j).start()

    @pl.when(nv_ref[i + 2] > 0)
    def _():
        _xs_copy(xs_hbm, xbuf, isem, i + 2, lax.rem(i + 2, EXPERT_IN_SLOTS)).start()

    @pl.when((i >= 2) & (nv_ref[jnp.maximum(i - 2, 0)] > 0))
    def _():
        _ys_copy(ybuf, ys_hbm, osem, i - 2, oslot).wait()

    @pl.when((n_valid > 0) & (cur_ref[0] != be_ref[i]))
    def _():
        wgu_s[:, :EXPERT_FF] = wg_ref[...].astype(BF16)
        wgu_s[:, EXPERT_FF:] = wu_ref[...].astype(BF16)
        wd_s[...] = wd_ref[...].astype(BF16)
        cur_ref[0] = be_ref[i]

    def rows_bf16(r0):
        valid = lax.broadcasted_iota(I32, (half, xbuf.shape[2]), 0) + r0 < n_valid
        lo, hi = _unpack_bf16_pairs(jnp.where(valid, xbuf[slot, r0:r0 + half, :], jnp.zeros((), U32)))
        return jnp.concatenate([lo.astype(BF16), hi.astype(BF16)], axis=1)

    def hidden(gu):
        return (_silu(gu[:, :EXPERT_FF]) * gu[:, EXPERT_FF:]).astype(BF16)

    @pl.when(n_valid > 0)
    def _():
        _xs_copy(xs_hbm, xbuf, isem, i, slot).wait()

    zeros = jnp.zeros((half, ybuf.shape[2]), ybuf.dtype)
    for r0 in range(0, ROW_BLOCK, 2 * half):
        r1 = r0 + half

        @pl.when(n_valid > r1)
        def _():
            xa, xb = rows_bf16(r0), rows_bf16(r1)
            gua = _dot(xa, wgu_s[...])
            gub = _dot(xb, wgu_s[...])
            ya = _dot(hidden(gua), wd_s[...])
            yb = _dot(hidden(gub), wd_s[...])
            ybuf[oslot, r0:r1, :] = _pack_bf16_pairs(ya)
            ybuf[oslot, r1:r1 + half, :] = _pack_bf16_pairs(yb)

        @pl.when((n_valid > r0) & (n_valid <= r1))
        def _():
            ya = _dot(hidden(_dot(rows_bf16(r0), wgu_s[...])), wd_s[...])
            ybuf[oslot, r0:r1, :] = _pack_bf16_pairs(ya)
            ybuf[oslot, r1:r1 + half, :] = zeros

        @pl.when((n_valid > 0) & (n_valid <= r0))
        def _():
            ybuf[oslot, r0:r1, :] = zeros
            ybuf[oslot, r1:r1 + half, :] = zeros

    @pl.when(n_valid > 0)
    def _():
        _ys_copy(ybuf, ys_hbm, osem, i, oslot).start()


def _experts(block_e, n_valid, xs, wg, wu, wd, layer):
    n_rows, width = xs.shape
    steps = n_rows // ROW_BLOCK + 2
    be = jnp.concatenate([block_e, jnp.full((2,), N_EXPERTS - 1, I32)])
    nv = jnp.concatenate([n_valid, jnp.zeros((4,), I32)])
    grid_spec = pltpu.PrefetchScalarGridSpec(
        num_scalar_prefetch=2,
        grid=(steps,),
        in_specs=[pl.BlockSpec(memory_space=pl.ANY),
                  pl.BlockSpec((None, None, D_MODEL, EXPERT_FF), lambda i, be, nv: (layer, be[i], 0, 0)),
                  pl.BlockSpec((None, None, D_MODEL, EXPERT_FF), lambda i, be, nv: (layer, be[i], 0, 0)),
                  pl.BlockSpec((None, None, EXPERT_FF, D_MODEL), lambda i, be, nv: (layer, be[i], 0, 0))],
        out_specs=pl.BlockSpec(memory_space=pl.ANY),
        scratch_shapes=[pltpu.VMEM((EXPERT_IN_SLOTS, ROW_BLOCK, width), xs.dtype),
                        pltpu.VMEM((2, ROW_BLOCK, width), xs.dtype),
                        pltpu.VMEM((D_MODEL, 2 * EXPERT_FF), BF16),
                        pltpu.VMEM((EXPERT_FF, D_MODEL), BF16),
                        pltpu.SMEM((1,), I32),
                        pltpu.SemaphoreType.DMA((EXPERT_IN_SLOTS,)),
                        pltpu.SemaphoreType.DMA((2,))],
    )
    return pl.pallas_call(
        _experts_kernel,
        grid_spec=grid_spec,
        out_shape=jax.ShapeDtypeStruct((n_rows, width), xs.dtype),
        compiler_params=_params("arbitrary"),
        name="experts",
    )(be, nv, xs, wg, wu, wd)


def _shared_kernel(x1b_ref, wsg_ref, wsu_ref, wsd_ref, anchor_ref, sh_ref):
    del anchor_ref
    xb = x1b_ref[...]
    hid = (_silu(_dot(xb, wsg_ref[...])) * _dot(xb, wsu_ref[...])).astype(BF16)
    sh_ref[...] = _dot(hid, wsd_ref[...]).astype(sh_ref.dtype)


def _shared(x1b, wsg, wsu, wsd, anchor, tt, row0, nrows):
    full = lambda i: (0, 0)
    first = row0 // tt
    return pl.pallas_call(
        _shared_kernel,
        grid=(nrows // tt,),
        in_specs=[pl.BlockSpec((tt, D_MODEL), lambda i: (first + i, 0)),
                  pl.BlockSpec((D_MODEL, SHARED_FF), full),
                  pl.BlockSpec((D_MODEL, SHARED_FF), full),
                  pl.BlockSpec((SHARED_FF, D_MODEL), full),
                  pl.BlockSpec(memory_space=pl.ANY)],
        out_specs=pl.BlockSpec((tt, D_MODEL), lambda i: (i, 0)),
        out_shape=jax.ShapeDtypeStruct((nrows, D_MODEL), BF16),
        compiler_params=_params("parallel"),
        name="shared",
    )(x1b, wsg, wsu, wsd, anchor)


def _combine_kernel(yg_ref, w_ref, x1_ref, sha_ref, shb_ref, vec_ref, x2_ref, *, half_steps):
    in_first = pl.program_id(0) < half_steps
    shared = jnp.where(in_first, sha_ref[...], shb_ref[...]).astype(F32)
    w = w_ref[...]
    half = D_MODEL // 2
    acc_lo, acc_hi = shared[:, :half], shared[:, half:]
    for k in range(TOP_K):
        lo, hi = _unpack_bf16_pairs(yg_ref[k])
        acc_lo = acc_lo + w[:, k:k + 1] * lo
        acc_hi = acc_hi + w[:, k:k + 1] * hi
    acc = jnp.concatenate([acc_lo, acc_hi], axis=1)
    x2 = _layer_norm(DEEPNORM_ALPHA * x1_ref[...] + acc, vec_ref[0:1, :], vec_ref[1:2, :])
    x2_ref[...] = x2


def _combine(yg, w_tok, x1, shared_a, shared_b, vec, tt):
    t = x1.shape[0]
    half_steps = shared_a.shape[0] // tt
    full = lambda i: (0, 0)
    return pl.pallas_call(
        functools.partial(_combine_kernel, half_steps=half_steps),
        grid=(t // tt,),
        in_specs=[pl.BlockSpec((TOP_K, tt, yg.shape[2]), lambda i: (0, i, 0)),
                  pl.BlockSpec((tt, TOP_K), lambda i: (i, 0)),
                  pl.BlockSpec((tt, D_MODEL), lambda i: (i, 0)),
                  pl.BlockSpec((tt, D_MODEL), lambda i: (jnp.minimum(i, half_steps - 1), 0)),
                  pl.BlockSpec((tt, D_MODEL), lambda i: (jnp.maximum(i - half_steps, 0), 0)),
                  pl.BlockSpec((8, D_MODEL), full)],
        out_specs=pl.BlockSpec((tt, D_MODEL), lambda i: (i, 0)),
        out_shape=jax.ShapeDtypeStruct((t, D_MODEL), F32),
        compiler_params=_params("parallel"),
        name="combine",
    )(yg, w_tok, x1, shared_a, shared_b, vec)


def _pad_rows(a, rows):
    return jnp.zeros((rows, a.shape[-1]), F32).at[:a.shape[0]].set(a.astype(F32))


def _layer(x, p, stacked, layer, batch, seq):
    t = batch * seq
    w_main, wa_hi, wa_lo, wb_hi, wb_lo = _regroup(jnp.swapaxes(stacked["w_in"], 1, 2), layer, REGROUP_ROWS)
    proj = _proj(x, w_main, min(PROJ_ROWS, t), PROJ_TILE_COLS)

    prm = jnp.zeros((8, LANES), F32)
    prm = prm.at[0, :GDN_HEADS].set(p["a_log"]).at[1, :GDN_HEADS].set(p["dt_bias"])
    ong = p["o_norm_g"].reshape(1, HEAD_DIM).astype(F32)
    u, w, qd, m2, eg = _gdn_intra(proj, x, wa_hi, wa_lo, wb_hi, wb_lo, p["conv_qkv"].astype(F32), prm,
                                  seq, min(GDN_TILE, seq))
    o_gdn = _gdn_scan(u, w, qd, m2, eg, proj, ong, batch, seq)

    ts = min(MIXER_ROWS, seq)
    dww = jnp.broadcast_to(p["dw_w"].astype(F32)[:, None, :], (CONF_KERNEL, 8, CONF_CH))
    vec = _pad_rows(jnp.stack([p["dw_b"], p["cln_g"], p["cln_b"], p["ln1_g"], p["ln1_b"]]), 8)
    x1, x1b, x1p = _mixer(proj, o_gdn, x, p["w_oa"].astype(BF16), p["w_ob"].astype(BF16),
                          p["w_o"].astype(BF16), dww, vec, batch, seq, ts)

    tt_r = min(ROUTE_TOKENS, t)
    wr_t = p["w_router"].T.astype(F32)
    wr_hi = wr_t.astype(BF16)
    wr_lo = (wr_t - wr_hi.astype(F32)).astype(BF16)
    bias = jnp.broadcast_to(p["router_bias"].astype(F32)[:, None], (N_EXPERTS, tt_r))
    eidx_t, wts_t, rank_t, cnt = _route(x1, wr_hi, wr_lo, bias, tt_r)

    n_blocks = -(-(t * TOP_K + N_EXPERTS * (ROW_BLOCK - 1)) // ROW_BLOCK)
    n_rows = n_blocks * ROW_BLOCK
    pstart, block_e, n_valid = _plan(cnt, n_blocks)

    dest = _dest(eidx_t, rank_t, pstart, min(DEST_TOKENS, t))
    sh_w = (p["w_sh_gate"].astype(BF16), p["w_sh_up"].astype(BF16), p["w_sh_down"].astype(BF16))
    tt_s = min(SHARED_ROWS, t // 2)
    xs = _dispatch(dest, x1p, n_rows)
    shared_a = _shared(x1b, *sh_w, dest, tt_s, 0, t // 2)
    ys = _experts(block_e, n_valid, xs, stacked["w_gate_e"], stacked["w_up_e"], stacked["w_down_e"], layer)
    yg = _gather_rows(ys, dest.reshape(TOP_K * t)).reshape(TOP_K, t, ys.shape[1])
    shared_b = _shared(x1b, *sh_w, ys, tt_s, t // 2, t // 2)
    vec2 = _pad_rows(jnp.stack([p["ln2_g"], p["ln2_b"]]), 8)
    return _combine(yg, wts_t.T, x1, shared_a, shared_b, vec2, min(COMBINE_ROWS, t // 2))


_PARAM_NAMES = ("w_in", "conv_qkv", "a_log", "dt_bias", "o_norm_g", "w_oa", "dw_w", "dw_b", "cln_g",
                "cln_b", "w_ob", "w_o", "ln1_g", "ln1_b", "w_router", "router_bias", "w_gate_e",
                "w_up_e", "w_down_e", "w_sh_gate", "w_sh_up", "w_sh_down", "ln2_g", "ln2_b")


_EXPERT_WEIGHTS = ("w_in", "w_gate_e", "w_up_e", "w_down_e")


def kernel(x, w_in, conv_qkv, a_log, dt_bias, o_norm_g, w_oa, dw_w, dw_b, cln_g, cln_b, w_ob, w_o,
           ln1_g, ln1_b, w_router, router_bias, w_gate_e, w_up_e, w_down_e, w_sh_gate, w_sh_up,
           w_sh_down, ln2_g, ln2_b):
    stacked = dict(zip(_PARAM_NAMES, (w_in, conv_qkv, a_log, dt_bias, o_norm_g, w_oa, dw_w, dw_b, cln_g,
                                      cln_b, w_ob, w_o, ln1_g, ln1_b, w_router, router_bias, w_gate_e,
                                      w_up_e, w_down_e, w_sh_gate, w_sh_up, w_sh_down, ln2_g, ln2_b)))
    batch, seq, d = x.shape
    assert d == D_MODEL and seq % CHUNK == 0
    xf = x.reshape(batch * seq, d).astype(F32)
    for layer in range(w_in.shape[0]):
        p = {name: arr[layer] for name, arr in stacked.items() if name not in _EXPERT_WEIGHTS}
        xf = _layer(xf, p, stacked, layer, batch, seq)
    return xf.reshape(batch, seq, d).astype(x.dtype)
```

```python
import functools

import jax
import jax.numpy as jnp
import numpy as np
from jax import lax
from jax.experimental import pallas as pl
from jax.experimental.pallas import tpu as pltpu
from jax.experimental.pallas import tpu_sc as plsc

F32 = jnp.float32
BF16 = jnp.bfloat16
I32 = jnp.int32
U32 = jnp.uint32
HI_MASK = np.uint32(0xFFFF0000)

D_MODEL = 1024
GDN_HEADS = 8
HEAD_DIM = 128
GDN_WIDTH = GDN_HEADS * HEAD_DIM
SHORT_CONV = 4
CONV_HALO = 16
CHUNK = 64
SOLVE_BLOCK = 16
GDN_TILE = 256
SCAN_CHUNKS = 8
SCAN_SEQS = 2
CONF_CH = D_MODEL
CONF_KERNEL = 31
CONF_HALO = 32
N_EXPERTS = 64
TOP_K = 8
N_GROUPS = 8
GROUP_SIZE = N_EXPERTS // N_GROUPS
TOPK_GROUPS = 4
EXPERT_FF = 256
SHARED_FF = 256
ROUTED_SCALE = 2.5
DEPTH = 2
DEEPNORM_ALPHA = (2 * DEPTH) ** 0.25
EPS = 1e-6

LANES = 128
PROJ_COLS = 8 * D_MODEL
ROW_BLOCK = 2048
EXPERT_SUB = 256
EXPERT_IN_SLOTS = 3
SC_SCATTER_ROWS = 128
SC_GATHER_ROWS = 64
VMEM_LIMIT = 56 * 1024 * 1024
REGROUP_ROWS = 512
PROJ_ROWS = 2048
PROJ_TILE_COLS = 2048
MIXER_ROWS = 256
MIXER_CONV_ROWS = 64
ROUTE_TOKENS = 1024
DEST_TOKENS = 2048
SHARED_ROWS = 512
COMBINE_ROWS = 512


def _params(*sem):
    return pltpu.CompilerParams(dimension_semantics=sem, vmem_limit_bytes=VMEM_LIMIT)


def _dot(a, b):
    return jnp.dot(a, b, preferred_element_type=F32)


def _dot_nt(a, b):
    return lax.dot_general(a, b, (((1,), (1,)), ((), ())), preferred_element_type=F32)


def _split(a):
    hi = a.astype(BF16)
    lo = (a - hi.astype(F32)).astype(BF16)
    return hi, lo


def _sigmoid(x):
    return 1.0 / (1.0 + jnp.exp(-x))


def _silu(x):
    return x * _sigmoid(x)


def _layer_norm(y, g, b):
    mu = jnp.mean(y, axis=-1, keepdims=True)
    yc = y - mu
    var = jnp.mean(yc * yc, axis=-1, keepdims=True)
    return yc * lax.rsqrt(var + EPS) * g + b


def _proj_kernel(a_ref, w_ref, o_ref):
    o_ref[...] = _dot_nt(a_ref[...].astype(BF16), w_ref[...]).astype(o_ref.dtype)


def _regroup_kernel(wt_hbm, main_ref, ahi_ref, alo_ref, bhi_ref, blo_ref, buf, ab_buf, sem, ab_sem, *, layer, tr):
    i = pl.program_id(0)
    half = PROJ_COLS // 2
    skip = 2 * GDN_HEADS

    def rows_copy(j, slot):
        start = pl.multiple_of(jnp.where(j < half // tr, 0, skip) + j * tr, skip)
        return pltpu.make_async_copy(wt_hbm.at[layer, pl.ds(start, tr), :], buf.at[slot], sem.at[slot])

    @pl.when(i == 0)
    def _():
        rows_copy(0, 0).start()

    @pl.when(i + 1 < pl.num_programs(0))
    def _():
        rows_copy(i + 1, lax.rem(i + 1, 2)).start()

    slot = lax.rem(i, 2)
    rows_copy(i, slot).wait()
    main_ref[...] = buf[slot].astype(BF16)

    @pl.when(i == 0)
    def _():
        cp_ab = pltpu.make_async_copy(wt_hbm.at[layer, pl.ds(half, LANES), :], ab_buf, ab_sem)
        cp_ab.start()
        cp_ab.wait()
        cols = ab_buf[...].T
        wide = jnp.concatenate([cols, jnp.zeros_like(cols)], axis=1)
        lane = lax.broadcasted_iota(I32, cols.shape, 1)
        for off, hi_ref, lo_ref in ((0, ahi_ref, alo_ref), (GDN_HEADS, bhi_ref, blo_ref)):
            hi, lo = _split(jnp.where(lane < GDN_HEADS, wide[:, off:off + LANES], 0.0))
            hi_ref[...] = hi
            lo_ref[...] = lo


def _regroup(w_in_t, layer, tr):
    d = w_in_t.shape[2]
    small = jax.ShapeDtypeStruct((d, LANES), BF16)
    whole = lambda i: (0, 0)
    return pl.pallas_call(
        functools.partial(_regroup_kernel, layer=layer, tr=tr),
        grid=(PROJ_COLS // tr,),
        in_specs=[pl.BlockSpec(memory_space=pl.ANY)],
        out_specs=[pl.BlockSpec((tr, d), lambda i: (i, 0))] + [pl.BlockSpec((d, LANES), whole)] * 4,
        out_shape=[jax.ShapeDtypeStruct((PROJ_COLS, d), BF16), small, small, small, small],
        scratch_shapes=[pltpu.VMEM((2, tr, d), F32), pltpu.VMEM((LANES, d), F32),
                        pltpu.SemaphoreType.DMA((2,)), pltpu.SemaphoreType.DMA],
        compiler_params=_params("arbitrary"),
        name="regroup",
    )(w_in_t)


def _proj(xb, w, tm, tn):
    m, k = xb.shape
    n = w.shape[0]
    return pl.pallas_call(
        _proj_kernel,
        grid=(m // tm, n // tn),
        in_specs=[pl.BlockSpec((tm, k), lambda i, j: (i, 0)),
                  pl.BlockSpec((tn, k), lambda i, j: (j, 0))],
        out_specs=pl.BlockSpec((tm, tn), lambda i, j: (i, j)),
        out_shape=jax.ShapeDtypeStruct((m, n), BF16),
        compiler_params=_params("parallel", "parallel"),
        name="proj",
    )(xb, w)


def _unit_lower_inverse4(als):
    c = CHUNK
    n = als[0].shape[1]
    row = lax.broadcasted_iota(I32, (c, n), 0)
    col = jnp.bitwise_and(lax.broadcasted_iota(I32, (c, n), 1), c - 1)
    shift = SOLVE_BLOCK.bit_length() - 1
    same = jnp.right_shift(row, shift) == jnp.right_shift(col, shift)
    eye = (row == col).astype(F32)
    cshift = c.bit_length() - 1
    brow = jnp.right_shift(lax.broadcasted_iota(I32, (n, n), 0), cshift)
    bcol = jnp.right_shift(lax.broadcasted_iota(I32, (n, n), 1), cshift)
    on_diag = brow == bcol

    def mm(x, y):
        yb = y.astype(BF16)
        bd = jnp.where(on_diag, jnp.concatenate([yb] * (n // c), axis=0), jnp.zeros((), BF16))
        return _dot(x.astype(BF16), bd)

    a_diag = [jnp.where(same, al, 0.0) for al in als]
    a_off = [al - ad for al, ad in zip(als, a_diag)]
    bp = [-ad for ad in a_diag]
    p = [eye + b for b in bp]
    for _ in range(3):
        bp = [mm(b, b) for b in bp]
        p = [x + mm(x, b) for x, b in zip(p, bp)]
    n1 = [mm(x, ao) for x, ao in zip(p, a_off)]
    n2 = [mm(x, x) for x in n1]
    q = [x + mm(y, x) for x, y in zip(p, n2)]
    return [x - mm(y, x) for x, y in zip(q, n1)]


def _shift_selectors(rt):
    sel = np.zeros((SHORT_CONV * rt, rt), np.float32)
    sel_halo = np.zeros((SHORT_CONV * 8, CONV_HALO), np.float32)
    for d in range(SHORT_CONV):
        for t in range(d, rt):
            sel[d * rt + t, t - d] = 1.0
        for t in range(d):
            sel_halo[d * 8 + t, CONV_HALO + t - d] = 1.0
    return jnp.asarray(sel, BF16), jnp.asarray(sel_halo, BF16)


def _gdn_intra_kernel(qkv_ref, prev_ref, x_ref, sel_ref, selh_ref, wah_ref, wal_ref, wbh_ref, wbl_ref, cw_ref,
                      prm_ref, u_ref, w_ref, qd_ref, m2_ref, eg_ref, xs_ref, *, rt, tiles_per_seq):
    c = CHUNK
    nc = rt // c
    first = (pl.program_id(0) % tiles_per_seq) == 0
    edge = jnp.where(first, 0.0, _dot(selh_ref[...], prev_ref[...]))

    def move_rows(lo, hi):
        moved = _dot(sel_ref[...], qkv_ref[:, lo:hi])
        for d in range(SHORT_CONV):
            xs_ref[d, 0:8, lo:hi] = moved[d * rt:d * rt + 8] + edge[d * 8:(d + 1) * 8, lo:hi]
            xs_ref[d, 8:rt, lo:hi] = moved[d * rt + 8:(d + 1) * rt]

    for part in range(3):
        move_rows(part * GDN_WIDTH, part * GDN_WIDTH + 2 * HEAD_DIM)

    xh, xl = _split(x_ref[...])

    def proj3(wh_ref, wl_ref):
        wh = wh_ref[...]
        return _dot(xh, wh) + _dot(xl, wh) + _dot(xh, wl_ref[...])

    a_raw = proj3(wah_ref, wal_ref)
    b_raw = proj3(wbh_ref, wbl_ref)
    sp_in = a_raw + prm_ref[1:2, :]
    softplus = jnp.maximum(sp_in, 0.0) + jnp.log(1.0 + jnp.exp(-jnp.abs(sp_in)))
    g = -jnp.exp(prm_ref[0:1, :]) * softplus
    beta = _sigmoid(b_raw)

    cshift = c.bit_length() - 1
    r2 = lax.broadcasted_iota(I32, (rt, rt), 0)
    c2 = lax.broadcasted_iota(I32, (rt, rt), 1)
    same_chunk = jnp.right_shift(r2, cshift) == jnp.right_shift(c2, cshift)
    ltri = jnp.where(r2 >= c2, jnp.where(same_chunk, 1.0, 0.0), 0.0).astype(BF16)
    g_hi = g.astype(BF16)
    g_r = g - g_hi.astype(F32)
    g_mid = g_r.astype(BF16)
    g_lo = (g_r - g_mid.astype(F32)).astype(BF16)
    gc = _dot(ltri, g_hi) + _dot(ltri, g_mid) + _dot(ltri, g_lo)
    gct = gc.T
    egc = jnp.exp(gc)
    gend = jnp.concatenate(
        [jnp.broadcast_to(gc[ci * c + c - 1:ci * c + c, :], (c, LANES)) for ci in range(nc)], axis=0)
    kfac = jnp.exp(gend - gc)
    bege = beta * egc
    for ci in range(nc):
        last = ci * c + c - 1
        eg_ref[ci * GDN_HEADS:(ci + 1) * GDN_HEADS, :] = jnp.broadcast_to(
            jnp.exp(gct[0:GDN_HEADS, last:last + 1]), (GDN_HEADS, LANES))

    lane_t = lax.broadcasted_iota(I32, (rt, LANES), 1) < c
    lane_lo = lax.broadcasted_iota(I32, (c, LANES), 1) < c
    lcol = jnp.bitwise_and(lax.broadcasted_iota(I32, (c, LANES), 1), c - 1)
    rowi = lax.broadcasted_iota(I32, (c, LANES), 0)
    causal = rowi >= lcol
    strict = rowi > lcol

    def conv(base, h):
        lo, hi = base + h * HEAD_DIM, base + (h + 1) * HEAD_DIM
        acc = cw_ref[SHORT_CONV - 1:SHORT_CONV, lo:hi] * xs_ref[0, :, lo:hi]
        for j in range(SHORT_CONV - 1):
            acc = acc + cw_ref[j:j + 1, lo:hi] * xs_ref[SHORT_CONV - 1 - j, :, lo:hi]
        return _silu(acc)

    a_pairs = [[None] * (GDN_HEADS // 2) for _ in range(nc)]
    rhs_pairs = [[None] * (GDN_HEADS // 2) for _ in range(nc)]
    for p in range(GDN_HEADS // 2):
        if p + 1 < GDN_HEADS // 2:
            for part in range(3):
                lo = part * GDN_WIDTH + (p + 1) * 2 * HEAD_DIM
                move_rows(lo, lo + 2 * HEAD_DIM)
        ks, kbs, qs, kds, rhss = [], [], [], [], []
        for h in (2 * p, 2 * p + 1):
            q = conv(0, h)
            k = conv(GDN_WIDTH, h)
            v = conv(2 * GDN_WIDTH, h)
            q = q * lax.rsqrt(jnp.sum(q * q, axis=-1, keepdims=True) + EPS) * (HEAD_DIM ** -0.5)
            k = k * lax.rsqrt(jnp.sum(k * k, axis=-1, keepdims=True) + EPS)
            beta_h = beta[:, h:h + 1]
            qd_ref[:, h * HEAD_DIM:(h + 1) * HEAD_DIM] = (q * egc[:, h:h + 1]).astype(BF16)
            ks.append(k)
            kbs.append(k * beta_h)
            qs.append(q)
            kds.append(k * kfac[:, h:h + 1])
            rhss.append(jnp.concatenate([v * beta_h, k * bege[:, h:h + 1]], axis=1))
        h0, h1 = 2 * p, 2 * p + 1
        gch = jnp.where(lane_t, gc[:, h0:h0 + 1], gc[:, h1:h1 + 1])
        for ci in range(nc):
            rows = slice(ci * c, (ci + 1) * c)
            wk = jnp.concatenate([ks[0][rows], ks[1][rows]], axis=0).astype(BF16)
            lhs = jnp.concatenate([kbs[0][rows], qs[0][rows], kbs[1][rows], qs[1][rows]],
                                  axis=0).astype(BF16)
            out = _dot_nt(lhs, wk)
            gcrow = jnp.concatenate([gct[h0:h0 + 1, rows], gct[h1:h1 + 1, rows]], axis=1)
            diff = gch[rows] - gcrow
            decay = jnp.where(causal, jnp.exp(jnp.where(causal, diff, 0.0)), 0.0)
            a_pairs[ci][p] = jnp.where(strict, jnp.where(lane_lo, out[0:c], out[2 * c:3 * c]) * decay, 0.0)
            qk = jnp.where(lane_lo, out[c:2 * c], out[3 * c:4 * c]) * decay
            kdt = jnp.concatenate([kds[0][rows], kds[1][rows]], axis=0).T
            m0 = ci * 3 * c
            m2_ref[m0:m0 + c, p * LANES:(p + 1) * LANES] = qk.astype(BF16)
            m2_ref[m0 + c:m0 + 3 * c, p * LANES:(p + 1) * LANES] = kdt.astype(BF16)
            rhs_pairs[ci][p] = (rhss[0][rows], rhss[1][rows])

    zeros = jnp.zeros((c, 2 * HEAD_DIM), BF16)
    ngrp = GDN_HEADS // 4
    tls = _unit_lower_inverse4(
        [jnp.concatenate([a_pairs[ci][2 * grp], a_pairs[ci][2 * grp + 1]], axis=1)
         for ci in range(nc) for grp in range(ngrp)])
    for ci in range(nc):
        rows = slice(ci * c, (ci + 1) * c)
        for grp in range(ngrp):
            tl = tls[ci * ngrp + grp]
            for j in range(2):
                p = 2 * grp + j
                r0, r1 = rhs_pairs[ci][p]
                bd = jnp.concatenate([jnp.concatenate([r0.astype(BF16), zeros], axis=1),
                                      jnp.concatenate([zeros, r1.astype(BF16)], axis=1)], axis=0)
                sol = _dot(tl[:, j * LANES:(j + 1) * LANES].astype(BF16), bd)
                for i in range(2):
                    h = 2 * p + i
                    lo, hi = h * HEAD_DIM, (h + 1) * HEAD_DIM
                    u_ref[rows, lo:hi] = sol[:, 2 * i * HEAD_DIM:(2 * i + 1) * HEAD_DIM]
                    w_ref[rows, lo:hi] = sol[:, (2 * i + 1) * HEAD_DIM:(2 * i + 2) * HEAD_DIM].astype(BF16)


def _gdn_intra(proj, x, wa_hi, wa_lo, wb_hi, wb_lo, conv_w, prm, seq, rt):
    t = x.shape[0]
    nc = rt // CHUNK
    kern = functools.partial(_gdn_intra_kernel, rt=rt, tiles_per_seq=seq // rt)
    full = lambda i: (0, 0)
    tile = lambda i: (i, 0)
    sel, sel_halo = _shift_selectors(rt)
    return pl.pallas_call(
        kern,
        grid=(t // rt,),
        in_specs=[
            pl.BlockSpec((rt, 3 * GDN_WIDTH), tile),
            pl.BlockSpec((CONV_HALO, 3 * GDN_WIDTH),
                         lambda i: (jnp.maximum(i * (rt // CONV_HALO) - 1, 0), 0)),
            pl.BlockSpec((rt, D_MODEL), tile),
            pl.BlockSpec(sel.shape, full),
            pl.BlockSpec(sel_halo.shape, full),
            pl.BlockSpec((D_MODEL, LANES), full),
            pl.BlockSpec((D_MODEL, LANES), full),
            pl.BlockSpec((D_MODEL, LANES), full),
            pl.BlockSpec((D_MODEL, LANES), full),
            pl.BlockSpec((SHORT_CONV, 3 * GDN_WIDTH), full),
            pl.BlockSpec((8, LANES), full),
        ],
        out_specs=[pl.BlockSpec((rt, GDN_WIDTH), tile),
                   pl.BlockSpec((rt, GDN_WIDTH), tile),
                   pl.BlockSpec((rt, GDN_WIDTH), tile),
                   pl.BlockSpec((nc * 3 * CHUNK, GDN_HEADS // 2 * LANES), tile),
                   pl.BlockSpec((nc * GDN_HEADS, LANES), tile)],
        out_shape=[jax.ShapeDtypeStruct((t, GDN_WIDTH), F32),
                   jax.ShapeDtypeStruct((t, GDN_WIDTH), BF16),
                   jax.ShapeDtypeStruct((t, GDN_WIDTH), BF16),
                   jax.ShapeDtypeStruct((t // CHUNK * 3 * CHUNK, GDN_HEADS // 2 * LANES), BF16),
                   jax.ShapeDtypeStruct((t // CHUNK * GDN_HEADS, LANES), F32)],
        scratch_shapes=[pltpu.VMEM((SHORT_CONV, rt, 3 * GDN_WIDTH), F32)],
        compiler_params=_params("parallel"),
        name="gdn_intra",
    )(proj, proj, x, sel, sel_halo, wa_hi, wa_lo, wb_hi, wb_lo, conv_w, prm)


def _gdn_scan_kernel(u_ref, w_ref, qd_ref, m2_ref, eg_ref, z_ref, ong_ref, o_ref, s_ref, *, nck):
    c = CHUNK
    nseq = u_ref.shape[0]

    @pl.when(pl.program_id(1) == 0)
    def _():
        s_ref[...] = jnp.zeros_like(s_ref)

    ong = ong_ref[...]
    zeros = jnp.zeros((c, HEAD_DIM), BF16)
    span = lambda h: slice(h * HEAD_DIM, (h + 1) * HEAD_DIM)
    chains = [(b, h) for b in range(nseq) for h in range(GDN_HEADS)]
    states = {bh: s_ref[bh[0], bh[1]] for bh in chains}
    for ci in range(nck):
        rows = slice(ci * c, (ci + 1) * c)
        rs = {(b, h): _dot(jnp.concatenate([w_ref[b, rows, span(h)], qd_ref[b, rows, span(h)]], axis=0),
                           states[b, h].astype(BF16)) for b, h in chains}
        v_new = {(b, h): (u_ref[b, rows, span(h)] - rs[b, h][:c]).astype(BF16) for b, h in chains}
        r2s = {}
        for b in range(nseq):
            for p in range(GDN_HEADS // 2):
                bd = jnp.concatenate([jnp.concatenate([v_new[b, 2 * p], zeros], axis=1),
                                      jnp.concatenate([zeros, v_new[b, 2 * p + 1]], axis=1)], axis=0)
                r2s[b, p] = _dot(m2_ref[b, ci * 3 * c:(ci + 1) * 3 * c, p * LANES:(p + 1) * LANES], bd)
        for b, h in chains:
            half = span(h % 2)
            r2 = r2s[b, h // 2]
            decay = eg_ref[b, ci * GDN_HEADS + h:ci * GDN_HEADS + h + 1, :]
            states[b, h] = states[b, h] * decay + r2[c:, half]
            o = rs[b, h][c:] + r2[:c, half]
            o = o * lax.rsqrt(jnp.mean(o * o, axis=-1, keepdims=True) + EPS) * ong
            o = o * _silu(z_ref[b, rows, span(h)].astype(F32))
            o_ref[b, rows, span(h)] = o.astype(o_ref.dtype)
    for b, h in chains:
        s_ref[b, h] = states[b, h]


def _gdn_scan(u, w, qd, m2, eg, proj, ong, batch, seq):
    t = batch * seq
    nck = min(SCAN_CHUNKS, seq // CHUNK)
    c = nck * CHUNK
    nch = seq // c
    ns = SCAN_SEQS if batch % SCAN_SEQS == 0 else 1
    groups = batch // ns
    split = lambda a: a.reshape(ns, a.shape[0] // ns, a.shape[1])
    blk = lambda b, n: (0, b * nch + n, 0)
    out = pl.pallas_call(
        functools.partial(_gdn_scan_kernel, nck=nck),
        grid=(groups, nch),
        in_specs=[
            pl.BlockSpec((ns, c, GDN_WIDTH), blk),
            pl.BlockSpec((ns, c, GDN_WIDTH), blk),
            pl.BlockSpec((ns, c, GDN_WIDTH), blk),
            pl.BlockSpec((ns, 3 * c, GDN_HEADS // 2 * LANES), blk),
            pl.BlockSpec((ns, nck * GDN_HEADS, LANES), blk),
            pl.BlockSpec((ns, c, GDN_WIDTH), lambda b, n: (0, b * nch + n, 3)),
            pl.BlockSpec((1, HEAD_DIM), lambda b, n: (0, 0)),
        ],
        out_specs=pl.BlockSpec((ns, c, GDN_WIDTH), blk),
        out_shape=jax.ShapeDtypeStruct((ns, t // ns, GDN_WIDTH), BF16),
        scratch_shapes=[pltpu.VMEM((ns, GDN_HEADS, HEAD_DIM, HEAD_DIM), F32)],
        compiler_params=_params("parallel", "arbitrary"),
        name="gdn_scan",
    )(split(u), split(w), split(qd), split(m2), split(eg), split(proj), ong)
    return out.reshape(t, GDN_WIDTH)


def _pack_bf16_pairs(y):
    n = y.shape[1] // 2
    yb = y.astype(BF16).astype(F32)
    lo = lax.bitcast_convert_type(yb[:, :n], U32)
    hi = lax.bitcast_convert_type(yb[:, n:], U32)
    return jnp.bitwise_or(jnp.right_shift(lo, 16), jnp.bitwise_and(hi, HI_MASK))


def _unpack_bf16_pairs(w):
    lo = lax.bitcast_convert_type(jnp.left_shift(w, 16), F32)
    hi = lax.bitcast_convert_type(jnp.bitwise_and(w, HI_MASK), F32)
    return lo, hi


def _mixer_kernel(glu_ref, ga_ref, gb_ref, o_ref, x_ref, woa_ref, wob_ref, wo_ref, dww_ref, vec_ref,
                  x1_ref, x1b_ref, x1p_ref, ubuf_ref, sh_ref, conv_ref, *, ts, rc):
    halo = CONF_HALO

    @pl.when(pl.program_id(1) == 0)
    def _():
        ubuf_ref[0:halo, :] = jnp.zeros((halo, CONF_CH), F32)

    @pl.when(pl.program_id(1) != 0)
    def _():
        ubuf_ref[0:halo, :] = ubuf_ref[ts:ts + halo, :]

    glu_a = glu_ref[:, :CONF_CH].astype(F32)
    glu_b = glu_ref[:, CONF_CH:].astype(F32)
    ubuf_ref[halo:halo + ts, :] = glu_a * _sigmoid(glu_b)

    dw_b = vec_ref[0:1, :]
    cln_g = vec_ref[1:2, :]
    cln_b = vec_ref[2:3, :]
    ln1_g = vec_ref[3:4, :]
    ln1_b = vec_ref[4:5, :]

    span = ts + halo - 8
    for s in range(1, 8):
        sh_ref[s - 1] = ubuf_ref[s:s + span, :]

    def tap_rows(j, r0):
        o = halo - (CONF_KERNEL - 1) + j
        q, s = o // 8, o % 8
        if s == 0:
            return ubuf_ref[r0 + o:r0 + o + rc, :]
        return sh_ref[s - 1, r0 + 8 * q:r0 + 8 * q + rc, :]

    hs = ts // 2
    for h0 in range(0, ts, hs):
        rows = slice(h0, h0 + hs)
        gated_a = _sigmoid(ga_ref[rows, :].astype(F32)) * _dot(o_ref[rows, :], woa_ref[...])
        for r0 in range(h0, h0 + hs, rc):
            tap_w = lambda j: jnp.concatenate([dww_ref[j]] * (rc // 8), axis=0)
            acc = tap_w(0) * tap_rows(0, r0)
            for j in range(1, CONF_KERNEL):
                acc = acc + tap_w(j) * tap_rows(j, r0)
            conv_ref[r0:r0 + rc, :] = acc
        uc = _silu(_layer_norm(conv_ref[rows, :] + dw_b, cln_g, cln_b))
        branch_b = _dot(uc.astype(BF16), wob_ref[...])
        hmix = gated_a + _sigmoid(gb_ref[rows, :].astype(F32)) * branch_b
        mix = _dot(hmix.astype(BF16), wo_ref[...])
        x1 = _layer_norm(DEEPNORM_ALPHA * x_ref[rows, :] + mix, ln1_g, ln1_b)
        x1_ref[rows, :] = x1
        x1b_ref[rows, :] = x1.astype(BF16)
        x1p_ref[rows, :] = _pack_bf16_pairs(x1)


def _mixer(proj, o_gdn, x, woa, wob, wo, dww, vec, batch, seq, ts):
    t = batch * seq
    nt = seq // ts
    rows = lambda b, n: b * nt + n
    full = lambda b, n: (0, 0)
    kern = functools.partial(_mixer_kernel, ts=ts, rc=MIXER_CONV_ROWS)
    return pl.pallas_call(
        kern,
        grid=(batch, nt),
        in_specs=[
            pl.BlockSpec((ts, 2 * CONF_CH), lambda b, n: (rows(b, n), 2)),
            pl.BlockSpec((ts, D_MODEL), lambda b, n: (rows(b, n), 6)),
            pl.BlockSpec((ts, D_MODEL), lambda b, n: (rows(b, n), 7)),
            pl.BlockSpec((ts, GDN_WIDTH), lambda b, n: (rows(b, n), 0)),
            pl.BlockSpec((ts, D_MODEL), lambda b, n: (rows(b, n), 0)),
            pl.BlockSpec((GDN_WIDTH, D_MODEL), full),
            pl.BlockSpec((CONF_CH, D_MODEL), full),
            pl.BlockSpec((D_MODEL, D_MODEL), full),
            pl.BlockSpec((CONF_KERNEL, 8, CONF_CH), lambda b, n: (0, 0, 0)),
            pl.BlockSpec((8, D_MODEL), full),
        ],
        out_specs=[pl.BlockSpec((ts, D_MODEL), lambda b, n: (rows(b, n), 0)),
                   pl.BlockSpec((ts, D_MODEL), lambda b, n: (rows(b, n), 0)),
                   pl.BlockSpec((ts, D_MODEL // 2), lambda b, n: (rows(b, n), 0))],
        out_shape=[jax.ShapeDtypeStruct((t, D_MODEL), F32),
                   jax.ShapeDtypeStruct((t, D_MODEL), BF16),
                   jax.ShapeDtypeStruct((t, D_MODEL // 2), U32)],
        scratch_shapes=[pltpu.VMEM((CONF_HALO + ts, CONF_CH), F32),
                        pltpu.VMEM((7, ts + CONF_HALO - 8, CONF_CH), F32),
                        pltpu.VMEM((ts, CONF_CH), F32)],
        compiler_params=_params("parallel", "arbitrary"),
        name="mixer",
    )(proj, proj, proj, o_gdn, x, woa, wob, wo, dww, vec)


def _route_kernel(x_ref, wrh_ref, wrl_ref, bias_ref, eidx_ref, wts_ref, rank_ref, cnt_ref, carry_ref, *, tt):
    @pl.when(pl.program_id(0) == 0)
    def _():
        carry_ref[...] = jnp.zeros_like(carry_ref)

    xh, xl = _split(x_ref[...])
    wrh = wrh_ref[...]
    logits = _dot_nt(wrh, xh) + _dot_nt(wrh, xl) + _dot_nt(wrl_ref[...], xh)
    s = _sigmoid(logits)
    biased = s + bias_ref[...]

    sub = lax.broadcasted_iota(I32, (GROUP_SIZE, tt), 0)
    groups = [biased[g * GROUP_SIZE:(g + 1) * GROUP_SIZE, :] for g in range(N_GROUPS)]
    gs = []
    for bg in groups:
        m1 = jnp.max(bg, axis=0, keepdims=True)
        first = jnp.min(jnp.where(bg == m1, sub, GROUP_SIZE), axis=0, keepdims=True)
        m2 = jnp.max(jnp.where(sub == first, -jnp.inf, bg), axis=0, keepdims=True)
        gs.append(m1 + m2)

    masked_parts = []
    for g in range(N_GROUPS):
        beaten = jnp.zeros((1, tt), I32)
        for o in range(N_GROUPS):
            if o == g:
                continue
            wins = (gs[o] >= gs[g]) if o < g else (gs[o] > gs[g])
            beaten = beaten + wins.astype(I32)
        keep = jnp.broadcast_to(beaten < TOPK_GROUPS, (GROUP_SIZE, tt))
        masked_parts.append(jnp.where(keep, groups[g], -jnp.inf))
    masked = jnp.concatenate(masked_parts, axis=0)

    eiota = lax.broadcasted_iota(I32, (N_EXPERTS, tt), 0)
    sel_all = jnp.zeros((N_EXPERTS, tt), F32)
    picks = []
    for _ in range(TOP_K):
        m = jnp.max(masked, axis=0, keepdims=True)
        idx = jnp.min(jnp.where(masked == m, eiota, N_EXPERTS), axis=0, keepdims=True)
        onehot = eiota == idx
        picks.append((idx, onehot))
        sel_all = jnp.where(onehot, 1.0, sel_all)
        masked = jnp.where(onehot, -jnp.inf, masked)

    tr = lax.broadcasted_iota(I32, (tt, tt), 0)
    tc = lax.broadcasted_iota(I32, (tt, tt), 1)
    before = (tr < tc).astype(BF16)
    sel_b = sel_all.astype(BF16)
    carry = carry_ref[...]
    rank_all = _dot(sel_b, before) + carry[:, 0:1]
    carry_new = carry + _dot(sel_b, jnp.ones((tt, LANES), BF16))
    carry_ref[...] = carry_new
    cnt_ref[...] = carry_new

    s_sel = [jnp.sum(jnp.where(oh, s, 0.0), axis=0, keepdims=True) for _, oh in picks]
    total = s_sel[0]
    for v in s_sel[1:]:
        total = total + v
    for k, (idx, oh) in enumerate(picks):
        eidx_ref[k:k + 1, :] = idx
        wts_ref[k:k + 1, :] = s_sel[k] / total * ROUTED_SCALE
        rank_ref[k:k + 1, :] = jnp.sum(jnp.where(oh, rank_all, 0.0), axis=0, keepdims=True).astype(I32)


def _route(x1, wr_hi, wr_lo, bias, tt):
    t = x1.shape[0]
    kern = functools.partial(_route_kernel, tt=tt)
    return pl.pallas_call(
        kern,
        grid=(t // tt,),
        in_specs=[pl.BlockSpec((tt, D_MODEL), lambda i: (i, 0)),
                  pl.BlockSpec((N_EXPERTS, D_MODEL), lambda i: (0, 0)),
                  pl.BlockSpec((N_EXPERTS, D_MODEL), lambda i: (0, 0)),
                  pl.BlockSpec((N_EXPERTS, tt), lambda i: (0, 0))],
        out_specs=[pl.BlockSpec((TOP_K, tt), lambda i: (0, i)),
                   pl.BlockSpec((TOP_K, tt), lambda i: (0, i)),
                   pl.BlockSpec((TOP_K, tt), lambda i: (0, i)),
                   pl.BlockSpec((N_EXPERTS, LANES), lambda i: (0, 0))],
        out_shape=[jax.ShapeDtypeStruct((TOP_K, t), I32),
                   jax.ShapeDtypeStruct((TOP_K, t), F32),
                   jax.ShapeDtypeStruct((TOP_K, t), I32),
                   jax.ShapeDtypeStruct((N_EXPERTS, LANES), F32)],
        scratch_shapes=[pltpu.VMEM((N_EXPERTS, LANES), F32)],
        compiler_params=_params("arbitrary"),
        name="route",
    )(x1, wr_hi, wr_lo, bias)


def _plan_kernel(cnt_ref, pstart_ref, plan_ref):
    e, nb = N_EXPERTS, plan_ref.shape[1]
    counts = cnt_ref[...]
    nblk = jnp.floor((counts + (ROW_BLOCK - 1)) * (1.0 / ROW_BLOCK))
    hi = jnp.floor(nblk * (1.0 / 256.0))
    lo = nblk - 256.0 * hi
    r = lax.broadcasted_iota(I32, (e, e), 0)
    c = lax.broadcasted_iota(I32, (e, e), 1)
    ltri = (r >= c).astype(BF16)
    bend = 256.0 * _dot(ltri, hi.astype(BF16)) + _dot(ltri, lo.astype(BF16))
    pend = bend * ROW_BLOCK
    pstart = pend - nblk * ROW_BLOCK
    pstart_ref[...] = pstart.astype(I32)

    bs = (lax.broadcasted_iota(I32, (e, nb), 1) * ROW_BLOCK).astype(F32)
    pend_b = jnp.broadcast_to(pend[:, 0:1], (e, nb))
    pstart_b = jnp.broadcast_to(pstart[:, 0:1], (e, nb))
    used_b = jnp.broadcast_to((pstart + counts)[:, 0:1], (e, nb))
    owner = jnp.sum(jnp.where(pend_b <= bs, 1.0, 0.0), axis=0, keepdims=True)
    inside = jnp.where(pstart_b <= bs, jnp.where(bs < pend_b, 1.0, 0.0), 0.0)
    real = jnp.sum(inside * jnp.clip(used_b - bs, 0.0, float(ROW_BLOCK)), axis=0, keepdims=True)
    plan_ref[0:1, :] = jnp.minimum(owner, float(e - 1)).astype(I32)
    plan_ref[1:2, :] = real.astype(I32)
    plan_ref[2:8, :] = jnp.zeros((6, nb), I32)


def _plan(cnt, n_blocks):
    nb = -(-n_blocks // LANES) * LANES
    pstart, plan = pl.pallas_call(
        _plan_kernel,
        out_shape=[jax.ShapeDtypeStruct((N_EXPERTS, LANES), I32),
                   jax.ShapeDtypeStruct((8, nb), I32)],
        name="plan",
    )(cnt)
    return pstart[:, 0], plan[0, :n_blocks], plan[1, :n_blocks]


def _dest_kernel(eidx_ref, rank_ref, pstart_ref, dest_ref):
    eidx = eidx_ref[...]
    acc = rank_ref[...]
    for e in range(N_EXPERTS):
        acc = acc + jnp.where(eidx == e, pstart_ref[e], 0)
    dest_ref[...] = acc


def _dest(eidx_t, rank_t, pstart, tt):
    t = eidx_t.shape[1]
    return pl.pallas_call(
        _dest_kernel,
        grid=(t // tt,),
        in_specs=[pl.BlockSpec((TOP_K, tt), lambda i: (0, i)),
                  pl.BlockSpec((TOP_K, tt), lambda i: (0, i)),
                  pl.BlockSpec(memory_space=pltpu.SMEM)],
        out_specs=pl.BlockSpec((TOP_K, tt), lambda i: (0, i)),
        out_shape=jax.ShapeDtypeStruct((TOP_K, t), I32),
        compiler_params=_params("parallel"),
        name="dest",
    )(eidx_t, rank_t, pstart)


def _dispatch(dest_kt, x1p, n_rows):
    t, width = x1p.shape
    info = plsc.get_sparse_core_info()
    nc, nw = info.num_cores, info.num_cores * info.num_subcores
    chunk = SC_SCATTER_ROWS
    per_w = t // nw
    n_chunks = per_w // chunk
    assert per_w % chunk == 0
    idx = dest_kt.reshape(TOP_K, nw, n_chunks, chunk).transpose(1, 2, 0, 3).reshape(nw, n_chunks * TOP_K, chunk)
    mesh = plsc.VectorSubcoreMesh(core_axis_name="c", subcore_axis_name="s")

    @functools.partial(
        pl.kernel, mesh=mesh, name="dispatch",
        out_type=jax.ShapeDtypeStruct((n_rows, width), x1p.dtype),
        scratch_types=[pltpu.VMEM((n_chunks * TOP_K, chunk), I32),
                       pltpu.VMEM((chunk, width), x1p.dtype),
                       pltpu.SemaphoreType.DMA])
    def scatter(x_hbm, idx_hbm, xs_hbm, idx_v, rows_v, sem):
        wid = lax.axis_index("s") * nc + lax.axis_index("c")
        base = wid * per_w
        pltpu.sync_copy(idx_hbm.at[wid], idx_v)
        for j in range(n_chunks):
            pltpu.sync_copy(x_hbm.at[pl.ds(base + j * chunk, chunk)], rows_v)
            copies = [pltpu.make_async_copy(rows_v, xs_hbm.at[idx_v.at[j * TOP_K + k]], sem)
                      for k in range(TOP_K)]
            for cp in copies:
                cp.start()
            for cp in copies:
                cp.wait()

    return scatter(x1p, idx)


def _gather_rows(table, idx):
    n = idx.shape[0]
    width = table.shape[1]
    info = plsc.get_sparse_core_info()
    nc, nw = info.num_cores, info.num_cores * info.num_subcores
    chunk = SC_GATHER_ROWS
    per_w = n // nw
    n_chunks = per_w // chunk
    assert per_w % (2 * chunk) == 0
    mesh = plsc.VectorSubcoreMesh(core_axis_name="c", subcore_axis_name="s")

    @functools.partial(
        pl.kernel, mesh=mesh, name="gather_rows",
        out_type=jax.ShapeDtypeStruct((n, width), table.dtype),
        scratch_types=[pltpu.VMEM((n_chunks, chunk), I32),
                       pltpu.VMEM((2, chunk, width), table.dtype),
                       pltpu.SemaphoreType.DMA((2,)),
                       pltpu.SemaphoreType.DMA((2,))])
    def gather(table_hbm, idx_hbm, out_hbm, idx_v, rows_v, gsem, osem):
        wid = lax.axis_index("s") * nc + lax.axis_index("c")
        base = wid * per_w
        pltpu.sync_copy(idx_hbm.at[wid], idx_v)

        def fetch(j, b):
            return pltpu.make_async_copy(table_hbm.at[idx_v.at[j]], rows_v.at[b], gsem.at[b])

        def put(j, b):
            return pltpu.make_async_copy(rows_v.at[b], out_hbm.at[pl.ds(base + j * chunk, chunk)], osem.at[b])

        fetch(0, 0).start()

        @pl.loop(0, n_chunks, step=2)
        def _(j0):
            for b in range(2):
                j = j0 + b
                fetch(j, b).wait()

                @pl.when(j + 1 < n_chunks)
                def _():
                    @pl.when(j >= 1)
                    def _():
                        put(j - 1, 1 - b).wait()

                    fetch(j + 1, 1 - b).start()

                put(j, b).start()

        put(n_chunks - 2, 0).wait()
        put(n_chunks - 1, 1).wait()

    return gather(table, idx.reshape(nw, n_chunks, chunk))


def _xs_copy(xs_hbm, xbuf, isem, j, slot):
    return pltpu.make_async_copy(xs_hbm.at[pl.ds(j * ROW_BLOCK, ROW_BLOCK)], xbuf.at[slot], isem.at[slot])


def _ys_copy(ybuf, ys_hbm, osem, j, slot):
    return pltpu.make_async_copy(ybuf.at[slot], ys_hbm.at[pl.ds(j * ROW_BLOCK, ROW_BLOCK)], osem.at[slot])


def _experts_kernel(be_ref, nv_ref, xs_hbm, wg_ref, wu_ref, wd_ref, ys_hbm,
                    xbuf, ybuf, wgu_s, wd_s, cur_ref, isem, osem):
    i = pl.program_id(0)
    n_valid = nv_ref[i]
    half = EXPERT_SUB
    slot = lax.rem(i, EXPERT_IN_SLOTS)
    oslot = lax.rem(i, 2)

    @pl.when(i == 0)
    def _():
        cur_ref[0] = -1
        for j in range(2):
            @pl.when(nv_ref[j] > 0)
            def _():
                _xs_copy(xs_hbm, xbuf, isem, j, j).start()

    @pl.when(nv_ref[i + 2] > 0)
    def _():
        _xs_copy(xs_hbm, xbuf, isem, i + 2, lax.rem(i + 2, EXPERT_IN_SLOTS)).start()

    @pl.when((i >= 2) & (nv_ref[jnp.maximum(i - 2, 0)] > 0))
    def _():
        _ys_copy(ybuf, ys_hbm, osem, i - 2, oslot).wait()

    @pl.when((n_valid > 0) & (cur_ref[0] != be_ref[i]))
    def _():
        wgu_s[:, :EXPERT_FF] = wg_ref[...].astype(BF16)
        wgu_s[:, EXPERT_FF:] = wu_ref[...].astype(BF16)
        wd_s[...] = wd_ref[...].astype(BF16)
        cur_ref[0] = be_ref[i]

    def rows_bf16(r0):
        valid = lax.broadcasted_iota(I32, (half, xbuf.shape[2]), 0) + r0 < n_valid
        lo, hi = _unpack_bf16_pairs(jnp.where(valid, xbuf[slot, r0:r0 + half, :], jnp.zeros((), U32)))
        return jnp.concatenate([lo.astype(BF16), hi.astype(BF16)], axis=1)

    def hidden(gu):
        return (_silu(gu[:, :EXPERT_FF]) * gu[:, EXPERT_FF:]).astype(BF16)

    @pl.when(n_valid > 0)
    def _():
        _xs_copy(xs_hbm, xbuf, isem, i, slot).wait()

    zeros = jnp.zeros((half, ybuf.shape[2]), ybuf.dtype)
    for r0 in range(0, ROW_BLOCK, 2 * half):
        r1 = r0 + half

        @pl.when(n_valid > r1)
        def _():
            xa, xb = rows_bf16(r0), rows_bf16(r1)
            gua = _dot(xa, wgu_s[...])
            gub = _dot(xb, wgu_s[...])
            ya = _dot(hidden(gua), wd_s[...])
            yb = _dot(hidden(gub), wd_s[...])
            ybuf[oslot, r0:r1, :] = _pack_bf16_pairs(ya)
            ybuf[oslot, r1:r1 + half, :] = _pack_bf16_pairs(yb)

        @pl.when((n_valid > r0) & (n_valid <= r1))
        def _():
            ya = _dot(hidden(_dot(rows_bf16(r0), wgu_s[...])), wd_s[...])
            ybuf[oslot, r0:r1, :] = _pack_bf16_pairs(ya)
            ybuf[oslot, r1:r1 + half, :] = zeros

        @pl.when((n_valid > 0) & (n_valid <= r0))
        def _():
            ybuf[oslot, r0:r1, :] = zeros
            ybuf[oslot, r1:r1 + half, :] = zeros

    @pl.when(n_valid > 0)
    def _():
        _ys_copy(ybuf, ys_hbm, osem, i, oslot).start()


def _experts(block_e, n_valid, xs, wg, wu, wd, layer):
    n_rows, width = xs.shape
    steps = n_rows // ROW_BLOCK + 2
    be = jnp.concatenate([block_e, jnp.full((2,), N_EXPERTS - 1, I32)])
    nv = jnp.concatenate([n_valid, jnp.zeros((4,), I32)])
    grid_spec = pltpu.PrefetchScalarGridSpec(
        num_scalar_prefetch=2,
        grid=(steps,),
        in_specs=[pl.BlockSpec(memory_space=pl.ANY),
                  pl.BlockSpec((None, None, D_MODEL, EXPERT_FF), lambda i, be, nv: (layer, be[i], 0, 0)),
                  pl.BlockSpec((None, None, D_MODEL, EXPERT_FF), lambda i, be, nv: (layer, be[i], 0, 0)),
                  pl.BlockSpec((None, None, EXPERT_FF, D_MODEL), lambda i, be, nv: (layer, be[i], 0, 0))],
        out_specs=pl.BlockSpec(memory_space=pl.ANY),
        scratch_shapes=[pltpu.VMEM((EXPERT_IN_SLOTS, ROW_BLOCK, width), xs.dtype),
                        pltpu.VMEM((2, ROW_BLOCK, width), xs.dtype),
                        pltpu.VMEM((D_MODEL, 2 * EXPERT_FF), BF16),
                        pltpu.VMEM((EXPERT_FF, D_MODEL), BF16),
                        pltpu.SMEM((1,), I32),
                        pltpu.SemaphoreType.DMA((EXPERT_IN_SLOTS,)),
                        pltpu.SemaphoreType.DMA((2,))],
    )
    return pl.pallas_call(
        _experts_kernel,
        grid_spec=grid_spec,
        out_shape=jax.ShapeDtypeStruct((n_rows, width), xs.dtype),
        compiler_params=_params("arbitrary"),
        name="experts",
    )(be, nv, xs, wg, wu, wd)


def _shared_kernel(x1b_ref, wsg_ref, wsu_ref, wsd_ref, anchor_ref, sh_ref):
    del anchor_ref
    xb = x1b_ref[...]
    hid = (_silu(_dot(xb, wsg_ref[...])) * _dot(xb, wsu_ref[...])).astype(BF16)
    sh_ref[...] = _dot(hid, wsd_ref[...]).astype(sh_ref.dtype)


def _shared(x1b, wsg, wsu, wsd, anchor, tt, row0, nrows):
    full = lambda i: (0, 0)
    first = row0 // tt
    return pl.pallas_call(
        _shared_kernel,
        grid=(nrows // tt,),
        in_specs=[pl.BlockSpec((tt, D_MODEL), lambda i: (first + i, 0)),
                  pl.BlockSpec((D_MODEL, SHARED_FF), full),
                  pl.BlockSpec((D_MODEL, SHARED_FF), full),
                  pl.BlockSpec((SHARED_FF, D_MODEL), full),
                  pl.BlockSpec(memory_space=pl.ANY)],
        out_specs=pl.BlockSpec((tt, D_MODEL), lambda i: (i, 0)),
        out_shape=jax.ShapeDtypeStruct((nrows, D_MODEL), BF16),
        compiler_params=_params("parallel"),
        name="shared",
    )(x1b, wsg, wsu, wsd, anchor)


def _combine_kernel(yg_ref, w_ref, x1_ref, sha_ref, shb_ref, vec_ref, x2_ref, *, half_steps):
    in_first = pl.program_id(0) < half_steps
    shared = jnp.where(in_first, sha_ref[...], shb_ref[...]).astype(F32)
    w = w_ref[...]
    half = D_MODEL // 2
    acc_lo, acc_hi = shared[:, :half], shared[:, half:]
    for k in range(TOP_K):
        lo, hi = _unpack_bf16_pairs(yg_ref[k])
        acc_lo = acc_lo + w[:, k:k + 1] * lo
        acc_hi = acc_hi + w[:, k:k + 1] * hi
    acc = jnp.concatenate([acc_lo, acc_hi], axis=1)
    x2 = _layer_norm(DEEPNORM_ALPHA * x1_ref[...] + acc, vec_ref[0:1, :], vec_ref[1:2, :])
    x2_ref[...] = x2


def _combine(yg, w_tok, x1, shared_a, shared_b, vec, tt):
    t = x1.shape[0]
    half_steps = shared_a.shape[0] // tt
    full = lambda i: (0, 0)
    return pl.pallas_call(
        functools.partial(_combine_kernel, half_steps=half_steps),
        grid=(t // tt,),
        in_specs=[pl.BlockSpec((TOP_K, tt, yg.shape[2]), lambda i: (0, i, 0)),
                  pl.BlockSpec((tt, TOP_K), lambda i: (i, 0)),
                  pl.BlockSpec((tt, D_MODEL), lambda i: (i, 0)),
                  pl.BlockSpec((tt, D_MODEL), lambda i: (jnp.minimum(i, half_steps - 1), 0)),
                  pl.BlockSpec((tt, D_MODEL), lambda i: (jnp.maximum(i - half_steps, 0), 0)),
                  pl.BlockSpec((8, D_MODEL), full)],
        out_specs=pl.BlockSpec((tt, D_MODEL), lambda i: (i, 0)),
        out_shape=jax.ShapeDtypeStruct((t, D_MODEL), F32),
        compiler_params=_params("parallel"),
        name="combine",
    )(yg, w_tok, x1, shared_a, shared_b, vec)


def _pad_rows(a, rows):
    return jnp.zeros((rows, a.shape[-1]), F32).at[:a.shape[0]].set(a.astype(F32))


def _layer(x, p, stacked, layer, batch, seq):
    t = batch * seq
    w_main, wa_hi, wa_lo, wb_hi, wb_lo = _regroup(jnp.swapaxes(stacked["w_in"], 1, 2), layer, REGROUP_ROWS)
    proj = _proj(x, w_main, min(PROJ_ROWS, t), PROJ_TILE_COLS)

    prm = jnp.zeros((8, LANES), F32)
    prm = prm.at[0, :GDN_HEADS].set(p["a_log"]).at[1, :GDN_HEADS].set(p["dt_bias"])
    ong = p["o_norm_g"].reshape(1, HEAD_DIM).astype(F32)
    u, w, qd, m2, eg = _gdn_intra(proj, x, wa_hi, wa_lo, wb_hi, wb_lo, p["conv_qkv"].astype(F32), prm,
                                  seq, min(GDN_TILE, seq))
    o_gdn = _gdn_scan(u, w, qd, m2, eg, proj, ong, batch, seq)

    ts = min(MIXER_ROWS, seq)
    dww = jnp.broadcast_to(p["dw_w"].astype(F32)[:, None, :], (CONF_KERNEL, 8, CONF_CH))
    vec = _pad_rows(jnp.stack([p["dw_b"], p["cln_g"], p["cln_b"], p["ln1_g"], p["ln1_b"]]), 8)
    x1, x1b, x1p = _mixer(proj, o_gdn, x, p["w_oa"].astype(BF16), p["w_ob"].astype(BF16),
                          p["w_o"].astype(BF16), dww, vec, batch, seq, ts)

    tt_r = min(ROUTE_TOKENS, t)
    wr_t = p["w_router"].T.astype(F32)
    wr_hi = wr_t.astype(BF16)
    wr_lo = (wr_t - wr_hi.astype(F32)).astype(BF16)
    bias = jnp.broadcast_to(p["router_bias"].astype(F32)[:, None], (N_EXPERTS, tt_r))
    eidx_t, wts_t, rank_t, cnt = _route(x1, wr_hi, wr_lo, bias, tt_r)

    n_blocks = -(-(t * TOP_K + N_EXPERTS * (ROW_BLOCK - 1)) // ROW_BLOCK)
    n_rows = n_blocks * ROW_BLOCK
    pstart, block_e, n_valid = _plan(cnt, n_blocks)

    dest = _dest(eidx_t, rank_t, pstart, min(DEST_TOKENS, t))
    sh_w = (p["w_sh_gate"].astype(BF16), p["w_sh_up"].astype(BF16), p["w_sh_down"].astype(BF16))
    tt_s = min(SHARED_ROWS, t // 2)
    xs = _dispatch(dest, x1p, n_rows)
    shared_a = _shared(x1b, *sh_w, dest, tt_s, 0, t // 2)
    ys = _experts(block_e, n_valid, xs, stacked["w_gate_e"], stacked["w_up_e"], stacked["w_down_e"], layer)
    yg = _gather_rows(ys, dest.reshape(TOP_K * t)).reshape(TOP_K, t, ys.shape[1])
    shared_b = _shared(x1b, *sh_w, ys, tt_s, t // 2, t // 2)
    vec2 = _pad_rows(jnp.stack([p["ln2_g"], p["ln2_b"]]), 8)
    return _combine(yg, wts_t.T, x1, shared_a, shared_b, vec2, min(COMBINE_ROWS, t // 2))


_PARAM_NAMES = ("w_in", "conv_qkv", "a_log", "dt_bias", "o_norm_g", "w_oa", "dw_w", "dw_b", "cln_g",
                "cln_b", "w_ob", "w_o", "ln1_g", "ln1_b", "w_router", "router_bias", "w_gate_e",
                "w_up_e", "w_down_e", "w_sh_gate", "w_sh_up", "w_sh_down", "ln2_g", "ln2_b")


_EXPERT_WEIGHTS = ("w_in", "w_gate_e", "w_up_e", "w_down_e")


def kernel(x, w_in, conv_qkv, a_log, dt_bias, o_norm_g, w_oa, dw_w, dw_b, cln_g, cln_b, w_ob, w_o,
           ln1_g, ln1_b, w_router, router_bias, w_gate_e, w_up_e, w_down_e, w_sh_gate, w_sh_up,
           w_sh_down, ln2_g, ln2_b):
    stacked = dict(zip(_PARAM_NAMES, (w_in, conv_qkv, a_log, dt_bias, o_norm_g, w_oa, dw_w, dw_b, cln_g,
                                      cln_b, w_ob, w_o, ln1_g, ln1_b, w_router, router_bias, w_gate_e,
                                      w_up_e, w_down_e, w_sh_gate, w_sh_up, w_sh_down, ln2_g, ln2_b)))
    batch, seq, d = x.shape
    assert d == D_MODEL and seq % CHUNK == 0
    xf = x.reshape(batch * seq, d).astype(F32)
    for layer in range(w_in.shape[0]):
        p = {name: arr[layer] for name, arr in stacked.items() if name not in _EXPERT_WEIGHTS}
        xf = _layer(xf, p, stacked, layer, batch, seq)
    return xf.reshape(batch, seq, d).astype(x.dtype)
```

```python
import functools

import jax
import jax.numpy as jnp
import numpy as np
from jax import lax
from jax.experimental import pallas as pl
from jax.experimental.pallas import tpu as pltpu
from jax.experimental.pallas import tpu_sc as plsc

F32 = jnp.float32
BF16 = jnp.bfloat16
I32 = jnp.int32
U32 = jnp.uint32
HI_MASK = np.uint32(0xFFFF0000)

D_MODEL = 1024
GDN_HEADS = 8
HEAD_DIM = 128
GDN_WIDTH = GDN_HEADS * HEAD_DIM
SHORT_CONV = 4
CONV_HALO = 16
CHUNK = 64
SOLVE_BLOCK = 16
GDN_TILE = 256
SCAN_CHUNKS = 8
SCAN_SEQS = 2
CONF_CH = D_MODEL
CONF_KERNEL = 31
CONF_HALO = 32
N_EXPERTS = 64
TOP_K = 8
N_GROUPS = 8
GROUP_SIZE = N_EXPERTS // N_GROUPS
TOPK_GROUPS = 4
EXPERT_FF = 256
SHARED_FF = 256
ROUTED_SCALE = 2.5
DEPTH = 2
DEEPNORM_ALPHA = (2 * DEPTH) ** 0.25
EPS = 1e-6

LANES = 128
PROJ_COLS = 8 * D_MODEL
ROW_BLOCK = 2048
EXPERT_SUB = 256
EXPERT_IN_SLOTS = 3
SC_SCATTER_ROWS = 128
SC_GATHER_ROWS = 64
VMEM_LIMIT = 56 * 1024 * 1024
REGROUP_ROWS = 512
PROJ_ROWS = 2048
PROJ_TILE_COLS = 2048
MIXER_ROWS = 256
MIXER_CONV_ROWS = 64
ROUTE_TOKENS = 1024
DEST_TOKENS = 2048
SHARED_ROWS = 512
COMBINE_ROWS = 512


def _params(*sem):
    return pltpu.CompilerParams(dimension_semantics=sem, vmem_limit_bytes=VMEM_LIMIT)


def _dot(a, b):
    return jnp.dot(a, b, preferred_element_type=F32)


def _dot_nt(a, b):
    return lax.dot_general(a, b, (((1,), (1,)), ((), ())), preferred_element_type=F32)


def _split(a):
    hi = a.astype(BF16)
    lo = (a - hi.astype(F32)).astype(BF16)
    return hi, lo


def _sigmoid(x):
    return 1.0 / (1.0 + jnp.exp(-x))


def _silu(x):
    return x * _sigmoid(x)


def _layer_norm(y, g, b):
    mu = jnp.mean(y, axis=-1, keepdims=True)
    yc = y - mu
    var = jnp.mean(yc * yc, axis=-1, keepdims=True)
    return yc * lax.rsqrt(var + EPS) * g + b


def _proj_kernel(a_ref, w_ref, o_ref):
    o_ref[...] = _dot_nt(a_ref[...].astype(BF16), w_ref[...]).astype(o_ref.dtype)


def _regroup_kernel(wt_hbm, main_ref, ahi_ref, alo_ref, bhi_ref, blo_ref, buf, ab_buf, sem, ab_sem, *, layer, tr):
    i = pl.program_id(0)
    half = PROJ_COLS // 2
    skip = 2 * GDN_HEADS

    def rows_copy(j, slot):
        start = pl.multiple_of(jnp.where(j < half // tr, 0, skip) + j * tr, skip)
        return pltpu.make_async_copy(wt_hbm.at[layer, pl.ds(start, tr), :], buf.at[slot], sem.at[slot])

    @pl.when(i == 0)
    def _():
        rows_copy(0, 0).start()

    @pl.when(i + 1 < pl.num_programs(0))
    def _():
        rows_copy(i + 1, lax.rem(i + 1, 2)).start()

    slot = lax.rem(i, 2)
    rows_copy(i, slot).wait()
    main_ref[...] = buf[slot].astype(BF16)

    @pl.when(i == 0)
    def _():
        cp_ab = pltpu.make_async_copy(wt_hbm.at[layer, pl.ds(half, LANES), :], ab_buf, ab_sem)
        cp_ab.start()
        cp_ab.wait()
        cols = ab_buf[...].T
        wide = jnp.concatenate([cols, jnp.zeros_like(cols)], axis=1)
        lane = lax.broadcasted_iota(I32, cols.shape, 1)
        for off, hi_ref, lo_ref in ((0, ahi_ref, alo_ref), (GDN_HEADS, bhi_ref, blo_ref)):
            hi, lo = _split(jnp.where(lane < GDN_HEADS, wide[:, off:off + LANES], 0.0))
            hi_ref[...] = hi
            lo_ref[...] = lo


def _regroup(w_in_t, layer, tr):
    d = w_in_t.shape[2]
    small = jax.ShapeDtypeStruct((d, LANES), BF16)
    whole = lambda i: (0, 0)
    return pl.pallas_call(
        functools.partial(_regroup_kernel, layer=layer, tr=tr),
        grid=(PROJ_COLS // tr,),
        in_specs=[pl.BlockSpec(memory_space=pl.ANY)],
        out_specs=[pl.BlockSpec((tr, d), lambda i: (i, 0))] + [pl.BlockSpec((d, LANES), whole)] * 4,
        out_shape=[jax.ShapeDtypeStruct((PROJ_COLS, d), BF16), small, small, small, small],
        scratch_shapes=[pltpu.VMEM((2, tr, d), F32), pltpu.VMEM((LANES, d), F32),
                        pltpu.SemaphoreType.DMA((2,)), pltpu.SemaphoreType.DMA],
        compiler_params=_params("arbitrary"),
        name="regroup",
    )(w_in_t)


def _proj(xb, w, tm, tn):
    m, k = xb.shape
    n = w.shape[0]
    return pl.pallas_call(
        _proj_kernel,
        grid=(m // tm, n // tn),
        in_specs=[pl.BlockSpec((tm, k), lambda i, j: (i, 0)),
                  pl.BlockSpec((tn, k), lambda i, j: (j, 0))],
        out_specs=pl.BlockSpec((tm, tn), lambda i, j: (i, j)),
        out_shape=jax.ShapeDtypeStruct((m, n), BF16),
        compiler_params=_params("parallel", "parallel"),
        name="proj",
    )(xb, w)


def _unit_lower_inverse4(als):
    c = CHUNK
    n = als[0].shape[1]
    row = lax.broadcasted_iota(I32, (c, n), 0)
    col = jnp.bitwise_and(lax.broadcasted_iota(I32, (c, n), 1), c - 1)
    shift = SOLVE_BLOCK.bit_length() - 1
    same = jnp.right_shift(row, shift) == jnp.right_shift(col, shift)
    eye = (row == col).astype(F32)
    cshift = c.bit_length() - 1
    brow = jnp.right_shift(lax.broadcasted_iota(I32, (n, n), 0), cshift)
    bcol = jnp.right_shift(lax.broadcasted_iota(I32, (n, n), 1), cshift)
    on_diag = brow == bcol

    def mm(x, y):
        yb = y.astype(BF16)
        bd = jnp.where(on_diag, jnp.concatenate([yb] * (n // c), axis=0), jnp.zeros((), BF16))
        return _dot(x.astype(BF16), bd)

    a_diag = [jnp.where(same, al, 0.0) for al in als]
    a_off = [al - ad for al, ad in zip(als, a_diag)]
    bp = [-ad for ad in a_diag]
    p = [eye + b for b in bp]
    for _ in range(3):
        bp = [mm(b, b) for b in bp]
        p = [x + mm(x, b) for x, b in zip(p, bp)]
    n1 = [mm(x, ao) for x, ao in zip(p, a_off)]
    n2 = [mm(x, x) for x in n1]
    q = [x + mm(y, x) for x, y in zip(p, n2)]
    return [x - mm(y, x) for x, y in zip(q, n1)]


def _shift_selectors(rt):
    sel = np.zeros((SHORT_CONV * rt, rt), np.float32)
    sel_halo = np.zeros((SHORT_CONV * 8, CONV_HALO), np.float32)
    for d in range(SHORT_CONV):
        for t in range(d, rt):
            sel[d * rt + t, t - d] = 1.0
        for t in range(d):
            sel_halo[d * 8 + t, CONV_HALO + t - d] = 1.0
    return jnp.asarray(sel, BF16), jnp.asarray(sel_halo, BF16)


def _gdn_intra_kernel(qkv_ref, prev_ref, x_ref, sel_ref, selh_ref, wah_ref, wal_ref, wbh_ref, wbl_ref, cw_ref,
                      prm_ref, u_ref, w_ref, qd_ref, m2_ref, eg_ref, xs_ref, *, rt, tiles_per_seq):
    c = CHUNK
    nc = rt // c
    first = (pl.program_id(0) % tiles_per_seq) == 0
    edge = jnp.where(first, 0.0, _dot(selh_ref[...], prev_ref[...]))

    def move_rows(lo, hi):
        moved = _dot(sel_ref[...], qkv_ref[:, lo:hi])
        for d in range(SHORT_CONV):
            xs_ref[d, 0:8, lo:hi] = moved[d * rt:d * rt + 8] + edge[d * 8:(d + 1) * 8, lo:hi]
            xs_ref[d, 8:rt, lo:hi] = moved[d * rt + 8:(d + 1) * rt]

    for part in range(3):
        move_rows(part * GDN_WIDTH, part * GDN_WIDTH + 2 * HEAD_DIM)

    xh, xl = _split(x_ref[...])

    def proj3(wh_ref, wl_ref):
        wh = wh_ref[...]
        return _dot(xh, wh) + _dot(xl, wh) + _dot(xh, wl_ref[...])

    a_raw = proj3(wah_ref, wal_ref)
    b_raw = proj3(wbh_ref, wbl_ref)
    sp_in = a_raw + prm_ref[1:2, :]
    softplus = jnp.maximum(sp_in, 0.0) + jnp.log(1.0 + jnp.exp(-jnp.abs(sp_in)))
    g = -jnp.exp(prm_ref[0:1, :]) * softplus
    beta = _sigmoid(b_raw)

    cshift = c.bit_length() - 1
    r2 = lax.broadcasted_iota(I32, (rt, rt), 0)
    c2 = lax.broadcasted_iota(I32, (rt, rt), 1)
    same_chunk = jnp.right_shift(r2, cshift) == jnp.right_shift(c2, cshift)
    ltri = jnp.where(r2 >= c2, jnp.where(same_chunk, 1.0, 0.0), 0.0).astype(BF16)
    g_hi = g.astype(BF16)
    g_r = g - g_hi.astype(F32)
    g_mid = g_r.astype(BF16)
    g_lo = (g_r - g_mid.astype(F32)).astype(BF16)
    gc = _dot(ltri, g_hi) + _dot(ltri, g_mid) + _dot(ltri, g_lo)
    gct = gc.T
    egc = jnp.exp(gc)
    gend = jnp.concatenate(
        [jnp.broadcast_to(gc[ci * c + c - 1:ci * c + c, :], (c, LANES)) for ci in range(nc)], axis=0)
    kfac = jnp.exp(gend - gc)
    bege = beta * egc
    for ci in range(nc):
        last = ci * c + c - 1
        eg_ref[ci * GDN_HEADS:(ci + 1) * GDN_HEADS, :] = jnp.broadcast_to(
            jnp.exp(gct[0:GDN_HEADS, last:last + 1]), (GDN_HEADS, LANES))

    lane_t = lax.broadcasted_iota(I32, (rt, LANES), 1) < c
    lane_lo = lax.broadcasted_iota(I32, (c, LANES), 1) < c
    lcol = jnp.bitwise_and(lax.broadcasted_iota(I32, (c, LANES), 1), c - 1)
    rowi = lax.broadcasted_iota(I32, (c, LANES), 0)
    causal = rowi >= lcol
    strict = rowi > lcol

    def conv(base, h):
        lo, hi = base + h * HEAD_DIM, base + (h + 1) * HEAD_DIM
        acc = cw_ref[SHORT_CONV - 1:SHORT_CONV, lo:hi] * xs_ref[0, :, lo:hi]
        for j in range(SHORT_CONV - 1):
            acc = acc + cw_ref[j:j + 1, lo:hi] * xs_ref[SHORT_CONV - 1 - j, :, lo:hi]
        return _silu(acc)

    a_pairs = [[None] * (GDN_HEADS // 2) for _ in range(nc)]
    rhs_pairs = [[None] * (GDN_HEADS // 2) for _ in range(nc)]
    for p in range(GDN_HEADS // 2):
        if p + 1 < GDN_HEADS // 2:
            for part in range(3):
                lo = part * GDN_WIDTH + (p + 1) * 2 * HEAD_DIM
                move_rows(lo, lo + 2 * HEAD_DIM)
        ks, kbs, qs, kds, rhss = [], [], [], [], []
        for h in (2 * p, 2 * p + 1):
            q = conv(0, h)
            k = conv(GDN_WIDTH, h)
            v = conv(2 * GDN_WIDTH, h)
            q = q * lax.rsqrt(jnp.sum(q * q, axis=-1, keepdims=True) + EPS) * (HEAD_DIM ** -0.5)
            k = k * lax.rsqrt(jnp.sum(k * k, axis=-1, keepdims=True) + EPS)
            beta_h = beta[:, h:h + 1]
            qd_ref[:, h * HEAD_DIM:(h + 1) * HEAD_DIM] = (q * egc[:, h:h + 1]).astype(BF16)
            ks.append(k)
            kbs.append(k * beta_h)
            qs.append(q)
            kds.append(k * kfac[:, h:h + 1])
            rhss.append(jnp.concatenate([v * beta_h, k * bege[:, h:h + 1]], axis=1))
        h0, h1 = 2 * p, 2 * p + 1
        gch = jnp.where(lane_t, gc[:, h0:h0 + 1], gc[:, h1:h1 + 1])
        for ci in range(nc):
            rows = slice(ci * c, (ci + 1) * c)
            wk = jnp.concatenate([ks[0][rows], ks[1][rows]], axis=0).astype(BF16)
            lhs = jnp.concatenate([kbs[0][rows], qs[0][rows], kbs[1][rows], qs[1][rows]],
                                  axis=0).astype(BF16)
            out = _dot_nt(lhs, wk)
            gcrow = jnp.concatenate([gct[h0:h0 + 1, rows], gct[h1:h1 + 1, rows]], axis=1)
            diff = gch[rows] - gcrow
            decay = jnp.where(causal, jnp.exp(jnp.where(causal, diff, 0.0)), 0.0)
            a_pairs[ci][p] = jnp.where(strict, jnp.where(lane_lo, out[0:c], out[2 * c:3 * c]) * decay, 0.0)
            qk = jnp.where(lane_lo, out[c:2 * c], out[3 * c:4 * c]) * decay
            kdt = jnp.concatenate([kds[0][rows], kds[1][rows]], axis=0).T
            m0 = ci * 3 * c
            m2_ref[m0:m0 + c, p * LANES:(p + 1) * LANES] = qk.astype(BF16)
            m2_ref[m0 + c:m0 + 3 * c, p * LANES:(p + 1) * LANES] = kdt.astype(BF16)
            rhs_pairs[ci][p] = (rhss[0][rows], rhss[1][rows])

    zeros = jnp.zeros((c, 2 * HEAD_DIM), BF16)
    ngrp = GDN_HEADS // 4
    tls = _unit_lower_inverse4(
        [jnp.concatenate([a_pairs[ci][2 * grp], a_pairs[ci][2 * grp + 1]], axis=1)
         for ci in range(nc) for grp in range(ngrp)])
    for ci in range(nc):
        rows = slice(ci * c, (ci + 1) * c)
        for grp in range(ngrp):
            tl = tls[ci * ngrp + grp]
            for j in range(2):
                p = 2 * grp + j
                r0, r1 = rhs_pairs[ci][p]
                bd = jnp.concatenate([jnp.concatenate([r0.astype(BF16), zeros], axis=1),
                                      jnp.concatenate([zeros, r1.astype(BF16)], axis=1)], axis=0)
                sol = _dot(tl[:, j * LANES:(j + 1) * LANES].astype(BF16), bd)
                for i in range(2):
                    h = 2 * p + i
                    lo, hi = h * HEAD_DIM, (h + 1) * HEAD_DIM
                    u_ref[rows, lo:hi] = sol[:, 2 * i * HEAD_DIM:(2 * i + 1) * HEAD_DIM]
                    w_ref[rows, lo:hi] = sol[:, (2 * i + 1) * HEAD_DIM:(2 * i + 2) * HEAD_DIM].astype(BF16)


def _gdn_intra(proj, x, wa_hi, wa_lo, wb_hi, wb_lo, conv_w, prm, seq, rt):
    t = x.shape[0]
    nc = rt // CHUNK
    kern = functools.partial(_gdn_intra_kernel, rt=rt, tiles_per_seq=seq // rt)
    full = lambda i: (0, 0)
    tile = lambda i: (i, 0)
    sel, sel_halo = _shift_selectors(rt)
    return pl.pallas_call(
        kern,
        grid=(t // rt,),
        in_specs=[
            pl.BlockSpec((rt, 3 * GDN_WIDTH), tile),
            pl.BlockSpec((CONV_HALO, 3 * GDN_WIDTH),
                         lambda i: (jnp.maximum(i * (rt // CONV_HALO) - 1, 0), 0)),
            pl.BlockSpec((rt, D_MODEL), tile),
            pl.BlockSpec(sel.shape, full),
            pl.BlockSpec(sel_halo.shape, full),
            pl.BlockSpec((D_MODEL, LANES), full),
            pl.BlockSpec((D_MODEL, LANES), full),
            pl.BlockSpec((D_MODEL, LANES), full),
            pl.BlockSpec((D_MODEL, LANES), full),
            pl.BlockSpec((SHORT_CONV, 3 * GDN_WIDTH), full),
            pl.BlockSpec((8, LANES), full),
        ],
        out_specs=[pl.BlockSpec((rt, GDN_WIDTH), tile),
                   pl.BlockSpec((rt, GDN_WIDTH), tile),
                   pl.BlockSpec((rt, GDN_WIDTH), tile),
                   pl.BlockSpec((nc * 3 * CHUNK, GDN_HEADS // 2 * LANES), tile),
                   pl.BlockSpec((nc * GDN_HEADS, LANES), tile)],
        out_shape=[jax.ShapeDtypeStruct((t, GDN_WIDTH), F32),
                   jax.ShapeDtypeStruct((t, GDN_WIDTH), BF16),
                   jax.ShapeDtypeStruct((t, GDN_WIDTH), BF16),
                   jax.ShapeDtypeStruct((t // CHUNK * 3 * CHUNK, GDN_HEADS // 2 * LANES), BF16),
                   jax.ShapeDtypeStruct((t // CHUNK * GDN_HEADS, LANES), F32)],
        scratch_shapes=[pltpu.VMEM((SHORT_CONV, rt, 3 * GDN_WIDTH), F32)],
        compiler_params=_params("parallel"),
        name="gdn_intra",
    )(proj, proj, x, sel, sel_halo, wa_hi, wa_lo, wb_hi, wb_lo, conv_w, prm)


def _gdn_scan_kernel(u_ref, w_ref, qd_ref, m2_ref, eg_ref, z_ref, ong_ref, o_ref, s_ref, *, nck):
    c = CHUNK
    nseq = u_ref.shape[0]

    @pl.when(pl.program_id(1) == 0)
    def _():
        s_ref[...] = jnp.zeros_like(s_ref)

    ong = ong_ref[...]
    zeros = jnp.zeros((c, HEAD_DIM), BF16)
    span = lambda h: slice(h * HEAD_DIM, (h + 1) * HEAD_DIM)
    chains = [(b, h) for b in range(nseq) for h in range(GDN_HEADS)]
    states = {bh: s_ref[bh[0], bh[1]] for bh in chains}
    for ci in range(nck):
        rows = slice(ci * c, (ci + 1) * c)
        rs = {(b, h): _dot(jnp.concatenate([w_ref[b, rows, span(h)], qd_ref[b, rows, span(h)]], axis=0),
                           states[b, h].astype(BF16)) for b, h in chains}
        v_new = {(b, h): (u_ref[b, rows, span(h)] - rs[b, h][:c]).astype(BF16) for b, h in chains}
        r2s = {}
        for b in range(nseq):
            for p in range(GDN_HEADS // 2):
                bd = jnp.concatenate([jnp.concatenate([v_new[b, 2 * p], zeros], axis=1),
                                      jnp.concatenate([zeros, v_new[b, 2 * p + 1]], axis=1)], axis=0)
                r2s[b, p] = _dot(m2_ref[b, ci * 3 * c:(ci + 1) * 3 * c, p * LANES:(p + 1) * LANES], bd)
        for b, h in chains:
            half = span(h % 2)
            r2 = r2s[b, h // 2]
            decay = eg_ref[b, ci * GDN_HEADS + h:ci * GDN_HEADS + h + 1, :]
            states[b, h] = states[b, h] * decay + r2[c:, half]
            o = rs[b, h][c:] + r2[:c, half]
            o = o * lax.rsqrt(jnp.mean(o * o, axis=-1, keepdims=True) + EPS) * ong
            o = o * _silu(z_ref[b, rows, span(h)].astype(F32))
            o_ref[b, rows, span(h)] = o.astype(o_ref.dtype)
    for b, h in chains:
        s_ref[b, h] = states[b, h]


def _gdn_scan(u, w, qd, m2, eg, proj, ong, batch, seq):
    t = batch * seq
    nck = min(SCAN_CHUNKS, seq // CHUNK)
    c = nck * CHUNK
    nch = seq // c
    ns = SCAN_SEQS if batch % SCAN_SEQS == 0 else 1
    groups = batch // ns
    split = lambda a: a.reshape(ns, a.shape[0] // ns, a.shape[1])
    blk = lambda b, n: (0, b * nch + n, 0)
    out = pl.pallas_call(
        functools.partial(_gdn_scan_kernel, nck=nck),
        grid=(groups, nch),
        in_specs=[
            pl.BlockSpec((ns, c, GDN_WIDTH), blk),
            pl.BlockSpec((ns, c, GDN_WIDTH), blk),
            pl.BlockSpec((ns, c, GDN_WIDTH), blk),
            pl.BlockSpec((ns, 3 * c, GDN_HEADS // 2 * LANES), blk),
            pl.BlockSpec((ns, nck * GDN_HEADS, LANES), blk),
            pl.BlockSpec((ns, c, GDN_WIDTH), lambda b, n: (0, b * nch + n, 3)),
            pl.BlockSpec((1, HEAD_DIM), lambda b, n: (0, 0)),
        ],
        out_specs=pl.BlockSpec((ns, c, GDN_WIDTH), blk),
        out_shape=jax.ShapeDtypeStruct((ns, t // ns, GDN_WIDTH), BF16),
        scratch_shapes=[pltpu.VMEM((ns, GDN_HEADS, HEAD_DIM, HEAD_DIM), F32)],
        compiler_params=_params("parallel", "arbitrary"),
        name="gdn_scan",
    )(split(u), split(w), split(qd), split(m2), split(eg), split(proj), ong)
    return out.reshape(t, GDN_WIDTH)


def _pack_bf16_pairs(y):
    n = y.shape[1] // 2
    yb = y.astype(BF16).astype(F32)
    lo = lax.bitcast_convert_type(yb[:, :n], U32)
    hi = lax.bitcast_convert_type(yb[:, n:], U32)
    return jnp.bitwise_or(jnp.right_shift(lo, 16), jnp.bitwise_and(hi, HI_MASK))


def _unpack_bf16_pairs(w):
    lo = lax.bitcast_convert_type(jnp.left_shift(w, 16), F32)
    hi = lax.bitcast_convert_type(jnp.bitwise_and(w, HI_MASK), F32)
    return lo, hi


def _mixer_kernel(glu_ref, ga_ref, gb_ref, o_ref, x_ref, woa_ref, wob_ref, wo_ref, dww_ref, vec_ref,
                  x1_ref, x1b_ref, x1p_ref, ubuf_ref, sh_ref, conv_ref, *, ts, rc):
    halo = CONF_HALO

    @pl.when(pl.program_id(1) == 0)
    def _():
        ubuf_ref[0:halo, :] = jnp.zeros((halo, CONF_CH), F32)

    @pl.when(pl.program_id(1) != 0)
    def _():
        ubuf_ref[0:halo, :] = ubuf_ref[ts:ts + halo, :]

    glu_a = glu_ref[:, :CONF_CH].astype(F32)
    glu_b = glu_ref[:, CONF_CH:].astype(F32)
    ubuf_ref[halo:halo + ts, :] = glu_a * _sigmoid(glu_b)

    dw_b = vec_ref[0:1, :]
    cln_g = vec_ref[1:2, :]
    cln_b = vec_ref[2:3, :]
    ln1_g = vec_ref[3:4, :]
    ln1_b = vec_ref[4:5, :]

    span = ts + halo - 8
    for s in range(1, 8):
        sh_ref[s - 1] = ubuf_ref[s:s + span, :]

    def tap_rows(j, r0):
        o = halo - (CONF_KERNEL - 1) + j
        q, s = o // 8, o % 8
        if s == 0:
            return ubuf_ref[r0 + o:r0 + o + rc, :]
        return sh_ref[s - 1, r0 + 8 * q:r0 + 8 * q + rc, :]

    hs = ts // 2
    for h0 in range(0, ts, hs):
        rows = slice(h0, h0 + hs)
        gated_a = _sigmoid(ga_ref[rows, :].astype(F32)) * _dot(o_ref[rows, :], woa_ref[...])
        for r0 in range(h0, h0 + hs, rc):
            tap_w = lambda j: jnp.concatenate([dww_ref[j]] * (rc // 8), axis=0)
            acc = tap_w(0) * tap_rows(0, r0)
            for j in range(1, CONF_KERNEL):
                acc = acc + tap_w(j) * tap_rows(j, r0)
            conv_ref[r0:r0 + rc, :] = acc
        uc = _silu(_layer_norm(conv_ref[rows, :] + dw_b, cln_g, cln_b))
        branch_b = _dot(uc.astype(BF16), wob_ref[...])
        hmix = gated_a + _sigmoid(gb_ref[rows, :].astype(F32)) * branch_b
        mix = _dot(hmix.astype(BF16), wo_ref[...])
        x1 = _layer_norm(DEEPNORM_ALPHA * x_ref[rows, :] + mix, ln1_g, ln1_b)
        x1_ref[rows, :] = x1
        x1b_ref[rows, :] = x1.astype(BF16)
        x1p_ref[rows, :] = _pack_bf16_pairs(x1)


def _mixer(proj, o_gdn, x, woa, wob, wo, dww, vec, batch, seq, ts):
    t = batch * seq
    nt = seq // ts
    rows = lambda b, n: b * nt + n
    full = lambda b, n: (0, 0)
    kern = functools.partial(_mixer_kernel, ts=ts, rc=MIXER_CONV_ROWS)
    return pl.pallas_call(
        kern,
        grid=(batch, nt),
        in_specs=[
            pl.BlockSpec((ts, 2 * CONF_CH), lambda b, n: (rows(b, n), 2)),
            pl.BlockSpec((ts, D_MODEL), lambda b, n: (rows(b, n), 6)),
            pl.BlockSpec((ts, D_MODEL), lambda b, n: (rows(b, n), 7)),
            pl.BlockSpec((ts, GDN_WIDTH), lambda b, n: (rows(b, n), 0)),
            pl.BlockSpec((ts, D_MODEL), lambda b, n: (rows(b, n), 0)),
            pl.BlockSpec((GDN_WIDTH, D_MODEL), full),
            pl.BlockSpec((CONF_CH, D_MODEL), full),
            pl.BlockSpec((D_MODEL, D_MODEL), full),
            pl.BlockSpec((CONF_KERNEL, 8, CONF_CH), lambda b, n: (0, 0, 0)),
            pl.BlockSpec((8, D_MODEL), full),
        ],
        out_specs=[pl.BlockSpec((ts, D_MODEL), lambda b, n: (rows(b, n), 0)),
                   pl.BlockSpec((ts, D_MODEL), lambda b, n: (rows(b, n), 0)),
                   pl.BlockSpec((ts, D_MODEL // 2), lambda b, n: (rows(b, n), 0))],
        out_shape=[jax.ShapeDtypeStruct((t, D_MODEL), F32),
                   jax.ShapeDtypeStruct((t, D_MODEL), BF16),
                   jax.ShapeDtypeStruct((t, D_MODEL // 2), U32)],
        scratch_shapes=[pltpu.VMEM((CONF_HALO + ts, CONF_CH), F32),
                        pltpu.VMEM((7, ts + CONF_HALO - 8, CONF_CH), F32),
                        pltpu.VMEM((ts, CONF_CH), F32)],
        compiler_params=_params("parallel", "arbitrary"),
        name="mixer",
    )(proj, proj, proj, o_gdn, x, woa, wob, wo, dww, vec)


def _route_kernel(x_ref, wrh_ref, wrl_ref, bias_ref, eidx_ref, wts_ref, rank_ref, cnt_ref, carry_ref, *, tt):
    @pl.when(pl.program_id(0) == 0)
    def _():
        carry_ref[...] = jnp.zeros_like(carry_ref)

    xh, xl = _split(x_ref[...])
    wrh = wrh_ref[...]
    logits = _dot_nt(wrh, xh) + _dot_nt(wrh, xl) + _dot_nt(wrl_ref[...], xh)
    s = _sigmoid(logits)
    biased = s + bias_ref[...]

    sub = lax.broadcasted_iota(I32, (GROUP_SIZE, tt), 0)
    groups = [biased[g * GROUP_SIZE:(g + 1) * GROUP_SIZE, :] for g in range(N_GROUPS)]
    gs = []
    for bg in groups:
        m1 = jnp.max(bg, axis=0, keepdims=True)
        first = jnp.min(jnp.where(bg == m1, sub, GROUP_SIZE), axis=0, keepdims=True)
        m2 = jnp.max(jnp.where(sub == first, -jnp.inf, bg), axis=0, keepdims=True)
        gs.append(m1 + m2)

    masked_parts = []
    for g in range(N_GROUPS):
        beaten = jnp.zeros((1, tt), I32)
        for o in range(N_GROUPS):
            if o == g:
                continue
            wins = (gs[o] >= gs[g]) if o < g else (gs[o] > gs[g])
            beaten = beaten + wins.astype(I32)
        keep = jnp.broadcast_to(beaten < TOPK_GROUPS, (GROUP_SIZE, tt))
        masked_parts.append(jnp.where(keep, groups[g], -jnp.inf))
    masked = jnp.concatenate(masked_parts, axis=0)

    eiota = lax.broadcasted_iota(I32, (N_EXPERTS, tt), 0)
    sel_all = jnp.zeros((N_EXPERTS, tt), F32)
    picks = []
    for _ in range(TOP_K):
        m = jnp.max(masked, axis=0, keepdims=True)
        idx = jnp.min(jnp.where(masked == m, eiota, N_EXPERTS), axis=0, keepdims=True)
        onehot = eiota == idx
        picks.append((idx, onehot))
        sel_all = jnp.where(onehot, 1.0, sel_all)
        masked = jnp.where(onehot, -jnp.inf, masked)

    tr = lax.broadcasted_iota(I32, (tt, tt), 0)
    tc = lax.broadcasted_iota(I32, (tt, tt), 1)
    before = (tr < tc).astype(BF16)
    sel_b = sel_all.astype(BF16)
    carry = carry_ref[...]
    rank_all = _dot(sel_b, before) + carry[:, 0:1]
    carry_new = carry + _dot(sel_b, jnp.ones((tt, LANES), BF16))
    carry_ref[...] = carry_new
    cnt_ref[...] = carry_new

    s_sel = [jnp.sum(jnp.where(oh, s, 0.0), axis=0, keepdims=True) for _, oh in picks]
    total = s_sel[0]
    for v in s_sel[1:]:
        total = total + v
    for k, (idx, oh) in enumerate(picks):
        eidx_ref[k:k + 1, :] = idx
        wts_ref[k:k + 1, :] = s_sel[k] / total * ROUTED_SCALE
        rank_ref[k:k + 1, :] = jnp.sum(jnp.where(oh, rank_all, 0.0), axis=0, keepdims=True).astype(I32)


def _route(x1, wr_hi, wr_lo, bias, tt):
    t = x1.shape[0]
    kern = functools.partial(_route_kernel, tt=tt)
    return pl.pallas_call(
        kern,
        grid=(t // tt,),
        in_specs=[pl.BlockSpec((tt, D_MODEL), lambda i: (i, 0)),
                  pl.BlockSpec((N_EXPERTS, D_MODEL), lambda i: (0, 0)),
                  pl.BlockSpec((N_EXPERTS, D_MODEL), lambda i: (0, 0)),
                  pl.BlockSpec((N_EXPERTS, tt), lambda i: (0, 0))],
        out_specs=[pl.BlockSpec((TOP_K, tt), lambda i: (0, i)),
                   pl.BlockSpec((TOP_K, tt), lambda i: (0, i)),
                   pl.BlockSpec((TOP_K, tt), lambda i: (0, i)),
                   pl.BlockSpec((N_EXPERTS, LANES), lambda i: (0, 0))],
        out_shape=[jax.ShapeDtypeStruct((TOP_K, t), I32),
                   jax.ShapeDtypeStruct((TOP_K, t), F32),
                   jax.ShapeDtypeStruct((TOP_K, t), I32),
                   jax.ShapeDtypeStruct((N_EXPERTS, LANES), F32)],
        scratch_shapes=[pltpu.VMEM((N_EXPERTS, LANES), F32)],
        compiler_params=_params("arbitrary"),
        name="route",
    )(x1, wr_hi, wr_lo, bias)


def _plan_kernel(cnt_ref, pstart_ref, plan_ref):
    e, nb = N_EXPERTS, plan_ref.shape[1]
    counts = cnt_ref[...]
    nblk = jnp.floor((counts + (ROW_BLOCK - 1)) * (1.0 / ROW_BLOCK))
    hi = jnp.floor(nblk * (1.0 / 256.0))
    lo = nblk - 256.0 * hi
    r = lax.broadcasted_iota(I32, (e, e), 0)
    c = lax.broadcasted_iota(I32, (e, e), 1)
    ltri = (r >= c).astype(BF16)
    bend = 256.0 * _dot(ltri, hi.astype(BF16)) + _dot(ltri, lo.astype(BF16))
    pend = bend * ROW_BLOCK
    pstart = pend - nblk * ROW_BLOCK
    pstart_ref[...] = pstart.astype(I32)

    bs = (lax.broadcasted_iota(I32, (e, nb), 1) * ROW_BLOCK).astype(F32)
    pend_b = jnp.broadcast_to(pend[:, 0:1], (e, nb))
    pstart_b = jnp.broadcast_to(pstart[:, 0:1], (e, nb))
    used_b = jnp.broadcast_to((pstart + counts)[:, 0:1], (e, nb))
    owner = jnp.sum(jnp.where(pend_b <= bs, 1.0, 0.0), axis=0, keepdims=True)
    inside = jnp.where(pstart_b <= bs, jnp.where(bs < pend_b, 1.0, 0.0), 0.0)
    real = jnp.sum(inside * jnp.clip(used_b - bs, 0.0, float(ROW_BLOCK)), axis=0, keepdims=True)
    plan_ref[0:1, :] = jnp.minimum(owner, float(e - 1)).astype(I32)
    plan_ref[1:2, :] = real.astype(I32)
    plan_ref[2:8, :] = jnp.zeros((6, nb), I32)


def _plan(cnt, n_blocks):
    nb = -(-n_blocks // LANES) * LANES
    pstart, plan = pl.pallas_call(
        _plan_kernel,
        out_shape=[jax.ShapeDtypeStruct((N_EXPERTS, LANES), I32),
                   jax.ShapeDtypeStruct((8, nb), I32)],
        name="plan",
    )(cnt)
    return pstart[:, 0], plan[0, :n_blocks], plan[1, :n_blocks]


def _dest_kernel(eidx_ref, rank_ref, pstart_ref, dest_ref):
    eidx = eidx_ref[...]
    acc = rank_ref[...]
    for e in range(N_EXPERTS):
        acc = acc + jnp.where(eidx == e, pstart_ref[e], 0)
    dest_ref[...] = acc


def _dest(eidx_t, rank_t, pstart, tt):
    t = eidx_t.shape[1]
    return pl.pallas_call(
        _dest_kernel,
        grid=(t // tt,),
        in_specs=[pl.BlockSpec((TOP_K, tt), lambda i: (0, i)),
                  pl.BlockSpec((TOP_K, tt), lambda i: (0, i)),
                  pl.BlockSpec(memory_space=pltpu.SMEM)],
        out_specs=pl.BlockSpec((TOP_K, tt), lambda i: (0, i)),
        out_shape=jax.ShapeDtypeStruct((TOP_K, t), I32),
        compiler_params=_params("parallel"),
        name="dest",
    )(eidx_t, rank_t, pstart)


def _dispatch(dest_kt, x1p, n_rows):
    t, width = x1p.shape
    info = plsc.get_sparse_core_info()
    nc, nw = info.num_cores, info.num_cores * info.num_subcores
    chunk = SC_SCATTER_ROWS
    per_w = t // nw
    n_chunks = per_w // chunk
    assert per_w % chunk == 0
    idx = dest_kt.reshape(TOP_K, nw, n_chunks, chunk).transpose(1, 2, 0, 3).reshape(nw, n_chunks * TOP_K, chunk)
    mesh = plsc.VectorSubcoreMesh(core_axis_name="c", subcore_axis_name="s")

    @functools.partial(
        pl.kernel, mesh=mesh, name="dispatch",
        out_type=jax.ShapeDtypeStruct((n_rows, width), x1p.dtype),
        scratch_types=[pltpu.VMEM((n_chunks * TOP_K, chunk), I32),
                       pltpu.VMEM((chunk, width), x1p.dtype),
                       pltpu.SemaphoreType.DMA])
    def scatter(x_hbm, idx_hbm, xs_hbm, idx_v, rows_v, sem):
        wid = lax.axis_index("s") * nc + lax.axis_index("c")
        base = wid * per_w
        pltpu.sync_copy(idx_hbm.at[wid], idx_v)
        for j in range(n_chunks):
            pltpu.sync_copy(x_hbm.at[pl.ds(base + j * chunk, chunk)], rows_v)
            copies = [pltpu.make_async_copy(rows_v, xs_hbm.at[idx_v.at[j * TOP_K + k]], sem)
                      for k in range(TOP_K)]
            for cp in copies:
                cp.start()
            for cp in copies:
                cp.wait()

    return scatter(x1p, idx)


def _gather_rows(table, idx):
    n = idx.shape[0]
    width = table.shape[1]
    info = plsc.get_sparse_core_info()
    nc, nw = info.num_cores, info.num_cores * info.num_subcores
    chunk = SC_GATHER_ROWS
    per_w = n // nw
    n_chunks = per_w // chunk
    assert per_w % (2 * chunk) == 0
    mesh = plsc.VectorSubcoreMesh(core_axis_name="c", subcore_axis_name="s")

    @functools.partial(
        pl.kernel, mesh=mesh, name="gather_rows",
        out_type=jax.ShapeDtypeStruct((n, width), table.dtype),
        scratch_types=[pltpu.VMEM((n_chunks, chunk), I32),
                       pltpu.VMEM((2, chunk, width), table.dtype),
                       pltpu.SemaphoreType.DMA((2,)),
                       pltpu.SemaphoreType.DMA((2,))])
    def gather(table_hbm, idx_hbm, out_hbm, idx_v, rows_v, gsem, osem):
        wid = lax.axis_index("s") * nc + lax.axis_index("c")
        base = wid * per_w
        pltpu.sync_copy(idx_hbm.at[wid], idx_v)

        def fetch(j, b):
            return pltpu.make_async_copy(table_hbm.at[idx_v.at[j]], rows_v.at[b], gsem.at[b])

        def put(j, b):
            return pltpu.make_async_copy(rows_v.at[b], out_hbm.at[pl.ds(base + j * chunk, chunk)], osem.at[b])

        fetch(0, 0).start()

        @pl.loop(0, n_chunks, step=2)
        def _(j0):
            for b in range(2):
                j = j0 + b
                fetch(j, b).wait()

                @pl.when(j + 1 < n_chunks)
                def _():
                    @pl.when(j >= 1)
                    def _():
                        put(j - 1, 1 - b).wait()

                    fetch(j + 1, 1 - b).start()

                put(j, b).start()

        put(n_chunks - 2, 0).wait()
        put(n_chunks - 1, 1).wait()

    return gather(table, idx.reshape(nw, n_chunks, chunk))


def _xs_copy(xs_hbm, xbuf, isem, j, slot):
    return pltpu.make_async_copy(xs_hbm.at[pl.ds(j * ROW_BLOCK, ROW_BLOCK)], xbuf.at[slot], isem.at[slot])


def _ys_copy(ybuf, ys_hbm, osem, j, slot):
    return pltpu.make_async_copy(ybuf.at[slot], ys_hbm.at[pl.ds(j * ROW_BLOCK, ROW_BLOCK)], osem.at[slot])


def _experts_kernel(be_ref, nv_ref, xs_hbm, wg_ref, wu_ref, wd_ref, ys_hbm,
                    xbuf, ybuf, wgu_s, wd_s, cur_ref, isem, osem):
    i = pl.program_id(0)
    n_valid = nv_ref[i]
    half = EXPERT_SUB
    slot = lax.rem(i, EXPERT_IN_SLOTS)
    oslot = lax.rem(i, 2)

    @pl.when(i == 0)
    def _():
        cur_ref[0] = -1
        for j in range(2):
            @pl.when(nv_ref[j] > 0)
            def _():
                _xs_copy(xs_hbm, xbuf, isem, j, j).start()

    @pl.when(nv_ref[i + 2] > 0)
    def _():
        _xs_copy(xs_hbm, xbuf, isem, i + 2, lax.rem(i + 2, EXPERT_IN_SLOTS)).start()

    @pl.when((i >= 2) & (nv_ref[jnp.maximum(i - 2, 0)] > 0))
    def _():
        _ys_copy(ybuf, ys_hbm, osem, i - 2, oslot).wait()

    @pl.when((n_valid > 0) & (cur_ref[0] != be_ref[i]))
    def _():
        wgu_s[:, :EXPERT_FF] = wg_ref[...].astype(BF16)
        wgu_s[:, EXPERT_FF:] = wu_ref[...].astype(BF16)
        wd_s[...] = wd_ref[...].astype(BF16)
        cur_ref[0] = be_ref[i]

    def rows_bf16(r0):
        valid = lax.broadcasted_iota(I32, (half, xbuf.shape[2]), 0) + r0 < n_valid
        lo, hi = _unpack_bf16_pairs(jnp.where(valid, xbuf[slot, r0:r0 + half, :], jnp.zeros((), U32)))
        return jnp.concatenate([lo.astype(BF16), hi.astype(BF16)], axis=1)

    def hidden(gu):
        return (_silu(gu[:, :EXPERT_FF]) * gu[:, EXPERT_FF:]).astype(BF16)

    @pl.when(n_valid > 0)
    def _():
        _xs_copy(xs_hbm, xbuf, isem, i, slot).wait()

    zeros = jnp.zeros((half, ybuf.shape[2]), ybuf.dtype)
    for r0 in range(0, ROW_BLOCK, 2 * half):
        r1 = r0 + half

        @pl.when(n_valid > r1)
        def _():
            xa, xb = rows_bf16(r0), rows_bf16(r1)
            gua = _dot(xa, wgu_s[...])
            gub = _dot(xb, wgu_s[...])
            ya = _dot(hidden(gua), wd_s[...])
            yb = _dot(hidden(gub), wd_s[...])
            ybuf[oslot, r0:r1, :] = _pack_bf16_pairs(ya)
            ybuf[oslot, r1:r1 + half, :] = _pack_bf16_pairs(yb)

        @pl.when((n_valid > r0) & (n_valid <= r1))
        def _():
            ya = _dot(hidden(_dot(rows_bf16(r0), wgu_s[...])), wd_s[...])
            ybuf[oslot, r0:r1, :] = _pack_bf16_pairs(ya)
            ybuf[oslot, r1:r1 + half, :] = zeros

        @pl.when((n_valid > 0) & (n_valid <= r0))
        def _():
            ybuf[oslot, r0:r1, :] = zeros
            ybuf[oslot, r1:r1 + half, :] = zeros

    @pl.when(n_valid > 0)
    def _():
        _ys_copy(ybuf, ys_hbm, osem, i, oslot).start(priority=1)


def _experts(block_e, n_valid, xs, wg, wu, wd, layer):
    n_rows, width = xs.shape
    steps = n_rows // ROW_BLOCK + 2
    be = jnp.concatenate([block_e, jnp.full((2,), N_EXPERTS - 1, I32)])
    nv = jnp.concatenate([n_valid, jnp.zeros((4,), I32)])
    grid_spec = pltpu.PrefetchScalarGridSpec(
        num_scalar_prefetch=2,
        grid=(steps,),
        in_specs=[pl.BlockSpec(memory_space=pl.ANY),
                  pl.BlockSpec((None, None, D_MODEL, EXPERT_FF), lambda i, be, nv: (layer, be[i], 0, 0)),
                  pl.BlockSpec((None, None, D_MODEL, EXPERT_FF), lambda i, be, nv: (layer, be[i], 0, 0)),
                  pl.BlockSpec((None, None, EXPERT_FF, D_MODEL), lambda i, be, nv: (layer, be[i], 0, 0))],
        out_specs=pl.BlockSpec(memory_space=pl.ANY),
        scratch_shapes=[pltpu.VMEM((EXPERT_IN_SLOTS, ROW_BLOCK, width), xs.dtype),
                        pltpu.VMEM((2, ROW_BLOCK, width), xs.dtype),
                        pltpu.VMEM((D_MODEL, 2 * EXPERT_FF), BF16),
                        pltpu.VMEM((EXPERT_FF, D_MODEL), BF16),
                        pltpu.SMEM((1,), I32),
                        pltpu.SemaphoreType.DMA((EXPERT_IN_SLOTS,)),
                        pltpu.SemaphoreType.DMA((2,))],
    )
    return pl.pallas_call(
        _experts_kernel,
        grid_spec=grid_spec,
        out_shape=jax.ShapeDtypeStruct((n_rows, width), xs.dtype),
        compiler_params=_params("arbitrary"),
        name="experts",
    )(be, nv, xs, wg, wu, wd)


def _shared_kernel(x1b_ref, wsg_ref, wsu_ref, wsd_ref, anchor_ref, sh_ref):
    del anchor_ref
    xb = x1b_ref[...]
    hid = (_silu(_dot(xb, wsg_ref[...])) * _dot(xb, wsu_ref[...])).astype(BF16)
    sh_ref[...] = _dot(hid, wsd_ref[...]).astype(sh_ref.dtype)


def _shared(x1b, wsg, wsu, wsd, anchor, tt, row0, nrows):
    full = lambda i: (0, 0)
    first = row0 // tt
    return pl.pallas_call(
        _shared_kernel,
        grid=(nrows // tt,),
        in_specs=[pl.BlockSpec((tt, D_MODEL), lambda i: (first + i, 0)),
                  pl.BlockSpec((D_MODEL, SHARED_FF), full),
                  pl.BlockSpec((D_MODEL, SHARED_FF), full),
                  pl.BlockSpec((SHARED_FF, D_MODEL), full),
                  pl.BlockSpec(memory_space=pl.ANY)],
        out_specs=pl.BlockSpec((tt, D_MODEL), lambda i: (i, 0)),
        out_shape=jax.ShapeDtypeStruct((nrows, D_MODEL), BF16),
        compiler_params=_params("parallel"),
        name="shared",
    )(x1b, wsg, wsu, wsd, anchor)


def _combine_kernel(yg_ref, w_ref, x1_ref, sha_ref, shb_ref, vec_ref, x2_ref, *, half_steps):
    in_first = pl.program_id(0) < half_steps
    shared = jnp.where(in_first, sha_ref[...], shb_ref[...]).astype(F32)
    w = w_ref[...]
    half = D_MODEL // 2
    acc_lo, acc_hi = shared[:, :half], shared[:, half:]
    for k in range(TOP_K):
        lo, hi = _unpack_bf16_pairs(yg_ref[k])
        acc_lo = acc_lo + w[:, k:k + 1] * lo
        acc_hi = acc_hi + w[:, k:k + 1] * hi
    acc = jnp.concatenate([acc_lo, acc_hi], axis=1)
    x2 = _layer_norm(DEEPNORM_ALPHA * x1_ref[...] + acc, vec_ref[0:1, :], vec_ref[1:2, :])
    x2_ref[...] = x2


def _combine(yg, w_tok, x1, shared_a, shared_b, vec, tt):
    t = x1.shape[0]
    half_steps = shared_a.shape[0] // tt
    full = lambda i: (0, 0)
    return pl.pallas_call(
        functools.partial(_combine_kernel, half_steps=half_steps),
        grid=(t // tt,),
        in_specs=[pl.BlockSpec((TOP_K, tt, yg.shape[2]), lambda i: (0, i, 0)),
                  pl.BlockSpec((tt, TOP_K), lambda i: (i, 0)),
                  pl.BlockSpec((tt, D_MODEL), lambda i: (i, 0)),
                  pl.BlockSpec((tt, D_MODEL), lambda i: (jnp.minimum(i, half_steps - 1), 0)),
                  pl.BlockSpec((tt, D_MODEL), lambda i: (jnp.maximum(i - half_steps, 0), 0)),
                  pl.BlockSpec((8, D_MODEL), full)],
        out_specs=pl.BlockSpec((tt, D_MODEL), lambda i: (i, 0)),
        out_shape=jax.ShapeDtypeStruct((t, D_MODEL), F32),
        compiler_params=_params("parallel"),
        name="combine",
    )(yg, w_tok, x1, shared_a, shared_b, vec)


def _pad_rows(a, rows):
    return jnp.zeros((rows, a.shape[-1]), F32).at[:a.shape[0]].set(a.astype(F32))


def _layer(x, p, stacked, layer, batch, seq):
    t = batch * seq
    w_main, wa_hi, wa_lo, wb_hi, wb_lo = _regroup(jnp.swapaxes(stacked["w_in"], 1, 2), layer, REGROUP_ROWS)
    proj = _proj(x, w_main, min(PROJ_ROWS, t), PROJ_TILE_COLS)

    prm = jnp.zeros((8, LANES), F32)
    prm = prm.at[0, :GDN_HEADS].set(p["a_log"]).at[1, :GDN_HEADS].set(p["dt_bias"])
    ong = p["o_norm_g"].reshape(1, HEAD_DIM).astype(F32)
    u, w, qd, m2, eg = _gdn_intra(proj, x, wa_hi, wa_lo, wb_hi, wb_lo, p["conv_qkv"].astype(F32), prm,
                                  seq, min(GDN_TILE, seq))
    o_gdn = _gdn_scan(u, w, qd, m2, eg, proj, ong, batch, seq)

    ts = min(MIXER_ROWS, seq)
    dww = jnp.broadcast_to(p["dw_w"].astype(F32)[:, None, :], (CONF_KERNEL, 8, CONF_CH))
    vec = _pad_rows(jnp.stack([p["dw_b"], p["cln_g"], p["cln_b"], p["ln1_g"], p["ln1_b"]]), 8)
    x1, x1b, x1p = _mixer(proj, o_gdn, x, p["w_oa"].astype(BF16), p["w_ob"].astype(BF16),
                          p["w_o"].astype(BF16), dww, vec, batch, seq, ts)

    tt_r = min(ROUTE_TOKENS, t)
    wr_t = p["w_router"].T.astype(F32)
    wr_hi = wr_t.astype(BF16)
    wr_lo = (wr_t - wr_hi.astype(F32)).astype(BF16)
    bias = jnp.broadcast_to(p["router_bias"].astype(F32)[:, None], (N_EXPERTS, tt_r))
    eidx_t, wts_t, rank_t, cnt = _route(x1, wr_hi, wr_lo, bias, tt_r)

    n_blocks = -(-(t * TOP_K + N_EXPERTS * (ROW_BLOCK - 1)) // ROW_BLOCK)
    n_rows = n_blocks * ROW_BLOCK
    pstart, block_e, n_valid = _plan(cnt, n_blocks)

    dest = _dest(eidx_t, rank_t, pstart, min(DEST_TOKENS, t))
    sh_w = (p["w_sh_gate"].astype(BF16), p["w_sh_up"].astype(BF16), p["w_sh_down"].astype(BF16))
    tt_s = min(SHARED_ROWS, t // 2)
    xs = _dispatch(dest, x1p, n_rows)
    shared_a = _shared(x1b, *sh_w, dest, tt_s, 0, t // 2)
    ys = _experts(block_e, n_valid, xs, stacked["w_gate_e"], stacked["w_up_e"], stacked["w_down_e"], layer)
    yg = _gather_rows(ys, dest.reshape(TOP_K * t)).reshape(TOP_K, t, ys.shape[1])
    shared_b = _shared(x1b, *sh_w, ys, tt_s, t // 2, t // 2)
    vec2 = _pad_rows(jnp.stack([p["ln2_g"], p["ln2_b"]]), 8)
    return _combine(yg, wts_t.T, x1, shared_a, shared_b, vec2, min(COMBINE_ROWS, t // 2))


_PARAM_NAMES = ("w_in", "conv_qkv", "a_log", "dt_bias", "o_norm_g", "w_oa", "dw_w", "dw_b", "cln_g",
                "cln_b", "w_ob", "w_o", "ln1_g", "ln1_b", "w_router", "router_bias", "w_gate_e",
                "w_up_e", "w_down_e", "w_sh_gate", "w_sh_up", "w_sh_down", "ln2_g", "ln2_b")


_EXPERT_WEIGHTS = ("w_in", "w_gate_e", "w_up_e", "w_down_e")


def kernel(x, w_in, conv_qkv, a_log, dt_bias, o_norm_g, w_oa, dw_w, dw_b, cln_g, cln_b, w_ob, w_o,
           ln1_g, ln1_b, w_router, router_bias, w_gate_e, w_up_e, w_down_e, w_sh_gate, w_sh_up,
           w_sh_down, ln2_g, ln2_b):
    stacked = dict(zip(_PARAM_NAMES, (w_in, conv_qkv, a_log, dt_bias, o_norm_g, w_oa, dw_w, dw_b, cln_g,
                                      cln_b, w_ob, w_o, ln1_g, ln1_b, w_router, router_bias, w_gate_e,
                                      w_up_e, w_down_e, w_sh_gate, w_sh_up, w_sh_down, ln2_g, ln2_b)))
    batch, seq, d = x.shape
    assert d == D_MODEL and seq % CHUNK == 0
    xf = x.reshape(batch * seq, d).astype(F32)
    for layer in range(w_in.shape[0]):
        p = {name: arr[layer] for name, arr in stacked.items() if name not in _EXPERT_WEIGHTS}
        xf = _layer(xf, p, stacked, layer, batch, seq)
    return xf.reshape(batch, seq, d).astype(x.dtype)
```
